```python
import jax, jax.numpy as jnp
from jax import lax
import numpy as np

D_MODEL = 1024
BATCH = 8
SEQ = 2048
DEPTH = 1
DEC_BATCH = 128
DEC_SEQ = 8
PAST_LEN = 16384
PAGE_SIZE = 128

PLE_DIM = 256
CONV_CH = D_MODEL // 2
CONV_K = 31
RET_WIDTH = D_MODEL - CONV_CH
RET_HEADS = 4
RET_HEAD_DIM = RET_WIDTH // RET_HEADS
CHUNK = 128
ROPE_BASE = 10000.0
N_GROUPS = 4
EXPERTS_PER_GROUP = 8
N_EXPERTS = N_GROUPS * EXPERTS_PER_GROUP
TOP_K = 2
EXPERT_FF = D_MODEL // 4
IN_COLS = 2 * CONV_CH + 4 * RET_WIDTH
EPS = 1e-6

kernel_name = "hymba_conformer_retnet_hmoe_step"


def rmsnorm(x, g):
    xf = x.astype(jnp.float32)
    y = xf * lax.rsqrt(jnp.mean(xf * xf, axis=-1, keepdims=True) + EPS)
    return (y * g.astype(jnp.float32)).astype(x.dtype)


def layernorm(x, g, b):
    xf = x.astype(jnp.float32)
    mu = jnp.mean(xf, axis=-1, keepdims=True)
    var = jnp.mean(jnp.square(xf - mu), axis=-1, keepdims=True)
    y = (xf - mu) * lax.rsqrt(var + EPS)
    return (y * g.astype(jnp.float32) + b.astype(jnp.float32)).astype(x.dtype)


def rope(x, pos0):
    l, d = x.shape[1], x.shape[-1]
    half = d // 2
    inv = ROPE_BASE ** (-jnp.arange(half, dtype=jnp.float32) / half)
    pos = jnp.arange(l, dtype=jnp.float32) + jnp.float32(pos0)
    ang = pos[:, None] * inv[None, :]
    cos = jnp.cos(ang)[None, :, None, :]
    sin = jnp.sin(ang)[None, :, None, :]
    xf = x.astype(jnp.float32)
    x1, x2 = xf[..., :half], xf[..., half:]
    return jnp.concatenate([x1 * cos - x2 * sin, x1 * sin + x2 * cos], axis=-1).astype(x.dtype)


def causal_depthwise_conv(u, buf, w, b):
    ext = jnp.concatenate([buf.astype(u.dtype), u], axis=1)
    out = lax.conv_general_dilated(
        ext, w[:, None, :].astype(u.dtype), window_strides=(1,), padding='VALID',
        dimension_numbers=('NWC', 'WIO', 'NWC'), feature_group_count=u.shape[-1])
    return out + b, ext[:, -(CONV_K - 1):]


def retention(q, k, v, r0):
    n, l, hh, dk = q.shape
    dv = v.shape[-1]
    c = CHUNK if l % CHUNK == 0 else l
    nc = l // c

    def blocks(x):
        return x.astype(jnp.float32).reshape(n, nc, c, hh, x.shape[-1]).transpose(1, 0, 3, 2, 4)

    lg = jnp.log(1.0 - 2.0 ** (-5.0 - jnp.arange(hh, dtype=jnp.float32)))
    idx = jnp.arange(c, dtype=jnp.float32)
    rel = idx[:, None] - idx[None, :]
    dmat = jnp.where(rel[None] >= 0, jnp.exp(jnp.maximum(rel, 0.0)[None] * lg[:, None, None]), 0.0)
    xi = jnp.exp((idx + 1.0)[None, :] * lg[:, None])
    zeta = jnp.exp((c - 1.0 - idx)[None, :] * lg[:, None])
    gc = jnp.exp(c * lg)

    def step(r, blk):
        qc, kc, vc = blk
        s = jnp.einsum('nhid,nhjd->nhij', qc, kc) * dmat
        o = (jnp.einsum('nhij,nhje->nhie', s, vc)
             + jnp.einsum('nhid,nhde->nhie', qc, r) * xi[:, :, None])
        r = r * gc[:, None, None] + jnp.einsum('nhjd,nhje->nhde', kc * zeta[:, :, None], vc)
        return r, o

    r, o = lax.scan(step, r0.astype(jnp.float32), (blocks(q), blocks(k), blocks(v)))
    o = o.transpose(1, 0, 3, 2, 4).reshape(n, l, hh, dv)
    return o, r


def token_mix(h, conv_buf, ret_state, pos0, w_in, conv_w, conv_b, conv_ln_g, conv_ln_b, w_out):
    n, l, _ = h.shape
    z = h @ w_in
    a, b, q, k, v, g = jnp.split(
        z, [CONV_CH, 2 * CONV_CH, 2 * CONV_CH + RET_WIDTH, 2 * CONV_CH + 2 * RET_WIDTH,
            2 * CONV_CH + 3 * RET_WIDTH], axis=-1)
    u = a * jax.nn.sigmoid(b)
    c, new_buf = causal_depthwise_conv(u, conv_buf, conv_w, conv_b)
    c = jax.nn.silu(layernorm(c, conv_ln_g, conv_ln_b))
    q = rope(q.reshape(n, l, RET_HEADS, RET_HEAD_DIM), pos0)
    k = rope(k.reshape(n, l, RET_HEADS, RET_HEAD_DIM), pos0) * (RET_HEAD_DIM ** -0.5)
    v = v.reshape(n, l, RET_HEADS, RET_HEAD_DIM)
    o, new_state = retention(q, k, v, ret_state)
    mu = jnp.mean(o, axis=-1, keepdims=True)
    var = jnp.mean(jnp.square(o - mu), axis=-1, keepdims=True)
    o = ((o - mu) * lax.rsqrt(var + EPS)).reshape(n, l, RET_WIDTH).astype(g.dtype)
    o = jax.nn.silu(g) * o
    y = jnp.concatenate([c, o], axis=-1) @ w_out
    return y, new_buf, new_state


def hier_moe(h, rg_w, rg_b, re_w, re_b, w_gate, w_up, w_down):
    n, l, d = h.shape
    t = h.reshape(n * l, d)
    g_prob = jax.nn.softmax((t @ rg_w + rg_b).astype(jnp.float32), axis=-1)
    g_top, g_idx = lax.top_k(g_prob, 1)
    e_logit = (t @ re_w + re_b).astype(jnp.float32).reshape(-1, N_GROUPS, EXPERTS_PER_GROUP)
    e_logit = jnp.take_along_axis(e_logit, g_idx[:, :, None], axis=1)[:, 0]
    e_top, e_idx = lax.top_k(jax.nn.softmax(e_logit, axis=-1), TOP_K)
    e_top = e_top / jnp.sum(e_top, axis=-1, keepdims=True)
    gate = g_top * e_top
    eid = g_idx * EXPERTS_PER_GROUP + e_idx
    combine = jnp.sum(jax.nn.one_hot(eid, N_EXPERTS, dtype=jnp.float32) * gate[..., None], axis=1)
    hid = jax.nn.silu(jnp.einsum('td,edf->tef', t, w_gate)) * jnp.einsum('td,edf->tef', t, w_up)
    hid = hid * combine[..., None].astype(hid.dtype)
    y = jnp.einsum('tef,efd->td', hid, w_down)
    return y.reshape(n, l, d)


def layer(x, p, conv_buf, ret_state, pos0, w_in, conv_w, conv_b, conv_ln_g, conv_ln_b, w_out,
          norm1_g, norm2_g, rg_w, rg_b, re_w, re_b, w_gate, w_up, w_down, w_ple, ple_norm_g, w_ple_gate):
    mix, new_buf, new_state = token_mix(rmsnorm(x, norm1_g), conv_buf, ret_state, pos0,
                                        w_in, conv_w, conv_b, conv_ln_g, conv_ln_b, w_out)
    x = x + mix
    x = x + hier_moe(rmsnorm(x, norm2_g), rg_w, rg_b, re_w, re_b, w_gate, w_up, w_down)
    ple = rmsnorm(p @ w_ple, ple_norm_g)
    x = x + ple * jax.nn.sigmoid(x @ w_ple_gate)
    return x, new_buf, new_state


def setup_inputs(seed: int = 0) -> dict:
    key = jax.random.key(seed)
    ks = jax.random.split(key, 32)
    f32 = jnp.float32
    nrm = lambda k, shape, s: jax.random.normal(k, shape, f32) * s
    return {
        "x_prompt": nrm(ks[0], (BATCH, SEQ, D_MODEL), 1.0),
        "x_sample": nrm(ks[1], (DEC_BATCH, DEC_SEQ, D_MODEL), 1.0),
        "p_prompt": nrm(ks[2], (DEPTH, BATCH, SEQ, PLE_DIM), 1.0),
        "p_sample": nrm(ks[3], (DEPTH, DEC_BATCH, DEC_SEQ, PLE_DIM), 1.0),
        "state_conv": nrm(ks[4], (DEPTH, DEC_BATCH, CONV_K - 1, CONV_CH), 0.5),
        "state_ret": nrm(ks[5], (DEPTH, DEC_BATCH, RET_HEADS, RET_HEAD_DIM, RET_HEAD_DIM), 0.1),
        "w_in": nrm(ks[6], (DEPTH, D_MODEL, IN_COLS), D_MODEL ** -0.5),
        "conv_w": nrm(ks[7], (DEPTH, CONV_K, CONV_CH), CONV_K ** -0.5),
        "conv_b": nrm(ks[8], (DEPTH, CONV_CH), 0.01),
        "conv_ln_g": 1.0 + nrm(ks[9], (DEPTH, CONV_CH), 0.01),
        "conv_ln_b": nrm(ks[10], (DEPTH, CONV_CH), 0.01),
        "w_out": nrm(ks[11], (DEPTH, D_MODEL, D_MODEL), D_MODEL ** -0.5),
        "norm1_g": 1.0 + nrm(ks[12], (DEPTH, D_MODEL), 0.01),
        "norm2_g": 1.0 + nrm(ks[13], (DEPTH, D_MODEL), 0.01),
        "router_group_w": nrm(ks[14], (DEPTH, D_MODEL, N_GROUPS), D_MODEL ** -0.5),
        "router_group_b": nrm(ks[15], (DEPTH, N_GROUPS), 0.01),
        "router_expert_w": nrm(ks[16], (DEPTH, D_MODEL, N_EXPERTS), D_MODEL ** -0.5),
        "router_expert_b": nrm(ks[17], (DEPTH, N_EXPERTS), 0.01),
        "w_expert_gate": nrm(ks[18], (DEPTH, N_EXPERTS, D_MODEL, EXPERT_FF), D_MODEL ** -0.5),
        "w_expert_up": nrm(ks[19], (DEPTH, N_EXPERTS, D_MODEL, EXPERT_FF), D_MODEL ** -0.5),
        "w_expert_down": nrm(ks[20], (DEPTH, N_EXPERTS, EXPERT_FF, D_MODEL), EXPERT_FF ** -0.5),
        "w_ple": nrm(ks[21], (DEPTH, PLE_DIM, D_MODEL), PLE_DIM ** -0.5),
        "ple_norm_g": 1.0 + nrm(ks[22], (DEPTH, D_MODEL), 0.01),
        "w_ple_gate": nrm(ks[23], (DEPTH, D_MODEL, D_MODEL), D_MODEL ** -0.5),
        "final_norm_g": 1.0 + nrm(ks[24], (D_MODEL,), 0.01),
    }


def reference(x_prompt, x_sample, p_prompt, p_sample, state_conv, state_ret,
              w_in, conv_w, conv_b, conv_ln_g, conv_ln_b, w_out, norm1_g, norm2_g,
              router_group_w, router_group_b, router_expert_w, router_expert_b,
              w_expert_gate, w_expert_up, w_expert_down, w_ple, ple_norm_g, w_ple_gate,
              final_norm_g):
    hp = x_prompt
    hs = x_sample
    conv_p, ret_p, conv_s, ret_s = [], [], [], []
    for i in range(DEPTH):
        w = (w_in[i], conv_w[i], conv_b[i], conv_ln_g[i], conv_ln_b[i], w_out[i], norm1_g[i], norm2_g[i],
             router_group_w[i], router_group_b[i], router_expert_w[i], router_expert_b[i],
             w_expert_gate[i], w_expert_up[i], w_expert_down[i], w_ple[i], ple_norm_g[i], w_ple_gate[i])
        buf0 = jnp.zeros((hp.shape[0], CONV_K - 1, CONV_CH), hp.dtype)
        st0 = jnp.zeros((hp.shape[0], RET_HEADS, RET_HEAD_DIM, RET_HEAD_DIM), jnp.float32)
        hp, nb, ns = layer(hp, p_prompt[i], buf0, st0, 0, *w)
        conv_p.append(nb)
        ret_p.append(ns)
        hs, nb, ns = layer(hs, p_sample[i], state_conv[i], state_ret[i], PAST_LEN, *w)
        conv_s.append(nb)
        ret_s.append(ns)
    y_prompt = rmsnorm(hp, final_norm_g)
    y_sample = rmsnorm(hs, final_norm_g)
    return (y_prompt, y_sample, jnp.stack(conv_p), jnp.stack(ret_p), jnp.stack(conv_s), jnp.stack(ret_s))
```

```python
import functools

import jax
import jax.numpy as jnp
from jax import lax
from jax.experimental import pallas as pl
from jax.experimental.pallas import tpu as pltpu

F32 = jnp.float32
BF16 = jnp.bfloat16

D_MODEL = 1024
PLE_DIM = 256
CONV_CH = 512
CONV_K = 31
RET_WIDTH = 512
RET_HEADS = 4
HEAD_DIM = 128
CHUNK = 128
ROPE_BASE = 10000.0
N_GROUPS = 4
EXPERTS_PER_GROUP = 8
N_EXPERTS = 32
EXPERT_FF = 256
IN_COLS = 3072
EPS = 1e-6
PAST_LEN = 16384

LANES = 128
HALO = 32
HALO_OFF = HALO - (CONV_K - 1)
VMEM_LIMIT = 48 * 1024 * 1024


def _cparams(sem):
    return pltpu.CompilerParams(dimension_semantics=sem, vmem_limit_bytes=VMEM_LIMIT)


def _rms(x, g):
    return x * lax.rsqrt(jnp.mean(x * x, axis=-1, keepdims=True) + EPS) * g


def _inproj_body(x_ref, g1_ref, w_ref, cos_ref, sin_ref, u_ref, q_ref, k_ref, v_ref, gs_ref):
    h = _rms(x_ref[...], g1_ref[...]).astype(BF16)
    z = jnp.dot(h, w_ref[...], preferred_element_type=F32)
    a = z[:, :CONV_CH]
    b = z[:, CONV_CH:2 * CONV_CH]
    u_ref[...] = a * jax.nn.sigmoid(b)
    cos = cos_ref[...]
    sin = sin_ref[...]
    q0 = 2 * CONV_CH
    k0 = q0 + RET_WIDTH
    for hh in range(RET_HEADS):
        sl = slice(hh * HEAD_DIM, (hh + 1) * HEAD_DIM)
        qh = z[:, q0 + hh * HEAD_DIM:q0 + (hh + 1) * HEAD_DIM]
        kh = z[:, k0 + hh * HEAD_DIM:k0 + (hh + 1) * HEAD_DIM]
        q_ref[:, sl] = (qh * cos + pltpu.roll(qh, HEAD_DIM // 2, 1) * sin).astype(q_ref.dtype)
        kr = (kh * cos + pltpu.roll(kh, HEAD_DIM // 2, 1) * sin) * (HEAD_DIM ** -0.5)
        k_ref[:, sl] = kr.astype(k_ref.dtype)
    v_ref[...] = z[:, k0 + RET_WIDTH:k0 + 2 * RET_WIDTH].astype(v_ref.dtype)
    g = z[:, k0 + 2 * RET_WIDTH:]
    gs_ref[...] = g * jax.nn.sigmoid(g)


def _inproj(x, g1, w_in, cos, sin, tm, table_blocks, qkv_dtype):
    t = x.shape[0]
    row = lambda i: (i, 0)
    const = lambda i: (0, 0)
    tab = (lambda i: (i % table_blocks, 0)) if table_blocks > 1 else const
    return pl.pallas_call(
        _inproj_body,
        grid=(t // tm,),
        in_specs=[
            pl.BlockSpec((tm, D_MODEL), row),
            pl.BlockSpec((1, D_MODEL), const),
            pl.BlockSpec((D_MODEL, IN_COLS), const),
            pl.BlockSpec((tm, HEAD_DIM), tab),
            pl.BlockSpec((tm, HEAD_DIM), tab),
        ],
        out_specs=[pl.BlockSpec((tm, CONV_CH), row)] + [pl.BlockSpec((tm, RET_WIDTH), row)] * 4,
        out_shape=[
            jax.ShapeDtypeStruct((t, CONV_CH), F32),
            jax.ShapeDtypeStruct((t, RET_WIDTH), qkv_dtype),
            jax.ShapeDtypeStruct((t, RET_WIDTH), qkv_dtype),
            jax.ShapeDtypeStruct((t, RET_WIDTH), qkv_dtype),
            jax.ShapeDtypeStruct((t, RET_WIDTH), F32),
        ],
        compiler_params=_cparams(("parallel",)),
        name="inproj",
    )(x, g1, w_in, cos, sin)


def _ln_silu(acc, g, b):
    mu = jnp.mean(acc, axis=-1, keepdims=True)
    d = acc - mu
    var = jnp.mean(d * d, axis=-1, keepdims=True)
    y = d * lax.rsqrt(var + EPS) * g + b
    return y * jax.nn.sigmoid(y)


def _conv_prompt_body(u_ref, w_ref, cb_ref, lg_ref, lb_ref, c_ref, st_ref, ext_ref):
    j = pl.program_id(1)
    tl = u_ref.shape[1]

    @pl.when(j == 0)
    def _():
        ext_ref[0:HALO, :] = jnp.zeros((HALO, CONV_CH), F32)

    @pl.when(j > 0)
    def _():
        ext_ref[0:HALO, :] = ext_ref[tl:tl + HALO, :]

    ext_ref[HALO:, :] = u_ref[0]
    acc = jnp.zeros((tl, CONV_CH), F32) + cb_ref[...]
    for k in range(CONV_K):
        acc = acc + ext_ref[k + HALO_OFF:k + HALO_OFF + tl, :] * w_ref[k:k + 1, :]
    c_ref[0] = _ln_silu(acc, lg_ref[...], lb_ref[...]).astype(c_ref.dtype)
    st_ref[0] = ext_ref[tl + HALO_OFF:tl + HALO, :]


def _conv_prompt(u, conv_w, conv_b, ln_g, ln_b, tl):
    n, l, _ = u.shape
    const = lambda b, j: (0, 0)
    return pl.pallas_call(
        _conv_prompt_body,
        grid=(n, l // tl),
        in_specs=[
            pl.BlockSpec((1, tl, CONV_CH), lambda b, j: (b, j, 0)),
            pl.BlockSpec((CONV_K, CONV_CH), const),
            pl.BlockSpec((1, CONV_CH), const),
            pl.BlockSpec((1, CONV_CH), const),
            pl.BlockSpec((1, CONV_CH), const),
        ],
        out_specs=[
            pl.BlockSpec((1, tl, CONV_CH), lambda b, j: (b, j, 0)),
            pl.BlockSpec((1, CONV_K - 1, CONV_CH), lambda b, j: (b, 0, 0)),
        ],
        out_shape=[
            jax.ShapeDtypeStruct((n, l, CONV_CH), BF16),
            jax.ShapeDtypeStruct((n, CONV_K - 1, CONV_CH), F32),
        ],
        scratch_shapes=[pltpu.VMEM((tl + HALO, CONV_CH), F32)],
        compiler_params=_cparams(("arbitrary", "arbitrary")),
        name="conv_prompt",
    )(u, conv_w, conv_b, ln_g, ln_b)


def _conv_sample_body(u_ref, st_ref, w_ref, cb_ref, lg_ref, lb_ref, c_ref, nst_ref, ext_ref):
    nb, l, _ = u_ref.shape
    ext_ref[:, HALO_OFF:HALO, :] = st_ref[...]
    ext_ref[:, HALO:, :] = u_ref[...]
    acc = jnp.zeros((nb, l, CONV_CH), F32) + cb_ref[...]
    for k in range(CONV_K):
        acc = acc + ext_ref[:, k + HALO_OFF:k + HALO_OFF + l, :] * w_ref[k:k + 1, :]
    c_ref[...] = _ln_silu(acc, lg_ref[...], lb_ref[...]).astype(c_ref.dtype)
    nst_ref[...] = ext_ref[:, l + HALO_OFF:l + HALO, :]


def _conv_sample(u, state, conv_w, conv_b, ln_g, ln_b, nb):
    n, l, _ = u.shape
    const = lambda b: (0, 0)
    blk3 = lambda b: (b, 0, 0)
    return pl.pallas_call(
        _conv_sample_body,
        grid=(n // nb,),
        in_specs=[
            pl.BlockSpec((nb, l, CONV_CH), blk3),
            pl.BlockSpec((nb, CONV_K - 1, CONV_CH), blk3),
            pl.BlockSpec((CONV_K, CONV_CH), const),
            pl.BlockSpec((1, CONV_CH), const),
            pl.BlockSpec((1, CONV_CH), const),
            pl.BlockSpec((1, CONV_CH), const),
        ],
        out_specs=[
            pl.BlockSpec((nb, l, CONV_CH), blk3),
            pl.BlockSpec((nb, CONV_K - 1, CONV_CH), blk3),
        ],
        out_shape=[
            jax.ShapeDtypeStruct((n, l, CONV_CH), BF16),
            jax.ShapeDtypeStruct((n, CONV_K - 1, CONV_CH), F32),
        ],
        scratch_shapes=[pltpu.VMEM((nb, l + HALO, CONV_CH), F32)],
        compiler_params=_cparams(("parallel",)),
        name="conv_sample",
    )(u, state, conv_w, conv_b, ln_g, ln_b)


def _decay_tables(c):
    lg = jnp.log(1.0 - 2.0 ** (-5.0 - jnp.arange(RET_HEADS, dtype=F32)))
    idx = jnp.arange(c, dtype=F32)
    rel = idx[:, None] - idx[None, :]
    dmat = jnp.where(rel[None] >= 0, jnp.exp(jnp.maximum(rel, 0.0)[None] * lg[:, None, None]), 0.0)
    xi = jnp.exp((idx + 1.0)[None, :] * lg[:, None])
    zeta = jnp.exp((c - 1.0 - idx)[None, :] * lg[:, None])
    gc = jnp.exp(c * lg)
    xi_b = jnp.broadcast_to(xi[:, :, None], (RET_HEADS, c, HEAD_DIM))
    zeta_b = jnp.broadcast_to(zeta[:, :, None], (RET_HEADS, c, HEAD_DIM))
    gc_b = jnp.broadcast_to(gc[:, None, None], (RET_HEADS, 1, HEAD_DIM))
    return dmat, xi_b, zeta_b, gc_b


def _group_norm(o):
    mu = jnp.mean(o, axis=-1, keepdims=True)
    d = o - mu
    var = jnp.mean(d * d, axis=-1, keepdims=True)
    return d * lax.rsqrt(var + EPS)


def _ret_chunk(qh, kh, vh, r, dmat, xi, zeta, gc):
    qb = qh.astype(BF16)
    kb = kh.astype(BF16)
    vb = vh.astype(BF16)
    s = lax.dot_general(qb, kb, (((1,), (1,)), ((), ())), preferred_element_type=F32) * dmat
    o = jnp.dot(s.astype(BF16), vb, preferred_element_type=F32)
    o = o + jnp.dot(qb, r.astype(BF16), preferred_element_type=F32) * xi
    kz = (kh.astype(F32) * zeta).astype(BF16)
    r_new = r * gc + lax.dot_general(kz, vb, (((0,), (0,)), ((), ())), preferred_element_type=F32)
    return o, r_new


def _ret_prompt_body(q_ref, k_ref, v_ref, gs_ref, d_ref, xi_ref, zeta_ref, gc_ref, o_ref, st_ref, r_ref):
    j = pl.program_id(1)

    @pl.when(j == 0)
    def _():
        r_ref[...] = jnp.zeros_like(r_ref)

    n_chunks = q_ref.shape[0] // CHUNK
    for hh in range(RET_HEADS):
        sl = slice(hh * HEAD_DIM, (hh + 1) * HEAD_DIM)
        r = r_ref[hh]
        for ci in range(n_chunks):
            rows = slice(ci * CHUNK, (ci + 1) * CHUNK)
            o, r = _ret_chunk(q_ref[rows, sl], k_ref[rows, sl], v_ref[rows, sl], r,
                              d_ref[hh], xi_ref[hh], zeta_ref[hh], gc_ref[hh])
            o_ref[rows, sl] = (gs_ref[rows, sl] * _group_norm(o)).astype(o_ref.dtype)
        r_ref[hh] = r
    st_ref[0] = r_ref[...]


def _ret_prompt(q, k, v, gs, n, l, tl):
    dmat, xi, zeta, gc = _decay_tables(CHUNK)
    per = l // tl
    row = lambda b, j: (b * per + j, 0)
    c3 = lambda b, j: (0, 0, 0)
    return pl.pallas_call(
        _ret_prompt_body,
        grid=(n, per),
        in_specs=[pl.BlockSpec((tl, RET_WIDTH), row)] * 4 + [
            pl.BlockSpec((RET_HEADS, CHUNK, CHUNK), c3),
            pl.BlockSpec((RET_HEADS, CHUNK, HEAD_DIM), c3),
            pl.BlockSpec((RET_HEADS, CHUNK, HEAD_DIM), c3),
            pl.BlockSpec((RET_HEADS, 1, HEAD_DIM), c3),
        ],
        out_specs=[
            pl.BlockSpec((tl, RET_WIDTH), row),
            pl.BlockSpec((1, RET_HEADS, HEAD_DIM, HEAD_DIM), lambda b, j: (b, 0, 0, 0)),
        ],
        out_shape=[
            jax.ShapeDtypeStruct((n * l, RET_WIDTH), BF16),
            jax.ShapeDtypeStruct((n, RET_HEADS, HEAD_DIM, HEAD_DIM), F32),
        ],
        scratch_shapes=[pltpu.VMEM((RET_HEADS, HEAD_DIM, HEAD_DIM), F32)],
        compiler_params=_cparams(("arbitrary", "arbitrary")),
        name="ret_prompt",
    )(q, k, v, gs, dmat, xi, zeta, gc)


def _ret_sample_body(q_ref, k_ref, v_ref, gs_ref, st_ref, d_ref, xi_ref, zeta_ref, gc_ref, o_ref, nst_ref):
    nb = st_ref.shape[0]
    l = q_ref.shape[0] // nb
    for b in range(nb):
        rows = slice(b * l, (b + 1) * l)
        for hh in range(RET_HEADS):
            sl = slice(hh * HEAD_DIM, (hh + 1) * HEAD_DIM)
            o, r = _ret_chunk(q_ref[rows, sl], k_ref[rows, sl], v_ref[rows, sl], st_ref[b, hh],
                              d_ref[hh], xi_ref[hh], zeta_ref[hh], gc_ref[hh])
            o_ref[rows, sl] = (gs_ref[rows, sl] * _group_norm(o)).astype(o_ref.dtype)
            nst_ref[b, hh] = r


def _ret_sample(q, k, v, gs, state, l, nb):
    n = state.shape[0]
    dmat, xi, zeta, gc = _decay_tables(l)
    row = lambda b: (b, 0)
    c3 = lambda b: (0, 0, 0)
    blk4 = lambda b: (b, 0, 0, 0)
    return pl.pallas_call(
        _ret_sample_body,
        grid=(n // nb,),
        in_specs=[pl.BlockSpec((nb * l, RET_WIDTH), row)] * 4 + [
            pl.BlockSpec((nb, RET_HEADS, HEAD_DIM, HEAD_DIM), blk4),
            pl.BlockSpec((RET_HEADS, l, l), c3),
            pl.BlockSpec((RET_HEADS, l, HEAD_DIM), c3),
            pl.BlockSpec((RET_HEADS, l, HEAD_DIM), c3),
            pl.BlockSpec((RET_HEADS, 1, HEAD_DIM), c3),
        ],
        out_specs=[
            pl.BlockSpec((nb * l, RET_WIDTH), row),
            pl.BlockSpec((nb, RET_HEADS, HEAD_DIM, HEAD_DIM), blk4),
        ],
        out_shape=[
            jax.ShapeDtypeStruct((n * l, RET_WIDTH), BF16),
            jax.ShapeDtypeStruct((n, RET_HEADS, HEAD_DIM, HEAD_DIM), F32),
        ],
        compiler_params=_cparams(("parallel",)),
        name="ret_sample",
    )(q, k, v, gs, state, dmat, xi, zeta, gc)


def _split3(x):
    hi = x.astype(BF16)
    r1 = x - hi.astype(F32)
    mid = r1.astype(BF16)
    lo = (r1 - mid.astype(F32)).astype(BF16)
    return hi, mid, lo


def _dot_hp(t, w_hi, w_mid, w_lo):
    t_hi, t_mid, t_lo = _split3(t)
    d = functools.partial(jnp.dot, preferred_element_type=F32)
    small = d(t_lo, w_hi) + d(t_mid, w_mid) + d(t_hi, w_lo)
    return (d(t_hi, w_hi) + (d(t_mid, w_hi) + d(t_hi, w_mid))) + small


def _route(lg, le):
    lane = lax.broadcasted_iota(jnp.int32, lg.shape, 1)
    big = jnp.int32(LANES)
    m = jnp.max(lg, axis=-1, keepdims=True)
    g_top = 1.0 / jnp.sum(jnp.exp(lg - m), axis=-1, keepdims=True)
    g_idx = jnp.min(jnp.where(lg == m, lane, big), axis=-1, keepdims=True)
    in_group = (lane >= g_idx * EXPERTS_PER_GROUP) & (lane < (g_idx + 1) * EXPERTS_PER_GROUP)
    lem = jnp.where(in_group, le, -1e30)
    m2 = jnp.max(lem, axis=-1, keepdims=True)
    pe = jnp.where(in_group, jnp.exp(lem - m2), 0.0)
    p1 = jnp.max(pe, axis=-1, keepdims=True)
    e1 = jnp.min(jnp.where(in_group & (pe == p1), lane, big), axis=-1, keepdims=True)
    rest = in_group & (lane != e1)
    pe2 = jnp.where(rest, pe, -1.0)
    p2 = jnp.max(pe2, axis=-1, keepdims=True)
    e2 = jnp.min(jnp.where(rest & (pe2 == p2), lane, big), axis=-1, keepdims=True)
    scale = g_top / (p1 + p2)
    return jnp.where(lane == e1, p1 * scale, 0.0) + jnp.where(lane == e2, p2 * scale, 0.0)


def _outproj_body(c_ref, o_ref, x_ref, wo_ref, g2_ref, rg_ref, rgb_ref, re_ref, reb_ref,
                  x1_ref, t_ref, comb_ref):
    x1 = x_ref[...] + jnp.dot(c_ref[...], wo_ref[0:CONV_CH, :], preferred_element_type=F32)
    x1 = x1 + jnp.dot(o_ref[...], wo_ref[CONV_CH:, :], preferred_element_type=F32)
    x1_ref[...] = x1
    t = _rms(x1, g2_ref[...])
    t_ref[...] = t.astype(t_ref.dtype)
    lg = _dot_hp(t, rg_ref[0], rg_ref[1], rg_ref[2]) + rgb_ref[...]
    le = _dot_hp(t, re_ref[0], re_ref[1], re_ref[2]) + reb_ref[...]
    comb_ref[...] = _route(lg, le)


def _outproj(c, o, x, w_out, g2, rg3, rgb, re3, reb, tm):
    t = x.shape[0]
    row = lambda i: (i, 0)
    const = lambda i: (0, 0)
    c3 = lambda i: (0, 0, 0)
    return pl.pallas_call(
        _outproj_body,
        grid=(t // tm,),
        in_specs=[
            pl.BlockSpec((tm, CONV_CH), row),
            pl.BlockSpec((tm, RET_WIDTH), row),
            pl.BlockSpec((tm, D_MODEL), row),
            pl.BlockSpec((D_MODEL, D_MODEL), const),
            pl.BlockSpec((1, D_MODEL), const),
            pl.BlockSpec((3, D_MODEL, LANES), c3),
            pl.BlockSpec((1, LANES), const),
            pl.BlockSpec((3, D_MODEL, LANES), c3),
            pl.BlockSpec((1, LANES), const),
        ],
        out_specs=[
            pl.BlockSpec((tm, D_MODEL), row),
            pl.BlockSpec((tm, D_MODEL), row),
            pl.BlockSpec((tm, LANES), row),
        ],
        out_shape=[
            jax.ShapeDtypeStruct((t, D_MODEL), F32),
            jax.ShapeDtypeStruct((t, D_MODEL), BF16),
            jax.ShapeDtypeStruct((t, LANES), F32),
        ],
        compiler_params=_cparams(("parallel",)),
        name="outproj_router",
    )(c, o, x, w_out, g2, rg3, rgb, re3, reb)


def _moe_body(t_ref, comb_ref, x1_ref, wg_ref, wu_ref, wd_ref, x2_ref, acc_ref):
    j = pl.program_id(1)
    ec = wg_ref.shape[0]

    @pl.when(j == 0)
    def _():
        acc_ref[...] = x1_ref[...]

    t = t_ref[...]
    comb = comb_ref[...]
    lane = lax.broadcasted_iota(jnp.int32, comb.shape, 1)
    acc = acc_ref[...]
    for e in range(ec):
        w = jnp.sum(jnp.where(lane == j * ec + e, comb, 0.0), axis=-1, keepdims=True)
        h1 = jnp.dot(t, wg_ref[e], preferred_element_type=F32)
        h2 = jnp.dot(t, wu_ref[e], preferred_element_type=F32)
        hid = (h1 * jax.nn.sigmoid(h1)) * h2 * w
        acc = acc + jnp.dot(hid.astype(BF16), wd_ref[e], preferred_element_type=F32)
    acc_ref[...] = acc

    @pl.when(j == pl.num_programs(1) - 1)
    def _():
        x2_ref[...] = acc_ref[...]


def _moe(t, comb, x1, wg, wu, wd, tm, ec):
    n = t.shape[0]
    row = lambda i, j: (i, 0)
    wblk = lambda i, j: (j, 0, 0)
    return pl.pallas_call(
        _moe_body,
        grid=(n // tm, N_EXPERTS // ec),
        in_specs=[
            pl.BlockSpec((tm, D_MODEL), row),
            pl.BlockSpec((tm, LANES), row),
            pl.BlockSpec((tm, D_MODEL), row),
            pl.BlockSpec((ec, D_MODEL, EXPERT_FF), wblk),
            pl.BlockSpec((ec, D_MODEL, EXPERT_FF), wblk),
            pl.BlockSpec((ec, EXPERT_FF, D_MODEL), wblk),
        ],
        out_specs=pl.BlockSpec((tm, D_MODEL), row),
        out_shape=jax.ShapeDtypeStruct((n, D_MODEL), F32),
        scratch_shapes=[pltpu.VMEM((tm, D_MODEL), F32)],
        compiler_params=_cparams(("parallel", "arbitrary")),
        name="moe",
    )(t, comb, x1, wg, wu, wd)


def _ple_body(x2_ref, p_ref, wp_ref, gp_ref, wpg_ref, gf_ref, y_ref):
    x2 = x2_ref[...]
    ple = _rms(jnp.dot(p_ref[...].astype(BF16), wp_ref[...], preferred_element_type=F32), gp_ref[...])
    gate = jax.nn.sigmoid(jnp.dot(x2.astype(BF16), wpg_ref[...], preferred_element_type=F32))
    y_ref[...] = _rms(x2 + ple * gate, gf_ref[...])


def _ple(x2, p, w_ple, gp, w_ple_gate, gf, tm):
    t = x2.shape[0]
    row = lambda i: (i, 0)
    const = lambda i: (0, 0)
    return pl.pallas_call(
        _ple_body,
        grid=(t // tm,),
        in_specs=[
            pl.BlockSpec((tm, D_MODEL), row),
            pl.BlockSpec((tm, PLE_DIM), row),
            pl.BlockSpec((PLE_DIM, D_MODEL), const),
            pl.BlockSpec((1, D_MODEL), const),
            pl.BlockSpec((D_MODEL, D_MODEL), const),
            pl.BlockSpec((1, D_MODEL), const),
        ],
        out_specs=pl.BlockSpec((tm, D_MODEL), row),
        out_shape=jax.ShapeDtypeStruct((t, D_MODEL), F32),
        compiler_params=_cparams(("parallel",)),
        name="ple_final",
    )(x2, p, w_ple, gp, w_ple_gate, gf)


def _rope_tables(pos):
    half = HEAD_DIM // 2
    inv = ROPE_BASE ** (-jnp.arange(half, dtype=F32) / half)
    ang = pos[:, None] * inv[None, :]
    cos = jnp.cos(ang)
    sin = jnp.sin(ang)
    return jnp.concatenate([cos, cos], axis=-1), jnp.concatenate([-sin, sin], axis=-1)


def _pad_router(w, b):
    pad = LANES - w.shape[1]
    w3 = jnp.stack(_split3(jnp.pad(w, ((0, 0), (0, pad)))))
    bp = jnp.pad(b, (0, pad), constant_values=-1e30)[None, :]
    return w3, bp


def kernel(x_prompt, x_sample, p_prompt, p_sample, state_conv, state_ret, w_in, conv_w, conv_b, conv_ln_g, conv_ln_b, w_out, norm1_g, norm2_g, router_group_w, router_group_b, router_expert_w, router_expert_b, w_expert_gate, w_expert_up, w_expert_down, w_ple, ple_norm_g, w_ple_gate, final_norm_g):
    assert w_in.shape[0] == 1, "single-layer trunk"
    nb, seq, _ = x_prompt.shape
    ns, dseq, _ = x_sample.shape
    tm = 512

    w_in_b = w_in[0].astype(BF16)
    w_out_b = w_out[0].astype(BF16)
    wg_b = w_expert_gate[0].astype(BF16)
    wu_b = w_expert_up[0].astype(BF16)
    wd_b = w_expert_down[0].astype(BF16)
    w_ple_b = w_ple[0].astype(BF16)
    w_pg_b = w_ple_gate[0].astype(BF16)
    g1 = norm1_g[0][None, :]
    g2 = norm2_g[0][None, :]
    gp = ple_norm_g[0][None, :]
    gf = final_norm_g[None, :]
    cb = conv_b[0][None, :]
    lng = conv_ln_g[0][None, :]
    lnb = conv_ln_b[0][None, :]
    rg3, rgb = _pad_router(router_group_w[0], router_group_b[0])
    re3, reb = _pad_router(router_expert_w[0], router_expert_b[0])

    cos_p, sin_p = _rope_tables(jnp.arange(seq, dtype=F32) + jnp.float32(0))
    pos_s = jnp.tile(jnp.arange(dseq, dtype=F32) + jnp.float32(PAST_LEN), tm // dseq)
    cos_s, sin_s = _rope_tables(pos_s)

    def tail(c, o, x, p):
        x1, t, comb = _outproj(c, o, x, w_out_b, g2, rg3, rgb, re3, reb, tm)
        x2 = _moe(t, comb, x1, wg_b, wu_b, wd_b, tm, 4)
        return _ple(x2, p, w_ple_b, gp, w_pg_b, gf, tm)

    xp = x_prompt.reshape(nb * seq, D_MODEL)
    u, q, k, v, gs = _inproj(xp, g1, w_in_b, cos_p, sin_p, tm, seq // tm, BF16)
    c, conv_p = _conv_prompt(u.reshape(nb, seq, CONV_CH), conv_w[0], cb, lng, lnb, tm)
    o, ret_p = _ret_prompt(q, k, v, gs, nb, seq, tm)
    y_p = tail(c.reshape(nb * seq, CONV_CH), o, xp, p_prompt[0].reshape(nb * seq, PLE_DIM))

    xs = x_sample.reshape(ns * dseq, D_MODEL)
    u, q, k, v, gs = _inproj(xs, g1, w_in_b, cos_s, sin_s, tm, 1, F32)
    c, conv_s = _conv_sample(u.reshape(ns, dseq, CONV_CH), state_conv[0], conv_w[0], cb, lng, lnb, 16)
    o, ret_s = _ret_sample(q, k, v, gs, state_ret[0], dseq, 8)
    y_s = tail(c.reshape(ns * dseq, CONV_CH), o, xs, p_sample[0].reshape(ns * dseq, PLE_DIM))

    return (y_p.reshape(nb, seq, D_MODEL), y_s.reshape(ns, dseq, D_MODEL),
            conv_p[None], ret_p[None], conv_s[None], ret_s[None])
```

```python
import functools

import jax
import jax.numpy as jnp
from jax import lax
from jax.experimental import pallas as pl
from jax.experimental.pallas import tpu as pltpu

F32 = jnp.float32
BF16 = jnp.bfloat16

D_MODEL = 1024
PLE_DIM = 256
CONV_CH = 512
CONV_K = 31
RET_WIDTH = 512
RET_HEADS = 4
HEAD_DIM = 128
CHUNK = 128
ROPE_BASE = 10000.0
N_GROUPS = 4
EXPERTS_PER_GROUP = 8
N_EXPERTS = 32
EXPERT_FF = 256
IN_COLS = 3072
EPS = 1e-6
PAST_LEN = 16384

LANES = 128
HALO = 32
HALO_OFF = HALO - (CONV_K - 1)
VMEM_LIMIT = 48 * 1024 * 1024
MOE_VMEM_LIMIT = 56 * 1024 * 1024

SUB = 256
ROW_ALIGN = 16
PBLK = 256
SUBP = -(-(2 * SUB + N_EXPERTS * (ROW_ALIGN - 1)) // PBLK) * PBLK
MBLK = 256
MOE_WINDOW = 2048


def _cparams(sem):
    return pltpu.CompilerParams(dimension_semantics=sem, vmem_limit_bytes=VMEM_LIMIT)


def _rms(x, g):
    return x * lax.rsqrt(jnp.mean(x * x, axis=-1, keepdims=True) + EPS) * g


def _inproj_body(x_ref, g1_ref, w_ref, cos_ref, sin_ref, u_ref, q_ref, k_ref, v_ref, gs_ref):
    h = _rms(x_ref[...], g1_ref[...]).astype(BF16)
    z = jnp.dot(h, w_ref[...], preferred_element_type=F32)
    a = z[:, :CONV_CH]
    b = z[:, CONV_CH:2 * CONV_CH]
    u_ref[...] = a * jax.nn.sigmoid(b)
    cos = cos_ref[...]
    sin = sin_ref[...]
    q0 = 2 * CONV_CH
    k0 = q0 + RET_WIDTH
    for hh in range(RET_HEADS):
        sl = slice(hh * HEAD_DIM, (hh + 1) * HEAD_DIM)
        qh = z[:, q0 + hh * HEAD_DIM:q0 + (hh + 1) * HEAD_DIM]
        kh = z[:, k0 + hh * HEAD_DIM:k0 + (hh + 1) * HEAD_DIM]
        q_ref[:, sl] = (qh * cos + pltpu.roll(qh, HEAD_DIM // 2, 1) * sin).astype(q_ref.dtype)
        kr = (kh * cos + pltpu.roll(kh, HEAD_DIM // 2, 1) * sin) * (HEAD_DIM ** -0.5)
        k_ref[:, sl] = kr.astype(k_ref.dtype)
    v_ref[...] = z[:, k0 + RET_WIDTH:k0 + 2 * RET_WIDTH].astype(v_ref.dtype)
    g = z[:, k0 + 2 * RET_WIDTH:]
    gs_ref[...] = g * jax.nn.sigmoid(g)


def _inproj(x, g1, w_in, cos, sin, tm, table_blocks, qkv_dtype):
    t = x.shape[0]
    row = lambda i: (i, 0)
    const = lambda i: (0, 0)
    tab = (lambda i: (i % table_blocks, 0)) if table_blocks > 1 else const
    return pl.pallas_call(
        _inproj_body,
        grid=(t // tm,),
        in_specs=[
            pl.BlockSpec((tm, D_MODEL), row),
            pl.BlockSpec((1, D_MODEL), const),
            pl.BlockSpec((D_MODEL, IN_COLS), const),
            pl.BlockSpec((tm, HEAD_DIM), tab),
            pl.BlockSpec((tm, HEAD_DIM), tab),
        ],
        out_specs=[pl.BlockSpec((tm, CONV_CH), row)] + [pl.BlockSpec((tm, RET_WIDTH), row)] * 4,
        out_shape=[
            jax.ShapeDtypeStruct((t, CONV_CH), F32),
            jax.ShapeDtypeStruct((t, RET_WIDTH), qkv_dtype),
            jax.ShapeDtypeStruct((t, RET_WIDTH), qkv_dtype),
            jax.ShapeDtypeStruct((t, RET_WIDTH), qkv_dtype),
            jax.ShapeDtypeStruct((t, RET_WIDTH), F32),
        ],
        compiler_params=_cparams(("parallel",)),
        name="inproj",
    )(x, g1, w_in, cos, sin)


def _ln_silu(acc, g, b):
    mu = jnp.mean(acc, axis=-1, keepdims=True)
    d = acc - mu
    var = jnp.mean(d * d, axis=-1, keepdims=True)
    y = d * lax.rsqrt(var + EPS) * g + b
    return y * jax.nn.sigmoid(y)


def _conv_prompt_body(u_ref, w_ref, cb_ref, lg_ref, lb_ref, c_ref, st_ref, ext_ref):
    j = pl.program_id(1)
    tl = u_ref.shape[1]

    @pl.when(j == 0)
    def _():
        ext_ref[0:HALO, :] = jnp.zeros((HALO, CONV_CH), F32)

    @pl.when(j > 0)
    def _():
        ext_ref[0:HALO, :] = ext_ref[tl:tl + HALO, :]

    ext_ref[HALO:, :] = u_ref[0]
    acc = jnp.zeros((tl, CONV_CH), F32) + cb_ref[...]
    for k in range(CONV_K):
        acc = acc + ext_ref[k + HALO_OFF:k + HALO_OFF + tl, :] * w_ref[k:k + 1, :]
    c_ref[0] = _ln_silu(acc, lg_ref[...], lb_ref[...]).astype(c_ref.dtype)
    st_ref[0] = ext_ref[tl + HALO_OFF:tl + HALO, :]


def _conv_prompt(u, conv_w, conv_b, ln_g, ln_b, tl):
    n, l, _ = u.shape
    const = lambda b, j: (0, 0)
    return pl.pallas_call(
        _conv_prompt_body,
        grid=(n, l // tl),
        in_specs=[
            pl.BlockSpec((1, tl, CONV_CH), lambda b, j: (b, j, 0)),
            pl.BlockSpec((CONV_K, CONV_CH), const),
            pl.BlockSpec((1, CONV_CH), const),
            pl.BlockSpec((1, CONV_CH), const),
            pl.BlockSpec((1, CONV_CH), const),
        ],
        out_specs=[
            pl.BlockSpec((1, tl, CONV_CH), lambda b, j: (b, j, 0)),
            pl.BlockSpec((1, CONV_K - 1, CONV_CH), lambda b, j: (b, 0, 0)),
        ],
        out_shape=[
            jax.ShapeDtypeStruct((n, l, CONV_CH), BF16),
            jax.ShapeDtypeStruct((n, CONV_K - 1, CONV_CH), F32),
        ],
        scratch_shapes=[pltpu.VMEM((tl + HALO, CONV_CH), F32)],
        compiler_params=_cparams(("arbitrary", "arbitrary")),
        name="conv_prompt",
    )(u, conv_w, conv_b, ln_g, ln_b)


def _conv_sample_body(u_ref, st_ref, w_ref, cb_ref, lg_ref, lb_ref, c_ref, nst_ref, ext_ref):
    nb, l, _ = u_ref.shape
    ext_ref[:, HALO_OFF:HALO, :] = st_ref[...]
    ext_ref[:, HALO:, :] = u_ref[...]
    acc = jnp.zeros((nb, l, CONV_CH), F32) + cb_ref[...]
    for k in range(CONV_K):
        acc = acc + ext_ref[:, k + HALO_OFF:k + HALO_OFF + l, :] * w_ref[k:k + 1, :]
    c_ref[...] = _ln_silu(acc, lg_ref[...], lb_ref[...]).astype(c_ref.dtype)
    nst_ref[...] = ext_ref[:, l + HALO_OFF:l + HALO, :]


def _conv_sample(u, state, conv_w, conv_b, ln_g, ln_b, nb):
    n, l, _ = u.shape
    const = lambda b: (0, 0)
    blk3 = lambda b: (b, 0, 0)
    return pl.pallas_call(
        _conv_sample_body,
        grid=(n // nb,),
        in_specs=[
            pl.BlockSpec((nb, l, CONV_CH), blk3),
            pl.BlockSpec((nb, CONV_K - 1, CONV_CH), blk3),
            pl.BlockSpec((CONV_K, CONV_CH), const),
            pl.BlockSpec((1, CONV_CH), const),
            pl.BlockSpec((1, CONV_CH), const),
            pl.BlockSpec((1, CONV_CH), const),
        ],
        out_specs=[
            pl.BlockSpec((nb, l, CONV_CH), blk3),
            pl.BlockSpec((nb, CONV_K - 1, CONV_CH), blk3),
        ],
        out_shape=[
            jax.ShapeDtypeStruct((n, l, CONV_CH), BF16),
            jax.ShapeDtypeStruct((n, CONV_K - 1, CONV_CH), F32),
        ],
        scratch_shapes=[pltpu.VMEM((nb, l + HALO, CONV_CH), F32)],
        compiler_params=_cparams(("parallel",)),
        name="conv_sample",
    )(u, state, conv_w, conv_b, ln_g, ln_b)


def _decay_tables(c):
    lg = jnp.log(1.0 - 2.0 ** (-5.0 - jnp.arange(RET_HEADS, dtype=F32)))
    idx = jnp.arange(c, dtype=F32)
    rel = idx[:, None] - idx[None, :]
    dmat = jnp.where(rel[None] >= 0, jnp.exp(jnp.maximum(rel, 0.0)[None] * lg[:, None, None]), 0.0)
    xi = jnp.exp((idx + 1.0)[None, :] * lg[:, None])
    zeta = jnp.exp((c - 1.0 - idx)[None, :] * lg[:, None])
    gc = jnp.exp(c * lg)
    xi_b = jnp.broadcast_to(xi[:, :, None], (RET_HEADS, c, HEAD_DIM))
    zeta_b = jnp.broadcast_to(zeta[:, :, None], (RET_HEADS, c, HEAD_DIM))
    gc_b = jnp.broadcast_to(gc[:, None, None], (RET_HEADS, 1, HEAD_DIM))
    return dmat, xi_b, zeta_b, gc_b


def _group_norm(o):
    mu = jnp.mean(o, axis=-1, keepdims=True)
    d = o - mu
    var = jnp.mean(d * d, axis=-1, keepdims=True)
    return d * lax.rsqrt(var + EPS)


def _ret_chunk(qh, kh, vh, r, dmat, xi, zeta, gc):
    qb = qh.astype(BF16)
    kb = kh.astype(BF16)
    vb = vh.astype(BF16)
    s = lax.dot_general(qb, kb, (((1,), (1,)), ((), ())), preferred_element_type=F32) * dmat
    o = jnp.dot(s.astype(BF16), vb, preferred_element_type=F32)
    o = o + jnp.dot(qb, r.astype(BF16), preferred_element_type=F32) * xi
    kz = (kh.astype(F32) * zeta).astype(BF16)
    r_new = r * gc + lax.dot_general(kz, vb, (((0,), (0,)), ((), ())), preferred_element_type=F32)
    return o, r_new


def _ret_prompt_body(q_ref, k_ref, v_ref, gs_ref, d_ref, xi_ref, zeta_ref, gc_ref, o_ref, st_ref, r_ref):
    j = pl.program_id(1)

    @pl.when(j == 0)
    def _():
        r_ref[...] = jnp.zeros_like(r_ref)

    n_chunks = q_ref.shape[0] // CHUNK
    for hh in range(RET_HEADS):
        sl = slice(hh * HEAD_DIM, (hh + 1) * HEAD_DIM)
        r = r_ref[hh]
        for ci in range(n_chunks):
            rows = slice(ci * CHUNK, (ci + 1) * CHUNK)
            o, r = _ret_chunk(q_ref[rows, sl], k_ref[rows, sl], v_ref[rows, sl], r,
                              d_ref[hh], xi_ref[hh], zeta_ref[hh], gc_ref[hh])
            o_ref[rows, sl] = (gs_ref[rows, sl] * _group_norm(o)).astype(o_ref.dtype)
        r_ref[hh] = r
    st_ref[0] = r_ref[...]


def _ret_prompt(q, k, v, gs, n, l, tl):
    dmat, xi, zeta, gc = _decay_tables(CHUNK)
    per = l // tl
    row = lambda b, j: (b * per + j, 0)
    c3 = lambda b, j: (0, 0, 0)
    return pl.pallas_call(
        _ret_prompt_body,
        grid=(n, per),
        in_specs=[pl.BlockSpec((tl, RET_WIDTH), row)] * 4 + [
            pl.BlockSpec((RET_HEADS, CHUNK, CHUNK), c3),
            pl.BlockSpec((RET_HEADS, CHUNK, HEAD_DIM), c3),
            pl.BlockSpec((RET_HEADS, CHUNK, HEAD_DIM), c3),
            pl.BlockSpec((RET_HEADS, 1, HEAD_DIM), c3),
        ],
        out_specs=[
            pl.BlockSpec((tl, RET_WIDTH), row),
            pl.BlockSpec((1, RET_HEADS, HEAD_DIM, HEAD_DIM), lambda b, j: (b, 0, 0, 0)),
        ],
        out_shape=[
            jax.ShapeDtypeStruct((n * l, RET_WIDTH), BF16),
            jax.ShapeDtypeStruct((n, RET_HEADS, HEAD_DIM, HEAD_DIM), F32),
        ],
        scratch_shapes=[pltpu.VMEM((RET_HEADS, HEAD_DIM, HEAD_DIM), F32)],
        compiler_params=_cparams(("arbitrary", "arbitrary")),
        name="ret_prompt",
    )(q, k, v, gs, dmat, xi, zeta, gc)


def _ret_sample_body(q_ref, k_ref, v_ref, gs_ref, st_ref, d_ref, xi_ref, zeta_ref, gc_ref, o_ref, nst_ref):
    nb = st_ref.shape[0]
    l = q_ref.shape[0] // nb
    for b in range(nb):
        rows = slice(b * l, (b + 1) * l)
        for hh in range(RET_HEADS):
            sl = slice(hh * HEAD_DIM, (hh + 1) * HEAD_DIM)
            o, r = _ret_chunk(q_ref[rows, sl], k_ref[rows, sl], v_ref[rows, sl], st_ref[b, hh],
                              d_ref[hh], xi_ref[hh], zeta_ref[hh], gc_ref[hh])
            o_ref[rows, sl] = (gs_ref[rows, sl] * _group_norm(o)).astype(o_ref.dtype)
            nst_ref[b, hh] = r


def _ret_sample(q, k, v, gs, state, l, nb):
    n = state.shape[0]
    dmat, xi, zeta, gc = _decay_tables(l)
    row = lambda b: (b, 0)
    c3 = lambda b: (0, 0, 0)
    blk4 = lambda b: (b, 0, 0, 0)
    return pl.pallas_call(
        _ret_sample_body,
        grid=(n // nb,),
        in_specs=[pl.BlockSpec((nb * l, RET_WIDTH), row)] * 4 + [
            pl.BlockSpec((nb, RET_HEADS, HEAD_DIM, HEAD_DIM), blk4),
            pl.BlockSpec((RET_HEADS, l, l), c3),
            pl.BlockSpec((RET_HEADS, l, HEAD_DIM), c3),
            pl.BlockSpec((RET_HEADS, l, HEAD_DIM), c3),
            pl.BlockSpec((RET_HEADS, 1, HEAD_DIM), c3),
        ],
        out_specs=[
            pl.BlockSpec((nb * l, RET_WIDTH), row),
            pl.BlockSpec((nb, RET_HEADS, HEAD_DIM, HEAD_DIM), blk4),
        ],
        out_shape=[
            jax.ShapeDtypeStruct((n * l, RET_WIDTH), BF16),
            jax.ShapeDtypeStruct((n, RET_HEADS, HEAD_DIM, HEAD_DIM), F32),
        ],
        compiler_params=_cparams(("parallel",)),
        name="ret_sample",
    )(q, k, v, gs, state, dmat, xi, zeta, gc)


def _split3(x):
    hi = x.astype(BF16)
    r1 = x - hi.astype(F32)
    mid = r1.astype(BF16)
    lo = (r1 - mid.astype(F32)).astype(BF16)
    return hi, mid, lo


def _dot_hp(t, w_hi, w_mid, w_lo):
    t_hi, t_mid, t_lo = _split3(t)
    d = functools.partial(jnp.dot, preferred_element_type=F32)
    small = d(t_lo, w_hi) + d(t_mid, w_mid) + d(t_hi, w_lo)
    return (d(t_hi, w_hi) + (d(t_mid, w_hi) + d(t_hi, w_mid))) + small


def _route(lg, le):
    lane = lax.broadcasted_iota(jnp.int32, lg.shape, 1)
    big = jnp.int32(LANES)
    m = jnp.max(lg, axis=-1, keepdims=True)
    g_top = 1.0 / jnp.sum(jnp.exp(lg - m), axis=-1, keepdims=True)
    g_idx = jnp.min(jnp.where(lg == m, lane, big), axis=-1, keepdims=True)
    in_group = (lane >= g_idx * EXPERTS_PER_GROUP) & (lane < (g_idx + 1) * EXPERTS_PER_GROUP)
    lem = jnp.where(in_group, le, -1e30)
    m2 = jnp.max(lem, axis=-1, keepdims=True)
    pe = jnp.where(in_group, jnp.exp(lem - m2), 0.0)
    p1 = jnp.max(pe, axis=-1, keepdims=True)
    e1 = jnp.min(jnp.where(in_group & (pe == p1), lane, big), axis=-1, keepdims=True)
    rest = in_group & (lane != e1)
    pe2 = jnp.where(rest, pe, -1.0)
    p2 = jnp.max(pe2, axis=-1, keepdims=True)
    e2 = jnp.min(jnp.where(rest & (pe2 == p2), lane, big), axis=-1, keepdims=True)
    scale = g_top / (p1 + p2)
    rec = jnp.where(lane == 0, e1.astype(F32), jnp.where(lane == 1, e2.astype(F32), 0.0))
    return rec + jnp.where(lane == 2, p1 * scale, jnp.where(lane == 3, p2 * scale, 0.0))


def _outproj_body(c_ref, o_ref, x_ref, wo_ref, g2_ref, rg_ref, rgb_ref, re_ref, reb_ref,
                  x1_ref, t_ref, rec_ref):
    x1 = x_ref[...] + jnp.dot(c_ref[...], wo_ref[0:CONV_CH, :], preferred_element_type=F32)
    x1 = x1 + jnp.dot(o_ref[...], wo_ref[CONV_CH:, :], preferred_element_type=F32)
    x1_ref[...] = x1
    t = _rms(x1, g2_ref[...])
    t_ref[...] = t.astype(t_ref.dtype)
    lg = _dot_hp(t, rg_ref[0], rg_ref[1], rg_ref[2]) + rgb_ref[...]
    le = _dot_hp(t, re_ref[0], re_ref[1], re_ref[2]) + reb_ref[...]
    rec_ref[...] = _route(lg, le)


def _outproj(c, o, x, w_out, g2, rg3, rgb, re3, reb, tm):
    t = x.shape[0]
    row = lambda i: (i, 0)
    const = lambda i: (0, 0)
    c3 = lambda i: (0, 0, 0)
    return pl.pallas_call(
        _outproj_body,
        grid=(t // tm,),
        in_specs=[
            pl.BlockSpec((tm, CONV_CH), row),
            pl.BlockSpec((tm, RET_WIDTH), row),
            pl.BlockSpec((tm, D_MODEL), row),
            pl.BlockSpec((D_MODEL, D_MODEL), const),
            pl.BlockSpec((1, D_MODEL), const),
            pl.BlockSpec((3, D_MODEL, LANES), c3),
            pl.BlockSpec((1, LANES), const),
            pl.BlockSpec((3, D_MODEL, LANES), c3),
            pl.BlockSpec((1, LANES), const),
        ],
        out_specs=[
            pl.BlockSpec((tm, D_MODEL), row),
            pl.BlockSpec((tm, D_MODEL), row),
            pl.BlockSpec((tm, LANES), row),
        ],
        out_shape=[
            jax.ShapeDtypeStruct((t, D_MODEL), F32),
            jax.ShapeDtypeStruct((t, D_MODEL), BF16),
            jax.ShapeDtypeStruct((t, LANES), F32),
        ],
        compiler_params=_cparams(("parallel",)),
        name="outproj_router",
    )(c, o, x, w_out, g2, rg3, rgb, re3, reb)


def _iota_f32(shape, dim):
    return lax.broadcasted_iota(jnp.int32, shape, dim).astype(F32)


def _plan_body(rec_ref, pos_ref, post_ref, gsp_ref, meta_ref):
    rec = rec_ref[...]
    lane = _iota_f32(rec.shape, 1)
    a1 = lane == rec[:, 0:1]
    a2 = lane == rec[:, 1:2]
    a1f = jnp.where(a1, 1.0, 0.0)
    a2f = jnp.where(a2, 1.0, 0.0)
    ltri = jnp.where(_iota_f32((SUB, SUB), 1) < _iota_f32((SUB, SUB), 0), 1.0, 0.0).astype(BF16)
    c1 = jnp.dot(ltri, a1f.astype(BF16), preferred_element_type=F32)
    c2 = jnp.dot(ltri, a2f.astype(BF16), preferred_element_type=F32)
    n1 = jnp.sum(a1f, axis=0, keepdims=True)
    n2 = jnp.sum(a2f, axis=0, keepdims=True)
    cnt = jnp.floor((n1 + n2 + (ROW_ALIGN - 1.0)) * (1.0 / ROW_ALIGN))
    utri = jnp.where(_iota_f32((LANES, LANES), 0) < _iota_f32((LANES, LANES), 1), 1.0, 0.0).astype(BF16)
    start = jnp.dot(jnp.broadcast_to(cnt, (8, LANES)).astype(BF16), utri, preferred_element_type=F32)[0:1]
    base1 = start * ROW_ALIGN
    base2 = base1 + n1
    pos1 = jnp.sum(jnp.where(a1, c1 + base1, 0.0), axis=1, keepdims=True)
    pos2 = jnp.sum(jnp.where(a2, c2 + base2, 0.0), axis=1, keepdims=True)
    posm = jnp.where(lane == 0.0, pos1, jnp.where(lane == 1.0, pos2, 0.0))
    pos_ref[...] = posm
    post_ref[0] = posm.T[0:8, :]
    g1 = _split3(rec[:, 2:3])
    g2 = _split3(rec[:, 3:4])
    gsp = jnp.zeros(rec.shape, F32)
    for i in range(3):
        gsp = jnp.where(lane == float(i), g1[i].astype(F32), gsp)
        gsp = jnp.where(lane == float(3 + i), g2[i].astype(F32), gsp)
    gsp_ref[...] = gsp.astype(BF16)
    row = _iota_f32((8, LANES), 0)
    meta_ref[0] = jnp.where(row == 0.0, start, jnp.where(row == 1.0, cnt, 0.0))


def _plan(rec):
    t = rec.shape[0]
    nsub = t // SUB
    row = lambda i: (i, 0)
    blk3 = lambda i: (i, 0, 0)
    return pl.pallas_call(
        _plan_body,
        grid=(nsub,),
        in_specs=[pl.BlockSpec((SUB, LANES), row)],
        out_specs=[
            pl.BlockSpec((SUB, LANES), row),
            pl.BlockSpec((1, 8, SUB), blk3),
            pl.BlockSpec((SUB, LANES), row),
            pl.BlockSpec((1, 8, LANES), blk3),
        ],
        out_shape=[
            jax.ShapeDtypeStruct((t, LANES), F32),
            jax.ShapeDtypeStruct((nsub, 8, SUB), F32),
            jax.ShapeDtypeStruct((t, LANES), BF16),
            jax.ShapeDtypeStruct((nsub, 8, LANES), F32),
        ],
        compiler_params=_cparams(("parallel",)),
        name="moe_plan",
    )(rec)


def _moe_body(start_ref, cnt_ref, t_ref, pos_ref, post_ref, gsp_ref, wg_ref, wu_ref, wd_ref, y_ref,
              s_ref, gs_ref, lhs_ref, gl_ref, acc_ref):
    w = pl.program_id(0)
    j = pl.program_id(1)
    nsw = t_ref.shape[0] // SUB
    ec = wg_ref.shape[0]
    dot = functools.partial(jnp.dot, preferred_element_type=F32)

    def used_rows(s):
        last = (w * nsw + s) * N_EXPERTS + (N_EXPERTS - 1)
        return (start_ref[last] + cnt_ref[last]) * ROW_ALIGN

    @pl.when(j == 0)
    def _():
        @pl.when(w == 0)
        def _():
            lhs_ref[...] = jnp.zeros_like(lhs_ref)
            gl_ref[...] = jnp.zeros_like(gl_ref)

        for s in range(nsw):
            p1 = post_ref[s, 0:1, :]
            p2 = post_ref[s, 1:2, :]
            tok = t_ref[s * SUB:(s + 1) * SUB, :]
            gsp = gsp_ref[s * SUB:(s + 1) * SUB, :]
            used = used_rows(s)
            for b in range(SUBP // PBLK):
                @pl.when(b * PBLK < used)
                def _(b=b):
                    r = _iota_f32((PBLK, SUB), 0) + float(b * PBLK)
                    m1 = jnp.where(r == p1, 1.0, 0.0)
                    m2 = jnp.where(r == p2, 1.0, 0.0)
                    rows = pl.ds(s * SUBP + b * PBLK, PBLK)
                    s_ref[rows, :] = dot((m1 + m2).astype(BF16), tok).astype(BF16)
                    g = dot(m1.astype(BF16), gsp)[:, 0:3] + dot(m2.astype(BF16), gsp)[:, 3:6]
                    g = g[:, 0:1] + g[:, 1:2] + g[:, 2:3]
                    gs_ref[rows, :] = jnp.broadcast_to(g, (PBLK, LANES))

    for el in range(ec):
        e = j * ec + el

        def run_copy(to_lhs, e=e):
            off = jnp.int32(0)
            for s in range(nsw):
                idx = (w * nsw + s) * N_EXPERTS + e
                c = cnt_ref[idx]
                src0 = s * SUBP + start_ref[idx] * ROW_ALIGN

                def cp(i, carry, src0=src0, off=off):
                    a = pl.ds(pl.multiple_of(src0 + i * ROW_ALIGN, ROW_ALIGN), ROW_ALIGN)
                    b = pl.ds(pl.multiple_of(off + i * ROW_ALIGN, ROW_ALIGN), ROW_ALIGN)
                    if to_lhs:
                        lhs_ref[b, :] = s_ref[a, :]
                        gl_ref[b, :] = gs_ref[a, :]
                    else:
                        s_ref[a, :] = lhs_ref[b, :]
                    return carry

                lax.fori_loop(0, c, cp, 0)
                off = off + c * ROW_ALIGN
            return off

        total = run_copy(True)

        def mlp(i, carry, el=el):
            rows = pl.ds(pl.multiple_of(i * MBLK, MBLK), MBLK)
            x = lhs_ref[rows, :]
            g = gl_ref[rows, :]
            h1 = dot(x, wg_ref[el])
            h2 = dot(x, wu_ref[el])
            hid = (h1 * jax.nn.sigmoid(h1)) * h2
            hid = jnp.concatenate([hid[:, i0:i0 + LANES] * g for i0 in range(0, EXPERT_FF, LANES)], axis=1)
            lhs_ref[rows, :] = dot(hid.astype(BF16), wd_ref[el]).astype(BF16)
            return carry

        lax.fori_loop(0, lax.div(total + (MBLK - 1), MBLK), mlp, 0)
        run_copy(False)

    @pl.when(j == pl.num_programs(1) - 1)
    def _():
        for s in range(nsw):
            p1 = pos_ref[s * SUB:(s + 1) * SUB, 0:1]
            p2 = pos_ref[s * SUB:(s + 1) * SUB, 1:2]
            used = used_rows(s)
            acc_ref[...] = jnp.zeros_like(acc_ref)
            for b in range(SUBP // PBLK):
                @pl.when(b * PBLK < used)
                def _(b=b):
                    r = _iota_f32((SUB, PBLK), 1) + float(b * PBLK)
                    m = jnp.where(r == p1, 1.0, jnp.where(r == p2, 1.0, 0.0)).astype(BF16)
                    acc_ref[...] += dot(m, s_ref[pl.ds(s * SUBP + b * PBLK, PBLK), :])
            y_ref[s * SUB:(s + 1) * SUB, :] = acc_ref[...].astype(y_ref.dtype)


def _moe(t, pos, post, gsp, start, cnt, wg, wu, wd, win, ec):
    n = t.shape[0]
    nsw = win // SUB
    row = lambda i, j, *_: (i, 0)
    wblk = lambda i, j, *_: (j, 0, 0)
    cap = -(-(win + nsw * ROW_ALIGN) // MBLK) * MBLK
    grid_spec = pltpu.PrefetchScalarGridSpec(
        num_scalar_prefetch=2,
        grid=(n // win, N_EXPERTS // ec),
        in_specs=[
            pl.BlockSpec((win, D_MODEL), row),
            pl.BlockSpec((win, LANES), row),
            pl.BlockSpec((nsw, 8, SUB), lambda i, j, *_: (i, 0, 0)),
            pl.BlockSpec((win, LANES), row),
            pl.BlockSpec((ec, D_MODEL, EXPERT_FF), wblk),
            pl.BlockSpec((ec, D_MODEL, EXPERT_FF), wblk),
            pl.BlockSpec((ec, EXPERT_FF, D_MODEL), wblk),
        ],
        out_specs=pl.BlockSpec((win, D_MODEL), row),
        scratch_shapes=[
            pltpu.VMEM((nsw * SUBP, D_MODEL), BF16),
            pltpu.VMEM((nsw * SUBP, LANES), F32),
            pltpu.VMEM((cap, D_MODEL), BF16),
            pltpu.VMEM((cap, LANES), F32),
            pltpu.VMEM((SUB, D_MODEL), F32),
        ],
    )
    return pl.pallas_call(
        _moe_body,
        grid_spec=grid_spec,
        out_shape=jax.ShapeDtypeStruct((n, D_MODEL), BF16),
        compiler_params=pltpu.CompilerParams(dimension_semantics=("arbitrary", "arbitrary"),
                                             vmem_limit_bytes=MOE_VMEM_LIMIT),
        name="moe",
    )(start, cnt, t, pos, post, gsp, wg, wu, wd)


def _ple_body(x1_ref, ym_ref, p_ref, wp_ref, gp_ref, wpg_ref, gf_ref, y_ref):
    x2 = x1_ref[...] + ym_ref[...].astype(F32)
    ple = _rms(jnp.dot(p_ref[...].astype(BF16), wp_ref[...], preferred_element_type=F32), gp_ref[...])
    gate = jax.nn.sigmoid(jnp.dot(x2.astype(BF16), wpg_ref[...], preferred_element_type=F32))
    y_ref[...] = _rms(x2 + ple * gate, gf_ref[...])


def _ple(x1, ym, p, w_ple, gp, w_ple_gate, gf, tm):
    t = x1.shape[0]
    row = lambda i: (i, 0)
    const = lambda i: (0, 0)
    return pl.pallas_call(
        _ple_body,
        grid=(t // tm,),
        in_specs=[
            pl.BlockSpec((tm, D_MODEL), row),
            pl.BlockSpec((tm, D_MODEL), row),
            pl.BlockSpec((tm, PLE_DIM), row),
            pl.BlockSpec((PLE_DIM, D_MODEL), const),
            pl.BlockSpec((1, D_MODEL), const),
            pl.BlockSpec((D_MODEL, D_MODEL), const),
            pl.BlockSpec((1, D_MODEL), const),
        ],
        out_specs=pl.BlockSpec((tm, D_MODEL), row),
        out_shape=jax.ShapeDtypeStruct((t, D_MODEL), F32),
        compiler_params=_cparams(("parallel",)),
        name="ple_final",
    )(x1, ym, p, w_ple, gp, w_ple_gate, gf)


def _rope_tables(pos):
    half = HEAD_DIM // 2
    inv = ROPE_BASE ** (-jnp.arange(half, dtype=F32) / half)
    ang = pos[:, None] * inv[None, :]
    cos = jnp.cos(ang)
    sin = jnp.sin(ang)
    return jnp.concatenate([cos, cos], axis=-1), jnp.concatenate([-sin, sin], axis=-1)


def _pad_router(w, b):
    pad = LANES - w.shape[1]
    w3 = jnp.stack(_split3(jnp.pad(w, ((0, 0), (0, pad)))))
    bp = jnp.pad(b, (0, pad), constant_values=-1e30)[None, :]
    return w3, bp


def kernel(x_prompt, x_sample, p_prompt, p_sample, state_conv, state_ret, w_in, conv_w, conv_b, conv_ln_g, conv_ln_b, w_out, norm1_g, norm2_g, router_group_w, router_group_b, router_expert_w, router_expert_b, w_expert_gate, w_expert_up, w_expert_down, w_ple, ple_norm_g, w_ple_gate, final_norm_g):
    assert w_in.shape[0] == 1, "single-layer trunk"
    nb, seq, _ = x_prompt.shape
    ns, dseq, _ = x_sample.shape
    tm = 512

    w_in_b = w_in[0].astype(BF16)
    w_out_b = w_out[0].astype(BF16)
    wg_b = w_expert_gate[0].astype(BF16)
    wu_b = w_expert_up[0].astype(BF16)
    wd_b = w_expert_down[0].astype(BF16)
    w_ple_b = w_ple[0].astype(BF16)
    w_pg_b = w_ple_gate[0].astype(BF16)
    g1 = norm1_g[0][None, :]
    g2 = norm2_g[0][None, :]
    gp = ple_norm_g[0][None, :]
    gf = final_norm_g[None, :]
    cb = conv_b[0][None, :]
    lng = conv_ln_g[0][None, :]
    lnb = conv_ln_b[0][None, :]
    rg3, rgb = _pad_router(router_group_w[0], router_group_b[0])
    re3, reb = _pad_router(router_expert_w[0], router_expert_b[0])

    cos_p, sin_p = _rope_tables(jnp.arange(seq, dtype=F32) + jnp.float32(0))
    pos_s = jnp.tile(jnp.arange(dseq, dtype=F32) + jnp.float32(PAST_LEN), tm // dseq)
    cos_s, sin_s = _rope_tables(pos_s)

    def tail(c, o, x, p):
        x1, t, rec = _outproj(c, o, x, w_out_b, g2, rg3, rgb, re3, reb, tm)
        pos, post, gsp, meta = _plan(rec)
        start = meta[:, 0, :N_EXPERTS].astype(jnp.int32).reshape(-1)
        cnt = meta[:, 1, :N_EXPERTS].astype(jnp.int32).reshape(-1)
        ym = _moe(t, pos, post, gsp, start, cnt, wg_b, wu_b, wd_b, min(MOE_WINDOW, x.shape[0]), 2)
        return _ple(x1, ym, p, w_ple_b, gp, w_pg_b, gf, tm)

    xp = x_prompt.reshape(nb * seq, D_MODEL)
    u, q, k, v, gs = _inproj(xp, g1, w_in_b, cos_p, sin_p, tm, seq // tm, BF16)
    c, conv_p = _conv_prompt(u.reshape(nb, seq, CONV_CH), conv_w[0], cb, lng, lnb, tm)
    o, ret_p = _ret_prompt(q, k, v, gs, nb, seq, tm)
    y_p = tail(c.reshape(nb * seq, CONV_CH), o, xp, p_prompt[0].reshape(nb * seq, PLE_DIM))

    xs = x_sample.reshape(ns * dseq, D_MODEL)
    u, q, k, v, gs = _inproj(xs, g1, w_in_b, cos_s, sin_s, tm, 1, F32)
    c, conv_s = _conv_sample(u.reshape(ns, dseq, CONV_CH), state_conv[0], conv_w[0], cb, lng, lnb, 16)
    o, ret_s = _ret_sample(q, k, v, gs, state_ret[0], dseq, 8)
    y_s = tail(c.reshape(ns * dseq, CONV_CH), o, xs, p_sample[0].reshape(ns * dseq, PLE_DIM))

    return (y_p.reshape(nb, seq, D_MODEL), y_s.reshape(ns, dseq, D_MODEL),
            conv_p[None], ret_p[None], conv_s[None], ret_s[None])
```

```python
import functools

import jax
import jax.numpy as jnp
from jax import lax
from jax.experimental import pallas as pl
from jax.experimental.pallas import tpu as pltpu

F32 = jnp.float32
BF16 = jnp.bfloat16

D_MODEL = 1024
PLE_DIM = 256
CONV_CH = 512
CONV_K = 31
RET_WIDTH = 512
RET_HEADS = 4
HEAD_DIM = 128
CHUNK = 128
ROPE_BASE = 10000.0
N_GROUPS = 4
EXPERTS_PER_GROUP = 8
N_EXPERTS = 32
EXPERT_FF = 256
IN_COLS = 3072
EPS = 1e-6
PAST_LEN = 16384

LANES = 128
SUBLANES = 8
HALO = 32
HALO_OFF = HALO - (CONV_K - 1)
VMEM_LIMIT = 48 * 1024 * 1024
MOE_VMEM_LIMIT = 56 * 1024 * 1024

SUB = 256
ROW_ALIGN = 16
PBLK = 256
SUBP = -(-(2 * SUB + N_EXPERTS * (ROW_ALIGN - 1)) // PBLK) * PBLK
MBLK = 256
MOE_WINDOW = 2048


def _cparams(sem):
    return pltpu.CompilerParams(dimension_semantics=sem, vmem_limit_bytes=VMEM_LIMIT)


def _rms(x, g):
    return x * lax.rsqrt(jnp.mean(x * x, axis=-1, keepdims=True) + EPS) * g


def _inproj_body(x_ref, g1_ref, w_ref, cos_ref, sin_ref, u_ref, q_ref, k_ref, v_ref, gs_ref):
    h = _rms(x_ref[...], g1_ref[...]).astype(BF16)
    z = jnp.dot(h, w_ref[...], preferred_element_type=F32)
    a = z[:, :CONV_CH]
    b = z[:, CONV_CH:2 * CONV_CH]
    u_ref[...] = a * jax.nn.sigmoid(b)
    cos = cos_ref[...]
    sin = sin_ref[...]
    q0 = 2 * CONV_CH
    k0 = q0 + RET_WIDTH
    for hh in range(RET_HEADS):
        sl = slice(hh * HEAD_DIM, (hh + 1) * HEAD_DIM)
        qh = z[:, q0 + hh * HEAD_DIM:q0 + (hh + 1) * HEAD_DIM]
        kh = z[:, k0 + hh * HEAD_DIM:k0 + (hh + 1) * HEAD_DIM]
        q_ref[:, sl] = (qh * cos + pltpu.roll(qh, HEAD_DIM // 2, 1) * sin).astype(q_ref.dtype)
        kr = (kh * cos + pltpu.roll(kh, HEAD_DIM // 2, 1) * sin) * (HEAD_DIM ** -0.5)
        k_ref[:, sl] = kr.astype(k_ref.dtype)
    v_ref[...] = z[:, k0 + RET_WIDTH:k0 + 2 * RET_WIDTH].astype(v_ref.dtype)
    g = z[:, k0 + 2 * RET_WIDTH:]
    gs_ref[...] = g * jax.nn.sigmoid(g)


def _inproj(x, g1, w_in, cos, sin, tm, table_blocks, qkv_dtype):
    t = x.shape[0]
    row = lambda i: (i, 0)
    const = lambda i: (0, 0)
    tab = (lambda i: (i % table_blocks, 0)) if table_blocks > 1 else const
    return pl.pallas_call(
        _inproj_body,
        grid=(t // tm,),
        in_specs=[
            pl.BlockSpec((tm, D_MODEL), row),
            pl.BlockSpec((1, D_MODEL), const),
            pl.BlockSpec((D_MODEL, IN_COLS), const),
            pl.BlockSpec((tm, HEAD_DIM), tab),
            pl.BlockSpec((tm, HEAD_DIM), tab),
        ],
        out_specs=[pl.BlockSpec((tm, CONV_CH), row)] + [pl.BlockSpec((tm, RET_WIDTH), row)] * 4,
        out_shape=[
            jax.ShapeDtypeStruct((t, CONV_CH), F32),
            jax.ShapeDtypeStruct((t, RET_WIDTH), qkv_dtype),
            jax.ShapeDtypeStruct((t, RET_WIDTH), qkv_dtype),
            jax.ShapeDtypeStruct((t, RET_WIDTH), qkv_dtype),
            jax.ShapeDtypeStruct((t, RET_WIDTH), F32),
        ],
        compiler_params=_cparams(("parallel",)),
        name="inproj",
    )(x, g1, w_in, cos, sin)


def _ln_silu(acc, g, b):
    mu = jnp.mean(acc, axis=-1, keepdims=True)
    d = acc - mu
    var = jnp.mean(d * d, axis=-1, keepdims=True)
    y = d * lax.rsqrt(var + EPS) * g + b
    return y * jax.nn.sigmoid(y)


def _dwconv(load, w_ref, rows, time_axis):
    acc = None
    for b in range(SUBLANES):
        part = None
        for a in range((CONV_K + HALO_OFF) // SUBLANES + 1):
            k = SUBLANES * a + b - HALO_OFF
            if 0 <= k < CONV_K:
                term = load(SUBLANES * a, rows + SUBLANES) * w_ref[k:k + 1, :]
                part = term if part is None else part + term
        if part is not None:
            shifted = lax.slice_in_dim(part, b, b + rows, axis=time_axis)
            acc = shifted if acc is None else acc + shifted
    return acc


def _conv_prompt_body(u_ref, w_ref, cb_ref, lg_ref, lb_ref, c_ref, st_ref, ext_ref):
    j = pl.program_id(1)
    tl = u_ref.shape[1]

    @pl.when(j == 0)
    def _():
        ext_ref[0:HALO, :] = jnp.zeros((HALO, CONV_CH), F32)
        ext_ref[tl + HALO:, :] = jnp.zeros((SUBLANES, CONV_CH), F32)

    @pl.when(j > 0)
    def _():
        ext_ref[0:HALO, :] = ext_ref[tl:tl + HALO, :]

    ext_ref[HALO:tl + HALO, :] = u_ref[0]
    acc = _dwconv(lambda s, n: ext_ref[s:s + n, :], w_ref, tl, 0) + cb_ref[...]
    c_ref[0] = _ln_silu(acc, lg_ref[...], lb_ref[...]).astype(c_ref.dtype)
    st_ref[0] = ext_ref[tl + HALO_OFF:tl + HALO, :]


def _conv_prompt(u, conv_w, conv_b, ln_g, ln_b, tl):
    n, l, _ = u.shape
    const = lambda b, j: (0, 0)
    return pl.pallas_call(
        _conv_prompt_body,
        grid=(n, l // tl),
        in_specs=[
            pl.BlockSpec((1, tl, CONV_CH), lambda b, j: (b, j, 0)),
            pl.BlockSpec((CONV_K, CONV_CH), const),
            pl.BlockSpec((1, CONV_CH), const),
            pl.BlockSpec((1, CONV_CH), const),
            pl.BlockSpec((1, CONV_CH), const),
        ],
        out_specs=[
            pl.BlockSpec((1, tl, CONV_CH), lambda b, j: (b, j, 0)),
            pl.BlockSpec((1, CONV_K - 1, CONV_CH), lambda b, j: (b, 0, 0)),
        ],
        out_shape=[
            jax.ShapeDtypeStruct((n, l, CONV_CH), BF16),
            jax.ShapeDtypeStruct((n, CONV_K - 1, CONV_CH), F32),
        ],
        scratch_shapes=[pltpu.VMEM((tl + HALO + SUBLANES, CONV_CH), F32)],
        compiler_params=_cparams(("arbitrary", "arbitrary")),
        name="conv_prompt",
    )(u, conv_w, conv_b, ln_g, ln_b)


def _conv_sample_body(u_ref, st_ref, w_ref, cb_ref, lg_ref, lb_ref, c_ref, nst_ref, ext_ref):
    nb, l, _ = u_ref.shape
    ext_ref[:, 0:HALO_OFF, :] = jnp.zeros((nb, HALO_OFF, CONV_CH), F32)
    ext_ref[:, HALO_OFF:HALO, :] = st_ref[...]
    ext_ref[:, HALO:l + HALO, :] = u_ref[...]
    ext_ref[:, l + HALO:, :] = jnp.zeros((nb, SUBLANES, CONV_CH), F32)
    acc = _dwconv(lambda s, n: ext_ref[:, s:s + n, :], w_ref, l, 1) + cb_ref[...]
    c_ref[...] = _ln_silu(acc, lg_ref[...], lb_ref[...]).astype(c_ref.dtype)
    nst_ref[...] = ext_ref[:, l + HALO_OFF:l + HALO, :]


def _conv_sample(u, state, conv_w, conv_b, ln_g, ln_b, nb):
    n, l, _ = u.shape
    const = lambda b: (0, 0)
    blk3 = lambda b: (b, 0, 0)
    return pl.pallas_call(
        _conv_sample_body,
        grid=(n // nb,),
        in_specs=[
            pl.BlockSpec((nb, l, CONV_CH), blk3),
            pl.BlockSpec((nb, CONV_K - 1, CONV_CH), blk3),
            pl.BlockSpec((CONV_K, CONV_CH), const),
            pl.BlockSpec((1, CONV_CH), const),
            pl.BlockSpec((1, CONV_CH), const),
            pl.BlockSpec((1, CONV_CH), const),
        ],
        out_specs=[
            pl.BlockSpec((nb, l, CONV_CH), blk3),
            pl.BlockSpec((nb, CONV_K - 1, CONV_CH), blk3),
        ],
        out_shape=[
            jax.ShapeDtypeStruct((n, l, CONV_CH), BF16),
            jax.ShapeDtypeStruct((n, CONV_K - 1, CONV_CH), F32),
        ],
        scratch_shapes=[pltpu.VMEM((nb, l + HALO + SUBLANES, CONV_CH), F32)],
        compiler_params=_cparams(("parallel",)),
        name="conv_sample",
    )(u, state, conv_w, conv_b, ln_g, ln_b)


def _decay_tables(c):
    lg = jnp.log(1.0 - 2.0 ** (-5.0 - jnp.arange(RET_HEADS, dtype=F32)))
    idx = jnp.arange(c, dtype=F32)
    rel = idx[:, None] - idx[None, :]
    dmat = jnp.where(rel[None] >= 0, jnp.exp(jnp.maximum(rel, 0.0)[None] * lg[:, None, None]), 0.0)
    xi = jnp.exp((idx + 1.0)[None, :] * lg[:, None])
    zeta = jnp.exp((c - 1.0 - idx)[None, :] * lg[:, None])
    gc = jnp.exp(c * lg)
    xi_b = jnp.broadcast_to(xi[:, :, None], (RET_HEADS, c, HEAD_DIM))
    zeta_b = jnp.broadcast_to(zeta[:, :, None], (RET_HEADS, c, HEAD_DIM))
    gc_b = jnp.broadcast_to(gc[:, None, None], (RET_HEADS, 1, HEAD_DIM))
    return dmat, xi_b, zeta_b, gc_b


def _group_norm(o):
    mu = jnp.mean(o, axis=-1, keepdims=True)
    d = o - mu
    var = jnp.mean(d * d, axis=-1, keepdims=True)
    return d * lax.rsqrt(var + EPS)


def _ret_chunk(qh, kh, vh, r, dmat, xi, zeta, gc):
    qb = qh.astype(BF16)
    kb = kh.astype(BF16)
    vb = vh.astype(BF16)
    s = lax.dot_general(qb, kb, (((1,), (1,)), ((), ())), preferred_element_type=F32) * dmat
    o = jnp.dot(s.astype(BF16), vb, preferred_element_type=F32)
    o = o + jnp.dot(qb, r.astype(BF16), preferred_element_type=F32) * xi
    kz = (kh.astype(F32) * zeta).astype(BF16)
    r_new = r * gc + lax.dot_general(kz, vb, (((0,), (0,)), ((), ())), preferred_element_type=F32)
    return o, r_new


def _ret_prompt_body(q_ref, k_ref, v_ref, gs_ref, d_ref, xi_ref, zeta_ref, gc_ref, o_ref, st_ref, r_ref):
    j = pl.program_id(1)

    @pl.when(j == 0)
    def _():
        r_ref[...] = jnp.zeros_like(r_ref)

    n_chunks = q_ref.shape[0] // CHUNK
    for hh in range(RET_HEADS):
        sl = slice(hh * HEAD_DIM, (hh + 1) * HEAD_DIM)
        r = r_ref[hh]
        for ci in range(n_chunks):
            rows = slice(ci * CHUNK, (ci + 1) * CHUNK)
            o, r = _ret_chunk(q_ref[rows, sl], k_ref[rows, sl], v_ref[rows, sl], r,
                              d_ref[hh], xi_ref[hh], zeta_ref[hh], gc_ref[hh])
            o_ref[rows, sl] = (gs_ref[rows, sl] * _group_norm(o)).astype(o_ref.dtype)
        r_ref[hh] = r
    st_ref[0] = r_ref[...]


def _ret_prompt(q, k, v, gs, n, l, tl):
    dmat, xi, zeta, gc = _decay_tables(CHUNK)
    per = l // tl
    row = lambda b, j: (b * per + j, 0)
    c3 = lambda b, j: (0, 0, 0)
    return pl.pallas_call(
        _ret_prompt_body,
        grid=(n, per),
        in_specs=[pl.BlockSpec((tl, RET_WIDTH), row)] * 4 + [
            pl.BlockSpec((RET_HEADS, CHUNK, CHUNK), c3),
            pl.BlockSpec((RET_HEADS, CHUNK, HEAD_DIM), c3),
            pl.BlockSpec((RET_HEADS, CHUNK, HEAD_DIM), c3),
            pl.BlockSpec((RET_HEADS, 1, HEAD_DIM), c3),
        ],
        out_specs=[
            pl.BlockSpec((tl, RET_WIDTH), row),
            pl.BlockSpec((1, RET_HEADS, HEAD_DIM, HEAD_DIM), lambda b, j: (b, 0, 0, 0)),
        ],
        out_shape=[
            jax.ShapeDtypeStruct((n * l, RET_WIDTH), BF16),
            jax.ShapeDtypeStruct((n, RET_HEADS, HEAD_DIM, HEAD_DIM), F32),
        ],
        scratch_shapes=[pltpu.VMEM((RET_HEADS, HEAD_DIM, HEAD_DIM), F32)],
        compiler_params=_cparams(("arbitrary", "arbitrary")),
        name="ret_prompt",
    )(q, k, v, gs, dmat, xi, zeta, gc)


def _ret_sample_body(q_ref, k_ref, v_ref, gs_ref, st_ref, d_ref, xi_ref, zeta_ref, gc_ref, o_ref, nst_ref):
    nb = st_ref.shape[0]
    l = q_ref.shape[0] // nb
    for b in range(nb):
        rows = slice(b * l, (b + 1) * l)
        for hh in range(RET_HEADS):
            sl = slice(hh * HEAD_DIM, (hh + 1) * HEAD_DIM)
            o, r = _ret_chunk(q_ref[rows, sl], k_ref[rows, sl], v_ref[rows, sl], st_ref[b, hh],
                              d_ref[hh], xi_ref[hh], zeta_ref[hh], gc_ref[hh])
            o_ref[rows, sl] = (gs_ref[rows, sl] * _group_norm(o)).astype(o_ref.dtype)
            nst_ref[b, hh] = r


def _ret_sample(q, k, v, gs, state, l, nb):
    n = state.shape[0]
    dmat, xi, zeta, gc = _decay_tables(l)
    row = lambda b: (b, 0)
    c3 = lambda b: (0, 0, 0)
    blk4 = lambda b: (b, 0, 0, 0)
    return pl.pallas_call(
        _ret_sample_body,
        grid=(n // nb,),
        in_specs=[pl.BlockSpec((nb * l, RET_WIDTH), row)] * 4 + [
            pl.BlockSpec((nb, RET_HEADS, HEAD_DIM, HEAD_DIM), blk4),
            pl.BlockSpec((RET_HEADS, l, l), c3),
            pl.BlockSpec((RET_HEADS, l, HEAD_DIM), c3),
            pl.BlockSpec((RET_HEADS, l, HEAD_DIM), c3),
            pl.BlockSpec((RET_HEADS, 1, HEAD_DIM), c3),
        ],
        out_specs=[
            pl.BlockSpec((nb * l, RET_WIDTH), row),
            pl.BlockSpec((nb, RET_HEADS, HEAD_DIM, HEAD_DIM), blk4),
        ],
        out_shape=[
            jax.ShapeDtypeStruct((n * l, RET_WIDTH), BF16),
            jax.ShapeDtypeStruct((n, RET_HEADS, HEAD_DIM, HEAD_DIM), F32),
        ],
        compiler_params=_cparams(("parallel",)),
        name="ret_sample",
    )(q, k, v, gs, state, dmat, xi, zeta, gc)


def _split3(x):
    hi = x.astype(BF16)
    r1 = x - hi.astype(F32)
    mid = r1.astype(BF16)
    lo = (r1 - mid.astype(F32)).astype(BF16)
    return hi, mid, lo


def _dot_hp(t, w_hi, w_mid):
    t_hi, t_mid, _ = _split3(t)
    d = functools.partial(jnp.dot, preferred_element_type=F32)
    return d(t_hi, w_hi) + (d(t_mid, w_hi) + d(t_hi, w_mid))


def _route(logits):
    lane = lax.broadcasted_iota(jnp.int32, logits.shape, 1)
    big = jnp.int32(LANES)
    is_group = (lane >= N_EXPERTS) & (lane < N_EXPERTS + N_GROUPS)
    lg = jnp.where(is_group, logits, -1e30)
    m = jnp.max(lg, axis=-1, keepdims=True)
    g_top = 1.0 / jnp.sum(jnp.exp(lg - m), axis=-1, keepdims=True)
    g_idx = jnp.min(jnp.where(lg == m, lane, big), axis=-1, keepdims=True) - N_EXPERTS
    in_group = (lane >= g_idx * EXPERTS_PER_GROUP) & (lane < (g_idx + 1) * EXPERTS_PER_GROUP)
    lem = jnp.where(in_group, logits, -1e30)
    m2 = jnp.max(lem, axis=-1, keepdims=True)
    pe = jnp.where(in_group, jnp.exp(lem - m2), 0.0)
    p1 = jnp.max(pe, axis=-1, keepdims=True)
    e1 = jnp.min(jnp.where(in_group & (pe == p1), lane, big), axis=-1, keepdims=True)
    rest = in_group & (lane != e1)
    pe2 = jnp.where(rest, pe, -1.0)
    p2 = jnp.max(pe2, axis=-1, keepdims=True)
    e2 = jnp.min(jnp.where(rest & (pe2 == p2), lane, big), axis=-1, keepdims=True)
    scale = g_top / (p1 + p2)
    rec = jnp.where(lane == 0, e1.astype(F32), jnp.where(lane == 1, e2.astype(F32), 0.0))
    return rec + jnp.where(lane == 2, p1 * scale, jnp.where(lane == 3, p2 * scale, 0.0))


def _outproj_body(c_ref, o_ref, x_ref, wo_ref, g2_ref, wr_ref, br_ref, x1_ref, t_ref, rec_ref):
    x1 = x_ref[...] + jnp.dot(c_ref[...], wo_ref[0:CONV_CH, :], preferred_element_type=F32)
    x1 = x1 + jnp.dot(o_ref[...], wo_ref[CONV_CH:, :], preferred_element_type=F32)
    x1_ref[...] = x1
    t = _rms(x1, g2_ref[...])
    t_ref[...] = t.astype(t_ref.dtype)
    rec_ref[...] = _route(_dot_hp(t, wr_ref[0], wr_ref[1]) + br_ref[...])


def _outproj(c, o, x, w_out, g2, wr2, br, tm):
    t = x.shape[0]
    row = lambda i: (i, 0)
    const = lambda i: (0, 0)
    c3 = lambda i: (0, 0, 0)
    return pl.pallas_call(
        _outproj_body,
        grid=(t // tm,),
        in_specs=[
            pl.BlockSpec((tm, CONV_CH), row),
            pl.BlockSpec((tm, RET_WIDTH), row),
            pl.BlockSpec((tm, D_MODEL), row),
            pl.BlockSpec((D_MODEL, D_MODEL), const),
            pl.BlockSpec((1, D_MODEL), const),
            pl.BlockSpec((2, D_MODEL, LANES), c3),
            pl.BlockSpec((1, LANES), const),
        ],
        out_specs=[
            pl.BlockSpec((tm, D_MODEL), row),
            pl.BlockSpec((tm, D_MODEL), row),
            pl.BlockSpec((tm, LANES), row),
        ],
        out_shape=[
            jax.ShapeDtypeStruct((t, D_MODEL), F32),
            jax.ShapeDtypeStruct((t, D_MODEL), BF16),
            jax.ShapeDtypeStruct((t, LANES), F32),
        ],
        compiler_params=_cparams(("parallel",)),
        name="outproj_router",
    )(c, o, x, w_out, g2, wr2, br)


def _iota_f32(shape, dim):
    return lax.broadcasted_iota(jnp.int32, shape, dim).astype(F32)


def _plan_body(rec_ref, pos_ref, post_ref, gsp_ref, meta_ref):
    rec = rec_ref[...]
    lane = _iota_f32(rec.shape, 1)
    a1 = lane == rec[:, 0:1]
    a2 = lane == rec[:, 1:2]
    a1f = jnp.where(a1, 1.0, 0.0)
    a2f = jnp.where(a2, 1.0, 0.0)
    ltri = jnp.where(_iota_f32((SUB, SUB), 1) < _iota_f32((SUB, SUB), 0), 1.0, 0.0).astype(BF16)
    c1 = jnp.dot(ltri, a1f.astype(BF16), preferred_element_type=F32)
    c2 = jnp.dot(ltri, a2f.astype(BF16), preferred_element_type=F32)
    n1 = jnp.sum(a1f, axis=0, keepdims=True)
    n2 = jnp.sum(a2f, axis=0, keepdims=True)
    cnt = jnp.floor((n1 + n2 + (ROW_ALIGN - 1.0)) * (1.0 / ROW_ALIGN))
    utri = jnp.where(_iota_f32((LANES, LANES), 0) < _iota_f32((LANES, LANES), 1), 1.0, 0.0).astype(BF16)
    start = jnp.dot(jnp.broadcast_to(cnt, (8, LANES)).astype(BF16), utri, preferred_element_type=F32)[0:1]
    base1 = start * ROW_ALIGN
    base2 = base1 + n1
    pos1 = jnp.sum(jnp.where(a1, c1 + base1, 0.0), axis=1, keepdims=True)
    pos2 = jnp.sum(jnp.where(a2, c2 + base2, 0.0), axis=1, keepdims=True)
    posm = jnp.where(lane == 0.0, pos1, jnp.where(lane == 1.0, pos2, 0.0))
    pos_ref[...] = posm
    post_ref[0] = posm.T[0:8, :]
    g1 = _split3(rec[:, 2:3])
    g2 = _split3(rec[:, 3:4])
    gsp = jnp.zeros(rec.shape, F32)
    for i in range(3):
        gsp = jnp.where(lane == float(i), g1[i].astype(F32), gsp)
        gsp = jnp.where(lane == float(3 + i), g2[i].astype(F32), gsp)
    gsp_ref[...] = gsp.astype(BF16)
    row = _iota_f32((8, LANES), 0)
    meta_ref[0] = jnp.where(row == 0.0, start, jnp.where(row == 1.0, cnt, 0.0))


def _plan(rec):
    t = rec.shape[0]
    nsub = t // SUB
    row = lambda i: (i, 0)
    blk3 = lambda i: (i, 0, 0)
    return pl.pallas_call(
        _plan_body,
        grid=(nsub,),
        in_specs=[pl.BlockSpec((SUB, LANES), row)],
        out_specs=[
            pl.BlockSpec((SUB, LANES), row),
            pl.BlockSpec((1, 8, SUB), blk3),
            pl.BlockSpec((SUB, LANES), row),
            pl.BlockSpec((1, 8, LANES), blk3),
        ],
        out_shape=[
            jax.ShapeDtypeStruct((t, LANES), F32),
            jax.ShapeDtypeStruct((nsub, 8, SUB), F32),
            jax.ShapeDtypeStruct((t, LANES), BF16),
            jax.ShapeDtypeStruct((nsub, 8, LANES), F32),
        ],
        compiler_params=_cparams(("parallel",)),
        name="moe_plan",
    )(rec)


def _moe_body(start_ref, cnt_ref, t_ref, pos_ref, post_ref, gsp_ref, wg_ref, wu_ref, wd_ref, y_ref,
              s_ref, gs_ref, lhs_ref, gl_ref, acc_ref):
    w = pl.program_id(0)
    j = pl.program_id(1)
    nsw = t_ref.shape[0] // SUB
    ec = wg_ref.shape[0]
    dot = functools.partial(jnp.dot, preferred_element_type=F32)

    def used_rows(s):
        last = (w * nsw + s) * N_EXPERTS + (N_EXPERTS - 1)
        return (start_ref[last] + cnt_ref[last]) * ROW_ALIGN

    @pl.when(j == 0)
    def _():
        @pl.when(w == 0)
        def _():
            lhs_ref[...] = jnp.zeros_like(lhs_ref)
            gl_ref[...] = jnp.zeros_like(gl_ref)

        for s in range(nsw):
            p1 = post_ref[s, 0:1, :]
            p2 = post_ref[s, 1:2, :]
            tok = t_ref[s * SUB:(s + 1) * SUB, :]
            gsp = gsp_ref[s * SUB:(s + 1) * SUB, :]
            used = used_rows(s)
            for b in range(SUBP // PBLK):
                @pl.when(b * PBLK < used)
                def _(b=b):
                    r = _iota_f32((PBLK, SUB), 0) + float(b * PBLK)
                    m1 = jnp.where(r == p1, 1.0, 0.0)
                    m2 = jnp.where(r == p2, 1.0, 0.0)
                    rows = pl.ds(s * SUBP + b * PBLK, PBLK)
                    s_ref[rows, :] = dot((m1 + m2).astype(BF16), tok).astype(BF16)
                    g = dot(m1.astype(BF16), gsp)[:, 0:3] + dot(m2.astype(BF16), gsp)[:, 3:6]
                    g = g[:, 0:1] + g[:, 1:2] + g[:, 2:3]
                    gs_ref[rows, :] = jnp.broadcast_to(g, (PBLK, LANES))

    for el in range(ec):
        e = j * ec + el

        def run_copy(to_lhs, e=e):
            off = jnp.int32(0)
            for s in range(nsw):
                idx = (w * nsw + s) * N_EXPERTS + e
                c = cnt_ref[idx]
                src0 = s * SUBP + start_ref[idx] * ROW_ALIGN

                def cp(i, carry, src0=src0, off=off):
                    a = pl.ds(pl.multiple_of(src0 + i * ROW_ALIGN, ROW_ALIGN), ROW_ALIGN)
                    b = pl.ds(pl.multiple_of(off + i * ROW_ALIGN, ROW_ALIGN), ROW_ALIGN)
                    if to_lhs:
                        lhs_ref[b, :] = s_ref[a, :]
                        gl_ref[b, :] = gs_ref[a, :]
                    else:
                        s_ref[a, :] = lhs_ref[b, :]
                    return carry

                lax.fori_loop(0, c, cp, 0)
                off = off + c * ROW_ALIGN
            return off

        total = run_copy(True)

        def mlp(i, carry, el=el):
            rows = pl.ds(pl.multiple_of(i * MBLK, MBLK), MBLK)
            x = lhs_ref[rows, :]
            g = gl_ref[rows, :]
            h1 = dot(x, wg_ref[el])
            h2 = dot(x, wu_ref[el])
            hid = (h1 * jax.nn.sigmoid(h1)) * h2
            hid = jnp.concatenate([hid[:, i0:i0 + LANES] * g for i0 in range(0, EXPERT_FF, LANES)], axis=1)
            lhs_ref[rows, :] = dot(hid.astype(BF16), wd_ref[el]).astype(BF16)
            return carry

        lax.fori_loop(0, lax.div(total + (MBLK - 1), MBLK), mlp, 0)
        run_copy(False)

    @pl.when(j == pl.num_programs(1) - 1)
    def _():
        for s in range(nsw):
            p1 = pos_ref[s * SUB:(s + 1) * SUB, 0:1]
            p2 = pos_ref[s * SUB:(s + 1) * SUB, 1:2]
            used = used_rows(s)
            acc_ref[...] = jnp.zeros_like(acc_ref)
            for b in range(SUBP // PBLK):
                @pl.when(b * PBLK < used)
                def _(b=b):
                    r = _iota_f32((SUB, PBLK), 1) + float(b * PBLK)
                    m = jnp.where(r == p1, 1.0, jnp.where(r == p2, 1.0, 0.0)).astype(BF16)
                    acc_ref[...] += dot(m, s_ref[pl.ds(s * SUBP + b * PBLK, PBLK), :])
            y_ref[s * SUB:(s + 1) * SUB, :] = acc_ref[...].astype(y_ref.dtype)


def _moe(t, pos, post, gsp, start, cnt, wg, wu, wd, win, ec):
    n = t.shape[0]
    nsw = win // SUB
    row = lambda i, j, *_: (i, 0)
    wblk = lambda i, j, *_: (j, 0, 0)
    cap = -(-(win + nsw * ROW_ALIGN) // MBLK) * MBLK
    grid_spec = pltpu.PrefetchScalarGridSpec(
        num_scalar_prefetch=2,
        grid=(n // win, N_EXPERTS // ec),
        in_specs=[
            pl.BlockSpec((win, D_MODEL), row),
            pl.BlockSpec((win, LANES), row),
            pl.BlockSpec((nsw, 8, SUB), lambda i, j, *_: (i, 0, 0)),
            pl.BlockSpec((win, LANES), row),
            pl.BlockSpec((ec, D_MODEL, EXPERT_FF), wblk),
            pl.BlockSpec((ec, D_MODEL, EXPERT_FF), wblk),
            pl.BlockSpec((ec, EXPERT_FF, D_MODEL), wblk),
        ],
        out_specs=pl.BlockSpec((win, D_MODEL), row),
        scratch_shapes=[
            pltpu.VMEM((nsw * SUBP, D_MODEL), BF16),
            pltpu.VMEM((nsw * SUBP, LANES), F32),
            pltpu.VMEM((cap, D_MODEL), BF16),
            pltpu.VMEM((cap, LANES), F32),
            pltpu.VMEM((SUB, D_MODEL), F32),
        ],
    )
    return pl.pallas_call(
        _moe_body,
        grid_spec=grid_spec,
        out_shape=jax.ShapeDtypeStruct((n, D_MODEL), BF16),
        compiler_params=pltpu.CompilerParams(dimension_semantics=("arbitrary", "arbitrary"),
                                             vmem_limit_bytes=MOE_VMEM_LIMIT),
        name="moe",
    )(start, cnt, t, pos, post, gsp, wg, wu, wd)


def _ple_body(x1_ref, ym_ref, p_ref, wp_ref, gp_ref, wpg_ref, gf_ref, y_ref):
    x2 = x1_ref[...] + ym_ref[...].astype(F32)
    ple = _rms(jnp.dot(p_ref[...].astype(BF16), wp_ref[...], preferred_element_type=F32), gp_ref[...])
    gate = jax.nn.sigmoid(jnp.dot(x2.astype(BF16), wpg_ref[...], preferred_element_type=F32))
    y_ref[...] = _rms(x2 + ple * gate, gf_ref[...])


def _ple(x1, ym, p, w_ple, gp, w_ple_gate, gf, tm):
    t = x1.shape[0]
    row = lambda i: (i, 0)
    const = lambda i: (0, 0)
    return pl.pallas_call(
        _ple_body,
        grid=(t // tm,),
        in_specs=[
            pl.BlockSpec((tm, D_MODEL), row),
            pl.BlockSpec((tm, D_MODEL), row),
            pl.BlockSpec((tm, PLE_DIM), row),
            pl.BlockSpec((PLE_DIM, D_MODEL), const),
            pl.BlockSpec((1, D_MODEL), const),
            pl.BlockSpec((D_MODEL, D_MODEL), const),
            pl.BlockSpec((1, D_MODEL), const),
        ],
        out_specs=pl.BlockSpec((tm, D_MODEL), row),
        out_shape=jax.ShapeDtypeStruct((t, D_MODEL), F32),
        compiler_params=_cparams(("parallel",)),
        name="ple_final",
    )(x1, ym, p, w_ple, gp, w_ple_gate, gf)


def _rope_tables(pos):
    half = HEAD_DIM // 2
    inv = ROPE_BASE ** (-jnp.arange(half, dtype=F32) / half)
    ang = pos[:, None] * inv[None, :]
    cos = jnp.cos(ang)
    sin = jnp.sin(ang)
    return jnp.concatenate([cos, cos], axis=-1), jnp.concatenate([-sin, sin], axis=-1)


def _router_params(we, be, wg, bg):
    pad = LANES - N_EXPERTS - N_GROUPS
    w = jnp.pad(jnp.concatenate([we, wg], axis=1), ((0, 0), (0, pad)))
    b = jnp.pad(jnp.concatenate([be, bg]), (0, pad))[None, :]
    return jnp.stack(_split3(w)[:2]), b


def kernel(x_prompt, x_sample, p_prompt, p_sample, state_conv, state_ret, w_in, conv_w, conv_b, conv_ln_g, conv_ln_b, w_out, norm1_g, norm2_g, router_group_w, router_group_b, router_expert_w, router_expert_b, w_expert_gate, w_expert_up, w_expert_down, w_ple, ple_norm_g, w_ple_gate, final_norm_g):
    assert w_in.shape[0] == 1, "single-layer trunk"
    nb, seq, _ = x_prompt.shape
    ns, dseq, _ = x_sample.shape
    tm = 512

    w_in_b = w_in[0].astype(BF16)
    w_out_b = w_out[0].astype(BF16)
    wg_b = w_expert_gate[0].astype(BF16)
    wu_b = w_expert_up[0].astype(BF16)
    wd_b = w_expert_down[0].astype(BF16)
    w_ple_b = w_ple[0].astype(BF16)
    w_pg_b = w_ple_gate[0].astype(BF16)
    g1 = norm1_g[0][None, :]
    g2 = norm2_g[0][None, :]
    gp = ple_norm_g[0][None, :]
    gf = final_norm_g[None, :]
    cb = conv_b[0][None, :]
    lng = conv_ln_g[0][None, :]
    lnb = conv_ln_b[0][None, :]
    wr2, br = _router_params(router_expert_w[0], router_expert_b[0], router_group_w[0], router_group_b[0])

    cos_p, sin_p = _rope_tables(jnp.arange(seq, dtype=F32) + jnp.float32(0))
    pos_s = jnp.tile(jnp.arange(dseq, dtype=F32) + jnp.float32(PAST_LEN), tm // dseq)
    cos_s, sin_s = _rope_tables(pos_s)

    def tail(c, o, x, p):
        x1, t, rec = _outproj(c, o, x, w_out_b, g2, wr2, br, tm)
        pos, post, gsp, meta = _plan(rec)
        start = meta[:, 0, :N_EXPERTS].astype(jnp.int32).reshape(-1)
        cnt = meta[:, 1, :N_EXPERTS].astype(jnp.int32).reshape(-1)
        ym = _moe(t, pos, post, gsp, start, cnt, wg_b, wu_b, wd_b, min(MOE_WINDOW, x.shape[0]), 2)
        return _ple(x1, ym, p, w_ple_b, gp, w_pg_b, gf, tm)

    xp = x_prompt.reshape(nb * seq, D_MODEL)
    u, q, k, v, gs = _inproj(xp, g1, w_in_b, cos_p, sin_p, tm, seq // tm, BF16)
    c, conv_p = _conv_prompt(u.reshape(nb, seq, CONV_CH), conv_w[0], cb, lng, lnb, tm)
    o, ret_p = _ret_prompt(q, k, v, gs, nb, seq, tm)
    y_p = tail(c.reshape(nb * seq, CONV_CH), o, xp, p_prompt[0].reshape(nb * seq, PLE_DIM))

    xs = x_sample.reshape(ns * dseq, D_MODEL)
    u, q, k, v, gs = _inproj(xs, g1, w_in_b, cos_s, sin_s, tm, 1, F32)
    c, conv_s = _conv_sample(u.reshape(ns, dseq, CONV_CH), state_conv[0], conv_w[0], cb, lng, lnb, 16)
    o, ret_s = _ret_sample(q, k, v, gs, state_ret[0], dseq, 8)
    y_s = tail(c.reshape(ns * dseq, CONV_CH), o, xs, p_sample[0].reshape(ns * dseq, PLE_DIM))

    return (y_p.reshape(nb, seq, D_MODEL), y_s.reshape(ns, dseq, D_MODEL),
            conv_p[None], ret_p[None], conv_s[None], ret_s[None])
```

```python
import functools

import jax
import jax.numpy as jnp
from jax import lax
from jax.experimental import pallas as pl
from jax.experimental.pallas import tpu as pltpu

F32 = jnp.float32
BF16 = jnp.bfloat16

D_MODEL = 1024
PLE_DIM = 256
CONV_CH = 512
CONV_K = 31
RET_WIDTH = 512
RET_HEADS = 4
HEAD_DIM = 128
CHUNK = 128
ROPE_BASE = 10000.0
N_GROUPS = 4
EXPERTS_PER_GROUP = 8
N_EXPERTS = 32
EXPERT_FF = 256
IN_COLS = 3072
EPS = 1e-6
PAST_LEN = 16384

LANES = 128
SUBLANES = 8
HALO = 32
HALO_OFF = HALO - (CONV_K - 1)
VMEM_LIMIT = 48 * 1024 * 1024

SUB = 256
ROW_ALIGN = 16
PBLK = 256
SUBP = -(-(2 * SUB + N_EXPERTS * (ROW_ALIGN - 1)) // PBLK) * PBLK
CHUNKS_PER_SUB = SUBP // ROW_ALIGN
MAX_CHUNKS_PER_SUB = (2 * SUB + N_EXPERTS * (ROW_ALIGN - 1)) // ROW_ALIGN
ROW_W = D_MODEL + LANES
MBLK = 512
CHUNKS_PER_BLK = MBLK // ROW_ALIGN


def _cparams(sem):
    return pltpu.CompilerParams(dimension_semantics=sem, vmem_limit_bytes=VMEM_LIMIT)


def _rms(x, g):
    return x * lax.rsqrt(jnp.mean(x * x, axis=-1, keepdims=True) + EPS) * g


def _inproj_body(x_ref, g1_ref, w_ref, cos_ref, sin_ref, u_ref, q_ref, k_ref, v_ref, gs_ref):
    h = _rms(x_ref[...], g1_ref[...]).astype(BF16)
    z = jnp.dot(h, w_ref[...], preferred_element_type=F32)
    a = z[:, :CONV_CH]
    b = z[:, CONV_CH:2 * CONV_CH]
    u_ref[...] = a * jax.nn.sigmoid(b)
    cos = cos_ref[...]
    sin = sin_ref[...]
    q0 = 2 * CONV_CH
    k0 = q0 + RET_WIDTH
    for hh in range(RET_HEADS):
        sl = slice(hh * HEAD_DIM, (hh + 1) * HEAD_DIM)
        qh = z[:, q0 + hh * HEAD_DIM:q0 + (hh + 1) * HEAD_DIM]
        kh = z[:, k0 + hh * HEAD_DIM:k0 + (hh + 1) * HEAD_DIM]
        q_ref[:, sl] = (qh * cos + pltpu.roll(qh, HEAD_DIM // 2, 1) * sin).astype(q_ref.dtype)
        kr = (kh * cos + pltpu.roll(kh, HEAD_DIM // 2, 1) * sin) * (HEAD_DIM ** -0.5)
        k_ref[:, sl] = kr.astype(k_ref.dtype)
    v_ref[...] = z[:, k0 + RET_WIDTH:k0 + 2 * RET_WIDTH].astype(v_ref.dtype)
    g = z[:, k0 + 2 * RET_WIDTH:]
    gs_ref[...] = g * jax.nn.sigmoid(g)


def _inproj(x, g1, w_in, cos, sin, tm, table_blocks, qkv_dtype):
    t = x.shape[0]
    row = lambda i: (i, 0)
    const = lambda i: (0, 0)
    tab = (lambda i: (i % table_blocks, 0)) if table_blocks > 1 else const
    return pl.pallas_call(
        _inproj_body,
        grid=(t // tm,),
        in_specs=[
            pl.BlockSpec((tm, D_MODEL), row),
            pl.BlockSpec((1, D_MODEL), const),
            pl.BlockSpec((D_MODEL, IN_COLS), const),
            pl.BlockSpec((tm, HEAD_DIM), tab),
            pl.BlockSpec((tm, HEAD_DIM), tab),
        ],
        out_specs=[pl.BlockSpec((tm, CONV_CH), row)] + [pl.BlockSpec((tm, RET_WIDTH), row)] * 4,
        out_shape=[
            jax.ShapeDtypeStruct((t, CONV_CH), F32),
            jax.ShapeDtypeStruct((t, RET_WIDTH), qkv_dtype),
            jax.ShapeDtypeStruct((t, RET_WIDTH), qkv_dtype),
            jax.ShapeDtypeStruct((t, RET_WIDTH), qkv_dtype),
            jax.ShapeDtypeStruct((t, RET_WIDTH), F32),
        ],
        compiler_params=_cparams(("parallel",)),
        name="inproj",
    )(x, g1, w_in, cos, sin)


def _ln_silu(acc, g, b):
    mu = jnp.mean(acc, axis=-1, keepdims=True)
    d = acc - mu
    var = jnp.mean(d * d, axis=-1, keepdims=True)
    y = d * lax.rsqrt(var + EPS) * g + b
    return y * jax.nn.sigmoid(y)


def _dwconv(load, w_ref, rows, time_axis):
    acc = None
    for b in range(SUBLANES):
        part = None
        for a in range((CONV_K + HALO_OFF) // SUBLANES + 1):
            k = SUBLANES * a + b - HALO_OFF
            if 0 <= k < CONV_K:
                term = load(SUBLANES * a, rows + SUBLANES) * w_ref[k:k + 1, :]
                part = term if part is None else part + term
        if part is not None:
            shifted = lax.slice_in_dim(part, b, b + rows, axis=time_axis)
            acc = shifted if acc is None else acc + shifted
    return acc


def _conv_prompt_body(u_ref, w_ref, cb_ref, lg_ref, lb_ref, c_ref, st_ref, ext_ref):
    j = pl.program_id(1)
    tl = u_ref.shape[1]

    @pl.when(j == 0)
    def _():
        ext_ref[0:HALO, :] = jnp.zeros((HALO, CONV_CH), F32)
        ext_ref[tl + HALO:, :] = jnp.zeros((SUBLANES, CONV_CH), F32)

    @pl.when(j > 0)
    def _():
        ext_ref[0:HALO, :] = ext_ref[tl:tl + HALO, :]

    ext_ref[HALO:tl + HALO, :] = u_ref[0]
    acc = _dwconv(lambda s, n: ext_ref[s:s + n, :], w_ref, tl, 0) + cb_ref[...]
    c_ref[0] = _ln_silu(acc, lg_ref[...], lb_ref[...]).astype(c_ref.dtype)
    st_ref[0] = ext_ref[tl + HALO_OFF:tl + HALO, :]


def _conv_prompt(u, conv_w, conv_b, ln_g, ln_b, tl):
    n, l, _ = u.shape
    const = lambda b, j: (0, 0)
    return pl.pallas_call(
        _conv_prompt_body,
        grid=(n, l // tl),
        in_specs=[
            pl.BlockSpec((1, tl, CONV_CH), lambda b, j: (b, j, 0)),
            pl.BlockSpec((CONV_K, CONV_CH), const),
            pl.BlockSpec((1, CONV_CH), const),
            pl.BlockSpec((1, CONV_CH), const),
            pl.BlockSpec((1, CONV_CH), const),
        ],
        out_specs=[
            pl.BlockSpec((1, tl, CONV_CH), lambda b, j: (b, j, 0)),
            pl.BlockSpec((1, CONV_K - 1, CONV_CH), lambda b, j: (b, 0, 0)),
        ],
        out_shape=[
            jax.ShapeDtypeStruct((n, l, CONV_CH), BF16),
            jax.ShapeDtypeStruct((n, CONV_K - 1, CONV_CH), F32),
        ],
        scratch_shapes=[pltpu.VMEM((tl + HALO + SUBLANES, CONV_CH), F32)],
        compiler_params=_cparams(("arbitrary", "arbitrary")),
        name="conv_prompt",
    )(u, conv_w, conv_b, ln_g, ln_b)


def _conv_sample_body(u_ref, st_ref, w_ref, cb_ref, lg_ref, lb_ref, c_ref, nst_ref, ext_ref):
    nb, l, _ = u_ref.shape
    ext_ref[:, 0:HALO_OFF, :] = jnp.zeros((nb, HALO_OFF, CONV_CH), F32)
    ext_ref[:, HALO_OFF:HALO, :] = st_ref[...]
    ext_ref[:, HALO:l + HALO, :] = u_ref[...]
    ext_ref[:, l + HALO:, :] = jnp.zeros((nb, SUBLANES, CONV_CH), F32)
    acc = _dwconv(lambda s, n: ext_ref[:, s:s + n, :], w_ref, l, 1) + cb_ref[...]
    c_ref[...] = _ln_silu(acc, lg_ref[...], lb_ref[...]).astype(c_ref.dtype)
    nst_ref[...] = ext_ref[:, l + HALO_OFF:l + HALO, :]


def _conv_sample(u, state, conv_w, conv_b, ln_g, ln_b, nb):
    n, l, _ = u.shape
    const = lambda b: (0, 0)
    blk3 = lambda b: (b, 0, 0)
    return pl.pallas_call(
        _conv_sample_body,
        grid=(n // nb,),
        in_specs=[
            pl.BlockSpec((nb, l, CONV_CH), blk3),
            pl.BlockSpec((nb, CONV_K - 1, CONV_CH), blk3),
            pl.BlockSpec((CONV_K, CONV_CH), const),
            pl.BlockSpec((1, CONV_CH), const),
            pl.BlockSpec((1, CONV_CH), const),
            pl.BlockSpec((1, CONV_CH), const),
        ],
        out_specs=[
            pl.BlockSpec((nb, l, CONV_CH), blk3),
            pl.BlockSpec((nb, CONV_K - 1, CONV_CH), blk3),
        ],
        out_shape=[
            jax.ShapeDtypeStruct((n, l, CONV_CH), BF16),
            jax.ShapeDtypeStruct((n, CONV_K - 1, CONV_CH), F32),
        ],
        scratch_shapes=[pltpu.VMEM((nb, l + HALO + SUBLANES, CONV_CH), F32)],
        compiler_params=_cparams(("parallel",)),
        name="conv_sample",
    )(u, state, conv_w, conv_b, ln_g, ln_b)


def _decay_tables(c):
    lg = jnp.log(1.0 - 2.0 ** (-5.0 - jnp.arange(RET_HEADS, dtype=F32)))
    idx = jnp.arange(c, dtype=F32)
    rel = idx[:, None] - idx[None, :]
    dmat = jnp.where(rel[None] >= 0, jnp.exp(jnp.maximum(rel, 0.0)[None] * lg[:, None, None]), 0.0)
    xi = jnp.exp((idx + 1.0)[None, :] * lg[:, None])
    zeta = jnp.exp((c - 1.0 - idx)[None, :] * lg[:, None])
    gc = jnp.exp(c * lg)
    xi_b = jnp.broadcast_to(xi[:, :, None], (RET_HEADS, c, HEAD_DIM))
    zeta_b = jnp.broadcast_to(zeta[:, :, None], (RET_HEADS, c, HEAD_DIM))
    gc_b = jnp.broadcast_to(gc[:, None, None], (RET_HEADS, 1, HEAD_DIM))
    return dmat, xi_b, zeta_b, gc_b


def _group_norm(o):
    mu = jnp.mean(o, axis=-1, keepdims=True)
    d = o - mu
    var = jnp.mean(d * d, axis=-1, keepdims=True)
    return d * lax.rsqrt(var + EPS)


def _ret_chunk(qh, kh, vh, r, dmat, xi, zeta, gc):
    qb = qh.astype(BF16)
    kb = kh.astype(BF16)
    vb = vh.astype(BF16)
    s = lax.dot_general(qb, kb, (((1,), (1,)), ((), ())), preferred_element_type=F32) * dmat
    o = jnp.dot(s.astype(BF16), vb, preferred_element_type=F32)
    o = o + jnp.dot(qb, r.astype(BF16), preferred_element_type=F32) * xi
    kz = (kh.astype(F32) * zeta).astype(BF16)
    r_new = r * gc + lax.dot_general(kz, vb, (((0,), (0,)), ((), ())), preferred_element_type=F32)
    return o, r_new


def _ret_prompt_body(q_ref, k_ref, v_ref, gs_ref, d_ref, xi_ref, zeta_ref, gc_ref, o_ref, st_ref, r_ref):
    j = pl.program_id(1)

    @pl.when(j == 0)
    def _():
        r_ref[...] = jnp.zeros_like(r_ref)

    n_chunks = q_ref.shape[0] // CHUNK
    for hh in range(RET_HEADS):
        sl = slice(hh * HEAD_DIM, (hh + 1) * HEAD_DIM)
        r = r_ref[hh]
        for ci in range(n_chunks):
            rows = slice(ci * CHUNK, (ci + 1) * CHUNK)
            o, r = _ret_chunk(q_ref[rows, sl], k_ref[rows, sl], v_ref[rows, sl], r,
                              d_ref[hh], xi_ref[hh], zeta_ref[hh], gc_ref[hh])
            o_ref[rows, sl] = (gs_ref[rows, sl] * _group_norm(o)).astype(o_ref.dtype)
        r_ref[hh] = r
    st_ref[0] = r_ref[...]


def _ret_prompt(q, k, v, gs, n, l, tl):
    dmat, xi, zeta, gc = _decay_tables(CHUNK)
    per = l // tl
    row = lambda b, j: (b * per + j, 0)
    c3 = lambda b, j: (0, 0, 0)
    return pl.pallas_call(
        _ret_prompt_body,
        grid=(n, per),
        in_specs=[pl.BlockSpec((tl, RET_WIDTH), row)] * 4 + [
            pl.BlockSpec((RET_HEADS, CHUNK, CHUNK), c3),
            pl.BlockSpec((RET_HEADS, CHUNK, HEAD_DIM), c3),
            pl.BlockSpec((RET_HEADS, CHUNK, HEAD_DIM), c3),
            pl.BlockSpec((RET_HEADS, 1, HEAD_DIM), c3),
        ],
        out_specs=[
            pl.BlockSpec((tl, RET_WIDTH), row),
            pl.BlockSpec((1, RET_HEADS, HEAD_DIM, HEAD_DIM), lambda b, j: (b, 0, 0, 0)),
        ],
        out_shape=[
            jax.ShapeDtypeStruct((n * l, RET_WIDTH), BF16),
            jax.ShapeDtypeStruct((n, RET_HEADS, HEAD_DIM, HEAD_DIM), F32),
        ],
        scratch_shapes=[pltpu.VMEM((RET_HEADS, HEAD_DIM, HEAD_DIM), F32)],
        compiler_params=_cparams(("arbitrary", "arbitrary")),
        name="ret_prompt",
    )(q, k, v, gs, dmat, xi, zeta, gc)


def _ret_sample_body(q_ref, k_ref, v_ref, gs_ref, st_ref, d_ref, xi_ref, zeta_ref, gc_ref, o_ref, nst_ref):
    nb = st_ref.shape[0]
    l = q_ref.shape[0] // nb
    for b in range(nb):
        rows = slice(b * l, (b + 1) * l)
        for hh in range(RET_HEADS):
            sl = slice(hh * HEAD_DIM, (hh + 1) * HEAD_DIM)
            o, r = _ret_chunk(q_ref[rows, sl], k_ref[rows, sl], v_ref[rows, sl], st_ref[b, hh],
                              d_ref[hh], xi_ref[hh], zeta_ref[hh], gc_ref[hh])
            o_ref[rows, sl] = (gs_ref[rows, sl] * _group_norm(o)).astype(o_ref.dtype)
            nst_ref[b, hh] = r


def _ret_sample(q, k, v, gs, state, l, nb):
    n = state.shape[0]
    dmat, xi, zeta, gc = _decay_tables(l)
    row = lambda b: (b, 0)
    c3 = lambda b: (0, 0, 0)
    blk4 = lambda b: (b, 0, 0, 0)
    return pl.pallas_call(
        _ret_sample_body,
        grid=(n // nb,),
        in_specs=[pl.BlockSpec((nb * l, RET_WIDTH), row)] * 4 + [
            pl.BlockSpec((nb, RET_HEADS, HEAD_DIM, HEAD_DIM), blk4),
            pl.BlockSpec((RET_HEADS, l, l), c3),
            pl.BlockSpec((RET_HEADS, l, HEAD_DIM), c3),
            pl.BlockSpec((RET_HEADS, l, HEAD_DIM), c3),
            pl.BlockSpec((RET_HEADS, 1, HEAD_DIM), c3),
        ],
        out_specs=[
            pl.BlockSpec((nb * l, RET_WIDTH), row),
            pl.BlockSpec((nb, RET_HEADS, HEAD_DIM, HEAD_DIM), blk4),
        ],
        out_shape=[
            jax.ShapeDtypeStruct((n * l, RET_WIDTH), BF16),
            jax.ShapeDtypeStruct((n, RET_HEADS, HEAD_DIM, HEAD_DIM), F32),
        ],
        compiler_params=_cparams(("parallel",)),
        name="ret_sample",
    )(q, k, v, gs, state, dmat, xi, zeta, gc)


def _split3(x):
    hi = x.astype(BF16)
    r1 = x - hi.astype(F32)
    mid = r1.astype(BF16)
    lo = (r1 - mid.astype(F32)).astype(BF16)
    return hi, mid, lo


def _dot_hp(t, w_hi, w_mid):
    t_hi, t_mid, _ = _split3(t)
    d = functools.partial(jnp.dot, preferred_element_type=F32)
    return d(t_hi, w_hi) + (d(t_mid, w_hi) + d(t_hi, w_mid))


def _route(logits):
    lane = lax.broadcasted_iota(jnp.int32, logits.shape, 1)
    big = jnp.int32(LANES)
    is_group = (lane >= N_EXPERTS) & (lane < N_EXPERTS + N_GROUPS)
    lg = jnp.where(is_group, logits, -1e30)
    m = jnp.max(lg, axis=-1, keepdims=True)
    g_top = 1.0 / jnp.sum(jnp.exp(lg - m), axis=-1, keepdims=True)
    g_idx = jnp.min(jnp.where(lg == m, lane, big), axis=-1, keepdims=True) - N_EXPERTS
    in_group = (lane >= g_idx * EXPERTS_PER_GROUP) & (lane < (g_idx + 1) * EXPERTS_PER_GROUP)
    lem = jnp.where(in_group, logits, -1e30)
    m2 = jnp.max(lem, axis=-1, keepdims=True)
    pe = jnp.where(in_group, jnp.exp(lem - m2), 0.0)
    p1 = jnp.max(pe, axis=-1, keepdims=True)
    e1 = jnp.min(jnp.where(in_group & (pe == p1), lane, big), axis=-1, keepdims=True)
    rest = in_group & (lane != e1)
    pe2 = jnp.where(rest, pe, -1.0)
    p2 = jnp.max(pe2, axis=-1, keepdims=True)
    e2 = jnp.min(jnp.where(rest & (pe2 == p2), lane, big), axis=-1, keepdims=True)
    scale = g_top / (p1 + p2)
    rec = jnp.where(lane == 0, e1.astype(F32), jnp.where(lane == 1, e2.astype(F32), 0.0))
    return rec + jnp.where(lane == 2, p1 * scale, jnp.where(lane == 3, p2 * scale, 0.0))


def _outproj_body(c_ref, o_ref, x_ref, wo_ref, g2_ref, wr_ref, br_ref, x1_ref, t_ref, rec_ref):
    x1 = x_ref[...] + jnp.dot(c_ref[...], wo_ref[0:CONV_CH, :], preferred_element_type=F32)
    x1 = x1 + jnp.dot(o_ref[...], wo_ref[CONV_CH:, :], preferred_element_type=F32)
    x1_ref[...] = x1
    t = _rms(x1, g2_ref[...])
    t_ref[...] = t.astype(t_ref.dtype)
    rec_ref[...] = _route(_dot_hp(t, wr_ref[0], wr_ref[1]) + br_ref[...])


def _outproj(c, o, x, w_out, g2, wr2, br, tm):
    t = x.shape[0]
    row = lambda i: (i, 0)
    const = lambda i: (0, 0)
    c3 = lambda i: (0, 0, 0)
    return pl.pallas_call(
        _outproj_body,
        grid=(t // tm,),
        in_specs=[
            pl.BlockSpec((tm, CONV_CH), row),
            pl.BlockSpec((tm, RET_WIDTH), row),
            pl.BlockSpec((tm, D_MODEL), row),
            pl.BlockSpec((D_MODEL, D_MODEL), const),
            pl.BlockSpec((1, D_MODEL), const),
            pl.BlockSpec((2, D_MODEL, LANES), c3),
            pl.BlockSpec((1, LANES), const),
        ],
        out_specs=[
            pl.BlockSpec((tm, D_MODEL), row),
            pl.BlockSpec((tm, D_MODEL), row),
            pl.BlockSpec((tm, LANES), row),
        ],
        out_shape=[
            jax.ShapeDtypeStruct((t, D_MODEL), F32),
            jax.ShapeDtypeStruct((t, D_MODEL), BF16),
            jax.ShapeDtypeStruct((t, LANES), F32),
        ],
        compiler_params=_cparams(("parallel",)),
        name="outproj_router",
    )(c, o, x, w_out, g2, wr2, br)


def _iota_f32(shape, dim):
    return lax.broadcasted_iota(jnp.int32, shape, dim).astype(F32)


def _dispatch_body(rec_a_ref, t_a_ref, rec_b_ref, t_b_ref, s_ref, pos_ref, meta_ref, *, nsub_a):
    from_a = pl.program_id(0) < nsub_a
    rec = jnp.where(from_a, rec_a_ref[...], rec_b_ref[...])
    tok = jnp.where(from_a, t_a_ref[...], t_b_ref[...])
    lane = _iota_f32(rec.shape, 1)
    a1 = lane == rec[:, 0:1]
    a2 = lane == rec[:, 1:2]
    a1f = jnp.where(a1, 1.0, 0.0)
    a2f = jnp.where(a2, 1.0, 0.0)
    ltri = jnp.where(_iota_f32((SUB, SUB), 1) < _iota_f32((SUB, SUB), 0), 1.0, 0.0).astype(BF16)
    c1 = jnp.dot(ltri, a1f.astype(BF16), preferred_element_type=F32)
    c2 = jnp.dot(ltri, a2f.astype(BF16), preferred_element_type=F32)
    n1 = jnp.sum(a1f, axis=0, keepdims=True)
    n2 = jnp.sum(a2f, axis=0, keepdims=True)
    cnt = jnp.floor((n1 + n2 + (ROW_ALIGN - 1.0)) * (1.0 / ROW_ALIGN))
    utri = jnp.where(_iota_f32((LANES, LANES), 0) < _iota_f32((LANES, LANES), 1), 1.0, 0.0).astype(BF16)
    start = jnp.dot(jnp.broadcast_to(cnt, (SUBLANES, LANES)).astype(BF16), utri,
                    preferred_element_type=F32)[0:1]
    base1 = start * ROW_ALIGN
    base2 = base1 + n1
    pos1 = jnp.sum(jnp.where(a1, c1 + base1, 0.0), axis=1, keepdims=True)
    pos2 = jnp.sum(jnp.where(a2, c2 + base2, 0.0), axis=1, keepdims=True)
    posm = jnp.where(lane == 0.0, pos1, jnp.where(lane == 1.0, pos2, 0.0))
    pos_ref[...] = posm
    row = _iota_f32((SUBLANES, LANES), 0)
    meta_ref[0] = jnp.where(row == 0.0, start, jnp.where(row == 1.0, cnt, 0.0))

    g1 = _split3(rec[:, 2:3])
    g2 = _split3(rec[:, 3:4])
    info = jnp.where(lane == 6.0, rec[:, 0:1], jnp.where(lane == 7.0, rec[:, 1:2], 0.0))
    for i in range(3):
        info = jnp.where(lane == float(i), g1[i].astype(F32), info)
        info = jnp.where(lane == float(3 + i), g2[i].astype(F32), info)
    src = jnp.concatenate([tok, info.astype(BF16)], axis=1)

    post = posm.T
    p1 = post[0:1, :]
    p2 = post[1:2, :]
    used = jnp.sum(cnt) * ROW_ALIGN
    for b in range(SUBP // PBLK):
        rows = slice(b * PBLK, (b + 1) * PBLK)

        @pl.when(b * PBLK < used)
        def _(b=b, rows=rows):
            r = _iota_f32((PBLK, SUB), 0) + float(b * PBLK)
            onehot = jnp.where(r == p1, 1.0, jnp.where(r == p2, 1.0, 0.0)).astype(BF16)
            s_ref[rows, :] = jnp.dot(onehot, src, preferred_element_type=F32).astype(BF16)

        @pl.when(b * PBLK >= used)
        def _(rows=rows):
            s_ref[rows, :] = jnp.zeros((PBLK, ROW_W), BF16)


def _dispatch(rec_a, t_a, rec_b, t_b):
    nsub_a = rec_a.shape[0] // SUB
    nsub_b = rec_b.shape[0] // SUB
    nsub = nsub_a + nsub_b
    row = lambda i: (i, 0)
    row_a = lambda i: (jnp.minimum(i, nsub_a - 1), 0)
    row_b = lambda i: (jnp.maximum(i - nsub_a, 0), 0)
    return pl.pallas_call(
        functools.partial(_dispatch_body, nsub_a=nsub_a),
        grid=(nsub,),
        in_specs=[
            pl.BlockSpec((SUB, LANES), row_a),
            pl.BlockSpec((SUB, D_MODEL), row_a),
            pl.BlockSpec((SUB, LANES), row_b),
            pl.BlockSpec((SUB, D_MODEL), row_b),
        ],
        out_specs=[
            pl.BlockSpec((SUBP, ROW_W), row),
            pl.BlockSpec((SUB, LANES), row),
            pl.BlockSpec((1, SUBLANES, LANES), lambda i: (i, 0, 0)),
        ],
        out_shape=[
            jax.ShapeDtypeStruct((nsub * SUBP, ROW_W), BF16),
            jax.ShapeDtypeStruct((nsub * SUB, LANES), F32),
            jax.ShapeDtypeStruct((nsub, SUBLANES, LANES), F32),
        ],
        compiler_params=_cparams(("parallel",)),
        name="moe_dispatch",
    )(rec_a, t_a, rec_b, t_b)


def _chunk_schedule(start, cnt):
    nsub = start.shape[0]
    cnt_es = cnt.T
    n_e = jnp.sum(cnt_es, axis=1)
    padded = (n_e + CHUNKS_PER_BLK - 1) // CHUNKS_PER_BLK * CHUNKS_PER_BLK
    cstart = jnp.concatenate([jnp.zeros((1,), jnp.int32), jnp.cumsum(padded).astype(jnp.int32)])
    run_first = cstart[:-1, None] + jnp.cumsum(cnt_es, axis=1) - cnt_es
    base = (jnp.arange(nsub, dtype=jnp.int32) * CHUNKS_PER_SUB)[None, :] + start.T
    n_list = nsub * MAX_CHUNKS_PER_SUB + N_EXPERTS * (CHUNKS_PER_BLK - 1)
    k = jnp.arange(n_list, dtype=jnp.int32)
    run = jnp.searchsorted(run_first.reshape(-1), k, side="right").astype(jnp.int32) - 1
    within = k - run_first.reshape(-1)[run]
    table = jnp.where(within < cnt_es.reshape(-1)[run], base.reshape(-1)[run] + within, -1)
    used = (start[:, -1] + cnt[:, -1]) * ROW_ALIGN
    return table.astype(jnp.int32), cstart, used.astype(jnp.int32)


def _experts_body(tab_ref, cstart_ref, s_in, wg_ref, wu_ref, wd_ref, s_hbm,
                  xbuf, ybuf, wgu_ref, wdb_ref, gsem, ssem, inflight_ref):
    del s_in
    e = pl.program_id(0)
    ne = pl.num_programs(0)
    c0 = cstart_ref[e]
    nblk = (cstart_ref[e + 1] - c0) // CHUNKS_PER_BLK

    def chunk_rows(i):
        return pl.ds(pl.multiple_of(i * ROW_ALIGN, ROW_ALIGN), ROW_ALIGN)

    def gather_copy(src, i, slot):
        return pltpu.make_async_copy(s_hbm.at[chunk_rows(src), :], xbuf.at[slot, chunk_rows(i), :], gsem.at[slot])

    def scatter_copy(dst, i, slot):
        return pltpu.make_async_copy(ybuf.at[slot, chunk_rows(i), :],
                                     s_hbm.at[chunk_rows(dst), pl.ds(0, D_MODEL)], ssem.at[slot])

    def start_all(copy, first, slot, counter):
        def body(i, n):
            chunk = tab_ref[first + i]

            @pl.when(chunk >= 0)
            def _():
                copy(chunk, i, slot).start()
            return n + jnp.where(chunk >= 0, 1, 0)
        inflight_ref[counter] = lax.fori_loop(0, CHUNKS_PER_BLK, body, jnp.int32(0))

    def wait_all(copy, slot, counter):
        def body(i, carry):
            copy(0, 0, slot).wait()
            return carry
        lax.fori_loop(0, inflight_ref[counter], body, 0)
        inflight_ref[counter] = 0

    def gather_start(first, slot):
        start_all(gather_copy, first, slot, slot)

    def gather_wait(first, slot):
        del first
        wait_all(gather_copy, slot, slot)

    def scatter_start(first, slot):
        start_all(scatter_copy, first, slot, 2 + slot)

    def scatter_wait(slot):
        wait_all(scatter_copy, slot, 2 + slot)

    @pl.when(e == 0)
    def _():
        for i in range(4):
            inflight_ref[i] = 0
        xbuf[...] = jnp.zeros_like(xbuf)

        @pl.when(nblk > 0)
        def _():
            gather_start(c0, 0)

    wgu_ref[:, 0:EXPERT_FF] = wg_ref[0].astype(BF16)
    wgu_ref[:, EXPERT_FF:] = wu_ref[0].astype(BF16)
    wdb_ref[...] = wd_ref[0].astype(BF16)
    ef = e.astype(F32)

    def block(b, carry):
        slot = b & 1
        first = c0 + b * CHUNKS_PER_BLK

        @pl.when(b + 1 < nblk)
        def _():
            gather_start(first + CHUNKS_PER_BLK, 1 - slot)

        gather_wait(first, slot)
        scatter_wait(slot)
        x = xbuf[slot]
        info = x[:, D_MODEL:].astype(F32)
        g_first = info[:, 0:1] + info[:, 1:2] + info[:, 2:3]
        g_second = info[:, 3:4] + info[:, 4:5] + info[:, 5:6]
        gate = jnp.where(info[:, 6:7] == ef, g_first, g_second)
        h = jnp.dot(x[:, :D_MODEL], wgu_ref[...], preferred_element_type=F32)
        h1 = h[:, :EXPERT_FF]
        hid = (h1 * jax.nn.sigmoid(h1)) * h[:, EXPERT_FF:] * gate
        ybuf[slot] = jnp.dot(hid.astype(BF16), wdb_ref[...], preferred_element_type=F32).astype(BF16)
        scatter_start(first, slot)
        return carry

    lax.fori_loop(0, nblk, block, 0)

    @pl.when(e + 1 < ne)
    def _():
        c1 = cstart_ref[e + 1]

        @pl.when(cstart_ref[e + 2] > c1)
        def _():
            gather_start(c1, 0)

    @pl.when(e == ne - 1)
    def _():
        scatter_wait(0)
        scatter_wait(1)


def _experts(table, cstart, staged, wg, wu, wd):
    wblk = lambda e, *_: (e, 0, 0)
    grid_spec = pltpu.PrefetchScalarGridSpec(
        num_scalar_prefetch=2,
        grid=(N_EXPERTS,),
        in_specs=[
            pl.BlockSpec(memory_space=pl.ANY),
            pl.BlockSpec((1, D_MODEL, EXPERT_FF), wblk),
            pl.BlockSpec((1, D_MODEL, EXPERT_FF), wblk),
            pl.BlockSpec((1, EXPERT_FF, D_MODEL), wblk),
        ],
        out_specs=pl.BlockSpec(memory_space=pl.ANY),
        scratch_shapes=[
            pltpu.VMEM((2, MBLK, ROW_W), BF16),
            pltpu.VMEM((2, MBLK, D_MODEL), BF16),
            pltpu.VMEM((D_MODEL, 2 * EXPERT_FF), BF16),
            pltpu.VMEM((EXPERT_FF, D_MODEL), BF16),
            pltpu.SemaphoreType.DMA((2,)),
            pltpu.SemaphoreType.DMA((2,)),
            pltpu.SMEM((4,), jnp.int32),
        ],
    )
    return pl.pallas_call(
        _experts_body,
        grid_spec=grid_spec,
        out_shape=jax.ShapeDtypeStruct(staged.shape, staged.dtype),
        input_output_aliases={2: 0},
        compiler_params=_cparams(("arbitrary",)),
        name="moe_experts",
    )(table, cstart, staged, wg, wu, wd)


def _combine_body(used_ref, ys_ref, pos_ref, x1_ref, p_ref, wp_ref, gp_ref, wpg_ref, gf_ref, y_ref, acc_ref,
                  *, sub_off):
    used = used_ref[pl.program_id(0) + sub_off]
    p1 = pos_ref[:, 0:1]
    p2 = pos_ref[:, 1:2]
    acc_ref[...] = x1_ref[...]
    for b in range(SUBP // PBLK):
        @pl.when(b * PBLK < used)
        def _(b=b):
            r = _iota_f32((SUB, PBLK), 1) + float(b * PBLK)
            onehot = jnp.where(r == p1, 1.0, jnp.where(r == p2, 1.0, 0.0)).astype(BF16)
            acc_ref[...] += jnp.dot(onehot, ys_ref[b * PBLK:(b + 1) * PBLK, :], preferred_element_type=F32)
    x2 = acc_ref[...]
    ple = _rms(jnp.dot(p_ref[...].astype(BF16), wp_ref[...], preferred_element_type=F32), gp_ref[...])
    gate = jax.nn.sigmoid(jnp.dot(x2.astype(BF16), wpg_ref[...], preferred_element_type=F32))
    y_ref[...] = _rms(x2 + ple * gate, gf_ref[...])


def _combine(used, ys, pos, x1, p, w_ple, gp, w_ple_gate, gf, sub_off):
    t = x1.shape[0]
    row = lambda i, *_: (i, 0)
    const = lambda i, *_: (0, 0)
    grid_spec = pltpu.PrefetchScalarGridSpec(
        num_scalar_prefetch=1,
        grid=(t // SUB,),
        in_specs=[
            pl.BlockSpec((SUBP, D_MODEL), lambda i, *_: (i + sub_off, 0)),
            pl.BlockSpec((SUB, LANES), lambda i, *_: (i + sub_off, 0)),
            pl.BlockSpec((SUB, D_MODEL), row),
            pl.BlockSpec((SUB, PLE_DIM), row),
            pl.BlockSpec((PLE_DIM, D_MODEL), const),
            pl.BlockSpec((1, D_MODEL), const),
            pl.BlockSpec((D_MODEL, D_MODEL), const),
            pl.BlockSpec((1, D_MODEL), const),
        ],
        out_specs=pl.BlockSpec((SUB, D_MODEL), row),
        scratch_shapes=[pltpu.VMEM((SUB, D_MODEL), F32)],
    )
    return pl.pallas_call(
        functools.partial(_combine_body, sub_off=sub_off),
        grid_spec=grid_spec,
        out_shape=jax.ShapeDtypeStruct((t, D_MODEL), F32),
        compiler_params=_cparams(("arbitrary",)),
        name="moe_combine_ple",
    )(used, ys, pos, x1, p, w_ple, gp, w_ple_gate, gf)


def _rope_tables(pos):
    half = HEAD_DIM // 2
    inv = ROPE_BASE ** (-jnp.arange(half, dtype=F32) / half)
    ang = pos[:, None] * inv[None, :]
    cos = jnp.cos(ang)
    sin = jnp.sin(ang)
    return jnp.concatenate([cos, cos], axis=-1), jnp.concatenate([-sin, sin], axis=-1)


def _router_params(we, be, wg, bg):
    pad = LANES - N_EXPERTS - N_GROUPS
    w = jnp.pad(jnp.concatenate([we, wg], axis=1), ((0, 0), (0, pad)))
    b = jnp.pad(jnp.concatenate([be, bg]), (0, pad))[None, :]
    return jnp.stack(_split3(w)[:2]), b


def kernel(x_prompt, x_sample, p_prompt, p_sample, state_conv, state_ret, w_in, conv_w, conv_b, conv_ln_g, conv_ln_b, w_out, norm1_g, norm2_g, router_group_w, router_group_b, router_expert_w, router_expert_b, w_expert_gate, w_expert_up, w_expert_down, w_ple, ple_norm_g, w_ple_gate, final_norm_g):
    assert w_in.shape[0] == 1, "single-layer trunk"
    nb, seq, _ = x_prompt.shape
    ns, dseq, _ = x_sample.shape
    tm = 512

    w_in_b = w_in[0].astype(BF16)
    w_out_b = w_out[0].astype(BF16)
    w_ple_b = w_ple[0].astype(BF16)
    w_pg_b = w_ple_gate[0].astype(BF16)
    g1 = norm1_g[0][None, :]
    g2 = norm2_g[0][None, :]
    gp = ple_norm_g[0][None, :]
    gf = final_norm_g[None, :]
    cb = conv_b[0][None, :]
    lng = conv_ln_g[0][None, :]
    lnb = conv_ln_b[0][None, :]
    wr2, br = _router_params(router_expert_w[0], router_expert_b[0], router_group_w[0], router_group_b[0])

    cos_p, sin_p = _rope_tables(jnp.arange(seq, dtype=F32) + jnp.float32(0))
    pos_s = jnp.tile(jnp.arange(dseq, dtype=F32) + jnp.float32(PAST_LEN), tm // dseq)
    cos_s, sin_s = _rope_tables(pos_s)

    xp = x_prompt.reshape(nb * seq, D_MODEL)
    u, q, k, v, gs = _inproj(xp, g1, w_in_b, cos_p, sin_p, tm, seq // tm, BF16)
    c, conv_p = _conv_prompt(u.reshape(nb, seq, CONV_CH), conv_w[0], cb, lng, lnb, tm)
    o, ret_p = _ret_prompt(q, k, v, gs, nb, seq, tm)
    x1_p, t_p, rec_p = _outproj(c.reshape(nb * seq, CONV_CH), o, xp, w_out_b, g2, wr2, br, tm)

    xs = x_sample.reshape(ns * dseq, D_MODEL)
    u, q, k, v, gs = _inproj(xs, g1, w_in_b, cos_s, sin_s, tm, 1, F32)
    c, conv_s = _conv_sample(u.reshape(ns, dseq, CONV_CH), state_conv[0], conv_w[0], cb, lng, lnb, 16)
    o, ret_s = _ret_sample(q, k, v, gs, state_ret[0], dseq, 8)
    x1_s, t_s, rec_s = _outproj(c.reshape(ns * dseq, CONV_CH), o, xs, w_out_b, g2, wr2, br, tm)

    staged, pos, meta = _dispatch(rec_p, t_p, rec_s, t_s)
    start = meta[:, 0, :N_EXPERTS].astype(jnp.int32)
    cnt = meta[:, 1, :N_EXPERTS].astype(jnp.int32)
    table, cstart, used = _chunk_schedule(start, cnt)
    ys = _experts(table, cstart, staged, w_expert_gate[0], w_expert_up[0], w_expert_down[0])

    y_p = _combine(used, ys, pos, x1_p, p_prompt[0].reshape(nb * seq, PLE_DIM), w_ple_b, gp, w_pg_b, gf, 0)
    y_s = _combine(used, ys, pos, x1_s, p_sample[0].reshape(ns * dseq, PLE_DIM), w_ple_b, gp, w_pg_b, gf,
                   nb * seq // SUB)

    return (y_p.reshape(nb, seq, D_MODEL), y_s.reshape(ns, dseq, D_MODEL),
            conv_p[None], ret_p[None], conv_s[None], ret_s[None])
```

```python
import functools

import jax
import jax.numpy as jnp
from jax import lax
from jax.experimental import pallas as pl
from jax.experimental.pallas import tpu as pltpu

F32 = jnp.float32
BF16 = jnp.bfloat16

D_MODEL = 1024
PLE_DIM = 256
CONV_CH = 512
CONV_K = 31
RET_WIDTH = 512
RET_HEADS = 4
HEAD_DIM = 128
CHUNK = 128
ROPE_BASE = 10000.0
N_GROUPS = 4
EXPERTS_PER_GROUP = 8
N_EXPERTS = 32
EXPERT_FF = 256
IN_COLS = 3072
EPS = 1e-6
PAST_LEN = 16384

LANES = 128
SUBLANES = 8
HALO = 32
HALO_OFF = HALO - (CONV_K - 1)
VMEM_LIMIT = 48 * 1024 * 1024

SUB = 256
ROW_ALIGN = 16
PBLK = 256
SUBP = -(-(2 * SUB + N_EXPERTS * (ROW_ALIGN - 1)) // PBLK) * PBLK
CHUNKS_PER_SUB = SUBP // ROW_ALIGN
ROW_W = D_MODEL + LANES
MBLK = 512
CHUNKS_PER_BLK = MBLK // ROW_ALIGN


def _cparams(sem):
    return pltpu.CompilerParams(dimension_semantics=sem, vmem_limit_bytes=VMEM_LIMIT)


def _rms(x, g):
    return x * lax.rsqrt(jnp.mean(x * x, axis=-1, keepdims=True) + EPS) * g


def _inproj_body(x_ref, g1_ref, w_ref, cos_ref, sin_ref, u_ref, q_ref, k_ref, v_ref, gs_ref):
    h = _rms(x_ref[...], g1_ref[...]).astype(BF16)
    z = jnp.dot(h, w_ref[...], preferred_element_type=F32)
    a = z[:, :CONV_CH]
    b = z[:, CONV_CH:2 * CONV_CH]
    u_ref[...] = a * jax.nn.sigmoid(b)
    cos = cos_ref[...]
    sin = sin_ref[...]
    q0 = 2 * CONV_CH
    k0 = q0 + RET_WIDTH
    for hh in range(RET_HEADS):
        sl = slice(hh * HEAD_DIM, (hh + 1) * HEAD_DIM)
        qh = z[:, q0 + hh * HEAD_DIM:q0 + (hh + 1) * HEAD_DIM]
        kh = z[:, k0 + hh * HEAD_DIM:k0 + (hh + 1) * HEAD_DIM]
        q_ref[:, sl] = (qh * cos + pltpu.roll(qh, HEAD_DIM // 2, 1) * sin).astype(q_ref.dtype)
        kr = (kh * cos + pltpu.roll(kh, HEAD_DIM // 2, 1) * sin) * (HEAD_DIM ** -0.5)
        k_ref[:, sl] = kr.astype(k_ref.dtype)
    v_ref[...] = z[:, k0 + RET_WIDTH:k0 + 2 * RET_WIDTH].astype(v_ref.dtype)
    g = z[:, k0 + 2 * RET_WIDTH:]
    gs_ref[...] = g * jax.nn.sigmoid(g)


def _inproj(x, g1, w_in, cos, sin, tm, table_blocks, qkv_dtype):
    t = x.shape[0]
    row = lambda i: (i, 0)
    const = lambda i: (0, 0)
    tab = (lambda i: (i % table_blocks, 0)) if table_blocks > 1 else const
    return pl.pallas_call(
        _inproj_body,
        grid=(t // tm,),
        in_specs=[
            pl.BlockSpec((tm, D_MODEL), row),
            pl.BlockSpec((1, D_MODEL), const),
            pl.BlockSpec((D_MODEL, IN_COLS), const),
            pl.BlockSpec((tm, HEAD_DIM), tab),
            pl.BlockSpec((tm, HEAD_DIM), tab),
        ],
        out_specs=[pl.BlockSpec((tm, CONV_CH), row)] + [pl.BlockSpec((tm, RET_WIDTH), row)] * 4,
        out_shape=[
            jax.ShapeDtypeStruct((t, CONV_CH), F32),
            jax.ShapeDtypeStruct((t, RET_WIDTH), qkv_dtype),
            jax.ShapeDtypeStruct((t, RET_WIDTH), qkv_dtype),
            jax.ShapeDtypeStruct((t, RET_WIDTH), qkv_dtype),
            jax.ShapeDtypeStruct((t, RET_WIDTH), F32),
        ],
        compiler_params=_cparams(("parallel",)),
        name="inproj",
    )(x, g1, w_in, cos, sin)


def _ln_silu(acc, g, b):
    mu = jnp.mean(acc, axis=-1, keepdims=True)
    d = acc - mu
    var = jnp.mean(d * d, axis=-1, keepdims=True)
    y = d * lax.rsqrt(var + EPS) * g + b
    return y * jax.nn.sigmoid(y)


def _dwconv(load, w_ref, rows, time_axis):
    acc = None
    for b in range(SUBLANES):
        part = None
        for a in range((CONV_K + HALO_OFF) // SUBLANES + 1):
            k = SUBLANES * a + b - HALO_OFF
            if 0 <= k < CONV_K:
                term = load(SUBLANES * a, rows + SUBLANES) * w_ref[k:k + 1, :]
                part = term if part is None else part + term
        if part is not None:
            shifted = lax.slice_in_dim(part, b, b + rows, axis=time_axis)
            acc = shifted if acc is None else acc + shifted
    return acc


def _conv_prompt_body(u_ref, w_ref, cb_ref, lg_ref, lb_ref, c_ref, st_ref, ext_ref):
    j = pl.program_id(1)
    tl = u_ref.shape[1]

    @pl.when(j == 0)
    def _():
        ext_ref[0:HALO, :] = jnp.zeros((HALO, CONV_CH), F32)
        ext_ref[tl + HALO:, :] = jnp.zeros((SUBLANES, CONV_CH), F32)

    @pl.when(j > 0)
    def _():
        ext_ref[0:HALO, :] = ext_ref[tl:tl + HALO, :]

    ext_ref[HALO:tl + HALO, :] = u_ref[0]
    acc = _dwconv(lambda s, n: ext_ref[s:s + n, :], w_ref, tl, 0) + cb_ref[...]
    c_ref[0] = _ln_silu(acc, lg_ref[...], lb_ref[...]).astype(c_ref.dtype)
    st_ref[0] = ext_ref[tl + HALO_OFF:tl + HALO, :]


def _conv_prompt(u, conv_w, conv_b, ln_g, ln_b, tl):
    n, l, _ = u.shape
    const = lambda b, j: (0, 0)
    return pl.pallas_call(
        _conv_prompt_body,
        grid=(n, l // tl),
        in_specs=[
            pl.BlockSpec((1, tl, CONV_CH), lambda b, j: (b, j, 0)),
            pl.BlockSpec((CONV_K, CONV_CH), const),
            pl.BlockSpec((1, CONV_CH), const),
            pl.BlockSpec((1, CONV_CH), const),
            pl.BlockSpec((1, CONV_CH), const),
        ],
        out_specs=[
            pl.BlockSpec((1, tl, CONV_CH), lambda b, j: (b, j, 0)),
            pl.BlockSpec((1, CONV_K - 1, CONV_CH), lambda b, j: (b, 0, 0)),
        ],
        out_shape=[
            jax.ShapeDtypeStruct((n, l, CONV_CH), BF16),
            jax.ShapeDtypeStruct((n, CONV_K - 1, CONV_CH), F32),
        ],
        scratch_shapes=[pltpu.VMEM((tl + HALO + SUBLANES, CONV_CH), F32)],
        compiler_params=_cparams(("arbitrary", "arbitrary")),
        name="conv_prompt",
    )(u, conv_w, conv_b, ln_g, ln_b)


def _conv_sample_body(u_ref, st_ref, w_ref, cb_ref, lg_ref, lb_ref, c_ref, nst_ref, ext_ref):
    nb, l, _ = u_ref.shape
    ext_ref[:, 0:HALO_OFF, :] = jnp.zeros((nb, HALO_OFF, CONV_CH), F32)
    ext_ref[:, HALO_OFF:HALO, :] = st_ref[...]
    ext_ref[:, HALO:l + HALO, :] = u_ref[...]
    ext_ref[:, l + HALO:, :] = jnp.zeros((nb, SUBLANES, CONV_CH), F32)
    acc = _dwconv(lambda s, n: ext_ref[:, s:s + n, :], w_ref, l, 1) + cb_ref[...]
    c_ref[...] = _ln_silu(acc, lg_ref[...], lb_ref[...]).astype(c_ref.dtype)
    nst_ref[...] = ext_ref[:, l + HALO_OFF:l + HALO, :]


def _conv_sample(u, state, conv_w, conv_b, ln_g, ln_b, nb):
    n, l, _ = u.shape
    const = lambda b: (0, 0)
    blk3 = lambda b: (b, 0, 0)
    return pl.pallas_call(
        _conv_sample_body,
        grid=(n // nb,),
        in_specs=[
            pl.BlockSpec((nb, l, CONV_CH), blk3),
            pl.BlockSpec((nb, CONV_K - 1, CONV_CH), blk3),
            pl.BlockSpec((CONV_K, CONV_CH), const),
            pl.BlockSpec((1, CONV_CH), const),
            pl.BlockSpec((1, CONV_CH), const),
            pl.BlockSpec((1, CONV_CH), const),
        ],
        out_specs=[
            pl.BlockSpec((nb, l, CONV_CH), blk3),
            pl.BlockSpec((nb, CONV_K - 1, CONV_CH), blk3),
        ],
        out_shape=[
            jax.ShapeDtypeStruct((n, l, CONV_CH), BF16),
            jax.ShapeDtypeStruct((n, CONV_K - 1, CONV_CH), F32),
        ],
        scratch_shapes=[pltpu.VMEM((nb, l + HALO + SUBLANES, CONV_CH), F32)],
        compiler_params=_cparams(("parallel",)),
        name="conv_sample",
    )(u, state, conv_w, conv_b, ln_g, ln_b)


def _decay_tables(c):
    lg = jnp.log(1.0 - 2.0 ** (-5.0 - jnp.arange(RET_HEADS, dtype=F32)))
    idx = jnp.arange(c, dtype=F32)
    rel = idx[:, None] - idx[None, :]
    dmat = jnp.where(rel[None] >= 0, jnp.exp(jnp.maximum(rel, 0.0)[None] * lg[:, None, None]), 0.0)
    xi = jnp.exp((idx + 1.0)[None, :] * lg[:, None])
    zeta = jnp.exp((c - 1.0 - idx)[None, :] * lg[:, None])
    gc = jnp.exp(c * lg)
    xi_b = jnp.broadcast_to(xi[:, :, None], (RET_HEADS, c, HEAD_DIM))
    zeta_b = jnp.broadcast_to(zeta[:, :, None], (RET_HEADS, c, HEAD_DIM))
    gc_b = jnp.broadcast_to(gc[:, None, None], (RET_HEADS, 1, HEAD_DIM))
    return dmat, xi_b, zeta_b, gc_b


def _group_norm(o):
    mu = jnp.mean(o, axis=-1, keepdims=True)
    d = o - mu
    var = jnp.mean(d * d, axis=-1, keepdims=True)
    return d * lax.rsqrt(var + EPS)


def _ret_chunk(qh, kh, vh, r, dmat, xi, zeta, gc):
    qb = qh.astype(BF16)
    kb = kh.astype(BF16)
    vb = vh.astype(BF16)
    s = lax.dot_general(qb, kb, (((1,), (1,)), ((), ())), preferred_element_type=F32) * dmat
    o = jnp.dot(s.astype(BF16), vb, preferred_element_type=F32)
    o = o + jnp.dot(qb, r.astype(BF16), preferred_element_type=F32) * xi
    kz = (kh.astype(F32) * zeta).astype(BF16)
    r_new = r * gc + lax.dot_general(kz, vb, (((0,), (0,)), ((), ())), preferred_element_type=F32)
    return o, r_new


def _ret_prompt_body(q_ref, k_ref, v_ref, gs_ref, d_ref, xi_ref, zeta_ref, gc_ref, o_ref, st_ref, r_ref):
    j = pl.program_id(1)

    @pl.when(j == 0)
    def _():
        r_ref[...] = jnp.zeros_like(r_ref)

    n_chunks = q_ref.shape[0] // CHUNK
    for hh in range(RET_HEADS):
        sl = slice(hh * HEAD_DIM, (hh + 1) * HEAD_DIM)
        r = r_ref[hh]
        for ci in range(n_chunks):
            rows = slice(ci * CHUNK, (ci + 1) * CHUNK)
            o, r = _ret_chunk(q_ref[rows, sl], k_ref[rows, sl], v_ref[rows, sl], r,
                              d_ref[hh], xi_ref[hh], zeta_ref[hh], gc_ref[hh])
            o_ref[rows, sl] = (gs_ref[rows, sl] * _group_norm(o)).astype(o_ref.dtype)
        r_ref[hh] = r
    st_ref[0] = r_ref[...]


def _ret_prompt(q, k, v, gs, n, l, tl):
    dmat, xi, zeta, gc = _decay_tables(CHUNK)
    per = l // tl
    row = lambda b, j: (b * per + j, 0)
    c3 = lambda b, j: (0, 0, 0)
    return pl.pallas_call(
        _ret_prompt_body,
        grid=(n, per),
        in_specs=[pl.BlockSpec((tl, RET_WIDTH), row)] * 4 + [
            pl.BlockSpec((RET_HEADS, CHUNK, CHUNK), c3),
            pl.BlockSpec((RET_HEADS, CHUNK, HEAD_DIM), c3),
            pl.BlockSpec((RET_HEADS, CHUNK, HEAD_DIM), c3),
            pl.BlockSpec((RET_HEADS, 1, HEAD_DIM), c3),
        ],
        out_specs=[
            pl.BlockSpec((tl, RET_WIDTH), row),
            pl.BlockSpec((1, RET_HEADS, HEAD_DIM, HEAD_DIM), lambda b, j: (b, 0, 0, 0)),
        ],
        out_shape=[
            jax.ShapeDtypeStruct((n * l, RET_WIDTH), BF16),
            jax.ShapeDtypeStruct((n, RET_HEADS, HEAD_DIM, HEAD_DIM), F32),
        ],
        scratch_shapes=[pltpu.VMEM((RET_HEADS, HEAD_DIM, HEAD_DIM), F32)],
        compiler_params=_cparams(("arbitrary", "arbitrary")),
        name="ret_prompt",
    )(q, k, v, gs, dmat, xi, zeta, gc)


def _ret_sample_body(q_ref, k_ref, v_ref, gs_ref, st_ref, d_ref, xi_ref, zeta_ref, gc_ref, o_ref, nst_ref):
    nb = st_ref.shape[0]
    l = q_ref.shape[0] // nb
    for b in range(nb):
        rows = slice(b * l, (b + 1) * l)
        for hh in range(RET_HEADS):
            sl = slice(hh * HEAD_DIM, (hh + 1) * HEAD_DIM)
            o, r = _ret_chunk(q_ref[rows, sl], k_ref[rows, sl], v_ref[rows, sl], st_ref[b, hh],
                              d_ref[hh], xi_ref[hh], zeta_ref[hh], gc_ref[hh])
            o_ref[rows, sl] = (gs_ref[rows, sl] * _group_norm(o)).astype(o_ref.dtype)
            nst_ref[b, hh] = r


def _ret_sample(q, k, v, gs, state, l, nb):
    n = state.shape[0]
    dmat, xi, zeta, gc = _decay_tables(l)
    row = lambda b: (b, 0)
    c3 = lambda b: (0, 0, 0)
    blk4 = lambda b: (b, 0, 0, 0)
    return pl.pallas_call(
        _ret_sample_body,
        grid=(n // nb,),
        in_specs=[pl.BlockSpec((nb * l, RET_WIDTH), row)] * 4 + [
            pl.BlockSpec((nb, RET_HEADS, HEAD_DIM, HEAD_DIM), blk4),
            pl.BlockSpec((RET_HEADS, l, l), c3),
            pl.BlockSpec((RET_HEADS, l, HEAD_DIM), c3),
            pl.BlockSpec((RET_HEADS, l, HEAD_DIM), c3),
            pl.BlockSpec((RET_HEADS, 1, HEAD_DIM), c3),
        ],
        out_specs=[
            pl.BlockSpec((nb * l, RET_WIDTH), row),
            pl.BlockSpec((nb, RET_HEADS, HEAD_DIM, HEAD_DIM), blk4),
        ],
        out_shape=[
            jax.ShapeDtypeStruct((n * l, RET_WIDTH), BF16),
            jax.ShapeDtypeStruct((n, RET_HEADS, HEAD_DIM, HEAD_DIM), F32),
        ],
        compiler_params=_cparams(("parallel",)),
        name="ret_sample",
    )(q, k, v, gs, state, dmat, xi, zeta, gc)


def _split3(x):
    hi = x.astype(BF16)
    r1 = x - hi.astype(F32)
    mid = r1.astype(BF16)
    lo = (r1 - mid.astype(F32)).astype(BF16)
    return hi, mid, lo


def _dot_hp(t, w_hi, w_mid):
    t_hi, t_mid, _ = _split3(t)
    d = functools.partial(jnp.dot, preferred_element_type=F32)
    return d(t_hi, w_hi) + (d(t_mid, w_hi) + d(t_hi, w_mid))


def _route(logits):
    lane = lax.broadcasted_iota(jnp.int32, logits.shape, 1)
    big = jnp.int32(LANES)
    is_group = (lane >= N_EXPERTS) & (lane < N_EXPERTS + N_GROUPS)
    lg = jnp.where(is_group, logits, -1e30)
    m = jnp.max(lg, axis=-1, keepdims=True)
    g_top = 1.0 / jnp.sum(jnp.exp(lg - m), axis=-1, keepdims=True)
    g_idx = jnp.min(jnp.where(lg == m, lane, big), axis=-1, keepdims=True) - N_EXPERTS
    in_group = (lane >= g_idx * EXPERTS_PER_GROUP) & (lane < (g_idx + 1) * EXPERTS_PER_GROUP)
    lem = jnp.where(in_group, logits, -1e30)
    m2 = jnp.max(lem, axis=-1, keepdims=True)
    pe = jnp.where(in_group, jnp.exp(lem - m2), 0.0)
    p1 = jnp.max(pe, axis=-1, keepdims=True)
    e1 = jnp.min(jnp.where(in_group & (pe == p1), lane, big), axis=-1, keepdims=True)
    rest = in_group & (lane != e1)
    pe2 = jnp.where(rest, pe, -1.0)
    p2 = jnp.max(pe2, axis=-1, keepdims=True)
    e2 = jnp.min(jnp.where(rest & (pe2 == p2), lane, big), axis=-1, keepdims=True)
    scale = g_top / (p1 + p2)
    rec = jnp.where(lane == 0, e1.astype(F32), jnp.where(lane == 1, e2.astype(F32), 0.0))
    return rec + jnp.where(lane == 2, p1 * scale, jnp.where(lane == 3, p2 * scale, 0.0))


def _outproj_body(c_ref, o_ref, x_ref, wo_ref, g2_ref, wr_ref, br_ref, x1_ref, t_ref, rec_ref):
    x1 = x_ref[...] + jnp.dot(c_ref[...], wo_ref[0:CONV_CH, :], preferred_element_type=F32)
    x1 = x1 + jnp.dot(o_ref[...], wo_ref[CONV_CH:, :], preferred_element_type=F32)
    x1_ref[...] = x1
    t = _rms(x1, g2_ref[...])
    t_ref[...] = t.astype(t_ref.dtype)
    rec_ref[...] = _route(_dot_hp(t, wr_ref[0], wr_ref[1]) + br_ref[...])


def _outproj(c, o, x, w_out, g2, wr2, br, tm):
    t = x.shape[0]
    row = lambda i: (i, 0)
    const = lambda i: (0, 0)
    c3 = lambda i: (0, 0, 0)
    return pl.pallas_call(
        _outproj_body,
        grid=(t // tm,),
        in_specs=[
            pl.BlockSpec((tm, CONV_CH), row),
            pl.BlockSpec((tm, RET_WIDTH), row),
            pl.BlockSpec((tm, D_MODEL), row),
            pl.BlockSpec((D_MODEL, D_MODEL), const),
            pl.BlockSpec((1, D_MODEL), const),
            pl.BlockSpec((2, D_MODEL, LANES), c3),
            pl.BlockSpec((1, LANES), const),
        ],
        out_specs=[
            pl.BlockSpec((tm, D_MODEL), row),
            pl.BlockSpec((tm, D_MODEL), row),
            pl.BlockSpec((tm, LANES), row),
        ],
        out_shape=[
            jax.ShapeDtypeStruct((t, D_MODEL), F32),
            jax.ShapeDtypeStruct((t, D_MODEL), BF16),
            jax.ShapeDtypeStruct((t, LANES), F32),
        ],
        compiler_params=_cparams(("parallel",)),
        name="outproj_router",
    )(c, o, x, w_out, g2, wr2, br)


def _iota_f32(shape, dim):
    return lax.broadcasted_iota(jnp.int32, shape, dim).astype(F32)


def _dispatch_body(rec_a_ref, t_a_ref, rec_b_ref, t_b_ref, s_ref, pos_ref, meta_ref, *, nsub_a):
    from_a = pl.program_id(0) < nsub_a
    rec = jnp.where(from_a, rec_a_ref[...], rec_b_ref[...])
    tok = jnp.where(from_a, t_a_ref[...], t_b_ref[...])
    lane = _iota_f32(rec.shape, 1)
    a1 = lane == rec[:, 0:1]
    a2 = lane == rec[:, 1:2]
    a1f = jnp.where(a1, 1.0, 0.0)
    a2f = jnp.where(a2, 1.0, 0.0)
    ltri = jnp.where(_iota_f32((SUB, SUB), 1) < _iota_f32((SUB, SUB), 0), 1.0, 0.0).astype(BF16)
    c1 = jnp.dot(ltri, a1f.astype(BF16), preferred_element_type=F32)
    c2 = jnp.dot(ltri, a2f.astype(BF16), preferred_element_type=F32)
    n1 = jnp.sum(a1f, axis=0, keepdims=True)
    n2 = jnp.sum(a2f, axis=0, keepdims=True)
    cnt = jnp.floor((n1 + n2 + (ROW_ALIGN - 1.0)) * (1.0 / ROW_ALIGN))
    utri = jnp.where(_iota_f32((LANES, LANES), 0) < _iota_f32((LANES, LANES), 1), 1.0, 0.0).astype(BF16)
    start = jnp.dot(jnp.broadcast_to(cnt, (SUBLANES, LANES)).astype(BF16), utri,
                    preferred_element_type=F32)[0:1]
    base1 = start * ROW_ALIGN
    base2 = base1 + n1
    pos1 = jnp.sum(jnp.where(a1, c1 + base1, 0.0), axis=1, keepdims=True)
    pos2 = jnp.sum(jnp.where(a2, c2 + base2, 0.0), axis=1, keepdims=True)
    posm = jnp.where(lane == 0.0, pos1, jnp.where(lane == 1.0, pos2, 0.0))
    pos_ref[...] = posm
    row = _iota_f32((SUBLANES, LANES), 0)
    meta_ref[0] = jnp.where(row == 0.0, start, jnp.where(row == 1.0, cnt, 0.0))

    g1 = _split3(rec[:, 2:3])
    g2 = _split3(rec[:, 3:4])
    info = jnp.where(lane == 6.0, rec[:, 0:1], jnp.where(lane == 7.0, rec[:, 1:2], 0.0))
    for i in range(3):
        info = jnp.where(lane == float(i), g1[i].astype(F32), info)
        info = jnp.where(lane == float(3 + i), g2[i].astype(F32), info)
    src = jnp.concatenate([tok, info.astype(BF16)], axis=1)

    post = posm.T
    p1 = post[0:1, :]
    p2 = post[1:2, :]
    used = jnp.sum(cnt) * ROW_ALIGN
    for b in range(SUBP // PBLK):
        rows = slice(b * PBLK, (b + 1) * PBLK)

        @pl.when(b * PBLK < used)
        def _(b=b, rows=rows):
            r = _iota_f32((PBLK, SUB), 0) + float(b * PBLK)
            onehot = jnp.where(r == p1, 1.0, jnp.where(r == p2, 1.0, 0.0)).astype(BF16)
            s_ref[rows, :] = jnp.dot(onehot, src, preferred_element_type=F32).astype(BF16)

        @pl.when(b * PBLK >= used)
        def _(rows=rows):
            s_ref[rows, :] = jnp.zeros((PBLK, ROW_W), BF16)


def _dispatch(rec_a, t_a, rec_b, t_b):
    nsub_a = rec_a.shape[0] // SUB
    nsub_b = rec_b.shape[0] // SUB
    nsub = nsub_a + nsub_b
    row = lambda i: (i, 0)
    row_a = lambda i: (jnp.minimum(i, nsub_a - 1), 0)
    row_b = lambda i: (jnp.maximum(i - nsub_a, 0), 0)
    return pl.pallas_call(
        functools.partial(_dispatch_body, nsub_a=nsub_a),
        grid=(nsub,),
        in_specs=[
            pl.BlockSpec((SUB, LANES), row_a),
            pl.BlockSpec((SUB, D_MODEL), row_a),
            pl.BlockSpec((SUB, LANES), row_b),
            pl.BlockSpec((SUB, D_MODEL), row_b),
        ],
        out_specs=[
            pl.BlockSpec((SUBP, ROW_W), row),
            pl.BlockSpec((SUB, LANES), row),
            pl.BlockSpec((1, SUBLANES, LANES), lambda i: (i, 0, 0)),
        ],
        out_shape=[
            jax.ShapeDtypeStruct((nsub * SUBP, ROW_W), BF16),
            jax.ShapeDtypeStruct((nsub * SUB, LANES), F32),
            jax.ShapeDtypeStruct((nsub, SUBLANES, LANES), F32),
        ],
        compiler_params=_cparams(("parallel",)),
        name="moe_dispatch",
    )(rec_a, t_a, rec_b, t_b)


def _experts_body(start_ref, cnt_ref, s_in, wg_ref, wu_ref, wd_ref, s_hbm,
                  xbuf, ybuf, wgu_ref, wdb_ref, gsem, ssem, list_ref, state_ref, *, nsub):
    del s_in
    e = pl.program_id(0)
    ne = pl.num_programs(0)
    par = e & 1

    def chunk_rows(i):
        return pl.ds(pl.multiple_of(i * ROW_ALIGN, ROW_ALIGN), ROW_ALIGN)

    def gather_copy(src, i, slot):
        return pltpu.make_async_copy(s_hbm.at[chunk_rows(src), :], xbuf.at[slot, chunk_rows(i), :], gsem.at[slot])

    def scatter_copy(dst, i, slot):
        return pltpu.make_async_copy(ybuf.at[slot, chunk_rows(i), :],
                                     s_hbm.at[chunk_rows(dst), pl.ds(0, D_MODEL)], ssem.at[slot])

    def build_list(x, which):
        def per_sub(s, k):
            run = s * N_EXPERTS + x
            c = cnt_ref[run]
            base = s * CHUNKS_PER_SUB + start_ref[run]

            def per_chunk(i, carry):
                list_ref[which, k + i] = base + i
                return carry
            lax.fori_loop(0, c, per_chunk, 0)
            return k + c
        state_ref[which] = lax.fori_loop(0, nsub, per_sub, jnp.int32(0))

    def start_all(copy, which, first, n, slot, counter):
        def body(i, carry):
            copy(list_ref[which, first + i], i, slot).start()
            return carry
        lax.fori_loop(0, n, body, 0)
        state_ref[counter] = n

    def wait_all(copy, slot, counter):
        def body(i, carry):
            copy(0, 0, slot).wait()
            return carry
        lax.fori_loop(0, state_ref[counter], body, 0)
        state_ref[counter] = 0

    def block_chunks(total, b):
        return jnp.minimum(total - b * CHUNKS_PER_BLK, CHUNKS_PER_BLK)

    @pl.when(e == 0)
    def _():
        for i in range(6):
            state_ref[i] = 0
        xbuf[...] = jnp.zeros_like(xbuf)
        build_list(0, 0)
        n0 = state_ref[0]

        @pl.when(n0 > 0)
        def _():
            start_all(gather_copy, 0, 0, block_chunks(n0, 0), 0, 2)

    @pl.when(e + 1 < ne)
    def _():
        build_list(e + 1, 1 - par)

    total = state_ref[par]
    nblk = (total + CHUNKS_PER_BLK - 1) // CHUNKS_PER_BLK
    wgu_ref[:, 0:EXPERT_FF] = wg_ref[0].astype(BF16)
    wgu_ref[:, EXPERT_FF:] = wu_ref[0].astype(BF16)
    wdb_ref[...] = wd_ref[0].astype(BF16)
    ef = e.astype(F32)

    def block(b, carry):
        slot = b & 1
        first = b * CHUNKS_PER_BLK

        @pl.when(b + 1 < nblk)
        def _():
            start_all(gather_copy, par, first + CHUNKS_PER_BLK, block_chunks(total, b + 1), 1 - slot, 3 - slot)

        wait_all(gather_copy, slot, 2 + slot)
        wait_all(scatter_copy, slot, 4 + slot)
        x = xbuf[slot]
        info = x[:, D_MODEL:].astype(F32)
        g_first = info[:, 0:1] + info[:, 1:2] + info[:, 2:3]
        g_second = info[:, 3:4] + info[:, 4:5] + info[:, 5:6]
        gate = jnp.where(info[:, 6:7] == ef, g_first, g_second)
        h = jnp.dot(x[:, :D_MODEL], wgu_ref[...], preferred_element_type=F32)
        h1 = h[:, :EXPERT_FF]
        hid = (h1 * jax.nn.sigmoid(h1)) * h[:, EXPERT_FF:] * gate
        ybuf[slot] = jnp.dot(hid.astype(BF16), wdb_ref[...], preferred_element_type=F32).astype(BF16)
        start_all(scatter_copy, par, first, block_chunks(total, b), slot, 4 + slot)
        return carry

    lax.fori_loop(0, nblk, block, 0)

    @pl.when(e + 1 < ne)
    def _():
        n1 = state_ref[1 - par]

        @pl.when(n1 > 0)
        def _():
            start_all(gather_copy, 1 - par, 0, block_chunks(n1, 0), 0, 2)

    @pl.when(e == ne - 1)
    def _():
        wait_all(scatter_copy, 0, 4)
        wait_all(scatter_copy, 1, 5)


def _experts(start, cnt, staged, wg, wu, wd):
    nsub = staged.shape[0] // SUBP
    list_max = nsub * SUB // ROW_ALIGN + nsub
    wblk = lambda e, *_: (e, 0, 0)
    grid_spec = pltpu.PrefetchScalarGridSpec(
        num_scalar_prefetch=2,
        grid=(N_EXPERTS,),
        in_specs=[
            pl.BlockSpec(memory_space=pl.ANY),
            pl.BlockSpec((1, D_MODEL, EXPERT_FF), wblk),
            pl.BlockSpec((1, D_MODEL, EXPERT_FF), wblk),
            pl.BlockSpec((1, EXPERT_FF, D_MODEL), wblk),
        ],
        out_specs=pl.BlockSpec(memory_space=pl.ANY),
        scratch_shapes=[
            pltpu.VMEM((2, MBLK, ROW_W), BF16),
            pltpu.VMEM((2, MBLK, D_MODEL), BF16),
            pltpu.VMEM((D_MODEL, 2 * EXPERT_FF), BF16),
            pltpu.VMEM((EXPERT_FF, D_MODEL), BF16),
            pltpu.SemaphoreType.DMA((2,)),
            pltpu.SemaphoreType.DMA((2,)),
            pltpu.SMEM((2, list_max), jnp.int32),
            pltpu.SMEM((6,), jnp.int32),
        ],
    )
    return pl.pallas_call(
        functools.partial(_experts_body, nsub=nsub),
        grid_spec=grid_spec,
        out_shape=jax.ShapeDtypeStruct(staged.shape, staged.dtype),
        input_output_aliases={2: 0},
        compiler_params=_cparams(("arbitrary",)),
        name="moe_experts",
    )(start, cnt, staged, wg, wu, wd)


def _combine_body(start_ref, cnt_ref, ys_ref, pos_ref, x1_ref, p_ref, wp_ref, gp_ref, wpg_ref, gf_ref, y_ref,
                  acc_ref, *, sub_off):
    last_run = (pl.program_id(0) + sub_off) * N_EXPERTS + (N_EXPERTS - 1)
    used = (start_ref[last_run] + cnt_ref[last_run]) * ROW_ALIGN
    p1 = pos_ref[:, 0:1]
    p2 = pos_ref[:, 1:2]
    acc_ref[...] = x1_ref[...]
    for b in range(SUBP // PBLK):
        @pl.when(b * PBLK < used)
        def _(b=b):
            r = _iota_f32((SUB, PBLK), 1) + float(b * PBLK)
            onehot = jnp.where(r == p1, 1.0, jnp.where(r == p2, 1.0, 0.0)).astype(BF16)
            acc_ref[...] += jnp.dot(onehot, ys_ref[b * PBLK:(b + 1) * PBLK, :], preferred_element_type=F32)
    x2 = acc_ref[...]
    ple = _rms(jnp.dot(p_ref[...].astype(BF16), wp_ref[...], preferred_element_type=F32), gp_ref[...])
    gate = jax.nn.sigmoid(jnp.dot(x2.astype(BF16), wpg_ref[...], preferred_element_type=F32))
    y_ref[...] = _rms(x2 + ple * gate, gf_ref[...])


def _combine(start, cnt, ys, pos, x1, p, w_ple, gp, w_ple_gate, gf, sub_off):
    t = x1.shape[0]
    row = lambda i, *_: (i, 0)
    const = lambda i, *_: (0, 0)
    grid_spec = pltpu.PrefetchScalarGridSpec(
        num_scalar_prefetch=2,
        grid=(t // SUB,),
        in_specs=[
            pl.BlockSpec((SUBP, D_MODEL), lambda i, *_: (i + sub_off, 0)),
            pl.BlockSpec((SUB, LANES), lambda i, *_: (i + sub_off, 0)),
            pl.BlockSpec((SUB, D_MODEL), row),
            pl.BlockSpec((SUB, PLE_DIM), row),
            pl.BlockSpec((PLE_DIM, D_MODEL), const),
            pl.BlockSpec((1, D_MODEL), const),
            pl.BlockSpec((D_MODEL, D_MODEL), const),
            pl.BlockSpec((1, D_MODEL), const),
        ],
        out_specs=pl.BlockSpec((SUB, D_MODEL), row),
        scratch_shapes=[pltpu.VMEM((SUB, D_MODEL), F32)],
    )
    return pl.pallas_call(
        functools.partial(_combine_body, sub_off=sub_off),
        grid_spec=grid_spec,
        out_shape=jax.ShapeDtypeStruct((t, D_MODEL), F32),
        compiler_params=_cparams(("arbitrary",)),
        name="moe_combine_ple",
    )(start, cnt, ys, pos, x1, p, w_ple, gp, w_ple_gate, gf)


def _rope_tables(pos):
    half = HEAD_DIM // 2
    inv = ROPE_BASE ** (-jnp.arange(half, dtype=F32) / half)
    ang = pos[:, None] * inv[None, :]
    cos = jnp.cos(ang)
    sin = jnp.sin(ang)
    return jnp.concatenate([cos, cos], axis=-1), jnp.concatenate([-sin, sin], axis=-1)


def _router_params(we, be, wg, bg):
    pad = LANES - N_EXPERTS - N_GROUPS
    w = jnp.pad(jnp.concatenate([we, wg], axis=1), ((0, 0), (0, pad)))
    b = jnp.pad(jnp.concatenate([be, bg]), (0, pad))[None, :]
    return jnp.stack(_split3(w)[:2]), b


def kernel(x_prompt, x_sample, p_prompt, p_sample, state_conv, state_ret, w_in, conv_w, conv_b, conv_ln_g, conv_ln_b, w_out, norm1_g, norm2_g, router_group_w, router_group_b, router_expert_w, router_expert_b, w_expert_gate, w_expert_up, w_expert_down, w_ple, ple_norm_g, w_ple_gate, final_norm_g):
    assert w_in.shape[0] == 1, "single-layer trunk"
    nb, seq, _ = x_prompt.shape
    ns, dseq, _ = x_sample.shape
    tm = 512

    w_in_b = w_in[0].astype(BF16)
    w_out_b = w_out[0].astype(BF16)
    w_ple_b = w_ple[0].astype(BF16)
    w_pg_b = w_ple_gate[0].astype(BF16)
    g1 = norm1_g[0][None, :]
    g2 = norm2_g[0][None, :]
    gp = ple_norm_g[0][None, :]
    gf = final_norm_g[None, :]
    cb = conv_b[0][None, :]
    lng = conv_ln_g[0][None, :]
    lnb = conv_ln_b[0][None, :]
    wr2, br = _router_params(router_expert_w[0], router_expert_b[0], router_group_w[0], router_group_b[0])

    cos_p, sin_p = _rope_tables(jnp.arange(seq, dtype=F32) + jnp.float32(0))
    pos_s = jnp.tile(jnp.arange(dseq, dtype=F32) + jnp.float32(PAST_LEN), tm // dseq)
    cos_s, sin_s = _rope_tables(pos_s)

    xp = x_prompt.reshape(nb * seq, D_MODEL)
    u, q, k, v, gs = _inproj(xp, g1, w_in_b, cos_p, sin_p, tm, seq // tm, BF16)
    c, conv_p = _conv_prompt(u.reshape(nb, seq, CONV_CH), conv_w[0], cb, lng, lnb, tm)
    o, ret_p = _ret_prompt(q, k, v, gs, nb, seq, tm)
    x1_p, t_p, rec_p = _outproj(c.reshape(nb * seq, CONV_CH), o, xp, w_out_b, g2, wr2, br, tm)

    xs = x_sample.reshape(ns * dseq, D_MODEL)
    u, q, k, v, gs = _inproj(xs, g1, w_in_b, cos_s, sin_s, tm, 1, F32)
    c, conv_s = _conv_sample(u.reshape(ns, dseq, CONV_CH), state_conv[0], conv_w[0], cb, lng, lnb, 16)
    o, ret_s = _ret_sample(q, k, v, gs, state_ret[0], dseq, 8)
    x1_s, t_s, rec_s = _outproj(c.reshape(ns * dseq, CONV_CH), o, xs, w_out_b, g2, wr2, br, tm)

    staged, pos, meta = _dispatch(rec_p, t_p, rec_s, t_s)
    start = meta[:, 0, :N_EXPERTS].astype(jnp.int32).reshape(-1)
    cnt = meta[:, 1, :N_EXPERTS].astype(jnp.int32).reshape(-1)
    ys = _experts(start, cnt, staged, w_expert_gate[0], w_expert_up[0], w_expert_down[0])

    y_p = _combine(start, cnt, ys, pos, x1_p, p_prompt[0].reshape(nb * seq, PLE_DIM), w_ple_b, gp, w_pg_b, gf, 0)
    y_s = _combine(start, cnt, ys, pos, x1_s, p_sample[0].reshape(ns * dseq, PLE_DIM), w_ple_b, gp, w_pg_b, gf,
                   nb * seq // SUB)

    return (y_p.reshape(nb, seq, D_MODEL), y_s.reshape(ns, dseq, D_MODEL),
            conv_p[None], ret_p[None], conv_s[None], ret_s[None])
```

```python
import functools

import jax
import jax.numpy as jnp
from jax import lax
from jax.experimental import pallas as pl
from jax.experimental.pallas import tpu as pltpu

F32 = jnp.float32
BF16 = jnp.bfloat16

D_MODEL = 1024
PLE_DIM = 256
CONV_CH = 512
CONV_K = 31
RET_WIDTH = 512
RET_HEADS = 4
HEAD_DIM = 128
CHUNK = 128
ROPE_BASE = 10000.0
N_GROUPS = 4
EXPERTS_PER_GROUP = 8
N_EXPERTS = 32
EXPERT_FF = 256
IN_COLS = 3072
EPS = 1e-6
PAST_LEN = 16384

LANES = 128
SUBLANES = 8
HALO = 32
HALO_OFF = HALO - (CONV_K - 1)
VMEM_LIMIT = 48 * 1024 * 1024

SUB = 256
ROW_ALIGN = 16
PBLK = 256
SUBP = -(-(2 * SUB + N_EXPERTS * (ROW_ALIGN - 1)) // PBLK) * PBLK
CHUNKS_PER_SUB = SUBP // ROW_ALIGN
ROW_W = D_MODEL + LANES
MBLK = 512
CHUNKS_PER_BLK = MBLK // ROW_ALIGN
LIST_SLACK = 2


def _cparams(sem):
    return pltpu.CompilerParams(dimension_semantics=sem, vmem_limit_bytes=VMEM_LIMIT)


def _rms(x, g):
    return x * lax.rsqrt(jnp.mean(x * x, axis=-1, keepdims=True) + EPS) * g


def _inproj_body(x_ref, g1_ref, w_ref, cos_ref, sin_ref, u_ref, q_ref, k_ref, v_ref, gs_ref):
    h = _rms(x_ref[...], g1_ref[...]).astype(BF16)
    z = jnp.dot(h, w_ref[...], preferred_element_type=F32)
    a = z[:, :CONV_CH]
    b = z[:, CONV_CH:2 * CONV_CH]
    u_ref[...] = a * jax.nn.sigmoid(b)
    cos = cos_ref[...]
    sin = sin_ref[...]
    q0 = 2 * CONV_CH
    k0 = q0 + RET_WIDTH
    for hh in range(RET_HEADS):
        sl = slice(hh * HEAD_DIM, (hh + 1) * HEAD_DIM)
        qh = z[:, q0 + hh * HEAD_DIM:q0 + (hh + 1) * HEAD_DIM]
        kh = z[:, k0 + hh * HEAD_DIM:k0 + (hh + 1) * HEAD_DIM]
        q_ref[:, sl] = (qh * cos + pltpu.roll(qh, HEAD_DIM // 2, 1) * sin).astype(q_ref.dtype)
        kr = (kh * cos + pltpu.roll(kh, HEAD_DIM // 2, 1) * sin) * (HEAD_DIM ** -0.5)
        k_ref[:, sl] = kr.astype(k_ref.dtype)
    v_ref[...] = z[:, k0 + RET_WIDTH:k0 + 2 * RET_WIDTH].astype(v_ref.dtype)
    g = z[:, k0 + 2 * RET_WIDTH:]
    gs_ref[...] = g * jax.nn.sigmoid(g)


def _inproj(x, g1, w_in, cos, sin, tm, table_blocks, qkv_dtype):
    t = x.shape[0]
    row = lambda i: (i, 0)
    const = lambda i: (0, 0)
    tab = (lambda i: (i % table_blocks, 0)) if table_blocks > 1 else const
    return pl.pallas_call(
        _inproj_body,
        grid=(t // tm,),
        in_specs=[
            pl.BlockSpec((tm, D_MODEL), row),
            pl.BlockSpec((1, D_MODEL), const),
            pl.BlockSpec((D_MODEL, IN_COLS), const),
            pl.BlockSpec((tm, HEAD_DIM), tab),
            pl.BlockSpec((tm, HEAD_DIM), tab),
        ],
        out_specs=[pl.BlockSpec((tm, CONV_CH), row)] + [pl.BlockSpec((tm, RET_WIDTH), row)] * 4,
        out_shape=[
            jax.ShapeDtypeStruct((t, CONV_CH), F32),
            jax.ShapeDtypeStruct((t, RET_WIDTH), qkv_dtype),
            jax.ShapeDtypeStruct((t, RET_WIDTH), qkv_dtype),
            jax.ShapeDtypeStruct((t, RET_WIDTH), qkv_dtype),
            jax.ShapeDtypeStruct((t, RET_WIDTH), F32),
        ],
        compiler_params=_cparams(("parallel",)),
        name="inproj",
    )(x, g1, w_in, cos, sin)


def _ln_silu(acc, g, b):
    mu = jnp.mean(acc, axis=-1, keepdims=True)
    d = acc - mu
    var = jnp.mean(d * d, axis=-1, keepdims=True)
    y = d * lax.rsqrt(var + EPS) * g + b
    return y * jax.nn.sigmoid(y)


def _dwconv(load, w_ref, rows, time_axis):
    acc = None
    for b in range(SUBLANES):
        part = None
        for a in range((CONV_K + HALO_OFF) // SUBLANES + 1):
            k = SUBLANES * a + b - HALO_OFF
            if 0 <= k < CONV_K:
                term = load(SUBLANES * a, rows + SUBLANES) * w_ref[k:k + 1, :]
                part = term if part is None else part + term
        if part is not None:
            shifted = lax.slice_in_dim(part, b, b + rows, axis=time_axis)
            acc = shifted if acc is None else acc + shifted
    return acc


def _conv_prompt_body(u_ref, w_ref, cb_ref, lg_ref, lb_ref, c_ref, st_ref, ext_ref):
    j = pl.program_id(1)
    tl = u_ref.shape[1]

    @pl.when(j == 0)
    def _():
        ext_ref[0:HALO, :] = jnp.zeros((HALO, CONV_CH), F32)
        ext_ref[tl + HALO:, :] = jnp.zeros((SUBLANES, CONV_CH), F32)

    @pl.when(j > 0)
    def _():
        ext_ref[0:HALO, :] = ext_ref[tl:tl + HALO, :]

    ext_ref[HALO:tl + HALO, :] = u_ref[0]
    acc = _dwconv(lambda s, n: ext_ref[s:s + n, :], w_ref, tl, 0) + cb_ref[...]
    c_ref[0] = _ln_silu(acc, lg_ref[...], lb_ref[...]).astype(c_ref.dtype)
    st_ref[0] = ext_ref[tl + HALO_OFF:tl + HALO, :]


def _conv_prompt(u, conv_w, conv_b, ln_g, ln_b, tl):
    n, l, _ = u.shape
    const = lambda b, j: (0, 0)
    return pl.pallas_call(
        _conv_prompt_body,
        grid=(n, l // tl),
        in_specs=[
            pl.BlockSpec((1, tl, CONV_CH), lambda b, j: (b, j, 0)),
            pl.BlockSpec((CONV_K, CONV_CH), const),
            pl.BlockSpec((1, CONV_CH), const),
            pl.BlockSpec((1, CONV_CH), const),
            pl.BlockSpec((1, CONV_CH), const),
        ],
        out_specs=[
            pl.BlockSpec((1, tl, CONV_CH), lambda b, j: (b, j, 0)),
            pl.BlockSpec((1, CONV_K - 1, CONV_CH), lambda b, j: (b, 0, 0)),
        ],
        out_shape=[
            jax.ShapeDtypeStruct((n, l, CONV_CH), BF16),
            jax.ShapeDtypeStruct((n, CONV_K - 1, CONV_CH), F32),
        ],
        scratch_shapes=[pltpu.VMEM((tl + HALO + SUBLANES, CONV_CH), F32)],
        compiler_params=_cparams(("arbitrary", "arbitrary")),
        name="conv_prompt",
    )(u, conv_w, conv_b, ln_g, ln_b)


def _conv_sample_body(u_ref, st_ref, w_ref, cb_ref, lg_ref, lb_ref, c_ref, nst_ref, ext_ref):
    nb, l, _ = u_ref.shape
    ext_ref[:, 0:HALO_OFF, :] = jnp.zeros((nb, HALO_OFF, CONV_CH), F32)
    ext_ref[:, HALO_OFF:HALO, :] = st_ref[...]
    ext_ref[:, HALO:l + HALO, :] = u_ref[...]
    ext_ref[:, l + HALO:, :] = jnp.zeros((nb, SUBLANES, CONV_CH), F32)
    acc = _dwconv(lambda s, n: ext_ref[:, s:s + n, :], w_ref, l, 1) + cb_ref[...]
    c_ref[...] = _ln_silu(acc, lg_ref[...], lb_ref[...]).astype(c_ref.dtype)
    nst_ref[...] = ext_ref[:, l + HALO_OFF:l + HALO, :]


def _conv_sample(u, state, conv_w, conv_b, ln_g, ln_b, nb):
    n, l, _ = u.shape
    const = lambda b: (0, 0)
    blk3 = lambda b: (b, 0, 0)
    return pl.pallas_call(
        _conv_sample_body,
        grid=(n // nb,),
        in_specs=[
            pl.BlockSpec((nb, l, CONV_CH), blk3),
            pl.BlockSpec((nb, CONV_K - 1, CONV_CH), blk3),
            pl.BlockSpec((CONV_K, CONV_CH), const),
            pl.BlockSpec((1, CONV_CH), const),
            pl.BlockSpec((1, CONV_CH), const),
            pl.BlockSpec((1, CONV_CH), const),
        ],
        out_specs=[
            pl.BlockSpec((nb, l, CONV_CH), blk3),
            pl.BlockSpec((nb, CONV_K - 1, CONV_CH), blk3),
        ],
        out_shape=[
            jax.ShapeDtypeStruct((n, l, CONV_CH), BF16),
            jax.ShapeDtypeStruct((n, CONV_K - 1, CONV_CH), F32),
        ],
        scratch_shapes=[pltpu.VMEM((nb, l + HALO + SUBLANES, CONV_CH), F32)],
        compiler_params=_cparams(("parallel",)),
        name="conv_sample",
    )(u, state, conv_w, conv_b, ln_g, ln_b)


def _decay_tables(c):
    lg = jnp.log(1.0 - 2.0 ** (-5.0 - jnp.arange(RET_HEADS, dtype=F32)))
    idx = jnp.arange(c, dtype=F32)
    rel = idx[:, None] - idx[None, :]
    dmat = jnp.where(rel[None] >= 0, jnp.exp(jnp.maximum(rel, 0.0)[None] * lg[:, None, None]), 0.0)
    xi = jnp.exp((idx + 1.0)[None, :] * lg[:, None])
    zeta = jnp.exp((c - 1.0 - idx)[None, :] * lg[:, None])
    gc = jnp.exp(c * lg)
    xi_b = jnp.broadcast_to(xi[:, :, None], (RET_HEADS, c, HEAD_DIM))
    zeta_b = jnp.broadcast_to(zeta[:, :, None], (RET_HEADS, c, HEAD_DIM))
    gc_b = jnp.broadcast_to(gc[:, None, None], (RET_HEADS, 1, HEAD_DIM))
    return dmat, xi_b, zeta_b, gc_b


def _group_norm(o):
    mu = jnp.mean(o, axis=-1, keepdims=True)
    d = o - mu
    var = jnp.mean(d * d, axis=-1, keepdims=True)
    return d * lax.rsqrt(var + EPS)


def _ret_chunk(qh, kh, vh, r, dmat, xi, zeta, gc):
    qb = qh.astype(BF16)
    kb = kh.astype(BF16)
    vb = vh.astype(BF16)
    s = lax.dot_general(qb, kb, (((1,), (1,)), ((), ())), preferred_element_type=F32) * dmat
    o = jnp.dot(s.astype(BF16), vb, preferred_element_type=F32)
    o = o + jnp.dot(qb, r.astype(BF16), preferred_element_type=F32) * xi
    kz = (kh.astype(F32) * zeta).astype(BF16)
    r_new = r * gc + lax.dot_general(kz, vb, (((0,), (0,)), ((), ())), preferred_element_type=F32)
    return o, r_new


def _ret_prompt_body(q_ref, k_ref, v_ref, gs_ref, d_ref, xi_ref, zeta_ref, gc_ref, o_ref, st_ref, r_ref):
    j = pl.program_id(1)

    @pl.when(j == 0)
    def _():
        r_ref[...] = jnp.zeros_like(r_ref)

    n_chunks = q_ref.shape[0] // CHUNK
    for hh in range(RET_HEADS):
        sl = slice(hh * HEAD_DIM, (hh + 1) * HEAD_DIM)
        r = r_ref[hh]
        for ci in range(n_chunks):
            rows = slice(ci * CHUNK, (ci + 1) * CHUNK)
            o, r = _ret_chunk(q_ref[rows, sl], k_ref[rows, sl], v_ref[rows, sl], r,
                              d_ref[hh], xi_ref[hh], zeta_ref[hh], gc_ref[hh])
            o_ref[rows, sl] = (gs_ref[rows, sl] * _group_norm(o)).astype(o_ref.dtype)
        r_ref[hh] = r
    st_ref[0] = r_ref[...]


def _ret_prompt(q, k, v, gs, n, l, tl):
    dmat, xi, zeta, gc = _decay_tables(CHUNK)
    per = l // tl
    row = lambda b, j: (b * per + j, 0)
    c3 = lambda b, j: (0, 0, 0)
    return pl.pallas_call(
        _ret_prompt_body,
        grid=(n, per),
        in_specs=[pl.BlockSpec((tl, RET_WIDTH), row)] * 4 + [
            pl.BlockSpec((RET_HEADS, CHUNK, CHUNK), c3),
            pl.BlockSpec((RET_HEADS, CHUNK, HEAD_DIM), c3),
            pl.BlockSpec((RET_HEADS, CHUNK, HEAD_DIM), c3),
            pl.BlockSpec((RET_HEADS, 1, HEAD_DIM), c3),
        ],
        out_specs=[
            pl.BlockSpec((tl, RET_WIDTH), row),
            pl.BlockSpec((1, RET_HEADS, HEAD_DIM, HEAD_DIM), lambda b, j: (b, 0, 0, 0)),
        ],
        out_shape=[
            jax.ShapeDtypeStruct((n * l, RET_WIDTH), BF16),
            jax.ShapeDtypeStruct((n, RET_HEADS, HEAD_DIM, HEAD_DIM), F32),
        ],
        scratch_shapes=[pltpu.VMEM((RET_HEADS, HEAD_DIM, HEAD_DIM), F32)],
        compiler_params=_cparams(("arbitrary", "arbitrary")),
        name="ret_prompt",
    )(q, k, v, gs, dmat, xi, zeta, gc)


def _ret_sample_body(q_ref, k_ref, v_ref, gs_ref, st_ref, d_ref, xi_ref, zeta_ref, gc_ref, o_ref, nst_ref):
    nb = st_ref.shape[0]
    l = q_ref.shape[0] // nb
    for b in range(nb):
        rows = slice(b * l, (b + 1) * l)
        for hh in range(RET_HEADS):
            sl = slice(hh * HEAD_DIM, (hh + 1) * HEAD_DIM)
            o, r = _ret_chunk(q_ref[rows, sl], k_ref[rows, sl], v_ref[rows, sl], st_ref[b, hh],
                              d_ref[hh], xi_ref[hh], zeta_ref[hh], gc_ref[hh])
            o_ref[rows, sl] = (gs_ref[rows, sl] * _group_norm(o)).astype(o_ref.dtype)
            nst_ref[b, hh] = r


def _ret_sample(q, k, v, gs, state, l, nb):
    n = state.shape[0]
    dmat, xi, zeta, gc = _decay_tables(l)
    row = lambda b: (b, 0)
    c3 = lambda b: (0, 0, 0)
    blk4 = lambda b: (b, 0, 0, 0)
    return pl.pallas_call(
        _ret_sample_body,
        grid=(n // nb,),
        in_specs=[pl.BlockSpec((nb * l, RET_WIDTH), row)] * 4 + [
            pl.BlockSpec((nb, RET_HEADS, HEAD_DIM, HEAD_DIM), blk4),
            pl.BlockSpec((RET_HEADS, l, l), c3),
            pl.BlockSpec((RET_HEADS, l, HEAD_DIM), c3),
            pl.BlockSpec((RET_HEADS, l, HEAD_DIM), c3),
            pl.BlockSpec((RET_HEADS, 1, HEAD_DIM), c3),
        ],
        out_specs=[
            pl.BlockSpec((nb * l, RET_WIDTH), row),
            pl.BlockSpec((nb, RET_HEADS, HEAD_DIM, HEAD_DIM), blk4),
        ],
        out_shape=[
            jax.ShapeDtypeStruct((n * l, RET_WIDTH), BF16),
            jax.ShapeDtypeStruct((n, RET_HEADS, HEAD_DIM, HEAD_DIM), F32),
        ],
        compiler_params=_cparams(("parallel",)),
        name="ret_sample",
    )(q, k, v, gs, state, dmat, xi, zeta, gc)


def _split3(x):
    hi = x.astype(BF16)
    r1 = x - hi.astype(F32)
    mid = r1.astype(BF16)
    lo = (r1 - mid.astype(F32)).astype(BF16)
    return hi, mid, lo


def _dot_hp(t, w_hi, w_mid):
    t_hi, t_mid, _ = _split3(t)
    d = functools.partial(jnp.dot, preferred_element_type=F32)
    return d(t_hi, w_hi) + (d(t_mid, w_hi) + d(t_hi, w_mid))


def _route(logits):
    lane = lax.broadcasted_iota(jnp.int32, logits.shape, 1)
    big = jnp.int32(LANES)
    is_group = (lane >= N_EXPERTS) & (lane < N_EXPERTS + N_GROUPS)
    lg = jnp.where(is_group, logits, -1e30)
    m = jnp.max(lg, axis=-1, keepdims=True)
    g_top = 1.0 / jnp.sum(jnp.exp(lg - m), axis=-1, keepdims=True)
    g_idx = jnp.min(jnp.where(lg == m, lane, big), axis=-1, keepdims=True) - N_EXPERTS
    in_group = (lane >= g_idx * EXPERTS_PER_GROUP) & (lane < (g_idx + 1) * EXPERTS_PER_GROUP)
    lem = jnp.where(in_group, logits, -1e30)
    m2 = jnp.max(lem, axis=-1, keepdims=True)
    pe = jnp.where(in_group, jnp.exp(lem - m2), 0.0)
    p1 = jnp.max(pe, axis=-1, keepdims=True)
    e1 = jnp.min(jnp.where(in_group & (pe == p1), lane, big), axis=-1, keepdims=True)
    rest = in_group & (lane != e1)
    pe2 = jnp.where(rest, pe, -1.0)
    p2 = jnp.max(pe2, axis=-1, keepdims=True)
    e2 = jnp.min(jnp.where(rest & (pe2 == p2), lane, big), axis=-1, keepdims=True)
    scale = g_top / (p1 + p2)
    rec = jnp.where(lane == 0, e1.astype(F32), jnp.where(lane == 1, e2.astype(F32), 0.0))
    return rec + jnp.where(lane == 2, p1 * scale, jnp.where(lane == 3, p2 * scale, 0.0))


def _outproj_body(c_ref, o_ref, x_ref, wo_ref, g2_ref, wr_ref, br_ref, x1_ref, t_ref, rec_ref):
    x1 = x_ref[...] + jnp.dot(c_ref[...], wo_ref[0:CONV_CH, :], preferred_element_type=F32)
    x1 = x1 + jnp.dot(o_ref[...], wo_ref[CONV_CH:, :], preferred_element_type=F32)
    x1_ref[...] = x1
    t = _rms(x1, g2_ref[...])
    t_ref[...] = t.astype(t_ref.dtype)
    rec_ref[...] = _route(_dot_hp(t, wr_ref[0], wr_ref[1]) + br_ref[...])


def _outproj(c, o, x, w_out, g2, wr2, br, tm):
    t = x.shape[0]
    row = lambda i: (i, 0)
    const = lambda i: (0, 0)
    c3 = lambda i: (0, 0, 0)
    return pl.pallas_call(
        _outproj_body,
        grid=(t // tm,),
        in_specs=[
            pl.BlockSpec((tm, CONV_CH), row),
            pl.BlockSpec((tm, RET_WIDTH), row),
            pl.BlockSpec((tm, D_MODEL), row),
            pl.BlockSpec((D_MODEL, D_MODEL), const),
            pl.BlockSpec((1, D_MODEL), const),
            pl.BlockSpec((2, D_MODEL, LANES), c3),
            pl.BlockSpec((1, LANES), const),
        ],
        out_specs=[
            pl.BlockSpec((tm, D_MODEL), row),
            pl.BlockSpec((tm, D_MODEL), row),
            pl.BlockSpec((tm, LANES), row),
        ],
        out_shape=[
            jax.ShapeDtypeStruct((t, D_MODEL), F32),
            jax.ShapeDtypeStruct((t, D_MODEL), BF16),
            jax.ShapeDtypeStruct((t, LANES), F32),
        ],
        compiler_params=_cparams(("parallel",)),
        name="outproj_router",
    )(c, o, x, w_out, g2, wr2, br)


def _iota_f32(shape, dim):
    return lax.broadcasted_iota(jnp.int32, shape, dim).astype(F32)


def _dispatch_body(rec_a_ref, t_a_ref, rec_b_ref, t_b_ref, s_ref, pos_ref, meta_ref, *, nsub_a):
    from_a = pl.program_id(0) < nsub_a
    rec = jnp.where(from_a, rec_a_ref[...], rec_b_ref[...])
    tok = jnp.where(from_a, t_a_ref[...], t_b_ref[...])
    lane = _iota_f32(rec.shape, 1)
    a1 = lane == rec[:, 0:1]
    a2 = lane == rec[:, 1:2]
    a1f = jnp.where(a1, 1.0, 0.0)
    a2f = jnp.where(a2, 1.0, 0.0)
    ltri = jnp.where(_iota_f32((SUB, SUB), 1) < _iota_f32((SUB, SUB), 0), 1.0, 0.0).astype(BF16)
    c1 = jnp.dot(ltri, a1f.astype(BF16), preferred_element_type=F32)
    c2 = jnp.dot(ltri, a2f.astype(BF16), preferred_element_type=F32)
    n1 = jnp.sum(a1f, axis=0, keepdims=True)
    n2 = jnp.sum(a2f, axis=0, keepdims=True)
    cnt = jnp.floor((n1 + n2 + (ROW_ALIGN - 1.0)) * (1.0 / ROW_ALIGN))
    utri = jnp.where(_iota_f32((LANES, LANES), 0) < _iota_f32((LANES, LANES), 1), 1.0, 0.0).astype(BF16)
    start = jnp.dot(jnp.broadcast_to(cnt, (SUBLANES, LANES)).astype(BF16), utri,
                    preferred_element_type=F32)[0:1]
    base1 = start * ROW_ALIGN
    base2 = base1 + n1
    pos1 = jnp.sum(jnp.where(a1, c1 + base1, 0.0), axis=1, keepdims=True)
    pos2 = jnp.sum(jnp.where(a2, c2 + base2, 0.0), axis=1, keepdims=True)
    posm = jnp.where(lane == 0.0, pos1, jnp.where(lane == 1.0, pos2, 0.0))
    pos_ref[...] = posm
    row = _iota_f32((SUBLANES, LANES), 0)
    meta_ref[0] = jnp.where(row == 0.0, start, jnp.where(row == 1.0, cnt, 0.0))

    g1 = _split3(rec[:, 2:3])
    g2 = _split3(rec[:, 3:4])
    info = jnp.where(lane == 6.0, rec[:, 0:1], jnp.where(lane == 7.0, rec[:, 1:2], 0.0))
    for i in range(3):
        info = jnp.where(lane == float(i), g1[i].astype(F32), info)
        info = jnp.where(lane == float(3 + i), g2[i].astype(F32), info)
    src = jnp.concatenate([tok, info.astype(BF16)], axis=1)

    post = posm.T
    p1 = post[0:1, :]
    p2 = post[1:2, :]
    used = jnp.sum(cnt) * ROW_ALIGN
    for b in range(SUBP // PBLK):
        chunks = slice(b * PBLK // ROW_ALIGN, (b + 1) * PBLK // ROW_ALIGN)

        @pl.when(b * PBLK < used)
        def _(b=b, chunks=chunks):
            r = _iota_f32((PBLK, SUB), 0) + float(b * PBLK)
            onehot = jnp.where(r == p1, 1.0, jnp.where(r == p2, 1.0, 0.0)).astype(BF16)
            sorted_rows = jnp.dot(onehot, src, preferred_element_type=F32).astype(BF16)
            s_ref[chunks] = sorted_rows.reshape(PBLK // ROW_ALIGN, ROW_ALIGN, ROW_W)

        @pl.when(b * PBLK >= used)
        def _(chunks=chunks):
            s_ref[chunks] = jnp.zeros((PBLK // ROW_ALIGN, ROW_ALIGN, ROW_W), BF16)


def _dispatch(rec_a, t_a, rec_b, t_b):
    nsub_a = rec_a.shape[0] // SUB
    nsub_b = rec_b.shape[0] // SUB
    nsub = nsub_a + nsub_b
    row = lambda i: (i, 0)
    row_a = lambda i: (jnp.minimum(i, nsub_a - 1), 0)
    row_b = lambda i: (jnp.maximum(i - nsub_a, 0), 0)
    return pl.pallas_call(
        functools.partial(_dispatch_body, nsub_a=nsub_a),
        grid=(nsub,),
        in_specs=[
            pl.BlockSpec((SUB, LANES), row_a),
            pl.BlockSpec((SUB, D_MODEL), row_a),
            pl.BlockSpec((SUB, LANES), row_b),
            pl.BlockSpec((SUB, D_MODEL), row_b),
        ],
        out_specs=[
            pl.BlockSpec((CHUNKS_PER_SUB, ROW_ALIGN, ROW_W), lambda i: (i, 0, 0)),
            pl.BlockSpec((SUB, LANES), row),
            pl.BlockSpec((1, SUBLANES, LANES), lambda i: (i, 0, 0)),
        ],
        out_shape=[
            jax.ShapeDtypeStruct((nsub * CHUNKS_PER_SUB, ROW_ALIGN, ROW_W), BF16),
            jax.ShapeDtypeStruct((nsub * SUB, LANES), F32),
            jax.ShapeDtypeStruct((nsub, SUBLANES, LANES), F32),
        ],
        compiler_params=_cparams(("parallel",)),
        name="moe_dispatch",
    )(rec_a, t_a, rec_b, t_b)


def _experts_body(start_ref, cnt_ref, s_in, wg_ref, wu_ref, wd_ref, s_hbm,
                  xbuf, ybuf, wgu_ref, wdb_ref, gsem, ssem, list_ref, state_ref, *, nsub):
    del s_in
    e = pl.program_id(0)
    ne = pl.num_programs(0)
    par = e & 1

    list_max = list_ref.shape[0] // 2

    def gather_copy(src, i, slot):
        return pltpu.make_async_copy(s_hbm.at[src], xbuf.at[slot, i], gsem.at[slot])

    def scatter_copy(dst, i, slot):
        return pltpu.make_async_copy(ybuf.at[slot, i], s_hbm.at[dst], ssem.at[slot])

    def build_list(x, which):
        def per_sub(s, k):
            run = s * N_EXPERTS + x
            c = cnt_ref[run]
            base = s * CHUNKS_PER_SUB + start_ref[run]
            list_ref[k] = base
            list_ref[k + 1] = base + 1

            def per_chunk(i, carry):
                list_ref[k + i] = base + i
                return carry
            lax.fori_loop(2, c, per_chunk, 0)
            return k + c
        first = which * list_max
        state_ref[which] = lax.fori_loop(0, nsub, per_sub, first) - first

    def start_all(copy, which, first, n, slot, counter):
        def body(i, carry):
            copy(list_ref[which * list_max + first + i], i, slot).start()
            return carry
        lax.fori_loop(0, n, body, 0)
        state_ref[counter] = n

    def wait_all(copy, slot, counter):
        def body(i, carry):
            copy(0, 0, slot).wait()
            return carry
        lax.fori_loop(0, state_ref[counter], body, 0)
        state_ref[counter] = 0

    def block_chunks(total, b):
        return jnp.minimum(total - b * CHUNKS_PER_BLK, CHUNKS_PER_BLK)

    @pl.when(e == 0)
    def _():
        for i in range(6):
            state_ref[i] = 0
        xbuf[...] = jnp.zeros_like(xbuf)
        build_list(0, 0)
        n0 = state_ref[0]

        @pl.when(n0 > 0)
        def _():
            start_all(gather_copy, 0, 0, block_chunks(n0, 0), 0, 2)

    @pl.when(e + 1 < ne)
    def _():
        build_list(e + 1, 1 - par)

    total = state_ref[par]
    nblk = (total + CHUNKS_PER_BLK - 1) // CHUNKS_PER_BLK
    wgu_ref[:, 0:EXPERT_FF] = wg_ref[0].astype(BF16)
    wgu_ref[:, EXPERT_FF:] = wu_ref[0].astype(BF16)
    wdb_ref[...] = wd_ref[0].astype(BF16)
    ef = e.astype(F32)

    def block(b, carry):
        slot = b & 1
        first = b * CHUNKS_PER_BLK

        @pl.when(b + 1 < nblk)
        def _():
            start_all(gather_copy, par, first + CHUNKS_PER_BLK, block_chunks(total, b + 1), 1 - slot, 3 - slot)

        wait_all(gather_copy, slot, 2 + slot)
        wait_all(scatter_copy, slot, 4 + slot)
        x = xbuf[slot].reshape(MBLK, ROW_W)
        info = x[:, D_MODEL:].astype(F32)
        g_first = info[:, 0:1] + info[:, 1:2] + info[:, 2:3]
        g_second = info[:, 3:4] + info[:, 4:5] + info[:, 5:6]
        gate = jnp.where(info[:, 6:7] == ef, g_first, g_second)
        h = jnp.dot(x[:, :D_MODEL], wgu_ref[...], preferred_element_type=F32)
        h1 = h[:, :EXPERT_FF]
        hid = (h1 * jax.nn.sigmoid(h1)) * h[:, EXPERT_FF:] * gate
        y = jnp.dot(hid.astype(BF16), wdb_ref[...], preferred_element_type=F32).astype(BF16)
        ybuf[slot] = jnp.concatenate([y, x[:, D_MODEL:]], axis=1).reshape(CHUNKS_PER_BLK, ROW_ALIGN, ROW_W)
        start_all(scatter_copy, par, first, block_chunks(total, b), slot, 4 + slot)
        return carry

    lax.fori_loop(0, nblk, block, 0)

    @pl.when(e + 1 < ne)
    def _():
        n1 = state_ref[1 - par]

        @pl.when(n1 > 0)
        def _():
            start_all(gather_copy, 1 - par, 0, block_chunks(n1, 0), 0, 2)

    @pl.when(e == ne - 1)
    def _():
        wait_all(scatter_copy, 0, 4)
        wait_all(scatter_copy, 1, 5)


def _experts(start, cnt, staged, wg, wu, wd):
    nsub = staged.shape[0] // CHUNKS_PER_SUB
    list_max = nsub * SUB // ROW_ALIGN + nsub + LIST_SLACK
    wblk = lambda e, *_: (e, 0, 0)
    grid_spec = pltpu.PrefetchScalarGridSpec(
        num_scalar_prefetch=2,
        grid=(N_EXPERTS,),
        in_specs=[
            pl.BlockSpec(memory_space=pl.ANY),
            pl.BlockSpec((1, D_MODEL, EXPERT_FF), wblk),
            pl.BlockSpec((1, D_MODEL, EXPERT_FF), wblk),
            pl.BlockSpec((1, EXPERT_FF, D_MODEL), wblk),
        ],
        out_specs=pl.BlockSpec(memory_space=pl.ANY),
        scratch_shapes=[
            pltpu.VMEM((2, CHUNKS_PER_BLK, ROW_ALIGN, ROW_W), BF16),
            pltpu.VMEM((2, CHUNKS_PER_BLK, ROW_ALIGN, ROW_W), BF16),
            pltpu.VMEM((D_MODEL, 2 * EXPERT_FF), BF16),
            pltpu.VMEM((EXPERT_FF, D_MODEL), BF16),
            pltpu.SemaphoreType.DMA((2,)),
            pltpu.SemaphoreType.DMA((2,)),
            pltpu.SMEM((2 * list_max,), jnp.int32),
            pltpu.SMEM((6,), jnp.int32),
        ],
    )
    return pl.pallas_call(
        functools.partial(_experts_body, nsub=nsub),
        grid_spec=grid_spec,
        out_shape=jax.ShapeDtypeStruct(staged.shape, staged.dtype),
        input_output_aliases={2: 0},
        compiler_params=_cparams(("arbitrary",)),
        name="moe_experts",
    )(start, cnt, staged, wg, wu, wd)


def _combine_body(start_ref, cnt_ref, ys_ref, pos_ref, x1_ref, p_ref, wp_ref, gp_ref, wpg_ref, gf_ref, y_ref,
                  acc_ref, *, sub_off):
    last_run = (pl.program_id(0) + sub_off) * N_EXPERTS + (N_EXPERTS - 1)
    used = (start_ref[last_run] + cnt_ref[last_run]) * ROW_ALIGN
    p1 = pos_ref[:, 0:1]
    p2 = pos_ref[:, 1:2]
    acc_ref[...] = x1_ref[...]
    for b in range(SUBP // PBLK):
        @pl.when(b * PBLK < used)
        def _(b=b):
            r = _iota_f32((SUB, PBLK), 1) + float(b * PBLK)
            onehot = jnp.where(r == p1, 1.0, jnp.where(r == p2, 1.0, 0.0)).astype(BF16)
            ys = ys_ref[b * PBLK // ROW_ALIGN:(b + 1) * PBLK // ROW_ALIGN].reshape(PBLK, D_MODEL)
            acc_ref[...] += jnp.dot(onehot, ys, preferred_element_type=F32)
    x2 = acc_ref[...]
    ple = _rms(jnp.dot(p_ref[...].astype(BF16), wp_ref[...], preferred_element_type=F32), gp_ref[...])
    gate = jax.nn.sigmoid(jnp.dot(x2.astype(BF16), wpg_ref[...], preferred_element_type=F32))
    y_ref[...] = _rms(x2 + ple * gate, gf_ref[...])


def _combine(start, cnt, ys, pos, x1, p, w_ple, gp, w_ple_gate, gf, sub_off):
    t = x1.shape[0]
    row = lambda i, *_: (i, 0)
    const = lambda i, *_: (0, 0)
    grid_spec = pltpu.PrefetchScalarGridSpec(
        num_scalar_prefetch=2,
        grid=(t // SUB,),
        in_specs=[
            pl.BlockSpec((CHUNKS_PER_SUB, ROW_ALIGN, D_MODEL), lambda i, *_: (i + sub_off, 0, 0)),
            pl.BlockSpec((SUB, LANES), lambda i, *_: (i + sub_off, 0)),
            pl.BlockSpec((SUB, D_MODEL), row),
            pl.BlockSpec((SUB, PLE_DIM), row),
            pl.BlockSpec((PLE_DIM, D_MODEL), const),
            pl.BlockSpec((1, D_MODEL), const),
            pl.BlockSpec((D_MODEL, D_MODEL), const),
            pl.BlockSpec((1, D_MODEL), const),
        ],
        out_specs=pl.BlockSpec((SUB, D_MODEL), row),
        scratch_shapes=[pltpu.VMEM((SUB, D_MODEL), F32)],
    )
    return pl.pallas_call(
        functools.partial(_combine_body, sub_off=sub_off),
        grid_spec=grid_spec,
        out_shape=jax.ShapeDtypeStruct((t, D_MODEL), F32),
        compiler_params=_cparams(("arbitrary",)),
        name="moe_combine_ple",
    )(start, cnt, ys, pos, x1, p, w_ple, gp, w_ple_gate, gf)


def _rope_tables(pos):
    half = HEAD_DIM // 2
    inv = ROPE_BASE ** (-jnp.arange(half, dtype=F32) / half)
    ang = pos[:, None] * inv[None, :]
    cos = jnp.cos(ang)
    sin = jnp.sin(ang)
    return jnp.concatenate([cos, cos], axis=-1), jnp.concatenate([-sin, sin], axis=-1)


def _router_params(we, be, wg, bg):
    pad = LANES - N_EXPERTS - N_GROUPS
    w = jnp.pad(jnp.concatenate([we, wg], axis=1), ((0, 0), (0, pad)))
    b = jnp.pad(jnp.concatenate([be, bg]), (0, pad))[None, :]
    return jnp.stack(_split3(w)[:2]), b


def kernel(x_prompt, x_sample, p_prompt, p_sample, state_conv, state_ret, w_in, conv_w, conv_b, conv_ln_g, conv_ln_b, w_out, norm1_g, norm2_g, router_group_w, router_group_b, router_expert_w, router_expert_b, w_expert_gate, w_expert_up, w_expert_down, w_ple, ple_norm_g, w_ple_gate, final_norm_g):
    assert w_in.shape[0] == 1, "single-layer trunk"
    nb, seq, _ = x_prompt.shape
    ns, dseq, _ = x_sample.shape
    tm = 512

    w_in_b = w_in[0].astype(BF16)
    w_out_b = w_out[0].astype(BF16)
    w_ple_b = w_ple[0].astype(BF16)
    w_pg_b = w_ple_gate[0].astype(BF16)
    g1 = norm1_g[0][None, :]
    g2 = norm2_g[0][None, :]
    gp = ple_norm_g[0][None, :]
    gf = final_norm_g[None, :]
    cb = conv_b[0][None, :]
    lng = conv_ln_g[0][None, :]
    lnb = conv_ln_b[0][None, :]
    wr2, br = _router_params(router_expert_w[0], router_expert_b[0], router_group_w[0], router_group_b[0])

    cos_p, sin_p = _rope_tables(jnp.arange(seq, dtype=F32) + jnp.float32(0))
    pos_s = jnp.tile(jnp.arange(dseq, dtype=F32) + jnp.float32(PAST_LEN), tm // dseq)
    cos_s, sin_s = _rope_tables(pos_s)

    xp = x_prompt.reshape(nb * seq, D_MODEL)
    u, q, k, v, gs = _inproj(xp, g1, w_in_b, cos_p, sin_p, tm, seq // tm, BF16)
    c, conv_p = _conv_prompt(u.reshape(nb, seq, CONV_CH), conv_w[0], cb, lng, lnb, tm)
    o, ret_p = _ret_prompt(q, k, v, gs, nb, seq, tm)
    x1_p, t_p, rec_p = _outproj(c.reshape(nb * seq, CONV_CH), o, xp, w_out_b, g2, wr2, br, tm)

    xs = x_sample.reshape(ns * dseq, D_MODEL)
    u, q, k, v, gs = _inproj(xs, g1, w_in_b, cos_s, sin_s, tm, 1, F32)
    c, conv_s = _conv_sample(u.reshape(ns, dseq, CONV_CH), state_conv[0], conv_w[0], cb, lng, lnb, 16)
    o, ret_s = _ret_sample(q, k, v, gs, state_ret[0], dseq, 8)
    x1_s, t_s, rec_s = _outproj(c.reshape(ns * dseq, CONV_CH), o, xs, w_out_b, g2, wr2, br, tm)

    staged, pos, meta = _dispatch(rec_p, t_p, rec_s, t_s)
    start = meta[:, 0, :N_EXPERTS].astype(jnp.int32).reshape(-1)
    cnt = meta[:, 1, :N_EXPERTS].astype(jnp.int32).reshape(-1)
    ys = _experts(start, cnt, staged, w_expert_gate[0], w_expert_up[0], w_expert_down[0])

    y_p = _combine(start, cnt, ys, pos, x1_p, p_prompt[0].reshape(nb * seq, PLE_DIM), w_ple_b, gp, w_pg_b, gf, 0)
    y_s = _combine(start, cnt, ys, pos, x1_s, p_sample[0].reshape(ns * dseq, PLE_DIM), w_ple_b, gp, w_pg_b, gf,
                   nb * seq // SUB)

    return (y_p.reshape(nb, seq, D_MODEL), y_s.reshape(ns, dseq, D_MODEL),
            conv_p[None], ret_p[None], conv_s[None], ret_s[None])
```

```python
import functools

import jax
import jax.numpy as jnp
from jax import lax
from jax.experimental import pallas as pl
from jax.experimental.pallas import tpu as pltpu

F32 = jnp.float32
BF16 = jnp.bfloat16

D_MODEL = 1024
PLE_DIM = 256
CONV_CH = 512
CONV_K = 31
RET_WIDTH = 512
RET_HEADS = 4
HEAD_DIM = 128
CHUNK = 128
ROPE_BASE = 10000.0
N_GROUPS = 4
EXPERTS_PER_GROUP = 8
N_EXPERTS = 32
EXPERT_FF = 256
IN_COLS = 3072
EPS = 1e-6
PAST_LEN = 16384

LANES = 128
SUBLANES = 8
HALO = 32
HALO_OFF = HALO - (CONV_K - 1)
VMEM_LIMIT = 48 * 1024 * 1024

SUB = 256
ROW_ALIGN = 16
PBLK = 256
SUBP = -(-(2 * SUB + N_EXPERTS * (ROW_ALIGN - 1)) // PBLK) * PBLK
CHUNKS_PER_SUB = SUBP // ROW_ALIGN
ROW_W = D_MODEL + LANES
MBLK = 512
CHUNKS_PER_BLK = MBLK // ROW_ALIGN
LIST_SLACK = 2


def _cparams(sem):
    return pltpu.CompilerParams(dimension_semantics=sem, vmem_limit_bytes=VMEM_LIMIT)


def _rms(x, g):
    return x * lax.rsqrt(jnp.mean(x * x, axis=-1, keepdims=True) + EPS) * g


def _inproj_body(x_ref, g1_ref, w_ref, cos_ref, sin_ref, u_ref, q_ref, k_ref, v_ref, gs_ref):
    h = _rms(x_ref[...], g1_ref[...]).astype(BF16)
    z = jnp.dot(h, w_ref[...], preferred_element_type=F32)
    a = z[:, :CONV_CH]
    b = z[:, CONV_CH:2 * CONV_CH]
    u_ref[...] = a * jax.nn.sigmoid(b)
    cos = cos_ref[...]
    sin = sin_ref[...]
    q0 = 2 * CONV_CH
    k0 = q0 + RET_WIDTH
    for hh in range(RET_HEADS):
        sl = slice(hh * HEAD_DIM, (hh + 1) * HEAD_DIM)
        qh = z[:, q0 + hh * HEAD_DIM:q0 + (hh + 1) * HEAD_DIM]
        kh = z[:, k0 + hh * HEAD_DIM:k0 + (hh + 1) * HEAD_DIM]
        q_ref[:, sl] = (qh * cos + pltpu.roll(qh, HEAD_DIM // 2, 1) * sin).astype(q_ref.dtype)
        kr = (kh * cos + pltpu.roll(kh, HEAD_DIM // 2, 1) * sin) * (HEAD_DIM ** -0.5)
        k_ref[:, sl] = kr.astype(k_ref.dtype)
    v_ref[...] = z[:, k0 + RET_WIDTH:k0 + 2 * RET_WIDTH].astype(v_ref.dtype)
    g = z[:, k0 + 2 * RET_WIDTH:]
    gs_ref[...] = g * jax.nn.sigmoid(g)


def _inproj(x, g1, w_in, cos, sin, tm, table_blocks, qkv_dtype):
    t = x.shape[0]
    row = lambda i: (i, 0)
    const = lambda i: (0, 0)
    tab = (lambda i: (i % table_blocks, 0)) if table_blocks > 1 else const
    return pl.pallas_call(
        _inproj_body,
        grid=(t // tm,),
        in_specs=[
            pl.BlockSpec((tm, D_MODEL), row),
            pl.BlockSpec((1, D_MODEL), const),
            pl.BlockSpec((D_MODEL, IN_COLS), const),
            pl.BlockSpec((tm, HEAD_DIM), tab),
            pl.BlockSpec((tm, HEAD_DIM), tab),
        ],
        out_specs=[pl.BlockSpec((tm, CONV_CH), row)] + [pl.BlockSpec((tm, RET_WIDTH), row)] * 4,
        out_shape=[
            jax.ShapeDtypeStruct((t, CONV_CH), F32),
            jax.ShapeDtypeStruct((t, RET_WIDTH), qkv_dtype),
            jax.ShapeDtypeStruct((t, RET_WIDTH), qkv_dtype),
            jax.ShapeDtypeStruct((t, RET_WIDTH), qkv_dtype),
            jax.ShapeDtypeStruct((t, RET_WIDTH), F32),
        ],
        compiler_params=_cparams(("parallel",)),
        name="inproj",
    )(x, g1, w_in, cos, sin)


def _ln_silu(acc, g, b):
    mu = jnp.mean(acc, axis=-1, keepdims=True)
    d = acc - mu
    var = jnp.mean(d * d, axis=-1, keepdims=True)
    y = d * lax.rsqrt(var + EPS) * g + b
    return y * jax.nn.sigmoid(y)


def _dwconv(load, w_ref, rows, time_axis):
    acc = None
    for b in range(SUBLANES):
        part = None
        for a in range((CONV_K + HALO_OFF) // SUBLANES + 1):
            k = SUBLANES * a + b - HALO_OFF
            if 0 <= k < CONV_K:
                term = load(SUBLANES * a, rows + SUBLANES) * w_ref[k:k + 1, :]
                part = term if part is None else part + term
        if part is not None:
            shifted = lax.slice_in_dim(part, b, b + rows, axis=time_axis)
            acc = shifted if acc is None else acc + shifted
    return acc


def _conv_prompt_body(u_ref, w_ref, cb_ref, lg_ref, lb_ref, c_ref, st_ref, ext_ref):
    j = pl.program_id(1)
    tl = u_ref.shape[1]

    @pl.when(j == 0)
    def _():
        ext_ref[0:HALO, :] = jnp.zeros((HALO, CONV_CH), F32)
        ext_ref[tl + HALO:, :] = jnp.zeros((SUBLANES, CONV_CH), F32)

    @pl.when(j > 0)
    def _():
        ext_ref[0:HALO, :] = ext_ref[tl:tl + HALO, :]

    ext_ref[HALO:tl + HALO, :] = u_ref[0]
    acc = _dwconv(lambda s, n: ext_ref[s:s + n, :], w_ref, tl, 0) + cb_ref[...]
    c_ref[0] = _ln_silu(acc, lg_ref[...], lb_ref[...]).astype(c_ref.dtype)
    st_ref[0] = ext_ref[tl + HALO_OFF:tl + HALO, :]


def _conv_prompt(u, conv_w, conv_b, ln_g, ln_b, tl):
    n, l, _ = u.shape
    const = lambda b, j: (0, 0)
    return pl.pallas_call(
        _conv_prompt_body,
        grid=(n, l // tl),
        in_specs=[
            pl.BlockSpec((1, tl, CONV_CH), lambda b, j: (b, j, 0)),
            pl.BlockSpec((CONV_K, CONV_CH), const),
            pl.BlockSpec((1, CONV_CH), const),
            pl.BlockSpec((1, CONV_CH), const),
            pl.BlockSpec((1, CONV_CH), const),
        ],
        out_specs=[
            pl.BlockSpec((1, tl, CONV_CH), lambda b, j: (b, j, 0)),
            pl.BlockSpec((1, CONV_K - 1, CONV_CH), lambda b, j: (b, 0, 0)),
        ],
        out_shape=[
            jax.ShapeDtypeStruct((n, l, CONV_CH), BF16),
            jax.ShapeDtypeStruct((n, CONV_K - 1, CONV_CH), F32),
        ],
        scratch_shapes=[pltpu.VMEM((tl + HALO + SUBLANES, CONV_CH), F32)],
        compiler_params=_cparams(("arbitrary", "arbitrary")),
        name="conv_prompt",
    )(u, conv_w, conv_b, ln_g, ln_b)


def _conv_sample_body(u_ref, st_ref, w_ref, cb_ref, lg_ref, lb_ref, c_ref, nst_ref, ext_ref):
    nb, l, _ = u_ref.shape
    ext_ref[:, 0:HALO_OFF, :] = jnp.zeros((nb, HALO_OFF, CONV_CH), F32)
    ext_ref[:, HALO_OFF:HALO, :] = st_ref[...]
    ext_ref[:, HALO:l + HALO, :] = u_ref[...]
    ext_ref[:, l + HALO:, :] = jnp.zeros((nb, SUBLANES, CONV_CH), F32)
    acc = _dwconv(lambda s, n: ext_ref[:, s:s + n, :], w_ref, l, 1) + cb_ref[...]
    c_ref[...] = _ln_silu(acc, lg_ref[...], lb_ref[...]).astype(c_ref.dtype)
    nst_ref[...] = ext_ref[:, l + HALO_OFF:l + HALO, :]


def _conv_sample(u, state, conv_w, conv_b, ln_g, ln_b, nb):
    n, l, _ = u.shape
    const = lambda b: (0, 0)
    blk3 = lambda b: (b, 0, 0)
    return pl.pallas_call(
        _conv_sample_body,
        grid=(n // nb,),
        in_specs=[
            pl.BlockSpec((nb, l, CONV_CH), blk3),
            pl.BlockSpec((nb, CONV_K - 1, CONV_CH), blk3),
            pl.BlockSpec((CONV_K, CONV_CH), const),
            pl.BlockSpec((1, CONV_CH), const),
            pl.BlockSpec((1, CONV_CH), const),
            pl.BlockSpec((1, CONV_CH), const),
        ],
        out_specs=[
            pl.BlockSpec((nb, l, CONV_CH), blk3),
            pl.BlockSpec((nb, CONV_K - 1, CONV_CH), blk3),
        ],
        out_shape=[
            jax.ShapeDtypeStruct((n, l, CONV_CH), BF16),
            jax.ShapeDtypeStruct((n, CONV_K - 1, CONV_CH), F32),
        ],
        scratch_shapes=[pltpu.VMEM((nb, l + HALO + SUBLANES, CONV_CH), F32)],
        compiler_params=_cparams(("parallel",)),
        name="conv_sample",
    )(u, state, conv_w, conv_b, ln_g, ln_b)


def _decay_tables(c):
    lg = jnp.log(1.0 - 2.0 ** (-5.0 - jnp.arange(RET_HEADS, dtype=F32)))
    idx = jnp.arange(c, dtype=F32)
    rel = idx[:, None] - idx[None, :]
    dmat = jnp.where(rel[None] >= 0, jnp.exp(jnp.maximum(rel, 0.0)[None] * lg[:, None, None]), 0.0)
    xi = jnp.exp((idx + 1.0)[None, :] * lg[:, None])
    zeta = jnp.exp((c - 1.0 - idx)[None, :] * lg[:, None])
    gc = jnp.exp(c * lg)
    xi_b = jnp.broadcast_to(xi[:, :, None], (RET_HEADS, c, HEAD_DIM))
    zeta_b = jnp.broadcast_to(zeta[:, :, None], (RET_HEADS, c, HEAD_DIM))
    gc_b = jnp.broadcast_to(gc[:, None, None], (RET_HEADS, 1, HEAD_DIM))
    return dmat, xi_b, zeta_b, gc_b


def _group_norm(o):
    mu = jnp.mean(o, axis=-1, keepdims=True)
    d = o - mu
    var = jnp.mean(d * d, axis=-1, keepdims=True)
    return d * lax.rsqrt(var + EPS)


def _ret_chunk(qh, kh, vh, r, dmat, xi, zeta, gc):
    qb = qh.astype(BF16)
    kb = kh.astype(BF16)
    vb = vh.astype(BF16)
    s = lax.dot_general(qb, kb, (((1,), (1,)), ((), ())), preferred_element_type=F32) * dmat
    o = jnp.dot(s.astype(BF16), vb, preferred_element_type=F32)
    o = o + jnp.dot(qb, r.astype(BF16), preferred_element_type=F32) * xi
    kz = (kh.astype(F32) * zeta).astype(BF16)
    r_new = r * gc + lax.dot_general(kz, vb, (((0,), (0,)), ((), ())), preferred_element_type=F32)
    return o, r_new


def _ret_prompt_body(q_ref, k_ref, v_ref, gs_ref, d_ref, xi_ref, zeta_ref, gc_ref, o_ref, st_ref, r_ref):
    j = pl.program_id(1)

    @pl.when(j == 0)
    def _():
        r_ref[...] = jnp.zeros_like(r_ref)

    n_chunks = q_ref.shape[0] // CHUNK
    for hh in range(RET_HEADS):
        sl = slice(hh * HEAD_DIM, (hh + 1) * HEAD_DIM)
        r = r_ref[hh]
        for ci in range(n_chunks):
            rows = slice(ci * CHUNK, (ci + 1) * CHUNK)
            o, r = _ret_chunk(q_ref[rows, sl], k_ref[rows, sl], v_ref[rows, sl], r,
                              d_ref[hh], xi_ref[hh], zeta_ref[hh], gc_ref[hh])
            o_ref[rows, sl] = (gs_ref[rows, sl] * _group_norm(o)).astype(o_ref.dtype)
        r_ref[hh] = r
    st_ref[0] = r_ref[...]


def _ret_prompt(q, k, v, gs, n, l, tl):
    dmat, xi, zeta, gc = _decay_tables(CHUNK)
    per = l // tl
    row = lambda b, j: (b * per + j, 0)
    c3 = lambda b, j: (0, 0, 0)
    return pl.pallas_call(
        _ret_prompt_body,
        grid=(n, per),
        in_specs=[pl.BlockSpec((tl, RET_WIDTH), row)] * 4 + [
            pl.BlockSpec((RET_HEADS, CHUNK, CHUNK), c3),
            pl.BlockSpec((RET_HEADS, CHUNK, HEAD_DIM), c3),
            pl.BlockSpec((RET_HEADS, CHUNK, HEAD_DIM), c3),
            pl.BlockSpec((RET_HEADS, 1, HEAD_DIM), c3),
        ],
        out_specs=[
            pl.BlockSpec((tl, RET_WIDTH), row),
            pl.BlockSpec((1, RET_HEADS, HEAD_DIM, HEAD_DIM), lambda b, j: (b, 0, 0, 0)),
        ],
        out_shape=[
            jax.ShapeDtypeStruct((n * l, RET_WIDTH), BF16),
            jax.ShapeDtypeStruct((n, RET_HEADS, HEAD_DIM, HEAD_DIM), F32),
        ],
        scratch_shapes=[pltpu.VMEM((RET_HEADS, HEAD_DIM, HEAD_DIM), F32)],
        compiler_params=_cparams(("arbitrary", "arbitrary")),
        name="ret_prompt",
    )(q, k, v, gs, dmat, xi, zeta, gc)


def _ret_sample_body(q_ref, k_ref, v_ref, gs_ref, st_ref, d_ref, xi_ref, zeta_ref, gc_ref, o_ref, nst_ref):
    nb = st_ref.shape[0]
    l = q_ref.shape[0] // nb
    for b in range(nb):
        rows = slice(b * l, (b + 1) * l)
        for hh in range(RET_HEADS):
            sl = slice(hh * HEAD_DIM, (hh + 1) * HEAD_DIM)
            o, r = _ret_chunk(q_ref[rows, sl], k_ref[rows, sl], v_ref[rows, sl], st_ref[b, hh],
                              d_ref[hh], xi_ref[hh], zeta_ref[hh], gc_ref[hh])
            o_ref[rows, sl] = (gs_ref[rows, sl] * _group_norm(o)).astype(o_ref.dtype)
            nst_ref[b, hh] = r


def _ret_sample(q, k, v, gs, state, l, nb):
    n = state.shape[0]
    dmat, xi, zeta, gc = _decay_tables(l)
    row = lambda b: (b, 0)
    c3 = lambda b: (0, 0, 0)
    blk4 = lambda b: (b, 0, 0, 0)
    return pl.pallas_call(
        _ret_sample_body,
        grid=(n // nb,),
        in_specs=[pl.BlockSpec((nb * l, RET_WIDTH), row)] * 4 + [
            pl.BlockSpec((nb, RET_HEADS, HEAD_DIM, HEAD_DIM), blk4),
            pl.BlockSpec((RET_HEADS, l, l), c3),
            pl.BlockSpec((RET_HEADS, l, HEAD_DIM), c3),
            pl.BlockSpec((RET_HEADS, l, HEAD_DIM), c3),
            pl.BlockSpec((RET_HEADS, 1, HEAD_DIM), c3),
        ],
        out_specs=[
            pl.BlockSpec((nb * l, RET_WIDTH), row),
            pl.BlockSpec((nb, RET_HEADS, HEAD_DIM, HEAD_DIM), blk4),
        ],
        out_shape=[
            jax.ShapeDtypeStruct((n * l, RET_WIDTH), BF16),
            jax.ShapeDtypeStruct((n, RET_HEADS, HEAD_DIM, HEAD_DIM), F32),
        ],
        compiler_params=_cparams(("parallel",)),
        name="ret_sample",
    )(q, k, v, gs, state, dmat, xi, zeta, gc)


def _split3(x):
    hi = x.astype(BF16)
    r1 = x - hi.astype(F32)
    mid = r1.astype(BF16)
    lo = (r1 - mid.astype(F32)).astype(BF16)
    return hi, mid, lo


def _dot_hp(t, w_hi, w_mid):
    t_hi, t_mid, _ = _split3(t)
    d = functools.partial(jnp.dot, preferred_element_type=F32)
    return d(t_hi, w_hi) + (d(t_mid, w_hi) + d(t_hi, w_mid))


def _route(logits):
    lane = lax.broadcasted_iota(jnp.int32, logits.shape, 1)
    big = jnp.int32(LANES)
    is_group = (lane >= N_EXPERTS) & (lane < N_EXPERTS + N_GROUPS)
    lg = jnp.where(is_group, logits, -1e30)
    m = jnp.max(lg, axis=-1, keepdims=True)
    g_top = 1.0 / jnp.sum(jnp.exp(lg - m), axis=-1, keepdims=True)
    g_idx = jnp.min(jnp.where(lg == m, lane, big), axis=-1, keepdims=True) - N_EXPERTS
    in_group = (lane >= g_idx * EXPERTS_PER_GROUP) & (lane < (g_idx + 1) * EXPERTS_PER_GROUP)
    lem = jnp.where(in_group, logits, -1e30)
    m2 = jnp.max(lem, axis=-1, keepdims=True)
    pe = jnp.where(in_group, jnp.exp(lem - m2), 0.0)
    p1 = jnp.max(pe, axis=-1, keepdims=True)
    e1 = jnp.min(jnp.where(in_group & (pe == p1), lane, big), axis=-1, keepdims=True)
    rest = in_group & (lane != e1)
    pe2 = jnp.where(rest, pe, -1.0)
    p2 = jnp.max(pe2, axis=-1, keepdims=True)
    e2 = jnp.min(jnp.where(rest & (pe2 == p2), lane, big), axis=-1, keepdims=True)
    scale = g_top / (p1 + p2)
    rec = jnp.where(lane == 0, e1.astype(F32), jnp.where(lane == 1, e2.astype(F32), 0.0))
    return rec + jnp.where(lane == 2, p1 * scale, jnp.where(lane == 3, p2 * scale, 0.0))


def _outproj_body(c_ref, o_ref, x_ref, wo_ref, g2_ref, wr_ref, br_ref, x1_ref, t_ref, rec_ref):
    x1 = x_ref[...] + jnp.dot(c_ref[...], wo_ref[0:CONV_CH, :], preferred_element_type=F32)
    x1 = x1 + jnp.dot(o_ref[...], wo_ref[CONV_CH:, :], preferred_element_type=F32)
    x1_ref[...] = x1
    t = _rms(x1, g2_ref[...])
    t_ref[...] = t.astype(t_ref.dtype)
    rec_ref[...] = _route(_dot_hp(t, wr_ref[0], wr_ref[1]) + br_ref[...])


def _outproj(c, o, x, w_out, g2, wr2, br, tm):
    t = x.shape[0]
    row = lambda i: (i, 0)
    const = lambda i: (0, 0)
    c3 = lambda i: (0, 0, 0)
    return pl.pallas_call(
        _outproj_body,
        grid=(t // tm,),
        in_specs=[
            pl.BlockSpec((tm, CONV_CH), row),
            pl.BlockSpec((tm, RET_WIDTH), row),
            pl.BlockSpec((tm, D_MODEL), row),
            pl.BlockSpec((D_MODEL, D_MODEL), const),
            pl.BlockSpec((1, D_MODEL), const),
            pl.BlockSpec((2, D_MODEL, LANES), c3),
            pl.BlockSpec((1, LANES), const),
        ],
        out_specs=[
            pl.BlockSpec((tm, D_MODEL), row),
            pl.BlockSpec((tm, D_MODEL), row),
            pl.BlockSpec((tm, LANES), row),
        ],
        out_shape=[
            jax.ShapeDtypeStruct((t, D_MODEL), F32),
            jax.ShapeDtypeStruct((t, D_MODEL), BF16),
            jax.ShapeDtypeStruct((t, LANES), F32),
        ],
        compiler_params=_cparams(("parallel",)),
        name="outproj_router",
    )(c, o, x, w_out, g2, wr2, br)


def _iota_f32(shape, dim):
    return lax.broadcasted_iota(jnp.int32, shape, dim).astype(F32)


def _dispatch_body(rec_a_ref, t_a_ref, rec_b_ref, t_b_ref, s_ref, pos_ref, meta_ref, *, nsub_a):
    from_a = pl.program_id(0) < nsub_a
    rec = jnp.where(from_a, rec_a_ref[...], rec_b_ref[...])
    tok = jnp.where(from_a, t_a_ref[...], t_b_ref[...])
    lane = _iota_f32(rec.shape, 1)
    a1 = lane == rec[:, 0:1]
    a2 = lane == rec[:, 1:2]
    a1f = jnp.where(a1, 1.0, 0.0)
    a2f = jnp.where(a2, 1.0, 0.0)
    ltri = jnp.where(_iota_f32((SUB, SUB), 1) < _iota_f32((SUB, SUB), 0), 1.0, 0.0).astype(BF16)
    c1 = jnp.dot(ltri, a1f.astype(BF16), preferred_element_type=F32)
    c2 = jnp.dot(ltri, a2f.astype(BF16), preferred_element_type=F32)
    n1 = jnp.sum(a1f, axis=0, keepdims=True)
    n2 = jnp.sum(a2f, axis=0, keepdims=True)
    cnt = jnp.floor((n1 + n2 + (ROW_ALIGN - 1.0)) * (1.0 / ROW_ALIGN))
    utri = jnp.where(_iota_f32((LANES, LANES), 0) < _iota_f32((LANES, LANES), 1), 1.0, 0.0).astype(BF16)
    start = jnp.dot(jnp.broadcast_to(cnt, (SUBLANES, LANES)).astype(BF16), utri,
                    preferred_element_type=F32)[0:1]
    base1 = start * ROW_ALIGN
    base2 = base1 + n1
    pos1 = jnp.sum(jnp.where(a1, c1 + base1, 0.0), axis=1, keepdims=True)
    pos2 = jnp.sum(jnp.where(a2, c2 + base2, 0.0), axis=1, keepdims=True)
    posm = jnp.where(lane == 0.0, pos1, jnp.where(lane == 1.0, pos2, 0.0))
    pos_ref[...] = posm
    row = _iota_f32((SUBLANES, LANES), 0)
    meta_ref[0] = jnp.where(row == 0.0, start, jnp.where(row == 1.0, cnt, 0.0))

    g1 = _split3(rec[:, 2:3])
    g2 = _split3(rec[:, 3:4])
    info = jnp.where(lane == 6.0, rec[:, 0:1], jnp.where(lane == 7.0, rec[:, 1:2], 0.0))
    for i in range(3):
        info = jnp.where(lane == float(i), g1[i].astype(F32), info)
        info = jnp.where(lane == float(3 + i), g2[i].astype(F32), info)
    src = jnp.concatenate([tok, info.astype(BF16)], axis=1)

    post = posm.T
    p1 = post[0:1, :]
    p2 = post[1:2, :]
    used = jnp.sum(cnt) * ROW_ALIGN
    for b in range(SUBP // PBLK):
        chunks = slice(b * PBLK // ROW_ALIGN, (b + 1) * PBLK // ROW_ALIGN)

        @pl.when(b * PBLK < used)
        def _(b=b, chunks=chunks):
            r = _iota_f32((PBLK, SUB), 0) + float(b * PBLK)
            onehot = jnp.where(r == p1, 1.0, jnp.where(r == p2, 1.0, 0.0)).astype(BF16)
            sorted_rows = jnp.dot(onehot, src, preferred_element_type=F32).astype(BF16)
            s_ref[chunks] = sorted_rows.reshape(PBLK // ROW_ALIGN, ROW_ALIGN, ROW_W)

        @pl.when(b * PBLK >= used)
        def _(chunks=chunks):
            s_ref[chunks] = jnp.zeros((PBLK // ROW_ALIGN, ROW_ALIGN, ROW_W), BF16)


def _dispatch(rec_a, t_a, rec_b, t_b):
    nsub_a = rec_a.shape[0] // SUB
    nsub_b = rec_b.shape[0] // SUB
    nsub = nsub_a + nsub_b
    row = lambda i: (i, 0)
    row_a = lambda i: (jnp.minimum(i, nsub_a - 1), 0)
    row_b = lambda i: (jnp.maximum(i - nsub_a, 0), 0)
    return pl.pallas_call(
        functools.partial(_dispatch_body, nsub_a=nsub_a),
        grid=(nsub,),
        in_specs=[
            pl.BlockSpec((SUB, LANES), row_a),
            pl.BlockSpec((SUB, D_MODEL), row_a),
            pl.BlockSpec((SUB, LANES), row_b),
            pl.BlockSpec((SUB, D_MODEL), row_b),
        ],
        out_specs=[
            pl.BlockSpec((CHUNKS_PER_SUB, ROW_ALIGN, ROW_W), lambda i: (i, 0, 0)),
            pl.BlockSpec((SUB, LANES), row),
            pl.BlockSpec((1, SUBLANES, LANES), lambda i: (i, 0, 0)),
        ],
        out_shape=[
            jax.ShapeDtypeStruct((nsub * CHUNKS_PER_SUB, ROW_ALIGN, ROW_W), BF16),
            jax.ShapeDtypeStruct((nsub * SUB, LANES), F32),
            jax.ShapeDtypeStruct((nsub, SUBLANES, LANES), F32),
        ],
        compiler_params=_cparams(("parallel",)),
        name="moe_dispatch",
    )(rec_a, t_a, rec_b, t_b)


def _experts_body(start_ref, cnt_ref, s_in, wg_ref, wu_ref, wd_ref, s_hbm,
                  xbuf, ybuf, wgu_ref, wdb_ref, gsem, ssem, list_ref, state_ref, *, nsub):
    del s_in
    e = pl.program_id(0)
    ne = pl.num_programs(0)
    par = e & 1

    list_max = list_ref.shape[0] // 2

    def gather_copy(src, i, slot):
        return pltpu.make_async_copy(s_hbm.at[src], xbuf.at[slot, i], gsem.at[slot])

    def scatter_copy(dst, i, slot):
        return pltpu.make_async_copy(ybuf.at[slot, i], s_hbm.at[dst], ssem.at[slot])

    def build_list(x, which):
        def per_sub(s, k):
            run = s * N_EXPERTS + x
            c = cnt_ref[run]
            base = s * CHUNKS_PER_SUB + start_ref[run]
            list_ref[k] = base
            list_ref[k + 1] = base + 1

            def per_chunk(i, carry):
                list_ref[k + i] = base + i
                return carry
            lax.fori_loop(2, c, per_chunk, 0)
            return k + c
        first = which * list_max
        state_ref[which] = lax.fori_loop(0, nsub, per_sub, first) - first

    def start_all(copy, which, first, n, slot, counter):
        def body(i, carry):
            copy(list_ref[which * list_max + first + i], i, slot).start()
            return carry
        lax.fori_loop(0, n, body, 0)
        state_ref[counter] = n

    def wait_all(copy, block_copy, slot, counter):
        n = state_ref[counter]

        @pl.when(n == CHUNKS_PER_BLK)
        def _():
            block_copy(slot).wait()

        @pl.when(n < CHUNKS_PER_BLK)
        def _():
            def body(i, carry):
                copy(0, 0, slot).wait()
                return carry
            lax.fori_loop(0, n, body, 0)
        state_ref[counter] = 0

    def gather_block(slot):
        return pltpu.make_async_copy(s_hbm.at[pl.ds(0, CHUNKS_PER_BLK)], xbuf.at[slot], gsem.at[slot])

    def scatter_block(slot):
        return pltpu.make_async_copy(ybuf.at[slot], s_hbm.at[pl.ds(0, CHUNKS_PER_BLK)], ssem.at[slot])

    def block_chunks(total, b):
        return jnp.minimum(total - b * CHUNKS_PER_BLK, CHUNKS_PER_BLK)

    @pl.when(e == 0)
    def _():
        for i in range(6):
            state_ref[i] = 0
        xbuf[...] = jnp.zeros_like(xbuf)
        build_list(0, 0)
        n0 = state_ref[0]

        @pl.when(n0 > 0)
        def _():
            start_all(gather_copy, 0, 0, block_chunks(n0, 0), 0, 2)

    @pl.when(e + 1 < ne)
    def _():
        build_list(e + 1, 1 - par)

    total = state_ref[par]
    nblk = (total + CHUNKS_PER_BLK - 1) // CHUNKS_PER_BLK
    wgu_ref[:, 0:EXPERT_FF] = wg_ref[0].astype(BF16)
    wgu_ref[:, EXPERT_FF:] = wu_ref[0].astype(BF16)
    wdb_ref[...] = wd_ref[0].astype(BF16)
    ef = e.astype(F32)

    def block(b, carry):
        slot = b & 1
        first = b * CHUNKS_PER_BLK

        @pl.when(b + 1 < nblk)
        def _():
            start_all(gather_copy, par, first + CHUNKS_PER_BLK, block_chunks(total, b + 1), 1 - slot, 3 - slot)

        wait_all(gather_copy, gather_block, slot, 2 + slot)
        wait_all(scatter_copy, scatter_block, slot, 4 + slot)
        x = xbuf[slot].reshape(MBLK, ROW_W)
        info = x[:, D_MODEL:].astype(F32)
        g_first = info[:, 0:1] + info[:, 1:2] + info[:, 2:3]
        g_second = info[:, 3:4] + info[:, 4:5] + info[:, 5:6]
        gate = jnp.where(info[:, 6:7] == ef, g_first, g_second)
        h = jnp.dot(x[:, :D_MODEL], wgu_ref[...], preferred_element_type=F32)
        h1 = h[:, :EXPERT_FF]
        hid = (h1 * jax.nn.sigmoid(h1)) * h[:, EXPERT_FF:] * gate
        y = jnp.dot(hid.astype(BF16), wdb_ref[...], preferred_element_type=F32).astype(BF16)
        ybuf[slot] = jnp.concatenate([y, x[:, D_MODEL:]], axis=1).reshape(CHUNKS_PER_BLK, ROW_ALIGN, ROW_W)
        start_all(scatter_copy, par, first, block_chunks(total, b), slot, 4 + slot)
        return carry

    lax.fori_loop(0, nblk, block, 0)

    @pl.when(e + 1 < ne)
    def _():
        n1 = state_ref[1 - par]

        @pl.when(n1 > 0)
        def _():
            start_all(gather_copy, 1 - par, 0, block_chunks(n1, 0), 0, 2)

    @pl.when(e == ne - 1)
    def _():
        wait_all(scatter_copy, scatter_block, 0, 4)
        wait_all(scatter_copy, scatter_block, 1, 5)


def _experts(start, cnt, staged, wg, wu, wd):
    nsub = staged.shape[0] // CHUNKS_PER_SUB
    list_max = nsub * SUB // ROW_ALIGN + nsub + LIST_SLACK
    wblk = lambda e, *_: (e, 0, 0)
    grid_spec = pltpu.PrefetchScalarGridSpec(
        num_scalar_prefetch=2,
        grid=(N_EXPERTS,),
        in_specs=[
            pl.BlockSpec(memory_space=pl.ANY),
            pl.BlockSpec((1, D_MODEL, EXPERT_FF), wblk),
            pl.BlockSpec((1, D_MODEL, EXPERT_FF), wblk),
            pl.BlockSpec((1, EXPERT_FF, D_MODEL), wblk),
        ],
        out_specs=pl.BlockSpec(memory_space=pl.ANY),
        scratch_shapes=[
            pltpu.VMEM((2, CHUNKS_PER_BLK, ROW_ALIGN, ROW_W), BF16),
            pltpu.VMEM((2, CHUNKS_PER_BLK, ROW_ALIGN, ROW_W), BF16),
            pltpu.VMEM((D_MODEL, 2 * EXPERT_FF), BF16),
            pltpu.VMEM((EXPERT_FF, D_MODEL), BF16),
            pltpu.SemaphoreType.DMA((2,)),
            pltpu.SemaphoreType.DMA((2,)),
            pltpu.SMEM((2 * list_max,), jnp.int32),
            pltpu.SMEM((6,), jnp.int32),
        ],
    )
    return pl.pallas_call(
        functools.partial(_experts_body, nsub=nsub),
        grid_spec=grid_spec,
        out_shape=jax.ShapeDtypeStruct(staged.shape, staged.dtype),
        input_output_aliases={2: 0},
        compiler_params=_cparams(("arbitrary",)),
        name="moe_experts",
    )(start, cnt, staged, wg, wu, wd)


def _combine_body(ys_ref, pos_ref, x1_ref, p_ref, wp_ref, gp_ref, wpg_ref, gf_ref, y_ref):
    p1 = pos_ref[:, 0:1]
    p2 = pos_ref[:, 1:2]
    r = _iota_f32((SUB, SUBP), 1)
    onehot = jnp.where(r == p1, 1.0, jnp.where(r == p2, 1.0, 0.0)).astype(BF16)
    ys = ys_ref[...].reshape(SUBP, D_MODEL)
    x2 = x1_ref[...] + jnp.dot(onehot, ys, preferred_element_type=F32)
    ple = _rms(jnp.dot(p_ref[...].astype(BF16), wp_ref[...], preferred_element_type=F32), gp_ref[...])
    gate = jax.nn.sigmoid(jnp.dot(x2.astype(BF16), wpg_ref[...], preferred_element_type=F32))
    y_ref[...] = _rms(x2 + ple * gate, gf_ref[...])


def _combine(ys, pos, x1, p, w_ple, gp, w_ple_gate, gf, sub_off):
    t = x1.shape[0]
    row = lambda i: (i, 0)
    const = lambda i: (0, 0)
    return pl.pallas_call(
        _combine_body,
        grid=(t // SUB,),
        in_specs=[
            pl.BlockSpec((CHUNKS_PER_SUB, ROW_ALIGN, D_MODEL), lambda i: (i + sub_off, 0, 0)),
            pl.BlockSpec((SUB, LANES), lambda i: (i + sub_off, 0)),
            pl.BlockSpec((SUB, D_MODEL), row),
            pl.BlockSpec((SUB, PLE_DIM), row),
            pl.BlockSpec((PLE_DIM, D_MODEL), const),
            pl.BlockSpec((1, D_MODEL), const),
            pl.BlockSpec((D_MODEL, D_MODEL), const),
            pl.BlockSpec((1, D_MODEL), const),
        ],
        out_specs=pl.BlockSpec((SUB, D_MODEL), row),
        out_shape=jax.ShapeDtypeStruct((t, D_MODEL), F32),
        compiler_params=_cparams(("parallel",)),
        name="moe_combine_ple",
    )(ys, pos, x1, p, w_ple, gp, w_ple_gate, gf)


def _rope_tables(pos):
    half = HEAD_DIM // 2
    inv = ROPE_BASE ** (-jnp.arange(half, dtype=F32) / half)
    ang = pos[:, None] * inv[None, :]
    cos = jnp.cos(ang)
    sin = jnp.sin(ang)
    return jnp.concatenate([cos, cos], axis=-1), jnp.concatenate([-sin, sin], axis=-1)


def _router_params(we, be, wg, bg):
    pad = LANES - N_EXPERTS - N_GROUPS
    w = jnp.pad(jnp.concatenate([we, wg], axis=1), ((0, 0), (0, pad)))
    b = jnp.pad(jnp.concatenate([be, bg]), (0, pad))[None, :]
    return jnp.stack(_split3(w)[:2]), b


def kernel(x_prompt, x_sample, p_prompt, p_sample, state_conv, state_ret, w_in, conv_w, conv_b, conv_ln_g, conv_ln_b, w_out, norm1_g, norm2_g, router_group_w, router_group_b, router_expert_w, router_expert_b, w_expert_gate, w_expert_up, w_expert_down, w_ple, ple_norm_g, w_ple_gate, final_norm_g):
    assert w_in.shape[0] == 1, "single-layer trunk"
    nb, seq, _ = x_prompt.shape
    ns, dseq, _ = x_sample.shape
    tm = 512

    w_in_b = w_in[0].astype(BF16)
    w_out_b = w_out[0].astype(BF16)
    w_ple_b = w_ple[0].astype(BF16)
    w_pg_b = w_ple_gate[0].astype(BF16)
    g1 = norm1_g[0][None, :]
    g2 = norm2_g[0][None, :]
    gp = ple_norm_g[0][None, :]
    gf = final_norm_g[None, :]
    cb = conv_b[0][None, :]
    lng = conv_ln_g[0][None, :]
    lnb = conv_ln_b[0][None, :]
    wr2, br = _router_params(router_expert_w[0], router_expert_b[0], router_group_w[0], router_group_b[0])

    cos_p, sin_p = _rope_tables(jnp.arange(seq, dtype=F32) + jnp.float32(0))
    pos_s = jnp.tile(jnp.arange(dseq, dtype=F32) + jnp.float32(PAST_LEN), tm // dseq)
    cos_s, sin_s = _rope_tables(pos_s)

    xp = x_prompt.reshape(nb * seq, D_MODEL)
    u, q, k, v, gs = _inproj(xp, g1, w_in_b, cos_p, sin_p, tm, seq // tm, BF16)
    c, conv_p = _conv_prompt(u.reshape(nb, seq, CONV_CH), conv_w[0], cb, lng, lnb, tm)
    o, ret_p = _ret_prompt(q, k, v, gs, nb, seq, tm)
    x1_p, t_p, rec_p = _outproj(c.reshape(nb * seq, CONV_CH), o, xp, w_out_b, g2, wr2, br, tm)

    xs = x_sample.reshape(ns * dseq, D_MODEL)
    u, q, k, v, gs = _inproj(xs, g1, w_in_b, cos_s, sin_s, tm, 1, F32)
    c, conv_s = _conv_sample(u.reshape(ns, dseq, CONV_CH), state_conv[0], conv_w[0], cb, lng, lnb, 16)
    o, ret_s = _ret_sample(q, k, v, gs, state_ret[0], dseq, 8)
    x1_s, t_s, rec_s = _outproj(c.reshape(ns * dseq, CONV_CH), o, xs, w_out_b, g2, wr2, br, tm)

    staged, pos, meta = _dispatch(rec_p, t_p, rec_s, t_s)
    start = meta[:, 0, :N_EXPERTS].astype(jnp.int32).reshape(-1)
    cnt = meta[:, 1, :N_EXPERTS].astype(jnp.int32).reshape(-1)
    ys = _experts(start, cnt, staged, w_expert_gate[0], w_expert_up[0], w_expert_down[0])

    y_p = _combine(ys, pos, x1_p, p_prompt[0].reshape(nb * seq, PLE_DIM), w_ple_b, gp, w_pg_b, gf, 0)
    y_s = _combine(ys, pos, x1_s, p_sample[0].reshape(ns * dseq, PLE_DIM), w_ple_b, gp, w_pg_b, gf,
                   nb * seq // SUB)

    return (y_p.reshape(nb, seq, D_MODEL), y_s.reshape(ns, dseq, D_MODEL),
            conv_p[None], ret_p[None], conv_s[None], ret_s[None])
```

```python
import functools

import jax
import jax.numpy as jnp
from jax import lax
from jax.experimental import pallas as pl
from jax.experimental.pallas import tpu as pltpu

F32 = jnp.float32
BF16 = jnp.bfloat16

D_MODEL = 1024
PLE_DIM = 256
CONV_CH = 512
CONV_K = 31
RET_WIDTH = 512
RET_HEADS = 4
HEAD_DIM = 128
CHUNK = 128
ROPE_BASE = 10000.0
N_GROUPS = 4
EXPERTS_PER_GROUP = 8
N_EXPERTS = 32
EXPERT_FF = 256
IN_COLS = 3072
EPS = 1e-6
PAST_LEN = 16384

LANES = 128
SUBLANES = 8
HALO = 32
HALO_OFF = HALO - (CONV_K - 1)
VMEM_LIMIT = 48 * 1024 * 1024
MIX_VMEM_LIMIT = 56 * 1024 * 1024

SUB = 256
ROW_ALIGN = 16
PBLK = 256
SUBP = -(-(2 * SUB + N_EXPERTS * (ROW_ALIGN - 1)) // PBLK) * PBLK
CHUNKS_PER_SUB = SUBP // ROW_ALIGN
ROW_W = D_MODEL + LANES
MBLK = 512
CHUNKS_PER_BLK = MBLK // ROW_ALIGN
LIST_SLACK = 2


def _cparams(sem):
    return pltpu.CompilerParams(dimension_semantics=sem, vmem_limit_bytes=VMEM_LIMIT)


def _rms(x, g):
    return x * lax.rsqrt(jnp.mean(x * x, axis=-1, keepdims=True) + EPS) * g


def _inproj_body(x_ref, g1_ref, w_ref, cos_ref, sin_ref, u_ref, q_ref, k_ref, v_ref, gs_ref):
    h = _rms(x_ref[...], g1_ref[...]).astype(BF16)
    z = jnp.dot(h, w_ref[...], preferred_element_type=F32)
    a = z[:, :CONV_CH]
    b = z[:, CONV_CH:2 * CONV_CH]
    u_ref[...] = a * jax.nn.sigmoid(b)
    cos = cos_ref[...]
    sin = sin_ref[...]
    q0 = 2 * CONV_CH
    k0 = q0 + RET_WIDTH
    for hh in range(RET_HEADS):
        sl = slice(hh * HEAD_DIM, (hh + 1) * HEAD_DIM)
        qh = z[:, q0 + hh * HEAD_DIM:q0 + (hh + 1) * HEAD_DIM]
        kh = z[:, k0 + hh * HEAD_DIM:k0 + (hh + 1) * HEAD_DIM]
        q_ref[:, sl] = (qh * cos + pltpu.roll(qh, HEAD_DIM // 2, 1) * sin).astype(q_ref.dtype)
        kr = (kh * cos + pltpu.roll(kh, HEAD_DIM // 2, 1) * sin) * (HEAD_DIM ** -0.5)
        k_ref[:, sl] = kr.astype(k_ref.dtype)
    v_ref[...] = z[:, k0 + RET_WIDTH:k0 + 2 * RET_WIDTH].astype(v_ref.dtype)
    g = z[:, k0 + 2 * RET_WIDTH:]
    gs_ref[...] = g * jax.nn.sigmoid(g)


def _inproj(x, g1, w_in, cos, sin, tm, table_blocks, qkv_dtype):
    t = x.shape[0]
    row = lambda i: (i, 0)
    const = lambda i: (0, 0)
    tab = (lambda i: (i % table_blocks, 0)) if table_blocks > 1 else const
    return pl.pallas_call(
        _inproj_body,
        grid=(t // tm,),
        in_specs=[
            pl.BlockSpec((tm, D_MODEL), row),
            pl.BlockSpec((1, D_MODEL), const),
            pl.BlockSpec((D_MODEL, IN_COLS), const),
            pl.BlockSpec((tm, HEAD_DIM), tab),
            pl.BlockSpec((tm, HEAD_DIM), tab),
        ],
        out_specs=[pl.BlockSpec((tm, CONV_CH), row)] + [pl.BlockSpec((tm, RET_WIDTH), row)] * 4,
        out_shape=[
            jax.ShapeDtypeStruct((t, CONV_CH), F32),
            jax.ShapeDtypeStruct((t, RET_WIDTH), qkv_dtype),
            jax.ShapeDtypeStruct((t, RET_WIDTH), qkv_dtype),
            jax.ShapeDtypeStruct((t, RET_WIDTH), qkv_dtype),
            jax.ShapeDtypeStruct((t, RET_WIDTH), F32),
        ],
        compiler_params=_cparams(("parallel",)),
        name="inproj",
    )(x, g1, w_in, cos, sin)


def _ln_silu(acc, g, b):
    mu = jnp.mean(acc, axis=-1, keepdims=True)
    d = acc - mu
    var = jnp.mean(d * d, axis=-1, keepdims=True)
    y = d * lax.rsqrt(var + EPS) * g + b
    return y * jax.nn.sigmoid(y)


def _dwconv(load, w_ref, rows, time_axis):
    acc = None
    for b in range(SUBLANES):
        part = None
        for a in range((CONV_K + HALO_OFF) // SUBLANES + 1):
            k = SUBLANES * a + b - HALO_OFF
            if 0 <= k < CONV_K:
                term = load(SUBLANES * a, rows + SUBLANES) * w_ref[k:k + 1, :]
                part = term if part is None else part + term
        if part is not None:
            shifted = lax.slice_in_dim(part, b, b + rows, axis=time_axis)
            acc = shifted if acc is None else acc + shifted
    return acc


def _conv_prompt_body(u_ref, w_ref, cb_ref, lg_ref, lb_ref, c_ref, st_ref, ext_ref):
    j = pl.program_id(1)
    tl = u_ref.shape[1]

    @pl.when(j == 0)
    def _():
        ext_ref[0:HALO, :] = jnp.zeros((HALO, CONV_CH), F32)
        ext_ref[tl + HALO:, :] = jnp.zeros((SUBLANES, CONV_CH), F32)

    @pl.when(j > 0)
    def _():
        ext_ref[0:HALO, :] = ext_ref[tl:tl + HALO, :]

    ext_ref[HALO:tl + HALO, :] = u_ref[0]
    acc = _dwconv(lambda s, n: ext_ref[s:s + n, :], w_ref, tl, 0) + cb_ref[...]
    c_ref[0] = _ln_silu(acc, lg_ref[...], lb_ref[...]).astype(c_ref.dtype)
    st_ref[0] = ext_ref[tl + HALO_OFF:tl + HALO, :]


def _conv_prompt(u, conv_w, conv_b, ln_g, ln_b, tl):
    n, l, _ = u.shape
    const = lambda b, j: (0, 0)
    return pl.pallas_call(
        _conv_prompt_body,
        grid=(n, l // tl),
        in_specs=[
            pl.BlockSpec((1, tl, CONV_CH), lambda b, j: (b, j, 0)),
            pl.BlockSpec((CONV_K, CONV_CH), const),
            pl.BlockSpec((1, CONV_CH), const),
            pl.BlockSpec((1, CONV_CH), const),
            pl.BlockSpec((1, CONV_CH), const),
        ],
        out_specs=[
            pl.BlockSpec((1, tl, CONV_CH), lambda b, j: (b, j, 0)),
            pl.BlockSpec((1, CONV_K - 1, CONV_CH), lambda b, j: (b, 0, 0)),
        ],
        out_shape=[
            jax.ShapeDtypeStruct((n, l, CONV_CH), BF16),
            jax.ShapeDtypeStruct((n, CONV_K - 1, CONV_CH), F32),
        ],
        scratch_shapes=[pltpu.VMEM((tl + HALO + SUBLANES, CONV_CH), F32)],
        compiler_params=_cparams(("arbitrary", "arbitrary")),
        name="conv_prompt",
    )(u, conv_w, conv_b, ln_g, ln_b)


def _conv_sample_body(u_ref, st_ref, w_ref, cb_ref, lg_ref, lb_ref, c_ref, nst_ref, ext_ref):
    nb, l, _ = u_ref.shape
    ext_ref[:, 0:HALO_OFF, :] = jnp.zeros((nb, HALO_OFF, CONV_CH), F32)
    ext_ref[:, HALO_OFF:HALO, :] = st_ref[...]
    ext_ref[:, HALO:l + HALO, :] = u_ref[...]
    ext_ref[:, l + HALO:, :] = jnp.zeros((nb, SUBLANES, CONV_CH), F32)
    acc = _dwconv(lambda s, n: ext_ref[:, s:s + n, :], w_ref, l, 1) + cb_ref[...]
    c_ref[...] = _ln_silu(acc, lg_ref[...], lb_ref[...]).astype(c_ref.dtype)
    nst_ref[...] = ext_ref[:, l + HALO_OFF:l + HALO, :]


def _conv_sample(u, state, conv_w, conv_b, ln_g, ln_b, nb):
    n, l, _ = u.shape
    const = lambda b: (0, 0)
    blk3 = lambda b: (b, 0, 0)
    return pl.pallas_call(
        _conv_sample_body,
        grid=(n // nb,),
        in_specs=[
            pl.BlockSpec((nb, l, CONV_CH), blk3),
            pl.BlockSpec((nb, CONV_K - 1, CONV_CH), blk3),
            pl.BlockSpec((CONV_K, CONV_CH), const),
            pl.BlockSpec((1, CONV_CH), const),
            pl.BlockSpec((1, CONV_CH), const),
            pl.BlockSpec((1, CONV_CH), const),
        ],
        out_specs=[
            pl.BlockSpec((nb, l, CONV_CH), blk3),
            pl.BlockSpec((nb, CONV_K - 1, CONV_CH), blk3),
        ],
        out_shape=[
            jax.ShapeDtypeStruct((n, l, CONV_CH), BF16),
            jax.ShapeDtypeStruct((n, CONV_K - 1, CONV_CH), F32),
        ],
        scratch_shapes=[pltpu.VMEM((nb, l + HALO + SUBLANES, CONV_CH), F32)],
        compiler_params=_cparams(("parallel",)),
        name="conv_sample",
    )(u, state, conv_w, conv_b, ln_g, ln_b)


def _decay_tables(c):
    lg = jnp.log(1.0 - 2.0 ** (-5.0 - jnp.arange(RET_HEADS, dtype=F32)))
    idx = jnp.arange(c, dtype=F32)
    rel = idx[:, None] - idx[None, :]
    dmat = jnp.where(rel[None] >= 0, jnp.exp(jnp.maximum(rel, 0.0)[None] * lg[:, None, None]), 0.0)
    xi = jnp.exp((idx + 1.0)[None, :] * lg[:, None])
    zeta = jnp.exp((c - 1.0 - idx)[None, :] * lg[:, None])
    gc = jnp.exp(c * lg)
    xi_b = jnp.broadcast_to(xi[:, :, None], (RET_HEADS, c, HEAD_DIM))
    zeta_b = jnp.broadcast_to(zeta[:, :, None], (RET_HEADS, c, HEAD_DIM))
    gc_b = jnp.broadcast_to(gc[:, None, None], (RET_HEADS, 1, HEAD_DIM))
    return dmat, xi_b, zeta_b, gc_b


def _group_norm(o):
    mu = jnp.mean(o, axis=-1, keepdims=True)
    d = o - mu
    var = jnp.mean(d * d, axis=-1, keepdims=True)
    return d * lax.rsqrt(var + EPS)


def _ret_chunk(qh, kh, vh, r, dmat, xi, zeta, gc):
    qb = qh.astype(BF16)
    kb = kh.astype(BF16)
    vb = vh.astype(BF16)
    s = lax.dot_general(qb, kb, (((1,), (1,)), ((), ())), preferred_element_type=F32) * dmat
    o = jnp.dot(s.astype(BF16), vb, preferred_element_type=F32)
    o = o + jnp.dot(qb, r.astype(BF16), preferred_element_type=F32) * xi
    kz = (kh.astype(F32) * zeta).astype(BF16)
    r_new = r * gc + lax.dot_general(kz, vb, (((0,), (0,)), ((), ())), preferred_element_type=F32)
    return o, r_new


def _ret_prompt_body(q_ref, k_ref, v_ref, gs_ref, d_ref, xi_ref, zeta_ref, gc_ref, o_ref, st_ref, r_ref):
    j = pl.program_id(1)

    @pl.when(j == 0)
    def _():
        r_ref[...] = jnp.zeros_like(r_ref)

    n_chunks = q_ref.shape[0] // CHUNK
    for hh in range(RET_HEADS):
        sl = slice(hh * HEAD_DIM, (hh + 1) * HEAD_DIM)
        r = r_ref[hh]
        for ci in range(n_chunks):
            rows = slice(ci * CHUNK, (ci + 1) * CHUNK)
            o, r = _ret_chunk(q_ref[rows, sl], k_ref[rows, sl], v_ref[rows, sl], r,
                              d_ref[hh], xi_ref[hh], zeta_ref[hh], gc_ref[hh])
            o_ref[rows, sl] = (gs_ref[rows, sl] * _group_norm(o)).astype(o_ref.dtype)
        r_ref[hh] = r
    st_ref[0] = r_ref[...]


def _ret_prompt(q, k, v, gs, n, l, tl):
    dmat, xi, zeta, gc = _decay_tables(CHUNK)
    per = l // tl
    row = lambda b, j: (b * per + j, 0)
    c3 = lambda b, j: (0, 0, 0)
    return pl.pallas_call(
        _ret_prompt_body,
        grid=(n, per),
        in_specs=[pl.BlockSpec((tl, RET_WIDTH), row)] * 4 + [
            pl.BlockSpec((RET_HEADS, CHUNK, CHUNK), c3),
            pl.BlockSpec((RET_HEADS, CHUNK, HEAD_DIM), c3),
            pl.BlockSpec((RET_HEADS, CHUNK, HEAD_DIM), c3),
            pl.BlockSpec((RET_HEADS, 1, HEAD_DIM), c3),
        ],
        out_specs=[
            pl.BlockSpec((tl, RET_WIDTH), row),
            pl.BlockSpec((1, RET_HEADS, HEAD_DIM, HEAD_DIM), lambda b, j: (b, 0, 0, 0)),
        ],
        out_shape=[
            jax.ShapeDtypeStruct((n * l, RET_WIDTH), BF16),
            jax.ShapeDtypeStruct((n, RET_HEADS, HEAD_DIM, HEAD_DIM), F32),
        ],
        scratch_shapes=[pltpu.VMEM((RET_HEADS, HEAD_DIM, HEAD_DIM), F32)],
        compiler_params=_cparams(("arbitrary", "arbitrary")),
        name="ret_prompt",
    )(q, k, v, gs, dmat, xi, zeta, gc)


def _ret_sample_body(q_ref, k_ref, v_ref, gs_ref, st_ref, d_ref, xi_ref, zeta_ref, gc_ref, o_ref, nst_ref):
    nb = st_ref.shape[0]
    l = q_ref.shape[0] // nb
    for b in range(nb):
        rows = slice(b * l, (b + 1) * l)
        for hh in range(RET_HEADS):
            sl = slice(hh * HEAD_DIM, (hh + 1) * HEAD_DIM)
            o, r = _ret_chunk(q_ref[rows, sl], k_ref[rows, sl], v_ref[rows, sl], st_ref[b, hh],
                              d_ref[hh], xi_ref[hh], zeta_ref[hh], gc_ref[hh])
            o_ref[rows, sl] = (gs_ref[rows, sl] * _group_norm(o)).astype(o_ref.dtype)
            nst_ref[b, hh] = r


def _ret_sample(q, k, v, gs, state, l, nb):
    n = state.shape[0]
    dmat, xi, zeta, gc = _decay_tables(l)
    row = lambda b: (b, 0)
    c3 = lambda b: (0, 0, 0)
    blk4 = lambda b: (b, 0, 0, 0)
    return pl.pallas_call(
        _ret_sample_body,
        grid=(n // nb,),
        in_specs=[pl.BlockSpec((nb * l, RET_WIDTH), row)] * 4 + [
            pl.BlockSpec((nb, RET_HEADS, HEAD_DIM, HEAD_DIM), blk4),
            pl.BlockSpec((RET_HEADS, l, l), c3),
            pl.BlockSpec((RET_HEADS, l, HEAD_DIM), c3),
            pl.BlockSpec((RET_HEADS, l, HEAD_DIM), c3),
            pl.BlockSpec((RET_HEADS, 1, HEAD_DIM), c3),
        ],
        out_specs=[
            pl.BlockSpec((nb * l, RET_WIDTH), row),
            pl.BlockSpec((nb, RET_HEADS, HEAD_DIM, HEAD_DIM), blk4),
        ],
        out_shape=[
            jax.ShapeDtypeStruct((n * l, RET_WIDTH), BF16),
            jax.ShapeDtypeStruct((n, RET_HEADS, HEAD_DIM, HEAD_DIM), F32),
        ],
        compiler_params=_cparams(("parallel",)),
        name="ret_sample",
    )(q, k, v, gs, state, dmat, xi, zeta, gc)


def _split3(x):
    hi = x.astype(BF16)
    r1 = x - hi.astype(F32)
    mid = r1.astype(BF16)
    lo = (r1 - mid.astype(F32)).astype(BF16)
    return hi, mid, lo


def _dot_hp(t, w_hi, w_mid):
    t_hi, t_mid, _ = _split3(t)
    d = functools.partial(jnp.dot, preferred_element_type=F32)
    return d(t_hi, w_hi) + (d(t_mid, w_hi) + d(t_hi, w_mid))


def _route(logits):
    lane = lax.broadcasted_iota(jnp.int32, logits.shape, 1)
    big = jnp.int32(LANES)
    is_group = (lane >= N_EXPERTS) & (lane < N_EXPERTS + N_GROUPS)
    lg = jnp.where(is_group, logits, -1e30)
    m = jnp.max(lg, axis=-1, keepdims=True)
    g_top = 1.0 / jnp.sum(jnp.exp(lg - m), axis=-1, keepdims=True)
    g_idx = jnp.min(jnp.where(lg == m, lane, big), axis=-1, keepdims=True) - N_EXPERTS
    in_group = (lane >= g_idx * EXPERTS_PER_GROUP) & (lane < (g_idx + 1) * EXPERTS_PER_GROUP)
    lem = jnp.where(in_group, logits, -1e30)
    m2 = jnp.max(lem, axis=-1, keepdims=True)
    pe = jnp.where(in_group, jnp.exp(lem - m2), 0.0)
    p1 = jnp.max(pe, axis=-1, keepdims=True)
    e1 = jnp.min(jnp.where(in_group & (pe == p1), lane, big), axis=-1, keepdims=True)
    rest = in_group & (lane != e1)
    pe2 = jnp.where(rest, pe, -1.0)
    p2 = jnp.max(pe2, axis=-1, keepdims=True)
    e2 = jnp.min(jnp.where(rest & (pe2 == p2), lane, big), axis=-1, keepdims=True)
    scale = g_top / (p1 + p2)
    rec = jnp.where(lane == 0, e1.astype(F32), jnp.where(lane == 1, e2.astype(F32), 0.0))
    return rec + jnp.where(lane == 2, p1 * scale, jnp.where(lane == 3, p2 * scale, 0.0))


def _outproj_body(c_ref, o_ref, x_ref, wo_ref, g2_ref, wr_ref, br_ref, x1_ref, t_ref, rec_ref):
    x1 = x_ref[...] + jnp.dot(c_ref[...], wo_ref[0:CONV_CH, :], preferred_element_type=F32)
    x1 = x1 + jnp.dot(o_ref[...], wo_ref[CONV_CH:, :], preferred_element_type=F32)
    x1_ref[...] = x1
    t = _rms(x1, g2_ref[...])
    t_ref[...] = t.astype(t_ref.dtype)
    rec_ref[...] = _route(_dot_hp(t, wr_ref[0], wr_ref[1]) + br_ref[...])


def _outproj(c, o, x, w_out, g2, wr2, br, tm):
    t = x.shape[0]
    row = lambda i: (i, 0)
    const = lambda i: (0, 0)
    c3 = lambda i: (0, 0, 0)
    return pl.pallas_call(
        _outproj_body,
        grid=(t // tm,),
        in_specs=[
            pl.BlockSpec((tm, CONV_CH), row),
            pl.BlockSpec((tm, RET_WIDTH), row),
            pl.BlockSpec((tm, D_MODEL), row),
            pl.BlockSpec((D_MODEL, D_MODEL), const),
            pl.BlockSpec((1, D_MODEL), const),
            pl.BlockSpec((2, D_MODEL, LANES), c3),
            pl.BlockSpec((1, LANES), const),
        ],
        out_specs=[
            pl.BlockSpec((tm, D_MODEL), row),
            pl.BlockSpec((tm, D_MODEL), row),
            pl.BlockSpec((tm, LANES), row),
        ],
        out_shape=[
            jax.ShapeDtypeStruct((t, D_MODEL), F32),
            jax.ShapeDtypeStruct((t, D_MODEL), BF16),
            jax.ShapeDtypeStruct((t, LANES), F32),
        ],
        compiler_params=_cparams(("parallel",)),
        name="outproj_router",
    )(c, o, x, w_out, g2, wr2, br)


def _mix_body(x_ref, g1_ref, w_ref, cos_ref, sin_ref, cw_ref, cb_ref, lg_ref, lb_ref,
              d_ref, xi_ref, zeta_ref, gc_ref, wo_ref, g2_ref, wr_ref, br_ref,
              x1_ref, t_ref, rec_ref, cst_ref, rst_ref, ext_ref, r_ref, o_ref):
    j = pl.program_id(1)
    tl = x_ref.shape[0]
    x = x_ref[...]
    z = jnp.dot(_rms(x, g1_ref[...]).astype(BF16), w_ref[...], preferred_element_type=F32)

    @pl.when(j == 0)
    def _():
        ext_ref[0:HALO, :] = jnp.zeros((HALO, CONV_CH), F32)
        ext_ref[tl + HALO:, :] = jnp.zeros((SUBLANES, CONV_CH), F32)
        r_ref[...] = jnp.zeros_like(r_ref)

    @pl.when(j > 0)
    def _():
        ext_ref[0:HALO, :] = ext_ref[tl:tl + HALO, :]

    ext_ref[HALO:tl + HALO, :] = z[:, :CONV_CH] * jax.nn.sigmoid(z[:, CONV_CH:2 * CONV_CH])
    acc = _dwconv(lambda s, n: ext_ref[s:s + n, :], cw_ref, tl, 0) + cb_ref[...]
    c = _ln_silu(acc, lg_ref[...], lb_ref[...]).astype(BF16)
    cst_ref[0] = ext_ref[tl + HALO_OFF:tl + HALO, :]

    cos = cos_ref[...]
    sin = sin_ref[...]
    q0 = 2 * CONV_CH
    k0 = q0 + RET_WIDTH
    v0 = k0 + RET_WIDTH
    g0 = v0 + RET_WIDTH
    for hh in range(RET_HEADS):
        lo = hh * HEAD_DIM
        qh = z[:, q0 + lo:q0 + lo + HEAD_DIM]
        kh = z[:, k0 + lo:k0 + lo + HEAD_DIM]
        qr = (qh * cos + pltpu.roll(qh, HEAD_DIM // 2, 1) * sin).astype(BF16)
        kr = ((kh * cos + pltpu.roll(kh, HEAD_DIM // 2, 1) * sin) * (HEAD_DIM ** -0.5)).astype(BF16)
        vh = z[:, v0 + lo:v0 + lo + HEAD_DIM].astype(BF16)
        g = z[:, g0 + lo:g0 + lo + HEAD_DIM]
        gs = g * jax.nn.sigmoid(g)
        r = r_ref[hh]
        for ci in range(tl // CHUNK):
            rows = slice(ci * CHUNK, (ci + 1) * CHUNK)
            o, r = _ret_chunk(qr[rows], kr[rows], vh[rows], r, d_ref[hh], xi_ref[hh], zeta_ref[hh], gc_ref[hh])
            o_ref[rows, lo:lo + HEAD_DIM] = (gs[rows] * _group_norm(o)).astype(BF16)
        r_ref[hh] = r
    rst_ref[0] = r_ref[...]

    x1 = x + jnp.dot(c, wo_ref[0:CONV_CH, :], preferred_element_type=F32)
    x1 = x1 + jnp.dot(o_ref[...], wo_ref[CONV_CH:, :], preferred_element_type=F32)
    x1_ref[...] = x1
    t = _rms(x1, g2_ref[...])
    t_ref[...] = t.astype(t_ref.dtype)
    rec_ref[...] = _route(_dot_hp(t, wr_ref[0], wr_ref[1]) + br_ref[...])


def _mix(x, g1, w_in, cos, sin, conv_w, conv_b, ln_g, ln_b, w_out, g2, wr2, br, n, l, tl):
    dmat, xi, zeta, gc = _decay_tables(CHUNK)
    per = l // tl
    row = lambda b, j: (b * per + j, 0)
    tab = lambda b, j: (j, 0)
    const = lambda b, j: (0, 0)
    c3 = lambda b, j: (0, 0, 0)
    once = dict(pipeline_mode=pl.Buffered(1))
    return pl.pallas_call(
        _mix_body,
        grid=(n, per),
        in_specs=[
            pl.BlockSpec((tl, D_MODEL), row),
            pl.BlockSpec((1, D_MODEL), const),
            pl.BlockSpec((D_MODEL, IN_COLS), const, **once),
            pl.BlockSpec((tl, HEAD_DIM), tab),
            pl.BlockSpec((tl, HEAD_DIM), tab),
            pl.BlockSpec((CONV_K, CONV_CH), const),
            pl.BlockSpec((1, CONV_CH), const),
            pl.BlockSpec((1, CONV_CH), const),
            pl.BlockSpec((1, CONV_CH), const),
            pl.BlockSpec((RET_HEADS, CHUNK, CHUNK), c3),
            pl.BlockSpec((RET_HEADS, CHUNK, HEAD_DIM), c3),
            pl.BlockSpec((RET_HEADS, CHUNK, HEAD_DIM), c3),
            pl.BlockSpec((RET_HEADS, 1, HEAD_DIM), c3),
            pl.BlockSpec((D_MODEL, D_MODEL), const, **once),
            pl.BlockSpec((1, D_MODEL), const),
            pl.BlockSpec((2, D_MODEL, LANES), c3, **once),
            pl.BlockSpec((1, LANES), const),
        ],
        out_specs=[
            pl.BlockSpec((tl, D_MODEL), row),
            pl.BlockSpec((tl, D_MODEL), row),
            pl.BlockSpec((tl, LANES), row),
            pl.BlockSpec((1, CONV_K - 1, CONV_CH), lambda b, j: (b, 0, 0)),
            pl.BlockSpec((1, RET_HEADS, HEAD_DIM, HEAD_DIM), lambda b, j: (b, 0, 0, 0)),
        ],
        out_shape=[
            jax.ShapeDtypeStruct((n * l, D_MODEL), F32),
            jax.ShapeDtypeStruct((n * l, D_MODEL), BF16),
            jax.ShapeDtypeStruct((n * l, LANES), F32),
            jax.ShapeDtypeStruct((n, CONV_K - 1, CONV_CH), F32),
            jax.ShapeDtypeStruct((n, RET_HEADS, HEAD_DIM, HEAD_DIM), F32),
        ],
        scratch_shapes=[
            pltpu.VMEM((tl + HALO + SUBLANES, CONV_CH), F32),
            pltpu.VMEM((RET_HEADS, HEAD_DIM, HEAD_DIM), F32),
            pltpu.VMEM((tl, RET_WIDTH), BF16),
        ],
        compiler_params=pltpu.CompilerParams(dimension_semantics=("arbitrary", "arbitrary"),
                                             vmem_limit_bytes=MIX_VMEM_LIMIT),
        name="token_mix",
    )(x, g1, w_in, cos, sin, conv_w, conv_b, ln_g, ln_b, dmat, xi, zeta, gc, w_out, g2, wr2, br)


def _iota_f32(shape, dim):
    return lax.broadcasted_iota(jnp.int32, shape, dim).astype(F32)


def _dispatch_body(rec_a_ref, t_a_ref, rec_b_ref, t_b_ref, s_ref, pos_ref, meta_ref, *, nsub_a):
    from_a = pl.program_id(0) < nsub_a
    rec = jnp.where(from_a, rec_a_ref[...], rec_b_ref[...])
    tok = jnp.where(from_a, t_a_ref[...], t_b_ref[...])
    lane = _iota_f32(rec.shape, 1)
    a1 = lane == rec[:, 0:1]
    a2 = lane == rec[:, 1:2]
    a1f = jnp.where(a1, 1.0, 0.0)
    a2f = jnp.where(a2, 1.0, 0.0)
    ltri = jnp.where(_iota_f32((SUB, SUB), 1) < _iota_f32((SUB, SUB), 0), 1.0, 0.0).astype(BF16)
    c1 = jnp.dot(ltri, a1f.astype(BF16), preferred_element_type=F32)
    c2 = jnp.dot(ltri, a2f.astype(BF16), preferred_element_type=F32)
    n1 = jnp.sum(a1f, axis=0, keepdims=True)
    n2 = jnp.sum(a2f, axis=0, keepdims=True)
    cnt = jnp.floor((n1 + n2 + (ROW_ALIGN - 1.0)) * (1.0 / ROW_ALIGN))
    utri = jnp.where(_iota_f32((LANES, LANES), 0) < _iota_f32((LANES, LANES), 1), 1.0, 0.0).astype(BF16)
    start = jnp.dot(jnp.broadcast_to(cnt, (SUBLANES, LANES)).astype(BF16), utri,
                    preferred_element_type=F32)[0:1]
    base1 = start * ROW_ALIGN
    base2 = base1 + n1
    pos1 = jnp.sum(jnp.where(a1, c1 + base1, 0.0), axis=1, keepdims=True)
    pos2 = jnp.sum(jnp.where(a2, c2 + base2, 0.0), axis=1, keepdims=True)
    posm = jnp.where(lane == 0.0, pos1, jnp.where(lane == 1.0, pos2, 0.0))
    pos_ref[...] = posm
    row = _iota_f32((SUBLANES, LANES), 0)
    meta_ref[0] = jnp.where(row == 0.0, start, jnp.where(row == 1.0, cnt, 0.0))

    g1 = _split3(rec[:, 2:3])
    g2 = _split3(rec[:, 3:4])
    info = jnp.where(lane == 6.0, rec[:, 0:1], jnp.where(lane == 7.0, rec[:, 1:2], 0.0))
    for i in range(3):
        info = jnp.where(lane == float(i), g1[i].astype(F32), info)
        info = jnp.where(lane == float(3 + i), g2[i].astype(F32), info)
    src = jnp.concatenate([tok, info.astype(BF16)], axis=1)

    post = posm.T
    p1 = post[0:1, :]
    p2 = post[1:2, :]
    used = jnp.sum(cnt) * ROW_ALIGN
    for b in range(SUBP // PBLK):
        chunks = slice(b * PBLK // ROW_ALIGN, (b + 1) * PBLK // ROW_ALIGN)

        @pl.when(b * PBLK < used)
        def _(b=b, chunks=chunks):
            r = _iota_f32((PBLK, SUB), 0) + float(b * PBLK)
            onehot = jnp.where(r == p1, 1.0, jnp.where(r == p2, 1.0, 0.0)).astype(BF16)
            sorted_rows = jnp.dot(onehot, src, preferred_element_type=F32).astype(BF16)
            s_ref[chunks] = sorted_rows.reshape(PBLK // ROW_ALIGN, ROW_ALIGN, ROW_W)

        @pl.when(b * PBLK >= used)
        def _(chunks=chunks):
            s_ref[chunks] = jnp.zeros((PBLK // ROW_ALIGN, ROW_ALIGN, ROW_W), BF16)


def _dispatch(rec_a, t_a, rec_b, t_b):
    nsub_a = rec_a.shape[0] // SUB
    nsub_b = rec_b.shape[0] // SUB
    nsub = nsub_a + nsub_b
    row = lambda i: (i, 0)
    row_a = lambda i: (jnp.minimum(i, nsub_a - 1), 0)
    row_b = lambda i: (jnp.maximum(i - nsub_a, 0), 0)
    return pl.pallas_call(
        functools.partial(_dispatch_body, nsub_a=nsub_a),
        grid=(nsub,),
        in_specs=[
            pl.BlockSpec((SUB, LANES), row_a),
            pl.BlockSpec((SUB, D_MODEL), row_a),
            pl.BlockSpec((SUB, LANES), row_b),
            pl.BlockSpec((SUB, D_MODEL), row_b),
        ],
        out_specs=[
            pl.BlockSpec((CHUNKS_PER_SUB, ROW_ALIGN, ROW_W), lambda i: (i, 0, 0)),
            pl.BlockSpec((SUB, LANES), row),
            pl.BlockSpec((1, SUBLANES, LANES), lambda i: (i, 0, 0)),
        ],
        out_shape=[
            jax.ShapeDtypeStruct((nsub * CHUNKS_PER_SUB, ROW_ALIGN, ROW_W), BF16),
            jax.ShapeDtypeStruct((nsub * SUB, LANES), F32),
            jax.ShapeDtypeStruct((nsub, SUBLANES, LANES), F32),
        ],
        compiler_params=_cparams(("parallel",)),
        name="moe_dispatch",
    )(rec_a, t_a, rec_b, t_b)


def _experts_body(start_ref, cnt_ref, s_in, wg_ref, wu_ref, wd_ref, s_hbm,
                  xbuf, ybuf, wgu_ref, wdb_ref, gsem, ssem, list_ref, state_ref, *, nsub):
    del s_in
    e = pl.program_id(0)
    ne = pl.num_programs(0)
    par = e & 1

    list_max = list_ref.shape[0] // 2

    def gather_copy(src, i, slot):
        return pltpu.make_async_copy(s_hbm.at[src], xbuf.at[slot, i], gsem.at[slot])

    def scatter_copy(dst, i, slot):
        return pltpu.make_async_copy(ybuf.at[slot, i], s_hbm.at[dst], ssem.at[slot])

    def build_list(x, which):
        def per_sub(s, k):
            run = s * N_EXPERTS + x
            c = cnt_ref[run]
            base = s * CHUNKS_PER_SUB + start_ref[run]
            list_ref[k] = base
            list_ref[k + 1] = base + 1

            def per_chunk(i, carry):
                list_ref[k + i] = base + i
                return carry
            lax.fori_loop(2, c, per_chunk, 0)
            return k + c
        first = which * list_max
        state_ref[which] = lax.fori_loop(0, nsub, per_sub, first) - first

    def start_all(copy, which, first, n, slot, counter):
        def body(i, carry):
            copy(list_ref[which * list_max + first + i], i, slot).start()
            return carry
        lax.fori_loop(0, n, body, 0)
        state_ref[counter] = n

    def wait_all(copy, block_copy, slot, counter):
        n = state_ref[counter]

        @pl.when(n == CHUNKS_PER_BLK)
        def _():
            block_copy(slot).wait()

        @pl.when(n < CHUNKS_PER_BLK)
        def _():
            def body(i, carry):
                copy(0, 0, slot).wait()
                return carry
            lax.fori_loop(0, n, body, 0)
        state_ref[counter] = 0

    def gather_block(slot):
        return pltpu.make_async_copy(s_hbm.at[pl.ds(0, CHUNKS_PER_BLK)], xbuf.at[slot], gsem.at[slot])

    def scatter_block(slot):
        return pltpu.make_async_copy(ybuf.at[slot], s_hbm.at[pl.ds(0, CHUNKS_PER_BLK)], ssem.at[slot])

    def block_chunks(total, b):
        return jnp.minimum(total - b * CHUNKS_PER_BLK, CHUNKS_PER_BLK)

    @pl.when(e == 0)
    def _():
        for i in range(6):
            state_ref[i] = 0
        xbuf[...] = jnp.zeros_like(xbuf)
        build_list(0, 0)
        n0 = state_ref[0]

        @pl.when(n0 > 0)
        def _():
            start_all(gather_copy, 0, 0, block_chunks(n0, 0), 0, 2)

    @pl.when(e + 1 < ne)
    def _():
        build_list(e + 1, 1 - par)

    total = state_ref[par]
    nblk = (total + CHUNKS_PER_BLK - 1) // CHUNKS_PER_BLK
    wgu_ref[:, 0:EXPERT_FF] = wg_ref[0].astype(BF16)
    wgu_ref[:, EXPERT_FF:] = wu_ref[0].astype(BF16)
    wdb_ref[...] = wd_ref[0].astype(BF16)
    ef = e.astype(F32)

    def block(b, carry):
        slot = b & 1
        first = b * CHUNKS_PER_BLK

        @pl.when(b + 1 < nblk)
        def _():
            start_all(gather_copy, par, first + CHUNKS_PER_BLK, block_chunks(total, b + 1), 1 - slot, 3 - slot)

        wait_all(gather_copy, gather_block, slot, 2 + slot)
        wait_all(scatter_copy, scatter_block, slot, 4 + slot)
        x = xbuf[slot].reshape(MBLK, ROW_W)
        info = x[:, D_MODEL:].astype(F32)
        g_first = info[:, 0:1] + info[:, 1:2] + info[:, 2:3]
        g_second = info[:, 3:4] + info[:, 4:5] + info[:, 5:6]
        gate = jnp.where(info[:, 6:7] == ef, g_first, g_second)
        h = jnp.dot(x[:, :D_MODEL], wgu_ref[...], preferred_element_type=F32)
        h1 = h[:, :EXPERT_FF]
        hid = (h1 * jax.nn.sigmoid(h1)) * h[:, EXPERT_FF:] * gate
        y = jnp.dot(hid.astype(BF16), wdb_ref[...], preferred_element_type=F32).astype(BF16)
        ybuf[slot] = jnp.concatenate([y, x[:, D_MODEL:]], axis=1).reshape(CHUNKS_PER_BLK, ROW_ALIGN, ROW_W)
        start_all(scatter_copy, par, first, block_chunks(total, b), slot, 4 + slot)
        return carry

    lax.fori_loop(0, nblk, block, 0)

    @pl.when(e + 1 < ne)
    def _():
        n1 = state_ref[1 - par]

        @pl.when(n1 > 0)
        def _():
            start_all(gather_copy, 1 - par, 0, block_chunks(n1, 0), 0, 2)

    @pl.when(e == ne - 1)
    def _():
        wait_all(scatter_copy, scatter_block, 0, 4)
        wait_all(scatter_copy, scatter_block, 1, 5)


def _experts(start, cnt, staged, wg, wu, wd):
    nsub = staged.shape[0] // CHUNKS_PER_SUB
    list_max = nsub * SUB // ROW_ALIGN + nsub + LIST_SLACK
    wblk = lambda e, *_: (e, 0, 0)
    grid_spec = pltpu.PrefetchScalarGridSpec(
        num_scalar_prefetch=2,
        grid=(N_EXPERTS,),
        in_specs=[
            pl.BlockSpec(memory_space=pl.ANY),
            pl.BlockSpec((1, D_MODEL, EXPERT_FF), wblk),
            pl.BlockSpec((1, D_MODEL, EXPERT_FF), wblk),
            pl.BlockSpec((1, EXPERT_FF, D_MODEL), wblk),
        ],
        out_specs=pl.BlockSpec(memory_space=pl.ANY),
        scratch_shapes=[
            pltpu.VMEM((2, CHUNKS_PER_BLK, ROW_ALIGN, ROW_W), BF16),
            pltpu.VMEM((2, CHUNKS_PER_BLK, ROW_ALIGN, ROW_W), BF16),
            pltpu.VMEM((D_MODEL, 2 * EXPERT_FF), BF16),
            pltpu.VMEM((EXPERT_FF, D_MODEL), BF16),
            pltpu.SemaphoreType.DMA((2,)),
            pltpu.SemaphoreType.DMA((2,)),
            pltpu.SMEM((2 * list_max,), jnp.int32),
            pltpu.SMEM((6,), jnp.int32),
        ],
    )
    return pl.pallas_call(
        functools.partial(_experts_body, nsub=nsub),
        grid_spec=grid_spec,
        out_shape=jax.ShapeDtypeStruct(staged.shape, staged.dtype),
        input_output_aliases={2: 0},
        compiler_params=_cparams(("arbitrary",)),
        name="moe_experts",
    )(start, cnt, staged, wg, wu, wd)


def _combine_body(ys_ref, pos_ref, x1_ref, p_ref, wp_ref, gp_ref, wpg_ref, gf_ref, y_ref):
    p1 = pos_ref[:, 0:1]
    p2 = pos_ref[:, 1:2]
    r = _iota_f32((SUB, SUBP), 1)
    onehot = jnp.where(r == p1, 1.0, jnp.where(r == p2, 1.0, 0.0)).astype(BF16)
    ys = ys_ref[...].reshape(SUBP, D_MODEL)
    x2 = x1_ref[...] + jnp.dot(onehot, ys, preferred_element_type=F32)
    ple = _rms(jnp.dot(p_ref[...].astype(BF16), wp_ref[...], preferred_element_type=F32), gp_ref[...])
    gate = jax.nn.sigmoid(jnp.dot(x2.astype(BF16), wpg_ref[...], preferred_element_type=F32))
    y_ref[...] = _rms(x2 + ple * gate, gf_ref[...])


def _combine(ys, pos, x1, p, w_ple, gp, w_ple_gate, gf, sub_off):
    t = x1.shape[0]
    row = lambda i: (i, 0)
    const = lambda i: (0, 0)
    return pl.pallas_call(
        _combine_body,
        grid=(t // SUB,),
        in_specs=[
            pl.BlockSpec((CHUNKS_PER_SUB, ROW_ALIGN, D_MODEL), lambda i: (i + sub_off, 0, 0)),
            pl.BlockSpec((SUB, LANES), lambda i: (i + sub_off, 0)),
            pl.BlockSpec((SUB, D_MODEL), row),
            pl.BlockSpec((SUB, PLE_DIM), row),
            pl.BlockSpec((PLE_DIM, D_MODEL), const),
            pl.BlockSpec((1, D_MODEL), const),
            pl.BlockSpec((D_MODEL, D_MODEL), const),
            pl.BlockSpec((1, D_MODEL), const),
        ],
        out_specs=pl.BlockSpec((SUB, D_MODEL), row),
        out_shape=jax.ShapeDtypeStruct((t, D_MODEL), F32),
        compiler_params=_cparams(("parallel",)),
        name="moe_combine_ple",
    )(ys, pos, x1, p, w_ple, gp, w_ple_gate, gf)


def _rope_tables(pos):
    half = HEAD_DIM // 2
    inv = ROPE_BASE ** (-jnp.arange(half, dtype=F32) / half)
    ang = pos[:, None] * inv[None, :]
    cos = jnp.cos(ang)
    sin = jnp.sin(ang)
    return jnp.concatenate([cos, cos], axis=-1), jnp.concatenate([-sin, sin], axis=-1)


def _router_params(we, be, wg, bg):
    pad = LANES - N_EXPERTS - N_GROUPS
    w = jnp.pad(jnp.concatenate([we, wg], axis=1), ((0, 0), (0, pad)))
    b = jnp.pad(jnp.concatenate([be, bg]), (0, pad))[None, :]
    return jnp.stack(_split3(w)[:2]), b


def kernel(x_prompt, x_sample, p_prompt, p_sample, state_conv, state_ret, w_in, conv_w, conv_b, conv_ln_g, conv_ln_b, w_out, norm1_g, norm2_g, router_group_w, router_group_b, router_expert_w, router_expert_b, w_expert_gate, w_expert_up, w_expert_down, w_ple, ple_norm_g, w_ple_gate, final_norm_g):
    assert w_in.shape[0] == 1, "single-layer trunk"
    nb, seq, _ = x_prompt.shape
    ns, dseq, _ = x_sample.shape
    tm = 512

    w_in_b = w_in[0].astype(BF16)
    w_out_b = w_out[0].astype(BF16)
    w_ple_b = w_ple[0].astype(BF16)
    w_pg_b = w_ple_gate[0].astype(BF16)
    g1 = norm1_g[0][None, :]
    g2 = norm2_g[0][None, :]
    gp = ple_norm_g[0][None, :]
    gf = final_norm_g[None, :]
    cb = conv_b[0][None, :]
    lng = conv_ln_g[0][None, :]
    lnb = conv_ln_b[0][None, :]
    wr2, br = _router_params(router_expert_w[0], router_expert_b[0], router_group_w[0], router_group_b[0])

    cos_p, sin_p = _rope_tables(jnp.arange(seq, dtype=F32) + jnp.float32(0))
    pos_s = jnp.tile(jnp.arange(dseq, dtype=F32) + jnp.float32(PAST_LEN), tm // dseq)
    cos_s, sin_s = _rope_tables(pos_s)

    xp = x_prompt.reshape(nb * seq, D_MODEL)
    x1_p, t_p, rec_p, conv_p, ret_p = _mix(xp, g1, w_in_b, cos_p, sin_p, conv_w[0], cb, lng, lnb, w_out_b, g2,
                                           wr2, br, nb, seq, tm)

    xs = x_sample.reshape(ns * dseq, D_MODEL)
    u, q, k, v, gs = _inproj(xs, g1, w_in_b, cos_s, sin_s, tm, 1, F32)
    c, conv_s = _conv_sample(u.reshape(ns, dseq, CONV_CH), state_conv[0], conv_w[0], cb, lng, lnb, 16)
    o, ret_s = _ret_sample(q, k, v, gs, state_ret[0], dseq, 8)
    x1_s, t_s, rec_s = _outproj(c.reshape(ns * dseq, CONV_CH), o, xs, w_out_b, g2, wr2, br, tm)

    staged, pos, meta = _dispatch(rec_p, t_p, rec_s, t_s)
    start = meta[:, 0, :N_EXPERTS].astype(jnp.int32).reshape(-1)
    cnt = meta[:, 1, :N_EXPERTS].astype(jnp.int32).reshape(-1)
    ys = _experts(start, cnt, staged, w_expert_gate[0], w_expert_up[0], w_expert_down[0])

    y_p = _combine(ys, pos, x1_p, p_prompt[0].reshape(nb * seq, PLE_DIM), w_ple_b, gp, w_pg_b, gf, 0)
    y_s = _combine(ys, pos, x1_s, p_sample[0].reshape(ns * dseq, PLE_DIM), w_ple_b, gp, w_pg_b, gf,
                   nb * seq // SUB)

    return (y_p.reshape(nb, seq, D_MODEL), y_s.reshape(ns, dseq, D_MODEL),
            conv_p[None], ret_p[None], conv_s[None], ret_s[None])
```

```python
import functools

import jax
import jax.numpy as jnp
from jax import lax
from jax.experimental import pallas as pl
from jax.experimental.pallas import tpu as pltpu

F32 = jnp.float32
BF16 = jnp.bfloat16

D_MODEL = 1024
PLE_DIM = 256
CONV_CH = 512
CONV_K = 31
RET_WIDTH = 512
RET_HEADS = 4
HEAD_DIM = 128
CHUNK = 128
ROPE_BASE = 10000.0
N_GROUPS = 4
EXPERTS_PER_GROUP = 8
N_EXPERTS = 32
EXPERT_FF = 256
IN_COLS = 3072
EPS = 1e-6
PAST_LEN = 16384

LANES = 128
SUBLANES = 8
HALO = 32
HALO_OFF = HALO - (CONV_K - 1)
VMEM_LIMIT = 48 * 1024 * 1024
MIX_VMEM_LIMIT = 56 * 1024 * 1024

SUB = 256
ROW_ALIGN = 16
PBLK = 256
SUBP = -(-(2 * SUB + N_EXPERTS * (ROW_ALIGN - 1)) // PBLK) * PBLK
CHUNKS_PER_SUB = SUBP // ROW_ALIGN
ROW_W = D_MODEL + LANES
MBLK = 256
CHUNKS_PER_BLK = MBLK // ROW_ALIGN
LIST_SLACK = 2


def _cparams(sem):
    return pltpu.CompilerParams(dimension_semantics=sem, vmem_limit_bytes=VMEM_LIMIT)


def _rms(x, g):
    return x * lax.rsqrt(jnp.mean(x * x, axis=-1, keepdims=True) + EPS) * g


def _inproj_body(x_ref, g1_ref, w_ref, cos_ref, sin_ref, u_ref, q_ref, k_ref, v_ref, gs_ref):
    h = _rms(x_ref[...], g1_ref[...]).astype(BF16)
    z = jnp.dot(h, w_ref[...], preferred_element_type=F32)
    a = z[:, :CONV_CH]
    b = z[:, CONV_CH:2 * CONV_CH]
    u_ref[...] = a * jax.nn.sigmoid(b)
    cos = cos_ref[...]
    sin = sin_ref[...]
    q0 = 2 * CONV_CH
    k0 = q0 + RET_WIDTH
    for hh in range(RET_HEADS):
        sl = slice(hh * HEAD_DIM, (hh + 1) * HEAD_DIM)
        qh = z[:, q0 + hh * HEAD_DIM:q0 + (hh + 1) * HEAD_DIM]
        kh = z[:, k0 + hh * HEAD_DIM:k0 + (hh + 1) * HEAD_DIM]
        q_ref[:, sl] = (qh * cos + pltpu.roll(qh, HEAD_DIM // 2, 1) * sin).astype(q_ref.dtype)
        kr = (kh * cos + pltpu.roll(kh, HEAD_DIM // 2, 1) * sin) * (HEAD_DIM ** -0.5)
        k_ref[:, sl] = kr.astype(k_ref.dtype)
    v_ref[...] = z[:, k0 + RET_WIDTH:k0 + 2 * RET_WIDTH].astype(v_ref.dtype)
    g = z[:, k0 + 2 * RET_WIDTH:]
    gs_ref[...] = g * jax.nn.sigmoid(g)


def _inproj(x, g1, w_in, cos, sin, tm, table_blocks, qkv_dtype):
    t = x.shape[0]
    row = lambda i: (i, 0)
    const = lambda i: (0, 0)
    tab = (lambda i: (i % table_blocks, 0)) if table_blocks > 1 else const
    return pl.pallas_call(
        _inproj_body,
        grid=(t // tm,),
        in_specs=[
            pl.BlockSpec((tm, D_MODEL), row),
            pl.BlockSpec((1, D_MODEL), const),
            pl.BlockSpec((D_MODEL, IN_COLS), const),
            pl.BlockSpec((tm, HEAD_DIM), tab),
            pl.BlockSpec((tm, HEAD_DIM), tab),
        ],
        out_specs=[pl.BlockSpec((tm, CONV_CH), row)] + [pl.BlockSpec((tm, RET_WIDTH), row)] * 4,
        out_shape=[
            jax.ShapeDtypeStruct((t, CONV_CH), F32),
            jax.ShapeDtypeStruct((t, RET_WIDTH), qkv_dtype),
            jax.ShapeDtypeStruct((t, RET_WIDTH), qkv_dtype),
            jax.ShapeDtypeStruct((t, RET_WIDTH), qkv_dtype),
            jax.ShapeDtypeStruct((t, RET_WIDTH), F32),
        ],
        compiler_params=_cparams(("parallel",)),
        name="inproj",
    )(x, g1, w_in, cos, sin)


def _ln_silu(acc, g, b):
    mu = jnp.mean(acc, axis=-1, keepdims=True)
    d = acc - mu
    var = jnp.mean(d * d, axis=-1, keepdims=True)
    y = d * lax.rsqrt(var + EPS) * g + b
    return y * jax.nn.sigmoid(y)


def _dwconv(load, w_ref, rows, time_axis):
    acc = None
    for b in range(SUBLANES):
        part = None
        for a in range((CONV_K + HALO_OFF) // SUBLANES + 1):
            k = SUBLANES * a + b - HALO_OFF
            if 0 <= k < CONV_K:
                term = load(SUBLANES * a, rows + SUBLANES) * w_ref[k:k + 1, :]
                part = term if part is None else part + term
        if part is not None:
            shifted = lax.slice_in_dim(part, b, b + rows, axis=time_axis)
            acc = shifted if acc is None else acc + shifted
    return acc


def _conv_prompt_body(u_ref, w_ref, cb_ref, lg_ref, lb_ref, c_ref, st_ref, ext_ref):
    j = pl.program_id(1)
    tl = u_ref.shape[1]

    @pl.when(j == 0)
    def _():
        ext_ref[0:HALO, :] = jnp.zeros((HALO, CONV_CH), F32)
        ext_ref[tl + HALO:, :] = jnp.zeros((SUBLANES, CONV_CH), F32)

    @pl.when(j > 0)
    def _():
        ext_ref[0:HALO, :] = ext_ref[tl:tl + HALO, :]

    ext_ref[HALO:tl + HALO, :] = u_ref[0]
    acc = _dwconv(lambda s, n: ext_ref[s:s + n, :], w_ref, tl, 0) + cb_ref[...]
    c_ref[0] = _ln_silu(acc, lg_ref[...], lb_ref[...]).astype(c_ref.dtype)
    st_ref[0] = ext_ref[tl + HALO_OFF:tl + HALO, :]


def _conv_prompt(u, conv_w, conv_b, ln_g, ln_b, tl):
    n, l, _ = u.shape
    const = lambda b, j: (0, 0)
    return pl.pallas_call(
        _conv_prompt_body,
        grid=(n, l // tl),
        in_specs=[
            pl.BlockSpec((1, tl, CONV_CH), lambda b, j: (b, j, 0)),
            pl.BlockSpec((CONV_K, CONV_CH), const),
            pl.BlockSpec((1, CONV_CH), const),
            pl.BlockSpec((1, CONV_CH), const),
            pl.BlockSpec((1, CONV_CH), const),
        ],
        out_specs=[
            pl.BlockSpec((1, tl, CONV_CH), lambda b, j: (b, j, 0)),
            pl.BlockSpec((1, CONV_K - 1, CONV_CH), lambda b, j: (b, 0, 0)),
        ],
        out_shape=[
            jax.ShapeDtypeStruct((n, l, CONV_CH), BF16),
            jax.ShapeDtypeStruct((n, CONV_K - 1, CONV_CH), F32),
        ],
        scratch_shapes=[pltpu.VMEM((tl + HALO + SUBLANES, CONV_CH), F32)],
        compiler_params=_cparams(("arbitrary", "arbitrary")),
        name="conv_prompt",
    )(u, conv_w, conv_b, ln_g, ln_b)


def _conv_sample_body(u_ref, st_ref, w_ref, cb_ref, lg_ref, lb_ref, c_ref, nst_ref, ext_ref):
    nb, l, _ = u_ref.shape
    ext_ref[:, 0:HALO_OFF, :] = jnp.zeros((nb, HALO_OFF, CONV_CH), F32)
    ext_ref[:, HALO_OFF:HALO, :] = st_ref[...]
    ext_ref[:, HALO:l + HALO, :] = u_ref[...]
    ext_ref[:, l + HALO:, :] = jnp.zeros((nb, SUBLANES, CONV_CH), F32)
    acc = _dwconv(lambda s, n: ext_ref[:, s:s + n, :], w_ref, l, 1) + cb_ref[...]
    c_ref[...] = _ln_silu(acc, lg_ref[...], lb_ref[...]).astype(c_ref.dtype)
    nst_ref[...] = ext_ref[:, l + HALO_OFF:l + HALO, :]


def _conv_sample(u, state, conv_w, conv_b, ln_g, ln_b, nb):
    n, l, _ = u.shape
    const = lambda b: (0, 0)
    blk3 = lambda b: (b, 0, 0)
    return pl.pallas_call(
        _conv_sample_body,
        grid=(n // nb,),
        in_specs=[
            pl.BlockSpec((nb, l, CONV_CH), blk3),
            pl.BlockSpec((nb, CONV_K - 1, CONV_CH), blk3),
            pl.BlockSpec((CONV_K, CONV_CH), const),
            pl.BlockSpec((1, CONV_CH), const),
            pl.BlockSpec((1, CONV_CH), const),
            pl.BlockSpec((1, CONV_CH), const),
        ],
        out_specs=[
            pl.BlockSpec((nb, l, CONV_CH), blk3),
            pl.BlockSpec((nb, CONV_K - 1, CONV_CH), blk3),
        ],
        out_shape=[
            jax.ShapeDtypeStruct((n, l, CONV_CH), BF16),
            jax.ShapeDtypeStruct((n, CONV_K - 1, CONV_CH), F32),
        ],
        scratch_shapes=[pltpu.VMEM((nb, l + HALO + SUBLANES, CONV_CH), F32)],
        compiler_params=_cparams(("parallel",)),
        name="conv_sample",
    )(u, state, conv_w, conv_b, ln_g, ln_b)


def _decay_tables(c):
    lg = jnp.log(1.0 - 2.0 ** (-5.0 - jnp.arange(RET_HEADS, dtype=F32)))
    idx = jnp.arange(c, dtype=F32)
    rel = idx[:, None] - idx[None, :]
    dmat = jnp.where(rel[None] >= 0, jnp.exp(jnp.maximum(rel, 0.0)[None] * lg[:, None, None]), 0.0)
    xi = jnp.exp((idx + 1.0)[None, :] * lg[:, None])
    zeta = jnp.exp((c - 1.0 - idx)[None, :] * lg[:, None])
    gc = jnp.exp(c * lg)
    xi_b = jnp.broadcast_to(xi[:, :, None], (RET_HEADS, c, HEAD_DIM))
    zeta_b = jnp.broadcast_to(zeta[:, :, None], (RET_HEADS, c, HEAD_DIM))
    gc_b = jnp.broadcast_to(gc[:, None, None], (RET_HEADS, 1, HEAD_DIM))
    return dmat, xi_b, zeta_b, gc_b


def _group_norm(o):
    mu = jnp.mean(o, axis=-1, keepdims=True)
    d = o - mu
    var = jnp.mean(d * d, axis=-1, keepdims=True)
    return d * lax.rsqrt(var + EPS)


def _ret_chunk(qh, kh, vh, r, dmat, xi, zeta, gc):
    qb = qh.astype(BF16)
    kb = kh.astype(BF16)
    vb = vh.astype(BF16)
    s = lax.dot_general(qb, kb, (((1,), (1,)), ((), ())), preferred_element_type=F32) * dmat
    o = jnp.dot(s.astype(BF16), vb, preferred_element_type=F32)
    o = o + jnp.dot(qb, r.astype(BF16), preferred_element_type=F32) * xi
    kz = (kh.astype(F32) * zeta).astype(BF16)
    r_new = r * gc + lax.dot_general(kz, vb, (((0,), (0,)), ((), ())), preferred_element_type=F32)
    return o, r_new


def _ret_prompt_body(q_ref, k_ref, v_ref, gs_ref, d_ref, xi_ref, zeta_ref, gc_ref, o_ref, st_ref, r_ref):
    j = pl.program_id(1)

    @pl.when(j == 0)
    def _():
        r_ref[...] = jnp.zeros_like(r_ref)

    n_chunks = q_ref.shape[0] // CHUNK
    for hh in range(RET_HEADS):
        sl = slice(hh * HEAD_DIM, (hh + 1) * HEAD_DIM)
        r = r_ref[hh]
        for ci in range(n_chunks):
            rows = slice(ci * CHUNK, (ci + 1) * CHUNK)
            o, r = _ret_chunk(q_ref[rows, sl], k_ref[rows, sl], v_ref[rows, sl], r,
                              d_ref[hh], xi_ref[hh], zeta_ref[hh], gc_ref[hh])
            o_ref[rows, sl] = (gs_ref[rows, sl] * _group_norm(o)).astype(o_ref.dtype)
        r_ref[hh] = r
    st_ref[0] = r_ref[...]


def _ret_prompt(q, k, v, gs, n, l, tl):
    dmat, xi, zeta, gc = _decay_tables(CHUNK)
    per = l // tl
    row = lambda b, j: (b * per + j, 0)
    c3 = lambda b, j: (0, 0, 0)
    return pl.pallas_call(
        _ret_prompt_body,
        grid=(n, per),
        in_specs=[pl.BlockSpec((tl, RET_WIDTH), row)] * 4 + [
            pl.BlockSpec((RET_HEADS, CHUNK, CHUNK), c3),
            pl.BlockSpec((RET_HEADS, CHUNK, HEAD_DIM), c3),
            pl.BlockSpec((RET_HEADS, CHUNK, HEAD_DIM), c3),
            pl.BlockSpec((RET_HEADS, 1, HEAD_DIM), c3),
        ],
        out_specs=[
            pl.BlockSpec((tl, RET_WIDTH), row),
            pl.BlockSpec((1, RET_HEADS, HEAD_DIM, HEAD_DIM), lambda b, j: (b, 0, 0, 0)),
        ],
        out_shape=[
            jax.ShapeDtypeStruct((n * l, RET_WIDTH), BF16),
            jax.ShapeDtypeStruct((n, RET_HEADS, HEAD_DIM, HEAD_DIM), F32),
        ],
        scratch_shapes=[pltpu.VMEM((RET_HEADS, HEAD_DIM, HEAD_DIM), F32)],
        compiler_params=_cparams(("arbitrary", "arbitrary")),
        name="ret_prompt",
    )(q, k, v, gs, dmat, xi, zeta, gc)


def _ret_sample_body(q_ref, k_ref, v_ref, gs_ref, st_ref, d_ref, xi_ref, zeta_ref, gc_ref, o_ref, nst_ref):
    nb = st_ref.shape[0]
    l = q_ref.shape[0] // nb
    for b in range(nb):
        rows = slice(b * l, (b + 1) * l)
        for hh in range(RET_HEADS):
            sl = slice(hh * HEAD_DIM, (hh + 1) * HEAD_DIM)
            o, r = _ret_chunk(q_ref[rows, sl], k_ref[rows, sl], v_ref[rows, sl], st_ref[b, hh],
                              d_ref[hh], xi_ref[hh], zeta_ref[hh], gc_ref[hh])
            o_ref[rows, sl] = (gs_ref[rows, sl] * _group_norm(o)).astype(o_ref.dtype)
            nst_ref[b, hh] = r


def _ret_sample(q, k, v, gs, state, l, nb):
    n = state.shape[0]
    dmat, xi, zeta, gc = _decay_tables(l)
    row = lambda b: (b, 0)
    c3 = lambda b: (0, 0, 0)
    blk4 = lambda b: (b, 0, 0, 0)
    return pl.pallas_call(
        _ret_sample_body,
        grid=(n // nb,),
        in_specs=[pl.BlockSpec((nb * l, RET_WIDTH), row)] * 4 + [
            pl.BlockSpec((nb, RET_HEADS, HEAD_DIM, HEAD_DIM), blk4),
            pl.BlockSpec((RET_HEADS, l, l), c3),
            pl.BlockSpec((RET_HEADS, l, HEAD_DIM), c3),
            pl.BlockSpec((RET_HEADS, l, HEAD_DIM), c3),
            pl.BlockSpec((RET_HEADS, 1, HEAD_DIM), c3),
        ],
        out_specs=[
            pl.BlockSpec((nb * l, RET_WIDTH), row),
            pl.BlockSpec((nb, RET_HEADS, HEAD_DIM, HEAD_DIM), blk4),
        ],
        out_shape=[
            jax.ShapeDtypeStruct((n * l, RET_WIDTH), BF16),
            jax.ShapeDtypeStruct((n, RET_HEADS, HEAD_DIM, HEAD_DIM), F32),
        ],
        compiler_params=_cparams(("parallel",)),
        name="ret_sample",
    )(q, k, v, gs, state, dmat, xi, zeta, gc)


def _split3(x):
    hi = x.astype(BF16)
    r1 = x - hi.astype(F32)
    mid = r1.astype(BF16)
    lo = (r1 - mid.astype(F32)).astype(BF16)
    return hi, mid, lo


def _dot_hp(t, w_hi, w_mid):
    t_hi, t_mid, _ = _split3(t)
    d = functools.partial(jnp.dot, preferred_element_type=F32)
    return d(t_hi, w_hi) + (d(t_mid, w_hi) + d(t_hi, w_mid))


def _route(logits):
    lane = lax.broadcasted_iota(jnp.int32, logits.shape, 1)
    big = jnp.int32(LANES)
    is_group = (lane >= N_EXPERTS) & (lane < N_EXPERTS + N_GROUPS)
    lg = jnp.where(is_group, logits, -1e30)
    m = jnp.max(lg, axis=-1, keepdims=True)
    g_top = 1.0 / jnp.sum(jnp.exp(lg - m), axis=-1, keepdims=True)
    g_idx = jnp.min(jnp.where(lg == m, lane, big), axis=-1, keepdims=True) - N_EXPERTS
    in_group = (lane >= g_idx * EXPERTS_PER_GROUP) & (lane < (g_idx + 1) * EXPERTS_PER_GROUP)
    lem = jnp.where(in_group, logits, -1e30)
    m2 = jnp.max(lem, axis=-1, keepdims=True)
    pe = jnp.where(in_group, jnp.exp(lem - m2), 0.0)
    p1 = jnp.max(pe, axis=-1, keepdims=True)
    e1 = jnp.min(jnp.where(in_group & (pe == p1), lane, big), axis=-1, keepdims=True)
    rest = in_group & (lane != e1)
    pe2 = jnp.where(rest, pe, -1.0)
    p2 = jnp.max(pe2, axis=-1, keepdims=True)
    e2 = jnp.min(jnp.where(rest & (pe2 == p2), lane, big), axis=-1, keepdims=True)
    scale = g_top / (p1 + p2)
    rec = jnp.where(lane == 0, e1.astype(F32), jnp.where(lane == 1, e2.astype(F32), 0.0))
    return rec + jnp.where(lane == 2, p1 * scale, jnp.where(lane == 3, p2 * scale, 0.0))


def _outproj_body(c_ref, o_ref, x_ref, wo_ref, g2_ref, wr_ref, br_ref, x1_ref, t_ref, rec_ref):
    x1 = x_ref[...] + jnp.dot(c_ref[...], wo_ref[0:CONV_CH, :], preferred_element_type=F32)
    x1 = x1 + jnp.dot(o_ref[...], wo_ref[CONV_CH:, :], preferred_element_type=F32)
    x1_ref[...] = x1
    t = _rms(x1, g2_ref[...])
    t_ref[...] = t.astype(t_ref.dtype)
    rec_ref[...] = _route(_dot_hp(t, wr_ref[0], wr_ref[1]) + br_ref[...])


def _outproj(c, o, x, w_out, g2, wr2, br, tm):
    t = x.shape[0]
    row = lambda i: (i, 0)
    const = lambda i: (0, 0)
    c3 = lambda i: (0, 0, 0)
    return pl.pallas_call(
        _outproj_body,
        grid=(t // tm,),
        in_specs=[
            pl.BlockSpec((tm, CONV_CH), row),
            pl.BlockSpec((tm, RET_WIDTH), row),
            pl.BlockSpec((tm, D_MODEL), row),
            pl.BlockSpec((D_MODEL, D_MODEL), const),
            pl.BlockSpec((1, D_MODEL), const),
            pl.BlockSpec((2, D_MODEL, LANES), c3),
            pl.BlockSpec((1, LANES), const),
        ],
        out_specs=[
            pl.BlockSpec((tm, D_MODEL), row),
            pl.BlockSpec((tm, D_MODEL), row),
            pl.BlockSpec((tm, LANES), row),
        ],
        out_shape=[
            jax.ShapeDtypeStruct((t, D_MODEL), F32),
            jax.ShapeDtypeStruct((t, D_MODEL), BF16),
            jax.ShapeDtypeStruct((t, LANES), F32),
        ],
        compiler_params=_cparams(("parallel",)),
        name="outproj_router",
    )(c, o, x, w_out, g2, wr2, br)


def _mix_body(x_ref, g1_ref, w_ref, cos_ref, sin_ref, cw_ref, cb_ref, lg_ref, lb_ref,
              d_ref, xi_ref, zeta_ref, gc_ref, wo_ref, g2_ref, wr_ref, br_ref,
              x1_ref, t_ref, rec_ref, cst_ref, rst_ref, ext_ref, r_ref, o_ref):
    j = pl.program_id(1)
    tl = x_ref.shape[0]
    x = x_ref[...]
    z = jnp.dot(_rms(x, g1_ref[...]).astype(BF16), w_ref[...], preferred_element_type=F32)

    @pl.when(j == 0)
    def _():
        ext_ref[0:HALO, :] = jnp.zeros((HALO, CONV_CH), F32)
        ext_ref[tl + HALO:, :] = jnp.zeros((SUBLANES, CONV_CH), F32)
        r_ref[...] = jnp.zeros_like(r_ref)

    @pl.when(j > 0)
    def _():
        ext_ref[0:HALO, :] = ext_ref[tl:tl + HALO, :]

    ext_ref[HALO:tl + HALO, :] = z[:, :CONV_CH] * jax.nn.sigmoid(z[:, CONV_CH:2 * CONV_CH])
    acc = _dwconv(lambda s, n: ext_ref[s:s + n, :], cw_ref, tl, 0) + cb_ref[...]
    c = _ln_silu(acc, lg_ref[...], lb_ref[...]).astype(BF16)
    cst_ref[0] = ext_ref[tl + HALO_OFF:tl + HALO, :]

    cos = cos_ref[...]
    sin = sin_ref[...]
    q0 = 2 * CONV_CH
    k0 = q0 + RET_WIDTH
    v0 = k0 + RET_WIDTH
    g0 = v0 + RET_WIDTH
    for hh in range(RET_HEADS):
        lo = hh * HEAD_DIM
        qh = z[:, q0 + lo:q0 + lo + HEAD_DIM]
        kh = z[:, k0 + lo:k0 + lo + HEAD_DIM]
        qr = (qh * cos + pltpu.roll(qh, HEAD_DIM // 2, 1) * sin).astype(BF16)
        kr = ((kh * cos + pltpu.roll(kh, HEAD_DIM // 2, 1) * sin) * (HEAD_DIM ** -0.5)).astype(BF16)
        vh = z[:, v0 + lo:v0 + lo + HEAD_DIM].astype(BF16)
        g = z[:, g0 + lo:g0 + lo + HEAD_DIM]
        gs = g * jax.nn.sigmoid(g)
        r = r_ref[hh]
        for ci in range(tl // CHUNK):
            rows = slice(ci * CHUNK, (ci + 1) * CHUNK)
            o, r = _ret_chunk(qr[rows], kr[rows], vh[rows], r, d_ref[hh], xi_ref[hh], zeta_ref[hh], gc_ref[hh])
            o_ref[rows, lo:lo + HEAD_DIM] = (gs[rows] * _group_norm(o)).astype(BF16)
        r_ref[hh] = r
    rst_ref[0] = r_ref[...]

    x1 = x + jnp.dot(c, wo_ref[0:CONV_CH, :], preferred_element_type=F32)
    x1 = x1 + jnp.dot(o_ref[...], wo_ref[CONV_CH:, :], preferred_element_type=F32)
    x1_ref[...] = x1
    t = _rms(x1, g2_ref[...])
    t_ref[...] = t.astype(t_ref.dtype)
    rec_ref[...] = _route(_dot_hp(t, wr_ref[0], wr_ref[1]) + br_ref[...])


def _mix(x, g1, w_in, cos, sin, conv_w, conv_b, ln_g, ln_b, w_out, g2, wr2, br, n, l, tl):
    dmat, xi, zeta, gc = _decay_tables(CHUNK)
    per = l // tl
    row = lambda b, j: (b * per + j, 0)
    tab = lambda b, j: (j, 0)
    const = lambda b, j: (0, 0)
    c3 = lambda b, j: (0, 0, 0)
    once = dict(pipeline_mode=pl.Buffered(1))
    return pl.pallas_call(
        _mix_body,
        grid=(n, per),
        in_specs=[
            pl.BlockSpec((tl, D_MODEL), row),
            pl.BlockSpec((1, D_MODEL), const),
            pl.BlockSpec((D_MODEL, IN_COLS), const, **once),
            pl.BlockSpec((tl, HEAD_DIM), tab),
            pl.BlockSpec((tl, HEAD_DIM), tab),
            pl.BlockSpec((CONV_K, CONV_CH), const),
            pl.BlockSpec((1, CONV_CH), const),
            pl.BlockSpec((1, CONV_CH), const),
            pl.BlockSpec((1, CONV_CH), const),
            pl.BlockSpec((RET_HEADS, CHUNK, CHUNK), c3),
            pl.BlockSpec((RET_HEADS, CHUNK, HEAD_DIM), c3),
            pl.BlockSpec((RET_HEADS, CHUNK, HEAD_DIM), c3),
            pl.BlockSpec((RET_HEADS, 1, HEAD_DIM), c3),
            pl.BlockSpec((D_MODEL, D_MODEL), const, **once),
            pl.BlockSpec((1, D_MODEL), const),
            pl.BlockSpec((2, D_MODEL, LANES), c3, **once),
            pl.BlockSpec((1, LANES), const),
        ],
        out_specs=[
            pl.BlockSpec((tl, D_MODEL), row),
            pl.BlockSpec((tl, D_MODEL), row),
            pl.BlockSpec((tl, LANES), row),
            pl.BlockSpec((1, CONV_K - 1, CONV_CH), lambda b, j: (b, 0, 0)),
            pl.BlockSpec((1, RET_HEADS, HEAD_DIM, HEAD_DIM), lambda b, j: (b, 0, 0, 0)),
        ],
        out_shape=[
            jax.ShapeDtypeStruct((n * l, D_MODEL), F32),
            jax.ShapeDtypeStruct((n * l, D_MODEL), BF16),
            jax.ShapeDtypeStruct((n * l, LANES), F32),
            jax.ShapeDtypeStruct((n, CONV_K - 1, CONV_CH), F32),
            jax.ShapeDtypeStruct((n, RET_HEADS, HEAD_DIM, HEAD_DIM), F32),
        ],
        scratch_shapes=[
            pltpu.VMEM((tl + HALO + SUBLANES, CONV_CH), F32),
            pltpu.VMEM((RET_HEADS, HEAD_DIM, HEAD_DIM), F32),
            pltpu.VMEM((tl, RET_WIDTH), BF16),
        ],
        compiler_params=pltpu.CompilerParams(dimension_semantics=("arbitrary", "arbitrary"),
                                             vmem_limit_bytes=MIX_VMEM_LIMIT),
        name="token_mix",
    )(x, g1, w_in, cos, sin, conv_w, conv_b, ln_g, ln_b, dmat, xi, zeta, gc, w_out, g2, wr2, br)


def _iota_f32(shape, dim):
    return lax.broadcasted_iota(jnp.int32, shape, dim).astype(F32)


def _dispatch_body(rec_a_ref, t_a_ref, rec_b_ref, t_b_ref, s_ref, pos_ref, meta_ref, *, nsub_a):
    from_a = pl.program_id(0) < nsub_a
    rec = jnp.where(from_a, rec_a_ref[...], rec_b_ref[...])
    tok = jnp.where(from_a, t_a_ref[...], t_b_ref[...])
    lane = _iota_f32(rec.shape, 1)
    a1 = lane == rec[:, 0:1]
    a2 = lane == rec[:, 1:2]
    a1f = jnp.where(a1, 1.0, 0.0)
    a2f = jnp.where(a2, 1.0, 0.0)
    ltri = jnp.where(_iota_f32((SUB, SUB), 1) < _iota_f32((SUB, SUB), 0), 1.0, 0.0).astype(BF16)
    c1 = jnp.dot(ltri, a1f.astype(BF16), preferred_element_type=F32)
    c2 = jnp.dot(ltri, a2f.astype(BF16), preferred_element_type=F32)
    n1 = jnp.sum(a1f, axis=0, keepdims=True)
    n2 = jnp.sum(a2f, axis=0, keepdims=True)
    cnt = jnp.floor((n1 + n2 + (ROW_ALIGN - 1.0)) * (1.0 / ROW_ALIGN))
    utri = jnp.where(_iota_f32((LANES, LANES), 0) < _iota_f32((LANES, LANES), 1), 1.0, 0.0).astype(BF16)
    start = jnp.dot(jnp.broadcast_to(cnt, (SUBLANES, LANES)).astype(BF16), utri,
                    preferred_element_type=F32)[0:1]
    base1 = start * ROW_ALIGN
    base2 = base1 + n1
    pos1 = jnp.sum(jnp.where(a1, c1 + base1, 0.0), axis=1, keepdims=True)
    pos2 = jnp.sum(jnp.where(a2, c2 + base2, 0.0), axis=1, keepdims=True)
    posm = jnp.where(lane == 0.0, pos1, jnp.where(lane == 1.0, pos2, 0.0))
    pos_ref[...] = posm
    row = _iota_f32((SUBLANES, LANES), 0)
    meta_ref[0] = jnp.where(row == 0.0, start, jnp.where(row == 1.0, cnt, 0.0))

    g1 = _split3(rec[:, 2:3])
    g2 = _split3(rec[:, 3:4])
    info = jnp.where(lane == 6.0, rec[:, 0:1], jnp.where(lane == 7.0, rec[:, 1:2], 0.0))
    for i in range(3):
        info = jnp.where(lane == float(i), g1[i].astype(F32), info)
        info = jnp.where(lane == float(3 + i), g2[i].astype(F32), info)
    src = jnp.concatenate([tok, info.astype(BF16)], axis=1)

    post = posm.T
    p1 = post[0:1, :]
    p2 = post[1:2, :]
    used = jnp.sum(cnt) * ROW_ALIGN
    for b in range(SUBP // PBLK):
        chunks = slice(b * PBLK // ROW_ALIGN, (b + 1) * PBLK // ROW_ALIGN)

        @pl.when(b * PBLK < used)
        def _(b=b, chunks=chunks):
            r = _iota_f32((PBLK, SUB), 0) + float(b * PBLK)
            onehot = jnp.where(r == p1, 1.0, jnp.where(r == p2, 1.0, 0.0)).astype(BF16)
            sorted_rows = jnp.dot(onehot, src, preferred_element_type=F32).astype(BF16)
            s_ref[chunks] = sorted_rows.reshape(PBLK // ROW_ALIGN, ROW_ALIGN, ROW_W)

        @pl.when(b * PBLK >= used)
        def _(chunks=chunks):
            s_ref[chunks] = jnp.zeros((PBLK // ROW_ALIGN, ROW_ALIGN, ROW_W), BF16)


def _dispatch(rec_a, t_a, rec_b, t_b):
    nsub_a = rec_a.shape[0] // SUB
    nsub_b = rec_b.shape[0] // SUB
    nsub = nsub_a + nsub_b
    row = lambda i: (i, 0)
    row_a = lambda i: (jnp.minimum(i, nsub_a - 1), 0)
    row_b = lambda i: (jnp.maximum(i - nsub_a, 0), 0)
    return pl.pallas_call(
        functools.partial(_dispatch_body, nsub_a=nsub_a),
        grid=(nsub,),
        in_specs=[
            pl.BlockSpec((SUB, LANES), row_a),
            pl.BlockSpec((SUB, D_MODEL), row_a),
            pl.BlockSpec((SUB, LANES), row_b),
            pl.BlockSpec((SUB, D_MODEL), row_b),
        ],
        out_specs=[
            pl.BlockSpec((CHUNKS_PER_SUB, ROW_ALIGN, ROW_W), lambda i: (i, 0, 0)),
            pl.BlockSpec((SUB, LANES), row),
            pl.BlockSpec((1, SUBLANES, LANES), lambda i: (i, 0, 0)),
        ],
        out_shape=[
            jax.ShapeDtypeStruct((nsub * CHUNKS_PER_SUB, ROW_ALIGN, ROW_W), BF16),
            jax.ShapeDtypeStruct((nsub * SUB, LANES), F32),
            jax.ShapeDtypeStruct((nsub, SUBLANES, LANES), F32),
        ],
        compiler_params=_cparams(("parallel",)),
        name="moe_dispatch",
    )(rec_a, t_a, rec_b, t_b)


def _experts_body(start_ref, cnt_ref, s_in, wg_ref, wu_ref, wd_ref, s_hbm,
                  xbuf, ybuf, wgu_ref, wdb_ref, gsem, ssem, list_ref, state_ref, *, nsub):
    del s_in
    e = pl.program_id(0)
    ne = pl.num_programs(0)
    par = e & 1

    list_max = list_ref.shape[0] // 2

    def gather_copy(src, i, slot):
        return pltpu.make_async_copy(s_hbm.at[src], xbuf.at[slot, i], gsem.at[slot])

    def scatter_copy(dst, i, slot):
        return pltpu.make_async_copy(ybuf.at[slot, i], s_hbm.at[dst], ssem.at[slot])

    def build_list(x, which):
        def per_sub(s, k):
            run = s * N_EXPERTS + x
            c = cnt_ref[run]
            base = s * CHUNKS_PER_SUB + start_ref[run]
            list_ref[k] = base
            list_ref[k + 1] = base + 1

            def per_chunk(i, carry):
                list_ref[k + i] = base + i
                return carry
            lax.fori_loop(2, c, per_chunk, 0)
            return k + c
        first = which * list_max
        state_ref[which] = lax.fori_loop(0, nsub, per_sub, first) - first

    def start_all(copy, which, first, n, slot, counter):
        def body(i, carry):
            copy(list_ref[which * list_max + first + i], i, slot).start()
            return carry
        lax.fori_loop(0, n, body, 0)
        state_ref[counter] = n

    def wait_all(copy, block_copy, slot, counter):
        n = state_ref[counter]

        @pl.when(n == CHUNKS_PER_BLK)
        def _():
            block_copy(slot).wait()

        @pl.when(n < CHUNKS_PER_BLK)
        def _():
            def body(i, carry):
                copy(0, 0, slot).wait()
                return carry
            lax.fori_loop(0, n, body, 0)
        state_ref[counter] = 0

    def gather_block(slot):
        return pltpu.make_async_copy(s_hbm.at[pl.ds(0, CHUNKS_PER_BLK)], xbuf.at[slot], gsem.at[slot])

    def scatter_block(slot):
        return pltpu.make_async_copy(ybuf.at[slot], s_hbm.at[pl.ds(0, CHUNKS_PER_BLK)], ssem.at[slot])

    def block_chunks(total, b):
        return jnp.minimum(total - b * CHUNKS_PER_BLK, CHUNKS_PER_BLK)

    @pl.when(e == 0)
    def _():
        for i in range(6):
            state_ref[i] = 0
        xbuf[...] = jnp.zeros_like(xbuf)
        build_list(0, 0)
        n0 = state_ref[0]

        @pl.when(n0 > 0)
        def _():
            start_all(gather_copy, 0, 0, block_chunks(n0, 0), 0, 2)

    @pl.when(e + 1 < ne)
    def _():
        build_list(e + 1, 1 - par)

    total = state_ref[par]
    nblk = (total + CHUNKS_PER_BLK - 1) // CHUNKS_PER_BLK
    wgu_ref[:, 0:EXPERT_FF] = wg_ref[0].astype(BF16)
    wgu_ref[:, EXPERT_FF:] = wu_ref[0].astype(BF16)
    wdb_ref[...] = wd_ref[0].astype(BF16)
    ef = e.astype(F32)

    def block(b, carry):
        slot = b & 1
        first = b * CHUNKS_PER_BLK

        @pl.when(b + 1 < nblk)
        def _():
            start_all(gather_copy, par, first + CHUNKS_PER_BLK, block_chunks(total, b + 1), 1 - slot, 3 - slot)

        wait_all(gather_copy, gather_block, slot, 2 + slot)
        wait_all(scatter_copy, scatter_block, slot, 4 + slot)
        x = xbuf[slot].reshape(MBLK, ROW_W)
        info = x[:, D_MODEL:].astype(F32)
        g_first = info[:, 0:1] + info[:, 1:2] + info[:, 2:3]
        g_second = info[:, 3:4] + info[:, 4:5] + info[:, 5:6]
        gate = jnp.where(info[:, 6:7] == ef, g_first, g_second)
        h = jnp.dot(x[:, :D_MODEL], wgu_ref[...], preferred_element_type=F32)
        h1 = h[:, :EXPERT_FF]
        hid = (h1 * jax.nn.sigmoid(h1)) * h[:, EXPERT_FF:] * gate
        y = jnp.dot(hid.astype(BF16), wdb_ref[...], preferred_element_type=F32).astype(BF16)
        ybuf[slot] = jnp.concatenate([y, x[:, D_MODEL:]], axis=1).reshape(CHUNKS_PER_BLK, ROW_ALIGN, ROW_W)
        start_all(scatter_copy, par, first, block_chunks(total, b), slot, 4 + slot)
        return carry

    lax.fori_loop(0, nblk, block, 0)

    @pl.when(e + 1 < ne)
    def _():
        n1 = state_ref[1 - par]

        @pl.when(n1 > 0)
        def _():
            start_all(gather_copy, 1 - par, 0, block_chunks(n1, 0), 0, 2)

    @pl.when(e == ne - 1)
    def _():
        wait_all(scatter_copy, scatter_block, 0, 4)
        wait_all(scatter_copy, scatter_block, 1, 5)


def _experts(start, cnt, staged, wg, wu, wd):
    nsub = staged.shape[0] // CHUNKS_PER_SUB
    list_max = nsub * SUB // ROW_ALIGN + nsub + LIST_SLACK
    wblk = lambda e, *_: (e, 0, 0)
    grid_spec = pltpu.PrefetchScalarGridSpec(
        num_scalar_prefetch=2,
        grid=(N_EXPERTS,),
        in_specs=[
            pl.BlockSpec(memory_space=pl.ANY),
            pl.BlockSpec((1, D_MODEL, EXPERT_FF), wblk),
            pl.BlockSpec((1, D_MODEL, EXPERT_FF), wblk),
            pl.BlockSpec((1, EXPERT_FF, D_MODEL), wblk),
        ],
        out_specs=pl.BlockSpec(memory_space=pl.ANY),
        scratch_shapes=[
            pltpu.VMEM((2, CHUNKS_PER_BLK, ROW_ALIGN, ROW_W), BF16),
            pltpu.VMEM((2, CHUNKS_PER_BLK, ROW_ALIGN, ROW_W), BF16),
            pltpu.VMEM((D_MODEL, 2 * EXPERT_FF), BF16),
            pltpu.VMEM((EXPERT_FF, D_MODEL), BF16),
            pltpu.SemaphoreType.DMA((2,)),
            pltpu.SemaphoreType.DMA((2,)),
            pltpu.SMEM((2 * list_max,), jnp.int32),
            pltpu.SMEM((6,), jnp.int32),
        ],
    )
    return pl.pallas_call(
        functools.partial(_experts_body, nsub=nsub),
        grid_spec=grid_spec,
        out_shape=jax.ShapeDtypeStruct(staged.shape, staged.dtype),
        input_output_aliases={2: 0},
        compiler_params=_cparams(("arbitrary",)),
        name="moe_experts",
    )(start, cnt, staged, wg, wu, wd)


def _combine_body(ys_ref, pos_ref, x1_ref, p_ref, wp_ref, gp_ref, wpg_ref, gf_ref, y_ref):
    p1 = pos_ref[:, 0:1]
    p2 = pos_ref[:, 1:2]
    r = _iota_f32((SUB, SUBP), 1)
    onehot = jnp.where(r == p1, 1.0, jnp.where(r == p2, 1.0, 0.0)).astype(BF16)
    ys = ys_ref[...].reshape(SUBP, D_MODEL)
    x2 = x1_ref[...] + jnp.dot(onehot, ys, preferred_element_type=F32)
    ple = _rms(jnp.dot(p_ref[...].astype(BF16), wp_ref[...], preferred_element_type=F32), gp_ref[...])
    gate = jax.nn.sigmoid(jnp.dot(x2.astype(BF16), wpg_ref[...], preferred_element_type=F32))
    y_ref[...] = _rms(x2 + ple * gate, gf_ref[...])


def _combine(ys, pos, x1, p, w_ple, gp, w_ple_gate, gf, sub_off):
    t = x1.shape[0]
    row = lambda i: (i, 0)
    const = lambda i: (0, 0)
    return pl.pallas_call(
        _combine_body,
        grid=(t // SUB,),
        in_specs=[
            pl.BlockSpec((CHUNKS_PER_SUB, ROW_ALIGN, D_MODEL), lambda i: (i + sub_off, 0, 0)),
            pl.BlockSpec((SUB, LANES), lambda i: (i + sub_off, 0)),
            pl.BlockSpec((SUB, D_MODEL), row),
            pl.BlockSpec((SUB, PLE_DIM), row),
            pl.BlockSpec((PLE_DIM, D_MODEL), const),
            pl.BlockSpec((1, D_MODEL), const),
            pl.BlockSpec((D_MODEL, D_MODEL), const),
            pl.BlockSpec((1, D_MODEL), const),
        ],
        out_specs=pl.BlockSpec((SUB, D_MODEL), row),
        out_shape=jax.ShapeDtypeStruct((t, D_MODEL), F32),
        compiler_params=_cparams(("parallel",)),
        name="moe_combine_ple",
    )(ys, pos, x1, p, w_ple, gp, w_ple_gate, gf)


def _rope_tables(pos):
    half = HEAD_DIM // 2
    inv = ROPE_BASE ** (-jnp.arange(half, dtype=F32) / half)
    ang = pos[:, None] * inv[None, :]
    cos = jnp.cos(ang)
    sin = jnp.sin(ang)
    return jnp.concatenate([cos, cos], axis=-1), jnp.concatenate([-sin, sin], axis=-1)


def _router_params(we, be, wg, bg):
    pad = LANES - N_EXPERTS - N_GROUPS
    w = jnp.pad(jnp.concatenate([we, wg], axis=1), ((0, 0), (0, pad)))
    b = jnp.pad(jnp.concatenate([be, bg]), (0, pad))[None, :]
    return jnp.stack(_split3(w)[:2]), b


def kernel(x_prompt, x_sample, p_prompt, p_sample, state_conv, state_ret, w_in, conv_w, conv_b, conv_ln_g, conv_ln_b, w_out, norm1_g, norm2_g, router_group_w, router_group_b, router_expert_w, router_expert_b, w_expert_gate, w_expert_up, w_expert_down, w_ple, ple_norm_g, w_ple_gate, final_norm_g):
    assert w_in.shape[0] == 1, "single-layer trunk"
    nb, seq, _ = x_prompt.shape
    ns, dseq, _ = x_sample.shape
    tm = 512

    w_in_b = w_in[0].astype(BF16)
    w_out_b = w_out[0].astype(BF16)
    w_ple_b = w_ple[0].astype(BF16)
    w_pg_b = w_ple_gate[0].astype(BF16)
    g1 = norm1_g[0][None, :]
    g2 = norm2_g[0][None, :]
    gp = ple_norm_g[0][None, :]
    gf = final_norm_g[None, :]
    cb = conv_b[0][None, :]
    lng = conv_ln_g[0][None, :]
    lnb = conv_ln_b[0][None, :]
    wr2, br = _router_params(router_expert_w[0], router_expert_b[0], router_group_w[0], router_group_b[0])

    cos_p, sin_p = _rope_tables(jnp.arange(seq, dtype=F32) + jnp.float32(0))
    pos_s = jnp.tile(jnp.arange(dseq, dtype=F32) + jnp.float32(PAST_LEN), tm // dseq)
    cos_s, sin_s = _rope_tables(pos_s)

    xp = x_prompt.reshape(nb * seq, D_MODEL)
    x1_p, t_p, rec_p, conv_p, ret_p = _mix(xp, g1, w_in_b, cos_p, sin_p, conv_w[0], cb, lng, lnb, w_out_b, g2,
                                           wr2, br, nb, seq, tm)

    xs = x_sample.reshape(ns * dseq, D_MODEL)
    u, q, k, v, gs = _inproj(xs, g1, w_in_b, cos_s, sin_s, tm, 1, F32)
    c, conv_s = _conv_sample(u.reshape(ns, dseq, CONV_CH), state_conv[0], conv_w[0], cb, lng, lnb, 16)
    o, ret_s = _ret_sample(q, k, v, gs, state_ret[0], dseq, 8)
    x1_s, t_s, rec_s = _outproj(c.reshape(ns * dseq, CONV_CH), o, xs, w_out_b, g2, wr2, br, tm)

    staged, pos, meta = _dispatch(rec_p, t_p, rec_s, t_s)
    start = meta[:, 0, :N_EXPERTS].astype(jnp.int32).reshape(-1)
    cnt = meta[:, 1, :N_EXPERTS].astype(jnp.int32).reshape(-1)
    ys = _experts(start, cnt, staged, w_expert_gate[0], w_expert_up[0], w_expert_down[0])

    y_p = _combine(ys, pos, x1_p, p_prompt[0].reshape(nb * seq, PLE_DIM), w_ple_b, gp, w_pg_b, gf, 0)
    y_s = _combine(ys, pos, x1_s, p_sample[0].reshape(ns * dseq, PLE_DIM), w_ple_b, gp, w_pg_b, gf,
                   nb * seq // SUB)

    return (y_p.reshape(nb, seq, D_MODEL), y_s.reshape(ns, dseq, D_MODEL),
            conv_p[None], ret_p[None], conv_s[None], ret_s[None])
```

```python
import functools

import jax
import jax.numpy as jnp
from jax import lax
from jax.experimental import pallas as pl
from jax.experimental.pallas import tpu as pltpu

F32 = jnp.float32
BF16 = jnp.bfloat16

D_MODEL = 1024
PLE_DIM = 256
CONV_CH = 512
CONV_K = 31
RET_WIDTH = 512
RET_HEADS = 4
HEAD_DIM = 128
CHUNK = 128
ROPE_BASE = 10000.0
N_GROUPS = 4
EXPERTS_PER_GROUP = 8
N_EXPERTS = 32
EXPERT_FF = 256
IN_COLS = 3072
EPS = 1e-6
PAST_LEN = 16384

LANES = 128
SUBLANES = 8
HALO = 32
HALO_OFF = HALO - (CONV_K - 1)
VMEM_LIMIT = 48 * 1024 * 1024
MIX_VMEM_LIMIT = 56 * 1024 * 1024
MIX_TILE = 1024

SUB = 256
ROW_ALIGN = 16
PBLK = 256
SUBP = -(-(2 * SUB + N_EXPERTS * (ROW_ALIGN - 1)) // PBLK) * PBLK
CHUNKS_PER_SUB = SUBP // ROW_ALIGN
ROW_W = D_MODEL + LANES
MBLK = 512
CHUNKS_PER_BLK = MBLK // ROW_ALIGN
LIST_SLACK = 2
COMBINE_SUBS = 2


def _cparams(sem):
    return pltpu.CompilerParams(dimension_semantics=sem, vmem_limit_bytes=VMEM_LIMIT)


def _rms(x, g):
    return x * lax.rsqrt(jnp.mean(x * x, axis=-1, keepdims=True) + EPS) * g


def _inproj_body(x_ref, g1_ref, w_ref, cos_ref, sin_ref, u_ref, q_ref, k_ref, v_ref, gs_ref):
    h = _rms(x_ref[...], g1_ref[...]).astype(BF16)
    z = jnp.dot(h, w_ref[...], preferred_element_type=F32)
    a = z[:, :CONV_CH]
    b = z[:, CONV_CH:2 * CONV_CH]
    u_ref[...] = a * jax.nn.sigmoid(b)
    cos = cos_ref[...]
    sin = sin_ref[...]
    q0 = 2 * CONV_CH
    k0 = q0 + RET_WIDTH
    for hh in range(RET_HEADS):
        sl = slice(hh * HEAD_DIM, (hh + 1) * HEAD_DIM)
        qh = z[:, q0 + hh * HEAD_DIM:q0 + (hh + 1) * HEAD_DIM]
        kh = z[:, k0 + hh * HEAD_DIM:k0 + (hh + 1) * HEAD_DIM]
        q_ref[:, sl] = (qh * cos + pltpu.roll(qh, HEAD_DIM // 2, 1) * sin).astype(q_ref.dtype)
        kr = (kh * cos + pltpu.roll(kh, HEAD_DIM // 2, 1) * sin) * (HEAD_DIM ** -0.5)
        k_ref[:, sl] = kr.astype(k_ref.dtype)
    v_ref[...] = z[:, k0 + RET_WIDTH:k0 + 2 * RET_WIDTH].astype(v_ref.dtype)
    g = z[:, k0 + 2 * RET_WIDTH:]
    gs_ref[...] = g * jax.nn.sigmoid(g)


def _inproj(x, g1, w_in, cos, sin, tm, table_blocks, qkv_dtype):
    t = x.shape[0]
    row = lambda i: (i, 0)
    const = lambda i: (0, 0)
    tab = (lambda i: (i % table_blocks, 0)) if table_blocks > 1 else const
    return pl.pallas_call(
        _inproj_body,
        grid=(t // tm,),
        in_specs=[
            pl.BlockSpec((tm, D_MODEL), row),
            pl.BlockSpec((1, D_MODEL), const),
            pl.BlockSpec((D_MODEL, IN_COLS), const),
            pl.BlockSpec((tm, HEAD_DIM), tab),
            pl.BlockSpec((tm, HEAD_DIM), tab),
        ],
        out_specs=[pl.BlockSpec((tm, CONV_CH), row)] + [pl.BlockSpec((tm, RET_WIDTH), row)] * 4,
        out_shape=[
            jax.ShapeDtypeStruct((t, CONV_CH), F32),
            jax.ShapeDtypeStruct((t, RET_WIDTH), qkv_dtype),
            jax.ShapeDtypeStruct((t, RET_WIDTH), qkv_dtype),
            jax.ShapeDtypeStruct((t, RET_WIDTH), qkv_dtype),
            jax.ShapeDtypeStruct((t, RET_WIDTH), F32),
        ],
        compiler_params=_cparams(("parallel",)),
        name="inproj",
    )(x, g1, w_in, cos, sin)


def _ln_silu(acc, g, b):
    mu = jnp.mean(acc, axis=-1, keepdims=True)
    d = acc - mu
    var = jnp.mean(d * d, axis=-1, keepdims=True)
    y = d * lax.rsqrt(var + EPS) * g + b
    return y * jax.nn.sigmoid(y)


def _dwconv(load, w_ref, rows, time_axis):
    acc = None
    for b in range(SUBLANES):
        part = None
        for a in range((CONV_K + HALO_OFF) // SUBLANES + 1):
            k = SUBLANES * a + b - HALO_OFF
            if 0 <= k < CONV_K:
                term = load(SUBLANES * a, rows + SUBLANES) * w_ref[k:k + 1, :]
                part = term if part is None else part + term
        if part is not None:
            shifted = lax.slice_in_dim(part, b, b + rows, axis=time_axis)
            acc = shifted if acc is None else acc + shifted
    return acc


def _conv_prompt_body(u_ref, w_ref, cb_ref, lg_ref, lb_ref, c_ref, st_ref, ext_ref):
    j = pl.program_id(1)
    tl = u_ref.shape[1]

    @pl.when(j == 0)
    def _():
        ext_ref[0:HALO, :] = jnp.zeros((HALO, CONV_CH), F32)
        ext_ref[tl + HALO:, :] = jnp.zeros((SUBLANES, CONV_CH), F32)

    @pl.when(j > 0)
    def _():
        ext_ref[0:HALO, :] = ext_ref[tl:tl + HALO, :]

    ext_ref[HALO:tl + HALO, :] = u_ref[0]
    acc = _dwconv(lambda s, n: ext_ref[s:s + n, :], w_ref, tl, 0) + cb_ref[...]
    c_ref[0] = _ln_silu(acc, lg_ref[...], lb_ref[...]).astype(c_ref.dtype)
    st_ref[0] = ext_ref[tl + HALO_OFF:tl + HALO, :]


def _conv_prompt(u, conv_w, conv_b, ln_g, ln_b, tl):
    n, l, _ = u.shape
    const = lambda b, j: (0, 0)
    return pl.pallas_call(
        _conv_prompt_body,
        grid=(n, l // tl),
        in_specs=[
            pl.BlockSpec((1, tl, CONV_CH), lambda b, j: (b, j, 0)),
            pl.BlockSpec((CONV_K, CONV_CH), const),
            pl.BlockSpec((1, CONV_CH), const),
            pl.BlockSpec((1, CONV_CH), const),
            pl.BlockSpec((1, CONV_CH), const),
        ],
        out_specs=[
            pl.BlockSpec((1, tl, CONV_CH), lambda b, j: (b, j, 0)),
            pl.BlockSpec((1, CONV_K - 1, CONV_CH), lambda b, j: (b, 0, 0)),
        ],
        out_shape=[
            jax.ShapeDtypeStruct((n, l, CONV_CH), BF16),
            jax.ShapeDtypeStruct((n, CONV_K - 1, CONV_CH), F32),
        ],
        scratch_shapes=[pltpu.VMEM((tl + HALO + SUBLANES, CONV_CH), F32)],
        compiler_params=_cparams(("arbitrary", "arbitrary")),
        name="conv_prompt",
    )(u, conv_w, conv_b, ln_g, ln_b)


def _conv_sample_body(u_ref, st_ref, w_ref, cb_ref, lg_ref, lb_ref, c_ref, nst_ref, ext_ref):
    nb, l, _ = u_ref.shape
    ext_ref[:, 0:HALO_OFF, :] = jnp.zeros((nb, HALO_OFF, CONV_CH), F32)
    ext_ref[:, HALO_OFF:HALO, :] = st_ref[...]
    ext_ref[:, HALO:l + HALO, :] = u_ref[...]
    ext_ref[:, l + HALO:, :] = jnp.zeros((nb, SUBLANES, CONV_CH), F32)
    acc = _dwconv(lambda s, n: ext_ref[:, s:s + n, :], w_ref, l, 1) + cb_ref[...]
    c_ref[...] = _ln_silu(acc, lg_ref[...], lb_ref[...]).astype(c_ref.dtype)
    nst_ref[...] = ext_ref[:, l + HALO_OFF:l + HALO, :]


def _conv_sample(u, state, conv_w, conv_b, ln_g, ln_b, nb):
    n, l, _ = u.shape
    const = lambda b: (0, 0)
    blk3 = lambda b: (b, 0, 0)
    return pl.pallas_call(
        _conv_sample_body,
        grid=(n // nb,),
        in_specs=[
            pl.BlockSpec((nb, l, CONV_CH), blk3),
            pl.BlockSpec((nb, CONV_K - 1, CONV_CH), blk3),
            pl.BlockSpec((CONV_K, CONV_CH), const),
            pl.BlockSpec((1, CONV_CH), const),
            pl.BlockSpec((1, CONV_CH), const),
            pl.BlockSpec((1, CONV_CH), const),
        ],
        out_specs=[
            pl.BlockSpec((nb, l, CONV_CH), blk3),
            pl.BlockSpec((nb, CONV_K - 1, CONV_CH), blk3),
        ],
        out_shape=[
            jax.ShapeDtypeStruct((n, l, CONV_CH), BF16),
            jax.ShapeDtypeStruct((n, CONV_K - 1, CONV_CH), F32),
        ],
        scratch_shapes=[pltpu.VMEM((nb, l + HALO + SUBLANES, CONV_CH), F32)],
        compiler_params=_cparams(("parallel",)),
        name="conv_sample",
    )(u, state, conv_w, conv_b, ln_g, ln_b)


def _decay_tables(c):
    lg = jnp.log(1.0 - 2.0 ** (-5.0 - jnp.arange(RET_HEADS, dtype=F32)))
    idx = jnp.arange(c, dtype=F32)
    rel = idx[:, None] - idx[None, :]
    dmat = jnp.where(rel[None] >= 0, jnp.exp(jnp.maximum(rel, 0.0)[None] * lg[:, None, None]), 0.0)
    xi = jnp.exp((idx + 1.0)[None, :] * lg[:, None])
    zeta = jnp.exp((c - 1.0 - idx)[None, :] * lg[:, None])
    gc = jnp.exp(c * lg)
    xi_b = jnp.broadcast_to(xi[:, :, None], (RET_HEADS, c, HEAD_DIM))
    zeta_b = jnp.broadcast_to(zeta[:, :, None], (RET_HEADS, c, HEAD_DIM))
    gc_b = jnp.broadcast_to(gc[:, None, None], (RET_HEADS, 1, HEAD_DIM))
    return dmat, xi_b, zeta_b, gc_b


def _group_norm(o):
    mu = jnp.mean(o, axis=-1, keepdims=True)
    d = o - mu
    var = jnp.mean(d * d, axis=-1, keepdims=True)
    return d * lax.rsqrt(var + EPS)


def _ret_chunk(qh, kh, vh, r, dmat, xi, zeta, gc):
    qb = qh.astype(BF16)
    kb = kh.astype(BF16)
    vb = vh.astype(BF16)
    s = lax.dot_general(qb, kb, (((1,), (1,)), ((), ())), preferred_element_type=F32) * dmat
    o = jnp.dot(s.astype(BF16), vb, preferred_element_type=F32)
    o = o + jnp.dot(qb, r.astype(BF16), preferred_element_type=F32) * xi
    kz = (kh.astype(F32) * zeta).astype(BF16)
    r_new = r * gc + lax.dot_general(kz, vb, (((0,), (0,)), ((), ())), preferred_element_type=F32)
    return o, r_new


def _ret_prompt_body(q_ref, k_ref, v_ref, gs_ref, d_ref, xi_ref, zeta_ref, gc_ref, o_ref, st_ref, r_ref):
    j = pl.program_id(1)

    @pl.when(j == 0)
    def _():
        r_ref[...] = jnp.zeros_like(r_ref)

    n_chunks = q_ref.shape[0] // CHUNK
    for hh in range(RET_HEADS):
        sl = slice(hh * HEAD_DIM, (hh + 1) * HEAD_DIM)
        r = r_ref[hh]
        for ci in range(n_chunks):
            rows = slice(ci * CHUNK, (ci + 1) * CHUNK)
            o, r = _ret_chunk(q_ref[rows, sl], k_ref[rows, sl], v_ref[rows, sl], r,
                              d_ref[hh], xi_ref[hh], zeta_ref[hh], gc_ref[hh])
            o_ref[rows, sl] = (gs_ref[rows, sl] * _group_norm(o)).astype(o_ref.dtype)
        r_ref[hh] = r
    st_ref[0] = r_ref[...]


def _ret_prompt(q, k, v, gs, n, l, tl):
    dmat, xi, zeta, gc = _decay_tables(CHUNK)
    per = l // tl
    row = lambda b, j: (b * per + j, 0)
    c3 = lambda b, j: (0, 0, 0)
    return pl.pallas_call(
        _ret_prompt_body,
        grid=(n, per),
        in_specs=[pl.BlockSpec((tl, RET_WIDTH), row)] * 4 + [
            pl.BlockSpec((RET_HEADS, CHUNK, CHUNK), c3),
            pl.BlockSpec((RET_HEADS, CHUNK, HEAD_DIM), c3),
            pl.BlockSpec((RET_HEADS, CHUNK, HEAD_DIM), c3),
            pl.BlockSpec((RET_HEADS, 1, HEAD_DIM), c3),
        ],
        out_specs=[
            pl.BlockSpec((tl, RET_WIDTH), row),
            pl.BlockSpec((1, RET_HEADS, HEAD_DIM, HEAD_DIM), lambda b, j: (b, 0, 0, 0)),
        ],
        out_shape=[
            jax.ShapeDtypeStruct((n * l, RET_WIDTH), BF16),
            jax.ShapeDtypeStruct((n, RET_HEADS, HEAD_DIM, HEAD_DIM), F32),
        ],
        scratch_shapes=[pltpu.VMEM((RET_HEADS, HEAD_DIM, HEAD_DIM), F32)],
        compiler_params=_cparams(("arbitrary", "arbitrary")),
        name="ret_prompt",
    )(q, k, v, gs, dmat, xi, zeta, gc)


def _ret_sample_body(q_ref, k_ref, v_ref, gs_ref, st_ref, d_ref, xi_ref, zeta_ref, gc_ref, o_ref, nst_ref):
    nb = st_ref.shape[0]
    l = q_ref.shape[0] // nb
    for b in range(nb):
        rows = slice(b * l, (b + 1) * l)
        for hh in range(RET_HEADS):
            sl = slice(hh * HEAD_DIM, (hh + 1) * HEAD_DIM)
            o, r = _ret_chunk(q_ref[rows, sl], k_ref[rows, sl], v_ref[rows, sl], st_ref[b, hh],
                              d_ref[hh], xi_ref[hh], zeta_ref[hh], gc_ref[hh])
            o_ref[rows, sl] = (gs_ref[rows, sl] * _group_norm(o)).astype(o_ref.dtype)
            nst_ref[b, hh] = r


def _ret_sample(q, k, v, gs, state, l, nb):
    n = state.shape[0]
    dmat, xi, zeta, gc = _decay_tables(l)
    row = lambda b: (b, 0)
    c3 = lambda b: (0, 0, 0)
    blk4 = lambda b: (b, 0, 0, 0)
    return pl.pallas_call(
        _ret_sample_body,
        grid=(n // nb,),
        in_specs=[pl.BlockSpec((nb * l, RET_WIDTH), row)] * 4 + [
            pl.BlockSpec((nb, RET_HEADS, HEAD_DIM, HEAD_DIM), blk4),
            pl.BlockSpec((RET_HEADS, l, l), c3),
            pl.BlockSpec((RET_HEADS, l, HEAD_DIM), c3),
            pl.BlockSpec((RET_HEADS, l, HEAD_DIM), c3),
            pl.BlockSpec((RET_HEADS, 1, HEAD_DIM), c3),
        ],
        out_specs=[
            pl.BlockSpec((nb * l, RET_WIDTH), row),
            pl.BlockSpec((nb, RET_HEADS, HEAD_DIM, HEAD_DIM), blk4),
        ],
        out_shape=[
            jax.ShapeDtypeStruct((n * l, RET_WIDTH), BF16),
            jax.ShapeDtypeStruct((n, RET_HEADS, HEAD_DIM, HEAD_DIM), F32),
        ],
        compiler_params=_cparams(("parallel",)),
        name="ret_sample",
    )(q, k, v, gs, state, dmat, xi, zeta, gc)


def _split3(x):
    hi = x.astype(BF16)
    r1 = x - hi.astype(F32)
    mid = r1.astype(BF16)
    lo = (r1 - mid.astype(F32)).astype(BF16)
    return hi, mid, lo


def _dot_hp(t, w_hi, w_mid):
    t_hi, t_mid, _ = _split3(t)
    d = functools.partial(jnp.dot, preferred_element_type=F32)
    return d(t_hi, w_hi) + (d(t_mid, w_hi) + d(t_hi, w_mid))


def _route(logits):
    lane = lax.broadcasted_iota(jnp.int32, logits.shape, 1)
    big = jnp.int32(LANES)
    is_group = (lane >= N_EXPERTS) & (lane < N_EXPERTS + N_GROUPS)
    lg = jnp.where(is_group, logits, -1e30)
    m = jnp.max(lg, axis=-1, keepdims=True)
    g_top = 1.0 / jnp.sum(jnp.exp(lg - m), axis=-1, keepdims=True)
    g_idx = jnp.min(jnp.where(lg == m, lane, big), axis=-1, keepdims=True) - N_EXPERTS
    in_group = (lane >= g_idx * EXPERTS_PER_GROUP) & (lane < (g_idx + 1) * EXPERTS_PER_GROUP)
    lem = jnp.where(in_group, logits, -1e30)
    m2 = jnp.max(lem, axis=-1, keepdims=True)
    pe = jnp.where(in_group, jnp.exp(lem - m2), 0.0)
    p1 = jnp.max(pe, axis=-1, keepdims=True)
    e1 = jnp.min(jnp.where(in_group & (pe == p1), lane, big), axis=-1, keepdims=True)
    rest = in_group & (lane != e1)
    pe2 = jnp.where(rest, pe, -1.0)
    p2 = jnp.max(pe2, axis=-1, keepdims=True)
    e2 = jnp.min(jnp.where(rest & (pe2 == p2), lane, big), axis=-1, keepdims=True)
    scale = g_top / (p1 + p2)
    rec = jnp.where(lane == 0, e1.astype(F32), jnp.where(lane == 1, e2.astype(F32), 0.0))
    return rec + jnp.where(lane == 2, p1 * scale, jnp.where(lane == 3, p2 * scale, 0.0))


def _outproj_body(c_ref, o_ref, x_ref, wo_ref, g2_ref, wr_ref, br_ref, x1_ref, t_ref, rec_ref):
    x1 = x_ref[...] + jnp.dot(c_ref[...], wo_ref[0:CONV_CH, :], preferred_element_type=F32)
    x1 = x1 + jnp.dot(o_ref[...], wo_ref[CONV_CH:, :], preferred_element_type=F32)
    x1_ref[...] = x1
    t = _rms(x1, g2_ref[...])
    t_ref[...] = t.astype(t_ref.dtype)
    rec_ref[...] = _route(_dot_hp(t, wr_ref[0], wr_ref[1]) + br_ref[...])


def _outproj(c, o, x, w_out, g2, wr2, br, tm):
    t = x.shape[0]
    row = lambda i: (i, 0)
    const = lambda i: (0, 0)
    c3 = lambda i: (0, 0, 0)
    return pl.pallas_call(
        _outproj_body,
        grid=(t // tm,),
        in_specs=[
            pl.BlockSpec((tm, CONV_CH), row),
            pl.BlockSpec((tm, RET_WIDTH), row),
            pl.BlockSpec((tm, D_MODEL), row),
            pl.BlockSpec((D_MODEL, D_MODEL), const),
            pl.BlockSpec((1, D_MODEL), const),
            pl.BlockSpec((2, D_MODEL, LANES), c3),
            pl.BlockSpec((1, LANES), const),
        ],
        out_specs=[
            pl.BlockSpec((tm, D_MODEL), row),
            pl.BlockSpec((tm, D_MODEL), row),
            pl.BlockSpec((tm, LANES), row),
        ],
        out_shape=[
            jax.ShapeDtypeStruct((t, D_MODEL), F32),
            jax.ShapeDtypeStruct((t, D_MODEL), BF16),
            jax.ShapeDtypeStruct((t, LANES), F32),
        ],
        compiler_params=_cparams(("parallel",)),
        name="outproj_router",
    )(c, o, x, w_out, g2, wr2, br)


def _mix_body(x_ref, g1_ref, w_ref, cos_ref, sin_ref, cw_ref, cb_ref, lg_ref, lb_ref,
              d_ref, xi_ref, zeta_ref, gc_ref, wo_ref, g2_ref, wr_ref, br_ref,
              x1_ref, t_ref, rec_ref, cst_ref, rst_ref, ext_ref, r_ref, o_ref):
    j = pl.program_id(1)
    tl = x_ref.shape[0]
    x = x_ref[...]
    z = jnp.dot(_rms(x, g1_ref[...]).astype(BF16), w_ref[...], preferred_element_type=F32)

    @pl.when(j == 0)
    def _():
        ext_ref[0:HALO, :] = jnp.zeros((HALO, CONV_CH), F32)
        ext_ref[tl + HALO:, :] = jnp.zeros((SUBLANES, CONV_CH), F32)
        r_ref[...] = jnp.zeros_like(r_ref)

    @pl.when(j > 0)
    def _():
        ext_ref[0:HALO, :] = ext_ref[tl:tl + HALO, :]

    ext_ref[HALO:tl + HALO, :] = z[:, :CONV_CH] * jax.nn.sigmoid(z[:, CONV_CH:2 * CONV_CH])
    acc = _dwconv(lambda s, n: ext_ref[s:s + n, :], cw_ref, tl, 0) + cb_ref[...]
    c = _ln_silu(acc, lg_ref[...], lb_ref[...]).astype(BF16)
    cst_ref[0] = ext_ref[tl + HALO_OFF:tl + HALO, :]

    cos = cos_ref[...]
    sin = sin_ref[...]
    q0 = 2 * CONV_CH
    k0 = q0 + RET_WIDTH
    v0 = k0 + RET_WIDTH
    g0 = v0 + RET_WIDTH
    for hh in range(RET_HEADS):
        lo = hh * HEAD_DIM
        qh = z[:, q0 + lo:q0 + lo + HEAD_DIM]
        kh = z[:, k0 + lo:k0 + lo + HEAD_DIM]
        qr = (qh * cos + pltpu.roll(qh, HEAD_DIM // 2, 1) * sin).astype(BF16)
        kr = ((kh * cos + pltpu.roll(kh, HEAD_DIM // 2, 1) * sin) * (HEAD_DIM ** -0.5)).astype(BF16)
        vh = z[:, v0 + lo:v0 + lo + HEAD_DIM].astype(BF16)
        g = z[:, g0 + lo:g0 + lo + HEAD_DIM]
        gs = g * jax.nn.sigmoid(g)
        r = r_ref[hh]
        for ci in range(tl // CHUNK):
            rows = slice(ci * CHUNK, (ci + 1) * CHUNK)
            o, r = _ret_chunk(qr[rows], kr[rows], vh[rows], r, d_ref[hh], xi_ref[hh], zeta_ref[hh], gc_ref[hh])
            o_ref[rows, lo:lo + HEAD_DIM] = (gs[rows] * _group_norm(o)).astype(BF16)
        r_ref[hh] = r
    rst_ref[0] = r_ref[...]

    x1 = x + jnp.dot(c, wo_ref[0:CONV_CH, :], preferred_element_type=F32)
    x1 = x1 + jnp.dot(o_ref[...], wo_ref[CONV_CH:, :], preferred_element_type=F32)
    x1_ref[...] = x1
    t = _rms(x1, g2_ref[...])
    t_ref[...] = t.astype(t_ref.dtype)
    rec_ref[...] = _route(_dot_hp(t, wr_ref[0], wr_ref[1]) + br_ref[...])


def _mix(x, g1, w_in, cos, sin, conv_w, conv_b, ln_g, ln_b, w_out, g2, wr2, br, n, l, tl):
    dmat, xi, zeta, gc = _decay_tables(CHUNK)
    per = l // tl
    row = lambda b, j: (b * per + j, 0)
    tab = lambda b, j: (j, 0)
    const = lambda b, j: (0, 0)
    c3 = lambda b, j: (0, 0, 0)
    once = dict(pipeline_mode=pl.Buffered(1))
    return pl.pallas_call(
        _mix_body,
        grid=(n, per),
        in_specs=[
            pl.BlockSpec((tl, D_MODEL), row),
            pl.BlockSpec((1, D_MODEL), const),
            pl.BlockSpec((D_MODEL, IN_COLS), const, **once),
            pl.BlockSpec((tl, HEAD_DIM), tab),
            pl.BlockSpec((tl, HEAD_DIM), tab),
            pl.BlockSpec((CONV_K, CONV_CH), const),
            pl.BlockSpec((1, CONV_CH), const),
            pl.BlockSpec((1, CONV_CH), const),
            pl.BlockSpec((1, CONV_CH), const),
            pl.BlockSpec((RET_HEADS, CHUNK, CHUNK), c3),
            pl.BlockSpec((RET_HEADS, CHUNK, HEAD_DIM), c3),
            pl.BlockSpec((RET_HEADS, CHUNK, HEAD_DIM), c3),
            pl.BlockSpec((RET_HEADS, 1, HEAD_DIM), c3),
            pl.BlockSpec((D_MODEL, D_MODEL), const, **once),
            pl.BlockSpec((1, D_MODEL), const),
            pl.BlockSpec((2, D_MODEL, LANES), c3, **once),
            pl.BlockSpec((1, LANES), const),
        ],
        out_specs=[
            pl.BlockSpec((tl, D_MODEL), row),
            pl.BlockSpec((tl, D_MODEL), row),
            pl.BlockSpec((tl, LANES), row),
            pl.BlockSpec((1, CONV_K - 1, CONV_CH), lambda b, j: (b, 0, 0)),
            pl.BlockSpec((1, RET_HEADS, HEAD_DIM, HEAD_DIM), lambda b, j: (b, 0, 0, 0)),
        ],
        out_shape=[
            jax.ShapeDtypeStruct((n * l, D_MODEL), F32),
            jax.ShapeDtypeStruct((n * l, D_MODEL), BF16),
            jax.ShapeDtypeStruct((n * l, LANES), F32),
            jax.ShapeDtypeStruct((n, CONV_K - 1, CONV_CH), F32),
            jax.ShapeDtypeStruct((n, RET_HEADS, HEAD_DIM, HEAD_DIM), F32),
        ],
        scratch_shapes=[
            pltpu.VMEM((tl + HALO + SUBLANES, CONV_CH), F32),
            pltpu.VMEM((RET_HEADS, HEAD_DIM, HEAD_DIM), F32),
            pltpu.VMEM((tl, RET_WIDTH), BF16),
        ],
        compiler_params=pltpu.CompilerParams(dimension_semantics=("arbitrary", "arbitrary"),
                                             vmem_limit_bytes=MIX_VMEM_LIMIT),
        name="token_mix",
    )(x, g1, w_in, cos, sin, conv_w, conv_b, ln_g, ln_b, dmat, xi, zeta, gc, w_out, g2, wr2, br)


def _iota_f32(shape, dim):
    return lax.broadcasted_iota(jnp.int32, shape, dim).astype(F32)


def _dispatch_body(rec_a_ref, t_a_ref, rec_b_ref, t_b_ref, s_ref, pos_ref, meta_ref, *, nsub_a):
    from_a = pl.program_id(0) < nsub_a
    rec = jnp.where(from_a, rec_a_ref[...], rec_b_ref[...])
    tok = jnp.where(from_a, t_a_ref[...], t_b_ref[...])
    lane = _iota_f32(rec.shape, 1)
    a1 = lane == rec[:, 0:1]
    a2 = lane == rec[:, 1:2]
    a1f = jnp.where(a1, 1.0, 0.0)
    a2f = jnp.where(a2, 1.0, 0.0)
    ltri = jnp.where(_iota_f32((SUB, SUB), 1) < _iota_f32((SUB, SUB), 0), 1.0, 0.0).astype(BF16)
    c1 = jnp.dot(ltri, a1f.astype(BF16), preferred_element_type=F32)
    c2 = jnp.dot(ltri, a2f.astype(BF16), preferred_element_type=F32)
    n1 = jnp.sum(a1f, axis=0, keepdims=True)
    n2 = jnp.sum(a2f, axis=0, keepdims=True)
    cnt = jnp.floor((n1 + n2 + (ROW_ALIGN - 1.0)) * (1.0 / ROW_ALIGN))
    utri = jnp.where(_iota_f32((LANES, LANES), 0) < _iota_f32((LANES, LANES), 1), 1.0, 0.0).astype(BF16)
    start = jnp.dot(jnp.broadcast_to(cnt, (SUBLANES, LANES)).astype(BF16), utri,
                    preferred_element_type=F32)[0:1]
    base1 = start * ROW_ALIGN
    base2 = base1 + n1
    pos1 = jnp.sum(jnp.where(a1, c1 + base1, 0.0), axis=1, keepdims=True)
    pos2 = jnp.sum(jnp.where(a2, c2 + base2, 0.0), axis=1, keepdims=True)
    posm = jnp.where(lane == 0.0, pos1, jnp.where(lane == 1.0, pos2, 0.0))
    pos_ref[...] = posm
    row = _iota_f32((SUBLANES, LANES), 0)
    meta_ref[0] = jnp.where(row == 0.0, start, jnp.where(row == 1.0, cnt, 0.0))

    g1 = _split3(rec[:, 2:3])
    g2 = _split3(rec[:, 3:4])
    info = jnp.where(lane == 6.0, rec[:, 0:1], jnp.where(lane == 7.0, rec[:, 1:2], 0.0))
    for i in range(3):
        info = jnp.where(lane == float(i), g1[i].astype(F32), info)
        info = jnp.where(lane == float(3 + i), g2[i].astype(F32), info)
    src = jnp.concatenate([tok, info.astype(BF16)], axis=1)

    post = posm.T
    p1 = post[0:1, :]
    p2 = post[1:2, :]
    used = jnp.sum(cnt) * ROW_ALIGN
    for b in range(SUBP // PBLK):
        chunks = slice(b * PBLK // ROW_ALIGN, (b + 1) * PBLK // ROW_ALIGN)

        @pl.when(b * PBLK < used)
        def _(b=b, chunks=chunks):
            r = _iota_f32((PBLK, SUB), 0) + float(b * PBLK)
            onehot = jnp.where(r == p1, 1.0, jnp.where(r == p2, 1.0, 0.0)).astype(BF16)
            sorted_rows = jnp.dot(onehot, src, preferred_element_type=F32).astype(BF16)
            s_ref[chunks] = sorted_rows.reshape(PBLK // ROW_ALIGN, ROW_ALIGN, ROW_W)

        @pl.when(b * PBLK >= used)
        def _(chunks=chunks):
            s_ref[chunks] = jnp.zeros((PBLK // ROW_ALIGN, ROW_ALIGN, ROW_W), BF16)


def _dispatch(rec_a, t_a, rec_b, t_b):
    nsub_a = rec_a.shape[0] // SUB
    nsub_b = rec_b.shape[0] // SUB
    nsub = nsub_a + nsub_b
    row = lambda i: (i, 0)
    row_a = lambda i: (jnp.minimum(i, nsub_a - 1), 0)
    row_b = lambda i: (jnp.maximum(i - nsub_a, 0), 0)
    return pl.pallas_call(
        functools.partial(_dispatch_body, nsub_a=nsub_a),
        grid=(nsub,),
        in_specs=[
            pl.BlockSpec((SUB, LANES), row_a),
            pl.BlockSpec((SUB, D_MODEL), row_a),
            pl.BlockSpec((SUB, LANES), row_b),
            pl.BlockSpec((SUB, D_MODEL), row_b),
        ],
        out_specs=[
            pl.BlockSpec((CHUNKS_PER_SUB, ROW_ALIGN, ROW_W), lambda i: (i, 0, 0)),
            pl.BlockSpec((SUB, LANES), row),
            pl.BlockSpec((1, SUBLANES, LANES), lambda i: (i, 0, 0)),
        ],
        out_shape=[
            jax.ShapeDtypeStruct((nsub * CHUNKS_PER_SUB, ROW_ALIGN, ROW_W), BF16),
            jax.ShapeDtypeStruct((nsub * SUB, LANES), F32),
            jax.ShapeDtypeStruct((nsub, SUBLANES, LANES), F32),
        ],
        compiler_params=_cparams(("parallel",)),
        name="moe_dispatch",
    )(rec_a, t_a, rec_b, t_b)


def _experts_body(start_ref, cnt_ref, s_in, wg_ref, wu_ref, wd_ref, s_hbm,
                  xbuf, ybuf, wgu_ref, wdb_ref, gsem, ssem, list_ref, state_ref, *, nsub):
    del s_in
    e = pl.program_id(0)
    ne = pl.num_programs(0)
    par = e & 1

    list_max = list_ref.shape[0] // 2

    def gather_copy(src, i, slot):
        return pltpu.make_async_copy(s_hbm.at[src], xbuf.at[slot, i], gsem.at[slot])

    def scatter_copy(dst, i, slot):
        return pltpu.make_async_copy(ybuf.at[slot, i], s_hbm.at[dst], ssem.at[slot])

    def build_list(x, which):
        def per_sub(s, k):
            run = s * N_EXPERTS + x
            c = cnt_ref[run]
            base = s * CHUNKS_PER_SUB + start_ref[run]
            list_ref[k] = base
            list_ref[k + 1] = base + 1

            def per_chunk(i, carry):
                list_ref[k + i] = base + i
                return carry
            lax.fori_loop(2, c, per_chunk, 0)
            return k + c
        first = which * list_max
        state_ref[which] = lax.fori_loop(0, nsub, per_sub, first) - first

    def start_all(copy, which, first, n, slot, counter):
        def body(i, carry):
            copy(list_ref[which * list_max + first + i], i, slot).start()
            return carry
        lax.fori_loop(0, n, body, 0)
        state_ref[counter] = n

    def wait_all(copy, block_copy, slot, counter):
        n = state_ref[counter]

        @pl.when(n == CHUNKS_PER_BLK)
        def _():
            block_copy(slot).wait()

        @pl.when(n < CHUNKS_PER_BLK)
        def _():
            def body(i, carry):
                copy(0, 0, slot).wait()
                return carry
            lax.fori_loop(0, n, body, 0)
        state_ref[counter] = 0

    def gather_block(slot):
        return pltpu.make_async_copy(s_hbm.at[pl.ds(0, CHUNKS_PER_BLK)], xbuf.at[slot], gsem.at[slot])

    def scatter_block(slot):
        return pltpu.make_async_copy(ybuf.at[slot], s_hbm.at[pl.ds(0, CHUNKS_PER_BLK)], ssem.at[slot])

    def block_chunks(total, b):
        return jnp.minimum(total - b * CHUNKS_PER_BLK, CHUNKS_PER_BLK)

    @pl.when(e == 0)
    def _():
        for i in range(6):
            state_ref[i] = 0
        xbuf[...] = jnp.zeros_like(xbuf)
        build_list(0, 0)
        n0 = state_ref[0]

        @pl.when(n0 > 0)
        def _():
            start_all(gather_copy, 0, 0, block_chunks(n0, 0), 0, 2)

    @pl.when(e + 1 < ne)
    def _():
        build_list(e + 1, 1 - par)

    total = state_ref[par]
    nblk = (total + CHUNKS_PER_BLK - 1) // CHUNKS_PER_BLK
    wgu_ref[:, 0:EXPERT_FF] = wg_ref[0].astype(BF16)
    wgu_ref[:, EXPERT_FF:] = wu_ref[0].astype(BF16)
    wdb_ref[...] = wd_ref[0].astype(BF16)
    ef = e.astype(F32)

    def block(b, carry):
        slot = b & 1
        first = b * CHUNKS_PER_BLK

        @pl.when(b + 1 < nblk)
        def _():
            start_all(gather_copy, par, first + CHUNKS_PER_BLK, block_chunks(total, b + 1), 1 - slot, 3 - slot)

        wait_all(gather_copy, gather_block, slot, 2 + slot)
        wait_all(scatter_copy, scatter_block, slot, 4 + slot)
        x = xbuf[slot].reshape(MBLK, ROW_W)
        info = x[:, D_MODEL:].astype(F32)
        g_first = info[:, 0:1] + info[:, 1:2] + info[:, 2:3]
        g_second = info[:, 3:4] + info[:, 4:5] + info[:, 5:6]
        gate = jnp.where(info[:, 6:7] == ef, g_first, g_second)
        h = jnp.dot(x[:, :D_MODEL], wgu_ref[...], preferred_element_type=F32)
        h1 = h[:, :EXPERT_FF]
        hid = (h1 * jax.nn.sigmoid(h1)) * h[:, EXPERT_FF:] * gate
        y = jnp.dot(hid.astype(BF16), wdb_ref[...], preferred_element_type=F32).astype(BF16)
        ybuf[slot] = jnp.concatenate([y, x[:, D_MODEL:]], axis=1).reshape(CHUNKS_PER_BLK, ROW_ALIGN, ROW_W)
        start_all(scatter_copy, par, first, block_chunks(total, b), slot, 4 + slot)
        return carry

    lax.fori_loop(0, nblk, block, 0)

    @pl.when(e + 1 < ne)
    def _():
        n1 = state_ref[1 - par]

        @pl.when(n1 > 0)
        def _():
            start_all(gather_copy, 1 - par, 0, block_chunks(n1, 0), 0, 2)

    @pl.when(e == ne - 1)
    def _():
        wait_all(scatter_copy, scatter_block, 0, 4)
        wait_all(scatter_copy, scatter_block, 1, 5)


def _experts(start, cnt, staged, wg, wu, wd):
    nsub = staged.shape[0] // CHUNKS_PER_SUB
    list_max = nsub * SUB // ROW_ALIGN + nsub + LIST_SLACK
    wblk = lambda e, *_: (e, 0, 0)
    grid_spec = pltpu.PrefetchScalarGridSpec(
        num_scalar_prefetch=2,
        grid=(N_EXPERTS,),
        in_specs=[
            pl.BlockSpec(memory_space=pl.ANY),
            pl.BlockSpec((1, D_MODEL, EXPERT_FF), wblk),
            pl.BlockSpec((1, D_MODEL, EXPERT_FF), wblk),
            pl.BlockSpec((1, EXPERT_FF, D_MODEL), wblk),
        ],
        out_specs=pl.BlockSpec(memory_space=pl.ANY),
        scratch_shapes=[
            pltpu.VMEM((2, CHUNKS_PER_BLK, ROW_ALIGN, ROW_W), BF16),
            pltpu.VMEM((2, CHUNKS_PER_BLK, ROW_ALIGN, ROW_W), BF16),
            pltpu.VMEM((D_MODEL, 2 * EXPERT_FF), BF16),
            pltpu.VMEM((EXPERT_FF, D_MODEL), BF16),
            pltpu.SemaphoreType.DMA((2,)),
            pltpu.SemaphoreType.DMA((2,)),
            pltpu.SMEM((2 * list_max,), jnp.int32),
            pltpu.SMEM((6,), jnp.int32),
        ],
    )
    return pl.pallas_call(
        functools.partial(_experts_body, nsub=nsub),
        grid_spec=grid_spec,
        out_shape=jax.ShapeDtypeStruct(staged.shape, staged.dtype),
        input_output_aliases={2: 0},
        compiler_params=_cparams(("arbitrary",)),
        name="moe_experts",
    )(start, cnt, staged, wg, wu, wd)


def _combine_body(ys_ref, pos_ref, x1_ref, p_ref, wp_ref, gp_ref, wpg_ref, gf_ref, y_ref):
    r = _iota_f32((SUB, SUBP), 1)
    moe = []
    for s in range(x1_ref.shape[0] // SUB):
        p1 = pos_ref[s * SUB:(s + 1) * SUB, 0:1]
        p2 = pos_ref[s * SUB:(s + 1) * SUB, 1:2]
        onehot = jnp.where(r == p1, 1.0, jnp.where(r == p2, 1.0, 0.0)).astype(BF16)
        ys = ys_ref[s * CHUNKS_PER_SUB:(s + 1) * CHUNKS_PER_SUB].reshape(SUBP, D_MODEL)
        moe.append(jnp.dot(onehot, ys, preferred_element_type=F32))
    x2 = x1_ref[...] + jnp.concatenate(moe, axis=0)
    ple = _rms(jnp.dot(p_ref[...].astype(BF16), wp_ref[...], preferred_element_type=F32), gp_ref[...])
    gate = jax.nn.sigmoid(jnp.dot(x2.astype(BF16), wpg_ref[...], preferred_element_type=F32))
    y_ref[...] = _rms(x2 + ple * gate, gf_ref[...])


def _combine(ys, pos, x1, p, w_ple, gp, w_ple_gate, gf, sub_off):
    t = x1.shape[0]
    tm = COMBINE_SUBS * SUB
    blk_off = sub_off // COMBINE_SUBS
    assert sub_off % COMBINE_SUBS == 0 and t % tm == 0
    row = lambda i: (i, 0)
    const = lambda i: (0, 0)
    return pl.pallas_call(
        _combine_body,
        grid=(t // tm,),
        in_specs=[
            pl.BlockSpec((COMBINE_SUBS * CHUNKS_PER_SUB, ROW_ALIGN, D_MODEL), lambda i: (i + blk_off, 0, 0)),
            pl.BlockSpec((tm, LANES), lambda i: (i + blk_off, 0)),
            pl.BlockSpec((tm, D_MODEL), row),
            pl.BlockSpec((tm, PLE_DIM), row),
            pl.BlockSpec((PLE_DIM, D_MODEL), const),
            pl.BlockSpec((1, D_MODEL), const),
            pl.BlockSpec((D_MODEL, D_MODEL), const),
            pl.BlockSpec((1, D_MODEL), const),
        ],
        out_specs=pl.BlockSpec((tm, D_MODEL), row),
        out_shape=jax.ShapeDtypeStruct((t, D_MODEL), F32),
        compiler_params=_cparams(("parallel",)),
        name="moe_combine_ple",
    )(ys, pos, x1, p, w_ple, gp, w_ple_gate, gf)


def _rope_tables(pos):
    half = HEAD_DIM // 2
    inv = ROPE_BASE ** (-jnp.arange(half, dtype=F32) / half)
    ang = pos[:, None] * inv[None, :]
    cos = jnp.cos(ang)
    sin = jnp.sin(ang)
    return jnp.concatenate([cos, cos], axis=-1), jnp.concatenate([-sin, sin], axis=-1)


def _router_params(we, be, wg, bg):
    pad = LANES - N_EXPERTS - N_GROUPS
    w = jnp.pad(jnp.concatenate([we, wg], axis=1), ((0, 0), (0, pad)))
    b = jnp.pad(jnp.concatenate([be, bg]), (0, pad))[None, :]
    return jnp.stack(_split3(w)[:2]), b


def kernel(x_prompt, x_sample, p_prompt, p_sample, state_conv, state_ret, w_in, conv_w, conv_b, conv_ln_g, conv_ln_b, w_out, norm1_g, norm2_g, router_group_w, router_group_b, router_expert_w, router_expert_b, w_expert_gate, w_expert_up, w_expert_down, w_ple, ple_norm_g, w_ple_gate, final_norm_g):
    assert w_in.shape[0] == 1, "single-layer trunk"
    nb, seq, _ = x_prompt.shape
    ns, dseq, _ = x_sample.shape
    tm = 512

    w_in_b = w_in[0].astype(BF16)
    w_out_b = w_out[0].astype(BF16)
    w_ple_b = w_ple[0].astype(BF16)
    w_pg_b = w_ple_gate[0].astype(BF16)
    g1 = norm1_g[0][None, :]
    g2 = norm2_g[0][None, :]
    gp = ple_norm_g[0][None, :]
    gf = final_norm_g[None, :]
    cb = conv_b[0][None, :]
    lng = conv_ln_g[0][None, :]
    lnb = conv_ln_b[0][None, :]
    wr2, br = _router_params(router_expert_w[0], router_expert_b[0], router_group_w[0], router_group_b[0])

    cos_p, sin_p = _rope_tables(jnp.arange(seq, dtype=F32) + jnp.float32(0))
    pos_s = jnp.tile(jnp.arange(dseq, dtype=F32) + jnp.float32(PAST_LEN), tm // dseq)
    cos_s, sin_s = _rope_tables(pos_s)

    xp = x_prompt.reshape(nb * seq, D_MODEL)
    x1_p, t_p, rec_p, conv_p, ret_p = _mix(xp, g1, w_in_b, cos_p, sin_p, conv_w[0], cb, lng, lnb, w_out_b, g2,
                                           wr2, br, nb, seq, MIX_TILE)

    xs = x_sample.reshape(ns * dseq, D_MODEL)
    u, q, k, v, gs = _inproj(xs, g1, w_in_b, cos_s, sin_s, tm, 1, F32)
    c, conv_s = _conv_sample(u.reshape(ns, dseq, CONV_CH), state_conv[0], conv_w[0], cb, lng, lnb, 16)
    o, ret_s = _ret_sample(q, k, v, gs, state_ret[0], dseq, 8)
    x1_s, t_s, rec_s = _outproj(c.reshape(ns * dseq, CONV_CH), o, xs, w_out_b, g2, wr2, br, tm)

    staged, pos, meta = _dispatch(rec_p, t_p, rec_s, t_s)
    start = meta[:, 0, :N_EXPERTS].astype(jnp.int32).reshape(-1)
    cnt = meta[:, 1, :N_EXPERTS].astype(jnp.int32).reshape(-1)
    ys = _experts(start, cnt, staged, w_expert_gate[0], w_expert_up[0], w_expert_down[0])

    y_p = _combine(ys, pos, x1_p, p_prompt[0].reshape(nb * seq, PLE_DIM), w_ple_b, gp, w_pg_b, gf, 0)
    y_s = _combine(ys, pos, x1_s, p_sample[0].reshape(ns * dseq, PLE_DIM), w_ple_b, gp, w_pg_b, gf,
                   nb * seq // SUB)

    return (y_p.reshape(nb, seq, D_MODEL), y_s.reshape(ns, dseq, D_MODEL),
            conv_p[None], ret_p[None], conv_s[None], ret_s[None])
```

```python
import functools

import jax
import jax.numpy as jnp
from jax import lax
from jax.experimental import pallas as pl
from jax.experimental.pallas import tpu as pltpu

F32 = jnp.float32
BF16 = jnp.bfloat16

D_MODEL = 1024
PLE_DIM = 256
CONV_CH = 512
CONV_K = 31
RET_WIDTH = 512
RET_HEADS = 4
HEAD_DIM = 128
CHUNK = 128
ROPE_BASE = 10000.0
N_GROUPS = 4
EXPERTS_PER_GROUP = 8
N_EXPERTS = 32
EXPERT_FF = 256
IN_COLS = 3072
EPS = 1e-6
PAST_LEN = 16384

LANES = 128
SUBLANES = 8
HALO = 32
HALO_OFF = HALO - (CONV_K - 1)
VMEM_LIMIT = 48 * 1024 * 1024
MIX_VMEM_LIMIT = 56 * 1024 * 1024
MIX_TILE = 512

SUB = 256
ROW_ALIGN = 16
PBLK = 256
SUBP = -(-(2 * SUB + N_EXPERTS * (ROW_ALIGN - 1)) // PBLK) * PBLK
CHUNKS_PER_SUB = SUBP // ROW_ALIGN
ROW_W = D_MODEL + LANES
MBLK = 512
CHUNKS_PER_BLK = MBLK // ROW_ALIGN
LIST_SLACK = 2
COMBINE_SUBS = 4
DISPATCH_SUBS = 2


def _cparams(sem):
    return pltpu.CompilerParams(dimension_semantics=sem, vmem_limit_bytes=VMEM_LIMIT)


def _rms(x, g):
    return x * lax.rsqrt(jnp.mean(x * x, axis=-1, keepdims=True) + EPS) * g


def _inproj_body(x_ref, g1_ref, w_ref, cos_ref, sin_ref, u_ref, q_ref, k_ref, v_ref, gs_ref):
    h = _rms(x_ref[...], g1_ref[...]).astype(BF16)
    z = jnp.dot(h, w_ref[...], preferred_element_type=F32)
    a = z[:, :CONV_CH]
    b = z[:, CONV_CH:2 * CONV_CH]
    u_ref[...] = a * jax.nn.sigmoid(b)
    cos = cos_ref[...]
    sin = sin_ref[...]
    q0 = 2 * CONV_CH
    k0 = q0 + RET_WIDTH
    for hh in range(RET_HEADS):
        sl = slice(hh * HEAD_DIM, (hh + 1) * HEAD_DIM)
        qh = z[:, q0 + hh * HEAD_DIM:q0 + (hh + 1) * HEAD_DIM]
        kh = z[:, k0 + hh * HEAD_DIM:k0 + (hh + 1) * HEAD_DIM]
        q_ref[:, sl] = (qh * cos + pltpu.roll(qh, HEAD_DIM // 2, 1) * sin).astype(q_ref.dtype)
        kr = (kh * cos + pltpu.roll(kh, HEAD_DIM // 2, 1) * sin) * (HEAD_DIM ** -0.5)
        k_ref[:, sl] = kr.astype(k_ref.dtype)
    v_ref[...] = z[:, k0 + RET_WIDTH:k0 + 2 * RET_WIDTH].astype(v_ref.dtype)
    g = z[:, k0 + 2 * RET_WIDTH:]
    gs_ref[...] = g * jax.nn.sigmoid(g)


def _inproj(x, g1, w_in, cos, sin, tm, table_blocks, qkv_dtype):
    t = x.shape[0]
    row = lambda i: (i, 0)
    const = lambda i: (0, 0)
    tab = (lambda i: (i % table_blocks, 0)) if table_blocks > 1 else const
    return pl.pallas_call(
        _inproj_body,
        grid=(t // tm,),
        in_specs=[
            pl.BlockSpec((tm, D_MODEL), row),
            pl.BlockSpec((1, D_MODEL), const),
            pl.BlockSpec((D_MODEL, IN_COLS), const),
            pl.BlockSpec((tm, HEAD_DIM), tab),
            pl.BlockSpec((tm, HEAD_DIM), tab),
        ],
        out_specs=[pl.BlockSpec((tm, CONV_CH), row)] + [pl.BlockSpec((tm, RET_WIDTH), row)] * 4,
        out_shape=[
            jax.ShapeDtypeStruct((t, CONV_CH), F32),
            jax.ShapeDtypeStruct((t, RET_WIDTH), qkv_dtype),
            jax.ShapeDtypeStruct((t, RET_WIDTH), qkv_dtype),
            jax.ShapeDtypeStruct((t, RET_WIDTH), qkv_dtype),
            jax.ShapeDtypeStruct((t, RET_WIDTH), F32),
        ],
        compiler_params=_cparams(("parallel",)),
        name="inproj",
    )(x, g1, w_in, cos, sin)


def _ln_silu(acc, g, b):
    mu = jnp.mean(acc, axis=-1, keepdims=True)
    d = acc - mu
    var = jnp.mean(d * d, axis=-1, keepdims=True)
    y = d * lax.rsqrt(var + EPS) * g + b
    return y * jax.nn.sigmoid(y)


def _dwconv(load, w_ref, rows, time_axis):
    acc = None
    for b in range(SUBLANES):
        part = None
        for a in range((CONV_K + HALO_OFF) // SUBLANES + 1):
            k = SUBLANES * a + b - HALO_OFF
            if 0 <= k < CONV_K:
                term = load(SUBLANES * a, rows + SUBLANES) * w_ref[k:k + 1, :]
                part = term if part is None else part + term
        if part is not None:
            shifted = lax.slice_in_dim(part, b, b + rows, axis=time_axis)
            acc = shifted if acc is None else acc + shifted
    return acc


def _conv_prompt_body(u_ref, w_ref, cb_ref, lg_ref, lb_ref, c_ref, st_ref, ext_ref):
    j = pl.program_id(1)
    tl = u_ref.shape[1]

    @pl.when(j == 0)
    def _():
        ext_ref[0:HALO, :] = jnp.zeros((HALO, CONV_CH), F32)
        ext_ref[tl + HALO:, :] = jnp.zeros((SUBLANES, CONV_CH), F32)

    @pl.when(j > 0)
    def _():
        ext_ref[0:HALO, :] = ext_ref[tl:tl + HALO, :]

    ext_ref[HALO:tl + HALO, :] = u_ref[0]
    acc = _dwconv(lambda s, n: ext_ref[s:s + n, :], w_ref, tl, 0) + cb_ref[...]
    c_ref[0] = _ln_silu(acc, lg_ref[...], lb_ref[...]).astype(c_ref.dtype)
    st_ref[0] = ext_ref[tl + HALO_OFF:tl + HALO, :]


def _conv_prompt(u, conv_w, conv_b, ln_g, ln_b, tl):
    n, l, _ = u.shape
    const = lambda b, j: (0, 0)
    return pl.pallas_call(
        _conv_prompt_body,
        grid=(n, l // tl),
        in_specs=[
            pl.BlockSpec((1, tl, CONV_CH), lambda b, j: (b, j, 0)),
            pl.BlockSpec((CONV_K, CONV_CH), const),
            pl.BlockSpec((1, CONV_CH), const),
            pl.BlockSpec((1, CONV_CH), const),
            pl.BlockSpec((1, CONV_CH), const),
        ],
        out_specs=[
            pl.BlockSpec((1, tl, CONV_CH), lambda b, j: (b, j, 0)),
            pl.BlockSpec((1, CONV_K - 1, CONV_CH), lambda b, j: (b, 0, 0)),
        ],
        out_shape=[
            jax.ShapeDtypeStruct((n, l, CONV_CH), BF16),
            jax.ShapeDtypeStruct((n, CONV_K - 1, CONV_CH), F32),
        ],
        scratch_shapes=[pltpu.VMEM((tl + HALO + SUBLANES, CONV_CH), F32)],
        compiler_params=_cparams(("arbitrary", "arbitrary")),
        name="conv_prompt",
    )(u, conv_w, conv_b, ln_g, ln_b)


def _conv_sample_body(u_ref, st_ref, w_ref, cb_ref, lg_ref, lb_ref, c_ref, nst_ref, ext_ref):
    nb, l, _ = u_ref.shape
    ext_ref[:, 0:HALO_OFF, :] = jnp.zeros((nb, HALO_OFF, CONV_CH), F32)
    ext_ref[:, HALO_OFF:HALO, :] = st_ref[...]
    ext_ref[:, HALO:l + HALO, :] = u_ref[...]
    ext_ref[:, l + HALO:, :] = jnp.zeros((nb, SUBLANES, CONV_CH), F32)
    acc = _dwconv(lambda s, n: ext_ref[:, s:s + n, :], w_ref, l, 1) + cb_ref[...]
    c_ref[...] = _ln_silu(acc, lg_ref[...], lb_ref[...]).astype(c_ref.dtype)
    nst_ref[...] = ext_ref[:, l + HALO_OFF:l + HALO, :]


def _conv_sample(u, state, conv_w, conv_b, ln_g, ln_b, nb):
    n, l, _ = u.shape
    const = lambda b: (0, 0)
    blk3 = lambda b: (b, 0, 0)
    return pl.pallas_call(
        _conv_sample_body,
        grid=(n // nb,),
        in_specs=[
            pl.BlockSpec((nb, l, CONV_CH), blk3),
            pl.BlockSpec((nb, CONV_K - 1, CONV_CH), blk3),
            pl.BlockSpec((CONV_K, CONV_CH), const),
            pl.BlockSpec((1, CONV_CH), const),
            pl.BlockSpec((1, CONV_CH), const),
            pl.BlockSpec((1, CONV_CH), const),
        ],
        out_specs=[
            pl.BlockSpec((nb, l, CONV_CH), blk3),
            pl.BlockSpec((nb, CONV_K - 1, CONV_CH), blk3),
        ],
        out_shape=[
            jax.ShapeDtypeStruct((n, l, CONV_CH), BF16),
            jax.ShapeDtypeStruct((n, CONV_K - 1, CONV_CH), F32),
        ],
        scratch_shapes=[pltpu.VMEM((nb, l + HALO + SUBLANES, CONV_CH), F32)],
        compiler_params=_cparams(("parallel",)),
        name="conv_sample",
    )(u, state, conv_w, conv_b, ln_g, ln_b)


def _decay_tables(c):
    lg = jnp.log(1.0 - 2.0 ** (-5.0 - jnp.arange(RET_HEADS, dtype=F32)))
    idx = jnp.arange(c, dtype=F32)
    rel = idx[:, None] - idx[None, :]
    dmat = jnp.where(rel[None] >= 0, jnp.exp(jnp.maximum(rel, 0.0)[None] * lg[:, None, None]), 0.0)
    xi = jnp.exp((idx + 1.0)[None, :] * lg[:, None])
    zeta = jnp.exp((c - 1.0 - idx)[None, :] * lg[:, None])
    gc = jnp.exp(c * lg)
    xi_b = jnp.broadcast_to(xi[:, :, None], (RET_HEADS, c, HEAD_DIM))
    zeta_b = jnp.broadcast_to(zeta[:, :, None], (RET_HEADS, c, HEAD_DIM))
    gc_b = jnp.broadcast_to(gc[:, None, None], (RET_HEADS, 1, HEAD_DIM))
    return dmat, xi_b, zeta_b, gc_b


def _group_norm(o):
    mu = jnp.mean(o, axis=-1, keepdims=True)
    d = o - mu
    var = jnp.mean(d * d, axis=-1, keepdims=True)
    return d * lax.rsqrt(var + EPS)


def _ret_chunk(qh, kh, vh, r, dmat, xi, zeta, gc):
    qb = qh.astype(BF16)
    kb = kh.astype(BF16)
    vb = vh.astype(BF16)
    s = lax.dot_general(qb, kb, (((1,), (1,)), ((), ())), preferred_element_type=F32) * dmat
    o = jnp.dot(s.astype(BF16), vb, preferred_element_type=F32)
    o = o + jnp.dot(qb, r.astype(BF16), preferred_element_type=F32) * xi
    kz = (kh.astype(F32) * zeta).astype(BF16)
    r_new = r * gc + lax.dot_general(kz, vb, (((0,), (0,)), ((), ())), preferred_element_type=F32)
    return o, r_new


def _ret_prompt_body(q_ref, k_ref, v_ref, gs_ref, d_ref, xi_ref, zeta_ref, gc_ref, o_ref, st_ref, r_ref):
    j = pl.program_id(1)

    @pl.when(j == 0)
    def _():
        r_ref[...] = jnp.zeros_like(r_ref)

    n_chunks = q_ref.shape[0] // CHUNK
    for hh in range(RET_HEADS):
        sl = slice(hh * HEAD_DIM, (hh + 1) * HEAD_DIM)
        r = r_ref[hh]
        for ci in range(n_chunks):
            rows = slice(ci * CHUNK, (ci + 1) * CHUNK)
            o, r = _ret_chunk(q_ref[rows, sl], k_ref[rows, sl], v_ref[rows, sl], r,
                              d_ref[hh], xi_ref[hh], zeta_ref[hh], gc_ref[hh])
            o_ref[rows, sl] = (gs_ref[rows, sl] * _group_norm(o)).astype(o_ref.dtype)
        r_ref[hh] = r
    st_ref[0] = r_ref[...]


def _ret_prompt(q, k, v, gs, n, l, tl):
    dmat, xi, zeta, gc = _decay_tables(CHUNK)
    per = l // tl
    row = lambda b, j: (b * per + j, 0)
    c3 = lambda b, j: (0, 0, 0)
    return pl.pallas_call(
        _ret_prompt_body,
        grid=(n, per),
        in_specs=[pl.BlockSpec((tl, RET_WIDTH), row)] * 4 + [
            pl.BlockSpec((RET_HEADS, CHUNK, CHUNK), c3),
            pl.BlockSpec((RET_HEADS, CHUNK, HEAD_DIM), c3),
            pl.BlockSpec((RET_HEADS, CHUNK, HEAD_DIM), c3),
            pl.BlockSpec((RET_HEADS, 1, HEAD_DIM), c3),
        ],
        out_specs=[
            pl.BlockSpec((tl, RET_WIDTH), row),
            pl.BlockSpec((1, RET_HEADS, HEAD_DIM, HEAD_DIM), lambda b, j: (b, 0, 0, 0)),
        ],
        out_shape=[
            jax.ShapeDtypeStruct((n * l, RET_WIDTH), BF16),
            jax.ShapeDtypeStruct((n, RET_HEADS, HEAD_DIM, HEAD_DIM), F32),
        ],
        scratch_shapes=[pltpu.VMEM((RET_HEADS, HEAD_DIM, HEAD_DIM), F32)],
        compiler_params=_cparams(("arbitrary", "arbitrary")),
        name="ret_prompt",
    )(q, k, v, gs, dmat, xi, zeta, gc)


def _ret_sample_body(q_ref, k_ref, v_ref, gs_ref, st_ref, d_ref, xi_ref, zeta_ref, gc_ref, o_ref, nst_ref):
    nb = st_ref.shape[0]
    l = q_ref.shape[0] // nb
    for b in range(nb):
        rows = slice(b * l, (b + 1) * l)
        for hh in range(RET_HEADS):
            sl = slice(hh * HEAD_DIM, (hh + 1) * HEAD_DIM)
            o, r = _ret_chunk(q_ref[rows, sl], k_ref[rows, sl], v_ref[rows, sl], st_ref[b, hh],
                              d_ref[hh], xi_ref[hh], zeta_ref[hh], gc_ref[hh])
            o_ref[rows, sl] = (gs_ref[rows, sl] * _group_norm(o)).astype(o_ref.dtype)
            nst_ref[b, hh] = r


def _ret_sample(q, k, v, gs, state, l, nb):
    n = state.shape[0]
    dmat, xi, zeta, gc = _decay_tables(l)
    row = lambda b: (b, 0)
    c3 = lambda b: (0, 0, 0)
    blk4 = lambda b: (b, 0, 0, 0)
    return pl.pallas_call(
        _ret_sample_body,
        grid=(n // nb,),
        in_specs=[pl.BlockSpec((nb * l, RET_WIDTH), row)] * 4 + [
            pl.BlockSpec((nb, RET_HEADS, HEAD_DIM, HEAD_DIM), blk4),
            pl.BlockSpec((RET_HEADS, l, l), c3),
            pl.BlockSpec((RET_HEADS, l, HEAD_DIM), c3),
            pl.BlockSpec((RET_HEADS, l, HEAD_DIM), c3),
            pl.BlockSpec((RET_HEADS, 1, HEAD_DIM), c3),
        ],
        out_specs=[
            pl.BlockSpec((nb * l, RET_WIDTH), row),
            pl.BlockSpec((nb, RET_HEADS, HEAD_DIM, HEAD_DIM), blk4),
        ],
        out_shape=[
            jax.ShapeDtypeStruct((n * l, RET_WIDTH), BF16),
            jax.ShapeDtypeStruct((n, RET_HEADS, HEAD_DIM, HEAD_DIM), F32),
        ],
        compiler_params=_cparams(("parallel",)),
        name="ret_sample",
    )(q, k, v, gs, state, dmat, xi, zeta, gc)


def _split3(x):
    hi = x.astype(BF16)
    r1 = x - hi.astype(F32)
    mid = r1.astype(BF16)
    lo = (r1 - mid.astype(F32)).astype(BF16)
    return hi, mid, lo


def _dot_hp(t, w_hi, w_mid):
    t_hi, t_mid, _ = _split3(t)
    d = functools.partial(jnp.dot, preferred_element_type=F32)
    return d(t_hi, w_hi) + (d(t_mid, w_hi) + d(t_hi, w_mid))


def _route(logits):
    lane = lax.broadcasted_iota(jnp.int32, logits.shape, 1)
    big = jnp.int32(LANES)
    is_group = (lane >= N_EXPERTS) & (lane < N_EXPERTS + N_GROUPS)
    lg = jnp.where(is_group, logits, -1e30)
    m = jnp.max(lg, axis=-1, keepdims=True)
    g_top = 1.0 / jnp.sum(jnp.exp(lg - m), axis=-1, keepdims=True)
    g_idx = jnp.min(jnp.where(lg == m, lane, big), axis=-1, keepdims=True) - N_EXPERTS
    in_group = (lane >= g_idx * EXPERTS_PER_GROUP) & (lane < (g_idx + 1) * EXPERTS_PER_GROUP)
    lem = jnp.where(in_group, logits, -1e30)
    m2 = jnp.max(lem, axis=-1, keepdims=True)
    pe = jnp.where(in_group, jnp.exp(lem - m2), 0.0)
    p1 = jnp.max(pe, axis=-1, keepdims=True)
    e1 = jnp.min(jnp.where(in_group & (pe == p1), lane, big), axis=-1, keepdims=True)
    rest = in_group & (lane != e1)
    pe2 = jnp.where(rest, pe, -1.0)
    p2 = jnp.max(pe2, axis=-1, keepdims=True)
    e2 = jnp.min(jnp.where(rest & (pe2 == p2), lane, big), axis=-1, keepdims=True)
    scale = g_top / (p1 + p2)
    rec = jnp.where(lane == 0, e1.astype(F32), jnp.where(lane == 1, e2.astype(F32), 0.0))
    return rec + jnp.where(lane == 2, p1 * scale, jnp.where(lane == 3, p2 * scale, 0.0))


def _outproj_body(c_ref, o_ref, x_ref, wo_ref, g2_ref, wr_ref, br_ref, x1_ref, t_ref, rec_ref):
    x1 = x_ref[...] + jnp.dot(c_ref[...], wo_ref[0:CONV_CH, :], preferred_element_type=F32)
    x1 = x1 + jnp.dot(o_ref[...], wo_ref[CONV_CH:, :], preferred_element_type=F32)
    x1_ref[...] = x1
    t = _rms(x1, g2_ref[...])
    t_ref[...] = t.astype(t_ref.dtype)
    rec_ref[...] = _route(_dot_hp(t, wr_ref[0], wr_ref[1]) + br_ref[...])


def _outproj(c, o, x, w_out, g2, wr2, br, tm):
    t = x.shape[0]
    row = lambda i: (i, 0)
    const = lambda i: (0, 0)
    c3 = lambda i: (0, 0, 0)
    return pl.pallas_call(
        _outproj_body,
        grid=(t // tm,),
        in_specs=[
            pl.BlockSpec((tm, CONV_CH), row),
            pl.BlockSpec((tm, RET_WIDTH), row),
            pl.BlockSpec((tm, D_MODEL), row),
            pl.BlockSpec((D_MODEL, D_MODEL), const),
            pl.BlockSpec((1, D_MODEL), const),
            pl.BlockSpec((2, D_MODEL, LANES), c3),
            pl.BlockSpec((1, LANES), const),
        ],
        out_specs=[
            pl.BlockSpec((tm, D_MODEL), row),
            pl.BlockSpec((tm, D_MODEL), row),
            pl.BlockSpec((tm, LANES), row),
        ],
        out_shape=[
            jax.ShapeDtypeStruct((t, D_MODEL), F32),
            jax.ShapeDtypeStruct((t, D_MODEL), BF16),
            jax.ShapeDtypeStruct((t, LANES), F32),
        ],
        compiler_params=_cparams(("parallel",)),
        name="outproj_router",
    )(c, o, x, w_out, g2, wr2, br)


def _mix_body(x_ref, g1_ref, w_ref, cos_ref, sin_ref, cw_ref, cb_ref, lg_ref, lb_ref,
              d_ref, xi_ref, zeta_ref, gc_ref, wo_ref, g2_ref, wr_ref, br_ref,
              x1_ref, t_ref, rec_ref, cst_ref, rst_ref, ext_ref, r_ref, o_ref):
    j = pl.program_id(1)
    tl = x_ref.shape[0]
    x = x_ref[...]
    z = jnp.dot(_rms(x, g1_ref[...]).astype(BF16), w_ref[...], preferred_element_type=F32)

    @pl.when(j == 0)
    def _():
        ext_ref[0:HALO, :] = jnp.zeros((HALO, CONV_CH), F32)
        ext_ref[tl + HALO:, :] = jnp.zeros((SUBLANES, CONV_CH), F32)
        r_ref[...] = jnp.zeros_like(r_ref)

    @pl.when(j > 0)
    def _():
        ext_ref[0:HALO, :] = ext_ref[tl:tl + HALO, :]

    ext_ref[HALO:tl + HALO, :] = z[:, :CONV_CH] * jax.nn.sigmoid(z[:, CONV_CH:2 * CONV_CH])
    acc = _dwconv(lambda s, n: ext_ref[s:s + n, :], cw_ref, tl, 0) + cb_ref[...]
    c = _ln_silu(acc, lg_ref[...], lb_ref[...]).astype(BF16)
    cst_ref[0] = ext_ref[tl + HALO_OFF:tl + HALO, :]

    cos = cos_ref[...]
    sin = sin_ref[...]
    q0 = 2 * CONV_CH
    k0 = q0 + RET_WIDTH
    v0 = k0 + RET_WIDTH
    g0 = v0 + RET_WIDTH
    for hh in range(RET_HEADS):
        lo = hh * HEAD_DIM
        qh = z[:, q0 + lo:q0 + lo + HEAD_DIM]
        kh = z[:, k0 + lo:k0 + lo + HEAD_DIM]
        qr = (qh * cos + pltpu.roll(qh, HEAD_DIM // 2, 1) * sin).astype(BF16)
        kr = ((kh * cos + pltpu.roll(kh, HEAD_DIM // 2, 1) * sin) * (HEAD_DIM ** -0.5)).astype(BF16)
        vh = z[:, v0 + lo:v0 + lo + HEAD_DIM].astype(BF16)
        g = z[:, g0 + lo:g0 + lo + HEAD_DIM]
        gs = g * jax.nn.sigmoid(g)
        r = r_ref[hh]
        for ci in range(tl // CHUNK):
            rows = slice(ci * CHUNK, (ci + 1) * CHUNK)
            o, r = _ret_chunk(qr[rows], kr[rows], vh[rows], r, d_ref[hh], xi_ref[hh], zeta_ref[hh], gc_ref[hh])
            o_ref[rows, lo:lo + HEAD_DIM] = (gs[rows] * _group_norm(o)).astype(BF16)
        r_ref[hh] = r
    rst_ref[0] = r_ref[...]

    x1 = x + jnp.dot(c, wo_ref[0:CONV_CH, :], preferred_element_type=F32)
    x1 = x1 + jnp.dot(o_ref[...], wo_ref[CONV_CH:, :], preferred_element_type=F32)
    x1_ref[...] = x1
    t = _rms(x1, g2_ref[...])
    t_ref[...] = t.astype(t_ref.dtype)
    rec_ref[...] = _route(_dot_hp(t, wr_ref[0], wr_ref[1]) + br_ref[...])


def _mix(x, g1, w_in, cos, sin, conv_w, conv_b, ln_g, ln_b, w_out, g2, wr2, br, n, l, tl):
    dmat, xi, zeta, gc = _decay_tables(CHUNK)
    per = l // tl
    row = lambda b, j: (b * per + j, 0)
    tab = lambda b, j: (j, 0)
    const = lambda b, j: (0, 0)
    c3 = lambda b, j: (0, 0, 0)
    once = dict(pipeline_mode=pl.Buffered(1))
    return pl.pallas_call(
        _mix_body,
        grid=(n, per),
        in_specs=[
            pl.BlockSpec((tl, D_MODEL), row),
            pl.BlockSpec((1, D_MODEL), const),
            pl.BlockSpec((D_MODEL, IN_COLS), const, **once),
            pl.BlockSpec((tl, HEAD_DIM), tab),
            pl.BlockSpec((tl, HEAD_DIM), tab),
            pl.BlockSpec((CONV_K, CONV_CH), const),
            pl.BlockSpec((1, CONV_CH), const),
            pl.BlockSpec((1, CONV_CH), const),
            pl.BlockSpec((1, CONV_CH), const),
            pl.BlockSpec((RET_HEADS, CHUNK, CHUNK), c3),
            pl.BlockSpec((RET_HEADS, CHUNK, HEAD_DIM), c3),
            pl.BlockSpec((RET_HEADS, CHUNK, HEAD_DIM), c3),
            pl.BlockSpec((RET_HEADS, 1, HEAD_DIM), c3),
            pl.BlockSpec((D_MODEL, D_MODEL), const, **once),
            pl.BlockSpec((1, D_MODEL), const),
            pl.BlockSpec((2, D_MODEL, LANES), c3, **once),
            pl.BlockSpec((1, LANES), const),
        ],
        out_specs=[
            pl.BlockSpec((tl, D_MODEL), row),
            pl.BlockSpec((tl, D_MODEL), row),
            pl.BlockSpec((tl, LANES), row),
            pl.BlockSpec((1, CONV_K - 1, CONV_CH), lambda b, j: (b, 0, 0)),
            pl.BlockSpec((1, RET_HEADS, HEAD_DIM, HEAD_DIM), lambda b, j: (b, 0, 0, 0)),
        ],
        out_shape=[
            jax.ShapeDtypeStruct((n * l, D_MODEL), F32),
            jax.ShapeDtypeStruct((n * l, D_MODEL), BF16),
            jax.ShapeDtypeStruct((n * l, LANES), F32),
            jax.ShapeDtypeStruct((n, CONV_K - 1, CONV_CH), F32),
            jax.ShapeDtypeStruct((n, RET_HEADS, HEAD_DIM, HEAD_DIM), F32),
        ],
        scratch_shapes=[
            pltpu.VMEM((tl + HALO + SUBLANES, CONV_CH), F32),
            pltpu.VMEM((RET_HEADS, HEAD_DIM, HEAD_DIM), F32),
            pltpu.VMEM((tl, RET_WIDTH), BF16),
        ],
        compiler_params=pltpu.CompilerParams(dimension_semantics=("arbitrary", "arbitrary"),
                                             vmem_limit_bytes=MIX_VMEM_LIMIT),
        name="token_mix",
    )(x, g1, w_in, cos, sin, conv_w, conv_b, ln_g, ln_b, dmat, xi, zeta, gc, w_out, g2, wr2, br)


def _iota_f32(shape, dim):
    return lax.broadcasted_iota(jnp.int32, shape, dim).astype(F32)


def _dispatch_body(rec_a_ref, t_a_ref, rec_b_ref, t_b_ref, s_ref, pos_ref, meta_ref, *, nsub_a):
    from_a = pl.program_id(0) * DISPATCH_SUBS < nsub_a
    for s in range(DISPATCH_SUBS):
        rows = slice(s * SUB, (s + 1) * SUB)
        rec = jnp.where(from_a, rec_a_ref[rows, :], rec_b_ref[rows, :])
        tok = jnp.where(from_a, t_a_ref[rows, :], t_b_ref[rows, :])
        chunks = slice(s * CHUNKS_PER_SUB, (s + 1) * CHUNKS_PER_SUB)
        _dispatch_sub_tile(rec, tok, s_ref.at[chunks], pos_ref.at[rows], meta_ref.at[s])


def _dispatch_sub_tile(rec, tok, s_ref, pos_ref, meta_ref):
    lane = _iota_f32(rec.shape, 1)
    a1 = lane == rec[:, 0:1]
    a2 = lane == rec[:, 1:2]
    a1f = jnp.where(a1, 1.0, 0.0)
    a2f = jnp.where(a2, 1.0, 0.0)
    ltri = jnp.where(_iota_f32((SUB, SUB), 1) < _iota_f32((SUB, SUB), 0), 1.0, 0.0).astype(BF16)
    c1 = jnp.dot(ltri, a1f.astype(BF16), preferred_element_type=F32)
    c2 = jnp.dot(ltri, a2f.astype(BF16), preferred_element_type=F32)
    n1 = jnp.sum(a1f, axis=0, keepdims=True)
    n2 = jnp.sum(a2f, axis=0, keepdims=True)
    cnt = jnp.floor((n1 + n2 + (ROW_ALIGN - 1.0)) * (1.0 / ROW_ALIGN))
    utri = jnp.where(_iota_f32((LANES, LANES), 0) < _iota_f32((LANES, LANES), 1), 1.0, 0.0).astype(BF16)
    start = jnp.dot(jnp.broadcast_to(cnt, (SUBLANES, LANES)).astype(BF16), utri,
                    preferred_element_type=F32)[0:1]
    base1 = start * ROW_ALIGN
    base2 = base1 + n1
    pos1 = jnp.sum(jnp.where(a1, c1 + base1, 0.0), axis=1, keepdims=True)
    pos2 = jnp.sum(jnp.where(a2, c2 + base2, 0.0), axis=1, keepdims=True)
    posm = jnp.where(lane == 0.0, pos1, jnp.where(lane == 1.0, pos2, 0.0))
    pos_ref[...] = posm
    row = _iota_f32((SUBLANES, LANES), 0)
    meta_ref[...] = jnp.where(row == 0.0, start, jnp.where(row == 1.0, cnt, 0.0))

    g1 = _split3(rec[:, 2:3])
    g2 = _split3(rec[:, 3:4])
    info = jnp.where(lane == 6.0, rec[:, 0:1], jnp.where(lane == 7.0, rec[:, 1:2], 0.0))
    for i in range(3):
        info = jnp.where(lane == float(i), g1[i].astype(F32), info)
        info = jnp.where(lane == float(3 + i), g2[i].astype(F32), info)
    src = jnp.concatenate([tok, info.astype(BF16)], axis=1)

    post = posm.T
    r = _iota_f32((SUBP, SUB), 0)
    onehot = jnp.where(r == post[0:1, :], 1.0, jnp.where(r == post[1:2, :], 1.0, 0.0)).astype(BF16)
    sorted_rows = jnp.dot(onehot, src, preferred_element_type=F32).astype(BF16)
    s_ref[...] = sorted_rows.reshape(CHUNKS_PER_SUB, ROW_ALIGN, ROW_W)


def _dispatch(rec_a, t_a, rec_b, t_b):
    nsub_a = rec_a.shape[0] // SUB
    nsub_b = rec_b.shape[0] // SUB
    nsub = nsub_a + nsub_b
    assert nsub_a % DISPATCH_SUBS == 0 and nsub_b % DISPATCH_SUBS == 0
    steps_a = nsub_a // DISPATCH_SUBS
    tm = DISPATCH_SUBS * SUB
    row = lambda i: (i, 0)
    row_a = lambda i: (jnp.minimum(i, steps_a - 1), 0)
    row_b = lambda i: (jnp.maximum(i - steps_a, 0), 0)
    return pl.pallas_call(
        functools.partial(_dispatch_body, nsub_a=nsub_a),
        grid=(nsub // DISPATCH_SUBS,),
        in_specs=[
            pl.BlockSpec((tm, LANES), row_a),
            pl.BlockSpec((tm, D_MODEL), row_a),
            pl.BlockSpec((tm, LANES), row_b),
            pl.BlockSpec((tm, D_MODEL), row_b),
        ],
        out_specs=[
            pl.BlockSpec((DISPATCH_SUBS * CHUNKS_PER_SUB, ROW_ALIGN, ROW_W), lambda i: (i, 0, 0)),
            pl.BlockSpec((tm, LANES), row),
            pl.BlockSpec((DISPATCH_SUBS, SUBLANES, LANES), lambda i: (i, 0, 0)),
        ],
        out_shape=[
            jax.ShapeDtypeStruct((nsub * CHUNKS_PER_SUB, ROW_ALIGN, ROW_W), BF16),
            jax.ShapeDtypeStruct((nsub * SUB, LANES), F32),
            jax.ShapeDtypeStruct((nsub, SUBLANES, LANES), F32),
        ],
        compiler_params=_cparams(("parallel",)),
        name="moe_dispatch",
    )(rec_a, t_a, rec_b, t_b)


def _experts_body(start_ref, cnt_ref, s_in, wg_ref, wu_ref, wd_ref, s_hbm,
                  xbuf, ybuf, wgu_ref, wdb_ref, gsem, ssem, list_ref, state_ref, *, nsub):
    del s_in
    e = pl.program_id(0)
    ne = pl.num_programs(0)
    par = e & 1

    list_max = list_ref.shape[0] // 2

    def gather_copy(src, i, slot):
        return pltpu.make_async_copy(s_hbm.at[src], xbuf.at[slot, i], gsem.at[slot])

    def scatter_copy(dst, i, slot):
        return pltpu.make_async_copy(ybuf.at[slot, i], s_hbm.at[dst], ssem.at[slot])

    def build_list(x, which):
        def per_sub(s, k):
            run = s * N_EXPERTS + x
            c = cnt_ref[run]
            base = s * CHUNKS_PER_SUB + start_ref[run]
            list_ref[k] = base
            list_ref[k + 1] = base + 1

            def per_chunk(i, carry):
                list_ref[k + i] = base + i
                return carry
            lax.fori_loop(2, c, per_chunk, 0)
            return k + c
        first = which * list_max
        state_ref[which] = lax.fori_loop(0, nsub, per_sub, first) - first

    def start_all(copy, which, first, n, slot, counter):
        def body(i, carry):
            copy(list_ref[which * list_max + first + i], i, slot).start()
            return carry
        lax.fori_loop(0, n, body, 0)
        state_ref[counter] = n

    def wait_all(copy, block_copy, slot, counter):
        n = state_ref[counter]

        @pl.when(n == CHUNKS_PER_BLK)
        def _():
            block_copy(slot).wait()

        @pl.when(n < CHUNKS_PER_BLK)
        def _():
            def body(i, carry):
                copy(0, 0, slot).wait()
                return carry
            lax.fori_loop(0, n, body, 0)
        state_ref[counter] = 0

    def gather_block(slot):
        return pltpu.make_async_copy(s_hbm.at[pl.ds(0, CHUNKS_PER_BLK)], xbuf.at[slot], gsem.at[slot])

    def scatter_block(slot):
        return pltpu.make_async_copy(ybuf.at[slot], s_hbm.at[pl.ds(0, CHUNKS_PER_BLK)], ssem.at[slot])

    def block_chunks(total, b):
        return jnp.minimum(total - b * CHUNKS_PER_BLK, CHUNKS_PER_BLK)

    @pl.when(e == 0)
    def _():
        for i in range(6):
            state_ref[i] = 0
        xbuf[...] = jnp.zeros_like(xbuf)
        build_list(0, 0)
        n0 = state_ref[0]

        @pl.when(n0 > 0)
        def _():
            start_all(gather_copy, 0, 0, block_chunks(n0, 0), 0, 2)

    @pl.when(e + 1 < ne)
    def _():
        build_list(e + 1, 1 - par)

    total = state_ref[par]
    nblk = (total + CHUNKS_PER_BLK - 1) // CHUNKS_PER_BLK
    wgu_ref[:, 0:EXPERT_FF] = wg_ref[0].astype(BF16)
    wgu_ref[:, EXPERT_FF:] = wu_ref[0].astype(BF16)
    wdb_ref[...] = wd_ref[0].astype(BF16)
    ef = e.astype(F32)

    def block(b, carry):
        slot = b & 1
        first = b * CHUNKS_PER_BLK

        @pl.when(b + 1 < nblk)
        def _():
            start_all(gather_copy, par, first + CHUNKS_PER_BLK, block_chunks(total, b + 1), 1 - slot, 3 - slot)

        wait_all(gather_copy, gather_block, slot, 2 + slot)
        wait_all(scatter_copy, scatter_block, slot, 4 + slot)
        x = xbuf[slot].reshape(MBLK, ROW_W)
        info = x[:, D_MODEL:].astype(F32)
        g_first = info[:, 0:1] + info[:, 1:2] + info[:, 2:3]
        g_second = info[:, 3:4] + info[:, 4:5] + info[:, 5:6]
        gate = jnp.where(info[:, 6:7] == ef, g_first, g_second)
        h = jnp.dot(x[:, :D_MODEL], wgu_ref[...], preferred_element_type=F32)
        h1 = h[:, :EXPERT_FF]
        hid = (h1 * jax.nn.sigmoid(h1)) * h[:, EXPERT_FF:] * gate
        y = jnp.dot(hid.astype(BF16), wdb_ref[...], preferred_element_type=F32).astype(BF16)
        ybuf[slot] = jnp.concatenate([y, x[:, D_MODEL:]], axis=1).reshape(CHUNKS_PER_BLK, ROW_ALIGN, ROW_W)
        start_all(scatter_copy, par, first, block_chunks(total, b), slot, 4 + slot)
        return carry

    lax.fori_loop(0, nblk, block, 0)

    @pl.when(e + 1 < ne)
    def _():
        n1 = state_ref[1 - par]

        @pl.when(n1 > 0)
        def _():
            start_all(gather_copy, 1 - par, 0, block_chunks(n1, 0), 0, 2)

    @pl.when(e == ne - 1)
    def _():
        wait_all(scatter_copy, scatter_block, 0, 4)
        wait_all(scatter_copy, scatter_block, 1, 5)


def _experts(start, cnt, staged, wg, wu, wd):
    nsub = staged.shape[0] // CHUNKS_PER_SUB
    list_max = nsub * SUB // ROW_ALIGN + nsub + LIST_SLACK
    wblk = lambda e, *_: (e, 0, 0)
    grid_spec = pltpu.PrefetchScalarGridSpec(
        num_scalar_prefetch=2,
        grid=(N_EXPERTS,),
        in_specs=[
            pl.BlockSpec(memory_space=pl.ANY),
            pl.BlockSpec((1, D_MODEL, EXPERT_FF), wblk),
            pl.BlockSpec((1, D_MODEL, EXPERT_FF), wblk),
            pl.BlockSpec((1, EXPERT_FF, D_MODEL), wblk),
        ],
        out_specs=pl.BlockSpec(memory_space=pl.ANY),
        scratch_shapes=[
            pltpu.VMEM((2, CHUNKS_PER_BLK, ROW_ALIGN, ROW_W), BF16),
            pltpu.VMEM((2, CHUNKS_PER_BLK, ROW_ALIGN, ROW_W), BF16),
            pltpu.VMEM((D_MODEL, 2 * EXPERT_FF), BF16),
            pltpu.VMEM((EXPERT_FF, D_MODEL), BF16),
            pltpu.SemaphoreType.DMA((2,)),
            pltpu.SemaphoreType.DMA((2,)),
            pltpu.SMEM((2 * list_max,), jnp.int32),
            pltpu.SMEM((6,), jnp.int32),
        ],
    )
    return pl.pallas_call(
        functools.partial(_experts_body, nsub=nsub),
        grid_spec=grid_spec,
        out_shape=jax.ShapeDtypeStruct(staged.shape, staged.dtype),
        input_output_aliases={2: 0},
        compiler_params=_cparams(("arbitrary",)),
        name="moe_experts",
    )(start, cnt, staged, wg, wu, wd)


def _combine_body(ys_ref, pos_ref, x1_ref, p_ref, wp_ref, gp_ref, wpg_ref, gf_ref, y_ref):
    r = _iota_f32((SUB, SUBP), 1)
    moe = []
    for s in range(x1_ref.shape[0] // SUB):
        p1 = pos_ref[s * SUB:(s + 1) * SUB, 0:1]
        p2 = pos_ref[s * SUB:(s + 1) * SUB, 1:2]
        onehot = jnp.where(r == p1, 1.0, jnp.where(r == p2, 1.0, 0.0)).astype(BF16)
        ys = ys_ref[s * CHUNKS_PER_SUB:(s + 1) * CHUNKS_PER_SUB].reshape(SUBP, D_MODEL)
        moe.append(jnp.dot(onehot, ys, preferred_element_type=F32))
    x2 = x1_ref[...] + jnp.concatenate(moe, axis=0)
    ple = _rms(jnp.dot(p_ref[...].astype(BF16), wp_ref[...], preferred_element_type=F32), gp_ref[...])
    gate = jax.nn.sigmoid(jnp.dot(x2.astype(BF16), wpg_ref[...], preferred_element_type=F32))
    y_ref[...] = _rms(x2 + ple * gate, gf_ref[...])


def _combine(ys, pos, x1, p, w_ple, gp, w_ple_gate, gf, sub_off):
    t = x1.shape[0]
    tm = COMBINE_SUBS * SUB
    blk_off = sub_off // COMBINE_SUBS
    assert sub_off % COMBINE_SUBS == 0 and t % tm == 0
    row = lambda i: (i, 0)
    const = lambda i: (0, 0)
    return pl.pallas_call(
        _combine_body,
        grid=(t // tm,),
        in_specs=[
            pl.BlockSpec((COMBINE_SUBS * CHUNKS_PER_SUB, ROW_ALIGN, D_MODEL), lambda i: (i + blk_off, 0, 0)),
            pl.BlockSpec((tm, LANES), lambda i: (i + blk_off, 0)),
            pl.BlockSpec((tm, D_MODEL), row),
            pl.BlockSpec((tm, PLE_DIM), row),
            pl.BlockSpec((PLE_DIM, D_MODEL), const),
            pl.BlockSpec((1, D_MODEL), const),
            pl.BlockSpec((D_MODEL, D_MODEL), const),
            pl.BlockSpec((1, D_MODEL), const),
        ],
        out_specs=pl.BlockSpec((tm, D_MODEL), row),
        out_shape=jax.ShapeDtypeStruct((t, D_MODEL), F32),
        compiler_params=_cparams(("parallel",)),
        name="moe_combine_ple",
    )(ys, pos, x1, p, w_ple, gp, w_ple_gate, gf)


def _rope_tables(pos):
    half = HEAD_DIM // 2
    inv = ROPE_BASE ** (-jnp.arange(half, dtype=F32) / half)
    ang = pos[:, None] * inv[None, :]
    cos = jnp.cos(ang)
    sin = jnp.sin(ang)
    return jnp.concatenate([cos, cos], axis=-1), jnp.concatenate([-sin, sin], axis=-1)


def _router_params(we, be, wg, bg):
    pad = LANES - N_EXPERTS - N_GROUPS
    w = jnp.pad(jnp.concatenate([we, wg], axis=1), ((0, 0), (0, pad)))
    b = jnp.pad(jnp.concatenate([be, bg]), (0, pad))[None, :]
    return jnp.stack(_split3(w)[:2]), b


def kernel(x_prompt, x_sample, p_prompt, p_sample, state_conv, state_ret, w_in, conv_w, conv_b, conv_ln_g, conv_ln_b, w_out, norm1_g, norm2_g, router_group_w, router_group_b, router_expert_w, router_expert_b, w_expert_gate, w_expert_up, w_expert_down, w_ple, ple_norm_g, w_ple_gate, final_norm_g):
    assert w_in.shape[0] == 1, "single-layer trunk"
    nb, seq, _ = x_prompt.shape
    ns, dseq, _ = x_sample.shape
    tm = 512

    w_in_b = w_in[0].astype(BF16)
    w_out_b = w_out[0].astype(BF16)
    w_ple_b = w_ple[0].astype(BF16)
    w_pg_b = w_ple_gate[0].astype(BF16)
    g1 = norm1_g[0][None, :]
    g2 = norm2_g[0][None, :]
    gp = ple_norm_g[0][None, :]
    gf = final_norm_g[None, :]
    cb = conv_b[0][None, :]
    lng = conv_ln_g[0][None, :]
    lnb = conv_ln_b[0][None, :]
    wr2, br = _router_params(router_expert_w[0], router_expert_b[0], router_group_w[0], router_group_b[0])

    cos_p, sin_p = _rope_tables(jnp.arange(seq, dtype=F32) + jnp.float32(0))
    pos_s = jnp.tile(jnp.arange(dseq, dtype=F32) + jnp.float32(PAST_LEN), tm // dseq)
    cos_s, sin_s = _rope_tables(pos_s)

    xp = x_prompt.reshape(nb * seq, D_MODEL)
    x1_p, t_p, rec_p, conv_p, ret_p = _mix(xp, g1, w_in_b, cos_p, sin_p, conv_w[0], cb, lng, lnb, w_out_b, g2,
                                           wr2, br, nb, seq, MIX_TILE)

    xs = x_sample.reshape(ns * dseq, D_MODEL)
    u, q, k, v, gs = _inproj(xs, g1, w_in_b, cos_s, sin_s, tm, 1, F32)
    c, conv_s = _conv_sample(u.reshape(ns, dseq, CONV_CH), state_conv[0], conv_w[0], cb, lng, lnb, 16)
    o, ret_s = _ret_sample(q, k, v, gs, state_ret[0], dseq, 8)
    x1_s, t_s, rec_s = _outproj(c.reshape(ns * dseq, CONV_CH), o, xs, w_out_b, g2, wr2, br, tm)

    staged, pos, meta = _dispatch(rec_p, t_p, rec_s, t_s)
    start = meta[:, 0, :N_EXPERTS].astype(jnp.int32).reshape(-1)
    cnt = meta[:, 1, :N_EXPERTS].astype(jnp.int32).reshape(-1)
    ys = _experts(start, cnt, staged, w_expert_gate[0], w_expert_up[0], w_expert_down[0])

    y_p = _combine(ys, pos, x1_p, p_prompt[0].reshape(nb * seq, PLE_DIM), w_ple_b, gp, w_pg_b, gf, 0)
    y_s = _combine(ys, pos, x1_s, p_sample[0].reshape(ns * dseq, PLE_DIM), w_ple_b, gp, w_pg_b, gf,
                   nb * seq // SUB)

    return (y_p.reshape(nb, seq, D_MODEL), y_s.reshape(ns, dseq, D_MODEL),
            conv_p[None], ret_p[None], conv_s[None], ret_s[None])
```

```python
import functools

import jax
import jax.numpy as jnp
from jax import lax
from jax.experimental import pallas as pl
from jax.experimental.pallas import tpu as pltpu

F32 = jnp.float32
BF16 = jnp.bfloat16

D_MODEL = 1024
PLE_DIM = 256
CONV_CH = 512
CONV_K = 31
RET_WIDTH = 512
RET_HEADS = 4
HEAD_DIM = 128
CHUNK = 128
ROPE_BASE = 10000.0
N_GROUPS = 4
EXPERTS_PER_GROUP = 8
N_EXPERTS = 32
EXPERT_FF = 256
IN_COLS = 3072
EPS = 1e-6
PAST_LEN = 16384

LANES = 128
SUBLANES = 8
HALO = 32
HALO_OFF = HALO - (CONV_K - 1)
VMEM_LIMIT = 48 * 1024 * 1024
MIX_VMEM_LIMIT = 56 * 1024 * 1024
MIX_TILE = 512

SUB = 256
ROW_ALIGN = 16
PBLK = 256
SUBP = -(-(2 * SUB + N_EXPERTS * (ROW_ALIGN - 1)) // PBLK) * PBLK
CHUNKS_PER_SUB = SUBP // ROW_ALIGN
ROW_W = D_MODEL + LANES
MBLK = 512
CHUNKS_PER_BLK = MBLK // ROW_ALIGN
MLP_PARTS = 2
PART_CHUNKS = CHUNKS_PER_BLK // MLP_PARTS
LIST_SLACK = 2
COMBINE_SUBS = 4
DISPATCH_SUBS = 2


def _cparams(sem):
    return pltpu.CompilerParams(dimension_semantics=sem, vmem_limit_bytes=VMEM_LIMIT)


def _rms(x, g):
    return x * lax.rsqrt(jnp.mean(x * x, axis=-1, keepdims=True) + EPS) * g


def _inproj_body(x_ref, g1_ref, w_ref, cos_ref, sin_ref, u_ref, q_ref, k_ref, v_ref, gs_ref):
    h = _rms(x_ref[...], g1_ref[...]).astype(BF16)
    z = jnp.dot(h, w_ref[...], preferred_element_type=F32)
    a = z[:, :CONV_CH]
    b = z[:, CONV_CH:2 * CONV_CH]
    u_ref[...] = a * jax.nn.sigmoid(b)
    cos = cos_ref[...]
    sin = sin_ref[...]
    q0 = 2 * CONV_CH
    k0 = q0 + RET_WIDTH
    for hh in range(RET_HEADS):
        sl = slice(hh * HEAD_DIM, (hh + 1) * HEAD_DIM)
        qh = z[:, q0 + hh * HEAD_DIM:q0 + (hh + 1) * HEAD_DIM]
        kh = z[:, k0 + hh * HEAD_DIM:k0 + (hh + 1) * HEAD_DIM]
        q_ref[:, sl] = (qh * cos + pltpu.roll(qh, HEAD_DIM // 2, 1) * sin).astype(q_ref.dtype)
        kr = (kh * cos + pltpu.roll(kh, HEAD_DIM // 2, 1) * sin) * (HEAD_DIM ** -0.5)
        k_ref[:, sl] = kr.astype(k_ref.dtype)
    v_ref[...] = z[:, k0 + RET_WIDTH:k0 + 2 * RET_WIDTH].astype(v_ref.dtype)
    g = z[:, k0 + 2 * RET_WIDTH:]
    gs_ref[...] = g * jax.nn.sigmoid(g)


def _inproj(x, g1, w_in, cos, sin, tm, table_blocks, qkv_dtype):
    t = x.shape[0]
    row = lambda i: (i, 0)
    const = lambda i: (0, 0)
    tab = (lambda i: (i % table_blocks, 0)) if table_blocks > 1 else const
    return pl.pallas_call(
        _inproj_body,
        grid=(t // tm,),
        in_specs=[
            pl.BlockSpec((tm, D_MODEL), row),
            pl.BlockSpec((1, D_MODEL), const),
            pl.BlockSpec((D_MODEL, IN_COLS), const),
            pl.BlockSpec((tm, HEAD_DIM), tab),
            pl.BlockSpec((tm, HEAD_DIM), tab),
        ],
        out_specs=[pl.BlockSpec((tm, CONV_CH), row)] + [pl.BlockSpec((tm, RET_WIDTH), row)] * 4,
        out_shape=[
            jax.ShapeDtypeStruct((t, CONV_CH), F32),
            jax.ShapeDtypeStruct((t, RET_WIDTH), qkv_dtype),
            jax.ShapeDtypeStruct((t, RET_WIDTH), qkv_dtype),
            jax.ShapeDtypeStruct((t, RET_WIDTH), qkv_dtype),
            jax.ShapeDtypeStruct((t, RET_WIDTH), F32),
        ],
        compiler_params=_cparams(("parallel",)),
        name="inproj",
    )(x, g1, w_in, cos, sin)


def _ln_silu(acc, g, b):
    mu = jnp.mean(acc, axis=-1, keepdims=True)
    d = acc - mu
    var = jnp.mean(d * d, axis=-1, keepdims=True)
    y = d * lax.rsqrt(var + EPS) * g + b
    return y * jax.nn.sigmoid(y)


def _dwconv(load, w_ref, rows, time_axis):
    acc = None
    for b in range(SUBLANES):
        part = None
        for a in range((CONV_K + HALO_OFF) // SUBLANES + 1):
            k = SUBLANES * a + b - HALO_OFF
            if 0 <= k < CONV_K:
                term = load(SUBLANES * a, rows + SUBLANES) * w_ref[k:k + 1, :]
                part = term if part is None else part + term
        if part is not None:
            shifted = lax.slice_in_dim(part, b, b + rows, axis=time_axis)
            acc = shifted if acc is None else acc + shifted
    return acc


def _conv_prompt_body(u_ref, w_ref, cb_ref, lg_ref, lb_ref, c_ref, st_ref, ext_ref):
    j = pl.program_id(1)
    tl = u_ref.shape[1]

    @pl.when(j == 0)
    def _():
        ext_ref[0:HALO, :] = jnp.zeros((HALO, CONV_CH), F32)
        ext_ref[tl + HALO:, :] = jnp.zeros((SUBLANES, CONV_CH), F32)

    @pl.when(j > 0)
    def _():
        ext_ref[0:HALO, :] = ext_ref[tl:tl + HALO, :]

    ext_ref[HALO:tl + HALO, :] = u_ref[0]
    acc = _dwconv(lambda s, n: ext_ref[s:s + n, :], w_ref, tl, 0) + cb_ref[...]
    c_ref[0] = _ln_silu(acc, lg_ref[...], lb_ref[...]).astype(c_ref.dtype)
    st_ref[0] = ext_ref[tl + HALO_OFF:tl + HALO, :]


def _conv_prompt(u, conv_w, conv_b, ln_g, ln_b, tl):
    n, l, _ = u.shape
    const = lambda b, j: (0, 0)
    return pl.pallas_call(
        _conv_prompt_body,
        grid=(n, l // tl),
        in_specs=[
            pl.BlockSpec((1, tl, CONV_CH), lambda b, j: (b, j, 0)),
            pl.BlockSpec((CONV_K, CONV_CH), const),
            pl.BlockSpec((1, CONV_CH), const),
            pl.BlockSpec((1, CONV_CH), const),
            pl.BlockSpec((1, CONV_CH), const),
        ],
        out_specs=[
            pl.BlockSpec((1, tl, CONV_CH), lambda b, j: (b, j, 0)),
            pl.BlockSpec((1, CONV_K - 1, CONV_CH), lambda b, j: (b, 0, 0)),
        ],
        out_shape=[
            jax.ShapeDtypeStruct((n, l, CONV_CH), BF16),
            jax.ShapeDtypeStruct((n, CONV_K - 1, CONV_CH), F32),
        ],
        scratch_shapes=[pltpu.VMEM((tl + HALO + SUBLANES, CONV_CH), F32)],
        compiler_params=_cparams(("arbitrary", "arbitrary")),
        name="conv_prompt",
    )(u, conv_w, conv_b, ln_g, ln_b)


def _conv_sample_body(u_ref, st_ref, w_ref, cb_ref, lg_ref, lb_ref, c_ref, nst_ref, ext_ref):
    nb, l, _ = u_ref.shape
    ext_ref[:, 0:HALO_OFF, :] = jnp.zeros((nb, HALO_OFF, CONV_CH), F32)
    ext_ref[:, HALO_OFF:HALO, :] = st_ref[...]
    ext_ref[:, HALO:l + HALO, :] = u_ref[...]
    ext_ref[:, l + HALO:, :] = jnp.zeros((nb, SUBLANES, CONV_CH), F32)
    acc = _dwconv(lambda s, n: ext_ref[:, s:s + n, :], w_ref, l, 1) + cb_ref[...]
    c_ref[...] = _ln_silu(acc, lg_ref[...], lb_ref[...]).astype(c_ref.dtype)
    nst_ref[...] = ext_ref[:, l + HALO_OFF:l + HALO, :]


def _conv_sample(u, state, conv_w, conv_b, ln_g, ln_b, nb):
    n, l, _ = u.shape
    const = lambda b: (0, 0)
    blk3 = lambda b: (b, 0, 0)
    return pl.pallas_call(
        _conv_sample_body,
        grid=(n // nb,),
        in_specs=[
            pl.BlockSpec((nb, l, CONV_CH), blk3),
            pl.BlockSpec((nb, CONV_K - 1, CONV_CH), blk3),
            pl.BlockSpec((CONV_K, CONV_CH), const),
            pl.BlockSpec((1, CONV_CH), const),
            pl.BlockSpec((1, CONV_CH), const),
            pl.BlockSpec((1, CONV_CH), const),
        ],
        out_specs=[
            pl.BlockSpec((nb, l, CONV_CH), blk3),
            pl.BlockSpec((nb, CONV_K - 1, CONV_CH), blk3),
        ],
        out_shape=[
            jax.ShapeDtypeStruct((n, l, CONV_CH), BF16),
            jax.ShapeDtypeStruct((n, CONV_K - 1, CONV_CH), F32),
        ],
        scratch_shapes=[pltpu.VMEM((nb, l + HALO + SUBLANES, CONV_CH), F32)],
        compiler_params=_cparams(("parallel",)),
        name="conv_sample",
    )(u, state, conv_w, conv_b, ln_g, ln_b)


def _decay_tables(c):
    lg = jnp.log(1.0 - 2.0 ** (-5.0 - jnp.arange(RET_HEADS, dtype=F32)))
    idx = jnp.arange(c, dtype=F32)
    rel = idx[:, None] - idx[None, :]
    dmat = jnp.where(rel[None] >= 0, jnp.exp(jnp.maximum(rel, 0.0)[None] * lg[:, None, None]), 0.0)
    xi = jnp.exp((idx + 1.0)[None, :] * lg[:, None])
    zeta = jnp.exp((c - 1.0 - idx)[None, :] * lg[:, None])
    gc = jnp.exp(c * lg)
    xi_b = jnp.broadcast_to(xi[:, :, None], (RET_HEADS, c, HEAD_DIM))
    zeta_b = jnp.broadcast_to(zeta[:, :, None], (RET_HEADS, c, HEAD_DIM))
    gc_b = jnp.broadcast_to(gc[:, None, None], (RET_HEADS, 1, HEAD_DIM))
    return dmat, xi_b, zeta_b, gc_b


def _group_norm(o):
    mu = jnp.mean(o, axis=-1, keepdims=True)
    d = o - mu
    var = jnp.mean(d * d, axis=-1, keepdims=True)
    return d * lax.rsqrt(var + EPS)


def _ret_chunk(qh, kh, vh, r, dmat, xi, zeta, gc):
    qb = qh.astype(BF16)
    kb = kh.astype(BF16)
    vb = vh.astype(BF16)
    s = lax.dot_general(qb, kb, (((1,), (1,)), ((), ())), preferred_element_type=F32) * dmat
    o = jnp.dot(s.astype(BF16), vb, preferred_element_type=F32)
    o = o + jnp.dot(qb, r.astype(BF16), preferred_element_type=F32) * xi
    kz = (kh.astype(F32) * zeta).astype(BF16)
    r_new = r * gc + lax.dot_general(kz, vb, (((0,), (0,)), ((), ())), preferred_element_type=F32)
    return o, r_new


def _ret_prompt_body(q_ref, k_ref, v_ref, gs_ref, d_ref, xi_ref, zeta_ref, gc_ref, o_ref, st_ref, r_ref):
    j = pl.program_id(1)

    @pl.when(j == 0)
    def _():
        r_ref[...] = jnp.zeros_like(r_ref)

    n_chunks = q_ref.shape[0] // CHUNK
    for hh in range(RET_HEADS):
        sl = slice(hh * HEAD_DIM, (hh + 1) * HEAD_DIM)
        r = r_ref[hh]
        for ci in range(n_chunks):
            rows = slice(ci * CHUNK, (ci + 1) * CHUNK)
            o, r = _ret_chunk(q_ref[rows, sl], k_ref[rows, sl], v_ref[rows, sl], r,
                              d_ref[hh], xi_ref[hh], zeta_ref[hh], gc_ref[hh])
            o_ref[rows, sl] = (gs_ref[rows, sl] * _group_norm(o)).astype(o_ref.dtype)
        r_ref[hh] = r
    st_ref[0] = r_ref[...]


def _ret_prompt(q, k, v, gs, n, l, tl):
    dmat, xi, zeta, gc = _decay_tables(CHUNK)
    per = l // tl
    row = lambda b, j: (b * per + j, 0)
    c3 = lambda b, j: (0, 0, 0)
    return pl.pallas_call(
        _ret_prompt_body,
        grid=(n, per),
        in_specs=[pl.BlockSpec((tl, RET_WIDTH), row)] * 4 + [
            pl.BlockSpec((RET_HEADS, CHUNK, CHUNK), c3),
            pl.BlockSpec((RET_HEADS, CHUNK, HEAD_DIM), c3),
            pl.BlockSpec((RET_HEADS, CHUNK, HEAD_DIM), c3),
            pl.BlockSpec((RET_HEADS, 1, HEAD_DIM), c3),
        ],
        out_specs=[
            pl.BlockSpec((tl, RET_WIDTH), row),
            pl.BlockSpec((1, RET_HEADS, HEAD_DIM, HEAD_DIM), lambda b, j: (b, 0, 0, 0)),
        ],
        out_shape=[
            jax.ShapeDtypeStruct((n * l, RET_WIDTH), BF16),
            jax.ShapeDtypeStruct((n, RET_HEADS, HEAD_DIM, HEAD_DIM), F32),
        ],
        scratch_shapes=[pltpu.VMEM((RET_HEADS, HEAD_DIM, HEAD_DIM), F32)],
        compiler_params=_cparams(("arbitrary", "arbitrary")),
        name="ret_prompt",
    )(q, k, v, gs, dmat, xi, zeta, gc)


def _ret_sample_body(q_ref, k_ref, v_ref, gs_ref, st_ref, d_ref, xi_ref, zeta_ref, gc_ref, o_ref, nst_ref):
    nb = st_ref.shape[0]
    l = q_ref.shape[0] // nb
    for b in range(nb):
        rows = slice(b * l, (b + 1) * l)
        for hh in range(RET_HEADS):
            sl = slice(hh * HEAD_DIM, (hh + 1) * HEAD_DIM)
            o, r = _ret_chunk(q_ref[rows, sl], k_ref[rows, sl], v_ref[rows, sl], st_ref[b, hh],
                              d_ref[hh], xi_ref[hh], zeta_ref[hh], gc_ref[hh])
            o_ref[rows, sl] = (gs_ref[rows, sl] * _group_norm(o)).astype(o_ref.dtype)
            nst_ref[b, hh] = r


def _ret_sample(q, k, v, gs, state, l, nb):
    n = state.shape[0]
    dmat, xi, zeta, gc = _decay_tables(l)
    row = lambda b: (b, 0)
    c3 = lambda b: (0, 0, 0)
    blk4 = lambda b: (b, 0, 0, 0)
    return pl.pallas_call(
        _ret_sample_body,
        grid=(n // nb,),
        in_specs=[pl.BlockSpec((nb * l, RET_WIDTH), row)] * 4 + [
            pl.BlockSpec((nb, RET_HEADS, HEAD_DIM, HEAD_DIM), blk4),
            pl.BlockSpec((RET_HEADS, l, l), c3),
            pl.BlockSpec((RET_HEADS, l, HEAD_DIM), c3),
            pl.BlockSpec((RET_HEADS, l, HEAD_DIM), c3),
            pl.BlockSpec((RET_HEADS, 1, HEAD_DIM), c3),
        ],
        out_specs=[
            pl.BlockSpec((nb * l, RET_WIDTH), row),
            pl.BlockSpec((nb, RET_HEADS, HEAD_DIM, HEAD_DIM), blk4),
        ],
        out_shape=[
            jax.ShapeDtypeStruct((n * l, RET_WIDTH), BF16),
            jax.ShapeDtypeStruct((n, RET_HEADS, HEAD_DIM, HEAD_DIM), F32),
        ],
        compiler_params=_cparams(("parallel",)),
        name="ret_sample",
    )(q, k, v, gs, state, dmat, xi, zeta, gc)


def _split3(x):
    hi = x.astype(BF16)
    r1 = x - hi.astype(F32)
    mid = r1.astype(BF16)
    lo = (r1 - mid.astype(F32)).astype(BF16)
    return hi, mid, lo


def _dot_hp(t, w2_ref):
    t_hi, t_mid, _ = _split3(t)
    d = functools.partial(jnp.dot, preferred_element_type=F32)
    both = d(t_hi, w2_ref[...])
    return both[:, :LANES] + (d(t_mid, w2_ref[:, 0:LANES]) + both[:, LANES:])


def _route(logits):
    lt = logits.T
    tm = lt.shape[1]
    row = _iota_f32((SUBLANES, tm), 0)
    big = float(SUBLANES)

    def rmax(x):
        return jnp.max(x, axis=0, keepdims=True)

    def first_row(mask):
        return jnp.min(jnp.where(mask, row, big), axis=0, keepdims=True)

    lg = jnp.where(row < float(N_GROUPS), lt[N_EXPERTS:N_EXPERTS + SUBLANES, :], -1e30)
    m = rmax(lg)
    g_top = 1.0 / jnp.sum(jnp.exp(lg - m), axis=0, keepdims=True)
    g_idx = first_row(lg == m)
    lem = lt[0:EXPERTS_PER_GROUP, :]
    for g in range(1, N_GROUPS):
        lem = jnp.where(g_idx == float(g), lt[g * EXPERTS_PER_GROUP:(g + 1) * EXPERTS_PER_GROUP, :], lem)
    pe = jnp.exp(lem - rmax(lem))
    p1 = rmax(pe)
    e1 = first_row(pe == p1)
    rest = row != e1
    pe2 = jnp.where(rest, pe, -1.0)
    p2 = rmax(pe2)
    e2 = first_row(rest & (pe2 == p2))
    scale = g_top / (p1 + p2)
    base = g_idx * float(EXPERTS_PER_GROUP)
    rec_t = jnp.where(row == 0.0, base + e1, jnp.where(row == 1.0, base + e2, 0.0))
    rec_t = rec_t + jnp.where(row == 2.0, p1 * scale, jnp.where(row == 3.0, p2 * scale, 0.0))
    return jnp.concatenate([rec_t, jnp.zeros((LANES - SUBLANES, tm), F32)], axis=0).T


def _outproj_body(c_ref, o_ref, x_ref, wo_ref, g2_ref, wr_ref, br_ref, x1_ref, t_ref, rec_ref):
    x1 = x_ref[...] + jnp.dot(c_ref[...], wo_ref[0:CONV_CH, :], preferred_element_type=F32)
    x1 = x1 + jnp.dot(o_ref[...], wo_ref[CONV_CH:, :], preferred_element_type=F32)
    x1_ref[...] = x1
    t = _rms(x1, g2_ref[...])
    t_ref[...] = t.astype(t_ref.dtype)
    rec_ref[...] = _route(_dot_hp(t, wr_ref) + br_ref[...])


def _outproj(c, o, x, w_out, g2, wr2, br, tm):
    t = x.shape[0]
    row = lambda i: (i, 0)
    const = lambda i: (0, 0)
    c3 = lambda i: (0, 0, 0)
    return pl.pallas_call(
        _outproj_body,
        grid=(t // tm,),
        in_specs=[
            pl.BlockSpec((tm, CONV_CH), row),
            pl.BlockSpec((tm, RET_WIDTH), row),
            pl.BlockSpec((tm, D_MODEL), row),
            pl.BlockSpec((D_MODEL, D_MODEL), const),
            pl.BlockSpec((1, D_MODEL), const),
            pl.BlockSpec((D_MODEL, 2 * LANES), const),
            pl.BlockSpec((1, LANES), const),
        ],
        out_specs=[
            pl.BlockSpec((tm, D_MODEL), row),
            pl.BlockSpec((tm, D_MODEL), row),
            pl.BlockSpec((tm, LANES), row),
        ],
        out_shape=[
            jax.ShapeDtypeStruct((t, D_MODEL), F32),
            jax.ShapeDtypeStruct((t, D_MODEL), BF16),
            jax.ShapeDtypeStruct((t, LANES), F32),
        ],
        compiler_params=_cparams(("parallel",)),
        name="outproj_router",
    )(c, o, x, w_out, g2, wr2, br)


def _mix_body(x_ref, g1_ref, w_ref, cos_ref, sin_ref, cw_ref, cb_ref, lg_ref, lb_ref,
              d_ref, xi_ref, zeta_ref, gc_ref, wo_ref, g2_ref, wr_ref, br_ref,
              x1_ref, t_ref, rec_ref, cst_ref, rst_ref, ext_ref, r_ref, o_ref):
    j = pl.program_id(1)
    tl = x_ref.shape[0]
    x = x_ref[...]
    z = jnp.dot(_rms(x, g1_ref[...]).astype(BF16), w_ref[...], preferred_element_type=F32)

    @pl.when(j == 0)
    def _():
        ext_ref[0:HALO, :] = jnp.zeros((HALO, CONV_CH), F32)
        ext_ref[tl + HALO:, :] = jnp.zeros((SUBLANES, CONV_CH), F32)
        r_ref[...] = jnp.zeros_like(r_ref)

    @pl.when(j > 0)
    def _():
        ext_ref[0:HALO, :] = ext_ref[tl:tl + HALO, :]

    ext_ref[HALO:tl + HALO, :] = z[:, :CONV_CH] * jax.nn.sigmoid(z[:, CONV_CH:2 * CONV_CH])
    acc = _dwconv(lambda s, n: ext_ref[s:s + n, :], cw_ref, tl, 0) + cb_ref[...]
    c = _ln_silu(acc, lg_ref[...], lb_ref[...]).astype(BF16)
    cst_ref[0] = ext_ref[tl + HALO_OFF:tl + HALO, :]

    cos = cos_ref[...]
    sin = sin_ref[...]
    q0 = 2 * CONV_CH
    k0 = q0 + RET_WIDTH
    v0 = k0 + RET_WIDTH
    g0 = v0 + RET_WIDTH
    for hh in range(RET_HEADS):
        lo = hh * HEAD_DIM
        qh = z[:, q0 + lo:q0 + lo + HEAD_DIM]
        kh = z[:, k0 + lo:k0 + lo + HEAD_DIM]
        qr = (qh * cos + pltpu.roll(qh, HEAD_DIM // 2, 1) * sin).astype(BF16)
        kr = ((kh * cos + pltpu.roll(kh, HEAD_DIM // 2, 1) * sin) * (HEAD_DIM ** -0.5)).astype(BF16)
        vh = z[:, v0 + lo:v0 + lo + HEAD_DIM].astype(BF16)
        g = z[:, g0 + lo:g0 + lo + HEAD_DIM]
        gs = g * jax.nn.sigmoid(g)
        r = r_ref[hh]
        for ci in range(tl // CHUNK):
            rows = slice(ci * CHUNK, (ci + 1) * CHUNK)
            o, r = _ret_chunk(qr[rows], kr[rows], vh[rows], r, d_ref[hh], xi_ref[hh], zeta_ref[hh], gc_ref[hh])
            o_ref[rows, lo:lo + HEAD_DIM] = (gs[rows] * _group_norm(o)).astype(BF16)
        r_ref[hh] = r
    rst_ref[0] = r_ref[...]

    x1 = x + jnp.dot(c, wo_ref[0:CONV_CH, :], preferred_element_type=F32)
    x1 = x1 + jnp.dot(o_ref[...], wo_ref[CONV_CH:, :], preferred_element_type=F32)
    x1_ref[...] = x1
    t = _rms(x1, g2_ref[...])
    t_ref[...] = t.astype(t_ref.dtype)
    rec_ref[...] = _route(_dot_hp(t, wr_ref) + br_ref[...])


def _mix(x, g1, w_in, cos, sin, conv_w, conv_b, ln_g, ln_b, w_out, g2, wr2, br, n, l, tl):
    dmat, xi, zeta, gc = _decay_tables(CHUNK)
    per = l // tl
    row = lambda b, j: (b * per + j, 0)
    tab = lambda b, j: (j, 0)
    const = lambda b, j: (0, 0)
    c3 = lambda b, j: (0, 0, 0)
    once = dict(pipeline_mode=pl.Buffered(1))
    return pl.pallas_call(
        _mix_body,
        grid=(n, per),
        in_specs=[
            pl.BlockSpec((tl, D_MODEL), row),
            pl.BlockSpec((1, D_MODEL), const),
            pl.BlockSpec((D_MODEL, IN_COLS), const, **once),
            pl.BlockSpec((tl, HEAD_DIM), tab),
            pl.BlockSpec((tl, HEAD_DIM), tab),
            pl.BlockSpec((CONV_K, CONV_CH), const),
            pl.BlockSpec((1, CONV_CH), const),
            pl.BlockSpec((1, CONV_CH), const),
            pl.BlockSpec((1, CONV_CH), const),
            pl.BlockSpec((RET_HEADS, CHUNK, CHUNK), c3),
            pl.BlockSpec((RET_HEADS, CHUNK, HEAD_DIM), c3),
            pl.BlockSpec((RET_HEADS, CHUNK, HEAD_DIM), c3),
            pl.BlockSpec((RET_HEADS, 1, HEAD_DIM), c3),
            pl.BlockSpec((D_MODEL, D_MODEL), const, **once),
            pl.BlockSpec((1, D_MODEL), const),
            pl.BlockSpec((D_MODEL, 2 * LANES), const, **once),
            pl.BlockSpec((1, LANES), const),
        ],
        out_specs=[
            pl.BlockSpec((tl, D_MODEL), row),
            pl.BlockSpec((tl, D_MODEL), row),
            pl.BlockSpec((tl, LANES), row),
            pl.BlockSpec((1, CONV_K - 1, CONV_CH), lambda b, j: (b, 0, 0)),
            pl.BlockSpec((1, RET_HEADS, HEAD_DIM, HEAD_DIM), lambda b, j: (b, 0, 0, 0)),
        ],
        out_shape=[
            jax.ShapeDtypeStruct((n * l, D_MODEL), F32),
            jax.ShapeDtypeStruct((n * l, D_MODEL), BF16),
            jax.ShapeDtypeStruct((n * l, LANES), F32),
            jax.ShapeDtypeStruct((n, CONV_K - 1, CONV_CH), F32),
            jax.ShapeDtypeStruct((n, RET_HEADS, HEAD_DIM, HEAD_DIM), F32),
        ],
        scratch_shapes=[
            pltpu.VMEM((tl + HALO + SUBLANES, CONV_CH), F32),
            pltpu.VMEM((RET_HEADS, HEAD_DIM, HEAD_DIM), F32),
            pltpu.VMEM((tl, RET_WIDTH), BF16),
        ],
        compiler_params=pltpu.CompilerParams(dimension_semantics=("arbitrary", "arbitrary"),
                                             vmem_limit_bytes=MIX_VMEM_LIMIT),
        name="token_mix",
    )(x, g1, w_in, cos, sin, conv_w, conv_b, ln_g, ln_b, dmat, xi, zeta, gc, w_out, g2, wr2, br)


def _iota_f32(shape, dim):
    return lax.broadcasted_iota(jnp.int32, shape, dim).astype(F32)


def _dispatch_body(rec_a_ref, t_a_ref, rec_b_ref, t_b_ref, s_ref, pos_ref, meta_ref, *, nsub_a):
    from_a = pl.program_id(0) * DISPATCH_SUBS < nsub_a
    for s in range(DISPATCH_SUBS):
        rows = slice(s * SUB, (s + 1) * SUB)
        rec = jnp.where(from_a, rec_a_ref[rows, :], rec_b_ref[rows, :])
        tok = jnp.where(from_a, t_a_ref[rows, :], t_b_ref[rows, :])
        chunks = slice(s * CHUNKS_PER_SUB, (s + 1) * CHUNKS_PER_SUB)
        _dispatch_sub_tile(rec, tok, s_ref.at[chunks], pos_ref.at[rows], meta_ref.at[s])


def _dispatch_sub_tile(rec, tok, s_ref, pos_ref, meta_ref):
    lane = _iota_f32(rec.shape, 1)
    a1 = lane == rec[:, 0:1]
    a2 = lane == rec[:, 1:2]
    a1f = jnp.where(a1, 1.0, 0.0)
    a2f = jnp.where(a2, 1.0, 0.0)
    ltri = jnp.where(_iota_f32((SUB, SUB), 1) < _iota_f32((SUB, SUB), 0), 1.0, 0.0).astype(BF16)
    c1 = jnp.dot(ltri, a1f.astype(BF16), preferred_element_type=F32)
    c2 = jnp.dot(ltri, a2f.astype(BF16), preferred_element_type=F32)
    n1 = jnp.sum(a1f, axis=0, keepdims=True)
    n2 = jnp.sum(a2f, axis=0, keepdims=True)
    cnt = jnp.floor((n1 + n2 + (ROW_ALIGN - 1.0)) * (1.0 / ROW_ALIGN))
    utri = jnp.where(_iota_f32((LANES, LANES), 0) < _iota_f32((LANES, LANES), 1), 1.0, 0.0).astype(BF16)
    start = jnp.dot(jnp.broadcast_to(cnt, (SUBLANES, LANES)).astype(BF16), utri,
                    preferred_element_type=F32)[0:1]
    base1 = start * ROW_ALIGN
    base2 = base1 + n1
    pos1 = jnp.sum(jnp.where(a1, c1 + base1, 0.0), axis=1, keepdims=True)
    pos2 = jnp.sum(jnp.where(a2, c2 + base2, 0.0), axis=1, keepdims=True)
    posm = jnp.where(lane == 0.0, pos1, jnp.where(lane == 1.0, pos2, 0.0))
    pos_ref[...] = posm
    row = _iota_f32((SUBLANES, LANES), 0)
    meta_ref[...] = jnp.where(row == 0.0, start, jnp.where(row == 1.0, cnt, 0.0))

    g1 = _split3(rec[:, 2:3])
    g2 = _split3(rec[:, 3:4])
    info = jnp.where(lane == 6.0, rec[:, 0:1], jnp.where(lane == 7.0, rec[:, 1:2], 0.0))
    for i in range(3):
        info = jnp.where(lane == float(i), g1[i].astype(F32), info)
        info = jnp.where(lane == float(3 + i), g2[i].astype(F32), info)
    src = jnp.concatenate([tok, info.astype(BF16)], axis=1)

    post = posm.T
    r = _iota_f32((SUBP, SUB), 0)
    onehot = jnp.where(r == post[0:1, :], 1.0, jnp.where(r == post[1:2, :], 1.0, 0.0)).astype(BF16)
    sorted_rows = jnp.dot(onehot, src, preferred_element_type=F32).astype(BF16)
    s_ref[...] = sorted_rows.reshape(CHUNKS_PER_SUB, ROW_ALIGN, ROW_W)


def _dispatch(rec_a, t_a, rec_b, t_b):
    nsub_a = rec_a.shape[0] // SUB
    nsub_b = rec_b.shape[0] // SUB
    nsub = nsub_a + nsub_b
    assert nsub_a % DISPATCH_SUBS == 0 and nsub_b % DISPATCH_SUBS == 0
    steps_a = nsub_a // DISPATCH_SUBS
    tm = DISPATCH_SUBS * SUB
    row = lambda i: (i, 0)
    row_a = lambda i: (jnp.minimum(i, steps_a - 1), 0)
    row_b = lambda i: (jnp.maximum(i - steps_a, 0), 0)
    return pl.pallas_call(
        functools.partial(_dispatch_body, nsub_a=nsub_a),
        grid=(nsub // DISPATCH_SUBS,),
        in_specs=[
            pl.BlockSpec((tm, LANES), row_a),
            pl.BlockSpec((tm, D_MODEL), row_a),
            pl.BlockSpec((tm, LANES), row_b),
            pl.BlockSpec((tm, D_MODEL), row_b),
        ],
        out_specs=[
            pl.BlockSpec((DISPATCH_SUBS * CHUNKS_PER_SUB, ROW_ALIGN, ROW_W), lambda i: (i, 0, 0)),
            pl.BlockSpec((tm, LANES), row),
            pl.BlockSpec((DISPATCH_SUBS, SUBLANES, LANES), lambda i: (i, 0, 0)),
        ],
        out_shape=[
            jax.ShapeDtypeStruct((nsub * CHUNKS_PER_SUB, ROW_ALIGN, ROW_W), BF16),
            jax.ShapeDtypeStruct((nsub * SUB, LANES), F32),
            jax.ShapeDtypeStruct((nsub, SUBLANES, LANES), F32),
        ],
        compiler_params=_cparams(("parallel",)),
        name="moe_dispatch",
    )(rec_a, t_a, rec_b, t_b)


def _experts_body(start_ref, cnt_ref, s_in, wg_ref, wu_ref, wd_ref, s_hbm,
                  xbuf, ybuf, wgu_ref, wdb_ref, gsem, ssem, list_ref, state_ref, *, nsub):
    del s_in
    e = pl.program_id(0)
    ne = pl.num_programs(0)
    par = e & 1

    list_max = list_ref.shape[0] // 2

    def gather_copy(src, i, slot):
        return pltpu.make_async_copy(s_hbm.at[src], xbuf.at[slot, i], gsem.at[slot])

    def scatter_copy(dst, i, slot):
        return pltpu.make_async_copy(ybuf.at[slot, i], s_hbm.at[dst], ssem.at[slot])

    def build_list(x, which):
        def per_sub(s, k):
            run = s * N_EXPERTS + x
            c = cnt_ref[run]
            base = s * CHUNKS_PER_SUB + start_ref[run]
            list_ref[k] = base
            list_ref[k + 1] = base + 1

            def per_chunk(i, carry):
                list_ref[k + i] = base + i
                return carry
            lax.fori_loop(2, c, per_chunk, 0)
            return k + c
        first = which * list_max
        state_ref[which] = lax.fori_loop(0, nsub, per_sub, first) - first

    def start_all(copy, which, first, n, slot, counter):
        def body(i, carry):
            copy(list_ref[which * list_max + first + i], i, slot).start()
            return carry
        lax.fori_loop(0, n, body, 0)
        state_ref[counter] = n

    def wait_all(copy, block_copy, slot, counter):
        n = state_ref[counter]

        @pl.when(n == CHUNKS_PER_BLK)
        def _():
            block_copy(slot).wait()

        @pl.when(n < CHUNKS_PER_BLK)
        def _():
            def body(i, carry):
                copy(0, 0, slot).wait()
                return carry
            lax.fori_loop(0, n, body, 0)
        state_ref[counter] = 0

    def gather_block(slot):
        return pltpu.make_async_copy(s_hbm.at[pl.ds(0, CHUNKS_PER_BLK)], xbuf.at[slot], gsem.at[slot])

    def scatter_block(slot):
        return pltpu.make_async_copy(ybuf.at[slot], s_hbm.at[pl.ds(0, CHUNKS_PER_BLK)], ssem.at[slot])

    def block_chunks(total, b):
        return jnp.minimum(total - b * CHUNKS_PER_BLK, CHUNKS_PER_BLK)

    @pl.when(e == 0)
    def _():
        for i in range(6):
            state_ref[i] = 0
        xbuf[...] = jnp.zeros_like(xbuf)
        build_list(0, 0)
        n0 = state_ref[0]

        @pl.when(n0 > 0)
        def _():
            start_all(gather_copy, 0, 0, block_chunks(n0, 0), 0, 2)

    @pl.when(e + 1 < ne)
    def _():
        build_list(e + 1, 1 - par)

    total = state_ref[par]
    nblk = (total + CHUNKS_PER_BLK - 1) // CHUNKS_PER_BLK
    wgu_ref[:, 0:EXPERT_FF] = wg_ref[0].astype(BF16)
    wgu_ref[:, EXPERT_FF:] = wu_ref[0].astype(BF16)
    wdb_ref[...] = wd_ref[0].astype(BF16)
    ef = e.astype(F32)

    def block(b, carry):
        slot = b & 1
        first = b * CHUNKS_PER_BLK

        @pl.when(b + 1 < nblk)
        def _():
            start_all(gather_copy, par, first + CHUNKS_PER_BLK, block_chunks(total, b + 1), 1 - slot, 3 - slot)

        wait_all(gather_copy, gather_block, slot, 2 + slot)
        wait_all(scatter_copy, scatter_block, slot, 4 + slot)
        n_here = block_chunks(total, b)

        def mlp(part):
            chunks = slice(part * PART_CHUNKS, (part + 1) * PART_CHUNKS)
            x = xbuf[slot, chunks].reshape(PART_CHUNKS * ROW_ALIGN, ROW_W)
            info = x[:, D_MODEL:].astype(F32)
            g_first = info[:, 0:1] + info[:, 1:2] + info[:, 2:3]
            g_second = info[:, 3:4] + info[:, 4:5] + info[:, 5:6]
            gate = jnp.where(info[:, 6:7] == ef, g_first, g_second)
            h = jnp.dot(x[:, :D_MODEL], wgu_ref[...], preferred_element_type=F32)
            h1 = h[:, :EXPERT_FF]
            hid = (h1 * jax.nn.sigmoid(h1)) * h[:, EXPERT_FF:] * gate
            y = jnp.dot(hid.astype(BF16), wdb_ref[...], preferred_element_type=F32).astype(BF16)
            ybuf[slot, chunks] = jnp.concatenate([y, x[:, D_MODEL:]], axis=1).reshape(
                PART_CHUNKS, ROW_ALIGN, ROW_W)

        mlp(0)
        for part in range(1, MLP_PARTS):
            pl.when(n_here > part * PART_CHUNKS)(functools.partial(mlp, part))
        start_all(scatter_copy, par, first, n_here, slot, 4 + slot)
        return carry

    lax.fori_loop(0, nblk, block, 0)

    @pl.when(e + 1 < ne)
    def _():
        n1 = state_ref[1 - par]

        @pl.when(n1 > 0)
        def _():
            start_all(gather_copy, 1 - par, 0, block_chunks(n1, 0), 0, 2)

    @pl.when(e == ne - 1)
    def _():
        wait_all(scatter_copy, scatter_block, 0, 4)
        wait_all(scatter_copy, scatter_block, 1, 5)


def _experts(start, cnt, staged, wg, wu, wd):
    nsub = staged.shape[0] // CHUNKS_PER_SUB
    list_max = nsub * SUB // ROW_ALIGN + nsub + LIST_SLACK
    wblk = lambda e, *_: (e, 0, 0)
    grid_spec = pltpu.PrefetchScalarGridSpec(
        num_scalar_prefetch=2,
        grid=(N_EXPERTS,),
        in_specs=[
            pl.BlockSpec(memory_space=pl.ANY),
            pl.BlockSpec((1, D_MODEL, EXPERT_FF), wblk),
            pl.BlockSpec((1, D_MODEL, EXPERT_FF), wblk),
            pl.BlockSpec((1, EXPERT_FF, D_MODEL), wblk),
        ],
        out_specs=pl.BlockSpec(memory_space=pl.ANY),
        scratch_shapes=[
            pltpu.VMEM((2, CHUNKS_PER_BLK, ROW_ALIGN, ROW_W), BF16),
            pltpu.VMEM((2, CHUNKS_PER_BLK, ROW_ALIGN, ROW_W), BF16),
            pltpu.VMEM((D_MODEL, 2 * EXPERT_FF), BF16),
            pltpu.VMEM((EXPERT_FF, D_MODEL), BF16),
            pltpu.SemaphoreType.DMA((2,)),
            pltpu.SemaphoreType.DMA((2,)),
            pltpu.SMEM((2 * list_max,), jnp.int32),
            pltpu.SMEM((6,), jnp.int32),
        ],
    )
    return pl.pallas_call(
        functools.partial(_experts_body, nsub=nsub),
        grid_spec=grid_spec,
        out_shape=jax.ShapeDtypeStruct(staged.shape, staged.dtype),
        input_output_aliases={2: 0},
        compiler_params=_cparams(("arbitrary",)),
        name="moe_experts",
    )(start, cnt, staged, wg, wu, wd)


def _combine_body(ys_ref, pos_ref, x1_ref, p_ref, wp_ref, gp_ref, wpg_ref, gf_ref, y_ref):
    r = _iota_f32((SUB, SUBP), 1)
    moe = []
    for s in range(x1_ref.shape[0] // SUB):
        p1 = pos_ref[s * SUB:(s + 1) * SUB, 0:1]
        p2 = pos_ref[s * SUB:(s + 1) * SUB, 1:2]
        onehot = jnp.where(r == p1, 1.0, jnp.where(r == p2, 1.0, 0.0)).astype(BF16)
        ys = ys_ref[s * CHUNKS_PER_SUB:(s + 1) * CHUNKS_PER_SUB].reshape(SUBP, D_MODEL)
        moe.append(jnp.dot(onehot, ys, preferred_element_type=F32))
    x2 = x1_ref[...] + jnp.concatenate(moe, axis=0)
    ple = _rms(jnp.dot(p_ref[...].astype(BF16), wp_ref[...], preferred_element_type=F32), gp_ref[...])
    gate = jax.nn.sigmoid(jnp.dot(x2.astype(BF16), wpg_ref[...], preferred_element_type=F32))
    y_ref[...] = _rms(x2 + ple * gate, gf_ref[...])


def _combine(ys, pos, x1, p, w_ple, gp, w_ple_gate, gf, sub_off):
    t = x1.shape[0]
    tm = COMBINE_SUBS * SUB
    blk_off = sub_off // COMBINE_SUBS
    assert sub_off % COMBINE_SUBS == 0 and t % tm == 0
    row = lambda i: (i, 0)
    const = lambda i: (0, 0)
    return pl.pallas_call(
        _combine_body,
        grid=(t // tm,),
        in_specs=[
            pl.BlockSpec((COMBINE_SUBS * CHUNKS_PER_SUB, ROW_ALIGN, D_MODEL), lambda i: (i + blk_off, 0, 0)),
            pl.BlockSpec((tm, LANES), lambda i: (i + blk_off, 0)),
            pl.BlockSpec((tm, D_MODEL), row),
            pl.BlockSpec((tm, PLE_DIM), row),
            pl.BlockSpec((PLE_DIM, D_MODEL), const),
            pl.BlockSpec((1, D_MODEL), const),
            pl.BlockSpec((D_MODEL, D_MODEL), const),
            pl.BlockSpec((1, D_MODEL), const),
        ],
        out_specs=pl.BlockSpec((tm, D_MODEL), row),
        out_shape=jax.ShapeDtypeStruct((t, D_MODEL), F32),
        compiler_params=_cparams(("parallel",)),
        name="moe_combine_ple",
    )(ys, pos, x1, p, w_ple, gp, w_ple_gate, gf)


def _rope_tables(pos):
    half = HEAD_DIM // 2
    inv = ROPE_BASE ** (-jnp.arange(half, dtype=F32) / half)
    ang = pos[:, None] * inv[None, :]
    cos = jnp.cos(ang)
    sin = jnp.sin(ang)
    return jnp.concatenate([cos, cos], axis=-1), jnp.concatenate([-sin, sin], axis=-1)


def _router_params(we, be, wg, bg):
    pad = LANES - N_EXPERTS - N_GROUPS
    w = jnp.pad(jnp.concatenate([we, wg], axis=1), ((0, 0), (0, pad)))
    b = jnp.pad(jnp.concatenate([be, bg]), (0, pad))[None, :]
    return jnp.concatenate(_split3(w)[:2], axis=1), b


def kernel(x_prompt, x_sample, p_prompt, p_sample, state_conv, state_ret, w_in, conv_w, conv_b, conv_ln_g, conv_ln_b, w_out, norm1_g, norm2_g, router_group_w, router_group_b, router_expert_w, router_expert_b, w_expert_gate, w_expert_up, w_expert_down, w_ple, ple_norm_g, w_ple_gate, final_norm_g):
    assert w_in.shape[0] == 1, "single-layer trunk"
    nb, seq, _ = x_prompt.shape
    ns, dseq, _ = x_sample.shape
    tm = 512

    w_in_b = w_in[0].astype(BF16)
    w_out_b = w_out[0].astype(BF16)
    w_ple_b = w_ple[0].astype(BF16)
    w_pg_b = w_ple_gate[0].astype(BF16)
    g1 = norm1_g[0][None, :]
    g2 = norm2_g[0][None, :]
    gp = ple_norm_g[0][None, :]
    gf = final_norm_g[None, :]
    cb = conv_b[0][None, :]
    lng = conv_ln_g[0][None, :]
    lnb = conv_ln_b[0][None, :]
    wr2, br = _router_params(router_expert_w[0], router_expert_b[0], router_group_w[0], router_group_b[0])

    cos_p, sin_p = _rope_tables(jnp.arange(seq, dtype=F32) + jnp.float32(0))
    pos_s = jnp.tile(jnp.arange(dseq, dtype=F32) + jnp.float32(PAST_LEN), tm // dseq)
    cos_s, sin_s = _rope_tables(pos_s)

    xp = x_prompt.reshape(nb * seq, D_MODEL)
    x1_p, t_p, rec_p, conv_p, ret_p = _mix(xp, g1, w_in_b, cos_p, sin_p, conv_w[0], cb, lng, lnb, w_out_b, g2,
                                           wr2, br, nb, seq, MIX_TILE)

    xs = x_sample.reshape(ns * dseq, D_MODEL)
    u, q, k, v, gs = _inproj(xs, g1, w_in_b, cos_s, sin_s, tm, 1, F32)
    c, conv_s = _conv_sample(u.reshape(ns, dseq, CONV_CH), state_conv[0], conv_w[0], cb, lng, lnb, 16)
    o, ret_s = _ret_sample(q, k, v, gs, state_ret[0], dseq, 8)
    x1_s, t_s, rec_s = _outproj(c.reshape(ns * dseq, CONV_CH), o, xs, w_out_b, g2, wr2, br, tm)

    staged, pos, meta = _dispatch(rec_p, t_p, rec_s, t_s)
    start = meta[:, 0, :N_EXPERTS].astype(jnp.int32).reshape(-1)
    cnt = meta[:, 1, :N_EXPERTS].astype(jnp.int32).reshape(-1)
    ys = _experts(start, cnt, staged, w_expert_gate[0], w_expert_up[0], w_expert_down[0])

    y_p = _combine(ys, pos, x1_p, p_prompt[0].reshape(nb * seq, PLE_DIM), w_ple_b, gp, w_pg_b, gf, 0)
    y_s = _combine(ys, pos, x1_s, p_sample[0].reshape(ns * dseq, PLE_DIM), w_ple_b, gp, w_pg_b, gf,
                   nb * seq // SUB)

    return (y_p.reshape(nb, seq, D_MODEL), y_s.reshape(ns, dseq, D_MODEL),
            conv_p[None], ret_p[None], conv_s[None], ret_s[None])
```

```python
import functools

import jax
import jax.numpy as jnp
from jax import lax
from jax.experimental import pallas as pl
from jax.experimental.pallas import tpu as pltpu

F32 = jnp.float32
BF16 = jnp.bfloat16

D_MODEL = 1024
PLE_DIM = 256
CONV_CH = 512
CONV_K = 31
RET_WIDTH = 512
RET_HEADS = 4
HEAD_DIM = 128
CHUNK = 128
ROPE_BASE = 10000.0
N_GROUPS = 4
EXPERTS_PER_GROUP = 8
N_EXPERTS = 32
EXPERT_FF = 256
IN_COLS = 3072
EPS = 1e-6
PAST_LEN = 16384

LANES = 128
SUBLANES = 8
HALO = 32
HALO_OFF = HALO - (CONV_K - 1)
VMEM_LIMIT = 48 * 1024 * 1024
MIX_VMEM_LIMIT = 56 * 1024 * 1024
MIX_TILE = 512

SUB = 256
ROW_ALIGN = 16
PBLK = 256
SUBP = -(-(2 * SUB + N_EXPERTS * (ROW_ALIGN - 1)) // PBLK) * PBLK
CHUNKS_PER_SUB = SUBP // ROW_ALIGN
ROW_W = D_MODEL + LANES
MBLK = 512
CHUNKS_PER_BLK = MBLK // ROW_ALIGN
LIST_SLACK = 2
COMBINE_SUBS = 4
DISPATCH_SUBS = 4


def _cparams(sem):
    return pltpu.CompilerParams(dimension_semantics=sem, vmem_limit_bytes=VMEM_LIMIT)


def _rms(x, g):
    return x * lax.rsqrt(jnp.mean(x * x, axis=-1, keepdims=True) + EPS) * g


def _inproj_body(x_ref, g1_ref, w_ref, cos_ref, sin_ref, u_ref, q_ref, k_ref, v_ref, gs_ref):
    h = _rms(x_ref[...], g1_ref[...]).astype(BF16)
    z = jnp.dot(h, w_ref[...], preferred_element_type=F32)
    a = z[:, :CONV_CH]
    b = z[:, CONV_CH:2 * CONV_CH]
    u_ref[...] = a * jax.nn.sigmoid(b)
    cos = cos_ref[...]
    sin = sin_ref[...]
    q0 = 2 * CONV_CH
    k0 = q0 + RET_WIDTH
    for hh in range(RET_HEADS):
        sl = slice(hh * HEAD_DIM, (hh + 1) * HEAD_DIM)
        qh = z[:, q0 + hh * HEAD_DIM:q0 + (hh + 1) * HEAD_DIM]
        kh = z[:, k0 + hh * HEAD_DIM:k0 + (hh + 1) * HEAD_DIM]
        q_ref[:, sl] = (qh * cos + pltpu.roll(qh, HEAD_DIM // 2, 1) * sin).astype(q_ref.dtype)
        kr = (kh * cos + pltpu.roll(kh, HEAD_DIM // 2, 1) * sin) * (HEAD_DIM ** -0.5)
        k_ref[:, sl] = kr.astype(k_ref.dtype)
    v_ref[...] = z[:, k0 + RET_WIDTH:k0 + 2 * RET_WIDTH].astype(v_ref.dtype)
    g = z[:, k0 + 2 * RET_WIDTH:]
    gs_ref[...] = g * jax.nn.sigmoid(g)


def _inproj(x, g1, w_in, cos, sin, tm, table_blocks, qkv_dtype):
    t = x.shape[0]
    row = lambda i: (i, 0)
    const = lambda i: (0, 0)
    tab = (lambda i: (i % table_blocks, 0)) if table_blocks > 1 else const
    return pl.pallas_call(
        _inproj_body,
        grid=(t // tm,),
        in_specs=[
            pl.BlockSpec((tm, D_MODEL), row),
            pl.BlockSpec((1, D_MODEL), const),
            pl.BlockSpec((D_MODEL, IN_COLS), const),
            pl.BlockSpec((tm, HEAD_DIM), tab),
            pl.BlockSpec((tm, HEAD_DIM), tab),
        ],
        out_specs=[pl.BlockSpec((tm, CONV_CH), row)] + [pl.BlockSpec((tm, RET_WIDTH), row)] * 4,
        out_shape=[
            jax.ShapeDtypeStruct((t, CONV_CH), F32),
            jax.ShapeDtypeStruct((t, RET_WIDTH), qkv_dtype),
            jax.ShapeDtypeStruct((t, RET_WIDTH), qkv_dtype),
            jax.ShapeDtypeStruct((t, RET_WIDTH), qkv_dtype),
            jax.ShapeDtypeStruct((t, RET_WIDTH), F32),
        ],
        compiler_params=_cparams(("parallel",)),
        name="inproj",
    )(x, g1, w_in, cos, sin)


def _ln_silu(acc, g, b):
    mu = jnp.mean(acc, axis=-1, keepdims=True)
    d = acc - mu
    var = jnp.mean(d * d, axis=-1, keepdims=True)
    y = d * lax.rsqrt(var + EPS) * g + b
    return y * jax.nn.sigmoid(y)


def _dwconv(load, w_ref, rows, time_axis):
    acc = None
    for b in range(SUBLANES):
        part = None
        for a in range((CONV_K + HALO_OFF) // SUBLANES + 1):
            k = SUBLANES * a + b - HALO_OFF
            if 0 <= k < CONV_K:
                term = load(SUBLANES * a, rows + SUBLANES) * w_ref[k:k + 1, :]
                part = term if part is None else part + term
        if part is not None:
            shifted = lax.slice_in_dim(part, b, b + rows, axis=time_axis)
            acc = shifted if acc is None else acc + shifted
    return acc


def _conv_prompt_body(u_ref, w_ref, cb_ref, lg_ref, lb_ref, c_ref, st_ref, ext_ref):
    j = pl.program_id(1)
    tl = u_ref.shape[1]

    @pl.when(j == 0)
    def _():
        ext_ref[0:HALO, :] = jnp.zeros((HALO, CONV_CH), F32)
        ext_ref[tl + HALO:, :] = jnp.zeros((SUBLANES, CONV_CH), F32)

    @pl.when(j > 0)
    def _():
        ext_ref[0:HALO, :] = ext_ref[tl:tl + HALO, :]

    ext_ref[HALO:tl + HALO, :] = u_ref[0]
    acc = _dwconv(lambda s, n: ext_ref[s:s + n, :], w_ref, tl, 0) + cb_ref[...]
    c_ref[0] = _ln_silu(acc, lg_ref[...], lb_ref[...]).astype(c_ref.dtype)
    st_ref[0] = ext_ref[tl + HALO_OFF:tl + HALO, :]


def _conv_prompt(u, conv_w, conv_b, ln_g, ln_b, tl):
    n, l, _ = u.shape
    const = lambda b, j: (0, 0)
    return pl.pallas_call(
        _conv_prompt_body,
        grid=(n, l // tl),
        in_specs=[
            pl.BlockSpec((1, tl, CONV_CH), lambda b, j: (b, j, 0)),
            pl.BlockSpec((CONV_K, CONV_CH), const),
            pl.BlockSpec((1, CONV_CH), const),
            pl.BlockSpec((1, CONV_CH), const),
            pl.BlockSpec((1, CONV_CH), const),
        ],
        out_specs=[
            pl.BlockSpec((1, tl, CONV_CH), lambda b, j: (b, j, 0)),
            pl.BlockSpec((1, CONV_K - 1, CONV_CH), lambda b, j: (b, 0, 0)),
        ],
        out_shape=[
            jax.ShapeDtypeStruct((n, l, CONV_CH), BF16),
            jax.ShapeDtypeStruct((n, CONV_K - 1, CONV_CH), F32),
        ],
        scratch_shapes=[pltpu.VMEM((tl + HALO + SUBLANES, CONV_CH), F32)],
        compiler_params=_cparams(("arbitrary", "arbitrary")),
        name="conv_prompt",
    )(u, conv_w, conv_b, ln_g, ln_b)


def _conv_sample_body(u_ref, st_ref, w_ref, cb_ref, lg_ref, lb_ref, c_ref, nst_ref, ext_ref):
    nb, l, _ = u_ref.shape
    ext_ref[:, 0:HALO_OFF, :] = jnp.zeros((nb, HALO_OFF, CONV_CH), F32)
    ext_ref[:, HALO_OFF:HALO, :] = st_ref[...]
    ext_ref[:, HALO:l + HALO, :] = u_ref[...]
    ext_ref[:, l + HALO:, :] = jnp.zeros((nb, SUBLANES, CONV_CH), F32)
    acc = _dwconv(lambda s, n: ext_ref[:, s:s + n, :], w_ref, l, 1) + cb_ref[...]
    c_ref[...] = _ln_silu(acc, lg_ref[...], lb_ref[...]).astype(c_ref.dtype)
    nst_ref[...] = ext_ref[:, l + HALO_OFF:l + HALO, :]


def _conv_sample(u, state, conv_w, conv_b, ln_g, ln_b, nb):
    n, l, _ = u.shape
    const = lambda b: (0, 0)
    blk3 = lambda b: (b, 0, 0)
    return pl.pallas_call(
        _conv_sample_body,
        grid=(n // nb,),
        in_specs=[
            pl.BlockSpec((nb, l, CONV_CH), blk3),
            pl.BlockSpec((nb, CONV_K - 1, CONV_CH), blk3),
            pl.BlockSpec((CONV_K, CONV_CH), const),
            pl.BlockSpec((1, CONV_CH), const),
            pl.BlockSpec((1, CONV_CH), const),
            pl.BlockSpec((1, CONV_CH), const),
        ],
        out_specs=[
            pl.BlockSpec((nb, l, CONV_CH), blk3),
            pl.BlockSpec((nb, CONV_K - 1, CONV_CH), blk3),
        ],
        out_shape=[
            jax.ShapeDtypeStruct((n, l, CONV_CH), BF16),
            jax.ShapeDtypeStruct((n, CONV_K - 1, CONV_CH), F32),
        ],
        scratch_shapes=[pltpu.VMEM((nb, l + HALO + SUBLANES, CONV_CH), F32)],
        compiler_params=_cparams(("parallel",)),
        name="conv_sample",
    )(u, state, conv_w, conv_b, ln_g, ln_b)


def _decay_tables(c):
    lg = jnp.log(1.0 - 2.0 ** (-5.0 - jnp.arange(RET_HEADS, dtype=F32)))
    idx = jnp.arange(c, dtype=F32)
    rel = idx[:, None] - idx[None, :]
    dmat = jnp.where(rel[None] >= 0, jnp.exp(jnp.maximum(rel, 0.0)[None] * lg[:, None, None]), 0.0)
    xi = jnp.exp((idx + 1.0)[None, :] * lg[:, None])
    zeta = jnp.exp((c - 1.0 - idx)[None, :] * lg[:, None])
    gc = jnp.exp(c * lg)
    xi_b = jnp.broadcast_to(xi[:, :, None], (RET_HEADS, c, HEAD_DIM))
    zeta_b = jnp.broadcast_to(zeta[:, :, None], (RET_HEADS, c, HEAD_DIM))
    gc_b = jnp.broadcast_to(gc[:, None, None], (RET_HEADS, 1, HEAD_DIM))
    return dmat, xi_b, zeta_b, gc_b


def _group_norm(o):
    mu = jnp.mean(o, axis=-1, keepdims=True)
    d = o - mu
    var = jnp.mean(d * d, axis=-1, keepdims=True)
    return d * lax.rsqrt(var + EPS)


def _ret_chunk(qh, kh, vh, r, dmat, xi, zeta, gc):
    qb = qh.astype(BF16)
    kb = kh.astype(BF16)
    vb = vh.astype(BF16)
    s = lax.dot_general(qb, kb, (((1,), (1,)), ((), ())), preferred_element_type=F32) * dmat
    o = jnp.dot(s.astype(BF16), vb, preferred_element_type=F32)
    o = o + jnp.dot(qb, r.astype(BF16), preferred_element_type=F32) * xi
    kz = (kh.astype(F32) * zeta).astype(BF16)
    r_new = r * gc + lax.dot_general(kz, vb, (((0,), (0,)), ((), ())), preferred_element_type=F32)
    return o, r_new


def _ret_prompt_body(q_ref, k_ref, v_ref, gs_ref, d_ref, xi_ref, zeta_ref, gc_ref, o_ref, st_ref, r_ref):
    j = pl.program_id(1)

    @pl.when(j == 0)
    def _():
        r_ref[...] = jnp.zeros_like(r_ref)

    n_chunks = q_ref.shape[0] // CHUNK
    for hh in range(RET_HEADS):
        sl = slice(hh * HEAD_DIM, (hh + 1) * HEAD_DIM)
        r = r_ref[hh]
        for ci in range(n_chunks):
            rows = slice(ci * CHUNK, (ci + 1) * CHUNK)
            o, r = _ret_chunk(q_ref[rows, sl], k_ref[rows, sl], v_ref[rows, sl], r,
                              d_ref[hh], xi_ref[hh], zeta_ref[hh], gc_ref[hh])
            o_ref[rows, sl] = (gs_ref[rows, sl] * _group_norm(o)).astype(o_ref.dtype)
        r_ref[hh] = r
    st_ref[0] = r_ref[...]


def _ret_prompt(q, k, v, gs, n, l, tl):
    dmat, xi, zeta, gc = _decay_tables(CHUNK)
    per = l // tl
    row = lambda b, j: (b * per + j, 0)
    c3 = lambda b, j: (0, 0, 0)
    return pl.pallas_call(
        _ret_prompt_body,
        grid=(n, per),
        in_specs=[pl.BlockSpec((tl, RET_WIDTH), row)] * 4 + [
            pl.BlockSpec((RET_HEADS, CHUNK, CHUNK), c3),
            pl.BlockSpec((RET_HEADS, CHUNK, HEAD_DIM), c3),
            pl.BlockSpec((RET_HEADS, CHUNK, HEAD_DIM), c3),
            pl.BlockSpec((RET_HEADS, 1, HEAD_DIM), c3),
        ],
        out_specs=[
            pl.BlockSpec((tl, RET_WIDTH), row),
            pl.BlockSpec((1, RET_HEADS, HEAD_DIM, HEAD_DIM), lambda b, j: (b, 0, 0, 0)),
        ],
        out_shape=[
            jax.ShapeDtypeStruct((n * l, RET_WIDTH), BF16),
            jax.ShapeDtypeStruct((n, RET_HEADS, HEAD_DIM, HEAD_DIM), F32),
        ],
        scratch_shapes=[pltpu.VMEM((RET_HEADS, HEAD_DIM, HEAD_DIM), F32)],
        compiler_params=_cparams(("arbitrary", "arbitrary")),
        name="ret_prompt",
    )(q, k, v, gs, dmat, xi, zeta, gc)


def _ret_sample_body(q_ref, k_ref, v_ref, gs_ref, st_ref, d_ref, xi_ref, zeta_ref, gc_ref, o_ref, nst_ref):
    nb = st_ref.shape[0]
    l = q_ref.shape[0] // nb
    for b in range(nb):
        rows = slice(b * l, (b + 1) * l)
        for hh in range(RET_HEADS):
            sl = slice(hh * HEAD_DIM, (hh + 1) * HEAD_DIM)
            o, r = _ret_chunk(q_ref[rows, sl], k_ref[rows, sl], v_ref[rows, sl], st_ref[b, hh],
                              d_ref[hh], xi_ref[hh], zeta_ref[hh], gc_ref[hh])
            o_ref[rows, sl] = (gs_ref[rows, sl] * _group_norm(o)).astype(o_ref.dtype)
            nst_ref[b, hh] = r


def _ret_sample(q, k, v, gs, state, l, nb):
    n = state.shape[0]
    dmat, xi, zeta, gc = _decay_tables(l)
    row = lambda b: (b, 0)
    c3 = lambda b: (0, 0, 0)
    blk4 = lambda b: (b, 0, 0, 0)
    return pl.pallas_call(
        _ret_sample_body,
        grid=(n // nb,),
        in_specs=[pl.BlockSpec((nb * l, RET_WIDTH), row)] * 4 + [
            pl.BlockSpec((nb, RET_HEADS, HEAD_DIM, HEAD_DIM), blk4),
            pl.BlockSpec((RET_HEADS, l, l), c3),
            pl.BlockSpec((RET_HEADS, l, HEAD_DIM), c3),
            pl.BlockSpec((RET_HEADS, l, HEAD_DIM), c3),
            pl.BlockSpec((RET_HEADS, 1, HEAD_DIM), c3),
        ],
        out_specs=[
            pl.BlockSpec((nb * l, RET_WIDTH), row),
            pl.BlockSpec((nb, RET_HEADS, HEAD_DIM, HEAD_DIM), blk4),
        ],
        out_shape=[
            jax.ShapeDtypeStruct((n * l, RET_WIDTH), BF16),
            jax.ShapeDtypeStruct((n, RET_HEADS, HEAD_DIM, HEAD_DIM), F32),
        ],
        compiler_params=_cparams(("parallel",)),
        name="ret_sample",
    )(q, k, v, gs, state, dmat, xi, zeta, gc)


def _split3(x):
    hi = x.astype(BF16)
    r1 = x - hi.astype(F32)
    mid = r1.astype(BF16)
    lo = (r1 - mid.astype(F32)).astype(BF16)
    return hi, mid, lo


def _dot_hp(t, w2_ref):
    t_hi, t_mid, _ = _split3(t)
    d = functools.partial(jnp.dot, preferred_element_type=F32)
    both = d(t_hi, w2_ref[...])
    return both[:, :LANES] + (d(t_mid, w2_ref[:, 0:LANES]) + both[:, LANES:])


def _route(logits):
    lt = logits.T
    tm = lt.shape[1]
    row = _iota_f32((SUBLANES, tm), 0)
    big = float(SUBLANES)

    def rmax(x):
        return jnp.max(x, axis=0, keepdims=True)

    def first_row(mask):
        return jnp.min(jnp.where(mask, row, big), axis=0, keepdims=True)

    lg = jnp.where(row < float(N_GROUPS), lt[N_EXPERTS:N_EXPERTS + SUBLANES, :], -1e30)
    m = rmax(lg)
    g_top = 1.0 / jnp.sum(jnp.exp(lg - m), axis=0, keepdims=True)
    g_idx = first_row(lg == m)
    lem = lt[0:EXPERTS_PER_GROUP, :]
    for g in range(1, N_GROUPS):
        lem = jnp.where(g_idx == float(g), lt[g * EXPERTS_PER_GROUP:(g + 1) * EXPERTS_PER_GROUP, :], lem)
    pe = jnp.exp(lem - rmax(lem))
    p1 = rmax(pe)
    e1 = first_row(pe == p1)
    rest = row != e1
    pe2 = jnp.where(rest, pe, -1.0)
    p2 = rmax(pe2)
    e2 = first_row(rest & (pe2 == p2))
    scale = g_top / (p1 + p2)
    base = g_idx * float(EXPERTS_PER_GROUP)
    rec_t = jnp.where(row == 0.0, base + e1, jnp.where(row == 1.0, base + e2, 0.0))
    rec_t = rec_t + jnp.where(row == 2.0, p1 * scale, jnp.where(row == 3.0, p2 * scale, 0.0))
    return jnp.concatenate([rec_t, jnp.zeros((LANES - SUBLANES, tm), F32)], axis=0).T


def _outproj_body(c_ref, o_ref, x_ref, wo_ref, g2_ref, wr_ref, br_ref, x1_ref, t_ref, rec_ref):
    x1 = x_ref[...] + jnp.dot(c_ref[...], wo_ref[0:CONV_CH, :], preferred_element_type=F32)
    x1 = x1 + jnp.dot(o_ref[...], wo_ref[CONV_CH:, :], preferred_element_type=F32)
    x1_ref[...] = x1
    t = _rms(x1, g2_ref[...])
    t_ref[...] = t.astype(t_ref.dtype)
    rec_ref[...] = _route(_dot_hp(t, wr_ref) + br_ref[...])


def _outproj(c, o, x, w_out, g2, wr2, br, tm):
    t = x.shape[0]
    row = lambda i: (i, 0)
    const = lambda i: (0, 0)
    c3 = lambda i: (0, 0, 0)
    return pl.pallas_call(
        _outproj_body,
        grid=(t // tm,),
        in_specs=[
            pl.BlockSpec((tm, CONV_CH), row),
            pl.BlockSpec((tm, RET_WIDTH), row),
            pl.BlockSpec((tm, D_MODEL), row),
            pl.BlockSpec((D_MODEL, D_MODEL), const),
            pl.BlockSpec((1, D_MODEL), const),
            pl.BlockSpec((D_MODEL, 2 * LANES), const),
            pl.BlockSpec((1, LANES), const),
        ],
        out_specs=[
            pl.BlockSpec((tm, D_MODEL), row),
            pl.BlockSpec((tm, D_MODEL), row),
            pl.BlockSpec((tm, LANES), row),
        ],
        out_shape=[
            jax.ShapeDtypeStruct((t, D_MODEL), F32),
            jax.ShapeDtypeStruct((t, D_MODEL), BF16),
            jax.ShapeDtypeStruct((t, LANES), F32),
        ],
        compiler_params=_cparams(("parallel",)),
        name="outproj_router",
    )(c, o, x, w_out, g2, wr2, br)


def _mix_body(x_ref, g1_ref, w_ref, cos_ref, sin_ref, cw_ref, cb_ref, lg_ref, lb_ref,
              d_ref, xi_ref, zeta_ref, gc_ref, wo_ref, g2_ref, wr_ref, br_ref,
              x1_ref, t_ref, rec_ref, cst_ref, rst_ref, ext_ref, r_ref, o_ref):
    j = pl.program_id(1)
    tl = x_ref.shape[0]
    x = x_ref[...]
    z = jnp.dot(_rms(x, g1_ref[...]).astype(BF16), w_ref[...], preferred_element_type=F32)

    @pl.when(j == 0)
    def _():
        ext_ref[0:HALO, :] = jnp.zeros((HALO, CONV_CH), F32)
        ext_ref[tl + HALO:, :] = jnp.zeros((SUBLANES, CONV_CH), F32)
        r_ref[...] = jnp.zeros_like(r_ref)

    @pl.when(j > 0)
    def _():
        ext_ref[0:HALO, :] = ext_ref[tl:tl + HALO, :]

    ext_ref[HALO:tl + HALO, :] = z[:, :CONV_CH] * jax.nn.sigmoid(z[:, CONV_CH:2 * CONV_CH])
    acc = _dwconv(lambda s, n: ext_ref[s:s + n, :], cw_ref, tl, 0) + cb_ref[...]
    c = _ln_silu(acc, lg_ref[...], lb_ref[...]).astype(BF16)
    cst_ref[0] = ext_ref[tl + HALO_OFF:tl + HALO, :]

    cos = cos_ref[...]
    sin = sin_ref[...]
    q0 = 2 * CONV_CH
    k0 = q0 + RET_WIDTH
    v0 = k0 + RET_WIDTH
    g0 = v0 + RET_WIDTH
    for hh in range(RET_HEADS):
        lo = hh * HEAD_DIM
        qh = z[:, q0 + lo:q0 + lo + HEAD_DIM]
        kh = z[:, k0 + lo:k0 + lo + HEAD_DIM]
        qr = (qh * cos + pltpu.roll(qh, HEAD_DIM // 2, 1) * sin).astype(BF16)
        kr = ((kh * cos + pltpu.roll(kh, HEAD_DIM // 2, 1) * sin) * (HEAD_DIM ** -0.5)).astype(BF16)
        vh = z[:, v0 + lo:v0 + lo + HEAD_DIM].astype(BF16)
        g = z[:, g0 + lo:g0 + lo + HEAD_DIM]
        gs = g * jax.nn.sigmoid(g)
        r = r_ref[hh]
        for ci in range(tl // CHUNK):
            rows = slice(ci * CHUNK, (ci + 1) * CHUNK)
            o, r = _ret_chunk(qr[rows], kr[rows], vh[rows], r, d_ref[hh], xi_ref[hh], zeta_ref[hh], gc_ref[hh])
            o_ref[rows, lo:lo + HEAD_DIM] = (gs[rows] * _group_norm(o)).astype(BF16)
        r_ref[hh] = r
    rst_ref[0] = r_ref[...]

    x1 = x + jnp.dot(c, wo_ref[0:CONV_CH, :], preferred_element_type=F32)
    x1 = x1 + jnp.dot(o_ref[...], wo_ref[CONV_CH:, :], preferred_element_type=F32)
    x1_ref[...] = x1
    t = _rms(x1, g2_ref[...])
    t_ref[...] = t.astype(t_ref.dtype)
    rec_ref[...] = _route(_dot_hp(t, wr_ref) + br_ref[...])


def _mix(x, g1, w_in, cos, sin, conv_w, conv_b, ln_g, ln_b, w_out, g2, wr2, br, n, l, tl):
    dmat, xi, zeta, gc = _decay_tables(CHUNK)
    per = l // tl
    row = lambda b, j: (b * per + j, 0)
    tab = lambda b, j: (j, 0)
    const = lambda b, j: (0, 0)
    c3 = lambda b, j: (0, 0, 0)
    once = dict(pipeline_mode=pl.Buffered(1))
    return pl.pallas_call(
        _mix_body,
        grid=(n, per),
        in_specs=[
            pl.BlockSpec((tl, D_MODEL), row),
            pl.BlockSpec((1, D_MODEL), const),
            pl.BlockSpec((D_MODEL, IN_COLS), const, **once),
            pl.BlockSpec((tl, HEAD_DIM), tab),
            pl.BlockSpec((tl, HEAD_DIM), tab),
            pl.BlockSpec((CONV_K, CONV_CH), const),
            pl.BlockSpec((1, CONV_CH), const),
            pl.BlockSpec((1, CONV_CH), const),
            pl.BlockSpec((1, CONV_CH), const),
            pl.BlockSpec((RET_HEADS, CHUNK, CHUNK), c3),
            pl.BlockSpec((RET_HEADS, CHUNK, HEAD_DIM), c3),
            pl.BlockSpec((RET_HEADS, CHUNK, HEAD_DIM), c3),
            pl.BlockSpec((RET_HEADS, 1, HEAD_DIM), c3),
            pl.BlockSpec((D_MODEL, D_MODEL), const, **once),
            pl.BlockSpec((1, D_MODEL), const),
            pl.BlockSpec((D_MODEL, 2 * LANES), const, **once),
            pl.BlockSpec((1, LANES), const),
        ],
        out_specs=[
            pl.BlockSpec((tl, D_MODEL), row),
            pl.BlockSpec((tl, D_MODEL), row),
            pl.BlockSpec((tl, LANES), row),
            pl.BlockSpec((1, CONV_K - 1, CONV_CH), lambda b, j: (b, 0, 0)),
            pl.BlockSpec((1, RET_HEADS, HEAD_DIM, HEAD_DIM), lambda b, j: (b, 0, 0, 0)),
        ],
        out_shape=[
            jax.ShapeDtypeStruct((n * l, D_MODEL), F32),
            jax.ShapeDtypeStruct((n * l, D_MODEL), BF16),
            jax.ShapeDtypeStruct((n * l, LANES), F32),
            jax.ShapeDtypeStruct((n, CONV_K - 1, CONV_CH), F32),
            jax.ShapeDtypeStruct((n, RET_HEADS, HEAD_DIM, HEAD_DIM), F32),
        ],
        scratch_shapes=[
            pltpu.VMEM((tl + HALO + SUBLANES, CONV_CH), F32),
            pltpu.VMEM((RET_HEADS, HEAD_DIM, HEAD_DIM), F32),
            pltpu.VMEM((tl, RET_WIDTH), BF16),
        ],
        compiler_params=pltpu.CompilerParams(dimension_semantics=("arbitrary", "arbitrary"),
                                             vmem_limit_bytes=MIX_VMEM_LIMIT),
        name="token_mix",
    )(x, g1, w_in, cos, sin, conv_w, conv_b, ln_g, ln_b, dmat, xi, zeta, gc, w_out, g2, wr2, br)


def _iota_f32(shape, dim):
    return lax.broadcasted_iota(jnp.int32, shape, dim).astype(F32)


def _dispatch_body(rec_a_ref, t_a_ref, rec_b_ref, t_b_ref, s_ref, pos_ref, meta_ref, *, nsub_a):
    from_a = pl.program_id(0) * DISPATCH_SUBS < nsub_a
    for s in range(DISPATCH_SUBS):
        rows = slice(s * SUB, (s + 1) * SUB)
        rec = jnp.where(from_a, rec_a_ref[rows, :], rec_b_ref[rows, :])
        tok = jnp.where(from_a, t_a_ref[rows, :], t_b_ref[rows, :])
        chunks = slice(s * CHUNKS_PER_SUB, (s + 1) * CHUNKS_PER_SUB)
        _dispatch_sub_tile(rec, tok, s_ref.at[chunks], pos_ref.at[rows], meta_ref.at[s])


def _dispatch_sub_tile(rec, tok, s_ref, pos_ref, meta_ref):
    lane = _iota_f32(rec.shape, 1)
    a1 = lane == rec[:, 0:1]
    a2 = lane == rec[:, 1:2]
    a1f = jnp.where(a1, 1.0, 0.0)
    a2f = jnp.where(a2, 1.0, 0.0)
    ltri = jnp.where(_iota_f32((SUB, SUB), 1) < _iota_f32((SUB, SUB), 0), 1.0, 0.0).astype(BF16)
    c1 = jnp.dot(ltri, a1f.astype(BF16), preferred_element_type=F32)
    c2 = jnp.dot(ltri, a2f.astype(BF16), preferred_element_type=F32)
    n1 = jnp.sum(a1f, axis=0, keepdims=True)
    n2 = jnp.sum(a2f, axis=0, keepdims=True)
    cnt = jnp.floor((n1 + n2 + (ROW_ALIGN - 1.0)) * (1.0 / ROW_ALIGN))
    utri = jnp.where(_iota_f32((LANES, LANES), 0) < _iota_f32((LANES, LANES), 1), 1.0, 0.0).astype(BF16)
    start = jnp.dot(jnp.broadcast_to(cnt, (SUBLANES, LANES)).astype(BF16), utri,
                    preferred_element_type=F32)[0:1]
    base1 = start * ROW_ALIGN
    base2 = base1 + n1
    pos1 = jnp.sum(jnp.where(a1, c1 + base1, 0.0), axis=1, keepdims=True)
    pos2 = jnp.sum(jnp.where(a2, c2 + base2, 0.0), axis=1, keepdims=True)
    posm = jnp.where(lane == 0.0, pos1, jnp.where(lane == 1.0, pos2, 0.0))
    pos_ref[...] = posm
    row = _iota_f32((SUBLANES, LANES), 0)
    meta_ref[...] = jnp.where(row == 0.0, start, jnp.where(row == 1.0, cnt, 0.0))

    g1 = _split3(rec[:, 2:3])
    g2 = _split3(rec[:, 3:4])
    info = jnp.where(lane == 6.0, rec[:, 0:1], jnp.where(lane == 7.0, rec[:, 1:2], 0.0))
    for i in range(3):
        info = jnp.where(lane == float(i), g1[i].astype(F32), info)
        info = jnp.where(lane == float(3 + i), g2[i].astype(F32), info)
    src = jnp.concatenate([tok, info.astype(BF16)], axis=1)

    post = posm.T
    r = _iota_f32((SUBP, SUB), 0)
    onehot = jnp.where(r == post[0:1, :], 1.0, jnp.where(r == post[1:2, :], 1.0, 0.0)).astype(BF16)
    sorted_rows = jnp.dot(onehot, src, preferred_element_type=F32).astype(BF16)
    s_ref[...] = sorted_rows.reshape(CHUNKS_PER_SUB, ROW_ALIGN, ROW_W)


def _dispatch(rec_a, t_a, rec_b, t_b):
    nsub_a = rec_a.shape[0] // SUB
    nsub_b = rec_b.shape[0] // SUB
    nsub = nsub_a + nsub_b
    assert nsub_a % DISPATCH_SUBS == 0 and nsub_b % DISPATCH_SUBS == 0
    steps_a = nsub_a // DISPATCH_SUBS
    tm = DISPATCH_SUBS * SUB
    row = lambda i: (i, 0)
    row_a = lambda i: (jnp.minimum(i, steps_a - 1), 0)
    row_b = lambda i: (jnp.maximum(i - steps_a, 0), 0)
    return pl.pallas_call(
        functools.partial(_dispatch_body, nsub_a=nsub_a),
        grid=(nsub // DISPATCH_SUBS,),
        in_specs=[
            pl.BlockSpec((tm, LANES), row_a),
            pl.BlockSpec((tm, D_MODEL), row_a),
            pl.BlockSpec((tm, LANES), row_b),
            pl.BlockSpec((tm, D_MODEL), row_b),
        ],
        out_specs=[
            pl.BlockSpec((DISPATCH_SUBS * CHUNKS_PER_SUB, ROW_ALIGN, ROW_W), lambda i: (i, 0, 0)),
            pl.BlockSpec((tm, LANES), row),
            pl.BlockSpec((DISPATCH_SUBS, SUBLANES, LANES), lambda i: (i, 0, 0)),
        ],
        out_shape=[
            jax.ShapeDtypeStruct((nsub * CHUNKS_PER_SUB, ROW_ALIGN, ROW_W), BF16),
            jax.ShapeDtypeStruct((nsub * SUB, LANES), F32),
            jax.ShapeDtypeStruct((nsub, SUBLANES, LANES), F32),
        ],
        compiler_params=_cparams(("parallel",)),
        name="moe_dispatch",
    )(rec_a, t_a, rec_b, t_b)


def _experts_body(start_ref, cnt_ref, s_in, wg_ref, wu_ref, wd_ref, s_hbm,
                  xbuf, ybuf, wgu_ref, wdb_ref, gsem, ssem, list_ref, state_ref, *, nsub):
    del s_in
    e = pl.program_id(0)
    ne = pl.num_programs(0)
    par = e & 1

    list_max = list_ref.shape[0] // 2

    def gather_copy(src, i, slot):
        return pltpu.make_async_copy(s_hbm.at[src], xbuf.at[slot, i], gsem.at[slot])

    def scatter_copy(dst, i, slot):
        return pltpu.make_async_copy(ybuf.at[slot, i], s_hbm.at[dst], ssem.at[slot])

    def build_list(x, which):
        def per_sub(s, k):
            run = s * N_EXPERTS + x
            c = cnt_ref[run]
            base = s * CHUNKS_PER_SUB + start_ref[run]
            list_ref[k] = base
            list_ref[k + 1] = base + 1

            def per_chunk(i, carry):
                list_ref[k + i] = base + i
                return carry
            lax.fori_loop(2, c, per_chunk, 0)
            return k + c
        first = which * list_max
        state_ref[which] = lax.fori_loop(0, nsub, per_sub, first) - first

    def start_all(copy, which, first, n, slot, counter):
        def body(i, carry):
            copy(list_ref[which * list_max + first + i], i, slot).start()
            return carry
        lax.fori_loop(0, n, body, 0)
        state_ref[counter] = n

    def wait_all(copy, block_copy, slot, counter):
        n = state_ref[counter]

        @pl.when(n == CHUNKS_PER_BLK)
        def _():
            block_copy(slot).wait()

        @pl.when(n < CHUNKS_PER_BLK)
        def _():
            def body(i, carry):
                copy(0, 0, slot).wait()
                return carry
            lax.fori_loop(0, n, body, 0)
        state_ref[counter] = 0

    def gather_block(slot):
        return pltpu.make_async_copy(s_hbm.at[pl.ds(0, CHUNKS_PER_BLK)], xbuf.at[slot], gsem.at[slot])

    def scatter_block(slot):
        return pltpu.make_async_copy(ybuf.at[slot], s_hbm.at[pl.ds(0, CHUNKS_PER_BLK)], ssem.at[slot])

    def block_chunks(total, b):
        return jnp.minimum(total - b * CHUNKS_PER_BLK, CHUNKS_PER_BLK)

    @pl.when(e == 0)
    def _():
        for i in range(6):
            state_ref[i] = 0
        xbuf[...] = jnp.zeros_like(xbuf)
        build_list(0, 0)
        n0 = state_ref[0]

        @pl.when(n0 > 0)
        def _():
            start_all(gather_copy, 0, 0, block_chunks(n0, 0), 0, 2)

    @pl.when(e + 1 < ne)
    def _():
        build_list(e + 1, 1 - par)

    total = state_ref[par]
    nblk = (total + CHUNKS_PER_BLK - 1) // CHUNKS_PER_BLK
    wgu_ref[:, 0:EXPERT_FF] = wg_ref[0].astype(BF16)
    wgu_ref[:, EXPERT_FF:] = wu_ref[0].astype(BF16)
    wdb_ref[...] = wd_ref[0].astype(BF16)
    ef = e.astype(F32)

    def block(b, carry):
        slot = b & 1
        first = b * CHUNKS_PER_BLK

        @pl.when(b + 1 < nblk)
        def _():
            start_all(gather_copy, par, first + CHUNKS_PER_BLK, block_chunks(total, b + 1), 1 - slot, 3 - slot)

        wait_all(gather_copy, gather_block, slot, 2 + slot)
        wait_all(scatter_copy, scatter_block, slot, 4 + slot)
        x = xbuf[slot].reshape(MBLK, ROW_W)
        info = x[:, D_MODEL:].astype(F32)
        g_first = info[:, 0:1] + info[:, 1:2] + info[:, 2:3]
        g_second = info[:, 3:4] + info[:, 4:5] + info[:, 5:6]
        gate = jnp.where(info[:, 6:7] == ef, g_first, g_second)
        h = jnp.dot(x[:, :D_MODEL], wgu_ref[...], preferred_element_type=F32)
        h1 = h[:, :EXPERT_FF]
        hid = (h1 * jax.nn.sigmoid(h1)) * h[:, EXPERT_FF:] * gate
        y = jnp.dot(hid.astype(BF16), wdb_ref[...], preferred_element_type=F32).astype(BF16)
        ybuf[slot] = jnp.concatenate([y, x[:, D_MODEL:]], axis=1).reshape(CHUNKS_PER_BLK, ROW_ALIGN, ROW_W)
        start_all(scatter_copy, par, first, block_chunks(total, b), slot, 4 + slot)
        return carry

    lax.fori_loop(0, nblk, block, 0)

    @pl.when(e + 1 < ne)
    def _():
        n1 = state_ref[1 - par]

        @pl.when(n1 > 0)
        def _():
            start_all(gather_copy, 1 - par, 0, block_chunks(n1, 0), 0, 2)

    @pl.when(e == ne - 1)
    def _():
        wait_all(scatter_copy, scatter_block, 0, 4)
        wait_all(scatter_copy, scatter_block, 1, 5)


def _experts(start, cnt, staged, wg, wu, wd):
    nsub = staged.shape[0] // CHUNKS_PER_SUB
    list_max = nsub * SUB // ROW_ALIGN + nsub + LIST_SLACK
    wblk = lambda e, *_: (e, 0, 0)
    grid_spec = pltpu.PrefetchScalarGridSpec(
        num_scalar_prefetch=2,
        grid=(N_EXPERTS,),
        in_specs=[
            pl.BlockSpec(memory_space=pl.ANY),
            pl.BlockSpec((1, D_MODEL, EXPERT_FF), wblk),
            pl.BlockSpec((1, D_MODEL, EXPERT_FF), wblk),
            pl.BlockSpec((1, EXPERT_FF, D_MODEL), wblk),
        ],
        out_specs=pl.BlockSpec(memory_space=pl.ANY),
        scratch_shapes=[
            pltpu.VMEM((2, CHUNKS_PER_BLK, ROW_ALIGN, ROW_W), BF16),
            pltpu.VMEM((2, CHUNKS_PER_BLK, ROW_ALIGN, ROW_W), BF16),
            pltpu.VMEM((D_MODEL, 2 * EXPERT_FF), BF16),
            pltpu.VMEM((EXPERT_FF, D_MODEL), BF16),
            pltpu.SemaphoreType.DMA((2,)),
            pltpu.SemaphoreType.DMA((2,)),
            pltpu.SMEM((2 * list_max,), jnp.int32),
            pltpu.SMEM((6,), jnp.int32),
        ],
    )
    return pl.pallas_call(
        functools.partial(_experts_body, nsub=nsub),
        grid_spec=grid_spec,
        out_shape=jax.ShapeDtypeStruct(staged.shape, staged.dtype),
        input_output_aliases={2: 0},
        compiler_params=_cparams(("arbitrary",)),
        name="moe_experts",
    )(start, cnt, staged, wg, wu, wd)


def _combine_body(ys_ref, pos_ref, x1_ref, p_ref, wp_ref, gp_ref, wpg_ref, gf_ref, y_ref):
    r = _iota_f32((SUB, SUBP), 1)
    moe = []
    for s in range(x1_ref.shape[0] // SUB):
        p1 = pos_ref[s * SUB:(s + 1) * SUB, 0:1]
        p2 = pos_ref[s * SUB:(s + 1) * SUB, 1:2]
        onehot = jnp.where(r == p1, 1.0, jnp.where(r == p2, 1.0, 0.0)).astype(BF16)
        ys = ys_ref[s * CHUNKS_PER_SUB:(s + 1) * CHUNKS_PER_SUB].reshape(SUBP, D_MODEL)
        moe.append(jnp.dot(onehot, ys, preferred_element_type=F32))
    x2 = x1_ref[...] + jnp.concatenate(moe, axis=0)
    ple = _rms(jnp.dot(p_ref[...].astype(BF16), wp_ref[...], preferred_element_type=F32), gp_ref[...])
    gate = jax.nn.sigmoid(jnp.dot(x2.astype(BF16), wpg_ref[...], preferred_element_type=F32))
    y_ref[...] = _rms(x2 + ple * gate, gf_ref[...])


def _combine(ys, pos, x1, p, w_ple, gp, w_ple_gate, gf, sub_off):
    t = x1.shape[0]
    tm = COMBINE_SUBS * SUB
    blk_off = sub_off // COMBINE_SUBS
    assert sub_off % COMBINE_SUBS == 0 and t % tm == 0
    row = lambda i: (i, 0)
    const = lambda i: (0, 0)
    return pl.pallas_call(
        _combine_body,
        grid=(t // tm,),
        in_specs=[
            pl.BlockSpec((COMBINE_SUBS * CHUNKS_PER_SUB, ROW_ALIGN, D_MODEL), lambda i: (i + blk_off, 0, 0)),
            pl.BlockSpec((tm, LANES), lambda i: (i + blk_off, 0)),
            pl.BlockSpec((tm, D_MODEL), row),
            pl.BlockSpec((tm, PLE_DIM), row),
            pl.BlockSpec((PLE_DIM, D_MODEL), const),
            pl.BlockSpec((1, D_MODEL), const),
            pl.BlockSpec((D_MODEL, D_MODEL), const),
            pl.BlockSpec((1, D_MODEL), const),
        ],
        out_specs=pl.BlockSpec((tm, D_MODEL), row),
        out_shape=jax.ShapeDtypeStruct((t, D_MODEL), F32),
        compiler_params=_cparams(("parallel",)),
        name="moe_combine_ple",
    )(ys, pos, x1, p, w_ple, gp, w_ple_gate, gf)


def _rope_tables(pos):
    half = HEAD_DIM // 2
    inv = ROPE_BASE ** (-jnp.arange(half, dtype=F32) / half)
    ang = pos[:, None] * inv[None, :]
    cos = jnp.cos(ang)
    sin = jnp.sin(ang)
    return jnp.concatenate([cos, cos], axis=-1), jnp.concatenate([-sin, sin], axis=-1)


def _router_params(we, be, wg, bg):
    pad = LANES - N_EXPERTS - N_GROUPS
    w = jnp.pad(jnp.concatenate([we, wg], axis=1), ((0, 0), (0, pad)))
    b = jnp.pad(jnp.concatenate([be, bg]), (0, pad))[None, :]
    return jnp.concatenate(_split3(w)[:2], axis=1), b


def kernel(x_prompt, x_sample, p_prompt, p_sample, state_conv, state_ret, w_in, conv_w, conv_b, conv_ln_g, conv_ln_b, w_out, norm1_g, norm2_g, router_group_w, router_group_b, router_expert_w, router_expert_b, w_expert_gate, w_expert_up, w_expert_down, w_ple, ple_norm_g, w_ple_gate, final_norm_g):
    assert w_in.shape[0] == 1, "single-layer trunk"
    nb, seq, _ = x_prompt.shape
    ns, dseq, _ = x_sample.shape
    tm = 512

    w_in_b = w_in[0].astype(BF16)
    w_out_b = w_out[0].astype(BF16)
    w_ple_b = w_ple[0].astype(BF16)
    w_pg_b = w_ple_gate[0].astype(BF16)
    g1 = norm1_g[0][None, :]
    g2 = norm2_g[0][None, :]
    gp = ple_norm_g[0][None, :]
    gf = final_norm_g[None, :]
    cb = conv_b[0][None, :]
    lng = conv_ln_g[0][None, :]
    lnb = conv_ln_b[0][None, :]
    wr2, br = _router_params(router_expert_w[0], router_expert_b[0], router_group_w[0], router_group_b[0])

    cos_p, sin_p = _rope_tables(jnp.arange(seq, dtype=F32) + jnp.float32(0))
    pos_s = jnp.tile(jnp.arange(dseq, dtype=F32) + jnp.float32(PAST_LEN), tm // dseq)
    cos_s, sin_s = _rope_tables(pos_s)

    xp = x_prompt.reshape(nb * seq, D_MODEL)
    x1_p, t_p, rec_p, conv_p, ret_p = _mix(xp, g1, w_in_b, cos_p, sin_p, conv_w[0], cb, lng, lnb, w_out_b, g2,
                                           wr2, br, nb, seq, MIX_TILE)

    xs = x_sample.reshape(ns * dseq, D_MODEL)
    u, q, k, v, gs = _inproj(xs, g1, w_in_b, cos_s, sin_s, tm, 1, F32)
    c, conv_s = _conv_sample(u.reshape(ns, dseq, CONV_CH), state_conv[0], conv_w[0], cb, lng, lnb, 16)
    o, ret_s = _ret_sample(q, k, v, gs, state_ret[0], dseq, 8)
    x1_s, t_s, rec_s = _outproj(c.reshape(ns * dseq, CONV_CH), o, xs, w_out_b, g2, wr2, br, tm)

    staged, pos, meta = _dispatch(rec_p, t_p, rec_s, t_s)
    start = meta[:, 0, :N_EXPERTS].astype(jnp.int32).reshape(-1)
    cnt = meta[:, 1, :N_EXPERTS].astype(jnp.int32).reshape(-1)
    ys = _experts(start, cnt, staged, w_expert_gate[0], w_expert_up[0], w_expert_down[0])

    y_p = _combine(ys, pos, x1_p, p_prompt[0].reshape(nb * seq, PLE_DIM), w_ple_b, gp, w_pg_b, gf, 0)
    y_s = _combine(ys, pos, x1_s, p_sample[0].reshape(ns * dseq, PLE_DIM), w_ple_b, gp, w_pg_b, gf,
                   nb * seq // SUB)

    return (y_p.reshape(nb, seq, D_MODEL), y_s.reshape(ns, dseq, D_MODEL),
            conv_p[None], ret_p[None], conv_s[None], ret_s[None])
```

```python
import functools

import jax
import jax.numpy as jnp
from jax import lax
from jax.experimental import pallas as pl
from jax.experimental.pallas import tpu as pltpu

F32 = jnp.float32
BF16 = jnp.bfloat16

D_MODEL = 1024
PLE_DIM = 256
CONV_CH = 512
CONV_K = 31
RET_WIDTH = 512
RET_HEADS = 4
HEAD_DIM = 128
CHUNK = 128
ROPE_BASE = 10000.0
N_GROUPS = 4
EXPERTS_PER_GROUP = 8
N_EXPERTS = 32
EXPERT_FF = 256
IN_COLS = 3072
EPS = 1e-6
PAST_LEN = 16384

LANES = 128
SUBLANES = 8
HALO = 32
HALO_OFF = HALO - (CONV_K - 1)
VMEM_LIMIT = 48 * 1024 * 1024
MIX_VMEM_LIMIT = 56 * 1024 * 1024
MIX_TILE = 512

SUB = 256
ROW_ALIGN = 16
PBLK = 256
SUBP = -(-(2 * SUB + N_EXPERTS * (ROW_ALIGN - 1)) // PBLK) * PBLK
CHUNKS_PER_SUB = SUBP // ROW_ALIGN
ROW_W = D_MODEL + LANES
MBLK = 1024
CHUNKS_PER_BLK = MBLK // ROW_ALIGN
LIST_SLACK = 2
COMBINE_SUBS = 4
DISPATCH_SUBS = 4


def _cparams(sem):
    return pltpu.CompilerParams(dimension_semantics=sem, vmem_limit_bytes=VMEM_LIMIT)


def _rms(x, g):
    return x * lax.rsqrt(jnp.mean(x * x, axis=-1, keepdims=True) + EPS) * g


def _inproj_body(x_ref, g1_ref, w_ref, cos_ref, sin_ref, u_ref, q_ref, k_ref, v_ref, gs_ref):
    h = _rms(x_ref[...], g1_ref[...]).astype(BF16)
    z = jnp.dot(h, w_ref[...], preferred_element_type=F32)
    a = z[:, :CONV_CH]
    b = z[:, CONV_CH:2 * CONV_CH]
    u_ref[...] = a * jax.nn.sigmoid(b)
    cos = cos_ref[...]
    sin = sin_ref[...]
    q0 = 2 * CONV_CH
    k0 = q0 + RET_WIDTH
    for hh in range(RET_HEADS):
        sl = slice(hh * HEAD_DIM, (hh + 1) * HEAD_DIM)
        qh = z[:, q0 + hh * HEAD_DIM:q0 + (hh + 1) * HEAD_DIM]
        kh = z[:, k0 + hh * HEAD_DIM:k0 + (hh + 1) * HEAD_DIM]
        q_ref[:, sl] = (qh * cos + pltpu.roll(qh, HEAD_DIM // 2, 1) * sin).astype(q_ref.dtype)
        kr = (kh * cos + pltpu.roll(kh, HEAD_DIM // 2, 1) * sin) * (HEAD_DIM ** -0.5)
        k_ref[:, sl] = kr.astype(k_ref.dtype)
    v_ref[...] = z[:, k0 + RET_WIDTH:k0 + 2 * RET_WIDTH].astype(v_ref.dtype)
    g = z[:, k0 + 2 * RET_WIDTH:]
    gs_ref[...] = g * jax.nn.sigmoid(g)


def _inproj(x, g1, w_in, cos, sin, tm, table_blocks, qkv_dtype):
    t = x.shape[0]
    row = lambda i: (i, 0)
    const = lambda i: (0, 0)
    tab = (lambda i: (i % table_blocks, 0)) if table_blocks > 1 else const
    return pl.pallas_call(
        _inproj_body,
        grid=(t // tm,),
        in_specs=[
            pl.BlockSpec((tm, D_MODEL), row),
            pl.BlockSpec((1, D_MODEL), const),
            pl.BlockSpec((D_MODEL, IN_COLS), const),
            pl.BlockSpec((tm, HEAD_DIM), tab),
            pl.BlockSpec((tm, HEAD_DIM), tab),
        ],
        out_specs=[pl.BlockSpec((tm, CONV_CH), row)] + [pl.BlockSpec((tm, RET_WIDTH), row)] * 4,
        out_shape=[
            jax.ShapeDtypeStruct((t, CONV_CH), F32),
            jax.ShapeDtypeStruct((t, RET_WIDTH), qkv_dtype),
            jax.ShapeDtypeStruct((t, RET_WIDTH), qkv_dtype),
            jax.ShapeDtypeStruct((t, RET_WIDTH), qkv_dtype),
            jax.ShapeDtypeStruct((t, RET_WIDTH), F32),
        ],
        compiler_params=_cparams(("parallel",)),
        name="inproj",
    )(x, g1, w_in, cos, sin)


def _ln_silu(acc, g, b):
    mu = jnp.mean(acc, axis=-1, keepdims=True)
    d = acc - mu
    var = jnp.mean(d * d, axis=-1, keepdims=True)
    y = d * lax.rsqrt(var + EPS) * g + b
    return y * jax.nn.sigmoid(y)


def _dwconv(load, w_ref, rows, time_axis):
    acc = None
    for b in range(SUBLANES):
        part = None
        for a in range((CONV_K + HALO_OFF) // SUBLANES + 1):
            k = SUBLANES * a + b - HALO_OFF
            if 0 <= k < CONV_K:
                term = load(SUBLANES * a, rows + SUBLANES) * w_ref[k:k + 1, :]
                part = term if part is None else part + term
        if part is not None:
            shifted = lax.slice_in_dim(part, b, b + rows, axis=time_axis)
            acc = shifted if acc is None else acc + shifted
    return acc


def _conv_prompt_body(u_ref, w_ref, cb_ref, lg_ref, lb_ref, c_ref, st_ref, ext_ref):
    j = pl.program_id(1)
    tl = u_ref.shape[1]

    @pl.when(j == 0)
    def _():
        ext_ref[0:HALO, :] = jnp.zeros((HALO, CONV_CH), F32)
        ext_ref[tl + HALO:, :] = jnp.zeros((SUBLANES, CONV_CH), F32)

    @pl.when(j > 0)
    def _():
        ext_ref[0:HALO, :] = ext_ref[tl:tl + HALO, :]

    ext_ref[HALO:tl + HALO, :] = u_ref[0]
    acc = _dwconv(lambda s, n: ext_ref[s:s + n, :], w_ref, tl, 0) + cb_ref[...]
    c_ref[0] = _ln_silu(acc, lg_ref[...], lb_ref[...]).astype(c_ref.dtype)
    st_ref[0] = ext_ref[tl + HALO_OFF:tl + HALO, :]


def _conv_prompt(u, conv_w, conv_b, ln_g, ln_b, tl):
    n, l, _ = u.shape
    const = lambda b, j: (0, 0)
    return pl.pallas_call(
        _conv_prompt_body,
        grid=(n, l // tl),
        in_specs=[
            pl.BlockSpec((1, tl, CONV_CH), lambda b, j: (b, j, 0)),
            pl.BlockSpec((CONV_K, CONV_CH), const),
            pl.BlockSpec((1, CONV_CH), const),
            pl.BlockSpec((1, CONV_CH), const),
            pl.BlockSpec((1, CONV_CH), const),
        ],
        out_specs=[
            pl.BlockSpec((1, tl, CONV_CH), lambda b, j: (b, j, 0)),
            pl.BlockSpec((1, CONV_K - 1, CONV_CH), lambda b, j: (b, 0, 0)),
        ],
        out_shape=[
            jax.ShapeDtypeStruct((n, l, CONV_CH), BF16),
            jax.ShapeDtypeStruct((n, CONV_K - 1, CONV_CH), F32),
        ],
        scratch_shapes=[pltpu.VMEM((tl + HALO + SUBLANES, CONV_CH), F32)],
        compiler_params=_cparams(("arbitrary", "arbitrary")),
        name="conv_prompt",
    )(u, conv_w, conv_b, ln_g, ln_b)


def _conv_sample_body(u_ref, st_ref, w_ref, cb_ref, lg_ref, lb_ref, c_ref, nst_ref, ext_ref):
    nb, l, _ = u_ref.shape
    ext_ref[:, 0:HALO_OFF, :] = jnp.zeros((nb, HALO_OFF, CONV_CH), F32)
    ext_ref[:, HALO_OFF:HALO, :] = st_ref[...]
    ext_ref[:, HALO:l + HALO, :] = u_ref[...]
    ext_ref[:, l + HALO:, :] = jnp.zeros((nb, SUBLANES, CONV_CH), F32)
    acc = _dwconv(lambda s, n: ext_ref[:, s:s + n, :], w_ref, l, 1) + cb_ref[...]
    c_ref[...] = _ln_silu(acc, lg_ref[...], lb_ref[...]).astype(c_ref.dtype)
    nst_ref[...] = ext_ref[:, l + HALO_OFF:l + HALO, :]


def _conv_sample(u, state, conv_w, conv_b, ln_g, ln_b, nb):
    n, l, _ = u.shape
    const = lambda b: (0, 0)
    blk3 = lambda b: (b, 0, 0)
    return pl.pallas_call(
        _conv_sample_body,
        grid=(n // nb,),
        in_specs=[
            pl.BlockSpec((nb, l, CONV_CH), blk3),
            pl.BlockSpec((nb, CONV_K - 1, CONV_CH), blk3),
            pl.BlockSpec((CONV_K, CONV_CH), const),
            pl.BlockSpec((1, CONV_CH), const),
            pl.BlockSpec((1, CONV_CH), const),
            pl.BlockSpec((1, CONV_CH), const),
        ],
        out_specs=[
            pl.BlockSpec((nb, l, CONV_CH), blk3),
            pl.BlockSpec((nb, CONV_K - 1, CONV_CH), blk3),
        ],
        out_shape=[
            jax.ShapeDtypeStruct((n, l, CONV_CH), BF16),
            jax.ShapeDtypeStruct((n, CONV_K - 1, CONV_CH), F32),
        ],
        scratch_shapes=[pltpu.VMEM((nb, l + HALO + SUBLANES, CONV_CH), F32)],
        compiler_params=_cparams(("parallel",)),
        name="conv_sample",
    )(u, state, conv_w, conv_b, ln_g, ln_b)


def _decay_tables(c):
    lg = jnp.log(1.0 - 2.0 ** (-5.0 - jnp.arange(RET_HEADS, dtype=F32)))
    idx = jnp.arange(c, dtype=F32)
    rel = idx[:, None] - idx[None, :]
    dmat = jnp.where(rel[None] >= 0, jnp.exp(jnp.maximum(rel, 0.0)[None] * lg[:, None, None]), 0.0)
    xi = jnp.exp((idx + 1.0)[None, :] * lg[:, None])
    zeta = jnp.exp((c - 1.0 - idx)[None, :] * lg[:, None])
    gc = jnp.exp(c * lg)
    xi_b = jnp.broadcast_to(xi[:, :, None], (RET_HEADS, c, HEAD_DIM))
    zeta_b = jnp.broadcast_to(zeta[:, :, None], (RET_HEADS, c, HEAD_DIM))
    gc_b = jnp.broadcast_to(gc[:, None, None], (RET_HEADS, 1, HEAD_DIM))
    return dmat, xi_b, zeta_b, gc_b


def _group_norm(o):
    mu = jnp.mean(o, axis=-1, keepdims=True)
    d = o - mu
    var = jnp.mean(d * d, axis=-1, keepdims=True)
    return d * lax.rsqrt(var + EPS)


def _ret_chunk(qh, kh, vh, r, dmat, xi, zeta, gc):
    qb = qh.astype(BF16)
    kb = kh.astype(BF16)
    vb = vh.astype(BF16)
    s = lax.dot_general(qb, kb, (((1,), (1,)), ((), ())), preferred_element_type=F32) * dmat
    o = jnp.dot(s.astype(BF16), vb, preferred_element_type=F32)
    o = o + jnp.dot(qb, r.astype(BF16), preferred_element_type=F32) * xi
    kz = (kh.astype(F32) * zeta).astype(BF16)
    r_new = r * gc + lax.dot_general(kz, vb, (((0,), (0,)), ((), ())), preferred_element_type=F32)
    return o, r_new


def _ret_prompt_body(q_ref, k_ref, v_ref, gs_ref, d_ref, xi_ref, zeta_ref, gc_ref, o_ref, st_ref, r_ref):
    j = pl.program_id(1)

    @pl.when(j == 0)
    def _():
        r_ref[...] = jnp.zeros_like(r_ref)

    n_chunks = q_ref.shape[0] // CHUNK
    for hh in range(RET_HEADS):
        sl = slice(hh * HEAD_DIM, (hh + 1) * HEAD_DIM)
        r = r_ref[hh]
        for ci in range(n_chunks):
            rows = slice(ci * CHUNK, (ci + 1) * CHUNK)
            o, r = _ret_chunk(q_ref[rows, sl], k_ref[rows, sl], v_ref[rows, sl], r,
                              d_ref[hh], xi_ref[hh], zeta_ref[hh], gc_ref[hh])
            o_ref[rows, sl] = (gs_ref[rows, sl] * _group_norm(o)).astype(o_ref.dtype)
        r_ref[hh] = r
    st_ref[0] = r_ref[...]


def _ret_prompt(q, k, v, gs, n, l, tl):
    dmat, xi, zeta, gc = _decay_tables(CHUNK)
    per = l // tl
    row = lambda b, j: (b * per + j, 0)
    c3 = lambda b, j: (0, 0, 0)
    return pl.pallas_call(
        _ret_prompt_body,
        grid=(n, per),
        in_specs=[pl.BlockSpec((tl, RET_WIDTH), row)] * 4 + [
            pl.BlockSpec((RET_HEADS, CHUNK, CHUNK), c3),
            pl.BlockSpec((RET_HEADS, CHUNK, HEAD_DIM), c3),
            pl.BlockSpec((RET_HEADS, CHUNK, HEAD_DIM), c3),
            pl.BlockSpec((RET_HEADS, 1, HEAD_DIM), c3),
        ],
        out_specs=[
            pl.BlockSpec((tl, RET_WIDTH), row),
            pl.BlockSpec((1, RET_HEADS, HEAD_DIM, HEAD_DIM), lambda b, j: (b, 0, 0, 0)),
        ],
        out_shape=[
            jax.ShapeDtypeStruct((n * l, RET_WIDTH), BF16),
            jax.ShapeDtypeStruct((n, RET_HEADS, HEAD_DIM, HEAD_DIM), F32),
        ],
        scratch_shapes=[pltpu.VMEM((RET_HEADS, HEAD_DIM, HEAD_DIM), F32)],
        compiler_params=_cparams(("arbitrary", "arbitrary")),
        name="ret_prompt",
    )(q, k, v, gs, dmat, xi, zeta, gc)


def _ret_sample_body(q_ref, k_ref, v_ref, gs_ref, st_ref, d_ref, xi_ref, zeta_ref, gc_ref, o_ref, nst_ref):
    nb = st_ref.shape[0]
    l = q_ref.shape[0] // nb
    for b in range(nb):
        rows = slice(b * l, (b + 1) * l)
        for hh in range(RET_HEADS):
            sl = slice(hh * HEAD_DIM, (hh + 1) * HEAD_DIM)
            o, r = _ret_chunk(q_ref[rows, sl], k_ref[rows, sl], v_ref[rows, sl], st_ref[b, hh],
                              d_ref[hh], xi_ref[hh], zeta_ref[hh], gc_ref[hh])
            o_ref[rows, sl] = (gs_ref[rows, sl] * _group_norm(o)).astype(o_ref.dtype)
            nst_ref[b, hh] = r


def _ret_sample(q, k, v, gs, state, l, nb):
    n = state.shape[0]
    dmat, xi, zeta, gc = _decay_tables(l)
    row = lambda b: (b, 0)
    c3 = lambda b: (0, 0, 0)
    blk4 = lambda b: (b, 0, 0, 0)
    return pl.pallas_call(
        _ret_sample_body,
        grid=(n // nb,),
        in_specs=[pl.BlockSpec((nb * l, RET_WIDTH), row)] * 4 + [
            pl.BlockSpec((nb, RET_HEADS, HEAD_DIM, HEAD_DIM), blk4),
            pl.BlockSpec((RET_HEADS, l, l), c3),
            pl.BlockSpec((RET_HEADS, l, HEAD_DIM), c3),
            pl.BlockSpec((RET_HEADS, l, HEAD_DIM), c3),
            pl.BlockSpec((RET_HEADS, 1, HEAD_DIM), c3),
        ],
        out_specs=[
            pl.BlockSpec((nb * l, RET_WIDTH), row),
            pl.BlockSpec((nb, RET_HEADS, HEAD_DIM, HEAD_DIM), blk4),
        ],
        out_shape=[
            jax.ShapeDtypeStruct((n * l, RET_WIDTH), BF16),
            jax.ShapeDtypeStruct((n, RET_HEADS, HEAD_DIM, HEAD_DIM), F32),
        ],
        compiler_params=_cparams(("parallel",)),
        name="ret_sample",
    )(q, k, v, gs, state, dmat, xi, zeta, gc)


def _split3(x):
    hi = x.astype(BF16)
    r1 = x - hi.astype(F32)
    mid = r1.astype(BF16)
    lo = (r1 - mid.astype(F32)).astype(BF16)
    return hi, mid, lo


def _dot_hp(t, w2_ref):
    t_hi, t_mid, _ = _split3(t)
    d = functools.partial(jnp.dot, preferred_element_type=F32)
    both = d(t_hi, w2_ref[...])
    return both[:, :LANES] + (d(t_mid, w2_ref[:, 0:LANES]) + both[:, LANES:])


def _route(logits):
    lt = logits.T
    tm = lt.shape[1]
    row = _iota_f32((SUBLANES, tm), 0)
    big = float(SUBLANES)

    def rmax(x):
        return jnp.max(x, axis=0, keepdims=True)

    def first_row(mask):
        return jnp.min(jnp.where(mask, row, big), axis=0, keepdims=True)

    lg = jnp.where(row < float(N_GROUPS), lt[N_EXPERTS:N_EXPERTS + SUBLANES, :], -1e30)
    m = rmax(lg)
    g_top = 1.0 / jnp.sum(jnp.exp(lg - m), axis=0, keepdims=True)
    g_idx = first_row(lg == m)
    lem = lt[0:EXPERTS_PER_GROUP, :]
    for g in range(1, N_GROUPS):
        lem = jnp.where(g_idx == float(g), lt[g * EXPERTS_PER_GROUP:(g + 1) * EXPERTS_PER_GROUP, :], lem)
    pe = jnp.exp(lem - rmax(lem))
    p1 = rmax(pe)
    e1 = first_row(pe == p1)
    rest = row != e1
    pe2 = jnp.where(rest, pe, -1.0)
    p2 = rmax(pe2)
    e2 = first_row(rest & (pe2 == p2))
    scale = g_top / (p1 + p2)
    base = g_idx * float(EXPERTS_PER_GROUP)
    rec_t = jnp.where(row == 0.0, base + e1, jnp.where(row == 1.0, base + e2, 0.0))
    rec_t = rec_t + jnp.where(row == 2.0, p1 * scale, jnp.where(row == 3.0, p2 * scale, 0.0))
    return jnp.concatenate([rec_t, jnp.zeros((LANES - SUBLANES, tm), F32)], axis=0).T


def _outproj_body(c_ref, o_ref, x_ref, wo_ref, g2_ref, wr_ref, br_ref, x1_ref, t_ref, rec_ref):
    x1 = x_ref[...] + jnp.dot(c_ref[...], wo_ref[0:CONV_CH, :], preferred_element_type=F32)
    x1 = x1 + jnp.dot(o_ref[...], wo_ref[CONV_CH:, :], preferred_element_type=F32)
    x1_ref[...] = x1
    t = _rms(x1, g2_ref[...])
    t_ref[...] = t.astype(t_ref.dtype)
    rec_ref[...] = _route(_dot_hp(t, wr_ref) + br_ref[...])


def _outproj(c, o, x, w_out, g2, wr2, br, tm):
    t = x.shape[0]
    row = lambda i: (i, 0)
    const = lambda i: (0, 0)
    c3 = lambda i: (0, 0, 0)
    return pl.pallas_call(
        _outproj_body,
        grid=(t // tm,),
        in_specs=[
            pl.BlockSpec((tm, CONV_CH), row),
            pl.BlockSpec((tm, RET_WIDTH), row),
            pl.BlockSpec((tm, D_MODEL), row),
            pl.BlockSpec((D_MODEL, D_MODEL), const),
            pl.BlockSpec((1, D_MODEL), const),
            pl.BlockSpec((D_MODEL, 2 * LANES), const),
            pl.BlockSpec((1, LANES), const),
        ],
        out_specs=[
            pl.BlockSpec((tm, D_MODEL), row),
            pl.BlockSpec((tm, D_MODEL), row),
            pl.BlockSpec((tm, LANES), row),
        ],
        out_shape=[
            jax.ShapeDtypeStruct((t, D_MODEL), F32),
            jax.ShapeDtypeStruct((t, D_MODEL), BF16),
            jax.ShapeDtypeStruct((t, LANES), F32),
        ],
        compiler_params=_cparams(("parallel",)),
        name="outproj_router",
    )(c, o, x, w_out, g2, wr2, br)


def _mix_body(x_ref, g1_ref, w_ref, cos_ref, sin_ref, cw_ref, cb_ref, lg_ref, lb_ref,
              d_ref, xi_ref, zeta_ref, gc_ref, wo_ref, g2_ref, wr_ref, br_ref,
              x1_ref, t_ref, rec_ref, cst_ref, rst_ref, ext_ref, r_ref, o_ref):
    j = pl.program_id(1)
    tl = x_ref.shape[0]
    x = x_ref[...]
    z = jnp.dot(_rms(x, g1_ref[...]).astype(BF16), w_ref[...], preferred_element_type=F32)

    @pl.when(j == 0)
    def _():
        ext_ref[0:HALO, :] = jnp.zeros((HALO, CONV_CH), F32)
        ext_ref[tl + HALO:, :] = jnp.zeros((SUBLANES, CONV_CH), F32)
        r_ref[...] = jnp.zeros_like(r_ref)

    @pl.when(j > 0)
    def _():
        ext_ref[0:HALO, :] = ext_ref[tl:tl + HALO, :]

    ext_ref[HALO:tl + HALO, :] = z[:, :CONV_CH] * jax.nn.sigmoid(z[:, CONV_CH:2 * CONV_CH])
    acc = _dwconv(lambda s, n: ext_ref[s:s + n, :], cw_ref, tl, 0) + cb_ref[...]
    c = _ln_silu(acc, lg_ref[...], lb_ref[...]).astype(BF16)
    cst_ref[0] = ext_ref[tl + HALO_OFF:tl + HALO, :]

    cos = cos_ref[...]
    sin = sin_ref[...]
    q0 = 2 * CONV_CH
    k0 = q0 + RET_WIDTH
    v0 = k0 + RET_WIDTH
    g0 = v0 + RET_WIDTH
    for hh in range(RET_HEADS):
        lo = hh * HEAD_DIM
        qh = z[:, q0 + lo:q0 + lo + HEAD_DIM]
        kh = z[:, k0 + lo:k0 + lo + HEAD_DIM]
        qr = (qh * cos + pltpu.roll(qh, HEAD_DIM // 2, 1) * sin).astype(BF16)
        kr = ((kh * cos + pltpu.roll(kh, HEAD_DIM // 2, 1) * sin) * (HEAD_DIM ** -0.5)).astype(BF16)
        vh = z[:, v0 + lo:v0 + lo + HEAD_DIM].astype(BF16)
        g = z[:, g0 + lo:g0 + lo + HEAD_DIM]
        gs = g * jax.nn.sigmoid(g)
        r = r_ref[hh]
        for ci in range(tl // CHUNK):
            rows = slice(ci * CHUNK, (ci + 1) * CHUNK)
            o, r = _ret_chunk(qr[rows], kr[rows], vh[rows], r, d_ref[hh], xi_ref[hh], zeta_ref[hh], gc_ref[hh])
            o_ref[rows, lo:lo + HEAD_DIM] = (gs[rows] * _group_norm(o)).astype(BF16)
        r_ref[hh] = r
    rst_ref[0] = r_ref[...]

    x1 = x + jnp.dot(c, wo_ref[0:CONV_CH, :], preferred_element_type=F32)
    x1 = x1 + jnp.dot(o_ref[...], wo_ref[CONV_CH:, :], preferred_element_type=F32)
    x1_ref[...] = x1
    t = _rms(x1, g2_ref[...])
    t_ref[...] = t.astype(t_ref.dtype)
    rec_ref[...] = _route(_dot_hp(t, wr_ref) + br_ref[...])


def _mix(x, g1, w_in, cos, sin, conv_w, conv_b, ln_g, ln_b, w_out, g2, wr2, br, n, l, tl):
    dmat, xi, zeta, gc = _decay_tables(CHUNK)
    per = l // tl
    row = lambda b, j: (b * per + j, 0)
    tab = lambda b, j: (j, 0)
    const = lambda b, j: (0, 0)
    c3 = lambda b, j: (0, 0, 0)
    once = dict(pipeline_mode=pl.Buffered(1))
    return pl.pallas_call(
        _mix_body,
        grid=(n, per),
        in_specs=[
            pl.BlockSpec((tl, D_MODEL), row),
            pl.BlockSpec((1, D_MODEL), const),
            pl.BlockSpec((D_MODEL, IN_COLS), const, **once),
            pl.BlockSpec((tl, HEAD_DIM), tab),
            pl.BlockSpec((tl, HEAD_DIM), tab),
            pl.BlockSpec((CONV_K, CONV_CH), const),
            pl.BlockSpec((1, CONV_CH), const),
            pl.BlockSpec((1, CONV_CH), const),
            pl.BlockSpec((1, CONV_CH), const),
            pl.BlockSpec((RET_HEADS, CHUNK, CHUNK), c3),
            pl.BlockSpec((RET_HEADS, CHUNK, HEAD_DIM), c3),
            pl.BlockSpec((RET_HEADS, CHUNK, HEAD_DIM), c3),
            pl.BlockSpec((RET_HEADS, 1, HEAD_DIM), c3),
            pl.BlockSpec((D_MODEL, D_MODEL), const, **once),
            pl.BlockSpec((1, D_MODEL), const),
            pl.BlockSpec((D_MODEL, 2 * LANES), const, **once),
            pl.BlockSpec((1, LANES), const),
        ],
        out_specs=[
            pl.BlockSpec((tl, D_MODEL), row),
            pl.BlockSpec((tl, D_MODEL), row),
            pl.BlockSpec((tl, LANES), row),
            pl.BlockSpec((1, CONV_K - 1, CONV_CH), lambda b, j: (b, 0, 0)),
            pl.BlockSpec((1, RET_HEADS, HEAD_DIM, HEAD_DIM), lambda b, j: (b, 0, 0, 0)),
        ],
        out_shape=[
            jax.ShapeDtypeStruct((n * l, D_MODEL), F32),
            jax.ShapeDtypeStruct((n * l, D_MODEL), BF16),
            jax.ShapeDtypeStruct((n * l, LANES), F32),
            jax.ShapeDtypeStruct((n, CONV_K - 1, CONV_CH), F32),
            jax.ShapeDtypeStruct((n, RET_HEADS, HEAD_DIM, HEAD_DIM), F32),
        ],
        scratch_shapes=[
            pltpu.VMEM((tl + HALO + SUBLANES, CONV_CH), F32),
            pltpu.VMEM((RET_HEADS, HEAD_DIM, HEAD_DIM), F32),
            pltpu.VMEM((tl, RET_WIDTH), BF16),
        ],
        compiler_params=pltpu.CompilerParams(dimension_semantics=("arbitrary", "arbitrary"),
                                             vmem_limit_bytes=MIX_VMEM_LIMIT),
        name="token_mix",
    )(x, g1, w_in, cos, sin, conv_w, conv_b, ln_g, ln_b, dmat, xi, zeta, gc, w_out, g2, wr2, br)


def _iota_f32(shape, dim):
    return lax.broadcasted_iota(jnp.int32, shape, dim).astype(F32)


def _dispatch_body(rec_a_ref, t_a_ref, rec_b_ref, t_b_ref, s_ref, pos_ref, meta_ref, *, nsub_a):
    from_a = pl.program_id(0) * DISPATCH_SUBS < nsub_a
    for s in range(DISPATCH_SUBS):
        rows = slice(s * SUB, (s + 1) * SUB)
        rec = jnp.where(from_a, rec_a_ref[rows, :], rec_b_ref[rows, :])
        tok = jnp.where(from_a, t_a_ref[rows, :], t_b_ref[rows, :])
        chunks = slice(s * CHUNKS_PER_SUB, (s + 1) * CHUNKS_PER_SUB)
        _dispatch_sub_tile(rec, tok, s_ref.at[chunks], pos_ref.at[rows], meta_ref.at[s])


def _dispatch_sub_tile(rec, tok, s_ref, pos_ref, meta_ref):
    lane = _iota_f32(rec.shape, 1)
    a1 = lane == rec[:, 0:1]
    a2 = lane == rec[:, 1:2]
    a1f = jnp.where(a1, 1.0, 0.0)
    a2f = jnp.where(a2, 1.0, 0.0)
    ltri = jnp.where(_iota_f32((SUB, SUB), 1) < _iota_f32((SUB, SUB), 0), 1.0, 0.0).astype(BF16)
    c1 = jnp.dot(ltri, a1f.astype(BF16), preferred_element_type=F32)
    c2 = jnp.dot(ltri, a2f.astype(BF16), preferred_element_type=F32)
    n1 = jnp.sum(a1f, axis=0, keepdims=True)
    n2 = jnp.sum(a2f, axis=0, keepdims=True)
    cnt = jnp.floor((n1 + n2 + (ROW_ALIGN - 1.0)) * (1.0 / ROW_ALIGN))
    utri = jnp.where(_iota_f32((LANES, LANES), 0) < _iota_f32((LANES, LANES), 1), 1.0, 0.0).astype(BF16)
    start = jnp.dot(jnp.broadcast_to(cnt, (SUBLANES, LANES)).astype(BF16), utri,
                    preferred_element_type=F32)[0:1]
    base1 = start * ROW_ALIGN
    base2 = base1 + n1
    pos1 = jnp.sum(jnp.where(a1, c1 + base1, 0.0), axis=1, keepdims=True)
    pos2 = jnp.sum(jnp.where(a2, c2 + base2, 0.0), axis=1, keepdims=True)
    posm = jnp.where(lane == 0.0, pos1, jnp.where(lane == 1.0, pos2, 0.0))
    pos_ref[...] = posm
    row = _iota_f32((SUBLANES, LANES), 0)
    meta_ref[...] = jnp.where(row == 0.0, start, jnp.where(row == 1.0, cnt, 0.0))

    g1 = _split3(rec[:, 2:3])
    g2 = _split3(rec[:, 3:4])
    info = jnp.where(lane == 6.0, rec[:, 0:1], jnp.where(lane == 7.0, rec[:, 1:2], 0.0))
    for i in range(3):
        info = jnp.where(lane == float(i), g1[i].astype(F32), info)
        info = jnp.where(lane == float(3 + i), g2[i].astype(F32), info)
    src = jnp.concatenate([tok, info.astype(BF16)], axis=1)

    post = posm.T
    r = _iota_f32((SUBP, SUB), 0)
    onehot = jnp.where(r == post[0:1, :], 1.0, jnp.where(r == post[1:2, :], 1.0, 0.0)).astype(BF16)
    sorted_rows = jnp.dot(onehot, src, preferred_element_type=F32).astype(BF16)
    s_ref[...] = sorted_rows.reshape(CHUNKS_PER_SUB, ROW_ALIGN, ROW_W)


def _dispatch(rec_a, t_a, rec_b, t_b):
    nsub_a = rec_a.shape[0] // SUB
    nsub_b = rec_b.shape[0] // SUB
    nsub = nsub_a + nsub_b
    assert nsub_a % DISPATCH_SUBS == 0 and nsub_b % DISPATCH_SUBS == 0
    steps_a = nsub_a // DISPATCH_SUBS
    tm = DISPATCH_SUBS * SUB
    row = lambda i: (i, 0)
    row_a = lambda i: (jnp.minimum(i, steps_a - 1), 0)
    row_b = lambda i: (jnp.maximum(i - steps_a, 0), 0)
    return pl.pallas_call(
        functools.partial(_dispatch_body, nsub_a=nsub_a),
        grid=(nsub // DISPATCH_SUBS,),
        in_specs=[
            pl.BlockSpec((tm, LANES), row_a),
            pl.BlockSpec((tm, D_MODEL), row_a),
            pl.BlockSpec((tm, LANES), row_b),
            pl.BlockSpec((tm, D_MODEL), row_b),
        ],
        out_specs=[
            pl.BlockSpec((DISPATCH_SUBS * CHUNKS_PER_SUB, ROW_ALIGN, ROW_W), lambda i: (i, 0, 0)),
            pl.BlockSpec((tm, LANES), row),
            pl.BlockSpec((DISPATCH_SUBS, SUBLANES, LANES), lambda i: (i, 0, 0)),
        ],
        out_shape=[
            jax.ShapeDtypeStruct((nsub * CHUNKS_PER_SUB, ROW_ALIGN, ROW_W), BF16),
            jax.ShapeDtypeStruct((nsub * SUB, LANES), F32),
            jax.ShapeDtypeStruct((nsub, SUBLANES, LANES), F32),
        ],
        compiler_params=_cparams(("parallel",)),
        name="moe_dispatch",
    )(rec_a, t_a, rec_b, t_b)


def _experts_body(start_ref, cnt_ref, s_in, wg_ref, wu_ref, wd_ref, s_hbm,
                  xbuf, ybuf, wgu_ref, wdb_ref, gsem, ssem, list_ref, state_ref, *, nsub):
    del s_in
    e = pl.program_id(0)
    ne = pl.num_programs(0)
    par = e & 1

    list_max = list_ref.shape[0] // 2

    def gather_copy(src, i, slot):
        return pltpu.make_async_copy(s_hbm.at[src], xbuf.at[slot, i], gsem.at[slot])

    def scatter_copy(dst, i, slot):
        return pltpu.make_async_copy(ybuf.at[slot, i], s_hbm.at[dst], ssem.at[slot])

    def build_list(x, which):
        def per_sub(s, k):
            run = s * N_EXPERTS + x
            c = cnt_ref[run]
            base = s * CHUNKS_PER_SUB + start_ref[run]
            list_ref[k] = base
            list_ref[k + 1] = base + 1

            def per_chunk(i, carry):
                list_ref[k + i] = base + i
                return carry
            lax.fori_loop(2, c, per_chunk, 0)
            return k + c
        first = which * list_max
        state_ref[which] = lax.fori_loop(0, nsub, per_sub, first) - first

    def start_all(copy, which, first, n, slot, counter):
        def body(i, carry):
            copy(list_ref[which * list_max + first + i], i, slot).start()
            return carry
        lax.fori_loop(0, n, body, 0)
        state_ref[counter] = n

    def wait_all(copy, block_copy, slot, counter):
        n = state_ref[counter]

        @pl.when(n == CHUNKS_PER_BLK)
        def _():
            block_copy(slot).wait()

        @pl.when(n < CHUNKS_PER_BLK)
        def _():
            def body(i, carry):
                copy(0, 0, slot).wait()
                return carry
            lax.fori_loop(0, n, body, 0)
        state_ref[counter] = 0

    def gather_block(slot):
        return pltpu.make_async_copy(s_hbm.at[pl.ds(0, CHUNKS_PER_BLK)], xbuf.at[slot], gsem.at[slot])

    def scatter_block(slot):
        return pltpu.make_async_copy(ybuf.at[slot], s_hbm.at[pl.ds(0, CHUNKS_PER_BLK)], ssem.at[slot])

    def block_chunks(total, b):
        return jnp.minimum(total - b * CHUNKS_PER_BLK, CHUNKS_PER_BLK)

    @pl.when(e == 0)
    def _():
        for i in range(6):
            state_ref[i] = 0
        xbuf[...] = jnp.zeros_like(xbuf)
        build_list(0, 0)
        n0 = state_ref[0]

        @pl.when(n0 > 0)
        def _():
            start_all(gather_copy, 0, 0, block_chunks(n0, 0), 0, 2)

    @pl.when(e + 1 < ne)
    def _():
        build_list(e + 1, 1 - par)

    total = state_ref[par]
    nblk = (total + CHUNKS_PER_BLK - 1) // CHUNKS_PER_BLK
    wgu_ref[:, 0:EXPERT_FF] = wg_ref[0].astype(BF16)
    wgu_ref[:, EXPERT_FF:] = wu_ref[0].astype(BF16)
    wdb_ref[...] = wd_ref[0].astype(BF16)
    ef = e.astype(F32)

    def block(b, carry):
        slot = b & 1
        first = b * CHUNKS_PER_BLK

        @pl.when(b + 1 < nblk)
        def _():
            start_all(gather_copy, par, first + CHUNKS_PER_BLK, block_chunks(total, b + 1), 1 - slot, 3 - slot)

        wait_all(gather_copy, gather_block, slot, 2 + slot)
        wait_all(scatter_copy, scatter_block, slot, 4 + slot)
        x = xbuf[slot].reshape(MBLK, ROW_W)
        info = x[:, D_MODEL:].astype(F32)
        g_first = info[:, 0:1] + info[:, 1:2] + info[:, 2:3]
        g_second = info[:, 3:4] + info[:, 4:5] + info[:, 5:6]
        gate = jnp.where(info[:, 6:7] == ef, g_first, g_second)
        h = jnp.dot(x[:, :D_MODEL], wgu_ref[...], preferred_element_type=F32)
        h1 = h[:, :EXPERT_FF]
        hid = (h1 * jax.nn.sigmoid(h1)) * h[:, EXPERT_FF:] * gate
        y = jnp.dot(hid.astype(BF16), wdb_ref[...], preferred_element_type=F32).astype(BF16)
        ybuf[slot] = jnp.concatenate([y, x[:, D_MODEL:]], axis=1).reshape(CHUNKS_PER_BLK, ROW_ALIGN, ROW_W)
        start_all(scatter_copy, par, first, block_chunks(total, b), slot, 4 + slot)
        return carry

    lax.fori_loop(0, nblk, block, 0)

    @pl.when(e + 1 < ne)
    def _():
        n1 = state_ref[1 - par]

        @pl.when(n1 > 0)
        def _():
            start_all(gather_copy, 1 - par, 0, block_chunks(n1, 0), 0, 2)

    @pl.when(e == ne - 1)
    def _():
        wait_all(scatter_copy, scatter_block, 0, 4)
        wait_all(scatter_copy, scatter_block, 1, 5)


def _experts(start, cnt, staged, wg, wu, wd):
    nsub = staged.shape[0] // CHUNKS_PER_SUB
    list_max = nsub * SUB // ROW_ALIGN + nsub + LIST_SLACK
    wblk = lambda e, *_: (e, 0, 0)
    grid_spec = pltpu.PrefetchScalarGridSpec(
        num_scalar_prefetch=2,
        grid=(N_EXPERTS,),
        in_specs=[
            pl.BlockSpec(memory_space=pl.ANY),
            pl.BlockSpec((1, D_MODEL, EXPERT_FF), wblk),
            pl.BlockSpec((1, D_MODEL, EXPERT_FF), wblk),
            pl.BlockSpec((1, EXPERT_FF, D_MODEL), wblk),
        ],
        out_specs=pl.BlockSpec(memory_space=pl.ANY),
        scratch_shapes=[
            pltpu.VMEM((2, CHUNKS_PER_BLK, ROW_ALIGN, ROW_W), BF16),
            pltpu.VMEM((2, CHUNKS_PER_BLK, ROW_ALIGN, ROW_W), BF16),
            pltpu.VMEM((D_MODEL, 2 * EXPERT_FF), BF16),
            pltpu.VMEM((EXPERT_FF, D_MODEL), BF16),
            pltpu.SemaphoreType.DMA((2,)),
            pltpu.SemaphoreType.DMA((2,)),
            pltpu.SMEM((2 * list_max,), jnp.int32),
            pltpu.SMEM((6,), jnp.int32),
        ],
    )
    return pl.pallas_call(
        functools.partial(_experts_body, nsub=nsub),
        grid_spec=grid_spec,
        out_shape=jax.ShapeDtypeStruct(staged.shape, staged.dtype),
        input_output_aliases={2: 0},
        compiler_params=_cparams(("arbitrary",)),
        name="moe_experts",
    )(start, cnt, staged, wg, wu, wd)


def _combine_body(ys_ref, pos_ref, x1_ref, p_ref, wp_ref, gp_ref, wpg_ref, gf_ref, y_ref):
    r = _iota_f32((SUB, SUBP), 1)
    moe = []
    for s in range(x1_ref.shape[0] // SUB):
        p1 = pos_ref[s * SUB:(s + 1) * SUB, 0:1]
        p2 = pos_ref[s * SUB:(s + 1) * SUB, 1:2]
        onehot = jnp.where(r == p1, 1.0, jnp.where(r == p2, 1.0, 0.0)).astype(BF16)
        ys = ys_ref[s * CHUNKS_PER_SUB:(s + 1) * CHUNKS_PER_SUB].reshape(SUBP, D_MODEL)
        moe.append(jnp.dot(onehot, ys, preferred_element_type=F32))
    x2 = x1_ref[...] + jnp.concatenate(moe, axis=0)
    ple = _rms(jnp.dot(p_ref[...].astype(BF16), wp_ref[...], preferred_element_type=F32), gp_ref[...])
    gate = jax.nn.sigmoid(jnp.dot(x2.astype(BF16), wpg_ref[...], preferred_element_type=F32))
    y_ref[...] = _rms(x2 + ple * gate, gf_ref[...])


def _combine(ys, pos, x1, p, w_ple, gp, w_ple_gate, gf, sub_off):
    t = x1.shape[0]
    tm = COMBINE_SUBS * SUB
    blk_off = sub_off // COMBINE_SUBS
    assert sub_off % COMBINE_SUBS == 0 and t % tm == 0
    row = lambda i: (i, 0)
    const = lambda i: (0, 0)
    return pl.pallas_call(
        _combine_body,
        grid=(t // tm,),
        in_specs=[
            pl.BlockSpec((COMBINE_SUBS * CHUNKS_PER_SUB, ROW_ALIGN, D_MODEL), lambda i: (i + blk_off, 0, 0)),
            pl.BlockSpec((tm, LANES), lambda i: (i + blk_off, 0)),
            pl.BlockSpec((tm, D_MODEL), row),
            pl.BlockSpec((tm, PLE_DIM), row),
            pl.BlockSpec((PLE_DIM, D_MODEL), const),
            pl.BlockSpec((1, D_MODEL), const),
            pl.BlockSpec((D_MODEL, D_MODEL), const),
            pl.BlockSpec((1, D_MODEL), const),
        ],
        out_specs=pl.BlockSpec((tm, D_MODEL), row),
        out_shape=jax.ShapeDtypeStruct((t, D_MODEL), F32),
        compiler_params=_cparams(("parallel",)),
        name="moe_combine_ple",
    )(ys, pos, x1, p, w_ple, gp, w_ple_gate, gf)


def _rope_tables(pos):
    half = HEAD_DIM // 2
    inv = ROPE_BASE ** (-jnp.arange(half, dtype=F32) / half)
    ang = pos[:, None] * inv[None, :]
    cos = jnp.cos(ang)
    sin = jnp.sin(ang)
    return jnp.concatenate([cos, cos], axis=-1), jnp.concatenate([-sin, sin], axis=-1)


def _router_params(we, be, wg, bg):
    pad = LANES - N_EXPERTS - N_GROUPS
    w = jnp.pad(jnp.concatenate([we, wg], axis=1), ((0, 0), (0, pad)))
    b = jnp.pad(jnp.concatenate([be, bg]), (0, pad))[None, :]
    return jnp.concatenate(_split3(w)[:2], axis=1), b


def kernel(x_prompt, x_sample, p_prompt, p_sample, state_conv, state_ret, w_in, conv_w, conv_b, conv_ln_g, conv_ln_b, w_out, norm1_g, norm2_g, router_group_w, router_group_b, router_expert_w, router_expert_b, w_expert_gate, w_expert_up, w_expert_down, w_ple, ple_norm_g, w_ple_gate, final_norm_g):
    assert w_in.shape[0] == 1, "single-layer trunk"
    nb, seq, _ = x_prompt.shape
    ns, dseq, _ = x_sample.shape
    tm = 512

    w_in_b = w_in[0].astype(BF16)
    w_out_b = w_out[0].astype(BF16)
    w_ple_b = w_ple[0].astype(BF16)
    w_pg_b = w_ple_gate[0].astype(BF16)
    g1 = norm1_g[0][None, :]
    g2 = norm2_g[0][None, :]
    gp = ple_norm_g[0][None, :]
    gf = final_norm_g[None, :]
    cb = conv_b[0][None, :]
    lng = conv_ln_g[0][None, :]
    lnb = conv_ln_b[0][None, :]
    wr2, br = _router_params(router_expert_w[0], router_expert_b[0], router_group_w[0], router_group_b[0])

    cos_p, sin_p = _rope_tables(jnp.arange(seq, dtype=F32) + jnp.float32(0))
    pos_s = jnp.tile(jnp.arange(dseq, dtype=F32) + jnp.float32(PAST_LEN), tm // dseq)
    cos_s, sin_s = _rope_tables(pos_s)

    xp = x_prompt.reshape(nb * seq, D_MODEL)
    x1_p, t_p, rec_p, conv_p, ret_p = _mix(xp, g1, w_in_b, cos_p, sin_p, conv_w[0], cb, lng, lnb, w_out_b, g2,
                                           wr2, br, nb, seq, MIX_TILE)

    xs = x_sample.reshape(ns * dseq, D_MODEL)
    u, q, k, v, gs = _inproj(xs, g1, w_in_b, cos_s, sin_s, tm, 1, F32)
    c, conv_s = _conv_sample(u.reshape(ns, dseq, CONV_CH), state_conv[0], conv_w[0], cb, lng, lnb, 16)
    o, ret_s = _ret_sample(q, k, v, gs, state_ret[0], dseq, 8)
    x1_s, t_s, rec_s = _outproj(c.reshape(ns * dseq, CONV_CH), o, xs, w_out_b, g2, wr2, br, tm)

    staged, pos, meta = _dispatch(rec_p, t_p, rec_s, t_s)
    start = meta[:, 0, :N_EXPERTS].astype(jnp.int32).reshape(-1)
    cnt = meta[:, 1, :N_EXPERTS].astype(jnp.int32).reshape(-1)
    ys = _experts(start, cnt, staged, w_expert_gate[0], w_expert_up[0], w_expert_down[0])

    y_p = _combine(ys, pos, x1_p, p_prompt[0].reshape(nb * seq, PLE_DIM), w_ple_b, gp, w_pg_b, gf, 0)
    y_s = _combine(ys, pos, x1_s, p_sample[0].reshape(ns * dseq, PLE_DIM), w_ple_b, gp, w_pg_b, gf,
                   nb * seq // SUB)

    return (y_p.reshape(nb, seq, D_MODEL), y_s.reshape(ns, dseq, D_MODEL),
            conv_p[None], ret_p[None], conv_s[None], ret_s[None])
```

```python
import functools

import jax
import jax.numpy as jnp
from jax import lax
from jax.experimental import pallas as pl
from jax.experimental.pallas import tpu as pltpu

F32 = jnp.float32
BF16 = jnp.bfloat16

D_MODEL = 1024
PLE_DIM = 256
CONV_CH = 512
CONV_K = 31
RET_WIDTH = 512
RET_HEADS = 4
HEAD_DIM = 128
CHUNK = 128
ROPE_BASE = 10000.0
N_GROUPS = 4
EXPERTS_PER_GROUP = 8
N_EXPERTS = 32
EXPERT_FF = 256
IN_COLS = 3072
EPS = 1e-6
PAST_LEN = 16384

LANES = 128
SUBLANES = 8
HALO = 32
HALO_OFF = HALO - (CONV_K - 1)
VMEM_LIMIT = 48 * 1024 * 1024
MIX_VMEM_LIMIT = 56 * 1024 * 1024
MIX_TILE = 512

SUB = 256
ROW_ALIGN = 16
PBLK = 256
SUBP = -(-(2 * SUB + N_EXPERTS * (ROW_ALIGN - 1)) // PBLK) * PBLK
CHUNKS_PER_SUB = SUBP // ROW_ALIGN
ROW_W = D_MODEL + LANES
MBLK = 512
CHUNKS_PER_BLK = MBLK // ROW_ALIGN
LIST_SLACK = 2
COMBINE_SUBS = 4
DISPATCH_SUBS = 4


def _cparams(sem):
    return pltpu.CompilerParams(dimension_semantics=sem, vmem_limit_bytes=VMEM_LIMIT)


def _rms(x, g):
    return x * lax.rsqrt(jnp.mean(x * x, axis=-1, keepdims=True) + EPS) * g


def _inproj_body(x_ref, g1_ref, w_ref, cos_ref, sin_ref, u_ref, q_ref, k_ref, v_ref, gs_ref):
    h = _rms(x_ref[...], g1_ref[...]).astype(BF16)
    z = jnp.dot(h, w_ref[...], preferred_element_type=F32)
    a = z[:, :CONV_CH]
    b = z[:, CONV_CH:2 * CONV_CH]
    u_ref[...] = a * jax.nn.sigmoid(b)
    cos = cos_ref[...]
    sin = sin_ref[...]
    q0 = 2 * CONV_CH
    k0 = q0 + RET_WIDTH
    for hh in range(RET_HEADS):
        sl = slice(hh * HEAD_DIM, (hh + 1) * HEAD_DIM)
        qh = z[:, q0 + hh * HEAD_DIM:q0 + (hh + 1) * HEAD_DIM]
        kh = z[:, k0 + hh * HEAD_DIM:k0 + (hh + 1) * HEAD_DIM]
        q_ref[:, sl] = (qh * cos + pltpu.roll(qh, HEAD_DIM // 2, 1) * sin).astype(q_ref.dtype)
        kr = (kh * cos + pltpu.roll(kh, HEAD_DIM // 2, 1) * sin) * (HEAD_DIM ** -0.5)
        k_ref[:, sl] = kr.astype(k_ref.dtype)
    v_ref[...] = z[:, k0 + RET_WIDTH:k0 + 2 * RET_WIDTH].astype(v_ref.dtype)
    g = z[:, k0 + 2 * RET_WIDTH:]
    gs_ref[...] = g * jax.nn.sigmoid(g)


def _inproj(x, g1, w_in, cos, sin, tm, table_blocks, qkv_dtype):
    t = x.shape[0]
    row = lambda i: (i, 0)
    const = lambda i: (0, 0)
    tab = (lambda i: (i % table_blocks, 0)) if table_blocks > 1 else const
    return pl.pallas_call(
        _inproj_body,
        grid=(t // tm,),
        in_specs=[
            pl.BlockSpec((tm, D_MODEL), row),
            pl.BlockSpec((1, D_MODEL), const),
            pl.BlockSpec((D_MODEL, IN_COLS), const),
            pl.BlockSpec((tm, HEAD_DIM), tab),
            pl.BlockSpec((tm, HEAD_DIM), tab),
        ],
        out_specs=[pl.BlockSpec((tm, CONV_CH), row)] + [pl.BlockSpec((tm, RET_WIDTH), row)] * 4,
        out_shape=[
            jax.ShapeDtypeStruct((t, CONV_CH), F32),
            jax.ShapeDtypeStruct((t, RET_WIDTH), qkv_dtype),
            jax.ShapeDtypeStruct((t, RET_WIDTH), qkv_dtype),
            jax.ShapeDtypeStruct((t, RET_WIDTH), qkv_dtype),
            jax.ShapeDtypeStruct((t, RET_WIDTH), F32),
        ],
        compiler_params=_cparams(("parallel",)),
        name="inproj",
    )(x, g1, w_in, cos, sin)


def _ln_silu(acc, g, b):
    mu = jnp.mean(acc, axis=-1, keepdims=True)
    d = acc - mu
    var = jnp.mean(d * d, axis=-1, keepdims=True)
    y = d * lax.rsqrt(var + EPS) * g + b
    return y * jax.nn.sigmoid(y)


def _dwconv(load, w_ref, rows, time_axis):
    acc = None
    for b in range(SUBLANES):
        part = None
        for a in range((CONV_K + HALO_OFF) // SUBLANES + 1):
            k = SUBLANES * a + b - HALO_OFF
            if 0 <= k < CONV_K:
                term = load(SUBLANES * a, rows + SUBLANES) * w_ref[k:k + 1, :]
                part = term if part is None else part + term
        if part is not None:
            shifted = lax.slice_in_dim(part, b, b + rows, axis=time_axis)
            acc = shifted if acc is None else acc + shifted
    return acc


def _conv_prompt_body(u_ref, w_ref, cb_ref, lg_ref, lb_ref, c_ref, st_ref, ext_ref):
    j = pl.program_id(1)
    tl = u_ref.shape[1]

    @pl.when(j == 0)
    def _():
        ext_ref[0:HALO, :] = jnp.zeros((HALO, CONV_CH), F32)
        ext_ref[tl + HALO:, :] = jnp.zeros((SUBLANES, CONV_CH), F32)

    @pl.when(j > 0)
    def _():
        ext_ref[0:HALO, :] = ext_ref[tl:tl + HALO, :]

    ext_ref[HALO:tl + HALO, :] = u_ref[0]
    acc = _dwconv(lambda s, n: ext_ref[s:s + n, :], w_ref, tl, 0) + cb_ref[...]
    c_ref[0] = _ln_silu(acc, lg_ref[...], lb_ref[...]).astype(c_ref.dtype)
    st_ref[0] = ext_ref[tl + HALO_OFF:tl + HALO, :]


def _conv_prompt(u, conv_w, conv_b, ln_g, ln_b, tl):
    n, l, _ = u.shape
    const = lambda b, j: (0, 0)
    return pl.pallas_call(
        _conv_prompt_body,
        grid=(n, l // tl),
        in_specs=[
            pl.BlockSpec((1, tl, CONV_CH), lambda b, j: (b, j, 0)),
            pl.BlockSpec((CONV_K, CONV_CH), const),
            pl.BlockSpec((1, CONV_CH), const),
            pl.BlockSpec((1, CONV_CH), const),
            pl.BlockSpec((1, CONV_CH), const),
        ],
        out_specs=[
            pl.BlockSpec((1, tl, CONV_CH), lambda b, j: (b, j, 0)),
            pl.BlockSpec((1, CONV_K - 1, CONV_CH), lambda b, j: (b, 0, 0)),
        ],
        out_shape=[
            jax.ShapeDtypeStruct((n, l, CONV_CH), BF16),
            jax.ShapeDtypeStruct((n, CONV_K - 1, CONV_CH), F32),
        ],
        scratch_shapes=[pltpu.VMEM((tl + HALO + SUBLANES, CONV_CH), F32)],
        compiler_params=_cparams(("arbitrary", "arbitrary")),
        name="conv_prompt",
    )(u, conv_w, conv_b, ln_g, ln_b)


def _conv_sample_body(u_ref, st_ref, w_ref, cb_ref, lg_ref, lb_ref, c_ref, nst_ref, ext_ref):
    nb, l, _ = u_ref.shape
    ext_ref[:, 0:HALO_OFF, :] = jnp.zeros((nb, HALO_OFF, CONV_CH), F32)
    ext_ref[:, HALO_OFF:HALO, :] = st_ref[...]
    ext_ref[:, HALO:l + HALO, :] = u_ref[...]
    ext_ref[:, l + HALO:, :] = jnp.zeros((nb, SUBLANES, CONV_CH), F32)
    acc = _dwconv(lambda s, n: ext_ref[:, s:s + n, :], w_ref, l, 1) + cb_ref[...]
    c_ref[...] = _ln_silu(acc, lg_ref[...], lb_ref[...]).astype(c_ref.dtype)
    nst_ref[...] = ext_ref[:, l + HALO_OFF:l + HALO, :]


def _conv_sample(u, state, conv_w, conv_b, ln_g, ln_b, nb):
    n, l, _ = u.shape
    const = lambda b: (0, 0)
    blk3 = lambda b: (b, 0, 0)
    return pl.pallas_call(
        _conv_sample_body,
        grid=(n // nb,),
        in_specs=[
            pl.BlockSpec((nb, l, CONV_CH), blk3),
            pl.BlockSpec((nb, CONV_K - 1, CONV_CH), blk3),
            pl.BlockSpec((CONV_K, CONV_CH), const),
            pl.BlockSpec((1, CONV_CH), const),
            pl.BlockSpec((1, CONV_CH), const),
            pl.BlockSpec((1, CONV_CH), const),
        ],
        out_specs=[
            pl.BlockSpec((nb, l, CONV_CH), blk3),
            pl.BlockSpec((nb, CONV_K - 1, CONV_CH), blk3),
        ],
        out_shape=[
            jax.ShapeDtypeStruct((n, l, CONV_CH), BF16),
            jax.ShapeDtypeStruct((n, CONV_K - 1, CONV_CH), F32),
        ],
        scratch_shapes=[pltpu.VMEM((nb, l + HALO + SUBLANES, CONV_CH), F32)],
        compiler_params=_cparams(("parallel",)),
        name="conv_sample",
    )(u, state, conv_w, conv_b, ln_g, ln_b)


def _decay_tables(c):
    lg = jnp.log(1.0 - 2.0 ** (-5.0 - jnp.arange(RET_HEADS, dtype=F32)))
    idx = jnp.arange(c, dtype=F32)
    rel = idx[:, None] - idx[None, :]
    dmat = jnp.where(rel[None] >= 0, jnp.exp(jnp.maximum(rel, 0.0)[None] * lg[:, None, None]), 0.0)
    xi = jnp.exp((idx + 1.0)[None, :] * lg[:, None])
    zeta = jnp.exp((c - 1.0 - idx)[None, :] * lg[:, None])
    gc = jnp.exp(c * lg)
    xi_b = jnp.broadcast_to(xi[:, :, None], (RET_HEADS, c, HEAD_DIM))
    zeta_b = jnp.broadcast_to(zeta[:, :, None], (RET_HEADS, c, HEAD_DIM))
    gc_b = jnp.broadcast_to(gc[:, None, None], (RET_HEADS, 1, HEAD_DIM))
    return dmat, xi_b, zeta_b, gc_b


def _group_norm(o):
    mu = jnp.mean(o, axis=-1, keepdims=True)
    d = o - mu
    var = jnp.mean(d * d, axis=-1, keepdims=True)
    return d * lax.rsqrt(var + EPS)


def _ret_chunk(qh, kh, vh, r, dmat, xi, zeta, gc):
    qb = qh.astype(BF16)
    kb = kh.astype(BF16)
    vb = vh.astype(BF16)
    s = lax.dot_general(qb, kb, (((1,), (1,)), ((), ())), preferred_element_type=F32) * dmat
    o = jnp.dot(s.astype(BF16), vb, preferred_element_type=F32)
    o = o + jnp.dot(qb, r.astype(BF16), preferred_element_type=F32) * xi
    kz = (kh.astype(F32) * zeta).astype(BF16)
    r_new = r * gc + lax.dot_general(kz, vb, (((0,), (0,)), ((), ())), preferred_element_type=F32)
    return o, r_new


def _ret_prompt_body(q_ref, k_ref, v_ref, gs_ref, d_ref, xi_ref, zeta_ref, gc_ref, o_ref, st_ref, r_ref):
    j = pl.program_id(1)

    @pl.when(j == 0)
    def _():
        r_ref[...] = jnp.zeros_like(r_ref)

    n_chunks = q_ref.shape[0] // CHUNK
    for hh in range(RET_HEADS):
        sl = slice(hh * HEAD_DIM, (hh + 1) * HEAD_DIM)
        r = r_ref[hh]
        for ci in range(n_chunks):
            rows = slice(ci * CHUNK, (ci + 1) * CHUNK)
            o, r = _ret_chunk(q_ref[rows, sl], k_ref[rows, sl], v_ref[rows, sl], r,
                              d_ref[hh], xi_ref[hh], zeta_ref[hh], gc_ref[hh])
            o_ref[rows, sl] = (gs_ref[rows, sl] * _group_norm(o)).astype(o_ref.dtype)
        r_ref[hh] = r
    st_ref[0] = r_ref[...]


def _ret_prompt(q, k, v, gs, n, l, tl):
    dmat, xi, zeta, gc = _decay_tables(CHUNK)
    per = l // tl
    row = lambda b, j: (b * per + j, 0)
    c3 = lambda b, j: (0, 0, 0)
    return pl.pallas_call(
        _ret_prompt_body,
        grid=(n, per),
        in_specs=[pl.BlockSpec((tl, RET_WIDTH), row)] * 4 + [
            pl.BlockSpec((RET_HEADS, CHUNK, CHUNK), c3),
            pl.BlockSpec((RET_HEADS, CHUNK, HEAD_DIM), c3),
            pl.BlockSpec((RET_HEADS, CHUNK, HEAD_DIM), c3),
            pl.BlockSpec((RET_HEADS, 1, HEAD_DIM), c3),
        ],
        out_specs=[
            pl.BlockSpec((tl, RET_WIDTH), row),
            pl.BlockSpec((1, RET_HEADS, HEAD_DIM, HEAD_DIM), lambda b, j: (b, 0, 0, 0)),
        ],
        out_shape=[
            jax.ShapeDtypeStruct((n * l, RET_WIDTH), BF16),
            jax.ShapeDtypeStruct((n, RET_HEADS, HEAD_DIM, HEAD_DIM), F32),
        ],
        scratch_shapes=[pltpu.VMEM((RET_HEADS, HEAD_DIM, HEAD_DIM), F32)],
        compiler_params=_cparams(("arbitrary", "arbitrary")),
        name="ret_prompt",
    )(q, k, v, gs, dmat, xi, zeta, gc)


def _ret_sample_body(q_ref, k_ref, v_ref, gs_ref, st_ref, d_ref, xi_ref, zeta_ref, gc_ref, o_ref, nst_ref):
    nb = st_ref.shape[0]
    l = q_ref.shape[0] // nb
    for b in range(nb):
        rows = slice(b * l, (b + 1) * l)
        for hh in range(RET_HEADS):
            sl = slice(hh * HEAD_DIM, (hh + 1) * HEAD_DIM)
            o, r = _ret_chunk(q_ref[rows, sl], k_ref[rows, sl], v_ref[rows, sl], st_ref[b, hh],
                              d_ref[hh], xi_ref[hh], zeta_ref[hh], gc_ref[hh])
            o_ref[rows, sl] = (gs_ref[rows, sl] * _group_norm(o)).astype(o_ref.dtype)
            nst_ref[b, hh] = r


def _ret_sample(q, k, v, gs, state, l, nb):
    n = state.shape[0]
    dmat, xi, zeta, gc = _decay_tables(l)
    row = lambda b: (b, 0)
    c3 = lambda b: (0, 0, 0)
    blk4 = lambda b: (b, 0, 0, 0)
    return pl.pallas_call(
        _ret_sample_body,
        grid=(n // nb,),
        in_specs=[pl.BlockSpec((nb * l, RET_WIDTH), row)] * 4 + [
            pl.BlockSpec((nb, RET_HEADS, HEAD_DIM, HEAD_DIM), blk4),
            pl.BlockSpec((RET_HEADS, l, l), c3),
            pl.BlockSpec((RET_HEADS, l, HEAD_DIM), c3),
            pl.BlockSpec((RET_HEADS, l, HEAD_DIM), c3),
            pl.BlockSpec((RET_HEADS, 1, HEAD_DIM), c3),
        ],
        out_specs=[
            pl.BlockSpec((nb * l, RET_WIDTH), row),
            pl.BlockSpec((nb, RET_HEADS, HEAD_DIM, HEAD_DIM), blk4),
        ],
        out_shape=[
            jax.ShapeDtypeStruct((n * l, RET_WIDTH), BF16),
            jax.ShapeDtypeStruct((n, RET_HEADS, HEAD_DIM, HEAD_DIM), F32),
        ],
        compiler_params=_cparams(("parallel",)),
        name="ret_sample",
    )(q, k, v, gs, state, dmat, xi, zeta, gc)


def _split3(x):
    hi = x.astype(BF16)
    r1 = x - hi.astype(F32)
    mid = r1.astype(BF16)
    lo = (r1 - mid.astype(F32)).astype(BF16)
    return hi, mid, lo


def _dot_hp(t, w2_ref):
    t_hi, t_mid, _ = _split3(t)
    d = functools.partial(jnp.dot, preferred_element_type=F32)
    both = d(t_hi, w2_ref[...])
    return both[:, :LANES] + (d(t_mid, w2_ref[:, 0:LANES]) + both[:, LANES:])


def _route(logits):
    lt = logits.T
    tm = lt.shape[1]
    row = _iota_f32((SUBLANES, tm), 0)
    big = float(SUBLANES)

    def rmax(x):
        return jnp.max(x, axis=0, keepdims=True)

    def first_row(mask):
        return jnp.min(jnp.where(mask, row, big), axis=0, keepdims=True)

    lg = jnp.where(row < float(N_GROUPS), lt[N_EXPERTS:N_EXPERTS + SUBLANES, :], -1e30)
    m = rmax(lg)
    g_top = 1.0 / jnp.sum(jnp.exp(lg - m), axis=0, keepdims=True)
    g_idx = first_row(lg == m)
    lem = lt[0:EXPERTS_PER_GROUP, :]
    for g in range(1, N_GROUPS):
        lem = jnp.where(g_idx == float(g), lt[g * EXPERTS_PER_GROUP:(g + 1) * EXPERTS_PER_GROUP, :], lem)
    pe = jnp.exp(lem - rmax(lem))
    p1 = rmax(pe)
    e1 = first_row(pe == p1)
    rest = row != e1
    pe2 = jnp.where(rest, pe, -1.0)
    p2 = rmax(pe2)
    e2 = first_row(rest & (pe2 == p2))
    scale = g_top / (p1 + p2)
    base = g_idx * float(EXPERTS_PER_GROUP)
    rec_t = jnp.where(row == 0.0, base + e1, jnp.where(row == 1.0, base + e2, 0.0))
    rec_t = rec_t + jnp.where(row == 2.0, p1 * scale, jnp.where(row == 3.0, p2 * scale, 0.0))
    return jnp.concatenate([rec_t, jnp.zeros((LANES - SUBLANES, tm), F32)], axis=0).T


def _outproj_body(c_ref, o_ref, x_ref, wo_ref, g2_ref, wr_ref, br_ref, x1_ref, t_ref, rec_ref):
    x1 = x_ref[...] + jnp.dot(c_ref[...], wo_ref[0:CONV_CH, :], preferred_element_type=F32)
    x1 = x1 + jnp.dot(o_ref[...], wo_ref[CONV_CH:, :], preferred_element_type=F32)
    x1_ref[...] = x1
    t = _rms(x1, g2_ref[...])
    t_ref[...] = t.astype(t_ref.dtype)
    rec_ref[...] = _route(_dot_hp(t, wr_ref) + br_ref[...])


def _outproj(c, o, x, w_out, g2, wr2, br, tm):
    t = x.shape[0]
    row = lambda i: (i, 0)
    const = lambda i: (0, 0)
    c3 = lambda i: (0, 0, 0)
    return pl.pallas_call(
        _outproj_body,
        grid=(t // tm,),
        in_specs=[
            pl.BlockSpec((tm, CONV_CH), row),
            pl.BlockSpec((tm, RET_WIDTH), row),
            pl.BlockSpec((tm, D_MODEL), row),
            pl.BlockSpec((D_MODEL, D_MODEL), const),
            pl.BlockSpec((1, D_MODEL), const),
            pl.BlockSpec((D_MODEL, 2 * LANES), const),
            pl.BlockSpec((1, LANES), const),
        ],
        out_specs=[
            pl.BlockSpec((tm, D_MODEL), row),
            pl.BlockSpec((tm, D_MODEL), row),
            pl.BlockSpec((tm, LANES), row),
        ],
        out_shape=[
            jax.ShapeDtypeStruct((t, D_MODEL), F32),
            jax.ShapeDtypeStruct((t, D_MODEL), BF16),
            jax.ShapeDtypeStruct((t, LANES), F32),
        ],
        compiler_params=_cparams(("parallel",)),
        name="outproj_router",
    )(c, o, x, w_out, g2, wr2, br)


def _mix_body(x_ref, g1_ref, w_ref, cos_ref, sin_ref, cw_ref, cb_ref, lg_ref, lb_ref,
              d_ref, xi_ref, zeta_ref, gc_ref, wo_ref, g2_ref, wr_ref, br_ref,
              x1_ref, t_ref, rec_ref, cst_ref, rst_ref, ext_ref, r_ref, o_ref):
    j = pl.program_id(1)
    tl = x_ref.shape[0]
    x = x_ref[...]
    z = jnp.dot(_rms(x, g1_ref[...]).astype(BF16), w_ref[...], preferred_element_type=F32)

    @pl.when(j == 0)
    def _():
        ext_ref[0:HALO, :] = jnp.zeros((HALO, CONV_CH), F32)
        ext_ref[tl + HALO:, :] = jnp.zeros((SUBLANES, CONV_CH), F32)
        r_ref[...] = jnp.zeros_like(r_ref)

    @pl.when(j > 0)
    def _():
        ext_ref[0:HALO, :] = ext_ref[tl:tl + HALO, :]

    ext_ref[HALO:tl + HALO, :] = z[:, :CONV_CH] * jax.nn.sigmoid(z[:, CONV_CH:2 * CONV_CH])
    acc = _dwconv(lambda s, n: ext_ref[s:s + n, :], cw_ref, tl, 0) + cb_ref[...]
    c = _ln_silu(acc, lg_ref[...], lb_ref[...]).astype(BF16)
    cst_ref[0] = ext_ref[tl + HALO_OFF:tl + HALO, :]

    cos = cos_ref[...]
    sin = sin_ref[...]
    q0 = 2 * CONV_CH
    k0 = q0 + RET_WIDTH
    v0 = k0 + RET_WIDTH
    g0 = v0 + RET_WIDTH
    for hh in range(RET_HEADS):
        lo = hh * HEAD_DIM
        qh = z[:, q0 + lo:q0 + lo + HEAD_DIM]
        kh = z[:, k0 + lo:k0 + lo + HEAD_DIM]
        qr = (qh * cos + pltpu.roll(qh, HEAD_DIM // 2, 1) * sin).astype(BF16)
        kr = ((kh * cos + pltpu.roll(kh, HEAD_DIM // 2, 1) * sin) * (HEAD_DIM ** -0.5)).astype(BF16)
        vh = z[:, v0 + lo:v0 + lo + HEAD_DIM].astype(BF16)
        g = z[:, g0 + lo:g0 + lo + HEAD_DIM]
        gs = g * jax.nn.sigmoid(g)
        r = r_ref[hh]
        for ci in range(tl // CHUNK):
            rows = slice(ci * CHUNK, (ci + 1) * CHUNK)
            o, r = _ret_chunk(qr[rows], kr[rows], vh[rows], r, d_ref[hh], xi_ref[hh], zeta_ref[hh], gc_ref[hh])
            o_ref[rows, lo:lo + HEAD_DIM] = (gs[rows] * _group_norm(o)).astype(BF16)
        r_ref[hh] = r
    rst_ref[0] = r_ref[...]

    x1 = x + jnp.dot(c, wo_ref[0:CONV_CH, :], preferred_element_type=F32)
    x1 = x1 + jnp.dot(o_ref[...], wo_ref[CONV_CH:, :], preferred_element_type=F32)
    x1_ref[...] = x1
    t = _rms(x1, g2_ref[...])
    t_ref[...] = t.astype(t_ref.dtype)
    rec_ref[...] = _route(_dot_hp(t, wr_ref) + br_ref[...])


def _mix(x, g1, w_in, cos, sin, conv_w, conv_b, ln_g, ln_b, w_out, g2, wr2, br, n, l, tl):
    dmat, xi, zeta, gc = _decay_tables(CHUNK)
    per = l // tl
    row = lambda b, j: (b * per + j, 0)
    tab = lambda b, j: (j, 0)
    const = lambda b, j: (0, 0)
    c3 = lambda b, j: (0, 0, 0)
    once = dict(pipeline_mode=pl.Buffered(1))
    return pl.pallas_call(
        _mix_body,
        grid=(n, per),
        in_specs=[
            pl.BlockSpec((tl, D_MODEL), row),
            pl.BlockSpec((1, D_MODEL), const),
            pl.BlockSpec((D_MODEL, IN_COLS), const, **once),
            pl.BlockSpec((tl, HEAD_DIM), tab),
            pl.BlockSpec((tl, HEAD_DIM), tab),
            pl.BlockSpec((CONV_K, CONV_CH), const),
            pl.BlockSpec((1, CONV_CH), const),
            pl.BlockSpec((1, CONV_CH), const),
            pl.BlockSpec((1, CONV_CH), const),
            pl.BlockSpec((RET_HEADS, CHUNK, CHUNK), c3),
            pl.BlockSpec((RET_HEADS, CHUNK, HEAD_DIM), c3),
            pl.BlockSpec((RET_HEADS, CHUNK, HEAD_DIM), c3),
            pl.BlockSpec((RET_HEADS, 1, HEAD_DIM), c3),
            pl.BlockSpec((D_MODEL, D_MODEL), const, **once),
            pl.BlockSpec((1, D_MODEL), const),
            pl.BlockSpec((D_MODEL, 2 * LANES), const, **once),
            pl.BlockSpec((1, LANES), const),
        ],
        out_specs=[
            pl.BlockSpec((tl, D_MODEL), row),
            pl.BlockSpec((tl, D_MODEL), row),
            pl.BlockSpec((tl, LANES), row),
            pl.BlockSpec((1, CONV_K - 1, CONV_CH), lambda b, j: (b, 0, 0)),
            pl.BlockSpec((1, RET_HEADS, HEAD_DIM, HEAD_DIM), lambda b, j: (b, 0, 0, 0)),
        ],
        out_shape=[
            jax.ShapeDtypeStruct((n * l, D_MODEL), F32),
            jax.ShapeDtypeStruct((n * l, D_MODEL), BF16),
            jax.ShapeDtypeStruct((n * l, LANES), F32),
            jax.ShapeDtypeStruct((n, CONV_K - 1, CONV_CH), F32),
            jax.ShapeDtypeStruct((n, RET_HEADS, HEAD_DIM, HEAD_DIM), F32),
        ],
        scratch_shapes=[
            pltpu.VMEM((tl + HALO + SUBLANES, CONV_CH), F32),
            pltpu.VMEM((RET_HEADS, HEAD_DIM, HEAD_DIM), F32),
            pltpu.VMEM((tl, RET_WIDTH), BF16),
        ],
        compiler_params=pltpu.CompilerParams(dimension_semantics=("arbitrary", "arbitrary"),
                                             vmem_limit_bytes=MIX_VMEM_LIMIT),
        name="token_mix",
    )(x, g1, w_in, cos, sin, conv_w, conv_b, ln_g, ln_b, dmat, xi, zeta, gc, w_out, g2, wr2, br)


def _iota_f32(shape, dim):
    return lax.broadcasted_iota(jnp.int32, shape, dim).astype(F32)


def _dispatch_body(rec_a_ref, t_a_ref, rec_b_ref, t_b_ref, s_ref, pos_ref, meta_ref, *, nsub_a):
    from_a = pl.program_id(0) * DISPATCH_SUBS < nsub_a
    for s in range(DISPATCH_SUBS):
        rows = slice(s * SUB, (s + 1) * SUB)
        rec = jnp.where(from_a, rec_a_ref[rows, :], rec_b_ref[rows, :])
        tok = jnp.where(from_a, t_a_ref[rows, :], t_b_ref[rows, :])
        chunks = slice(s * CHUNKS_PER_SUB, (s + 1) * CHUNKS_PER_SUB)
        _dispatch_sub_tile(rec, tok, s_ref.at[chunks], pos_ref.at[rows], meta_ref.at[s])


def _dispatch_sub_tile(rec, tok, s_ref, pos_ref, meta_ref):
    lane = _iota_f32(rec.shape, 1)
    a1 = lane == rec[:, 0:1]
    a2 = lane == rec[:, 1:2]
    a1f = jnp.where(a1, 1.0, 0.0)
    a2f = jnp.where(a2, 1.0, 0.0)
    ltri = jnp.where(_iota_f32((SUB, SUB), 1) < _iota_f32((SUB, SUB), 0), 1.0, 0.0).astype(BF16)
    c1 = jnp.dot(ltri, a1f.astype(BF16), preferred_element_type=F32)
    c2 = jnp.dot(ltri, a2f.astype(BF16), preferred_element_type=F32)
    n1 = jnp.sum(a1f, axis=0, keepdims=True)
    n2 = jnp.sum(a2f, axis=0, keepdims=True)
    cnt = jnp.floor((n1 + n2 + (ROW_ALIGN - 1.0)) * (1.0 / ROW_ALIGN))
    utri = jnp.where(_iota_f32((LANES, LANES), 0) < _iota_f32((LANES, LANES), 1), 1.0, 0.0).astype(BF16)
    start = jnp.dot(jnp.broadcast_to(cnt, (SUBLANES, LANES)).astype(BF16), utri,
                    preferred_element_type=F32)[0:1]
    base1 = start * ROW_ALIGN
    base2 = base1 + n1
    pos1 = jnp.sum(jnp.where(a1, c1 + base1, 0.0), axis=1, keepdims=True)
    pos2 = jnp.sum(jnp.where(a2, c2 + base2, 0.0), axis=1, keepdims=True)
    posm = jnp.where(lane == 0.0, pos1, jnp.where(lane == 1.0, pos2, 0.0))
    pos_ref[...] = posm
    row = _iota_f32((SUBLANES, LANES), 0)
    meta_ref[...] = jnp.where(row == 0.0, start, jnp.where(row == 1.0, cnt, 0.0))

    g1 = _split3(rec[:, 2:3])
    g2 = _split3(rec[:, 3:4])
    info = jnp.where(lane == 6.0, rec[:, 0:1], jnp.where(lane == 7.0, rec[:, 1:2], 0.0))
    for i in range(3):
        info = jnp.where(lane == float(i), g1[i].astype(F32), info)
        info = jnp.where(lane == float(3 + i), g2[i].astype(F32), info)
    src = jnp.concatenate([tok, info.astype(BF16)], axis=1)

    post = posm.T
    r = _iota_f32((SUBP, SUB), 0)
    onehot = jnp.where(r == post[0:1, :], 1.0, jnp.where(r == post[1:2, :], 1.0, 0.0)).astype(BF16)
    sorted_rows = jnp.dot(onehot, src, preferred_element_type=F32).astype(BF16)
    s_ref[...] = sorted_rows.reshape(CHUNKS_PER_SUB, ROW_ALIGN, ROW_W)


def _dispatch(rec_a, t_a, rec_b, t_b):
    nsub_a = rec_a.shape[0] // SUB
    nsub_b = rec_b.shape[0] // SUB
    nsub = nsub_a + nsub_b
    assert nsub_a % DISPATCH_SUBS == 0 and nsub_b % DISPATCH_SUBS == 0
    steps_a = nsub_a // DISPATCH_SUBS
    tm = DISPATCH_SUBS * SUB
    row = lambda i: (i, 0)
    row_a = lambda i: (jnp.minimum(i, steps_a - 1), 0)
    row_b = lambda i: (jnp.maximum(i - steps_a, 0), 0)
    return pl.pallas_call(
        functools.partial(_dispatch_body, nsub_a=nsub_a),
        grid=(nsub // DISPATCH_SUBS,),
        in_specs=[
            pl.BlockSpec((tm, LANES), row_a),
            pl.BlockSpec((tm, D_MODEL), row_a),
            pl.BlockSpec((tm, LANES), row_b),
            pl.BlockSpec((tm, D_MODEL), row_b),
        ],
        out_specs=[
            pl.BlockSpec((DISPATCH_SUBS * CHUNKS_PER_SUB, ROW_ALIGN, ROW_W), lambda i: (i, 0, 0)),
            pl.BlockSpec((tm, LANES), row),
            pl.BlockSpec((DISPATCH_SUBS, SUBLANES, LANES), lambda i: (i, 0, 0)),
        ],
        out_shape=[
            jax.ShapeDtypeStruct((nsub * CHUNKS_PER_SUB, ROW_ALIGN, ROW_W), BF16),
            jax.ShapeDtypeStruct((nsub * SUB, LANES), F32),
            jax.ShapeDtypeStruct((nsub, SUBLANES, LANES), F32),
        ],
        compiler_params=_cparams(("parallel",)),
        name="moe_dispatch",
    )(rec_a, t_a, rec_b, t_b)


def _experts_body(start_ref, cnt_ref, s_in, wg_ref, wu_ref, wd_ref, s_hbm,
                  xbuf, ybuf, wgu_ref, wdb_ref, gsem, ssem, list_ref, state_ref, *, nsub):
    del s_in
    e = pl.program_id(0)
    ne = pl.num_programs(0)
    par = e & 1

    list_max = list_ref.shape[0] // 2

    def gather_copy(src, i, slot):
        return pltpu.make_async_copy(s_hbm.at[src], xbuf.at[slot, i], gsem.at[slot])

    def scatter_copy(dst, i, slot):
        return pltpu.make_async_copy(ybuf.at[slot, i], s_hbm.at[dst], ssem.at[slot])

    def build_list(x, which):
        def per_sub(s, k):
            run = s * N_EXPERTS + x
            c = cnt_ref[run]
            base = s * CHUNKS_PER_SUB + start_ref[run]
            list_ref[k] = base
            list_ref[k + 1] = base + 1

            def per_chunk(i, carry):
                list_ref[k + i] = base + i
                return carry
            lax.fori_loop(2, c, per_chunk, 0)
            return k + c
        first = which * list_max
        state_ref[which] = lax.fori_loop(0, nsub, per_sub, first) - first

    def start_all(copy, which, first, n, slot, counter):
        def body(i, carry):
            copy(list_ref[which * list_max + first + i], i, slot).start()
            return carry
        lax.fori_loop(0, n, body, 0)
        state_ref[counter] = n

    def wait_all(copy, block_copy, slot, counter):
        n = state_ref[counter]

        @pl.when(n == CHUNKS_PER_BLK)
        def _():
            block_copy(slot).wait()

        @pl.when(n < CHUNKS_PER_BLK)
        def _():
            def body(i, carry):
                copy(0, 0, slot).wait()
                return carry
            lax.fori_loop(0, n, body, 0)
        state_ref[counter] = 0

    def gather_block(slot):
        return pltpu.make_async_copy(s_hbm.at[pl.ds(0, CHUNKS_PER_BLK)], xbuf.at[slot], gsem.at[slot])

    def scatter_block(slot):
        return pltpu.make_async_copy(ybuf.at[slot], s_hbm.at[pl.ds(0, CHUNKS_PER_BLK)], ssem.at[slot])

    def block_chunks(total, b):
        return jnp.minimum(total - b * CHUNKS_PER_BLK, CHUNKS_PER_BLK)

    @pl.when(e == 0)
    def _():
        for i in range(6):
            state_ref[i] = 0
        xbuf[...] = jnp.zeros_like(xbuf)
        build_list(0, 0)
        n0 = state_ref[0]

        @pl.when(n0 > 0)
        def _():
            start_all(gather_copy, 0, 0, block_chunks(n0, 0), 0, 2)

    @pl.when(e + 1 < ne)
    def _():
        build_list(e + 1, 1 - par)

    total = state_ref[par]
    nblk = (total + CHUNKS_PER_BLK - 1) // CHUNKS_PER_BLK
    wgu_ref[:, 0:EXPERT_FF] = wg_ref[0].astype(BF16)
    wgu_ref[:, EXPERT_FF:] = wu_ref[0].astype(BF16)
    wdb_ref[...] = wd_ref[0].astype(BF16)
    ef = e.astype(F32)

    def mlp(slot):
        x = xbuf[slot].reshape(MBLK, ROW_W)
        info = x[:, D_MODEL:].astype(F32)
        g_first = info[:, 0:1] + info[:, 1:2] + info[:, 2:3]
        g_second = info[:, 3:4] + info[:, 4:5] + info[:, 5:6]
        gate = jnp.where(info[:, 6:7] == ef, g_first, g_second)
        h = jnp.dot(x[:, :D_MODEL], wgu_ref[...], preferred_element_type=F32)
        h1 = h[:, :EXPERT_FF]
        hid = (h1 * jax.nn.sigmoid(h1)) * h[:, EXPERT_FF:] * gate
        y = jnp.dot(hid.astype(BF16), wdb_ref[...], preferred_element_type=F32).astype(BF16)
        ybuf[slot] = jnp.concatenate([y, x[:, D_MODEL:]], axis=1).reshape(CHUNKS_PER_BLK, ROW_ALIGN, ROW_W)

    def block(b, carry):
        slot = b & 1
        first = b * CHUNKS_PER_BLK
        n_here = block_chunks(total, b)
        n_next = jnp.where(b + 1 < nblk, block_chunks(total, b + 1), 0)
        both_full = (n_here == CHUNKS_PER_BLK) & (n_next == CHUNKS_PER_BLK)

        @pl.when(both_full)
        def _():
            wait_all(gather_copy, gather_block, slot, 2 + slot)
            wait_all(scatter_copy, scatter_block, slot, 4 + slot)
            here = par * list_max + first
            for i in range(CHUNKS_PER_BLK):
                gather_copy(list_ref[here + CHUNKS_PER_BLK + i], i, 1 - slot).start()
            state_ref[3 - slot] = CHUNKS_PER_BLK
            mlp(slot)
            for i in range(CHUNKS_PER_BLK):
                scatter_copy(list_ref[here + i], i, slot).start()
            state_ref[4 + slot] = CHUNKS_PER_BLK

        @pl.when(jnp.logical_not(both_full))
        def _():
            @pl.when(n_next > 0)
            def _():
                start_all(gather_copy, par, first + CHUNKS_PER_BLK, n_next, 1 - slot, 3 - slot)

            wait_all(gather_copy, gather_block, slot, 2 + slot)
            wait_all(scatter_copy, scatter_block, slot, 4 + slot)
            mlp(slot)
            start_all(scatter_copy, par, first, n_here, slot, 4 + slot)
        return carry

    lax.fori_loop(0, nblk, block, 0)

    @pl.when(e + 1 < ne)
    def _():
        n1 = state_ref[1 - par]

        @pl.when(n1 > 0)
        def _():
            start_all(gather_copy, 1 - par, 0, block_chunks(n1, 0), 0, 2)

    @pl.when(e == ne - 1)
    def _():
        wait_all(scatter_copy, scatter_block, 0, 4)
        wait_all(scatter_copy, scatter_block, 1, 5)


def _experts(start, cnt, staged, wg, wu, wd):
    nsub = staged.shape[0] // CHUNKS_PER_SUB
    list_max = nsub * SUB // ROW_ALIGN + nsub + LIST_SLACK
    wblk = lambda e, *_: (e, 0, 0)
    grid_spec = pltpu.PrefetchScalarGridSpec(
        num_scalar_prefetch=2,
        grid=(N_EXPERTS,),
        in_specs=[
            pl.BlockSpec(memory_space=pl.ANY),
            pl.BlockSpec((1, D_MODEL, EXPERT_FF), wblk),
            pl.BlockSpec((1, D_MODEL, EXPERT_FF), wblk),
            pl.BlockSpec((1, EXPERT_FF, D_MODEL), wblk),
        ],
        out_specs=pl.BlockSpec(memory_space=pl.ANY),
        scratch_shapes=[
            pltpu.VMEM((2, CHUNKS_PER_BLK, ROW_ALIGN, ROW_W), BF16),
            pltpu.VMEM((2, CHUNKS_PER_BLK, ROW_ALIGN, ROW_W), BF16),
            pltpu.VMEM((D_MODEL, 2 * EXPERT_FF), BF16),
            pltpu.VMEM((EXPERT_FF, D_MODEL), BF16),
            pltpu.SemaphoreType.DMA((2,)),
            pltpu.SemaphoreType.DMA((2,)),
            pltpu.SMEM((2 * list_max,), jnp.int32),
            pltpu.SMEM((6,), jnp.int32),
        ],
    )
    return pl.pallas_call(
        functools.partial(_experts_body, nsub=nsub),
        grid_spec=grid_spec,
        out_shape=jax.ShapeDtypeStruct(staged.shape, staged.dtype),
        input_output_aliases={2: 0},
        compiler_params=_cparams(("arbitrary",)),
        name="moe_experts",
    )(start, cnt, staged, wg, wu, wd)


def _combine_body(ys_ref, pos_ref, x1_ref, p_ref, wp_ref, gp_ref, wpg_ref, gf_ref, y_ref):
    r = _iota_f32((SUB, SUBP), 1)
    moe = []
    for s in range(x1_ref.shape[0] // SUB):
        p1 = pos_ref[s * SUB:(s + 1) * SUB, 0:1]
        p2 = pos_ref[s * SUB:(s + 1) * SUB, 1:2]
        onehot = jnp.where(r == p1, 1.0, jnp.where(r == p2, 1.0, 0.0)).astype(BF16)
        ys = ys_ref[s * CHUNKS_PER_SUB:(s + 1) * CHUNKS_PER_SUB].reshape(SUBP, D_MODEL)
        moe.append(jnp.dot(onehot, ys, preferred_element_type=F32))
    x2 = x1_ref[...] + jnp.concatenate(moe, axis=0)
    ple = _rms(jnp.dot(p_ref[...].astype(BF16), wp_ref[...], preferred_element_type=F32), gp_ref[...])
    gate = jax.nn.sigmoid(jnp.dot(x2.astype(BF16), wpg_ref[...], preferred_element_type=F32))
    y_ref[...] = _rms(x2 + ple * gate, gf_ref[...])


def _combine(ys, pos, x1, p, w_ple, gp, w_ple_gate, gf, sub_off):
    t = x1.shape[0]
    tm = COMBINE_SUBS * SUB
    blk_off = sub_off // COMBINE_SUBS
    assert sub_off % COMBINE_SUBS == 0 and t % tm == 0
    row = lambda i: (i, 0)
    const = lambda i: (0, 0)
    return pl.pallas_call(
        _combine_body,
        grid=(t // tm,),
        in_specs=[
            pl.BlockSpec((COMBINE_SUBS * CHUNKS_PER_SUB, ROW_ALIGN, D_MODEL), lambda i: (i + blk_off, 0, 0)),
            pl.BlockSpec((tm, LANES), lambda i: (i + blk_off, 0)),
            pl.BlockSpec((tm, D_MODEL), row),
            pl.BlockSpec((tm, PLE_DIM), row),
            pl.BlockSpec((PLE_DIM, D_MODEL), const),
            pl.BlockSpec((1, D_MODEL), const),
            pl.BlockSpec((D_MODEL, D_MODEL), const),
            pl.BlockSpec((1, D_MODEL), const),
        ],
        out_specs=pl.BlockSpec((tm, D_MODEL), row),
        out_shape=jax.ShapeDtypeStruct((t, D_MODEL), F32),
        compiler_params=_cparams(("parallel",)),
        name="moe_combine_ple",
    )(ys, pos, x1, p, w_ple, gp, w_ple_gate, gf)


def _rope_tables(pos):
    half = HEAD_DIM // 2
    inv = ROPE_BASE ** (-jnp.arange(half, dtype=F32) / half)
    ang = pos[:, None] * inv[None, :]
    cos = jnp.cos(ang)
    sin = jnp.sin(ang)
    return jnp.concatenate([cos, cos], axis=-1), jnp.concatenate([-sin, sin], axis=-1)


def _router_params(we, be, wg, bg):
    pad = LANES - N_EXPERTS - N_GROUPS
    w = jnp.pad(jnp.concatenate([we, wg], axis=1), ((0, 0), (0, pad)))
    b = jnp.pad(jnp.concatenate([be, bg]), (0, pad))[None, :]
    return jnp.concatenate(_split3(w)[:2], axis=1), b


def kernel(x_prompt, x_sample, p_prompt, p_sample, state_conv, state_ret, w_in, conv_w, conv_b, conv_ln_g, conv_ln_b, w_out, norm1_g, norm2_g, router_group_w, router_group_b, router_expert_w, router_expert_b, w_expert_gate, w_expert_up, w_expert_down, w_ple, ple_norm_g, w_ple_gate, final_norm_g):
    assert w_in.shape[0] == 1, "single-layer trunk"
    nb, seq, _ = x_prompt.shape
    ns, dseq, _ = x_sample.shape
    tm = 512

    w_in_b = w_in[0].astype(BF16)
    w_out_b = w_out[0].astype(BF16)
    w_ple_b = w_ple[0].astype(BF16)
    w_pg_b = w_ple_gate[0].astype(BF16)
    g1 = norm1_g[0][None, :]
    g2 = norm2_g[0][None, :]
    gp = ple_norm_g[0][None, :]
    gf = final_norm_g[None, :]
    cb = conv_b[0][None, :]
    lng = conv_ln_g[0][None, :]
    lnb = conv_ln_b[0][None, :]
    wr2, br = _router_params(router_expert_w[0], router_expert_b[0], router_group_w[0], router_group_b[0])

    cos_p, sin_p = _rope_tables(jnp.arange(seq, dtype=F32) + jnp.float32(0))
    pos_s = jnp.tile(jnp.arange(dseq, dtype=F32) + jnp.float32(PAST_LEN), tm // dseq)
    cos_s, sin_s = _rope_tables(pos_s)

    xp = x_prompt.reshape(nb * seq, D_MODEL)
    x1_p, t_p, rec_p, conv_p, ret_p = _mix(xp, g1, w_in_b, cos_p, sin_p, conv_w[0], cb, lng, lnb, w_out_b, g2,
                                           wr2, br, nb, seq, MIX_TILE)

    xs = x_sample.reshape(ns * dseq, D_MODEL)
    u, q, k, v, gs = _inproj(xs, g1, w_in_b, cos_s, sin_s, tm, 1, F32)
    c, conv_s = _conv_sample(u.reshape(ns, dseq, CONV_CH), state_conv[0], conv_w[0], cb, lng, lnb, 16)
    o, ret_s = _ret_sample(q, k, v, gs, state_ret[0], dseq, 8)
    x1_s, t_s, rec_s = _outproj(c.reshape(ns * dseq, CONV_CH), o, xs, w_out_b, g2, wr2, br, tm)

    staged, pos, meta = _dispatch(rec_p, t_p, rec_s, t_s)
    start = meta[:, 0, :N_EXPERTS].astype(jnp.int32).reshape(-1)
    cnt = meta[:, 1, :N_EXPERTS].astype(jnp.int32).reshape(-1)
    ys = _experts(start, cnt, staged, w_expert_gate[0], w_expert_up[0], w_expert_down[0])

    y_p = _combine(ys, pos, x1_p, p_prompt[0].reshape(nb * seq, PLE_DIM), w_ple_b, gp, w_pg_b, gf, 0)
    y_s = _combine(ys, pos, x1_s, p_sample[0].reshape(ns * dseq, PLE_DIM), w_ple_b, gp, w_pg_b, gf,
                   nb * seq // SUB)

    return (y_p.reshape(nb, seq, D_MODEL), y_s.reshape(ns, dseq, D_MODEL),
            conv_p[None], ret_p[None], conv_s[None], ret_s[None])
```

```python
import functools

import jax
import jax.numpy as jnp
from jax import lax
from jax.experimental import pallas as pl
from jax.experimental.pallas import tpu as pltpu

F32 = jnp.float32
BF16 = jnp.bfloat16

D_MODEL = 1024
PLE_DIM = 256
CONV_CH = 512
CONV_K = 31
RET_WIDTH = 512
RET_HEADS = 4
HEAD_DIM = 128
CHUNK = 128
ROPE_BASE = 10000.0
N_GROUPS = 4
EXPERTS_PER_GROUP = 8
N_EXPERTS = 32
EXPERT_FF = 256
IN_COLS = 3072
EPS = 1e-6
PAST_LEN = 16384

LANES = 128
SUBLANES = 8
HALO = 32
HALO_OFF = HALO - (CONV_K - 1)
VMEM_LIMIT = 48 * 1024 * 1024
MIX_VMEM_LIMIT = 56 * 1024 * 1024
MIX_TILE = 512

SUB = 256
ROW_ALIGN = 16
PBLK = 256
SUBP = -(-(2 * SUB + N_EXPERTS * (ROW_ALIGN - 1)) // PBLK) * PBLK
CHUNKS_PER_SUB = SUBP // ROW_ALIGN
ROW_W = D_MODEL + LANES
MBLK = 512
CHUNKS_PER_BLK = MBLK // ROW_ALIGN
LIST_SLACK = 2
COMBINE_SUBS = 4
DISPATCH_SUBS = 4


def _cparams(sem):
    return pltpu.CompilerParams(dimension_semantics=sem, vmem_limit_bytes=VMEM_LIMIT)


def _rms(x, g):
    return x * lax.rsqrt(jnp.mean(x * x, axis=-1, keepdims=True) + EPS) * g


def _inproj_body(x_ref, g1_ref, w_ref, cos_ref, sin_ref, u_ref, q_ref, k_ref, v_ref, gs_ref):
    h = _rms(x_ref[...], g1_ref[...]).astype(BF16)
    z = jnp.dot(h, w_ref[...], preferred_element_type=F32)
    a = z[:, :CONV_CH]
    b = z[:, CONV_CH:2 * CONV_CH]
    u_ref[...] = a * jax.nn.sigmoid(b)
    cos = cos_ref[...]
    sin = sin_ref[...]
    q0 = 2 * CONV_CH
    k0 = q0 + RET_WIDTH
    for hh in range(RET_HEADS):
        sl = slice(hh * HEAD_DIM, (hh + 1) * HEAD_DIM)
        qh = z[:, q0 + hh * HEAD_DIM:q0 + (hh + 1) * HEAD_DIM]
        kh = z[:, k0 + hh * HEAD_DIM:k0 + (hh + 1) * HEAD_DIM]
        q_ref[:, sl] = (qh * cos + pltpu.roll(qh, HEAD_DIM // 2, 1) * sin).astype(q_ref.dtype)
        kr = (kh * cos + pltpu.roll(kh, HEAD_DIM // 2, 1) * sin) * (HEAD_DIM ** -0.5)
        k_ref[:, sl] = kr.astype(k_ref.dtype)
    v_ref[...] = z[:, k0 + RET_WIDTH:k0 + 2 * RET_WIDTH].astype(v_ref.dtype)
    g = z[:, k0 + 2 * RET_WIDTH:]
    gs_ref[...] = g * jax.nn.sigmoid(g)


def _inproj(x, g1, w_in, cos, sin, tm, table_blocks, qkv_dtype):
    t = x.shape[0]
    row = lambda i: (i, 0)
    const = lambda i: (0, 0)
    tab = (lambda i: (i % table_blocks, 0)) if table_blocks > 1 else const
    return pl.pallas_call(
        _inproj_body,
        grid=(t // tm,),
        in_specs=[
            pl.BlockSpec((tm, D_MODEL), row),
            pl.BlockSpec((1, D_MODEL), const),
            pl.BlockSpec((D_MODEL, IN_COLS), const),
            pl.BlockSpec((tm, HEAD_DIM), tab),
            pl.BlockSpec((tm, HEAD_DIM), tab),
        ],
        out_specs=[pl.BlockSpec((tm, CONV_CH), row)] + [pl.BlockSpec((tm, RET_WIDTH), row)] * 4,
        out_shape=[
            jax.ShapeDtypeStruct((t, CONV_CH), F32),
            jax.ShapeDtypeStruct((t, RET_WIDTH), qkv_dtype),
            jax.ShapeDtypeStruct((t, RET_WIDTH), qkv_dtype),
            jax.ShapeDtypeStruct((t, RET_WIDTH), qkv_dtype),
            jax.ShapeDtypeStruct((t, RET_WIDTH), F32),
        ],
        compiler_params=_cparams(("parallel",)),
        name="inproj",
    )(x, g1, w_in, cos, sin)


def _ln_silu(acc, g, b):
    mu = jnp.mean(acc, axis=-1, keepdims=True)
    d = acc - mu
    var = jnp.mean(d * d, axis=-1, keepdims=True)
    y = d * lax.rsqrt(var + EPS) * g + b
    return y * jax.nn.sigmoid(y)


def _dwconv(load, w_ref, rows, time_axis):
    acc = None
    for b in range(SUBLANES):
        part = None
        for a in range((CONV_K + HALO_OFF) // SUBLANES + 1):
            k = SUBLANES * a + b - HALO_OFF
            if 0 <= k < CONV_K:
                term = load(SUBLANES * a, rows + SUBLANES) * w_ref[k:k + 1, :]
                part = term if part is None else part + term
        if part is not None:
            shifted = lax.slice_in_dim(part, b, b + rows, axis=time_axis)
            acc = shifted if acc is None else acc + shifted
    return acc


def _conv_prompt_body(u_ref, w_ref, cb_ref, lg_ref, lb_ref, c_ref, st_ref, ext_ref):
    j = pl.program_id(1)
    tl = u_ref.shape[1]

    @pl.when(j == 0)
    def _():
        ext_ref[0:HALO, :] = jnp.zeros((HALO, CONV_CH), F32)
        ext_ref[tl + HALO:, :] = jnp.zeros((SUBLANES, CONV_CH), F32)

    @pl.when(j > 0)
    def _():
        ext_ref[0:HALO, :] = ext_ref[tl:tl + HALO, :]

    ext_ref[HALO:tl + HALO, :] = u_ref[0]
    acc = _dwconv(lambda s, n: ext_ref[s:s + n, :], w_ref, tl, 0) + cb_ref[...]
    c_ref[0] = _ln_silu(acc, lg_ref[...], lb_ref[...]).astype(c_ref.dtype)
    st_ref[0] = ext_ref[tl + HALO_OFF:tl + HALO, :]


def _conv_prompt(u, conv_w, conv_b, ln_g, ln_b, tl):
    n, l, _ = u.shape
    const = lambda b, j: (0, 0)
    return pl.pallas_call(
        _conv_prompt_body,
        grid=(n, l // tl),
        in_specs=[
            pl.BlockSpec((1, tl, CONV_CH), lambda b, j: (b, j, 0)),
            pl.BlockSpec((CONV_K, CONV_CH), const),
            pl.BlockSpec((1, CONV_CH), const),
            pl.BlockSpec((1, CONV_CH), const),
            pl.BlockSpec((1, CONV_CH), const),
        ],
        out_specs=[
            pl.BlockSpec((1, tl, CONV_CH), lambda b, j: (b, j, 0)),
            pl.BlockSpec((1, CONV_K - 1, CONV_CH), lambda b, j: (b, 0, 0)),
        ],
        out_shape=[
            jax.ShapeDtypeStruct((n, l, CONV_CH), BF16),
            jax.ShapeDtypeStruct((n, CONV_K - 1, CONV_CH), F32),
        ],
        scratch_shapes=[pltpu.VMEM((tl + HALO + SUBLANES, CONV_CH), F32)],
        compiler_params=_cparams(("arbitrary", "arbitrary")),
        name="conv_prompt",
    )(u, conv_w, conv_b, ln_g, ln_b)


def _conv_sample_body(u_ref, st_ref, w_ref, cb_ref, lg_ref, lb_ref, c_ref, nst_ref, ext_ref):
    nb, l, _ = u_ref.shape
    ext_ref[:, 0:HALO_OFF, :] = jnp.zeros((nb, HALO_OFF, CONV_CH), F32)
    ext_ref[:, HALO_OFF:HALO, :] = st_ref[...]
    ext_ref[:, HALO:l + HALO, :] = u_ref[...]
    ext_ref[:, l + HALO:, :] = jnp.zeros((nb, SUBLANES, CONV_CH), F32)
    acc = _dwconv(lambda s, n: ext_ref[:, s:s + n, :], w_ref, l, 1) + cb_ref[...]
    c_ref[...] = _ln_silu(acc, lg_ref[...], lb_ref[...]).astype(c_ref.dtype)
    nst_ref[...] = ext_ref[:, l + HALO_OFF:l + HALO, :]


def _conv_sample(u, state, conv_w, conv_b, ln_g, ln_b, nb):
    n, l, _ = u.shape
    const = lambda b: (0, 0)
    blk3 = lambda b: (b, 0, 0)
    return pl.pallas_call(
        _conv_sample_body,
        grid=(n // nb,),
        in_specs=[
            pl.BlockSpec((nb, l, CONV_CH), blk3),
            pl.BlockSpec((nb, CONV_K - 1, CONV_CH), blk3),
            pl.BlockSpec((CONV_K, CONV_CH), const),
            pl.BlockSpec((1, CONV_CH), const),
            pl.BlockSpec((1, CONV_CH), const),
            pl.BlockSpec((1, CONV_CH), const),
        ],
        out_specs=[
            pl.BlockSpec((nb, l, CONV_CH), blk3),
            pl.BlockSpec((nb, CONV_K - 1, CONV_CH), blk3),
        ],
        out_shape=[
            jax.ShapeDtypeStruct((n, l, CONV_CH), BF16),
            jax.ShapeDtypeStruct((n, CONV_K - 1, CONV_CH), F32),
        ],
        scratch_shapes=[pltpu.VMEM((nb, l + HALO + SUBLANES, CONV_CH), F32)],
        compiler_params=_cparams(("parallel",)),
        name="conv_sample",
    )(u, state, conv_w, conv_b, ln_g, ln_b)


def _decay_tables(c):
    lg = jnp.log(1.0 - 2.0 ** (-5.0 - jnp.arange(RET_HEADS, dtype=F32)))
    idx = jnp.arange(c, dtype=F32)
    rel = idx[:, None] - idx[None, :]
    dmat = jnp.where(rel[None] >= 0, jnp.exp(jnp.maximum(rel, 0.0)[None] * lg[:, None, None]), 0.0)
    xi = jnp.exp((idx + 1.0)[None, :] * lg[:, None])
    zeta = jnp.exp((c - 1.0 - idx)[None, :] * lg[:, None])
    gc = jnp.exp(c * lg)
    xi_b = jnp.broadcast_to(xi[:, :, None], (RET_HEADS, c, HEAD_DIM))
    zeta_b = jnp.broadcast_to(zeta[:, :, None], (RET_HEADS, c, HEAD_DIM))
    gc_b = jnp.broadcast_to(gc[:, None, None], (RET_HEADS, 1, HEAD_DIM))
    return dmat, xi_b, zeta_b, gc_b


def _group_norm(o):
    mu = jnp.mean(o, axis=-1, keepdims=True)
    d = o - mu
    var = jnp.mean(d * d, axis=-1, keepdims=True)
    return d * lax.rsqrt(var + EPS)


def _ret_chunk(qh, kh, vh, r, dmat, xi, zeta, gc):
    qb = qh.astype(BF16)
    kb = kh.astype(BF16)
    vb = vh.astype(BF16)
    s = lax.dot_general(qb, kb, (((1,), (1,)), ((), ())), preferred_element_type=F32) * dmat
    o = jnp.dot(s.astype(BF16), vb, preferred_element_type=F32)
    o = o + jnp.dot(qb, r.astype(BF16), preferred_element_type=F32) * xi
    kz = (kh.astype(F32) * zeta).astype(BF16)
    r_new = r * gc + lax.dot_general(kz, vb, (((0,), (0,)), ((), ())), preferred_element_type=F32)
    return o, r_new


def _ret_prompt_body(q_ref, k_ref, v_ref, gs_ref, d_ref, xi_ref, zeta_ref, gc_ref, o_ref, st_ref, r_ref):
    j = pl.program_id(1)

    @pl.when(j == 0)
    def _():
        r_ref[...] = jnp.zeros_like(r_ref)

    n_chunks = q_ref.shape[0] // CHUNK
    for hh in range(RET_HEADS):
        sl = slice(hh * HEAD_DIM, (hh + 1) * HEAD_DIM)
        r = r_ref[hh]
        for ci in range(n_chunks):
            rows = slice(ci * CHUNK, (ci + 1) * CHUNK)
            o, r = _ret_chunk(q_ref[rows, sl], k_ref[rows, sl], v_ref[rows, sl], r,
                              d_ref[hh], xi_ref[hh], zeta_ref[hh], gc_ref[hh])
            o_ref[rows, sl] = (gs_ref[rows, sl] * _group_norm(o)).astype(o_ref.dtype)
        r_ref[hh] = r
    st_ref[0] = r_ref[...]


def _ret_prompt(q, k, v, gs, n, l, tl):
    dmat, xi, zeta, gc = _decay_tables(CHUNK)
    per = l // tl
    row = lambda b, j: (b * per + j, 0)
    c3 = lambda b, j: (0, 0, 0)
    return pl.pallas_call(
        _ret_prompt_body,
        grid=(n, per),
        in_specs=[pl.BlockSpec((tl, RET_WIDTH), row)] * 4 + [
            pl.BlockSpec((RET_HEADS, CHUNK, CHUNK), c3),
            pl.BlockSpec((RET_HEADS, CHUNK, HEAD_DIM), c3),
            pl.BlockSpec((RET_HEADS, CHUNK, HEAD_DIM), c3),
            pl.BlockSpec((RET_HEADS, 1, HEAD_DIM), c3),
        ],
        out_specs=[
            pl.BlockSpec((tl, RET_WIDTH), row),
            pl.BlockSpec((1, RET_HEADS, HEAD_DIM, HEAD_DIM), lambda b, j: (b, 0, 0, 0)),
        ],
        out_shape=[
            jax.ShapeDtypeStruct((n * l, RET_WIDTH), BF16),
            jax.ShapeDtypeStruct((n, RET_HEADS, HEAD_DIM, HEAD_DIM), F32),
        ],
        scratch_shapes=[pltpu.VMEM((RET_HEADS, HEAD_DIM, HEAD_DIM), F32)],
        compiler_params=_cparams(("arbitrary", "arbitrary")),
        name="ret_prompt",
    )(q, k, v, gs, dmat, xi, zeta, gc)


def _ret_sample_body(q_ref, k_ref, v_ref, gs_ref, st_ref, d_ref, xi_ref, zeta_ref, gc_ref, o_ref, nst_ref):
    nb = st_ref.shape[0]
    l = q_ref.shape[0] // nb
    for b in range(nb):
        rows = slice(b * l, (b + 1) * l)
        for hh in range(RET_HEADS):
            sl = slice(hh * HEAD_DIM, (hh + 1) * HEAD_DIM)
            o, r = _ret_chunk(q_ref[rows, sl], k_ref[rows, sl], v_ref[rows, sl], st_ref[b, hh],
                              d_ref[hh], xi_ref[hh], zeta_ref[hh], gc_ref[hh])
            o_ref[rows, sl] = (gs_ref[rows, sl] * _group_norm(o)).astype(o_ref.dtype)
            nst_ref[b, hh] = r


def _ret_sample(q, k, v, gs, state, l, nb):
    n = state.shape[0]
    dmat, xi, zeta, gc = _decay_tables(l)
    row = lambda b: (b, 0)
    c3 = lambda b: (0, 0, 0)
    blk4 = lambda b: (b, 0, 0, 0)
    return pl.pallas_call(
        _ret_sample_body,
        grid=(n // nb,),
        in_specs=[pl.BlockSpec((nb * l, RET_WIDTH), row)] * 4 + [
            pl.BlockSpec((nb, RET_HEADS, HEAD_DIM, HEAD_DIM), blk4),
            pl.BlockSpec((RET_HEADS, l, l), c3),
            pl.BlockSpec((RET_HEADS, l, HEAD_DIM), c3),
            pl.BlockSpec((RET_HEADS, l, HEAD_DIM), c3),
            pl.BlockSpec((RET_HEADS, 1, HEAD_DIM), c3),
        ],
        out_specs=[
            pl.BlockSpec((nb * l, RET_WIDTH), row),
            pl.BlockSpec((nb, RET_HEADS, HEAD_DIM, HEAD_DIM), blk4),
        ],
        out_shape=[
            jax.ShapeDtypeStruct((n * l, RET_WIDTH), BF16),
            jax.ShapeDtypeStruct((n, RET_HEADS, HEAD_DIM, HEAD_DIM), F32),
        ],
        compiler_params=_cparams(("parallel",)),
        name="ret_sample",
    )(q, k, v, gs, state, dmat, xi, zeta, gc)


def _split3(x):
    hi = x.astype(BF16)
    r1 = x - hi.astype(F32)
    mid = r1.astype(BF16)
    lo = (r1 - mid.astype(F32)).astype(BF16)
    return hi, mid, lo


def _dot_hp(t, w2_ref):
    t_hi, t_mid, _ = _split3(t)
    d = functools.partial(jnp.dot, preferred_element_type=F32)
    both = d(t_hi, w2_ref[...])
    return both[:, :LANES] + (d(t_mid, w2_ref[:, 0:LANES]) + both[:, LANES:])


def _route(logits):
    lt = logits.T
    tm = lt.shape[1]
    row = _iota_f32((SUBLANES, tm), 0)
    big = float(SUBLANES)

    def rmax(x):
        return jnp.max(x, axis=0, keepdims=True)

    def first_row(mask):
        return jnp.min(jnp.where(mask, row, big), axis=0, keepdims=True)

    lg = jnp.where(row < float(N_GROUPS), lt[N_EXPERTS:N_EXPERTS + SUBLANES, :], -1e30)
    m = rmax(lg)
    g_top = 1.0 / jnp.sum(jnp.exp(lg - m), axis=0, keepdims=True)
    g_idx = first_row(lg == m)
    lem = lt[0:EXPERTS_PER_GROUP, :]
    for g in range(1, N_GROUPS):
        lem = jnp.where(g_idx == float(g), lt[g * EXPERTS_PER_GROUP:(g + 1) * EXPERTS_PER_GROUP, :], lem)
    pe = jnp.exp(lem - rmax(lem))
    p1 = rmax(pe)
    e1 = first_row(pe == p1)
    rest = row != e1
    pe2 = jnp.where(rest, pe, -1.0)
    p2 = rmax(pe2)
    e2 = first_row(rest & (pe2 == p2))
    scale = g_top / (p1 + p2)
    base = g_idx * float(EXPERTS_PER_GROUP)
    rec_t = jnp.where(row == 0.0, base + e1, jnp.where(row == 1.0, base + e2, 0.0))
    rec_t = rec_t + jnp.where(row == 2.0, p1 * scale, jnp.where(row == 3.0, p2 * scale, 0.0))
    return jnp.concatenate([rec_t, jnp.zeros((LANES - SUBLANES, tm), F32)], axis=0).T


def _outproj_body(c_ref, o_ref, x_ref, wo_ref, g2_ref, wr_ref, br_ref, x1_ref, t_ref, rec_ref):
    x1 = x_ref[...] + jnp.dot(c_ref[...], wo_ref[0:CONV_CH, :], preferred_element_type=F32)
    x1 = x1 + jnp.dot(o_ref[...], wo_ref[CONV_CH:, :], preferred_element_type=F32)
    x1_ref[...] = x1
    t = _rms(x1, g2_ref[...])
    t_ref[...] = t.astype(t_ref.dtype)
    rec_ref[...] = _route(_dot_hp(t, wr_ref) + br_ref[...])


def _outproj(c, o, x, w_out, g2, wr2, br, tm):
    t = x.shape[0]
    row = lambda i: (i, 0)
    const = lambda i: (0, 0)
    c3 = lambda i: (0, 0, 0)
    return pl.pallas_call(
        _outproj_body,
        grid=(t // tm,),
        in_specs=[
            pl.BlockSpec((tm, CONV_CH), row),
            pl.BlockSpec((tm, RET_WIDTH), row),
            pl.BlockSpec((tm, D_MODEL), row),
            pl.BlockSpec((D_MODEL, D_MODEL), const),
            pl.BlockSpec((1, D_MODEL), const),
            pl.BlockSpec((D_MODEL, 2 * LANES), const),
            pl.BlockSpec((1, LANES), const),
        ],
        out_specs=[
            pl.BlockSpec((tm, D_MODEL), row),
            pl.BlockSpec((tm, D_MODEL), row),
            pl.BlockSpec((tm, LANES), row),
        ],
        out_shape=[
            jax.ShapeDtypeStruct((t, D_MODEL), F32),
            jax.ShapeDtypeStruct((t, D_MODEL), BF16),
            jax.ShapeDtypeStruct((t, LANES), F32),
        ],
        compiler_params=_cparams(("parallel",)),
        name="outproj_router",
    )(c, o, x, w_out, g2, wr2, br)


def _mix_body(x_ref, g1_ref, w_ref, cos_ref, sin_ref, cw_ref, cb_ref, lg_ref, lb_ref,
              d_ref, xi_ref, zeta_ref, gc_ref, wo_ref, g2_ref, wr_ref, br_ref,
              x1_ref, t_ref, rec_ref, cst_ref, rst_ref, ext_ref, r_ref, o_ref):
    j = pl.program_id(1)
    tl = x_ref.shape[0]
    x = x_ref[...]
    z = jnp.dot(_rms(x, g1_ref[...]).astype(BF16), w_ref[...], preferred_element_type=F32)

    @pl.when(j == 0)
    def _():
        ext_ref[0:HALO, :] = jnp.zeros((HALO, CONV_CH), F32)
        ext_ref[tl + HALO:, :] = jnp.zeros((SUBLANES, CONV_CH), F32)
        r_ref[...] = jnp.zeros_like(r_ref)

    @pl.when(j > 0)
    def _():
        ext_ref[0:HALO, :] = ext_ref[tl:tl + HALO, :]

    ext_ref[HALO:tl + HALO, :] = z[:, :CONV_CH] * jax.nn.sigmoid(z[:, CONV_CH:2 * CONV_CH])
    acc = _dwconv(lambda s, n: ext_ref[s:s + n, :], cw_ref, tl, 0) + cb_ref[...]
    c = _ln_silu(acc, lg_ref[...], lb_ref[...]).astype(BF16)
    cst_ref[0] = ext_ref[tl + HALO_OFF:tl + HALO, :]

    cos = cos_ref[...]
    sin = sin_ref[...]
    q0 = 2 * CONV_CH
    k0 = q0 + RET_WIDTH
    v0 = k0 + RET_WIDTH
    g0 = v0 + RET_WIDTH
    for hh in range(RET_HEADS):
        lo = hh * HEAD_DIM
        qh = z[:, q0 + lo:q0 + lo + HEAD_DIM]
        kh = z[:, k0 + lo:k0 + lo + HEAD_DIM]
        qr = (qh * cos + pltpu.roll(qh, HEAD_DIM // 2, 1) * sin).astype(BF16)
        kr = ((kh * cos + pltpu.roll(kh, HEAD_DIM // 2, 1) * sin) * (HEAD_DIM ** -0.5)).astype(BF16)
        vh = z[:, v0 + lo:v0 + lo + HEAD_DIM].astype(BF16)
        g = z[:, g0 + lo:g0 + lo + HEAD_DIM]
        gs = g * jax.nn.sigmoid(g)
        r = r_ref[hh]
        for ci in range(tl // CHUNK):
            rows = slice(ci * CHUNK, (ci + 1) * CHUNK)
            o, r = _ret_chunk(qr[rows], kr[rows], vh[rows], r, d_ref[hh], xi_ref[hh], zeta_ref[hh], gc_ref[hh])
            o_ref[rows, lo:lo + HEAD_DIM] = (gs[rows] * _group_norm(o)).astype(BF16)
        r_ref[hh] = r
    rst_ref[0] = r_ref[...]

    x1 = x + jnp.dot(c, wo_ref[0:CONV_CH, :], preferred_element_type=F32)
    x1 = x1 + jnp.dot(o_ref[...], wo_ref[CONV_CH:, :], preferred_element_type=F32)
    x1_ref[...] = x1
    t = _rms(x1, g2_ref[...])
    t_ref[...] = t.astype(t_ref.dtype)
    rec_ref[...] = _route(_dot_hp(t, wr_ref) + br_ref[...])


def _mix(x, g1, w_in, cos, sin, conv_w, conv_b, ln_g, ln_b, w_out, g2, wr2, br, n, l, tl):
    dmat, xi, zeta, gc = _decay_tables(CHUNK)
    per = l // tl
    row = lambda b, j: (b * per + j, 0)
    tab = lambda b, j: (j, 0)
    const = lambda b, j: (0, 0)
    c3 = lambda b, j: (0, 0, 0)
    once = dict(pipeline_mode=pl.Buffered(1))
    return pl.pallas_call(
        _mix_body,
        grid=(n, per),
        in_specs=[
            pl.BlockSpec((tl, D_MODEL), row),
            pl.BlockSpec((1, D_MODEL), const),
            pl.BlockSpec((D_MODEL, IN_COLS), const, **once),
            pl.BlockSpec((tl, HEAD_DIM), tab),
            pl.BlockSpec((tl, HEAD_DIM), tab),
            pl.BlockSpec((CONV_K, CONV_CH), const),
            pl.BlockSpec((1, CONV_CH), const),
            pl.BlockSpec((1, CONV_CH), const),
            pl.BlockSpec((1, CONV_CH), const),
            pl.BlockSpec((RET_HEADS, CHUNK, CHUNK), c3),
            pl.BlockSpec((RET_HEADS, CHUNK, HEAD_DIM), c3),
            pl.BlockSpec((RET_HEADS, CHUNK, HEAD_DIM), c3),
            pl.BlockSpec((RET_HEADS, 1, HEAD_DIM), c3),
            pl.BlockSpec((D_MODEL, D_MODEL), const, **once),
            pl.BlockSpec((1, D_MODEL), const),
            pl.BlockSpec((D_MODEL, 2 * LANES), const, **once),
            pl.BlockSpec((1, LANES), const),
        ],
        out_specs=[
            pl.BlockSpec((tl, D_MODEL), row),
            pl.BlockSpec((tl, D_MODEL), row),
            pl.BlockSpec((tl, LANES), row),
            pl.BlockSpec((1, CONV_K - 1, CONV_CH), lambda b, j: (b, 0, 0)),
            pl.BlockSpec((1, RET_HEADS, HEAD_DIM, HEAD_DIM), lambda b, j: (b, 0, 0, 0)),
        ],
        out_shape=[
            jax.ShapeDtypeStruct((n * l, D_MODEL), F32),
            jax.ShapeDtypeStruct((n * l, D_MODEL), BF16),
            jax.ShapeDtypeStruct((n * l, LANES), F32),
            jax.ShapeDtypeStruct((n, CONV_K - 1, CONV_CH), F32),
            jax.ShapeDtypeStruct((n, RET_HEADS, HEAD_DIM, HEAD_DIM), F32),
        ],
        scratch_shapes=[
            pltpu.VMEM((tl + HALO + SUBLANES, CONV_CH), F32),
            pltpu.VMEM((RET_HEADS, HEAD_DIM, HEAD_DIM), F32),
            pltpu.VMEM((tl, RET_WIDTH), BF16),
        ],
        compiler_params=pltpu.CompilerParams(dimension_semantics=("arbitrary", "arbitrary"),
                                             vmem_limit_bytes=MIX_VMEM_LIMIT),
        name="token_mix",
    )(x, g1, w_in, cos, sin, conv_w, conv_b, ln_g, ln_b, dmat, xi, zeta, gc, w_out, g2, wr2, br)


def _iota_f32(shape, dim):
    return lax.broadcasted_iota(jnp.int32, shape, dim).astype(F32)


def _dispatch_body(rec_a_ref, t_a_ref, rec_b_ref, t_b_ref, s_ref, pos_ref, meta_ref, *, nsub_a):
    from_a = pl.program_id(0) * DISPATCH_SUBS < nsub_a
    for s in range(DISPATCH_SUBS):
        rows = slice(s * SUB, (s + 1) * SUB)
        rec = jnp.where(from_a, rec_a_ref[rows, :], rec_b_ref[rows, :])
        tok = jnp.where(from_a, t_a_ref[rows, :], t_b_ref[rows, :])
        chunks = slice(s * CHUNKS_PER_SUB, (s + 1) * CHUNKS_PER_SUB)
        _dispatch_sub_tile(rec, tok, s_ref.at[chunks], pos_ref.at[rows], meta_ref.at[s])


def _dispatch_sub_tile(rec, tok, s_ref, pos_ref, meta_ref):
    lane = _iota_f32(rec.shape, 1)
    a1 = lane == rec[:, 0:1]
    a2 = lane == rec[:, 1:2]
    a1f = jnp.where(a1, 1.0, 0.0)
    a2f = jnp.where(a2, 1.0, 0.0)
    ltri = jnp.where(_iota_f32((SUB, SUB), 1) < _iota_f32((SUB, SUB), 0), 1.0, 0.0).astype(BF16)
    c1 = jnp.dot(ltri, a1f.astype(BF16), preferred_element_type=F32)
    c2 = jnp.dot(ltri, a2f.astype(BF16), preferred_element_type=F32)
    n1 = jnp.sum(a1f, axis=0, keepdims=True)
    n2 = jnp.sum(a2f, axis=0, keepdims=True)
    cnt = jnp.floor((n1 + n2 + (ROW_ALIGN - 1.0)) * (1.0 / ROW_ALIGN))
    utri = jnp.where(_iota_f32((LANES, LANES), 0) < _iota_f32((LANES, LANES), 1), 1.0, 0.0).astype(BF16)
    start = jnp.dot(jnp.broadcast_to(cnt, (SUBLANES, LANES)).astype(BF16), utri,
                    preferred_element_type=F32)[0:1]
    base1 = start * ROW_ALIGN
    base2 = base1 + n1
    pos1 = jnp.sum(jnp.where(a1, c1 + base1, 0.0), axis=1, keepdims=True)
    pos2 = jnp.sum(jnp.where(a2, c2 + base2, 0.0), axis=1, keepdims=True)
    posm = jnp.where(lane == 0.0, pos1, jnp.where(lane == 1.0, pos2, 0.0))
    pos_ref[...] = posm
    row = _iota_f32((SUBLANES, LANES), 0)
    meta_ref[...] = jnp.where(row == 0.0, start, jnp.where(row == 1.0, cnt, 0.0))

    g1 = _split3(rec[:, 2:3])
    g2 = _split3(rec[:, 3:4])
    info = jnp.where(lane == 6.0, rec[:, 0:1], jnp.where(lane == 7.0, rec[:, 1:2], 0.0))
    for i in range(3):
        info = jnp.where(lane == float(i), g1[i].astype(F32), info)
        info = jnp.where(lane == float(3 + i), g2[i].astype(F32), info)
    src = jnp.concatenate([tok, info.astype(BF16)], axis=1)

    post = posm.T
    r = _iota_f32((SUBP, SUB), 0)
    onehot = jnp.where(r == post[0:1, :], 1.0, jnp.where(r == post[1:2, :], 1.0, 0.0)).astype(BF16)
    sorted_rows = jnp.dot(onehot, src, preferred_element_type=F32).astype(BF16)
    s_ref[...] = sorted_rows.reshape(CHUNKS_PER_SUB, ROW_ALIGN, ROW_W)


def _dispatch(rec_a, t_a, rec_b, t_b):
    nsub_a = rec_a.shape[0] // SUB
    nsub_b = rec_b.shape[0] // SUB
    nsub = nsub_a + nsub_b
    assert nsub_a % DISPATCH_SUBS == 0 and nsub_b % DISPATCH_SUBS == 0
    steps_a = nsub_a // DISPATCH_SUBS
    tm = DISPATCH_SUBS * SUB
    row = lambda i: (i, 0)
    row_a = lambda i: (jnp.minimum(i, steps_a - 1), 0)
    row_b = lambda i: (jnp.maximum(i - steps_a, 0), 0)
    return pl.pallas_call(
        functools.partial(_dispatch_body, nsub_a=nsub_a),
        grid=(nsub // DISPATCH_SUBS,),
        in_specs=[
            pl.BlockSpec((tm, LANES), row_a),
            pl.BlockSpec((tm, D_MODEL), row_a),
            pl.BlockSpec((tm, LANES), row_b),
            pl.BlockSpec((tm, D_MODEL), row_b),
        ],
        out_specs=[
            pl.BlockSpec((DISPATCH_SUBS * CHUNKS_PER_SUB, ROW_ALIGN, ROW_W), lambda i: (i, 0, 0)),
            pl.BlockSpec((tm, LANES), row),
            pl.BlockSpec((DISPATCH_SUBS, SUBLANES, LANES), lambda i: (i, 0, 0)),
        ],
        out_shape=[
            jax.ShapeDtypeStruct((nsub * CHUNKS_PER_SUB, ROW_ALIGN, ROW_W), BF16),
            jax.ShapeDtypeStruct((nsub * SUB, LANES), F32),
            jax.ShapeDtypeStruct((nsub, SUBLANES, LANES), F32),
        ],
        compiler_params=_cparams(("parallel",)),
        name="moe_dispatch",
    )(rec_a, t_a, rec_b, t_b)


def _experts_body(start_ref, cnt_ref, s_in, wg_ref, wu_ref, wd_ref, s_hbm,
                  xbuf, ybuf, wgu_ref, wdb_ref, gsem, ssem, list_ref, state_ref, *, nsub):
    del s_in
    e = pl.program_id(0)
    ne = pl.num_programs(0)
    par = e & 1

    list_max = list_ref.shape[0] // 2

    def gather_copy(src, i, slot):
        return pltpu.make_async_copy(s_hbm.at[src], xbuf.at[slot, i], gsem.at[slot])

    def scatter_copy(dst, i, slot):
        return pltpu.make_async_copy(ybuf.at[slot, i], s_hbm.at[dst], ssem.at[slot])

    def build_list(x, which):
        def per_sub(s, k):
            run = s * N_EXPERTS + x
            c = cnt_ref[run]
            base = s * CHUNKS_PER_SUB + start_ref[run]
            list_ref[k] = base
            list_ref[k + 1] = base + 1

            def per_chunk(i, carry):
                list_ref[k + i] = base + i
                return carry
            lax.fori_loop(2, c, per_chunk, 0)
            return k + c
        first = which * list_max
        state_ref[which] = lax.fori_loop(0, nsub, per_sub, first) - first

    def start_all(copy, which, first, n, slot, counter):
        priority = 1 if copy is scatter_copy else 0

        def body(i, carry):
            copy(list_ref[which * list_max + first + i], i, slot).start(priority=priority)
            return carry
        lax.fori_loop(0, n, body, 0)
        state_ref[counter] = n

    def wait_all(copy, block_copy, slot, counter):
        n = state_ref[counter]

        @pl.when(n == CHUNKS_PER_BLK)
        def _():
            block_copy(slot).wait()

        @pl.when(n < CHUNKS_PER_BLK)
        def _():
            def body(i, carry):
                copy(0, 0, slot).wait()
                return carry
            lax.fori_loop(0, n, body, 0)
        state_ref[counter] = 0

    def gather_block(slot):
        return pltpu.make_async_copy(s_hbm.at[pl.ds(0, CHUNKS_PER_BLK)], xbuf.at[slot], gsem.at[slot])

    def scatter_block(slot):
        return pltpu.make_async_copy(ybuf.at[slot], s_hbm.at[pl.ds(0, CHUNKS_PER_BLK)], ssem.at[slot])

    def block_chunks(total, b):
        return jnp.minimum(total - b * CHUNKS_PER_BLK, CHUNKS_PER_BLK)

    @pl.when(e == 0)
    def _():
        for i in range(6):
            state_ref[i] = 0
        xbuf[...] = jnp.zeros_like(xbuf)
        build_list(0, 0)
        n0 = state_ref[0]

        @pl.when(n0 > 0)
        def _():
            start_all(gather_copy, 0, 0, block_chunks(n0, 0), 0, 2)

    @pl.when(e + 1 < ne)
    def _():
        build_list(e + 1, 1 - par)

    total = state_ref[par]
    nblk = (total + CHUNKS_PER_BLK - 1) // CHUNKS_PER_BLK
    wgu_ref[:, 0:EXPERT_FF] = wg_ref[0].astype(BF16)
    wgu_ref[:, EXPERT_FF:] = wu_ref[0].astype(BF16)
    wdb_ref[...] = wd_ref[0].astype(BF16)
    ef = e.astype(F32)

    def block(b, carry):
        slot = b & 1
        first = b * CHUNKS_PER_BLK

        @pl.when(b + 1 < nblk)
        def _():
            start_all(gather_copy, par, first + CHUNKS_PER_BLK, block_chunks(total, b + 1), 1 - slot, 3 - slot)

        wait_all(gather_copy, gather_block, slot, 2 + slot)
        wait_all(scatter_copy, scatter_block, slot, 4 + slot)
        x = xbuf[slot].reshape(MBLK, ROW_W)
        info = x[:, D_MODEL:].astype(F32)
        g_first = info[:, 0:1] + info[:, 1:2] + info[:, 2:3]
        g_second = info[:, 3:4] + info[:, 4:5] + info[:, 5:6]
        gate = jnp.where(info[:, 6:7] == ef, g_first, g_second)
        h = jnp.dot(x[:, :D_MODEL], wgu_ref[...], preferred_element_type=F32)
        h1 = h[:, :EXPERT_FF]
        hid = (h1 * jax.nn.sigmoid(h1)) * h[:, EXPERT_FF:] * gate
        y = jnp.dot(hid.astype(BF16), wdb_ref[...], preferred_element_type=F32).astype(BF16)
        ybuf[slot] = jnp.concatenate([y, x[:, D_MODEL:]], axis=1).reshape(CHUNKS_PER_BLK, ROW_ALIGN, ROW_W)
        start_all(scatter_copy, par, first, block_chunks(total, b), slot, 4 + slot)
        return carry

    lax.fori_loop(0, nblk, block, 0)

    @pl.when(e + 1 < ne)
    def _():
        n1 = state_ref[1 - par]

        @pl.when(n1 > 0)
        def _():
            start_all(gather_copy, 1 - par, 0, block_chunks(n1, 0), 0, 2)

    @pl.when(e == ne - 1)
    def _():
        wait_all(scatter_copy, scatter_block, 0, 4)
        wait_all(scatter_copy, scatter_block, 1, 5)


def _experts(start, cnt, staged, wg, wu, wd):
    nsub = staged.shape[0] // CHUNKS_PER_SUB
    list_max = nsub * SUB // ROW_ALIGN + nsub + LIST_SLACK
    wblk = lambda e, *_: (e, 0, 0)
    grid_spec = pltpu.PrefetchScalarGridSpec(
        num_scalar_prefetch=2,
        grid=(N_EXPERTS,),
        in_specs=[
            pl.BlockSpec(memory_space=pl.ANY),
            pl.BlockSpec((1, D_MODEL, EXPERT_FF), wblk),
            pl.BlockSpec((1, D_MODEL, EXPERT_FF), wblk),
            pl.BlockSpec((1, EXPERT_FF, D_MODEL), wblk),
        ],
        out_specs=pl.BlockSpec(memory_space=pl.ANY),
        scratch_shapes=[
            pltpu.VMEM((2, CHUNKS_PER_BLK, ROW_ALIGN, ROW_W), BF16),
            pltpu.VMEM((2, CHUNKS_PER_BLK, ROW_ALIGN, ROW_W), BF16),
            pltpu.VMEM((D_MODEL, 2 * EXPERT_FF), BF16),
            pltpu.VMEM((EXPERT_FF, D_MODEL), BF16),
            pltpu.SemaphoreType.DMA((2,)),
            pltpu.SemaphoreType.DMA((2,)),
            pltpu.SMEM((2 * list_max,), jnp.int32),
            pltpu.SMEM((6,), jnp.int32),
        ],
    )
    return pl.pallas_call(
        functools.partial(_experts_body, nsub=nsub),
        grid_spec=grid_spec,
        out_shape=jax.ShapeDtypeStruct(staged.shape, staged.dtype),
        input_output_aliases={2: 0},
        compiler_params=_cparams(("arbitrary",)),
        name="moe_experts",
    )(start, cnt, staged, wg, wu, wd)


def _combine_body(ys_ref, pos_ref, x1_ref, p_ref, wp_ref, gp_ref, wpg_ref, gf_ref, y_ref):
    r = _iota_f32((SUB, SUBP), 1)
    moe = []
    for s in range(x1_ref.shape[0] // SUB):
        p1 = pos_ref[s * SUB:(s + 1) * SUB, 0:1]
        p2 = pos_ref[s * SUB:(s + 1) * SUB, 1:2]
        onehot = jnp.where(r == p1, 1.0, jnp.where(r == p2, 1.0, 0.0)).astype(BF16)
        ys = ys_ref[s * CHUNKS_PER_SUB:(s + 1) * CHUNKS_PER_SUB].reshape(SUBP, D_MODEL)
        moe.append(jnp.dot(onehot, ys, preferred_element_type=F32))
    x2 = x1_ref[...] + jnp.concatenate(moe, axis=0)
    ple = _rms(jnp.dot(p_ref[...].astype(BF16), wp_ref[...], preferred_element_type=F32), gp_ref[...])
    gate = jax.nn.sigmoid(jnp.dot(x2.astype(BF16), wpg_ref[...], preferred_element_type=F32))
    y_ref[...] = _rms(x2 + ple * gate, gf_ref[...])


def _combine(ys, pos, x1, p, w_ple, gp, w_ple_gate, gf, sub_off):
    t = x1.shape[0]
    tm = COMBINE_SUBS * SUB
    blk_off = sub_off // COMBINE_SUBS
    assert sub_off % COMBINE_SUBS == 0 and t % tm == 0
    row = lambda i: (i, 0)
    const = lambda i: (0, 0)
    return pl.pallas_call(
        _combine_body,
        grid=(t // tm,),
        in_specs=[
            pl.BlockSpec((COMBINE_SUBS * CHUNKS_PER_SUB, ROW_ALIGN, D_MODEL), lambda i: (i + blk_off, 0, 0)),
            pl.BlockSpec((tm, LANES), lambda i: (i + blk_off, 0)),
            pl.BlockSpec((tm, D_MODEL), row),
            pl.BlockSpec((tm, PLE_DIM), row),
            pl.BlockSpec((PLE_DIM, D_MODEL), const),
            pl.BlockSpec((1, D_MODEL), const),
            pl.BlockSpec((D_MODEL, D_MODEL), const),
            pl.BlockSpec((1, D_MODEL), const),
        ],
        out_specs=pl.BlockSpec((tm, D_MODEL), row),
        out_shape=jax.ShapeDtypeStruct((t, D_MODEL), F32),
        compiler_params=_cparams(("parallel",)),
        name="moe_combine_ple",
    )(ys, pos, x1, p, w_ple, gp, w_ple_gate, gf)


def _rope_tables(pos):
    half = HEAD_DIM // 2
    inv = ROPE_BASE ** (-jnp.arange(half, dtype=F32) / half)
    ang = pos[:, None] * inv[None, :]
    cos = jnp.cos(ang)
    sin = jnp.sin(ang)
    return jnp.concatenate([cos, cos], axis=-1), jnp.concatenate([-sin, sin], axis=-1)


def _router_params(we, be, wg, bg):
    pad = LANES - N_EXPERTS - N_GROUPS
    w = jnp.pad(jnp.concatenate([we, wg], axis=1), ((0, 0), (0, pad)))
    b = jnp.pad(jnp.concatenate([be, bg]), (0, pad))[None, :]
    return jnp.concatenate(_split3(w)[:2], axis=1), b


def kernel(x_prompt, x_sample, p_prompt, p_sample, state_conv, state_ret, w_in, conv_w, conv_b, conv_ln_g, conv_ln_b, w_out, norm1_g, norm2_g, router_group_w, router_group_b, router_expert_w, router_expert_b, w_expert_gate, w_expert_up, w_expert_down, w_ple, ple_norm_g, w_ple_gate, final_norm_g):
    assert w_in.shape[0] == 1, "single-layer trunk"
    nb, seq, _ = x_prompt.shape
    ns, dseq, _ = x_sample.shape
    tm = 512

    w_in_b = w_in[0].astype(BF16)
    w_out_b = w_out[0].astype(BF16)
    w_ple_b = w_ple[0].astype(BF16)
    w_pg_b = w_ple_gate[0].astype(BF16)
    g1 = norm1_g[0][None, :]
    g2 = norm2_g[0][None, :]
    gp = ple_norm_g[0][None, :]
    gf = final_norm_g[None, :]
    cb = conv_b[0][None, :]
    lng = conv_ln_g[0][None, :]
    lnb = conv_ln_b[0][None, :]
    wr2, br = _router_params(router_expert_w[0], router_expert_b[0], router_group_w[0], router_group_b[0])

    cos_p, sin_p = _rope_tables(jnp.arange(seq, dtype=F32) + jnp.float32(0))
    pos_s = jnp.tile(jnp.arange(dseq, dtype=F32) + jnp.float32(PAST_LEN), tm // dseq)
    cos_s, sin_s = _rope_tables(pos_s)

    xp = x_prompt.reshape(nb * seq, D_MODEL)
    x1_p, t_p, rec_p, conv_p, ret_p = _mix(xp, g1, w_in_b, cos_p, sin_p, conv_w[0], cb, lng, lnb, w_out_b, g2,
                                           wr2, br, nb, seq, MIX_TILE)

    xs = x_sample.reshape(ns * dseq, D_MODEL)
    u, q, k, v, gs = _inproj(xs, g1, w_in_b, cos_s, sin_s, tm, 1, F32)
    c, conv_s = _conv_sample(u.reshape(ns, dseq, CONV_CH), state_conv[0], conv_w[0], cb, lng, lnb, 16)
    o, ret_s = _ret_sample(q, k, v, gs, state_ret[0], dseq, 8)
    x1_s, t_s, rec_s = _outproj(c.reshape(ns * dseq, CONV_CH), o, xs, w_out_b, g2, wr2, br, tm)

    staged, pos, meta = _dispatch(rec_p, t_p, rec_s, t_s)
    start = meta[:, 0, :N_EXPERTS].astype(jnp.int32).reshape(-1)
    cnt = meta[:, 1, :N_EXPERTS].astype(jnp.int32).reshape(-1)
    ys = _experts(start, cnt, staged, w_expert_gate[0], w_expert_up[0], w_expert_down[0])

    y_p = _combine(ys, pos, x1_p, p_prompt[0].reshape(nb * seq, PLE_DIM), w_ple_b, gp, w_pg_b, gf, 0)
    y_s = _combine(ys, pos, x1_s, p_sample[0].reshape(ns * dseq, PLE_DIM), w_ple_b, gp, w_pg_b, gf,
                   nb * seq // SUB)

    return (y_p.reshape(nb, seq, D_MODEL), y_s.reshape(ns, dseq, D_MODEL),
            conv_p[None], ret_p[None], conv_s[None], ret_s[None])
```

```python
import functools

import jax
import jax.numpy as jnp
from jax import lax
from jax.experimental import pallas as pl
from jax.experimental.pallas import tpu as pltpu

F32 = jnp.float32
BF16 = jnp.bfloat16

D_MODEL = 1024
PLE_DIM = 256
CONV_CH = 512
CONV_K = 31
RET_WIDTH = 512
RET_HEADS = 4
HEAD_DIM = 128
CHUNK = 128
ROPE_BASE = 10000.0
N_GROUPS = 4
EXPERTS_PER_GROUP = 8
N_EXPERTS = 32
EXPERT_FF = 256
IN_COLS = 3072
EPS = 1e-6
PAST_LEN = 16384

LANES = 128
SUBLANES = 8
HALO = 32
HALO_OFF = HALO - (CONV_K - 1)
VMEM_LIMIT = 48 * 1024 * 1024
MIX_VMEM_LIMIT = 56 * 1024 * 1024
MIX_TILE = 512

SUB = 256
ROW_ALIGN = 16
PBLK = 256
SUBP = -(-(2 * SUB + N_EXPERTS * (ROW_ALIGN - 1)) // PBLK) * PBLK
CHUNKS_PER_SUB = SUBP // ROW_ALIGN
ROW_W = D_MODEL + LANES
MBLK = 512
CHUNKS_PER_BLK = MBLK // ROW_ALIGN
LIST_SLACK = 2
COMBINE_SUBS = 4
DISPATCH_SUBS = 4


def _cparams(sem):
    return pltpu.CompilerParams(dimension_semantics=sem, vmem_limit_bytes=VMEM_LIMIT)


def _rms(x, g):
    return x * lax.rsqrt(jnp.mean(x * x, axis=-1, keepdims=True) + EPS) * g


def _inproj_body(x_ref, g1_ref, w_ref, cos_ref, sin_ref, u_ref, q_ref, k_ref, v_ref, gs_ref):
    h = _rms(x_ref[...], g1_ref[...]).astype(BF16)
    z = jnp.dot(h, w_ref[...].astype(BF16), preferred_element_type=F32)
    a = z[:, :CONV_CH]
    b = z[:, CONV_CH:2 * CONV_CH]
    u_ref[...] = a * jax.nn.sigmoid(b)
    cos = cos_ref[...]
    sin = sin_ref[...]
    q0 = 2 * CONV_CH
    k0 = q0 + RET_WIDTH
    for hh in range(RET_HEADS):
        sl = slice(hh * HEAD_DIM, (hh + 1) * HEAD_DIM)
        qh = z[:, q0 + hh * HEAD_DIM:q0 + (hh + 1) * HEAD_DIM]
        kh = z[:, k0 + hh * HEAD_DIM:k0 + (hh + 1) * HEAD_DIM]
        q_ref[:, sl] = (qh * cos + pltpu.roll(qh, HEAD_DIM // 2, 1) * sin).astype(q_ref.dtype)
        kr = (kh * cos + pltpu.roll(kh, HEAD_DIM // 2, 1) * sin) * (HEAD_DIM ** -0.5)
        k_ref[:, sl] = kr.astype(k_ref.dtype)
    v_ref[...] = z[:, k0 + RET_WIDTH:k0 + 2 * RET_WIDTH].astype(v_ref.dtype)
    g = z[:, k0 + 2 * RET_WIDTH:]
    gs_ref[...] = g * jax.nn.sigmoid(g)


def _inproj(x, g1, w_in, cos, sin, tm, table_blocks, qkv_dtype):
    t = x.shape[0]
    row = lambda i: (i, 0)
    const = lambda i: (0, 0)
    tab = (lambda i: (i % table_blocks, 0)) if table_blocks > 1 else const
    return pl.pallas_call(
        _inproj_body,
        grid=(t // tm,),
        in_specs=[
            pl.BlockSpec((tm, D_MODEL), row),
            pl.BlockSpec((1, D_MODEL), const),
            pl.BlockSpec((D_MODEL, IN_COLS), const, pipeline_mode=pl.Buffered(1)),
            pl.BlockSpec((tm, HEAD_DIM), tab),
            pl.BlockSpec((tm, HEAD_DIM), tab),
        ],
        out_specs=[pl.BlockSpec((tm, CONV_CH), row)] + [pl.BlockSpec((tm, RET_WIDTH), row)] * 4,
        out_shape=[
            jax.ShapeDtypeStruct((t, CONV_CH), F32),
            jax.ShapeDtypeStruct((t, RET_WIDTH), qkv_dtype),
            jax.ShapeDtypeStruct((t, RET_WIDTH), qkv_dtype),
            jax.ShapeDtypeStruct((t, RET_WIDTH), qkv_dtype),
            jax.ShapeDtypeStruct((t, RET_WIDTH), F32),
        ],
        compiler_params=_cparams(("parallel",)),
        name="inproj",
    )(x, g1, w_in, cos, sin)


def _ln_silu(acc, g, b):
    mu = jnp.mean(acc, axis=-1, keepdims=True)
    d = acc - mu
    var = jnp.mean(d * d, axis=-1, keepdims=True)
    y = d * lax.rsqrt(var + EPS) * g + b
    return y * jax.nn.sigmoid(y)


def _dwconv(load, w_ref, rows, time_axis):
    acc = None
    for b in range(SUBLANES):
        part = None
        for a in range((CONV_K + HALO_OFF) // SUBLANES + 1):
            k = SUBLANES * a + b - HALO_OFF
            if 0 <= k < CONV_K:
                term = load(SUBLANES * a, rows + SUBLANES) * w_ref[k:k + 1, :]
                part = term if part is None else part + term
        if part is not None:
            shifted = lax.slice_in_dim(part, b, b + rows, axis=time_axis)
            acc = shifted if acc is None else acc + shifted
    return acc


def _conv_prompt_body(u_ref, w_ref, cb_ref, lg_ref, lb_ref, c_ref, st_ref, ext_ref):
    j = pl.program_id(1)
    tl = u_ref.shape[1]

    @pl.when(j == 0)
    def _():
        ext_ref[0:HALO, :] = jnp.zeros((HALO, CONV_CH), F32)
        ext_ref[tl + HALO:, :] = jnp.zeros((SUBLANES, CONV_CH), F32)

    @pl.when(j > 0)
    def _():
        ext_ref[0:HALO, :] = ext_ref[tl:tl + HALO, :]

    ext_ref[HALO:tl + HALO, :] = u_ref[0]
    acc = _dwconv(lambda s, n: ext_ref[s:s + n, :], w_ref, tl, 0) + cb_ref[...]
    c_ref[0] = _ln_silu(acc, lg_ref[...], lb_ref[...]).astype(c_ref.dtype)
    st_ref[0] = ext_ref[tl + HALO_OFF:tl + HALO, :]


def _conv_prompt(u, conv_w, conv_b, ln_g, ln_b, tl):
    n, l, _ = u.shape
    const = lambda b, j: (0, 0)
    return pl.pallas_call(
        _conv_prompt_body,
        grid=(n, l // tl),
        in_specs=[
            pl.BlockSpec((1, tl, CONV_CH), lambda b, j: (b, j, 0)),
            pl.BlockSpec((CONV_K, CONV_CH), const),
            pl.BlockSpec((1, CONV_CH), const),
            pl.BlockSpec((1, CONV_CH), const),
            pl.BlockSpec((1, CONV_CH), const),
        ],
        out_specs=[
            pl.BlockSpec((1, tl, CONV_CH), lambda b, j: (b, j, 0)),
            pl.BlockSpec((1, CONV_K - 1, CONV_CH), lambda b, j: (b, 0, 0)),
        ],
        out_shape=[
            jax.ShapeDtypeStruct((n, l, CONV_CH), BF16),
            jax.ShapeDtypeStruct((n, CONV_K - 1, CONV_CH), F32),
        ],
        scratch_shapes=[pltpu.VMEM((tl + HALO + SUBLANES, CONV_CH), F32)],
        compiler_params=_cparams(("arbitrary", "arbitrary")),
        name="conv_prompt",
    )(u, conv_w, conv_b, ln_g, ln_b)


def _conv_sample_body(u_ref, st_ref, w_ref, cb_ref, lg_ref, lb_ref, c_ref, nst_ref, ext_ref):
    nb, l, _ = u_ref.shape
    ext_ref[:, 0:HALO_OFF, :] = jnp.zeros((nb, HALO_OFF, CONV_CH), F32)
    ext_ref[:, HALO_OFF:HALO, :] = st_ref[...]
    ext_ref[:, HALO:l + HALO, :] = u_ref[...]
    ext_ref[:, l + HALO:, :] = jnp.zeros((nb, SUBLANES, CONV_CH), F32)
    acc = _dwconv(lambda s, n: ext_ref[:, s:s + n, :], w_ref, l, 1) + cb_ref[...]
    c_ref[...] = _ln_silu(acc, lg_ref[...], lb_ref[...]).astype(c_ref.dtype)
    nst_ref[...] = ext_ref[:, l + HALO_OFF:l + HALO, :]


def _conv_sample(u, state, conv_w, conv_b, ln_g, ln_b, nb):
    n, l, _ = u.shape
    const = lambda b: (0, 0)
    blk3 = lambda b: (b, 0, 0)
    return pl.pallas_call(
        _conv_sample_body,
        grid=(n // nb,),
        in_specs=[
            pl.BlockSpec((nb, l, CONV_CH), blk3),
            pl.BlockSpec((nb, CONV_K - 1, CONV_CH), blk3),
            pl.BlockSpec((CONV_K, CONV_CH), const),
            pl.BlockSpec((1, CONV_CH), const),
            pl.BlockSpec((1, CONV_CH), const),
            pl.BlockSpec((1, CONV_CH), const),
        ],
        out_specs=[
            pl.BlockSpec((nb, l, CONV_CH), blk3),
            pl.BlockSpec((nb, CONV_K - 1, CONV_CH), blk3),
        ],
        out_shape=[
            jax.ShapeDtypeStruct((n, l, CONV_CH), BF16),
            jax.ShapeDtypeStruct((n, CONV_K - 1, CONV_CH), F32),
        ],
        scratch_shapes=[pltpu.VMEM((nb, l + HALO + SUBLANES, CONV_CH), F32)],
        compiler_params=_cparams(("parallel",)),
        name="conv_sample",
    )(u, state, conv_w, conv_b, ln_g, ln_b)


def _decay_tables(c):
    lg = jnp.log(1.0 - 2.0 ** (-5.0 - jnp.arange(RET_HEADS, dtype=F32)))
    idx = jnp.arange(c, dtype=F32)
    rel = idx[:, None] - idx[None, :]
    dmat = jnp.where(rel[None] >= 0, jnp.exp(jnp.maximum(rel, 0.0)[None] * lg[:, None, None]), 0.0)
    xi = jnp.exp((idx + 1.0)[None, :] * lg[:, None])
    zeta = jnp.exp((c - 1.0 - idx)[None, :] * lg[:, None])
    gc = jnp.exp(c * lg)
    xi_b = jnp.broadcast_to(xi[:, :, None], (RET_HEADS, c, HEAD_DIM))
    zeta_b = jnp.broadcast_to(zeta[:, :, None], (RET_HEADS, c, HEAD_DIM))
    gc_b = jnp.broadcast_to(gc[:, None, None], (RET_HEADS, 1, HEAD_DIM))
    return dmat, xi_b, zeta_b, gc_b


def _group_norm(o):
    mu = jnp.mean(o, axis=-1, keepdims=True)
    d = o - mu
    var = jnp.mean(d * d, axis=-1, keepdims=True)
    return d * lax.rsqrt(var + EPS)


def _ret_chunk(qh, kh, vh, r, dmat, xi, zeta, gc):
    qb = qh.astype(BF16)
    kb = kh.astype(BF16)
    vb = vh.astype(BF16)
    s = lax.dot_general(qb, kb, (((1,), (1,)), ((), ())), preferred_element_type=F32) * dmat
    o = jnp.dot(s.astype(BF16), vb, preferred_element_type=F32)
    o = o + jnp.dot(qb, r.astype(BF16), preferred_element_type=F32) * xi
    kz = (kh.astype(F32) * zeta).astype(BF16)
    r_new = r * gc + lax.dot_general(kz, vb, (((0,), (0,)), ((), ())), preferred_element_type=F32)
    return o, r_new


def _ret_prompt_body(q_ref, k_ref, v_ref, gs_ref, d_ref, xi_ref, zeta_ref, gc_ref, o_ref, st_ref, r_ref):
    j = pl.program_id(1)

    @pl.when(j == 0)
    def _():
        r_ref[...] = jnp.zeros_like(r_ref)

    n_chunks = q_ref.shape[0] // CHUNK
    for hh in range(RET_HEADS):
        sl = slice(hh * HEAD_DIM, (hh + 1) * HEAD_DIM)
        r = r_ref[hh]
        for ci in range(n_chunks):
            rows = slice(ci * CHUNK, (ci + 1) * CHUNK)
            o, r = _ret_chunk(q_ref[rows, sl], k_ref[rows, sl], v_ref[rows, sl], r,
                              d_ref[hh], xi_ref[hh], zeta_ref[hh], gc_ref[hh])
            o_ref[rows, sl] = (gs_ref[rows, sl] * _group_norm(o)).astype(o_ref.dtype)
        r_ref[hh] = r
    st_ref[0] = r_ref[...]


def _ret_prompt(q, k, v, gs, n, l, tl):
    dmat, xi, zeta, gc = _decay_tables(CHUNK)
    per = l // tl
    row = lambda b, j: (b * per + j, 0)
    c3 = lambda b, j: (0, 0, 0)
    return pl.pallas_call(
        _ret_prompt_body,
        grid=(n, per),
        in_specs=[pl.BlockSpec((tl, RET_WIDTH), row)] * 4 + [
            pl.BlockSpec((RET_HEADS, CHUNK, CHUNK), c3),
            pl.BlockSpec((RET_HEADS, CHUNK, HEAD_DIM), c3),
            pl.BlockSpec((RET_HEADS, CHUNK, HEAD_DIM), c3),
            pl.BlockSpec((RET_HEADS, 1, HEAD_DIM), c3),
        ],
        out_specs=[
            pl.BlockSpec((tl, RET_WIDTH), row),
            pl.BlockSpec((1, RET_HEADS, HEAD_DIM, HEAD_DIM), lambda b, j: (b, 0, 0, 0)),
        ],
        out_shape=[
            jax.ShapeDtypeStruct((n * l, RET_WIDTH), BF16),
            jax.ShapeDtypeStruct((n, RET_HEADS, HEAD_DIM, HEAD_DIM), F32),
        ],
        scratch_shapes=[pltpu.VMEM((RET_HEADS, HEAD_DIM, HEAD_DIM), F32)],
        compiler_params=_cparams(("arbitrary", "arbitrary")),
        name="ret_prompt",
    )(q, k, v, gs, dmat, xi, zeta, gc)


def _ret_sample_body(q_ref, k_ref, v_ref, gs_ref, st_ref, d_ref, xi_ref, zeta_ref, gc_ref, o_ref, nst_ref):
    nb = st_ref.shape[0]
    l = q_ref.shape[0] // nb
    for b in range(nb):
        rows = slice(b * l, (b + 1) * l)
        for hh in range(RET_HEADS):
            sl = slice(hh * HEAD_DIM, (hh + 1) * HEAD_DIM)
            o, r = _ret_chunk(q_ref[rows, sl], k_ref[rows, sl], v_ref[rows, sl], st_ref[b, hh],
                              d_ref[hh], xi_ref[hh], zeta_ref[hh], gc_ref[hh])
            o_ref[rows, sl] = (gs_ref[rows, sl] * _group_norm(o)).astype(o_ref.dtype)
            nst_ref[b, hh] = r


def _ret_sample(q, k, v, gs, state, l, nb):
    n = state.shape[0]
    dmat, xi, zeta, gc = _decay_tables(l)
    row = lambda b: (b, 0)
    c3 = lambda b: (0, 0, 0)
    blk4 = lambda b: (b, 0, 0, 0)
    return pl.pallas_call(
        _ret_sample_body,
        grid=(n // nb,),
        in_specs=[pl.BlockSpec((nb * l, RET_WIDTH), row)] * 4 + [
            pl.BlockSpec((nb, RET_HEADS, HEAD_DIM, HEAD_DIM), blk4),
            pl.BlockSpec((RET_HEADS, l, l), c3),
            pl.BlockSpec((RET_HEADS, l, HEAD_DIM), c3),
            pl.BlockSpec((RET_HEADS, l, HEAD_DIM), c3),
            pl.BlockSpec((RET_HEADS, 1, HEAD_DIM), c3),
        ],
        out_specs=[
            pl.BlockSpec((nb * l, RET_WIDTH), row),
            pl.BlockSpec((nb, RET_HEADS, HEAD_DIM, HEAD_DIM), blk4),
        ],
        out_shape=[
            jax.ShapeDtypeStruct((n * l, RET_WIDTH), BF16),
            jax.ShapeDtypeStruct((n, RET_HEADS, HEAD_DIM, HEAD_DIM), F32),
        ],
        compiler_params=_cparams(("parallel",)),
        name="ret_sample",
    )(q, k, v, gs, state, dmat, xi, zeta, gc)


def _split3(x):
    hi = x.astype(BF16)
    r1 = x - hi.astype(F32)
    mid = r1.astype(BF16)
    lo = (r1 - mid.astype(F32)).astype(BF16)
    return hi, mid, lo


def _dot_hp(t, w2_ref):
    t_hi, t_mid, _ = _split3(t)
    d = functools.partial(jnp.dot, preferred_element_type=F32)
    both = d(t_hi, w2_ref[...])
    return both[:, :LANES] + (d(t_mid, w2_ref[:, 0:LANES]) + both[:, LANES:])


def _route(logits):
    lt = logits.T
    tm = lt.shape[1]
    row = _iota_f32((SUBLANES, tm), 0)
    big = float(SUBLANES)

    def rmax(x):
        return jnp.max(x, axis=0, keepdims=True)

    def first_row(mask):
        return jnp.min(jnp.where(mask, row, big), axis=0, keepdims=True)

    lg = jnp.where(row < float(N_GROUPS), lt[N_EXPERTS:N_EXPERTS + SUBLANES, :], -1e30)
    m = rmax(lg)
    g_top = 1.0 / jnp.sum(jnp.exp(lg - m), axis=0, keepdims=True)
    g_idx = first_row(lg == m)
    lem = lt[0:EXPERTS_PER_GROUP, :]
    for g in range(1, N_GROUPS):
        lem = jnp.where(g_idx == float(g), lt[g * EXPERTS_PER_GROUP:(g + 1) * EXPERTS_PER_GROUP, :], lem)
    pe = jnp.exp(lem - rmax(lem))
    p1 = rmax(pe)
    e1 = first_row(pe == p1)
    rest = row != e1
    pe2 = jnp.where(rest, pe, -1.0)
    p2 = rmax(pe2)
    e2 = first_row(rest & (pe2 == p2))
    scale = g_top / (p1 + p2)
    base = g_idx * float(EXPERTS_PER_GROUP)
    rec_t = jnp.where(row == 0.0, base + e1, jnp.where(row == 1.0, base + e2, 0.0))
    rec_t = rec_t + jnp.where(row == 2.0, p1 * scale, jnp.where(row == 3.0, p2 * scale, 0.0))
    return jnp.concatenate([rec_t, jnp.zeros((LANES - SUBLANES, tm), F32)], axis=0).T


def _outproj_body(c_ref, o_ref, x_ref, wo_ref, g2_ref, wr_ref, br_ref, x1_ref, t_ref, rec_ref):
    x1 = x_ref[...] + jnp.dot(c_ref[...], wo_ref[0:CONV_CH, :].astype(BF16), preferred_element_type=F32)
    x1 = x1 + jnp.dot(o_ref[...], wo_ref[CONV_CH:, :].astype(BF16), preferred_element_type=F32)
    x1_ref[...] = x1
    t = _rms(x1, g2_ref[...])
    t_ref[...] = t.astype(t_ref.dtype)
    rec_ref[...] = _route(_dot_hp(t, wr_ref) + br_ref[...])


def _outproj(c, o, x, w_out, g2, wr2, br, tm):
    t = x.shape[0]
    row = lambda i: (i, 0)
    const = lambda i: (0, 0)
    c3 = lambda i: (0, 0, 0)
    return pl.pallas_call(
        _outproj_body,
        grid=(t // tm,),
        in_specs=[
            pl.BlockSpec((tm, CONV_CH), row),
            pl.BlockSpec((tm, RET_WIDTH), row),
            pl.BlockSpec((tm, D_MODEL), row),
            pl.BlockSpec((D_MODEL, D_MODEL), const),
            pl.BlockSpec((1, D_MODEL), const),
            pl.BlockSpec((D_MODEL, 2 * LANES), const),
            pl.BlockSpec((1, LANES), const),
        ],
        out_specs=[
            pl.BlockSpec((tm, D_MODEL), row),
            pl.BlockSpec((tm, D_MODEL), row),
            pl.BlockSpec((tm, LANES), row),
        ],
        out_shape=[
            jax.ShapeDtypeStruct((t, D_MODEL), F32),
            jax.ShapeDtypeStruct((t, D_MODEL), BF16),
            jax.ShapeDtypeStruct((t, LANES), F32),
        ],
        compiler_params=_cparams(("parallel",)),
        name="outproj_router",
    )(c, o, x, w_out, g2, wr2, br)


def _mix_body(x_ref, g1_ref, w_ref, cos_ref, sin_ref, cw_ref, cb_ref, lg_ref, lb_ref,
              d_ref, xi_ref, zeta_ref, gc_ref, wo_ref, g2_ref, wr_ref, br_ref,
              x1_ref, t_ref, rec_ref, cst_ref, rst_ref, ext_ref, r_ref, o_ref, wb_ref, wob_ref):
    j = pl.program_id(1)
    tl = x_ref.shape[0]
    x = x_ref[...]

    @pl.when((pl.program_id(0) == 0) & (j == 0))
    def _():
        wb_ref[...] = w_ref[...].astype(BF16)
        wob_ref[...] = wo_ref[...].astype(BF16)

    z = jnp.dot(_rms(x, g1_ref[...]).astype(BF16), wb_ref[...], preferred_element_type=F32)

    @pl.when(j == 0)
    def _():
        ext_ref[0:HALO, :] = jnp.zeros((HALO, CONV_CH), F32)
        ext_ref[tl + HALO:, :] = jnp.zeros((SUBLANES, CONV_CH), F32)
        r_ref[...] = jnp.zeros_like(r_ref)

    @pl.when(j > 0)
    def _():
        ext_ref[0:HALO, :] = ext_ref[tl:tl + HALO, :]

    ext_ref[HALO:tl + HALO, :] = z[:, :CONV_CH] * jax.nn.sigmoid(z[:, CONV_CH:2 * CONV_CH])
    acc = _dwconv(lambda s, n: ext_ref[s:s + n, :], cw_ref, tl, 0) + cb_ref[...]
    c = _ln_silu(acc, lg_ref[...], lb_ref[...]).astype(BF16)
    cst_ref[0] = ext_ref[tl + HALO_OFF:tl + HALO, :]

    cos = cos_ref[...]
    sin = sin_ref[...]
    q0 = 2 * CONV_CH
    k0 = q0 + RET_WIDTH
    v0 = k0 + RET_WIDTH
    g0 = v0 + RET_WIDTH
    for hh in range(RET_HEADS):
        lo = hh * HEAD_DIM
        qh = z[:, q0 + lo:q0 + lo + HEAD_DIM]
        kh = z[:, k0 + lo:k0 + lo + HEAD_DIM]
        qr = (qh * cos + pltpu.roll(qh, HEAD_DIM // 2, 1) * sin).astype(BF16)
        kr = ((kh * cos + pltpu.roll(kh, HEAD_DIM // 2, 1) * sin) * (HEAD_DIM ** -0.5)).astype(BF16)
        vh = z[:, v0 + lo:v0 + lo + HEAD_DIM].astype(BF16)
        g = z[:, g0 + lo:g0 + lo + HEAD_DIM]
        gs = g * jax.nn.sigmoid(g)
        r = r_ref[hh]
        for ci in range(tl // CHUNK):
            rows = slice(ci * CHUNK, (ci + 1) * CHUNK)
            o, r = _ret_chunk(qr[rows], kr[rows], vh[rows], r, d_ref[hh], xi_ref[hh], zeta_ref[hh], gc_ref[hh])
            o_ref[rows, lo:lo + HEAD_DIM] = (gs[rows] * _group_norm(o)).astype(BF16)
        r_ref[hh] = r
    rst_ref[0] = r_ref[...]

    x1 = x + jnp.dot(c, wob_ref[0:CONV_CH, :], preferred_element_type=F32)
    x1 = x1 + jnp.dot(o_ref[...], wob_ref[CONV_CH:, :], preferred_element_type=F32)
    x1_ref[...] = x1
    t = _rms(x1, g2_ref[...])
    t_ref[...] = t.astype(t_ref.dtype)
    rec_ref[...] = _route(_dot_hp(t, wr_ref) + br_ref[...])


def _mix(x, g1, w_in, cos, sin, conv_w, conv_b, ln_g, ln_b, w_out, g2, wr2, br, n, l, tl):
    dmat, xi, zeta, gc = _decay_tables(CHUNK)
    per = l // tl
    row = lambda b, j: (b * per + j, 0)
    tab = lambda b, j: (j, 0)
    const = lambda b, j: (0, 0)
    c3 = lambda b, j: (0, 0, 0)
    once = dict(pipeline_mode=pl.Buffered(1))
    return pl.pallas_call(
        _mix_body,
        grid=(n, per),
        in_specs=[
            pl.BlockSpec((tl, D_MODEL), row),
            pl.BlockSpec((1, D_MODEL), const),
            pl.BlockSpec((D_MODEL, IN_COLS), const, **once),
            pl.BlockSpec((tl, HEAD_DIM), tab),
            pl.BlockSpec((tl, HEAD_DIM), tab),
            pl.BlockSpec((CONV_K, CONV_CH), const),
            pl.BlockSpec((1, CONV_CH), const),
            pl.BlockSpec((1, CONV_CH), const),
            pl.BlockSpec((1, CONV_CH), const),
            pl.BlockSpec((RET_HEADS, CHUNK, CHUNK), c3),
            pl.BlockSpec((RET_HEADS, CHUNK, HEAD_DIM), c3),
            pl.BlockSpec((RET_HEADS, CHUNK, HEAD_DIM), c3),
            pl.BlockSpec((RET_HEADS, 1, HEAD_DIM), c3),
            pl.BlockSpec((D_MODEL, D_MODEL), const, **once),
            pl.BlockSpec((1, D_MODEL), const),
            pl.BlockSpec((D_MODEL, 2 * LANES), const, **once),
            pl.BlockSpec((1, LANES), const),
        ],
        out_specs=[
            pl.BlockSpec((tl, D_MODEL), row),
            pl.BlockSpec((tl, D_MODEL), row),
            pl.BlockSpec((tl, LANES), row),
            pl.BlockSpec((1, CONV_K - 1, CONV_CH), lambda b, j: (b, 0, 0)),
            pl.BlockSpec((1, RET_HEADS, HEAD_DIM, HEAD_DIM), lambda b, j: (b, 0, 0, 0)),
        ],
        out_shape=[
            jax.ShapeDtypeStruct((n * l, D_MODEL), F32),
            jax.ShapeDtypeStruct((n * l, D_MODEL), BF16),
            jax.ShapeDtypeStruct((n * l, LANES), F32),
            jax.ShapeDtypeStruct((n, CONV_K - 1, CONV_CH), F32),
            jax.ShapeDtypeStruct((n, RET_HEADS, HEAD_DIM, HEAD_DIM), F32),
        ],
        scratch_shapes=[
            pltpu.VMEM((tl + HALO + SUBLANES, CONV_CH), F32),
            pltpu.VMEM((RET_HEADS, HEAD_DIM, HEAD_DIM), F32),
            pltpu.VMEM((tl, RET_WIDTH), BF16),
            pltpu.VMEM((D_MODEL, IN_COLS), BF16),
            pltpu.VMEM((D_MODEL, D_MODEL), BF16),
        ],
        compiler_params=pltpu.CompilerParams(dimension_semantics=("arbitrary", "arbitrary"),
                                             vmem_limit_bytes=MIX_VMEM_LIMIT),
        name="token_mix",
    )(x, g1, w_in, cos, sin, conv_w, conv_b, ln_g, ln_b, dmat, xi, zeta, gc, w_out, g2, wr2, br)


def _iota_f32(shape, dim):
    return lax.broadcasted_iota(jnp.int32, shape, dim).astype(F32)


def _dispatch_body(rec_a_ref, t_a_ref, rec_b_ref, t_b_ref, s_ref, pos_ref, meta_ref, *, nsub_a):
    from_a = pl.program_id(0) * DISPATCH_SUBS < nsub_a
    for s in range(DISPATCH_SUBS):
        rows = slice(s * SUB, (s + 1) * SUB)
        rec = jnp.where(from_a, rec_a_ref[rows, :], rec_b_ref[rows, :])
        tok = jnp.where(from_a, t_a_ref[rows, :], t_b_ref[rows, :])
        chunks = slice(s * CHUNKS_PER_SUB, (s + 1) * CHUNKS_PER_SUB)
        _dispatch_sub_tile(rec, tok, s_ref.at[chunks], pos_ref.at[rows], meta_ref.at[s])


def _dispatch_sub_tile(rec, tok, s_ref, pos_ref, meta_ref):
    lane = _iota_f32(rec.shape, 1)
    a1 = lane == rec[:, 0:1]
    a2 = lane == rec[:, 1:2]
    a1f = jnp.where(a1, 1.0, 0.0)
    a2f = jnp.where(a2, 1.0, 0.0)
    ltri = jnp.where(_iota_f32((SUB, SUB), 1) < _iota_f32((SUB, SUB), 0), 1.0, 0.0).astype(BF16)
    c1 = jnp.dot(ltri, a1f.astype(BF16), preferred_element_type=F32)
    c2 = jnp.dot(ltri, a2f.astype(BF16), preferred_element_type=F32)
    n1 = jnp.sum(a1f, axis=0, keepdims=True)
    n2 = jnp.sum(a2f, axis=0, keepdims=True)
    cnt = jnp.floor((n1 + n2 + (ROW_ALIGN - 1.0)) * (1.0 / ROW_ALIGN))
    utri = jnp.where(_iota_f32((LANES, LANES), 0) < _iota_f32((LANES, LANES), 1), 1.0, 0.0).astype(BF16)
    start = jnp.dot(jnp.broadcast_to(cnt, (SUBLANES, LANES)).astype(BF16), utri,
                    preferred_element_type=F32)[0:1]
    base1 = start * ROW_ALIGN
    base2 = base1 + n1
    pos1 = jnp.sum(jnp.where(a1, c1 + base1, 0.0), axis=1, keepdims=True)
    pos2 = jnp.sum(jnp.where(a2, c2 + base2, 0.0), axis=1, keepdims=True)
    posm = jnp.where(lane == 0.0, pos1, jnp.where(lane == 1.0, pos2, 0.0))
    pos_ref[...] = posm
    row = _iota_f32((SUBLANES, LANES), 0)
    meta_ref[...] = jnp.where(row == 0.0, start, jnp.where(row == 1.0, cnt, 0.0))

    g1 = _split3(rec[:, 2:3])
    g2 = _split3(rec[:, 3:4])
    info = jnp.where(lane == 6.0, rec[:, 0:1], jnp.where(lane == 7.0, rec[:, 1:2], 0.0))
    for i in range(3):
        info = jnp.where(lane == float(i), g1[i].astype(F32), info)
        info = jnp.where(lane == float(3 + i), g2[i].astype(F32), info)
    src = jnp.concatenate([tok, info.astype(BF16)], axis=1)

    post = posm.T
    r = _iota_f32((SUBP, SUB), 0)
    onehot = jnp.where(r == post[0:1, :], 1.0, jnp.where(r == post[1:2, :], 1.0, 0.0)).astype(BF16)
    sorted_rows = jnp.dot(onehot, src, preferred_element_type=F32).astype(BF16)
    s_ref[...] = sorted_rows.reshape(CHUNKS_PER_SUB, ROW_ALIGN, ROW_W)


def _dispatch(rec_a, t_a, rec_b, t_b):
    nsub_a = rec_a.shape[0] // SUB
    nsub_b = rec_b.shape[0] // SUB
    nsub = nsub_a + nsub_b
    assert nsub_a % DISPATCH_SUBS == 0 and nsub_b % DISPATCH_SUBS == 0
    steps_a = nsub_a // DISPATCH_SUBS
    tm = DISPATCH_SUBS * SUB
    row = lambda i: (i, 0)
    row_a = lambda i: (jnp.minimum(i, steps_a - 1), 0)
    row_b = lambda i: (jnp.maximum(i - steps_a, 0), 0)
    return pl.pallas_call(
        functools.partial(_dispatch_body, nsub_a=nsub_a),
        grid=(nsub // DISPATCH_SUBS,),
        in_specs=[
            pl.BlockSpec((tm, LANES), row_a),
            pl.BlockSpec((tm, D_MODEL), row_a),
            pl.BlockSpec((tm, LANES), row_b),
            pl.BlockSpec((tm, D_MODEL), row_b),
        ],
        out_specs=[
            pl.BlockSpec((DISPATCH_SUBS * CHUNKS_PER_SUB, ROW_ALIGN, ROW_W), lambda i: (i, 0, 0)),
            pl.BlockSpec((tm, LANES), row),
            pl.BlockSpec((DISPATCH_SUBS, SUBLANES, LANES), lambda i: (i, 0, 0)),
        ],
        out_shape=[
            jax.ShapeDtypeStruct((nsub * CHUNKS_PER_SUB, ROW_ALIGN, ROW_W), BF16),
            jax.ShapeDtypeStruct((nsub * SUB, LANES), F32),
            jax.ShapeDtypeStruct((nsub, SUBLANES, LANES), F32),
        ],
        compiler_params=_cparams(("parallel",)),
        name="moe_dispatch",
    )(rec_a, t_a, rec_b, t_b)


def _experts_body(start_ref, cnt_ref, s_in, wg_ref, wu_ref, wd_ref, s_hbm,
                  xbuf, ybuf, wgu_ref, wdb_ref, gsem, ssem, list_ref, state_ref, *, nsub):
    del s_in
    e = pl.program_id(0)
    ne = pl.num_programs(0)
    par = e & 1

    list_max = list_ref.shape[0] // 2

    def gather_copy(src, i, slot):
        return pltpu.make_async_copy(s_hbm.at[src], xbuf.at[slot, i], gsem.at[slot])

    def scatter_copy(dst, i, slot):
        return pltpu.make_async_copy(ybuf.at[slot, i], s_hbm.at[dst], ssem.at[slot])

    def build_list(x, which):
        def per_sub(s, k):
            run = s * N_EXPERTS + x
            c = cnt_ref[run]
            base = s * CHUNKS_PER_SUB + start_ref[run]
            list_ref[k] = base
            list_ref[k + 1] = base + 1

            def per_chunk(i, carry):
                list_ref[k + i] = base + i
                return carry
            lax.fori_loop(2, c, per_chunk, 0)
            return k + c
        first = which * list_max
        state_ref[which] = lax.fori_loop(0, nsub, per_sub, first) - first

    def start_all(copy, which, first, n, slot, counter):
        def body(i, carry):
            copy(list_ref[which * list_max + first + i], i, slot).start()
            return carry
        lax.fori_loop(0, n, body, 0)
        state_ref[counter] = n

    def wait_all(copy, block_copy, slot, counter):
        n = state_ref[counter]

        @pl.when(n == CHUNKS_PER_BLK)
        def _():
            block_copy(slot).wait()

        @pl.when(n < CHUNKS_PER_BLK)
        def _():
            def body(i, carry):
                copy(0, 0, slot).wait()
                return carry
            lax.fori_loop(0, n, body, 0)
        state_ref[counter] = 0

    def gather_block(slot):
        return pltpu.make_async_copy(s_hbm.at[pl.ds(0, CHUNKS_PER_BLK)], xbuf.at[slot], gsem.at[slot])

    def scatter_block(slot):
        return pltpu.make_async_copy(ybuf.at[slot], s_hbm.at[pl.ds(0, CHUNKS_PER_BLK)], ssem.at[slot])

    def block_chunks(total, b):
        return jnp.minimum(total - b * CHUNKS_PER_BLK, CHUNKS_PER_BLK)

    @pl.when(e == 0)
    def _():
        for i in range(6):
            state_ref[i] = 0
        xbuf[...] = jnp.zeros_like(xbuf)
        build_list(0, 0)
        n0 = state_ref[0]

        @pl.when(n0 > 0)
        def _():
            start_all(gather_copy, 0, 0, block_chunks(n0, 0), 0, 2)

    @pl.when(e + 1 < ne)
    def _():
        build_list(e + 1, 1 - par)

    total = state_ref[par]
    nblk = (total + CHUNKS_PER_BLK - 1) // CHUNKS_PER_BLK
    wgu_ref[:, 0:EXPERT_FF] = wg_ref[0].astype(BF16)
    wgu_ref[:, EXPERT_FF:] = wu_ref[0].astype(BF16)
    wdb_ref[...] = wd_ref[0].astype(BF16)
    ef = e.astype(F32)

    def block(b, carry):
        slot = b & 1
        first = b * CHUNKS_PER_BLK

        @pl.when(b + 1 < nblk)
        def _():
            start_all(gather_copy, par, first + CHUNKS_PER_BLK, block_chunks(total, b + 1), 1 - slot, 3 - slot)

        wait_all(gather_copy, gather_block, slot, 2 + slot)
        wait_all(scatter_copy, scatter_block, slot, 4 + slot)
        x = xbuf[slot].reshape(MBLK, ROW_W)
        info = x[:, D_MODEL:].astype(F32)
        g_first = info[:, 0:1] + info[:, 1:2] + info[:, 2:3]
        g_second = info[:, 3:4] + info[:, 4:5] + info[:, 5:6]
        gate = jnp.where(info[:, 6:7] == ef, g_first, g_second)
        h = jnp.dot(x[:, :D_MODEL], wgu_ref[...], preferred_element_type=F32)
        h1 = h[:, :EXPERT_FF]
        hid = (h1 * jax.nn.sigmoid(h1)) * h[:, EXPERT_FF:] * gate
        y = jnp.dot(hid.astype(BF16), wdb_ref[...], preferred_element_type=F32).astype(BF16)
        ybuf[slot] = jnp.concatenate([y, x[:, D_MODEL:]], axis=1).reshape(CHUNKS_PER_BLK, ROW_ALIGN, ROW_W)
        start_all(scatter_copy, par, first, block_chunks(total, b), slot, 4 + slot)
        return carry

    lax.fori_loop(0, nblk, block, 0)

    @pl.when(e + 1 < ne)
    def _():
        n1 = state_ref[1 - par]

        @pl.when(n1 > 0)
        def _():
            start_all(gather_copy, 1 - par, 0, block_chunks(n1, 0), 0, 2)

    @pl.when(e == ne - 1)
    def _():
        wait_all(scatter_copy, scatter_block, 0, 4)
        wait_all(scatter_copy, scatter_block, 1, 5)


def _experts(start, cnt, staged, wg, wu, wd):
    nsub = staged.shape[0] // CHUNKS_PER_SUB
    list_max = nsub * SUB // ROW_ALIGN + nsub + LIST_SLACK
    wblk = lambda e, *_: (e, 0, 0)
    grid_spec = pltpu.PrefetchScalarGridSpec(
        num_scalar_prefetch=2,
        grid=(N_EXPERTS,),
        in_specs=[
            pl.BlockSpec(memory_space=pl.ANY),
            pl.BlockSpec((1, D_MODEL, EXPERT_FF), wblk),
            pl.BlockSpec((1, D_MODEL, EXPERT_FF), wblk),
            pl.BlockSpec((1, EXPERT_FF, D_MODEL), wblk),
        ],
        out_specs=pl.BlockSpec(memory_space=pl.ANY),
        scratch_shapes=[
            pltpu.VMEM((2, CHUNKS_PER_BLK, ROW_ALIGN, ROW_W), BF16),
            pltpu.VMEM((2, CHUNKS_PER_BLK, ROW_ALIGN, ROW_W), BF16),
            pltpu.VMEM((D_MODEL, 2 * EXPERT_FF), BF16),
            pltpu.VMEM((EXPERT_FF, D_MODEL), BF16),
            pltpu.SemaphoreType.DMA((2,)),
            pltpu.SemaphoreType.DMA((2,)),
            pltpu.SMEM((2 * list_max,), jnp.int32),
            pltpu.SMEM((6,), jnp.int32),
        ],
    )
    return pl.pallas_call(
        functools.partial(_experts_body, nsub=nsub),
        grid_spec=grid_spec,
        out_shape=jax.ShapeDtypeStruct(staged.shape, staged.dtype),
        input_output_aliases={2: 0},
        compiler_params=_cparams(("arbitrary",)),
        name="moe_experts",
    )(start, cnt, staged, wg, wu, wd)


def _combine_body(ys_ref, pos_ref, x1_ref, p_ref, wp_ref, gp_ref, wpg_ref, gf_ref, y_ref):
    r = _iota_f32((SUB, SUBP), 1)
    moe = []
    for s in range(x1_ref.shape[0] // SUB):
        p1 = pos_ref[s * SUB:(s + 1) * SUB, 0:1]
        p2 = pos_ref[s * SUB:(s + 1) * SUB, 1:2]
        onehot = jnp.where(r == p1, 1.0, jnp.where(r == p2, 1.0, 0.0)).astype(BF16)
        ys = ys_ref[s * CHUNKS_PER_SUB:(s + 1) * CHUNKS_PER_SUB].reshape(SUBP, D_MODEL)
        moe.append(jnp.dot(onehot, ys, preferred_element_type=F32))
    x2 = x1_ref[...] + jnp.concatenate(moe, axis=0)
    ple = _rms(jnp.dot(p_ref[...].astype(BF16), wp_ref[...].astype(BF16), preferred_element_type=F32),
               gp_ref[...])
    gate = jax.nn.sigmoid(jnp.dot(x2.astype(BF16), wpg_ref[...].astype(BF16), preferred_element_type=F32))
    y_ref[...] = _rms(x2 + ple * gate, gf_ref[...])


def _combine(ys, pos, x1, p, w_ple, gp, w_ple_gate, gf, sub_off):
    t = x1.shape[0]
    tm = COMBINE_SUBS * SUB
    blk_off = sub_off // COMBINE_SUBS
    assert sub_off % COMBINE_SUBS == 0 and t % tm == 0
    row = lambda i: (i, 0)
    const = lambda i: (0, 0)
    return pl.pallas_call(
        _combine_body,
        grid=(t // tm,),
        in_specs=[
            pl.BlockSpec((COMBINE_SUBS * CHUNKS_PER_SUB, ROW_ALIGN, D_MODEL), lambda i: (i + blk_off, 0, 0)),
            pl.BlockSpec((tm, LANES), lambda i: (i + blk_off, 0)),
            pl.BlockSpec((tm, D_MODEL), row),
            pl.BlockSpec((tm, PLE_DIM), row),
            pl.BlockSpec((PLE_DIM, D_MODEL), const, pipeline_mode=pl.Buffered(1)),
            pl.BlockSpec((1, D_MODEL), const),
            pl.BlockSpec((D_MODEL, D_MODEL), const, pipeline_mode=pl.Buffered(1)),
            pl.BlockSpec((1, D_MODEL), const),
        ],
        out_specs=pl.BlockSpec((tm, D_MODEL), row),
        out_shape=jax.ShapeDtypeStruct((t, D_MODEL), F32),
        compiler_params=_cparams(("parallel",)),
        name="moe_combine_ple",
    )(ys, pos, x1, p, w_ple, gp, w_ple_gate, gf)


def _rope_tables(pos):
    half = HEAD_DIM // 2
    inv = ROPE_BASE ** (-jnp.arange(half, dtype=F32) / half)
    ang = pos[:, None] * inv[None, :]
    cos = jnp.cos(ang)
    sin = jnp.sin(ang)
    return jnp.concatenate([cos, cos], axis=-1), jnp.concatenate([-sin, sin], axis=-1)


def _router_params(we, be, wg, bg):
    pad = LANES - N_EXPERTS - N_GROUPS
    w = jnp.pad(jnp.concatenate([we, wg], axis=1), ((0, 0), (0, pad)))
    b = jnp.pad(jnp.concatenate([be, bg]), (0, pad))[None, :]
    return jnp.concatenate(_split3(w)[:2], axis=1), b


def kernel(x_prompt, x_sample, p_prompt, p_sample, state_conv, state_ret, w_in, conv_w, conv_b, conv_ln_g, conv_ln_b, w_out, norm1_g, norm2_g, router_group_w, router_group_b, router_expert_w, router_expert_b, w_expert_gate, w_expert_up, w_expert_down, w_ple, ple_norm_g, w_ple_gate, final_norm_g):
    assert w_in.shape[0] == 1, "single-layer trunk"
    nb, seq, _ = x_prompt.shape
    ns, dseq, _ = x_sample.shape
    tm = 512

    w_in_l, w_out_l, w_ple_l, w_pg_l = w_in[0], w_out[0], w_ple[0], w_ple_gate[0]
    g1 = norm1_g[0][None, :]
    g2 = norm2_g[0][None, :]
    gp = ple_norm_g[0][None, :]
    gf = final_norm_g[None, :]
    cb = conv_b[0][None, :]
    lng = conv_ln_g[0][None, :]
    lnb = conv_ln_b[0][None, :]
    wr2, br = _router_params(router_expert_w[0], router_expert_b[0], router_group_w[0], router_group_b[0])

    cos_p, sin_p = _rope_tables(jnp.arange(seq, dtype=F32) + jnp.float32(0))
    pos_s = jnp.tile(jnp.arange(dseq, dtype=F32) + jnp.float32(PAST_LEN), tm // dseq)
    cos_s, sin_s = _rope_tables(pos_s)

    xp = x_prompt.reshape(nb * seq, D_MODEL)
    x1_p, t_p, rec_p, conv_p, ret_p = _mix(xp, g1, w_in_l, cos_p, sin_p, conv_w[0], cb, lng, lnb, w_out_l, g2,
                                           wr2, br, nb, seq, MIX_TILE)

    xs = x_sample.reshape(ns * dseq, D_MODEL)
    u, q, k, v, gs = _inproj(xs, g1, w_in_l, cos_s, sin_s, tm, 1, F32)
    c, conv_s = _conv_sample(u.reshape(ns, dseq, CONV_CH), state_conv[0], conv_w[0], cb, lng, lnb, 16)
    o, ret_s = _ret_sample(q, k, v, gs, state_ret[0], dseq, 8)
    x1_s, t_s, rec_s = _outproj(c.reshape(ns * dseq, CONV_CH), o, xs, w_out_l, g2, wr2, br, tm)

    staged, pos, meta = _dispatch(rec_p, t_p, rec_s, t_s)
    start = meta[:, 0, :N_EXPERTS].astype(jnp.int32).reshape(-1)
    cnt = meta[:, 1, :N_EXPERTS].astype(jnp.int32).reshape(-1)
    ys = _experts(start, cnt, staged, w_expert_gate[0], w_expert_up[0], w_expert_down[0])

    y_p = _combine(ys, pos, x1_p, p_prompt[0].reshape(nb * seq, PLE_DIM), w_ple_l, gp, w_pg_l, gf, 0)
    y_s = _combine(ys, pos, x1_s, p_sample[0].reshape(ns * dseq, PLE_DIM), w_ple_l, gp, w_pg_l, gf,
                   nb * seq // SUB)

    return (y_p.reshape(nb, seq, D_MODEL), y_s.reshape(ns, dseq, D_MODEL),
            conv_p[None], ret_p[None], conv_s[None], ret_s[None])
```

```python
import functools

import jax
import jax.numpy as jnp
from jax import lax
from jax.experimental import pallas as pl
from jax.experimental.pallas import tpu as pltpu

F32 = jnp.float32
BF16 = jnp.bfloat16

D_MODEL = 1024
PLE_DIM = 256
CONV_CH = 512
CONV_K = 31
RET_WIDTH = 512
RET_HEADS = 4
HEAD_DIM = 128
CHUNK = 128
ROPE_BASE = 10000.0
N_GROUPS = 4
EXPERTS_PER_GROUP = 8
N_EXPERTS = 32
EXPERT_FF = 256
IN_COLS = 3072
EPS = 1e-6
PAST_LEN = 16384

LANES = 128
SUBLANES = 8
HALO = 32
HALO_OFF = HALO - (CONV_K - 1)
VMEM_LIMIT = 48 * 1024 * 1024
MIX_VMEM_LIMIT = 56 * 1024 * 1024
MIX_TILE = 512

SUB = 256
ROW_ALIGN = 16
PBLK = 256
SUBP = -(-(2 * SUB + N_EXPERTS * (ROW_ALIGN - 1)) // PBLK) * PBLK
CHUNKS_PER_SUB = SUBP // ROW_ALIGN
ROW_W = D_MODEL + LANES
MBLK = 512
CHUNKS_PER_BLK = MBLK // ROW_ALIGN
LIST_SLACK = 2
COMBINE_SUBS = 4
DISPATCH_SUBS = 4


def _cparams(sem):
    return pltpu.CompilerParams(dimension_semantics=sem, vmem_limit_bytes=VMEM_LIMIT)


def _rms(x, g):
    return x * lax.rsqrt(jnp.mean(x * x, axis=-1, keepdims=True) + EPS) * g


def _inproj_body(x_ref, g1_ref, w_ref, cos_ref, sin_ref, u_ref, q_ref, k_ref, v_ref, gs_ref):
    h = _rms(x_ref[...], g1_ref[...]).astype(BF16)
    z = jnp.dot(h, w_ref[...], preferred_element_type=F32)
    a = z[:, :CONV_CH]
    b = z[:, CONV_CH:2 * CONV_CH]
    u_ref[...] = a * jax.nn.sigmoid(b)
    cos = cos_ref[...]
    sin = sin_ref[...]
    q0 = 2 * CONV_CH
    k0 = q0 + RET_WIDTH
    for hh in range(RET_HEADS):
        sl = slice(hh * HEAD_DIM, (hh + 1) * HEAD_DIM)
        qh = z[:, q0 + hh * HEAD_DIM:q0 + (hh + 1) * HEAD_DIM]
        kh = z[:, k0 + hh * HEAD_DIM:k0 + (hh + 1) * HEAD_DIM]
        q_ref[:, sl] = (qh * cos + pltpu.roll(qh, HEAD_DIM // 2, 1) * sin).astype(q_ref.dtype)
        kr = (kh * cos + pltpu.roll(kh, HEAD_DIM // 2, 1) * sin) * (HEAD_DIM ** -0.5)
        k_ref[:, sl] = kr.astype(k_ref.dtype)
    v_ref[...] = z[:, k0 + RET_WIDTH:k0 + 2 * RET_WIDTH].astype(v_ref.dtype)
    g = z[:, k0 + 2 * RET_WIDTH:]
    gs_ref[...] = g * jax.nn.sigmoid(g)


def _inproj(x, g1, w_in, cos, sin, tm, table_blocks, qkv_dtype):
    t = x.shape[0]
    row = lambda i: (i, 0)
    const = lambda i: (0, 0)
    tab = (lambda i: (i % table_blocks, 0)) if table_blocks > 1 else const
    return pl.pallas_call(
        _inproj_body,
        grid=(t // tm,),
        in_specs=[
            pl.BlockSpec((tm, D_MODEL), row),
            pl.BlockSpec((1, D_MODEL), const),
            pl.BlockSpec((D_MODEL, IN_COLS), const),
            pl.BlockSpec((tm, HEAD_DIM), tab),
            pl.BlockSpec((tm, HEAD_DIM), tab),
        ],
        out_specs=[pl.BlockSpec((tm, CONV_CH), row)] + [pl.BlockSpec((tm, RET_WIDTH), row)] * 4,
        out_shape=[
            jax.ShapeDtypeStruct((t, CONV_CH), F32),
            jax.ShapeDtypeStruct((t, RET_WIDTH), qkv_dtype),
            jax.ShapeDtypeStruct((t, RET_WIDTH), qkv_dtype),
            jax.ShapeDtypeStruct((t, RET_WIDTH), qkv_dtype),
            jax.ShapeDtypeStruct((t, RET_WIDTH), F32),
        ],
        compiler_params=_cparams(("parallel",)),
        name="inproj",
    )(x, g1, w_in, cos, sin)


def _ln_silu(acc, g, b):
    mu = jnp.mean(acc, axis=-1, keepdims=True)
    d = acc - mu
    var = jnp.mean(d * d, axis=-1, keepdims=True)
    y = d * lax.rsqrt(var + EPS) * g + b
    return y * jax.nn.sigmoid(y)


def _dwconv(load, w_ref, rows):
    acc = None
    for b in range(SUBLANES):
        part = None
        for a in range((CONV_K + HALO_OFF) // SUBLANES + 1):
            k = SUBLANES * a + b - HALO_OFF
            if 0 <= k < CONV_K:
                term = load(SUBLANES * a, rows + SUBLANES) * w_ref[k:k + 1, :]
                part = term if part is None else part + term
        if part is not None:
            shifted = part[b:b + rows]
            acc = shifted if acc is None else acc + shifted
    return acc


def _conv_sample_body(u_ref, st_ref, w_ref, cb_ref, lg_ref, lb_ref, c_ref, nst_ref, ext_ref):
    l = u_ref.shape[1]
    hist = CONV_K - 1
    ext_ref[0:hist] = st_ref[...]
    ext_ref[hist:] = pltpu.einshape("nlc->lnc", u_ref[...])
    acc = ext_ref[0:l] * w_ref[0:1, :] + cb_ref[...]
    for k in range(1, CONV_K):
        acc = acc + ext_ref[k:k + l] * w_ref[k:k + 1, :]
    y = _ln_silu(acc, lg_ref[...], lb_ref[...])
    c_ref[...] = pltpu.einshape("lnc->nlc", y).astype(c_ref.dtype)
    nst_ref[...] = ext_ref[l:]


def _conv_sample(u, state_t, conv_w, conv_b, ln_g, ln_b, nb):
    n, l, _ = u.shape
    hist = CONV_K - 1
    const = lambda b: (0, 0)
    tok = lambda b: (b, 0, 0)
    tmaj = lambda b: (0, b, 0)
    return pl.pallas_call(
        _conv_sample_body,
        grid=(n // nb,),
        in_specs=[
            pl.BlockSpec((nb, l, CONV_CH), tok),
            pl.BlockSpec((hist, nb, CONV_CH), tmaj),
            pl.BlockSpec((CONV_K, CONV_CH), const),
            pl.BlockSpec((1, CONV_CH), const),
            pl.BlockSpec((1, CONV_CH), const),
            pl.BlockSpec((1, CONV_CH), const),
        ],
        out_specs=[
            pl.BlockSpec((nb, l, CONV_CH), tok),
            pl.BlockSpec((hist, nb, CONV_CH), tmaj),
        ],
        out_shape=[
            jax.ShapeDtypeStruct((n, l, CONV_CH), BF16),
            jax.ShapeDtypeStruct((hist, n, CONV_CH), F32),
        ],
        scratch_shapes=[pltpu.VMEM((hist + l, nb, CONV_CH), F32)],
        compiler_params=_cparams(("parallel",)),
        name="conv_sample",
    )(u, state_t, conv_w, conv_b, ln_g, ln_b)


def _decay_tables(c):
    lg = jnp.log(1.0 - 2.0 ** (-5.0 - jnp.arange(RET_HEADS, dtype=F32)))
    idx = jnp.arange(c, dtype=F32)
    rel = idx[:, None] - idx[None, :]
    dmat = jnp.where(rel[None] >= 0, jnp.exp(jnp.maximum(rel, 0.0)[None] * lg[:, None, None]), 0.0)
    xi = jnp.exp((idx + 1.0)[None, :] * lg[:, None])
    zeta = jnp.exp((c - 1.0 - idx)[None, :] * lg[:, None])
    gc = jnp.exp(c * lg)
    xi_b = jnp.broadcast_to(xi[:, :, None], (RET_HEADS, c, HEAD_DIM))
    zeta_b = jnp.broadcast_to(zeta[:, :, None], (RET_HEADS, c, HEAD_DIM))
    gc_b = jnp.broadcast_to(gc[:, None, None], (RET_HEADS, 1, HEAD_DIM))
    return dmat, xi_b, zeta_b, gc_b


def _group_norm(o):
    mu = jnp.mean(o, axis=-1, keepdims=True)
    d = o - mu
    var = jnp.mean(d * d, axis=-1, keepdims=True)
    return d * lax.rsqrt(var + EPS)


def _ret_chunk(qh, kh, vh, r, dmat, xi, zeta, gc):
    qb = qh.astype(BF16)
    kb = kh.astype(BF16)
    vb = vh.astype(BF16)
    s = lax.dot_general(qb, kb, (((1,), (1,)), ((), ())), preferred_element_type=F32) * dmat
    o = jnp.dot(s.astype(BF16), vb, preferred_element_type=F32)
    o = o + jnp.dot(qb, r.astype(BF16), preferred_element_type=F32) * xi
    kz = (kh.astype(F32) * zeta).astype(BF16)
    r_new = r * gc + lax.dot_general(kz, vb, (((0,), (0,)), ((), ())), preferred_element_type=F32)
    return o, r_new


def _ret_sample_body(q_ref, k_ref, v_ref, gs_ref, st_ref, d_ref, xi_ref, zeta_ref, gc_ref, o_ref, nst_ref):
    nb = st_ref.shape[0]
    l = q_ref.shape[0] // nb
    for b in range(nb):
        rows = slice(b * l, (b + 1) * l)
        for hh in range(RET_HEADS):
            sl = slice(hh * HEAD_DIM, (hh + 1) * HEAD_DIM)
            o, r = _ret_chunk(q_ref[rows, sl], k_ref[rows, sl], v_ref[rows, sl], st_ref[b, hh],
                              d_ref[hh], xi_ref[hh], zeta_ref[hh], gc_ref[hh])
            o_ref[rows, sl] = (gs_ref[rows, sl] * _group_norm(o)).astype(o_ref.dtype)
            nst_ref[b, hh] = r


def _ret_sample(q, k, v, gs, state, l, nb):
    n = state.shape[0]
    dmat, xi, zeta, gc = _decay_tables(l)
    row = lambda b: (b, 0)
    c3 = lambda b: (0, 0, 0)
    blk4 = lambda b: (b, 0, 0, 0)
    return pl.pallas_call(
        _ret_sample_body,
        grid=(n // nb,),
        in_specs=[pl.BlockSpec((nb * l, RET_WIDTH), row)] * 4 + [
            pl.BlockSpec((nb, RET_HEADS, HEAD_DIM, HEAD_DIM), blk4),
            pl.BlockSpec((RET_HEADS, l, l), c3),
            pl.BlockSpec((RET_HEADS, l, HEAD_DIM), c3),
            pl.BlockSpec((RET_HEADS, l, HEAD_DIM), c3),
            pl.BlockSpec((RET_HEADS, 1, HEAD_DIM), c3),
        ],
        out_specs=[
            pl.BlockSpec((nb * l, RET_WIDTH), row),
            pl.BlockSpec((nb, RET_HEADS, HEAD_DIM, HEAD_DIM), blk4),
        ],
        out_shape=[
            jax.ShapeDtypeStruct((n * l, RET_WIDTH), BF16),
            jax.ShapeDtypeStruct((n, RET_HEADS, HEAD_DIM, HEAD_DIM), F32),
        ],
        compiler_params=_cparams(("parallel",)),
        name="ret_sample",
    )(q, k, v, gs, state, dmat, xi, zeta, gc)


def _split3(x):
    hi = x.astype(BF16)
    r1 = x - hi.astype(F32)
    mid = r1.astype(BF16)
    lo = (r1 - mid.astype(F32)).astype(BF16)
    return hi, mid, lo


def _dot_hp(t, w2_ref):
    t_hi, t_mid, _ = _split3(t)
    d = functools.partial(jnp.dot, preferred_element_type=F32)
    both = d(t_hi, w2_ref[...])
    return both[:, :LANES] + (d(t_mid, w2_ref[:, 0:LANES]) + both[:, LANES:])


def _iota_f32(shape, dim):
    return lax.broadcasted_iota(jnp.int32, shape, dim).astype(F32)


def _route(logits):
    lt = logits.T
    tm = lt.shape[1]
    row = _iota_f32((SUBLANES, tm), 0)
    big = float(SUBLANES)

    def rmax(x):
        return jnp.max(x, axis=0, keepdims=True)

    def first_row(mask):
        return jnp.min(jnp.where(mask, row, big), axis=0, keepdims=True)

    lg = jnp.where(row < float(N_GROUPS), lt[N_EXPERTS:N_EXPERTS + SUBLANES, :], -1e30)
    m = rmax(lg)
    g_top = 1.0 / jnp.sum(jnp.exp(lg - m), axis=0, keepdims=True)
    g_idx = first_row(lg == m)
    lem = lt[0:EXPERTS_PER_GROUP, :]
    for g in range(1, N_GROUPS):
        lem = jnp.where(g_idx == float(g), lt[g * EXPERTS_PER_GROUP:(g + 1) * EXPERTS_PER_GROUP, :], lem)
    pe = jnp.exp(lem - rmax(lem))
    p1 = rmax(pe)
    e1 = first_row(pe == p1)
    rest = row != e1
    pe2 = jnp.where(rest, pe, -1.0)
    p2 = rmax(pe2)
    e2 = first_row(rest & (pe2 == p2))
    scale = g_top / (p1 + p2)
    base = g_idx * float(EXPERTS_PER_GROUP)
    rec_t = jnp.where(row == 0.0, base + e1, jnp.where(row == 1.0, base + e2, 0.0))
    rec_t = rec_t + jnp.where(row == 2.0, p1 * scale, jnp.where(row == 3.0, p2 * scale, 0.0))
    return jnp.concatenate([rec_t, jnp.zeros((LANES - SUBLANES, tm), F32)], axis=0).T


def _outproj_body(c_ref, o_ref, x_ref, wo_ref, g2_ref, wr_ref, br_ref, x1_ref, t_ref, rec_ref):
    x1 = x_ref[...] + jnp.dot(c_ref[...], wo_ref[0:CONV_CH, :], preferred_element_type=F32)
    x1 = x1 + jnp.dot(o_ref[...], wo_ref[CONV_CH:, :], preferred_element_type=F32)
    x1_ref[...] = x1
    t = _rms(x1, g2_ref[...])
    t_ref[...] = t.astype(t_ref.dtype)
    rec_ref[...] = _route(_dot_hp(t, wr_ref) + br_ref[...])


def _outproj(c, o, x, w_out, g2, wr2, br, tm):
    t = x.shape[0]
    row = lambda i: (i, 0)
    const = lambda i: (0, 0)
    return pl.pallas_call(
        _outproj_body,
        grid=(t // tm,),
        in_specs=[
            pl.BlockSpec((tm, CONV_CH), row),
            pl.BlockSpec((tm, RET_WIDTH), row),
            pl.BlockSpec((tm, D_MODEL), row),
            pl.BlockSpec((D_MODEL, D_MODEL), const),
            pl.BlockSpec((1, D_MODEL), const),
            pl.BlockSpec((D_MODEL, 2 * LANES), const),
            pl.BlockSpec((1, LANES), const),
        ],
        out_specs=[
            pl.BlockSpec((tm, D_MODEL), row),
            pl.BlockSpec((tm, D_MODEL), row),
            pl.BlockSpec((tm, LANES), row),
        ],
        out_shape=[
            jax.ShapeDtypeStruct((t, D_MODEL), F32),
            jax.ShapeDtypeStruct((t, D_MODEL), BF16),
            jax.ShapeDtypeStruct((t, LANES), F32),
        ],
        compiler_params=_cparams(("parallel",)),
        name="outproj_router",
    )(c, o, x, w_out, g2, wr2, br)


def _mix_body(x_ref, g1_ref, w_ref, cos_ref, sin_ref, cw_ref, cb_ref, lg_ref, lb_ref,
              d_ref, xi_ref, zeta_ref, gc_ref, wo_ref, g2_ref, wr_ref, br_ref,
              x1_ref, t_ref, rec_ref, cst_ref, rst_ref, ext_ref, r_ref, o_ref):
    j = pl.program_id(1)
    tl = x_ref.shape[0]
    x = x_ref[...]
    z = jnp.dot(_rms(x, g1_ref[...]).astype(BF16), w_ref[...], preferred_element_type=F32)

    @pl.when(j == 0)
    def _():
        ext_ref[0:HALO, :] = jnp.zeros((HALO, CONV_CH), F32)
        ext_ref[tl + HALO:, :] = jnp.zeros((SUBLANES, CONV_CH), F32)
        r_ref[...] = jnp.zeros_like(r_ref)

    @pl.when(j > 0)
    def _():
        ext_ref[0:HALO, :] = ext_ref[tl:tl + HALO, :]

    ext_ref[HALO:tl + HALO, :] = z[:, :CONV_CH] * jax.nn.sigmoid(z[:, CONV_CH:2 * CONV_CH])
    acc = _dwconv(lambda s, n: ext_ref[s:s + n, :], cw_ref, tl) + cb_ref[...]
    c = _ln_silu(acc, lg_ref[...], lb_ref[...]).astype(BF16)
    cst_ref[0] = ext_ref[tl + HALO_OFF:tl + HALO, :]

    cos = cos_ref[...]
    sin = sin_ref[...]
    q0 = 2 * CONV_CH
    k0 = q0 + RET_WIDTH
    v0 = k0 + RET_WIDTH
    g0 = v0 + RET_WIDTH
    for hh in range(RET_HEADS):
        lo = hh * HEAD_DIM
        qh = z[:, q0 + lo:q0 + lo + HEAD_DIM]
        kh = z[:, k0 + lo:k0 + lo + HEAD_DIM]
        qr = (qh * cos + pltpu.roll(qh, HEAD_DIM // 2, 1) * sin).astype(BF16)
        kr = ((kh * cos + pltpu.roll(kh, HEAD_DIM // 2, 1) * sin) * (HEAD_DIM ** -0.5)).astype(BF16)
        vh = z[:, v0 + lo:v0 + lo + HEAD_DIM].astype(BF16)
        g = z[:, g0 + lo:g0 + lo + HEAD_DIM]
        gs = g * jax.nn.sigmoid(g)
        r = r_ref[hh]
        for ci in range(tl // CHUNK):
            rows = slice(ci * CHUNK, (ci + 1) * CHUNK)
            o, r = _ret_chunk(qr[rows], kr[rows], vh[rows], r, d_ref[hh], xi_ref[hh], zeta_ref[hh], gc_ref[hh])
            o_ref[rows, lo:lo + HEAD_DIM] = (gs[rows] * _group_norm(o)).astype(BF16)
        r_ref[hh] = r
    rst_ref[0] = r_ref[...]

    x1 = x + jnp.dot(c, wo_ref[0:CONV_CH, :], preferred_element_type=F32)
    x1 = x1 + jnp.dot(o_ref[...], wo_ref[CONV_CH:, :], preferred_element_type=F32)
    x1_ref[...] = x1
    t = _rms(x1, g2_ref[...])
    t_ref[...] = t.astype(t_ref.dtype)
    rec_ref[...] = _route(_dot_hp(t, wr_ref) + br_ref[...])


def _mix(x, g1, w_in, cos, sin, conv_w, conv_b, ln_g, ln_b, w_out, g2, wr2, br, n, l, tl):
    dmat, xi, zeta, gc = _decay_tables(CHUNK)
    per = l // tl
    row = lambda b, j: (b * per + j, 0)
    tab = lambda b, j: (j, 0)
    const = lambda b, j: (0, 0)
    c3 = lambda b, j: (0, 0, 0)
    once = dict(pipeline_mode=pl.Buffered(1))
    return pl.pallas_call(
        _mix_body,
        grid=(n, per),
        in_specs=[
            pl.BlockSpec((tl, D_MODEL), row),
            pl.BlockSpec((1, D_MODEL), const),
            pl.BlockSpec((D_MODEL, IN_COLS), const, **once),
            pl.BlockSpec((tl, HEAD_DIM), tab),
            pl.BlockSpec((tl, HEAD_DIM), tab),
            pl.BlockSpec((CONV_K, CONV_CH), const),
            pl.BlockSpec((1, CONV_CH), const),
            pl.BlockSpec((1, CONV_CH), const),
            pl.BlockSpec((1, CONV_CH), const),
            pl.BlockSpec((RET_HEADS, CHUNK, CHUNK), c3),
            pl.BlockSpec((RET_HEADS, CHUNK, HEAD_DIM), c3),
            pl.BlockSpec((RET_HEADS, CHUNK, HEAD_DIM), c3),
            pl.BlockSpec((RET_HEADS, 1, HEAD_DIM), c3),
            pl.BlockSpec((D_MODEL, D_MODEL), const, **once),
            pl.BlockSpec((1, D_MODEL), const),
            pl.BlockSpec((D_MODEL, 2 * LANES), const, **once),
            pl.BlockSpec((1, LANES), const),
        ],
        out_specs=[
            pl.BlockSpec((tl, D_MODEL), row),
            pl.BlockSpec((tl, D_MODEL), row),
            pl.BlockSpec((tl, LANES), row),
            pl.BlockSpec((1, CONV_K - 1, CONV_CH), lambda b, j: (b, 0, 0)),
            pl.BlockSpec((1, RET_HEADS, HEAD_DIM, HEAD_DIM), lambda b, j: (b, 0, 0, 0)),
        ],
        out_shape=[
            jax.ShapeDtypeStruct((n * l, D_MODEL), F32),
            jax.ShapeDtypeStruct((n * l, D_MODEL), BF16),
            jax.ShapeDtypeStruct((n * l, LANES), F32),
            jax.ShapeDtypeStruct((n, CONV_K - 1, CONV_CH), F32),
            jax.ShapeDtypeStruct((n, RET_HEADS, HEAD_DIM, HEAD_DIM), F32),
        ],
        scratch_shapes=[
            pltpu.VMEM((tl + HALO + SUBLANES, CONV_CH), F32),
            pltpu.VMEM((RET_HEADS, HEAD_DIM, HEAD_DIM), F32),
            pltpu.VMEM((tl, RET_WIDTH), BF16),
        ],
        compiler_params=pltpu.CompilerParams(dimension_semantics=("arbitrary", "arbitrary"),
                                             vmem_limit_bytes=MIX_VMEM_LIMIT),
        name="token_mix",
    )(x, g1, w_in, cos, sin, conv_w, conv_b, ln_g, ln_b, dmat, xi, zeta, gc, w_out, g2, wr2, br)


def _dispatch_body(rec_a_ref, t_a_ref, rec_b_ref, t_b_ref, s_ref, pos_ref, meta_ref, *, nsub_a):
    from_a = pl.program_id(0) * DISPATCH_SUBS < nsub_a
    for s in range(DISPATCH_SUBS):
        rows = slice(s * SUB, (s + 1) * SUB)
        rec = jnp.where(from_a, rec_a_ref[rows, :], rec_b_ref[rows, :])
        tok = jnp.where(from_a, t_a_ref[rows, :], t_b_ref[rows, :])
        chunks = slice(s * CHUNKS_PER_SUB, (s + 1) * CHUNKS_PER_SUB)
        _dispatch_sub_tile(rec, tok, s_ref.at[chunks], pos_ref.at[rows], meta_ref.at[s])


def _dispatch_sub_tile(rec, tok, s_ref, pos_ref, meta_ref):
    lane = _iota_f32(rec.shape, 1)
    a1 = lane == rec[:, 0:1]
    a2 = lane == rec[:, 1:2]
    a1f = jnp.where(a1, 1.0, 0.0)
    a2f = jnp.where(a2, 1.0, 0.0)
    ltri = jnp.where(_iota_f32((SUB, SUB), 1) < _iota_f32((SUB, SUB), 0), 1.0, 0.0).astype(BF16)
    c1 = jnp.dot(ltri, a1f.astype(BF16), preferred_element_type=F32)
    c2 = jnp.dot(ltri, a2f.astype(BF16), preferred_element_type=F32)
    n1 = jnp.sum(a1f, axis=0, keepdims=True)
    n2 = jnp.sum(a2f, axis=0, keepdims=True)
    cnt = jnp.floor((n1 + n2 + (ROW_ALIGN - 1.0)) * (1.0 / ROW_ALIGN))
    utri = jnp.where(_iota_f32((LANES, LANES), 0) < _iota_f32((LANES, LANES), 1), 1.0, 0.0).astype(BF16)
    start = jnp.dot(jnp.broadcast_to(cnt, (SUBLANES, LANES)).astype(BF16), utri,
                    preferred_element_type=F32)[0:1]
    base1 = start * ROW_ALIGN
    base2 = base1 + n1
    pos1 = jnp.sum(jnp.where(a1, c1 + base1, 0.0), axis=1, keepdims=True)
    pos2 = jnp.sum(jnp.where(a2, c2 + base2, 0.0), axis=1, keepdims=True)
    posm = jnp.where(lane == 0.0, pos1, jnp.where(lane == 1.0, pos2, 0.0))
    pos_ref[...] = posm
    row = _iota_f32((SUBLANES, LANES), 0)
    meta_ref[...] = jnp.where(row == 0.0, start, jnp.where(row == 1.0, cnt, 0.0))

    g1 = _split3(rec[:, 2:3])
    g2 = _split3(rec[:, 3:4])
    info = jnp.where(lane == 6.0, rec[:, 0:1], jnp.where(lane == 7.0, rec[:, 1:2], 0.0))
    for i in range(3):
        info = jnp.where(lane == float(i), g1[i].astype(F32), info)
        info = jnp.where(lane == float(3 + i), g2[i].astype(F32), info)
    src = jnp.concatenate([tok, info.astype(BF16)], axis=1)

    post = posm.T
    r = _iota_f32((SUBP, SUB), 0)
    onehot = jnp.where(r == post[0:1, :], 1.0, jnp.where(r == post[1:2, :], 1.0, 0.0)).astype(BF16)
    sorted_rows = jnp.dot(onehot, src, preferred_element_type=F32).astype(BF16)
    s_ref[...] = sorted_rows.reshape(CHUNKS_PER_SUB, ROW_ALIGN, ROW_W)


def _dispatch(rec_a, t_a, rec_b, t_b):
    nsub_a = rec_a.shape[0] // SUB
    nsub_b = rec_b.shape[0] // SUB
    nsub = nsub_a + nsub_b
    assert nsub_a % DISPATCH_SUBS == 0 and nsub_b % DISPATCH_SUBS == 0
    steps_a = nsub_a // DISPATCH_SUBS
    tm = DISPATCH_SUBS * SUB
    row = lambda i: (i, 0)
    row_a = lambda i: (jnp.minimum(i, steps_a - 1), 0)
    row_b = lambda i: (jnp.maximum(i - steps_a, 0), 0)
    return pl.pallas_call(
        functools.partial(_dispatch_body, nsub_a=nsub_a),
        grid=(nsub // DISPATCH_SUBS,),
        in_specs=[
            pl.BlockSpec((tm, LANES), row_a),
            pl.BlockSpec((tm, D_MODEL), row_a),
            pl.BlockSpec((tm, LANES), row_b),
            pl.BlockSpec((tm, D_MODEL), row_b),
        ],
        out_specs=[
            pl.BlockSpec((DISPATCH_SUBS * CHUNKS_PER_SUB, ROW_ALIGN, ROW_W), lambda i: (i, 0, 0)),
            pl.BlockSpec((tm, LANES), row),
            pl.BlockSpec((DISPATCH_SUBS, SUBLANES, LANES), lambda i: (i, 0, 0)),
        ],
        out_shape=[
            jax.ShapeDtypeStruct((nsub * CHUNKS_PER_SUB, ROW_ALIGN, ROW_W), BF16),
            jax.ShapeDtypeStruct((nsub * SUB, LANES), F32),
            jax.ShapeDtypeStruct((nsub, SUBLANES, LANES), F32),
        ],
        compiler_params=_cparams(("parallel",)),
        name="moe_dispatch",
    )(rec_a, t_a, rec_b, t_b)


def _experts_body(start_ref, cnt_ref, s_in, wg_ref, wu_ref, wd_ref, s_hbm,
                  xbuf, ybuf, wgu_ref, wdb_ref, gsem, ssem, list_ref, state_ref, *, nsub):
    del s_in
    e = pl.program_id(0)
    ne = pl.num_programs(0)
    par = e & 1

    list_max = list_ref.shape[0] // 2

    def gather_copy(src, i, slot):
        return pltpu.make_async_copy(s_hbm.at[src], xbuf.at[slot, i], gsem.at[slot])

    def scatter_copy(dst, i, slot):
        return pltpu.make_async_copy(ybuf.at[slot, i], s_hbm.at[dst], ssem.at[slot])

    def build_list(x, which):
        def per_sub(s, k):
            run = s * N_EXPERTS + x
            c = cnt_ref[run]
            base = s * CHUNKS_PER_SUB + start_ref[run]
            list_ref[k] = base
            list_ref[k + 1] = base + 1

            def per_chunk(i, carry):
                list_ref[k + i] = base + i
                return carry
            lax.fori_loop(2, c, per_chunk, 0)
            return k + c
        first = which * list_max
        state_ref[which] = lax.fori_loop(0, nsub, per_sub, first) - first

    def start_all(copy, which, first, n, slot, counter):
        def body(i, carry):
            copy(list_ref[which * list_max + first + i], i, slot).start()
            return carry
        lax.fori_loop(0, n, body, 0)
        state_ref[counter] = n

    def wait_all(copy, block_copy, slot, counter):
        n = state_ref[counter]

        @pl.when(n == CHUNKS_PER_BLK)
        def _():
            block_copy(slot).wait()

        @pl.when(n < CHUNKS_PER_BLK)
        def _():
            def body(i, carry):
                copy(0, 0, slot).wait()
                return carry
            lax.fori_loop(0, n, body, 0)
        state_ref[counter] = 0

    def gather_block(slot):
        return pltpu.make_async_copy(s_hbm.at[pl.ds(0, CHUNKS_PER_BLK)], xbuf.at[slot], gsem.at[slot])

    def scatter_block(slot):
        return pltpu.make_async_copy(ybuf.at[slot], s_hbm.at[pl.ds(0, CHUNKS_PER_BLK)], ssem.at[slot])

    def block_chunks(total, b):
        return jnp.minimum(total - b * CHUNKS_PER_BLK, CHUNKS_PER_BLK)

    @pl.when(e == 0)
    def _():
        for i in range(6):
            state_ref[i] = 0
        xbuf[...] = jnp.zeros_like(xbuf)
        build_list(0, 0)
        n0 = state_ref[0]

        @pl.when(n0 > 0)
        def _():
            start_all(gather_copy, 0, 0, block_chunks(n0, 0), 0, 2)

    @pl.when(e + 1 < ne)
    def _():
        build_list(e + 1, 1 - par)

    total = state_ref[par]
    nblk = (total + CHUNKS_PER_BLK - 1) // CHUNKS_PER_BLK
    wgu_ref[:, 0:EXPERT_FF] = wg_ref[0].astype(BF16)
    wgu_ref[:, EXPERT_FF:] = wu_ref[0].astype(BF16)
    wdb_ref[...] = wd_ref[0].astype(BF16)
    ef = e.astype(F32)

    def block(b, carry):
        slot = b & 1
        first = b * CHUNKS_PER_BLK

        @pl.when(b + 1 < nblk)
        def _():
            start_all(gather_copy, par, first + CHUNKS_PER_BLK, block_chunks(total, b + 1), 1 - slot, 3 - slot)

        wait_all(gather_copy, gather_block, slot, 2 + slot)
        wait_all(scatter_copy, scatter_block, slot, 4 + slot)
        x = xbuf[slot].reshape(MBLK, ROW_W)
        info = x[:, D_MODEL:].astype(F32)
        g_first = info[:, 0:1] + info[:, 1:2] + info[:, 2:3]
        g_second = info[:, 3:4] + info[:, 4:5] + info[:, 5:6]
        gate = jnp.where(info[:, 6:7] == ef, g_first, g_second)
        h = jnp.dot(x[:, :D_MODEL], wgu_ref[...], preferred_element_type=F32)
        h1 = h[:, :EXPERT_FF]
        hid = (h1 * jax.nn.sigmoid(h1)) * h[:, EXPERT_FF:] * gate
        y = jnp.dot(hid.astype(BF16), wdb_ref[...], preferred_element_type=F32).astype(BF16)
        ybuf[slot] = jnp.concatenate([y, x[:, D_MODEL:]], axis=1).reshape(CHUNKS_PER_BLK, ROW_ALIGN, ROW_W)
        start_all(scatter_copy, par, first, block_chunks(total, b), slot, 4 + slot)
        return carry

    lax.fori_loop(0, nblk, block, 0)

    @pl.when(e + 1 < ne)
    def _():
        n1 = state_ref[1 - par]

        @pl.when(n1 > 0)
        def _():
            start_all(gather_copy, 1 - par, 0, block_chunks(n1, 0), 0, 2)

    @pl.when(e == ne - 1)
    def _():
        wait_all(scatter_copy, scatter_block, 0, 4)
        wait_all(scatter_copy, scatter_block, 1, 5)


def _experts(start, cnt, staged, wg, wu, wd):
    nsub = staged.shape[0] // CHUNKS_PER_SUB
    list_max = nsub * SUB // ROW_ALIGN + nsub + LIST_SLACK
    wblk = lambda e, *_: (e, 0, 0)
    grid_spec = pltpu.PrefetchScalarGridSpec(
        num_scalar_prefetch=2,
        grid=(N_EXPERTS,),
        in_specs=[
            pl.BlockSpec(memory_space=pl.ANY),
            pl.BlockSpec((1, D_MODEL, EXPERT_FF), wblk),
            pl.BlockSpec((1, D_MODEL, EXPERT_FF), wblk),
            pl.BlockSpec((1, EXPERT_FF, D_MODEL), wblk),
        ],
        out_specs=pl.BlockSpec(memory_space=pl.ANY),
        scratch_shapes=[
            pltpu.VMEM((2, CHUNKS_PER_BLK, ROW_ALIGN, ROW_W), BF16),
            pltpu.VMEM((2, CHUNKS_PER_BLK, ROW_ALIGN, ROW_W), BF16),
            pltpu.VMEM((D_MODEL, 2 * EXPERT_FF), BF16),
            pltpu.VMEM((EXPERT_FF, D_MODEL), BF16),
            pltpu.SemaphoreType.DMA((2,)),
            pltpu.SemaphoreType.DMA((2,)),
            pltpu.SMEM((2 * list_max,), jnp.int32),
            pltpu.SMEM((6,), jnp.int32),
        ],
    )
    return pl.pallas_call(
        functools.partial(_experts_body, nsub=nsub),
        grid_spec=grid_spec,
        out_shape=jax.ShapeDtypeStruct(staged.shape, staged.dtype),
        input_output_aliases={2: 0},
        compiler_params=_cparams(("arbitrary",)),
        name="moe_experts",
    )(start, cnt, staged, wg, wu, wd)


def _combine_body(ys_ref, pos_ref, x1_ref, p_ref, wp_ref, gp_ref, wpg_ref, gf_ref, y_ref):
    r = _iota_f32((SUB, SUBP), 1)
    moe = []
    for s in range(x1_ref.shape[0] // SUB):
        p1 = pos_ref[s * SUB:(s + 1) * SUB, 0:1]
        p2 = pos_ref[s * SUB:(s + 1) * SUB, 1:2]
        onehot = jnp.where(r == p1, 1.0, jnp.where(r == p2, 1.0, 0.0)).astype(BF16)
        ys = ys_ref[s * CHUNKS_PER_SUB:(s + 1) * CHUNKS_PER_SUB].reshape(SUBP, D_MODEL)
        moe.append(jnp.dot(onehot, ys, preferred_element_type=F32))
    x2 = x1_ref[...] + jnp.concatenate(moe, axis=0)
    ple = _rms(jnp.dot(p_ref[...].astype(BF16), wp_ref[...], preferred_element_type=F32), gp_ref[...])
    gate = jax.nn.sigmoid(jnp.dot(x2.astype(BF16), wpg_ref[...], preferred_element_type=F32))
    y_ref[...] = _rms(x2 + ple * gate, gf_ref[...])


def _combine(ys, pos, x1, p, w_ple, gp, w_ple_gate, gf, sub_off):
    t = x1.shape[0]
    tm = COMBINE_SUBS * SUB
    blk_off = sub_off // COMBINE_SUBS
    assert sub_off % COMBINE_SUBS == 0 and t % tm == 0
    row = lambda i: (i, 0)
    const = lambda i: (0, 0)
    return pl.pallas_call(
        _combine_body,
        grid=(t // tm,),
        in_specs=[
            pl.BlockSpec((COMBINE_SUBS * CHUNKS_PER_SUB, ROW_ALIGN, D_MODEL), lambda i: (i + blk_off, 0, 0)),
            pl.BlockSpec((tm, LANES), lambda i: (i + blk_off, 0)),
            pl.BlockSpec((tm, D_MODEL), row),
            pl.BlockSpec((tm, PLE_DIM), row),
            pl.BlockSpec((PLE_DIM, D_MODEL), const),
            pl.BlockSpec((1, D_MODEL), const),
            pl.BlockSpec((D_MODEL, D_MODEL), const),
            pl.BlockSpec((1, D_MODEL), const),
        ],
        out_specs=pl.BlockSpec((tm, D_MODEL), row),
        out_shape=jax.ShapeDtypeStruct((t, D_MODEL), F32),
        compiler_params=_cparams(("parallel",)),
        name="moe_combine_ple",
    )(ys, pos, x1, p, w_ple, gp, w_ple_gate, gf)


def _rope_tables(pos):
    half = HEAD_DIM // 2
    inv = ROPE_BASE ** (-jnp.arange(half, dtype=F32) / half)
    ang = pos[:, None] * inv[None, :]
    cos = jnp.cos(ang)
    sin = jnp.sin(ang)
    return jnp.concatenate([cos, cos], axis=-1), jnp.concatenate([-sin, sin], axis=-1)


def _router_params(we, be, wg, bg):
    pad = LANES - N_EXPERTS - N_GROUPS
    w = jnp.pad(jnp.concatenate([we, wg], axis=1), ((0, 0), (0, pad)))
    b = jnp.pad(jnp.concatenate([be, bg]), (0, pad))[None, :]
    return jnp.concatenate(_split3(w)[:2], axis=1), b


def kernel(x_prompt, x_sample, p_prompt, p_sample, state_conv, state_ret, w_in, conv_w, conv_b, conv_ln_g, conv_ln_b, w_out, norm1_g, norm2_g, router_group_w, router_group_b, router_expert_w, router_expert_b, w_expert_gate, w_expert_up, w_expert_down, w_ple, ple_norm_g, w_ple_gate, final_norm_g):
    assert w_in.shape[0] == 1, "single-layer trunk"
    nb, seq, _ = x_prompt.shape
    ns, dseq, _ = x_sample.shape
    tm = 512

    w_in_b = w_in[0].astype(BF16)
    w_out_b = w_out[0].astype(BF16)
    w_ple_b = w_ple[0].astype(BF16)
    w_pg_b = w_ple_gate[0].astype(BF16)
    g1 = norm1_g[0][None, :]
    g2 = norm2_g[0][None, :]
    gp = ple_norm_g[0][None, :]
    gf = final_norm_g[None, :]
    cb = conv_b[0][None, :]
    lng = conv_ln_g[0][None, :]
    lnb = conv_ln_b[0][None, :]
    wr2, br = _router_params(router_expert_w[0], router_expert_b[0], router_group_w[0], router_group_b[0])

    cos_p, sin_p = _rope_tables(jnp.arange(seq, dtype=F32) + jnp.float32(0))
    pos_s = jnp.tile(jnp.arange(dseq, dtype=F32) + jnp.float32(PAST_LEN), tm // dseq)
    cos_s, sin_s = _rope_tables(pos_s)

    xp = x_prompt.reshape(nb * seq, D_MODEL)
    x1_p, t_p, rec_p, conv_p, ret_p = _mix(xp, g1, w_in_b, cos_p, sin_p, conv_w[0], cb, lng, lnb, w_out_b, g2,
                                           wr2, br, nb, seq, MIX_TILE)

    xs = x_sample.reshape(ns * dseq, D_MODEL)
    u, q, k, v, gs = _inproj(xs, g1, w_in_b, cos_s, sin_s, tm, 1, F32)
    c, conv_s = _conv_sample(u.reshape(ns, dseq, CONV_CH), jnp.transpose(state_conv[0], (1, 0, 2)), conv_w[0], cb,
                             lng, lnb, 16)
    conv_s = jnp.transpose(conv_s, (1, 0, 2))
    o, ret_s = _ret_sample(q, k, v, gs, state_ret[0], dseq, 8)
    x1_s, t_s, rec_s = _outproj(c.reshape(ns * dseq, CONV_CH), o, xs, w_out_b, g2, wr2, br, tm)

    staged, pos, meta = _dispatch(rec_p, t_p, rec_s, t_s)
    start = meta[:, 0, :N_EXPERTS].astype(jnp.int32).reshape(-1)
    cnt = meta[:, 1, :N_EXPERTS].astype(jnp.int32).reshape(-1)
    ys = _experts(start, cnt, staged, w_expert_gate[0], w_expert_up[0], w_expert_down[0])

    y_p = _combine(ys, pos, x1_p, p_prompt[0].reshape(nb * seq, PLE_DIM), w_ple_b, gp, w_pg_b, gf, 0)
    y_s = _combine(ys, pos, x1_s, p_sample[0].reshape(ns * dseq, PLE_DIM), w_ple_b, gp, w_pg_b, gf,
                   nb * seq // SUB)

    return (y_p.reshape(nb, seq, D_MODEL), y_s.reshape(ns, dseq, D_MODEL),
            conv_p[None], ret_p[None], conv_s[None], ret_s[None])
```

```python
import functools

import jax
import jax.numpy as jnp
from jax import lax
from jax.experimental import pallas as pl
from jax.experimental.pallas import tpu as pltpu

F32 = jnp.float32
BF16 = jnp.bfloat16

D_MODEL = 1024
PLE_DIM = 256
CONV_CH = 512
CONV_K = 31
RET_WIDTH = 512
RET_HEADS = 4
HEAD_DIM = 128
CHUNK = 128
ROPE_BASE = 10000.0
N_GROUPS = 4
EXPERTS_PER_GROUP = 8
N_EXPERTS = 32
EXPERT_FF = 256
IN_COLS = 3072
EPS = 1e-6
PAST_LEN = 16384

LANES = 128
SUBLANES = 8
HALO = 32
HALO_OFF = HALO - (CONV_K - 1)
VMEM_LIMIT = 48 * 1024 * 1024
MIX_VMEM_LIMIT = 56 * 1024 * 1024
MIX_TILE = 512

SUB = 256
ROW_ALIGN = 16
PBLK = 256
SUBP = -(-(2 * SUB + N_EXPERTS * (ROW_ALIGN - 1)) // PBLK) * PBLK
CHUNKS_PER_SUB = SUBP // ROW_ALIGN
ROW_W = D_MODEL + LANES
MBLK = 512
CHUNKS_PER_BLK = MBLK // ROW_ALIGN
LIST_SLACK = 2
COMBINE_SUBS = 4
DISPATCH_SUBS = 4


def _cparams(sem):
    return pltpu.CompilerParams(dimension_semantics=sem, vmem_limit_bytes=VMEM_LIMIT)


def _rms(x, g):
    return x * lax.rsqrt(jnp.mean(x * x, axis=-1, keepdims=True) + EPS) * g


def _inproj_body(x_ref, g1_ref, w_ref, cos_ref, sin_ref, u_ref, q_ref, k_ref, v_ref, gs_ref):
    h = _rms(x_ref[...], g1_ref[...]).astype(BF16)
    z = jnp.dot(h, w_ref[...], preferred_element_type=F32)
    a = z[:, :CONV_CH]
    b = z[:, CONV_CH:2 * CONV_CH]
    u_ref[...] = a * jax.nn.sigmoid(b)
    cos = cos_ref[...]
    sin = sin_ref[...]
    q0 = 2 * CONV_CH
    k0 = q0 + RET_WIDTH
    for hh in range(RET_HEADS):
        sl = slice(hh * HEAD_DIM, (hh + 1) * HEAD_DIM)
        qh = z[:, q0 + hh * HEAD_DIM:q0 + (hh + 1) * HEAD_DIM]
        kh = z[:, k0 + hh * HEAD_DIM:k0 + (hh + 1) * HEAD_DIM]
        q_ref[:, sl] = (qh * cos + pltpu.roll(qh, HEAD_DIM // 2, 1) * sin).astype(q_ref.dtype)
        kr = (kh * cos + pltpu.roll(kh, HEAD_DIM // 2, 1) * sin) * (HEAD_DIM ** -0.5)
        k_ref[:, sl] = kr.astype(k_ref.dtype)
    v_ref[...] = z[:, k0 + RET_WIDTH:k0 + 2 * RET_WIDTH].astype(v_ref.dtype)
    g = z[:, k0 + 2 * RET_WIDTH:]
    gs_ref[...] = g * jax.nn.sigmoid(g)


def _inproj(x, g1, w_in, cos, sin, tm, table_blocks, qkv_dtype):
    t = x.shape[0]
    row = lambda i: (i, 0)
    const = lambda i: (0, 0)
    tab = (lambda i: (i % table_blocks, 0)) if table_blocks > 1 else const
    return pl.pallas_call(
        _inproj_body,
        grid=(t // tm,),
        in_specs=[
            pl.BlockSpec((tm, D_MODEL), row),
            pl.BlockSpec((1, D_MODEL), const),
            pl.BlockSpec((D_MODEL, IN_COLS), const),
            pl.BlockSpec((tm, HEAD_DIM), tab),
            pl.BlockSpec((tm, HEAD_DIM), tab),
        ],
        out_specs=[pl.BlockSpec((tm, CONV_CH), row)] + [pl.BlockSpec((tm, RET_WIDTH), row)] * 4,
        out_shape=[
            jax.ShapeDtypeStruct((t, CONV_CH), F32),
            jax.ShapeDtypeStruct((t, RET_WIDTH), qkv_dtype),
            jax.ShapeDtypeStruct((t, RET_WIDTH), qkv_dtype),
            jax.ShapeDtypeStruct((t, RET_WIDTH), qkv_dtype),
            jax.ShapeDtypeStruct((t, RET_WIDTH), F32),
        ],
        compiler_params=_cparams(("parallel",)),
        name="inproj",
    )(x, g1, w_in, cos, sin)


def _ln_silu(acc, g, b):
    mu = jnp.mean(acc, axis=-1, keepdims=True)
    d = acc - mu
    var = jnp.mean(d * d, axis=-1, keepdims=True)
    y = d * lax.rsqrt(var + EPS) * g + b
    return y * jax.nn.sigmoid(y)


def _dwconv(load, w_ref, rows):
    acc = None
    for b in range(SUBLANES):
        part = None
        for a in range((CONV_K + HALO_OFF) // SUBLANES + 1):
            k = SUBLANES * a + b - HALO_OFF
            if 0 <= k < CONV_K:
                term = load(SUBLANES * a, rows + SUBLANES) * w_ref[k:k + 1, :]
                part = term if part is None else part + term
        if part is not None:
            shifted = part[b:b + rows]
            acc = shifted if acc is None else acc + shifted
    return acc


def _conv_sample_body(u_ref, st_ref, w_ref, cb_ref, lg_ref, lb_ref, c_ref, nst_ref, ext_ref):
    l = u_ref.shape[1]
    hist = CONV_K - 1
    ext_ref[0:hist] = st_ref[...]
    ext_ref[hist:] = jnp.transpose(u_ref[...], (1, 0, 2))
    acc = ext_ref[0:l] * w_ref[0:1, :] + cb_ref[...]
    for k in range(1, CONV_K):
        acc = acc + ext_ref[k:k + l] * w_ref[k:k + 1, :]
    y = _ln_silu(acc, lg_ref[...], lb_ref[...])
    c_ref[...] = jnp.transpose(y, (1, 0, 2)).astype(c_ref.dtype)
    nst_ref[...] = ext_ref[l:]


def _conv_sample(u, state_t, conv_w, conv_b, ln_g, ln_b, nb):
    n, l, _ = u.shape
    hist = CONV_K - 1
    const = lambda b: (0, 0)
    tok = lambda b: (b, 0, 0)
    tmaj = lambda b: (0, b, 0)
    return pl.pallas_call(
        _conv_sample_body,
        grid=(n // nb,),
        in_specs=[
            pl.BlockSpec((nb, l, CONV_CH), tok),
            pl.BlockSpec((hist, nb, CONV_CH), tmaj),
            pl.BlockSpec((CONV_K, CONV_CH), const),
            pl.BlockSpec((1, CONV_CH), const),
            pl.BlockSpec((1, CONV_CH), const),
            pl.BlockSpec((1, CONV_CH), const),
        ],
        out_specs=[
            pl.BlockSpec((nb, l, CONV_CH), tok),
            pl.BlockSpec((hist, nb, CONV_CH), tmaj),
        ],
        out_shape=[
            jax.ShapeDtypeStruct((n, l, CONV_CH), BF16),
            jax.ShapeDtypeStruct((hist, n, CONV_CH), F32),
        ],
        scratch_shapes=[pltpu.VMEM((hist + l, nb, CONV_CH), F32)],
        compiler_params=_cparams(("parallel",)),
        name="conv_sample",
    )(u, state_t, conv_w, conv_b, ln_g, ln_b)


def _decay_tables(c):
    lg = jnp.log(1.0 - 2.0 ** (-5.0 - jnp.arange(RET_HEADS, dtype=F32)))
    idx = jnp.arange(c, dtype=F32)
    rel = idx[:, None] - idx[None, :]
    dmat = jnp.where(rel[None] >= 0, jnp.exp(jnp.maximum(rel, 0.0)[None] * lg[:, None, None]), 0.0)
    xi = jnp.exp((idx + 1.0)[None, :] * lg[:, None])
    zeta = jnp.exp((c - 1.0 - idx)[None, :] * lg[:, None])
    gc = jnp.exp(c * lg)
    xi_b = jnp.broadcast_to(xi[:, :, None], (RET_HEADS, c, HEAD_DIM))
    zeta_b = jnp.broadcast_to(zeta[:, :, None], (RET_HEADS, c, HEAD_DIM))
    gc_b = jnp.broadcast_to(gc[:, None, None], (RET_HEADS, 1, HEAD_DIM))
    return dmat, xi_b, zeta_b, gc_b


def _group_norm(o):
    mu = jnp.mean(o, axis=-1, keepdims=True)
    d = o - mu
    var = jnp.mean(d * d, axis=-1, keepdims=True)
    return d * lax.rsqrt(var + EPS)


def _ret_chunk(qh, kh, vh, r, dmat, xi, zeta, gc):
    qb = qh.astype(BF16)
    kb = kh.astype(BF16)
    vb = vh.astype(BF16)
    s = lax.dot_general(qb, kb, (((1,), (1,)), ((), ())), preferred_element_type=F32) * dmat
    o = jnp.dot(s.astype(BF16), vb, preferred_element_type=F32)
    o = o + jnp.dot(qb, r.astype(BF16), preferred_element_type=F32) * xi
    kz = (kh.astype(F32) * zeta).astype(BF16)
    r_new = r * gc + lax.dot_general(kz, vb, (((0,), (0,)), ((), ())), preferred_element_type=F32)
    return o, r_new


def _ret_sample_body(q_ref, k_ref, v_ref, gs_ref, st_ref, d_ref, xi_ref, zeta_ref, gc_ref, o_ref, nst_ref):
    nb = st_ref.shape[0]
    l = q_ref.shape[0] // nb
    for b in range(nb):
        rows = slice(b * l, (b + 1) * l)
        for hh in range(RET_HEADS):
            sl = slice(hh * HEAD_DIM, (hh + 1) * HEAD_DIM)
            o, r = _ret_chunk(q_ref[rows, sl], k_ref[rows, sl], v_ref[rows, sl], st_ref[b, hh],
                              d_ref[hh], xi_ref[hh], zeta_ref[hh], gc_ref[hh])
            o_ref[rows, sl] = (gs_ref[rows, sl] * _group_norm(o)).astype(o_ref.dtype)
            nst_ref[b, hh] = r


def _ret_sample(q, k, v, gs, state, l, nb):
    n = state.shape[0]
    dmat, xi, zeta, gc = _decay_tables(l)
    row = lambda b: (b, 0)
    c3 = lambda b: (0, 0, 0)
    blk4 = lambda b: (b, 0, 0, 0)
    return pl.pallas_call(
        _ret_sample_body,
        grid=(n // nb,),
        in_specs=[pl.BlockSpec((nb * l, RET_WIDTH), row)] * 4 + [
            pl.BlockSpec((nb, RET_HEADS, HEAD_DIM, HEAD_DIM), blk4),
            pl.BlockSpec((RET_HEADS, l, l), c3),
            pl.BlockSpec((RET_HEADS, l, HEAD_DIM), c3),
            pl.BlockSpec((RET_HEADS, l, HEAD_DIM), c3),
            pl.BlockSpec((RET_HEADS, 1, HEAD_DIM), c3),
        ],
        out_specs=[
            pl.BlockSpec((nb * l, RET_WIDTH), row),
            pl.BlockSpec((nb, RET_HEADS, HEAD_DIM, HEAD_DIM), blk4),
        ],
        out_shape=[
            jax.ShapeDtypeStruct((n * l, RET_WIDTH), BF16),
            jax.ShapeDtypeStruct((n, RET_HEADS, HEAD_DIM, HEAD_DIM), F32),
        ],
        compiler_params=_cparams(("parallel",)),
        name="ret_sample",
    )(q, k, v, gs, state, dmat, xi, zeta, gc)


def _split3(x):
    hi = x.astype(BF16)
    r1 = x - hi.astype(F32)
    mid = r1.astype(BF16)
    lo = (r1 - mid.astype(F32)).astype(BF16)
    return hi, mid, lo


def _dot_hp(t, w2_ref):
    t_hi, t_mid, _ = _split3(t)
    d = functools.partial(jnp.dot, preferred_element_type=F32)
    both = d(t_hi, w2_ref[...])
    return both[:, :LANES] + (d(t_mid, w2_ref[:, 0:LANES]) + both[:, LANES:])


def _iota_f32(shape, dim):
    return lax.broadcasted_iota(jnp.int32, shape, dim).astype(F32)


def _route(logits):
    lt = logits.T
    tm = lt.shape[1]
    row = _iota_f32((SUBLANES, tm), 0)
    big = float(SUBLANES)

    def rmax(x):
        return jnp.max(x, axis=0, keepdims=True)

    def first_row(mask):
        return jnp.min(jnp.where(mask, row, big), axis=0, keepdims=True)

    lg = jnp.where(row < float(N_GROUPS), lt[N_EXPERTS:N_EXPERTS + SUBLANES, :], -1e30)
    m = rmax(lg)
    g_top = 1.0 / jnp.sum(jnp.exp(lg - m), axis=0, keepdims=True)
    g_idx = first_row(lg == m)
    lem = lt[0:EXPERTS_PER_GROUP, :]
    for g in range(1, N_GROUPS):
        lem = jnp.where(g_idx == float(g), lt[g * EXPERTS_PER_GROUP:(g + 1) * EXPERTS_PER_GROUP, :], lem)
    pe = jnp.exp(lem - rmax(lem))
    p1 = rmax(pe)
    e1 = first_row(pe == p1)
    rest = row != e1
    pe2 = jnp.where(rest, pe, -1.0)
    p2 = rmax(pe2)
    e2 = first_row(rest & (pe2 == p2))
    scale = g_top / (p1 + p2)
    base = g_idx * float(EXPERTS_PER_GROUP)
    rec_t = jnp.where(row == 0.0, base + e1, jnp.where(row == 1.0, base + e2, 0.0))
    rec_t = rec_t + jnp.where(row == 2.0, p1 * scale, jnp.where(row == 3.0, p2 * scale, 0.0))
    return jnp.concatenate([rec_t, jnp.zeros((LANES - SUBLANES, tm), F32)], axis=0).T


def _outproj_body(c_ref, o_ref, x_ref, wo_ref, g2_ref, wr_ref, br_ref, x1_ref, t_ref, rec_ref):
    x1 = x_ref[...] + jnp.dot(c_ref[...], wo_ref[0:CONV_CH, :], preferred_element_type=F32)
    x1 = x1 + jnp.dot(o_ref[...], wo_ref[CONV_CH:, :], preferred_element_type=F32)
    x1_ref[...] = x1
    t = _rms(x1, g2_ref[...])
    t_ref[...] = t.astype(t_ref.dtype)
    rec_ref[...] = _route(_dot_hp(t, wr_ref) + br_ref[...])


def _outproj(c, o, x, w_out, g2, wr2, br, tm):
    t = x.shape[0]
    row = lambda i: (i, 0)
    const = lambda i: (0, 0)
    return pl.pallas_call(
        _outproj_body,
        grid=(t // tm,),
        in_specs=[
            pl.BlockSpec((tm, CONV_CH), row),
            pl.BlockSpec((tm, RET_WIDTH), row),
            pl.BlockSpec((tm, D_MODEL), row),
            pl.BlockSpec((D_MODEL, D_MODEL), const),
            pl.BlockSpec((1, D_MODEL), const),
            pl.BlockSpec((D_MODEL, 2 * LANES), const),
            pl.BlockSpec((1, LANES), const),
        ],
        out_specs=[
            pl.BlockSpec((tm, D_MODEL), row),
            pl.BlockSpec((tm, D_MODEL), row),
            pl.BlockSpec((tm, LANES), row),
        ],
        out_shape=[
            jax.ShapeDtypeStruct((t, D_MODEL), F32),
            jax.ShapeDtypeStruct((t, D_MODEL), BF16),
            jax.ShapeDtypeStruct((t, LANES), F32),
        ],
        compiler_params=_cparams(("parallel",)),
        name="outproj_router",
    )(c, o, x, w_out, g2, wr2, br)


def _mix_body(x_ref, g1_ref, w_ref, cos_ref, sin_ref, cw_ref, cb_ref, lg_ref, lb_ref,
              d_ref, xi_ref, zeta_ref, gc_ref, wo_ref, g2_ref, wr_ref, br_ref,
              x1_ref, t_ref, rec_ref, cst_ref, rst_ref, ext_ref, r_ref, o_ref):
    j = pl.program_id(1)
    tl = x_ref.shape[0]
    x = x_ref[...]
    z = jnp.dot(_rms(x, g1_ref[...]).astype(BF16), w_ref[...], preferred_element_type=F32)

    @pl.when(j == 0)
    def _():
        ext_ref[0:HALO, :] = jnp.zeros((HALO, CONV_CH), F32)
        ext_ref[tl + HALO:, :] = jnp.zeros((SUBLANES, CONV_CH), F32)
        r_ref[...] = jnp.zeros_like(r_ref)

    @pl.when(j > 0)
    def _():
        ext_ref[0:HALO, :] = ext_ref[tl:tl + HALO, :]

    ext_ref[HALO:tl + HALO, :] = z[:, :CONV_CH] * jax.nn.sigmoid(z[:, CONV_CH:2 * CONV_CH])
    acc = _dwconv(lambda s, n: ext_ref[s:s + n, :], cw_ref, tl) + cb_ref[...]
    c = _ln_silu(acc, lg_ref[...], lb_ref[...]).astype(BF16)
    cst_ref[0] = ext_ref[tl + HALO_OFF:tl + HALO, :]

    cos = cos_ref[...]
    sin = sin_ref[...]
    q0 = 2 * CONV_CH
    k0 = q0 + RET_WIDTH
    v0 = k0 + RET_WIDTH
    g0 = v0 + RET_WIDTH
    for hh in range(RET_HEADS):
        lo = hh * HEAD_DIM
        qh = z[:, q0 + lo:q0 + lo + HEAD_DIM]
        kh = z[:, k0 + lo:k0 + lo + HEAD_DIM]
        qr = (qh * cos + pltpu.roll(qh, HEAD_DIM // 2, 1) * sin).astype(BF16)
        kr = ((kh * cos + pltpu.roll(kh, HEAD_DIM // 2, 1) * sin) * (HEAD_DIM ** -0.5)).astype(BF16)
        vh = z[:, v0 + lo:v0 + lo + HEAD_DIM].astype(BF16)
        g = z[:, g0 + lo:g0 + lo + HEAD_DIM]
        gs = g * jax.nn.sigmoid(g)
        r = r_ref[hh]
        for ci in range(tl // CHUNK):
            rows = slice(ci * CHUNK, (ci + 1) * CHUNK)
            o, r = _ret_chunk(qr[rows], kr[rows], vh[rows], r, d_ref[hh], xi_ref[hh], zeta_ref[hh], gc_ref[hh])
            o_ref[rows, lo:lo + HEAD_DIM] = (gs[rows] * _group_norm(o)).astype(BF16)
        r_ref[hh] = r
    rst_ref[0] = r_ref[...]

    x1 = x + jnp.dot(c, wo_ref[0:CONV_CH, :], preferred_element_type=F32)
    x1 = x1 + jnp.dot(o_ref[...], wo_ref[CONV_CH:, :], preferred_element_type=F32)
    x1_ref[...] = x1
    t = _rms(x1, g2_ref[...])
    t_ref[...] = t.astype(t_ref.dtype)
    rec_ref[...] = _route(_dot_hp(t, wr_ref) + br_ref[...])


def _mix(x, g1, w_in, cos, sin, conv_w, conv_b, ln_g, ln_b, w_out, g2, wr2, br, n, l, tl):
    dmat, xi, zeta, gc = _decay_tables(CHUNK)
    per = l // tl
    row = lambda b, j: (b * per + j, 0)
    tab = lambda b, j: (j, 0)
    const = lambda b, j: (0, 0)
    c3 = lambda b, j: (0, 0, 0)
    once = dict(pipeline_mode=pl.Buffered(1))
    return pl.pallas_call(
        _mix_body,
        grid=(n, per),
        in_specs=[
            pl.BlockSpec((tl, D_MODEL), row),
            pl.BlockSpec((1, D_MODEL), const),
            pl.BlockSpec((D_MODEL, IN_COLS), const, **once),
            pl.BlockSpec((tl, HEAD_DIM), tab),
            pl.BlockSpec((tl, HEAD_DIM), tab),
            pl.BlockSpec((CONV_K, CONV_CH), const),
            pl.BlockSpec((1, CONV_CH), const),
            pl.BlockSpec((1, CONV_CH), const),
            pl.BlockSpec((1, CONV_CH), const),
            pl.BlockSpec((RET_HEADS, CHUNK, CHUNK), c3),
            pl.BlockSpec((RET_HEADS, CHUNK, HEAD_DIM), c3),
            pl.BlockSpec((RET_HEADS, CHUNK, HEAD_DIM), c3),
            pl.BlockSpec((RET_HEADS, 1, HEAD_DIM), c3),
            pl.BlockSpec((D_MODEL, D_MODEL), const, **once),
            pl.BlockSpec((1, D_MODEL), const),
            pl.BlockSpec((D_MODEL, 2 * LANES), const, **once),
            pl.BlockSpec((1, LANES), const),
        ],
        out_specs=[
            pl.BlockSpec((tl, D_MODEL), row),
            pl.BlockSpec((tl, D_MODEL), row),
            pl.BlockSpec((tl, LANES), row),
            pl.BlockSpec((1, CONV_K - 1, CONV_CH), lambda b, j: (b, 0, 0)),
            pl.BlockSpec((1, RET_HEADS, HEAD_DIM, HEAD_DIM), lambda b, j: (b, 0, 0, 0)),
        ],
        out_shape=[
            jax.ShapeDtypeStruct((n * l, D_MODEL), F32),
            jax.ShapeDtypeStruct((n * l, D_MODEL), BF16),
            jax.ShapeDtypeStruct((n * l, LANES), F32),
            jax.ShapeDtypeStruct((n, CONV_K - 1, CONV_CH), F32),
            jax.ShapeDtypeStruct((n, RET_HEADS, HEAD_DIM, HEAD_DIM), F32),
        ],
        scratch_shapes=[
            pltpu.VMEM((tl + HALO + SUBLANES, CONV_CH), F32),
            pltpu.VMEM((RET_HEADS, HEAD_DIM, HEAD_DIM), F32),
            pltpu.VMEM((tl, RET_WIDTH), BF16),
        ],
        compiler_params=pltpu.CompilerParams(dimension_semantics=("arbitrary", "arbitrary"),
                                             vmem_limit_bytes=MIX_VMEM_LIMIT),
        name="token_mix",
    )(x, g1, w_in, cos, sin, conv_w, conv_b, ln_g, ln_b, dmat, xi, zeta, gc, w_out, g2, wr2, br)


def _dispatch_body(rec_a_ref, t_a_ref, rec_b_ref, t_b_ref, s_ref, pos_ref, meta_ref, *, nsub_a):
    from_a = pl.program_id(0) * DISPATCH_SUBS < nsub_a
    for s in range(DISPATCH_SUBS):
        rows = slice(s * SUB, (s + 1) * SUB)
        rec = jnp.where(from_a, rec_a_ref[rows, :], rec_b_ref[rows, :])
        tok = jnp.where(from_a, t_a_ref[rows, :], t_b_ref[rows, :])
        chunks = slice(s * CHUNKS_PER_SUB, (s + 1) * CHUNKS_PER_SUB)
        _dispatch_sub_tile(rec, tok, s_ref.at[chunks], pos_ref.at[rows], meta_ref.at[s])


def _dispatch_sub_tile(rec, tok, s_ref, pos_ref, meta_ref):
    lane = _iota_f32(rec.shape, 1)
    a1 = lane == rec[:, 0:1]
    a2 = lane == rec[:, 1:2]
    a1f = jnp.where(a1, 1.0, 0.0)
    a2f = jnp.where(a2, 1.0, 0.0)
    ltri = jnp.where(_iota_f32((SUB, SUB), 1) < _iota_f32((SUB, SUB), 0), 1.0, 0.0).astype(BF16)
    c1 = jnp.dot(ltri, a1f.astype(BF16), preferred_element_type=F32)
    c2 = jnp.dot(ltri, a2f.astype(BF16), preferred_element_type=F32)
    n1 = jnp.sum(a1f, axis=0, keepdims=True)
    n2 = jnp.sum(a2f, axis=0, keepdims=True)
    cnt = jnp.floor((n1 + n2 + (ROW_ALIGN - 1.0)) * (1.0 / ROW_ALIGN))
    utri = jnp.where(_iota_f32((LANES, LANES), 0) < _iota_f32((LANES, LANES), 1), 1.0, 0.0).astype(BF16)
    start = jnp.dot(jnp.broadcast_to(cnt, (SUBLANES, LANES)).astype(BF16), utri,
                    preferred_element_type=F32)[0:1]
    base1 = start * ROW_ALIGN
    base2 = base1 + n1
    pos1 = jnp.sum(jnp.where(a1, c1 + base1, 0.0), axis=1, keepdims=True)
    pos2 = jnp.sum(jnp.where(a2, c2 + base2, 0.0), axis=1, keepdims=True)
    posm = jnp.where(lane == 0.0, pos1, jnp.where(lane == 1.0, pos2, 0.0))
    pos_ref[...] = posm
    row = _iota_f32((SUBLANES, LANES), 0)
    meta_ref[...] = jnp.where(row == 0.0, start, jnp.where(row == 1.0, cnt, 0.0))

    g1 = _split3(rec[:, 2:3])
    g2 = _split3(rec[:, 3:4])
    info = jnp.where(lane == 6.0, rec[:, 0:1], jnp.where(lane == 7.0, rec[:, 1:2], 0.0))
    for i in range(3):
        info = jnp.where(lane == float(i), g1[i].astype(F32), info)
        info = jnp.where(lane == float(3 + i), g2[i].astype(F32), info)
    src = jnp.concatenate([tok, info.astype(BF16)], axis=1)

    post = posm.T
    r = _iota_f32((SUBP, SUB), 0)
    onehot = jnp.where(r == post[0:1, :], 1.0, jnp.where(r == post[1:2, :], 1.0, 0.0)).astype(BF16)
    sorted_rows = jnp.dot(onehot, src, preferred_element_type=F32).astype(BF16)
    s_ref[...] = sorted_rows.reshape(CHUNKS_PER_SUB, ROW_ALIGN, ROW_W)


def _dispatch(rec_a, t_a, rec_b, t_b):
    nsub_a = rec_a.shape[0] // SUB
    nsub_b = rec_b.shape[0] // SUB
    nsub = nsub_a + nsub_b
    assert nsub_a % DISPATCH_SUBS == 0 and nsub_b % DISPATCH_SUBS == 0
    steps_a = nsub_a // DISPATCH_SUBS
    tm = DISPATCH_SUBS * SUB
    row = lambda i: (i, 0)
    row_a = lambda i: (jnp.minimum(i, steps_a - 1), 0)
    row_b = lambda i: (jnp.maximum(i - steps_a, 0), 0)
    return pl.pallas_call(
        functools.partial(_dispatch_body, nsub_a=nsub_a),
        grid=(nsub // DISPATCH_SUBS,),
        in_specs=[
            pl.BlockSpec((tm, LANES), row_a),
            pl.BlockSpec((tm, D_MODEL), row_a),
            pl.BlockSpec((tm, LANES), row_b),
            pl.BlockSpec((tm, D_MODEL), row_b),
        ],
        out_specs=[
            pl.BlockSpec((DISPATCH_SUBS * CHUNKS_PER_SUB, ROW_ALIGN, ROW_W), lambda i: (i, 0, 0)),
            pl.BlockSpec((tm, LANES), row),
            pl.BlockSpec((DISPATCH_SUBS, SUBLANES, LANES), lambda i: (i, 0, 0)),
        ],
        out_shape=[
            jax.ShapeDtypeStruct((nsub * CHUNKS_PER_SUB, ROW_ALIGN, ROW_W), BF16),
            jax.ShapeDtypeStruct((nsub * SUB, LANES), F32),
            jax.ShapeDtypeStruct((nsub, SUBLANES, LANES), F32),
        ],
        compiler_params=_cparams(("parallel",)),
        name="moe_dispatch",
    )(rec_a, t_a, rec_b, t_b)


def _experts_body(start_ref, cnt_ref, s_in, wg_ref, wu_ref, wd_ref, s_hbm,
                  xbuf, ybuf, wgu_ref, wdb_ref, gsem, ssem, list_ref, state_ref, *, nsub):
    del s_in
    e = pl.program_id(0)
    ne = pl.num_programs(0)
    par = e & 1

    list_max = list_ref.shape[0] // 2

    def gather_copy(src, i, slot):
        return pltpu.make_async_copy(s_hbm.at[src], xbuf.at[slot, i], gsem.at[slot])

    def scatter_copy(dst, i, slot):
        return pltpu.make_async_copy(ybuf.at[slot, i], s_hbm.at[dst], ssem.at[slot])

    def build_list(x, which):
        def per_sub(s, k):
            run = s * N_EXPERTS + x
            c = cnt_ref[run]
            base = s * CHUNKS_PER_SUB + start_ref[run]
            list_ref[k] = base
            list_ref[k + 1] = base + 1

            def per_chunk(i, carry):
                list_ref[k + i] = base + i
                return carry
            lax.fori_loop(2, c, per_chunk, 0)
            return k + c
        first = which * list_max
        state_ref[which] = lax.fori_loop(0, nsub, per_sub, first) - first

    def start_all(copy, which, first, n, slot, counter):
        def body(i, carry):
            copy(list_ref[which * list_max + first + i], i, slot).start()
            return carry
        lax.fori_loop(0, n, body, 0)
        state_ref[counter] = n

    def wait_all(copy, block_copy, slot, counter):
        n = state_ref[counter]

        @pl.when(n == CHUNKS_PER_BLK)
        def _():
            block_copy(slot).wait()

        @pl.when(n < CHUNKS_PER_BLK)
        def _():
            def body(i, carry):
                copy(0, 0, slot).wait()
                return carry
            lax.fori_loop(0, n, body, 0)
        state_ref[counter] = 0

    def gather_block(slot):
        return pltpu.make_async_copy(s_hbm.at[pl.ds(0, CHUNKS_PER_BLK)], xbuf.at[slot], gsem.at[slot])

    def scatter_block(slot):
        return pltpu.make_async_copy(ybuf.at[slot], s_hbm.at[pl.ds(0, CHUNKS_PER_BLK)], ssem.at[slot])

    def block_chunks(total, b):
        return jnp.minimum(total - b * CHUNKS_PER_BLK, CHUNKS_PER_BLK)

    @pl.when(e == 0)
    def _():
        for i in range(6):
            state_ref[i] = 0
        xbuf[...] = jnp.zeros_like(xbuf)
        build_list(0, 0)
        n0 = state_ref[0]

        @pl.when(n0 > 0)
        def _():
            start_all(gather_copy, 0, 0, block_chunks(n0, 0), 0, 2)

    @pl.when(e + 1 < ne)
    def _():
        build_list(e + 1, 1 - par)

    total = state_ref[par]
    nblk = (total + CHUNKS_PER_BLK - 1) // CHUNKS_PER_BLK
    wgu_ref[:, 0:EXPERT_FF] = wg_ref[0].astype(BF16)
    wgu_ref[:, EXPERT_FF:] = wu_ref[0].astype(BF16)
    wdb_ref[...] = wd_ref[0].astype(BF16)
    ef = e.astype(F32)

    def block(b, carry):
        slot = b & 1
        first = b * CHUNKS_PER_BLK

        @pl.when(b + 1 < nblk)
        def _():
            start_all(gather_copy, par, first + CHUNKS_PER_BLK, block_chunks(total, b + 1), 1 - slot, 3 - slot)

        wait_all(gather_copy, gather_block, slot, 2 + slot)
        wait_all(scatter_copy, scatter_block, slot, 4 + slot)
        x = xbuf[slot].reshape(MBLK, ROW_W)
        info = x[:, D_MODEL:].astype(F32)
        g_first = info[:, 0:1] + info[:, 1:2] + info[:, 2:3]
        g_second = info[:, 3:4] + info[:, 4:5] + info[:, 5:6]
        gate = jnp.where(info[:, 6:7] == ef, g_first, g_second)
        h = jnp.dot(x[:, :D_MODEL], wgu_ref[...], preferred_element_type=F32)
        h1 = h[:, :EXPERT_FF]
        hid = (h1 * jax.nn.sigmoid(h1)) * h[:, EXPERT_FF:] * gate
        y = jnp.dot(hid.astype(BF16), wdb_ref[...], preferred_element_type=F32).astype(BF16)
        ybuf[slot] = jnp.concatenate([y, x[:, D_MODEL:]], axis=1).reshape(CHUNKS_PER_BLK, ROW_ALIGN, ROW_W)
        start_all(scatter_copy, par, first, block_chunks(total, b), slot, 4 + slot)
        return carry

    lax.fori_loop(0, nblk, block, 0)

    @pl.when(e + 1 < ne)
    def _():
        n1 = state_ref[1 - par]

        @pl.when(n1 > 0)
        def _():
            start_all(gather_copy, 1 - par, 0, block_chunks(n1, 0), 0, 2)

    @pl.when(e == ne - 1)
    def _():
        wait_all(scatter_copy, scatter_block, 0, 4)
        wait_all(scatter_copy, scatter_block, 1, 5)


def _experts(start, cnt, staged, wg, wu, wd):
    nsub = staged.shape[0] // CHUNKS_PER_SUB
    list_max = nsub * SUB // ROW_ALIGN + nsub + LIST_SLACK
    wblk = lambda e, *_: (e, 0, 0)
    grid_spec = pltpu.PrefetchScalarGridSpec(
        num_scalar_prefetch=2,
        grid=(N_EXPERTS,),
        in_specs=[
            pl.BlockSpec(memory_space=pl.ANY),
            pl.BlockSpec((1, D_MODEL, EXPERT_FF), wblk),
            pl.BlockSpec((1, D_MODEL, EXPERT_FF), wblk),
            pl.BlockSpec((1, EXPERT_FF, D_MODEL), wblk),
        ],
        out_specs=pl.BlockSpec(memory_space=pl.ANY),
        scratch_shapes=[
            pltpu.VMEM((2, CHUNKS_PER_BLK, ROW_ALIGN, ROW_W), BF16),
            pltpu.VMEM((2, CHUNKS_PER_BLK, ROW_ALIGN, ROW_W), BF16),
            pltpu.VMEM((D_MODEL, 2 * EXPERT_FF), BF16),
            pltpu.VMEM((EXPERT_FF, D_MODEL), BF16),
            pltpu.SemaphoreType.DMA((2,)),
            pltpu.SemaphoreType.DMA((2,)),
            pltpu.SMEM((2 * list_max,), jnp.int32),
            pltpu.SMEM((6,), jnp.int32),
        ],
    )
    return pl.pallas_call(
        functools.partial(_experts_body, nsub=nsub),
        grid_spec=grid_spec,
        out_shape=jax.ShapeDtypeStruct(staged.shape, staged.dtype),
        input_output_aliases={2: 0},
        compiler_params=_cparams(("arbitrary",)),
        name="moe_experts",
    )(start, cnt, staged, wg, wu, wd)


def _combine_body(ys_ref, pos_ref, x1_ref, p_ref, wp_ref, gp_ref, wpg_ref, gf_ref, y_ref):
    r = _iota_f32((SUB, SUBP), 1)
    moe = []
    for s in range(x1_ref.shape[0] // SUB):
        p1 = pos_ref[s * SUB:(s + 1) * SUB, 0:1]
        p2 = pos_ref[s * SUB:(s + 1) * SUB, 1:2]
        onehot = jnp.where(r == p1, 1.0, jnp.where(r == p2, 1.0, 0.0)).astype(BF16)
        ys = ys_ref[s * CHUNKS_PER_SUB:(s + 1) * CHUNKS_PER_SUB].reshape(SUBP, D_MODEL)
        moe.append(jnp.dot(onehot, ys, preferred_element_type=F32))
    x2 = x1_ref[...] + jnp.concatenate(moe, axis=0)
    ple = _rms(jnp.dot(p_ref[...].astype(BF16), wp_ref[...], preferred_element_type=F32), gp_ref[...])
    gate = jax.nn.sigmoid(jnp.dot(x2.astype(BF16), wpg_ref[...], preferred_element_type=F32))
    y_ref[...] = _rms(x2 + ple * gate, gf_ref[...])


def _combine(ys, pos, x1, p, w_ple, gp, w_ple_gate, gf, sub_off):
    t = x1.shape[0]
    tm = COMBINE_SUBS * SUB
    blk_off = sub_off // COMBINE_SUBS
    assert sub_off % COMBINE_SUBS == 0 and t % tm == 0
    row = lambda i: (i, 0)
    const = lambda i: (0, 0)
    return pl.pallas_call(
        _combine_body,
        grid=(t // tm,),
        in_specs=[
            pl.BlockSpec((COMBINE_SUBS * CHUNKS_PER_SUB, ROW_ALIGN, D_MODEL), lambda i: (i + blk_off, 0, 0)),
            pl.BlockSpec((tm, LANES), lambda i: (i + blk_off, 0)),
            pl.BlockSpec((tm, D_MODEL), row),
            pl.BlockSpec((tm, PLE_DIM), row),
            pl.BlockSpec((PLE_DIM, D_MODEL), const),
            pl.BlockSpec((1, D_MODEL), const),
            pl.BlockSpec((D_MODEL, D_MODEL), const),
            pl.BlockSpec((1, D_MODEL), const),
        ],
        out_specs=pl.BlockSpec((tm, D_MODEL), row),
        out_shape=jax.ShapeDtypeStruct((t, D_MODEL), F32),
        compiler_params=_cparams(("parallel",)),
        name="moe_combine_ple",
    )(ys, pos, x1, p, w_ple, gp, w_ple_gate, gf)


def _rope_tables(pos):
    half = HEAD_DIM // 2
    inv = ROPE_BASE ** (-jnp.arange(half, dtype=F32) / half)
    ang = pos[:, None] * inv[None, :]
    cos = jnp.cos(ang)
    sin = jnp.sin(ang)
    return jnp.concatenate([cos, cos], axis=-1), jnp.concatenate([-sin, sin], axis=-1)


def _router_params(we, be, wg, bg):
    pad = LANES - N_EXPERTS - N_GROUPS
    w = jnp.pad(jnp.concatenate([we, wg], axis=1), ((0, 0), (0, pad)))
    b = jnp.pad(jnp.concatenate([be, bg]), (0, pad))[None, :]
    return jnp.concatenate(_split3(w)[:2], axis=1), b


def kernel(x_prompt, x_sample, p_prompt, p_sample, state_conv, state_ret, w_in, conv_w, conv_b, conv_ln_g, conv_ln_b, w_out, norm1_g, norm2_g, router_group_w, router_group_b, router_expert_w, router_expert_b, w_expert_gate, w_expert_up, w_expert_down, w_ple, ple_norm_g, w_ple_gate, final_norm_g):
    assert w_in.shape[0] == 1, "single-layer trunk"
    nb, seq, _ = x_prompt.shape
    ns, dseq, _ = x_sample.shape
    tm = 512

    w_in_b = w_in[0].astype(BF16)
    w_out_b = w_out[0].astype(BF16)
    w_ple_b = w_ple[0].astype(BF16)
    w_pg_b = w_ple_gate[0].astype(BF16)
    g1 = norm1_g[0][None, :]
    g2 = norm2_g[0][None, :]
    gp = ple_norm_g[0][None, :]
    gf = final_norm_g[None, :]
    cb = conv_b[0][None, :]
    lng = conv_ln_g[0][None, :]
    lnb = conv_ln_b[0][None, :]
    wr2, br = _router_params(router_expert_w[0], router_expert_b[0], router_group_w[0], router_group_b[0])

    cos_p, sin_p = _rope_tables(jnp.arange(seq, dtype=F32) + jnp.float32(0))
    pos_s = jnp.tile(jnp.arange(dseq, dtype=F32) + jnp.float32(PAST_LEN), tm // dseq)
    cos_s, sin_s = _rope_tables(pos_s)

    xp = x_prompt.reshape(nb * seq, D_MODEL)
    x1_p, t_p, rec_p, conv_p, ret_p = _mix(xp, g1, w_in_b, cos_p, sin_p, conv_w[0], cb, lng, lnb, w_out_b, g2,
                                           wr2, br, nb, seq, MIX_TILE)

    xs = x_sample.reshape(ns * dseq, D_MODEL)
    u, q, k, v, gs = _inproj(xs, g1, w_in_b, cos_s, sin_s, tm, 1, F32)
    c, conv_s = _conv_sample(u.reshape(ns, dseq, CONV_CH), jnp.transpose(state_conv[0], (1, 0, 2)), conv_w[0], cb,
                             lng, lnb, 16)
    conv_s = jnp.transpose(conv_s, (1, 0, 2))
    o, ret_s = _ret_sample(q, k, v, gs, state_ret[0], dseq, 8)
    x1_s, t_s, rec_s = _outproj(c.reshape(ns * dseq, CONV_CH), o, xs, w_out_b, g2, wr2, br, tm)

    staged, pos, meta = _dispatch(rec_p, t_p, rec_s, t_s)
    start = meta[:, 0, :N_EXPERTS].astype(jnp.int32).reshape(-1)
    cnt = meta[:, 1, :N_EXPERTS].astype(jnp.int32).reshape(-1)
    ys = _experts(start, cnt, staged, w_expert_gate[0], w_expert_up[0], w_expert_down[0])

    y_p = _combine(ys, pos, x1_p, p_prompt[0].reshape(nb * seq, PLE_DIM), w_ple_b, gp, w_pg_b, gf, 0)
    y_s = _combine(ys, pos, x1_s, p_sample[0].reshape(ns * dseq, PLE_DIM), w_ple_b, gp, w_pg_b, gf,
                   nb * seq // SUB)

    return (y_p.reshape(nb, seq, D_MODEL), y_s.reshape(ns, dseq, D_MODEL),
            conv_p[None], ret_p[None], conv_s[None], ret_s[None])
```

```python
import functools

import jax
import jax.numpy as jnp
from jax import lax
from jax.experimental import pallas as pl
from jax.experimental.pallas import tpu as pltpu

F32 = jnp.float32
BF16 = jnp.bfloat16

D_MODEL = 1024
PLE_DIM = 256
CONV_CH = 512
CONV_K = 31
RET_WIDTH = 512
RET_HEADS = 4
HEAD_DIM = 128
CHUNK = 128
ROPE_BASE = 10000.0
N_GROUPS = 4
EXPERTS_PER_GROUP = 8
N_EXPERTS = 32
EXPERT_FF = 256
IN_COLS = 3072
EPS = 1e-6
PAST_LEN = 16384

LANES = 128
SUBLANES = 8
HALO = 32
HALO_OFF = HALO - (CONV_K - 1)
VMEM_LIMIT = 48 * 1024 * 1024
MIX_VMEM_LIMIT = 56 * 1024 * 1024
MIX_TILE = 512
SAMPLE_TILE = 1024
CONV_SAMPLE_SEQS = 32
RET_SAMPLE_SEQS = 16

SUB = 256
ROW_ALIGN = 16
PBLK = 256
SUBP = -(-(2 * SUB + N_EXPERTS * (ROW_ALIGN - 1)) // PBLK) * PBLK
CHUNKS_PER_SUB = SUBP // ROW_ALIGN
ROW_W = D_MODEL + LANES
MBLK = 512
CHUNKS_PER_BLK = MBLK // ROW_ALIGN
LIST_SLACK = 2
COMBINE_SUBS = 4
DISPATCH_SUBS = 4


def _cparams(sem):
    return pltpu.CompilerParams(dimension_semantics=sem, vmem_limit_bytes=VMEM_LIMIT)


def _rms(x, g):
    return x * lax.rsqrt(jnp.mean(x * x, axis=-1, keepdims=True) + EPS) * g


def _inproj_body(x_ref, g1_ref, w_ref, cos_ref, sin_ref, u_ref, q_ref, k_ref, v_ref, gs_ref):
    h = _rms(x_ref[...], g1_ref[...]).astype(BF16)
    z = jnp.dot(h, w_ref[...], preferred_element_type=F32)
    a = z[:, :CONV_CH]
    b = z[:, CONV_CH:2 * CONV_CH]
    u_ref[...] = a * jax.nn.sigmoid(b)
    cos = cos_ref[...]
    sin = sin_ref[...]
    q0 = 2 * CONV_CH
    k0 = q0 + RET_WIDTH
    for hh in range(RET_HEADS):
        sl = slice(hh * HEAD_DIM, (hh + 1) * HEAD_DIM)
        qh = z[:, q0 + hh * HEAD_DIM:q0 + (hh + 1) * HEAD_DIM]
        kh = z[:, k0 + hh * HEAD_DIM:k0 + (hh + 1) * HEAD_DIM]
        q_ref[:, sl] = (qh * cos + pltpu.roll(qh, HEAD_DIM // 2, 1) * sin).astype(q_ref.dtype)
        kr = (kh * cos + pltpu.roll(kh, HEAD_DIM // 2, 1) * sin) * (HEAD_DIM ** -0.5)
        k_ref[:, sl] = kr.astype(k_ref.dtype)
    v_ref[...] = z[:, k0 + RET_WIDTH:k0 + 2 * RET_WIDTH].astype(v_ref.dtype)
    g = z[:, k0 + 2 * RET_WIDTH:]
    gs_ref[...] = g * jax.nn.sigmoid(g)


def _inproj(x, g1, w_in, cos, sin, tm, table_blocks, qkv_dtype):
    t = x.shape[0]
    row = lambda i: (i, 0)
    const = lambda i: (0, 0)
    tab = (lambda i: (i % table_blocks, 0)) if table_blocks > 1 else const
    return pl.pallas_call(
        _inproj_body,
        grid=(t // tm,),
        in_specs=[
            pl.BlockSpec((tm, D_MODEL), row),
            pl.BlockSpec((1, D_MODEL), const),
            pl.BlockSpec((D_MODEL, IN_COLS), const),
            pl.BlockSpec((tm, HEAD_DIM), tab),
            pl.BlockSpec((tm, HEAD_DIM), tab),
        ],
        out_specs=[pl.BlockSpec((tm, CONV_CH), row)] + [pl.BlockSpec((tm, RET_WIDTH), row)] * 4,
        out_shape=[
            jax.ShapeDtypeStruct((t, CONV_CH), F32),
            jax.ShapeDtypeStruct((t, RET_WIDTH), qkv_dtype),
            jax.ShapeDtypeStruct((t, RET_WIDTH), qkv_dtype),
            jax.ShapeDtypeStruct((t, RET_WIDTH), qkv_dtype),
            jax.ShapeDtypeStruct((t, RET_WIDTH), F32),
        ],
        compiler_params=_cparams(("parallel",)),
        name="inproj",
    )(x, g1, w_in, cos, sin)


def _ln_silu(acc, g, b):
    mu = jnp.mean(acc, axis=-1, keepdims=True)
    d = acc - mu
    var = jnp.mean(d * d, axis=-1, keepdims=True)
    y = d * lax.rsqrt(var + EPS) * g + b
    return y * jax.nn.sigmoid(y)


def _dwconv(load, w_ref, rows):
    acc = None
    for b in range(SUBLANES):
        part = None
        for a in range((CONV_K + HALO_OFF) // SUBLANES + 1):
            k = SUBLANES * a + b - HALO_OFF
            if 0 <= k < CONV_K:
                term = load(SUBLANES * a, rows + SUBLANES) * w_ref[k:k + 1, :]
                part = term if part is None else part + term
        if part is not None:
            shifted = part[b:b + rows]
            acc = shifted if acc is None else acc + shifted
    return acc


def _conv_sample_body(u_ref, st_ref, w_ref, cb_ref, lg_ref, lb_ref, c_ref, nst_ref, ext_ref):
    l = u_ref.shape[1]
    hist = CONV_K - 1
    ext_ref[0:hist] = st_ref[...]
    ext_ref[hist:] = jnp.transpose(u_ref[...], (1, 0, 2))
    acc = ext_ref[0:l] * w_ref[0:1, :] + cb_ref[...]
    for k in range(1, CONV_K):
        acc = acc + ext_ref[k:k + l] * w_ref[k:k + 1, :]
    y = _ln_silu(acc, lg_ref[...], lb_ref[...])
    c_ref[...] = jnp.transpose(y, (1, 0, 2)).astype(c_ref.dtype)
    nst_ref[...] = ext_ref[l:]


def _conv_sample(u, state_t, conv_w, conv_b, ln_g, ln_b, nb):
    n, l, _ = u.shape
    hist = CONV_K - 1
    const = lambda b: (0, 0)
    tok = lambda b: (b, 0, 0)
    tmaj = lambda b: (0, b, 0)
    return pl.pallas_call(
        _conv_sample_body,
        grid=(n // nb,),
        in_specs=[
            pl.BlockSpec((nb, l, CONV_CH), tok),
            pl.BlockSpec((hist, nb, CONV_CH), tmaj),
            pl.BlockSpec((CONV_K, CONV_CH), const),
            pl.BlockSpec((1, CONV_CH), const),
            pl.BlockSpec((1, CONV_CH), const),
            pl.BlockSpec((1, CONV_CH), const),
        ],
        out_specs=[
            pl.BlockSpec((nb, l, CONV_CH), tok),
            pl.BlockSpec((hist, nb, CONV_CH), tmaj),
        ],
        out_shape=[
            jax.ShapeDtypeStruct((n, l, CONV_CH), BF16),
            jax.ShapeDtypeStruct((hist, n, CONV_CH), F32),
        ],
        scratch_shapes=[pltpu.VMEM((hist + l, nb, CONV_CH), F32)],
        compiler_params=_cparams(("parallel",)),
        name="conv_sample",
    )(u, state_t, conv_w, conv_b, ln_g, ln_b)


def _decay_tables(c):
    lg = jnp.log(1.0 - 2.0 ** (-5.0 - jnp.arange(RET_HEADS, dtype=F32)))
    idx = jnp.arange(c, dtype=F32)
    rel = idx[:, None] - idx[None, :]
    dmat = jnp.where(rel[None] >= 0, jnp.exp(jnp.maximum(rel, 0.0)[None] * lg[:, None, None]), 0.0)
    xi = jnp.exp((idx + 1.0)[None, :] * lg[:, None])
    zeta = jnp.exp((c - 1.0 - idx)[None, :] * lg[:, None])
    gc = jnp.exp(c * lg)
    xi_b = jnp.broadcast_to(xi[:, :, None], (RET_HEADS, c, HEAD_DIM))
    zeta_b = jnp.broadcast_to(zeta[:, :, None], (RET_HEADS, c, HEAD_DIM))
    gc_b = jnp.broadcast_to(gc[:, None, None], (RET_HEADS, 1, HEAD_DIM))
    return dmat, xi_b, zeta_b, gc_b


def _group_norm(o):
    mu = jnp.mean(o, axis=-1, keepdims=True)
    d = o - mu
    var = jnp.mean(d * d, axis=-1, keepdims=True)
    return d * lax.rsqrt(var + EPS)


def _ret_chunk(qh, kh, vh, r, dmat, xi, zeta, gc):
    qb = qh.astype(BF16)
    kb = kh.astype(BF16)
    vb = vh.astype(BF16)
    s = lax.dot_general(qb, kb, (((1,), (1,)), ((), ())), preferred_element_type=F32) * dmat
    o = jnp.dot(s.astype(BF16), vb, preferred_element_type=F32)
    o = o + jnp.dot(qb, r.astype(BF16), preferred_element_type=F32) * xi
    kz = (kh.astype(F32) * zeta).astype(BF16)
    r_new = r * gc + lax.dot_general(kz, vb, (((0,), (0,)), ((), ())), preferred_element_type=F32)
    return o, r_new


def _ret_sample_body(q_ref, k_ref, v_ref, gs_ref, st_ref, d_ref, xi_ref, zeta_ref, gc_ref, o_ref, nst_ref):
    nb = st_ref.shape[0]
    l = q_ref.shape[0] // nb
    for b in range(nb):
        rows = slice(b * l, (b + 1) * l)
        for hh in range(RET_HEADS):
            sl = slice(hh * HEAD_DIM, (hh + 1) * HEAD_DIM)
            o, r = _ret_chunk(q_ref[rows, sl], k_ref[rows, sl], v_ref[rows, sl], st_ref[b, hh],
                              d_ref[hh], xi_ref[hh], zeta_ref[hh], gc_ref[hh])
            o_ref[rows, sl] = (gs_ref[rows, sl] * _group_norm(o)).astype(o_ref.dtype)
            nst_ref[b, hh] = r


def _ret_sample(q, k, v, gs, state, l, nb):
    n = state.shape[0]
    dmat, xi, zeta, gc = _decay_tables(l)
    row = lambda b: (b, 0)
    c3 = lambda b: (0, 0, 0)
    blk4 = lambda b: (b, 0, 0, 0)
    return pl.pallas_call(
        _ret_sample_body,
        grid=(n // nb,),
        in_specs=[pl.BlockSpec((nb * l, RET_WIDTH), row)] * 4 + [
            pl.BlockSpec((nb, RET_HEADS, HEAD_DIM, HEAD_DIM), blk4),
            pl.BlockSpec((RET_HEADS, l, l), c3),
            pl.BlockSpec((RET_HEADS, l, HEAD_DIM), c3),
            pl.BlockSpec((RET_HEADS, l, HEAD_DIM), c3),
            pl.BlockSpec((RET_HEADS, 1, HEAD_DIM), c3),
        ],
        out_specs=[
            pl.BlockSpec((nb * l, RET_WIDTH), row),
            pl.BlockSpec((nb, RET_HEADS, HEAD_DIM, HEAD_DIM), blk4),
        ],
        out_shape=[
            jax.ShapeDtypeStruct((n * l, RET_WIDTH), BF16),
            jax.ShapeDtypeStruct((n, RET_HEADS, HEAD_DIM, HEAD_DIM), F32),
        ],
        compiler_params=_cparams(("parallel",)),
        name="ret_sample",
    )(q, k, v, gs, state, dmat, xi, zeta, gc)


def _split3(x):
    hi = x.astype(BF16)
    r1 = x - hi.astype(F32)
    mid = r1.astype(BF16)
    lo = (r1 - mid.astype(F32)).astype(BF16)
    return hi, mid, lo


def _dot_hp(t, w2_ref):
    t_hi, t_mid, _ = _split3(t)
    d = functools.partial(jnp.dot, preferred_element_type=F32)
    both = d(t_hi, w2_ref[...])
    return both[:, :LANES] + (d(t_mid, w2_ref[:, 0:LANES]) + both[:, LANES:])


def _iota_f32(shape, dim):
    return lax.broadcasted_iota(jnp.int32, shape, dim).astype(F32)


def _route(logits):
    lt = logits.T
    tm = lt.shape[1]
    row = _iota_f32((SUBLANES, tm), 0)
    big = float(SUBLANES)

    def rmax(x):
        return jnp.max(x, axis=0, keepdims=True)

    def first_row(mask):
        return jnp.min(jnp.where(mask, row, big), axis=0, keepdims=True)

    lg = jnp.where(row < float(N_GROUPS), lt[N_EXPERTS:N_EXPERTS + SUBLANES, :], -1e30)
    m = rmax(lg)
    g_top = 1.0 / jnp.sum(jnp.exp(lg - m), axis=0, keepdims=True)
    g_idx = first_row(lg == m)
    lem = lt[0:EXPERTS_PER_GROUP, :]
    for g in range(1, N_GROUPS):
        lem = jnp.where(g_idx == float(g), lt[g * EXPERTS_PER_GROUP:(g + 1) * EXPERTS_PER_GROUP, :], lem)
    pe = jnp.exp(lem - rmax(lem))
    p1 = rmax(pe)
    e1 = first_row(pe == p1)
    rest = row != e1
    pe2 = jnp.where(rest, pe, -1.0)
    p2 = rmax(pe2)
    e2 = first_row(rest & (pe2 == p2))
    scale = g_top / (p1 + p2)
    base = g_idx * float(EXPERTS_PER_GROUP)
    rec_t = jnp.where(row == 0.0, base + e1, jnp.where(row == 1.0, base + e2, 0.0))
    rec_t = rec_t + jnp.where(row == 2.0, p1 * scale, jnp.where(row == 3.0, p2 * scale, 0.0))
    return jnp.concatenate([rec_t, jnp.zeros((LANES - SUBLANES, tm), F32)], axis=0).T


def _outproj_body(c_ref, o_ref, x_ref, wo_ref, g2_ref, wr_ref, br_ref, x1_ref, t_ref, rec_ref):
    x1 = x_ref[...] + jnp.dot(c_ref[...], wo_ref[0:CONV_CH, :], preferred_element_type=F32)
    x1 = x1 + jnp.dot(o_ref[...], wo_ref[CONV_CH:, :], preferred_element_type=F32)
    x1_ref[...] = x1
    t = _rms(x1, g2_ref[...])
    t_ref[...] = t.astype(t_ref.dtype)
    rec_ref[...] = _route(_dot_hp(t, wr_ref) + br_ref[...])


def _outproj(c, o, x, w_out, g2, wr2, br, tm):
    t = x.shape[0]
    row = lambda i: (i, 0)
    const = lambda i: (0, 0)
    return pl.pallas_call(
        _outproj_body,
        grid=(t // tm,),
        in_specs=[
            pl.BlockSpec((tm, CONV_CH), row),
            pl.BlockSpec((tm, RET_WIDTH), row),
            pl.BlockSpec((tm, D_MODEL), row),
            pl.BlockSpec((D_MODEL, D_MODEL), const),
            pl.BlockSpec((1, D_MODEL), const),
            pl.BlockSpec((D_MODEL, 2 * LANES), const),
            pl.BlockSpec((1, LANES), const),
        ],
        out_specs=[
            pl.BlockSpec((tm, D_MODEL), row),
            pl.BlockSpec((tm, D_MODEL), row),
            pl.BlockSpec((tm, LANES), row),
        ],
        out_shape=[
            jax.ShapeDtypeStruct((t, D_MODEL), F32),
            jax.ShapeDtypeStruct((t, D_MODEL), BF16),
            jax.ShapeDtypeStruct((t, LANES), F32),
        ],
        compiler_params=_cparams(("parallel",)),
        name="outproj_router",
    )(c, o, x, w_out, g2, wr2, br)


def _mix_body(x_ref, g1_ref, w_ref, cos_ref, sin_ref, cw_ref, cb_ref, lg_ref, lb_ref,
              d_ref, xi_ref, zeta_ref, gc_ref, wo_ref, g2_ref, wr_ref, br_ref,
              x1_ref, t_ref, rec_ref, cst_ref, rst_ref, ext_ref, r_ref, o_ref):
    j = pl.program_id(1)
    tl = x_ref.shape[0]
    x = x_ref[...]
    z = jnp.dot(_rms(x, g1_ref[...]).astype(BF16), w_ref[...], preferred_element_type=F32)

    @pl.when(j == 0)
    def _():
        ext_ref[0:HALO, :] = jnp.zeros((HALO, CONV_CH), F32)
        ext_ref[tl + HALO:, :] = jnp.zeros((SUBLANES, CONV_CH), F32)
        r_ref[...] = jnp.zeros_like(r_ref)

    @pl.when(j > 0)
    def _():
        ext_ref[0:HALO, :] = ext_ref[tl:tl + HALO, :]

    ext_ref[HALO:tl + HALO, :] = z[:, :CONV_CH] * jax.nn.sigmoid(z[:, CONV_CH:2 * CONV_CH])
    acc = _dwconv(lambda s, n: ext_ref[s:s + n, :], cw_ref, tl) + cb_ref[...]
    c = _ln_silu(acc, lg_ref[...], lb_ref[...]).astype(BF16)
    cst_ref[0] = ext_ref[tl + HALO_OFF:tl + HALO, :]

    cos = cos_ref[...]
    sin = sin_ref[...]
    q0 = 2 * CONV_CH
    k0 = q0 + RET_WIDTH
    v0 = k0 + RET_WIDTH
    g0 = v0 + RET_WIDTH
    for hh in range(RET_HEADS):
        lo = hh * HEAD_DIM
        qh = z[:, q0 + lo:q0 + lo + HEAD_DIM]
        kh = z[:, k0 + lo:k0 + lo + HEAD_DIM]
        qr = (qh * cos + pltpu.roll(qh, HEAD_DIM // 2, 1) * sin).astype(BF16)
        kr = ((kh * cos + pltpu.roll(kh, HEAD_DIM // 2, 1) * sin) * (HEAD_DIM ** -0.5)).astype(BF16)
        vh = z[:, v0 + lo:v0 + lo + HEAD_DIM].astype(BF16)
        g = z[:, g0 + lo:g0 + lo + HEAD_DIM]
        gs = g * jax.nn.sigmoid(g)
        r = r_ref[hh]
        for ci in range(tl // CHUNK):
            rows = slice(ci * CHUNK, (ci + 1) * CHUNK)
            o, r = _ret_chunk(qr[rows], kr[rows], vh[rows], r, d_ref[hh], xi_ref[hh], zeta_ref[hh], gc_ref[hh])
            o_ref[rows, lo:lo + HEAD_DIM] = (gs[rows] * _group_norm(o)).astype(BF16)
        r_ref[hh] = r
    rst_ref[0] = r_ref[...]

    x1 = x + jnp.dot(c, wo_ref[0:CONV_CH, :], preferred_element_type=F32)
    x1 = x1 + jnp.dot(o_ref[...], wo_ref[CONV_CH:, :], preferred_element_type=F32)
    x1_ref[...] = x1
    t = _rms(x1, g2_ref[...])
    t_ref[...] = t.astype(t_ref.dtype)
    rec_ref[...] = _route(_dot_hp(t, wr_ref) + br_ref[...])


def _mix(x, g1, w_in, cos, sin, conv_w, conv_b, ln_g, ln_b, w_out, g2, wr2, br, n, l, tl):
    dmat, xi, zeta, gc = _decay_tables(CHUNK)
    per = l // tl
    row = lambda b, j: (b * per + j, 0)
    tab = lambda b, j: (j, 0)
    const = lambda b, j: (0, 0)
    c3 = lambda b, j: (0, 0, 0)
    once = dict(pipeline_mode=pl.Buffered(1))
    return pl.pallas_call(
        _mix_body,
        grid=(n, per),
        in_specs=[
            pl.BlockSpec((tl, D_MODEL), row),
            pl.BlockSpec((1, D_MODEL), const),
            pl.BlockSpec((D_MODEL, IN_COLS), const, **once),
            pl.BlockSpec((tl, HEAD_DIM), tab),
            pl.BlockSpec((tl, HEAD_DIM), tab),
            pl.BlockSpec((CONV_K, CONV_CH), const),
            pl.BlockSpec((1, CONV_CH), const),
            pl.BlockSpec((1, CONV_CH), const),
            pl.BlockSpec((1, CONV_CH), const),
            pl.BlockSpec((RET_HEADS, CHUNK, CHUNK), c3),
            pl.BlockSpec((RET_HEADS, CHUNK, HEAD_DIM), c3),
            pl.BlockSpec((RET_HEADS, CHUNK, HEAD_DIM), c3),
            pl.BlockSpec((RET_HEADS, 1, HEAD_DIM), c3),
            pl.BlockSpec((D_MODEL, D_MODEL), const, **once),
            pl.BlockSpec((1, D_MODEL), const),
            pl.BlockSpec((D_MODEL, 2 * LANES), const, **once),
            pl.BlockSpec((1, LANES), const),
        ],
        out_specs=[
            pl.BlockSpec((tl, D_MODEL), row),
            pl.BlockSpec((tl, D_MODEL), row),
            pl.BlockSpec((tl, LANES), row),
            pl.BlockSpec((1, CONV_K - 1, CONV_CH), lambda b, j: (b, 0, 0)),
            pl.BlockSpec((1, RET_HEADS, HEAD_DIM, HEAD_DIM), lambda b, j: (b, 0, 0, 0)),
        ],
        out_shape=[
            jax.ShapeDtypeStruct((n * l, D_MODEL), F32),
            jax.ShapeDtypeStruct((n * l, D_MODEL), BF16),
            jax.ShapeDtypeStruct((n * l, LANES), F32),
            jax.ShapeDtypeStruct((n, CONV_K - 1, CONV_CH), F32),
            jax.ShapeDtypeStruct((n, RET_HEADS, HEAD_DIM, HEAD_DIM), F32),
        ],
        scratch_shapes=[
            pltpu.VMEM((tl + HALO + SUBLANES, CONV_CH), F32),
            pltpu.VMEM((RET_HEADS, HEAD_DIM, HEAD_DIM), F32),
            pltpu.VMEM((tl, RET_WIDTH), BF16),
        ],
        compiler_params=pltpu.CompilerParams(dimension_semantics=("arbitrary", "arbitrary"),
                                             vmem_limit_bytes=MIX_VMEM_LIMIT),
        name="token_mix",
    )(x, g1, w_in, cos, sin, conv_w, conv_b, ln_g, ln_b, dmat, xi, zeta, gc, w_out, g2, wr2, br)


def _dispatch_body(rec_a_ref, t_a_ref, rec_b_ref, t_b_ref, s_ref, pos_ref, meta_ref, *, nsub_a):
    from_a = pl.program_id(0) * DISPATCH_SUBS < nsub_a
    for s in range(DISPATCH_SUBS):
        rows = slice(s * SUB, (s + 1) * SUB)
        rec = jnp.where(from_a, rec_a_ref[rows, :], rec_b_ref[rows, :])
        tok = jnp.where(from_a, t_a_ref[rows, :], t_b_ref[rows, :])
        chunks = slice(s * CHUNKS_PER_SUB, (s + 1) * CHUNKS_PER_SUB)
        _dispatch_sub_tile(rec, tok, s_ref.at[chunks], pos_ref.at[rows], meta_ref.at[s])


def _dispatch_sub_tile(rec, tok, s_ref, pos_ref, meta_ref):
    lane = _iota_f32(rec.shape, 1)
    a1 = lane == rec[:, 0:1]
    a2 = lane == rec[:, 1:2]
    a1f = jnp.where(a1, 1.0, 0.0)
    a2f = jnp.where(a2, 1.0, 0.0)
    ltri = jnp.where(_iota_f32((SUB, SUB), 1) < _iota_f32((SUB, SUB), 0), 1.0, 0.0).astype(BF16)
    c1 = jnp.dot(ltri, a1f.astype(BF16), preferred_element_type=F32)
    c2 = jnp.dot(ltri, a2f.astype(BF16), preferred_element_type=F32)
    n1 = jnp.sum(a1f, axis=0, keepdims=True)
    n2 = jnp.sum(a2f, axis=0, keepdims=True)
    cnt = jnp.floor((n1 + n2 + (ROW_ALIGN - 1.0)) * (1.0 / ROW_ALIGN))
    utri = jnp.where(_iota_f32((LANES, LANES), 0) < _iota_f32((LANES, LANES), 1), 1.0, 0.0).astype(BF16)
    start = jnp.dot(jnp.broadcast_to(cnt, (SUBLANES, LANES)).astype(BF16), utri,
                    preferred_element_type=F32)[0:1]
    base1 = start * ROW_ALIGN
    base2 = base1 + n1
    pos1 = jnp.sum(jnp.where(a1, c1 + base1, 0.0), axis=1, keepdims=True)
    pos2 = jnp.sum(jnp.where(a2, c2 + base2, 0.0), axis=1, keepdims=True)
    posm = jnp.where(lane == 0.0, pos1, jnp.where(lane == 1.0, pos2, 0.0))
    pos_ref[...] = posm
    row = _iota_f32((SUBLANES, LANES), 0)
    meta_ref[...] = jnp.where(row == 0.0, start, jnp.where(row == 1.0, cnt, 0.0))

    g1 = _split3(rec[:, 2:3])
    g2 = _split3(rec[:, 3:4])
    info = jnp.where(lane == 6.0, rec[:, 0:1], jnp.where(lane == 7.0, rec[:, 1:2], 0.0))
    for i in range(3):
        info = jnp.where(lane == float(i), g1[i].astype(F32), info)
        info = jnp.where(lane == float(3 + i), g2[i].astype(F32), info)
    src = jnp.concatenate([tok, info.astype(BF16)], axis=1)

    post = posm.T
    r = _iota_f32((SUBP, SUB), 0)
    onehot = jnp.where(r == post[0:1, :], 1.0, jnp.where(r == post[1:2, :], 1.0, 0.0)).astype(BF16)
    sorted_rows = jnp.dot(onehot, src, preferred_element_type=F32).astype(BF16)
    s_ref[...] = sorted_rows.reshape(CHUNKS_PER_SUB, ROW_ALIGN, ROW_W)


def _dispatch(rec_a, t_a, rec_b, t_b):
    nsub_a = rec_a.shape[0] // SUB
    nsub_b = rec_b.shape[0] // SUB
    nsub = nsub_a + nsub_b
    assert nsub_a % DISPATCH_SUBS == 0 and nsub_b % DISPATCH_SUBS == 0
    steps_a = nsub_a // DISPATCH_SUBS
    tm = DISPATCH_SUBS * SUB
    row = lambda i: (i, 0)
    row_a = lambda i: (jnp.minimum(i, steps_a - 1), 0)
    row_b = lambda i: (jnp.maximum(i - steps_a, 0), 0)
    return pl.pallas_call(
        functools.partial(_dispatch_body, nsub_a=nsub_a),
        grid=(nsub // DISPATCH_SUBS,),
        in_specs=[
            pl.BlockSpec((tm, LANES), row_a),
            pl.BlockSpec((tm, D_MODEL), row_a),
            pl.BlockSpec((tm, LANES), row_b),
            pl.BlockSpec((tm, D_MODEL), row_b),
        ],
        out_specs=[
            pl.BlockSpec((DISPATCH_SUBS * CHUNKS_PER_SUB, ROW_ALIGN, ROW_W), lambda i: (i, 0, 0)),
            pl.BlockSpec((tm, LANES), row),
            pl.BlockSpec((DISPATCH_SUBS, SUBLANES, LANES), lambda i: (i, 0, 0)),
        ],
        out_shape=[
            jax.ShapeDtypeStruct((nsub * CHUNKS_PER_SUB, ROW_ALIGN, ROW_W), BF16),
            jax.ShapeDtypeStruct((nsub * SUB, LANES), F32),
            jax.ShapeDtypeStruct((nsub, SUBLANES, LANES), F32),
        ],
        compiler_params=_cparams(("parallel",)),
        name="moe_dispatch",
    )(rec_a, t_a, rec_b, t_b)


def _experts_body(start_ref, cnt_ref, s_in, wg_ref, wu_ref, wd_ref, s_hbm,
                  xbuf, ybuf, wgu_ref, wdb_ref, gsem, ssem, list_ref, state_ref, *, nsub):
    del s_in
    e = pl.program_id(0)
    ne = pl.num_programs(0)
    par = e & 1

    list_max = list_ref.shape[0] // 2

    def gather_copy(src, i, slot):
        return pltpu.make_async_copy(s_hbm.at[src], xbuf.at[slot, i], gsem.at[slot])

    def scatter_copy(dst, i, slot):
        return pltpu.make_async_copy(ybuf.at[slot, i], s_hbm.at[dst], ssem.at[slot])

    def build_list(x, which):
        def per_sub(s, k):
            run = s * N_EXPERTS + x
            c = cnt_ref[run]
            base = s * CHUNKS_PER_SUB + start_ref[run]
            list_ref[k] = base
            list_ref[k + 1] = base + 1

            def per_chunk(i, carry):
                list_ref[k + i] = base + i
                return carry
            lax.fori_loop(2, c, per_chunk, 0)
            return k + c
        first = which * list_max
        state_ref[which] = lax.fori_loop(0, nsub, per_sub, first) - first

    def start_all(copy, which, first, n, slot, counter):
        def body(i, carry):
            copy(list_ref[which * list_max + first + i], i, slot).start()
            return carry
        lax.fori_loop(0, n, body, 0)
        state_ref[counter] = n

    def wait_all(copy, block_copy, slot, counter):
        n = state_ref[counter]

        @pl.when(n == CHUNKS_PER_BLK)
        def _():
            block_copy(slot).wait()

        @pl.when(n < CHUNKS_PER_BLK)
        def _():
            def body(i, carry):
                copy(0, 0, slot).wait()
                return carry
            lax.fori_loop(0, n, body, 0)
        state_ref[counter] = 0

    def gather_block(slot):
        return pltpu.make_async_copy(s_hbm.at[pl.ds(0, CHUNKS_PER_BLK)], xbuf.at[slot], gsem.at[slot])

    def scatter_block(slot):
        return pltpu.make_async_copy(ybuf.at[slot], s_hbm.at[pl.ds(0, CHUNKS_PER_BLK)], ssem.at[slot])

    def block_chunks(total, b):
        return jnp.minimum(total - b * CHUNKS_PER_BLK, CHUNKS_PER_BLK)

    @pl.when(e == 0)
    def _():
        for i in range(6):
            state_ref[i] = 0
        xbuf[...] = jnp.zeros_like(xbuf)
        build_list(0, 0)
        n0 = state_ref[0]

        @pl.when(n0 > 0)
        def _():
            start_all(gather_copy, 0, 0, block_chunks(n0, 0), 0, 2)

    @pl.when(e + 1 < ne)
    def _():
        build_list(e + 1, 1 - par)

    total = state_ref[par]
    nblk = (total + CHUNKS_PER_BLK - 1) // CHUNKS_PER_BLK
    wgu_ref[:, 0:EXPERT_FF] = wg_ref[0].astype(BF16)
    wgu_ref[:, EXPERT_FF:] = wu_ref[0].astype(BF16)
    wdb_ref[...] = wd_ref[0].astype(BF16)
    ef = e.astype(F32)

    def block(b, carry):
        slot = b & 1
        first = b * CHUNKS_PER_BLK

        @pl.when(b + 1 < nblk)
        def _():
            start_all(gather_copy, par, first + CHUNKS_PER_BLK, block_chunks(total, b + 1), 1 - slot, 3 - slot)

        wait_all(gather_copy, gather_block, slot, 2 + slot)
        wait_all(scatter_copy, scatter_block, slot, 4 + slot)
        x = xbuf[slot].reshape(MBLK, ROW_W)
        info = x[:, D_MODEL:].astype(F32)
        g_first = info[:, 0:1] + info[:, 1:2] + info[:, 2:3]
        g_second = info[:, 3:4] + info[:, 4:5] + info[:, 5:6]
        gate = jnp.where(info[:, 6:7] == ef, g_first, g_second)
        h = jnp.dot(x[:, :D_MODEL], wgu_ref[...], preferred_element_type=F32)
        h1 = h[:, :EXPERT_FF]
        hid = (h1 * jax.nn.sigmoid(h1)) * h[:, EXPERT_FF:] * gate
        y = jnp.dot(hid.astype(BF16), wdb_ref[...], preferred_element_type=F32).astype(BF16)
        ybuf[slot] = jnp.concatenate([y, x[:, D_MODEL:]], axis=1).reshape(CHUNKS_PER_BLK, ROW_ALIGN, ROW_W)
        start_all(scatter_copy, par, first, block_chunks(total, b), slot, 4 + slot)
        return carry

    lax.fori_loop(0, nblk, block, 0)

    @pl.when(e + 1 < ne)
    def _():
        n1 = state_ref[1 - par]

        @pl.when(n1 > 0)
        def _():
            start_all(gather_copy, 1 - par, 0, block_chunks(n1, 0), 0, 2)

    @pl.when(e == ne - 1)
    def _():
        wait_all(scatter_copy, scatter_block, 0, 4)
        wait_all(scatter_copy, scatter_block, 1, 5)


def _experts(start, cnt, staged, wg, wu, wd):
    nsub = staged.shape[0] // CHUNKS_PER_SUB
    list_max = nsub * SUB // ROW_ALIGN + nsub + LIST_SLACK
    wblk = lambda e, *_: (e, 0, 0)
    grid_spec = pltpu.PrefetchScalarGridSpec(
        num_scalar_prefetch=2,
        grid=(N_EXPERTS,),
        in_specs=[
            pl.BlockSpec(memory_space=pl.ANY),
            pl.BlockSpec((1, D_MODEL, EXPERT_FF), wblk),
            pl.BlockSpec((1, D_MODEL, EXPERT_FF), wblk),
            pl.BlockSpec((1, EXPERT_FF, D_MODEL), wblk),
        ],
        out_specs=pl.BlockSpec(memory_space=pl.ANY),
        scratch_shapes=[
            pltpu.VMEM((2, CHUNKS_PER_BLK, ROW_ALIGN, ROW_W), BF16),
            pltpu.VMEM((2, CHUNKS_PER_BLK, ROW_ALIGN, ROW_W), BF16),
            pltpu.VMEM((D_MODEL, 2 * EXPERT_FF), BF16),
            pltpu.VMEM((EXPERT_FF, D_MODEL), BF16),
            pltpu.SemaphoreType.DMA((2,)),
            pltpu.SemaphoreType.DMA((2,)),
            pltpu.SMEM((2 * list_max,), jnp.int32),
            pltpu.SMEM((6,), jnp.int32),
        ],
    )
    return pl.pallas_call(
        functools.partial(_experts_body, nsub=nsub),
        grid_spec=grid_spec,
        out_shape=jax.ShapeDtypeStruct(staged.shape, staged.dtype),
        input_output_aliases={2: 0},
        compiler_params=_cparams(("arbitrary",)),
        name="moe_experts",
    )(start, cnt, staged, wg, wu, wd)


def _combine_body(ys_ref, pos_ref, x1_ref, p_ref, wp_ref, gp_ref, wpg_ref, gf_ref, y_ref):
    r = _iota_f32((SUB, SUBP), 1)
    moe = []
    for s in range(x1_ref.shape[0] // SUB):
        p1 = pos_ref[s * SUB:(s + 1) * SUB, 0:1]
        p2 = pos_ref[s * SUB:(s + 1) * SUB, 1:2]
        onehot = jnp.where(r == p1, 1.0, jnp.where(r == p2, 1.0, 0.0)).astype(BF16)
        ys = ys_ref[s * CHUNKS_PER_SUB:(s + 1) * CHUNKS_PER_SUB].reshape(SUBP, D_MODEL)
        moe.append(jnp.dot(onehot, ys, preferred_element_type=F32))
    x2 = x1_ref[...] + jnp.concatenate(moe, axis=0)
    ple = _rms(jnp.dot(p_ref[...].astype(BF16), wp_ref[...], preferred_element_type=F32), gp_ref[...])
    gate = jax.nn.sigmoid(jnp.dot(x2.astype(BF16), wpg_ref[...], preferred_element_type=F32))
    y_ref[...] = _rms(x2 + ple * gate, gf_ref[...])


def _combine(ys, pos, x1, p, w_ple, gp, w_ple_gate, gf, sub_off):
    t = x1.shape[0]
    tm = COMBINE_SUBS * SUB
    blk_off = sub_off // COMBINE_SUBS
    assert sub_off % COMBINE_SUBS == 0 and t % tm == 0
    row = lambda i: (i, 0)
    const = lambda i: (0, 0)
    return pl.pallas_call(
        _combine_body,
        grid=(t // tm,),
        in_specs=[
            pl.BlockSpec((COMBINE_SUBS * CHUNKS_PER_SUB, ROW_ALIGN, D_MODEL), lambda i: (i + blk_off, 0, 0)),
            pl.BlockSpec((tm, LANES), lambda i: (i + blk_off, 0)),
            pl.BlockSpec((tm, D_MODEL), row),
            pl.BlockSpec((tm, PLE_DIM), row),
            pl.BlockSpec((PLE_DIM, D_MODEL), const),
            pl.BlockSpec((1, D_MODEL), const),
            pl.BlockSpec((D_MODEL, D_MODEL), const),
            pl.BlockSpec((1, D_MODEL), const),
        ],
        out_specs=pl.BlockSpec((tm, D_MODEL), row),
        out_shape=jax.ShapeDtypeStruct((t, D_MODEL), F32),
        compiler_params=_cparams(("parallel",)),
        name="moe_combine_ple",
    )(ys, pos, x1, p, w_ple, gp, w_ple_gate, gf)


def _rope_tables(pos):
    half = HEAD_DIM // 2
    inv = ROPE_BASE ** (-jnp.arange(half, dtype=F32) / half)
    ang = pos[:, None] * inv[None, :]
    cos = jnp.cos(ang)
    sin = jnp.sin(ang)
    return jnp.concatenate([cos, cos], axis=-1), jnp.concatenate([-sin, sin], axis=-1)


def _router_params(we, be, wg, bg):
    pad = LANES - N_EXPERTS - N_GROUPS
    w = jnp.pad(jnp.concatenate([we, wg], axis=1), ((0, 0), (0, pad)))
    b = jnp.pad(jnp.concatenate([be, bg]), (0, pad))[None, :]
    return jnp.concatenate(_split3(w)[:2], axis=1), b


def kernel(x_prompt, x_sample, p_prompt, p_sample, state_conv, state_ret, w_in, conv_w, conv_b, conv_ln_g, conv_ln_b, w_out, norm1_g, norm2_g, router_group_w, router_group_b, router_expert_w, router_expert_b, w_expert_gate, w_expert_up, w_expert_down, w_ple, ple_norm_g, w_ple_gate, final_norm_g):
    assert w_in.shape[0] == 1, "single-layer trunk"
    nb, seq, _ = x_prompt.shape
    ns, dseq, _ = x_sample.shape
    tm = SAMPLE_TILE

    w_in_b = w_in[0].astype(BF16)
    w_out_b = w_out[0].astype(BF16)
    w_ple_b = w_ple[0].astype(BF16)
    w_pg_b = w_ple_gate[0].astype(BF16)
    g1 = norm1_g[0][None, :]
    g2 = norm2_g[0][None, :]
    gp = ple_norm_g[0][None, :]
    gf = final_norm_g[None, :]
    cb = conv_b[0][None, :]
    lng = conv_ln_g[0][None, :]
    lnb = conv_ln_b[0][None, :]
    wr2, br = _router_params(router_expert_w[0], router_expert_b[0], router_group_w[0], router_group_b[0])

    cos_p, sin_p = _rope_tables(jnp.arange(seq, dtype=F32) + jnp.float32(0))
    pos_s = jnp.tile(jnp.arange(dseq, dtype=F32) + jnp.float32(PAST_LEN), tm // dseq)
    cos_s, sin_s = _rope_tables(pos_s)

    xp = x_prompt.reshape(nb * seq, D_MODEL)
    x1_p, t_p, rec_p, conv_p, ret_p = _mix(xp, g1, w_in_b, cos_p, sin_p, conv_w[0], cb, lng, lnb, w_out_b, g2,
                                           wr2, br, nb, seq, MIX_TILE)

    xs = x_sample.reshape(ns * dseq, D_MODEL)
    u, q, k, v, gs = _inproj(xs, g1, w_in_b, cos_s, sin_s, tm, 1, F32)
    c, conv_s = _conv_sample(u.reshape(ns, dseq, CONV_CH), jnp.transpose(state_conv[0], (1, 0, 2)), conv_w[0], cb,
                             lng, lnb, CONV_SAMPLE_SEQS)
    conv_s = jnp.transpose(conv_s, (1, 0, 2))
    o, ret_s = _ret_sample(q, k, v, gs, state_ret[0], dseq, RET_SAMPLE_SEQS)
    x1_s, t_s, rec_s = _outproj(c.reshape(ns * dseq, CONV_CH), o, xs, w_out_b, g2, wr2, br, tm)

    staged, pos, meta = _dispatch(rec_p, t_p, rec_s, t_s)
    start = meta[:, 0, :N_EXPERTS].astype(jnp.int32).reshape(-1)
    cnt = meta[:, 1, :N_EXPERTS].astype(jnp.int32).reshape(-1)
    ys = _experts(start, cnt, staged, w_expert_gate[0], w_expert_up[0], w_expert_down[0])

    y_p = _combine(ys, pos, x1_p, p_prompt[0].reshape(nb * seq, PLE_DIM), w_ple_b, gp, w_pg_b, gf, 0)
    y_s = _combine(ys, pos, x1_s, p_sample[0].reshape(ns * dseq, PLE_DIM), w_ple_b, gp, w_pg_b, gf,
                   nb * seq // SUB)

    return (y_p.reshape(nb, seq, D_MODEL), y_s.reshape(ns, dseq, D_MODEL),
            conv_p[None], ret_p[None], conv_s[None], ret_s[None])
```

```python
import functools

import jax
import jax.numpy as jnp
from jax import lax
from jax.experimental import pallas as pl
from jax.experimental.pallas import tpu as pltpu

F32 = jnp.float32
BF16 = jnp.bfloat16

D_MODEL = 1024
PLE_DIM = 256
CONV_CH = 512
CONV_K = 31
RET_WIDTH = 512
RET_HEADS = 4
HEAD_DIM = 128
CHUNK = 128
ROPE_BASE = 10000.0
N_GROUPS = 4
EXPERTS_PER_GROUP = 8
N_EXPERTS = 32
EXPERT_FF = 256
IN_COLS = 3072
EPS = 1e-6
PAST_LEN = 16384

LANES = 128
SUBLANES = 8
HALO = 32
HALO_OFF = HALO - (CONV_K - 1)
VMEM_LIMIT = 48 * 1024 * 1024
MIX_VMEM_LIMIT = 56 * 1024 * 1024
MIX_TILE = 512
SAMPLE_TILE = 1024
CONV_SAMPLE_SEQS = 32
RET_SAMPLE_SEQS = 16

SUB = 256
ROW_ALIGN = 16
PBLK = 256
SUBP = -(-(2 * SUB + N_EXPERTS * (ROW_ALIGN - 1)) // PBLK) * PBLK
CHUNKS_PER_SUB = SUBP // ROW_ALIGN
ROW_W = D_MODEL + LANES
MBLK = 512
CHUNKS_PER_BLK = MBLK // ROW_ALIGN
LIST_SLACK = 2
COMBINE_SUBS = 4
DISPATCH_SUBS = 4


def _cparams(sem):
    return pltpu.CompilerParams(dimension_semantics=sem, vmem_limit_bytes=VMEM_LIMIT)


def _rms(x, g):
    return x * lax.rsqrt(jnp.mean(x * x, axis=-1, keepdims=True) + EPS) * g


def _inproj_body(x_ref, g1_ref, w_ref, cos_ref, sin_ref, u_ref, q_ref, k_ref, v_ref, gs_ref):
    h = _rms(x_ref[...], g1_ref[...]).astype(BF16)
    z = jnp.dot(h, w_ref[...], preferred_element_type=F32)
    a = z[:, :CONV_CH]
    b = z[:, CONV_CH:2 * CONV_CH]
    u_ref[...] = a * jax.nn.sigmoid(b)
    cos = cos_ref[...]
    sin = sin_ref[...]
    q0 = 2 * CONV_CH
    k0 = q0 + RET_WIDTH
    for hh in range(RET_HEADS):
        sl = slice(hh * HEAD_DIM, (hh + 1) * HEAD_DIM)
        qh = z[:, q0 + hh * HEAD_DIM:q0 + (hh + 1) * HEAD_DIM]
        kh = z[:, k0 + hh * HEAD_DIM:k0 + (hh + 1) * HEAD_DIM]
        q_ref[:, sl] = qh * cos + pltpu.roll(qh, HEAD_DIM // 2, 1) * sin
        k_ref[:, sl] = (kh * cos + pltpu.roll(kh, HEAD_DIM // 2, 1) * sin) * (HEAD_DIM ** -0.5)
    v_ref[...] = z[:, k0 + RET_WIDTH:k0 + 2 * RET_WIDTH]
    g = z[:, k0 + 2 * RET_WIDTH:]
    gs_ref[...] = g * jax.nn.sigmoid(g)


def _inproj(x, g1, w_in, cos, sin, tm):
    t = x.shape[0]
    row = lambda i: (i, 0)
    const = lambda i: (0, 0)
    return pl.pallas_call(
        _inproj_body,
        grid=(t // tm,),
        in_specs=[
            pl.BlockSpec((tm, D_MODEL), row),
            pl.BlockSpec((1, D_MODEL), const),
            pl.BlockSpec((D_MODEL, IN_COLS), const),
            pl.BlockSpec((tm, HEAD_DIM), const),
            pl.BlockSpec((tm, HEAD_DIM), const),
        ],
        out_specs=[pl.BlockSpec((tm, CONV_CH), row)] + [pl.BlockSpec((tm, RET_WIDTH), row)] * 4,
        out_shape=[jax.ShapeDtypeStruct((t, CONV_CH), F32)] + [jax.ShapeDtypeStruct((t, RET_WIDTH), F32)] * 4,
        compiler_params=_cparams(("parallel",)),
        name="inproj",
    )(x, g1, w_in, cos, sin)


def _ln_silu(acc, g, b):
    mu = jnp.mean(acc, axis=-1, keepdims=True)
    d = acc - mu
    var = jnp.mean(d * d, axis=-1, keepdims=True)
    y = d * lax.rsqrt(var + EPS) * g + b
    return y * jax.nn.sigmoid(y)


def _dwconv(load, w_ref, rows):
    acc = None
    for b in range(SUBLANES):
        part = None
        for a in range((CONV_K + HALO_OFF) // SUBLANES + 1):
            k = SUBLANES * a + b - HALO_OFF
            if 0 <= k < CONV_K:
                term = load(SUBLANES * a, rows + SUBLANES) * w_ref[k:k + 1, :]
                part = term if part is None else part + term
        if part is not None:
            shifted = part[b:b + rows]
            acc = shifted if acc is None else acc + shifted
    return acc


def _conv_sample_body(u_ref, st_ref, w_ref, cb_ref, lg_ref, lb_ref, c_ref, nst_ref, ext_ref):
    l = u_ref.shape[1]
    hist = CONV_K - 1
    ext_ref[0:hist] = st_ref[...]
    ext_ref[hist:] = jnp.transpose(u_ref[...], (1, 0, 2))
    acc = ext_ref[0:l] * w_ref[0:1, :] + cb_ref[...]
    for k in range(1, CONV_K):
        acc = acc + ext_ref[k:k + l] * w_ref[k:k + 1, :]
    y = _ln_silu(acc, lg_ref[...], lb_ref[...])
    c_ref[...] = jnp.transpose(y, (1, 0, 2)).astype(c_ref.dtype)
    nst_ref[...] = ext_ref[l:]


def _conv_sample(u, state_t, conv_w, conv_b, ln_g, ln_b, nb):
    n, l, _ = u.shape
    hist = CONV_K - 1
    const = lambda b: (0, 0)
    tok = lambda b: (b, 0, 0)
    tmaj = lambda b: (0, b, 0)
    return pl.pallas_call(
        _conv_sample_body,
        grid=(n // nb,),
        in_specs=[
            pl.BlockSpec((nb, l, CONV_CH), tok),
            pl.BlockSpec((hist, nb, CONV_CH), tmaj),
            pl.BlockSpec((CONV_K, CONV_CH), const),
            pl.BlockSpec((1, CONV_CH), const),
            pl.BlockSpec((1, CONV_CH), const),
            pl.BlockSpec((1, CONV_CH), const),
        ],
        out_specs=[
            pl.BlockSpec((nb, l, CONV_CH), tok),
            pl.BlockSpec((hist, nb, CONV_CH), tmaj),
        ],
        out_shape=[
            jax.ShapeDtypeStruct((n, l, CONV_CH), BF16),
            jax.ShapeDtypeStruct((hist, n, CONV_CH), F32),
        ],
        scratch_shapes=[pltpu.VMEM((hist + l, nb, CONV_CH), F32)],
        compiler_params=_cparams(("parallel",)),
        name="conv_sample",
    )(u, state_t, conv_w, conv_b, ln_g, ln_b)


def _decay_tables(c):
    lg = jnp.log(1.0 - 2.0 ** (-5.0 - jnp.arange(RET_HEADS, dtype=F32)))
    idx = jnp.arange(c, dtype=F32)
    rel = idx[:, None] - idx[None, :]
    dmat = jnp.where(rel[None] >= 0, jnp.exp(jnp.maximum(rel, 0.0)[None] * lg[:, None, None]), 0.0)
    xi = jnp.exp((idx + 1.0)[None, :] * lg[:, None])
    zeta = jnp.exp((c - 1.0 - idx)[None, :] * lg[:, None])
    gc = jnp.exp(c * lg)
    xi_b = jnp.broadcast_to(xi[:, :, None], (RET_HEADS, c, HEAD_DIM))
    zeta_b = jnp.broadcast_to(zeta[:, :, None], (RET_HEADS, c, HEAD_DIM))
    gc_b = jnp.broadcast_to(gc[:, None, None], (RET_HEADS, 1, HEAD_DIM))
    return dmat, xi_b, zeta_b, gc_b


def _group_norm(o):
    mu = jnp.mean(o, axis=-1, keepdims=True)
    d = o - mu
    var = jnp.mean(d * d, axis=-1, keepdims=True)
    return d * lax.rsqrt(var + EPS)


def _ret_chunk(qh, kh, vh, r, dmat, xi, zeta, gc):
    qb = qh.astype(BF16)
    kb = kh.astype(BF16)
    vb = vh.astype(BF16)
    s = lax.dot_general(qb, kb, (((1,), (1,)), ((), ())), preferred_element_type=F32) * dmat
    o = jnp.dot(s.astype(BF16), vb, preferred_element_type=F32)
    o = o + jnp.dot(qb, r.astype(BF16), preferred_element_type=F32) * xi
    kz = (kh.astype(F32) * zeta).astype(BF16)
    r_new = r * gc + lax.dot_general(kz, vb, (((0,), (0,)), ((), ())), preferred_element_type=F32)
    return o, r_new


def _ret_sample_body(q_ref, k_ref, v_ref, gs_ref, st_ref, d_ref, xi_ref, zeta_ref, gc_ref, o_ref, nst_ref):
    nb = st_ref.shape[0]
    l = q_ref.shape[0] // nb
    for b in range(nb):
        rows = slice(b * l, (b + 1) * l)
        for hh in range(RET_HEADS):
            sl = slice(hh * HEAD_DIM, (hh + 1) * HEAD_DIM)
            o, r = _ret_chunk(q_ref[rows, sl], k_ref[rows, sl], v_ref[rows, sl], st_ref[b, hh],
                              d_ref[hh], xi_ref[hh], zeta_ref[hh], gc_ref[hh])
            o_ref[rows, sl] = (gs_ref[rows, sl] * _group_norm(o)).astype(o_ref.dtype)
            nst_ref[b, hh] = r


def _ret_sample(q, k, v, gs, state, l, nb):
    n = state.shape[0]
    dmat, xi, zeta, gc = _decay_tables(l)
    row = lambda b: (b, 0)
    c3 = lambda b: (0, 0, 0)
    blk4 = lambda b: (b, 0, 0, 0)
    return pl.pallas_call(
        _ret_sample_body,
        grid=(n // nb,),
        in_specs=[pl.BlockSpec((nb * l, RET_WIDTH), row)] * 4 + [
            pl.BlockSpec((nb, RET_HEADS, HEAD_DIM, HEAD_DIM), blk4),
            pl.BlockSpec((RET_HEADS, l, l), c3),
            pl.BlockSpec((RET_HEADS, l, HEAD_DIM), c3),
            pl.BlockSpec((RET_HEADS, l, HEAD_DIM), c3),
            pl.BlockSpec((RET_HEADS, 1, HEAD_DIM), c3),
        ],
        out_specs=[
            pl.BlockSpec((nb * l, RET_WIDTH), row),
            pl.BlockSpec((nb, RET_HEADS, HEAD_DIM, HEAD_DIM), blk4),
        ],
        out_shape=[
            jax.ShapeDtypeStruct((n * l, RET_WIDTH), BF16),
            jax.ShapeDtypeStruct((n, RET_HEADS, HEAD_DIM, HEAD_DIM), F32),
        ],
        compiler_params=_cparams(("parallel",)),
        name="ret_sample",
    )(q, k, v, gs, state, dmat, xi, zeta, gc)


def _split3(x):
    hi = x.astype(BF16)
    r1 = x - hi.astype(F32)
    mid = r1.astype(BF16)
    lo = (r1 - mid.astype(F32)).astype(BF16)
    return hi, mid, lo


def _dot_hp(t, w2_ref):
    t_hi, t_mid, _ = _split3(t)
    d = functools.partial(jnp.dot, preferred_element_type=F32)
    both = d(t_hi, w2_ref[...])
    return both[:, :LANES] + (d(t_mid, w2_ref[:, 0:LANES]) + both[:, LANES:])


def _iota_f32(shape, dim):
    return lax.broadcasted_iota(jnp.int32, shape, dim).astype(F32)


def _route(logits):
    lt = logits.T
    tm = lt.shape[1]
    row = _iota_f32((SUBLANES, tm), 0)
    big = float(SUBLANES)

    def rmax(x):
        return jnp.max(x, axis=0, keepdims=True)

    def first_row(mask):
        return jnp.min(jnp.where(mask, row, big), axis=0, keepdims=True)

    lg = jnp.where(row < float(N_GROUPS), lt[N_EXPERTS:N_EXPERTS + SUBLANES, :], -1e30)
    m = rmax(lg)
    g_top = 1.0 / jnp.sum(jnp.exp(lg - m), axis=0, keepdims=True)
    g_idx = first_row(lg == m)
    lem = lt[0:EXPERTS_PER_GROUP, :]
    for g in range(1, N_GROUPS):
        lem = jnp.where(g_idx == float(g), lt[g * EXPERTS_PER_GROUP:(g + 1) * EXPERTS_PER_GROUP, :], lem)
    pe = jnp.exp(lem - rmax(lem))
    p1 = rmax(pe)
    e1 = first_row(pe == p1)
    rest = row != e1
    pe2 = jnp.where(rest, pe, -1.0)
    p2 = rmax(pe2)
    e2 = first_row(rest & (pe2 == p2))
    scale = g_top / (p1 + p2)
    base = g_idx * float(EXPERTS_PER_GROUP)
    rec_t = jnp.where(row == 0.0, base + e1, jnp.where(row == 1.0, base + e2, 0.0))
    rec_t = rec_t + jnp.where(row == 2.0, p1 * scale, jnp.where(row == 3.0, p2 * scale, 0.0))
    return jnp.concatenate([rec_t, jnp.zeros((LANES - SUBLANES, tm), F32)], axis=0).T


def _outproj_body(c_ref, o_ref, x_ref, wo_ref, g2_ref, wr_ref, br_ref, x1_ref, t_ref, rec_ref):
    x1 = x_ref[...] + jnp.dot(c_ref[...], wo_ref[0:CONV_CH, :], preferred_element_type=F32)
    x1 = x1 + jnp.dot(o_ref[...], wo_ref[CONV_CH:, :], preferred_element_type=F32)
    x1_ref[...] = x1
    t = _rms(x1, g2_ref[...])
    t_ref[...] = t.astype(t_ref.dtype)
    rec_ref[...] = _route(_dot_hp(t, wr_ref) + br_ref[...])


def _outproj(c, o, x, w_out, g2, wr2, br, tm):
    t = x.shape[0]
    row = lambda i: (i, 0)
    const = lambda i: (0, 0)
    return pl.pallas_call(
        _outproj_body,
        grid=(t // tm,),
        in_specs=[
            pl.BlockSpec((tm, CONV_CH), row),
            pl.BlockSpec((tm, RET_WIDTH), row),
            pl.BlockSpec((tm, D_MODEL), row),
            pl.BlockSpec((D_MODEL, D_MODEL), const),
            pl.BlockSpec((1, D_MODEL), const),
            pl.BlockSpec((D_MODEL, 2 * LANES), const),
            pl.BlockSpec((1, LANES), const),
        ],
        out_specs=[
            pl.BlockSpec((tm, D_MODEL), row),
            pl.BlockSpec((tm, D_MODEL), row),
            pl.BlockSpec((tm, LANES), row),
        ],
        out_shape=[
            jax.ShapeDtypeStruct((t, D_MODEL), F32),
            jax.ShapeDtypeStruct((t, D_MODEL), BF16),
            jax.ShapeDtypeStruct((t, LANES), F32),
        ],
        compiler_params=_cparams(("parallel",)),
        name="outproj_router",
    )(c, o, x, w_out, g2, wr2, br)


def _mix_body(x_ref, g1_ref, w_ref, cos_ref, sin_ref, cw_ref, cb_ref, lg_ref, lb_ref,
              d_ref, xi_ref, zeta_ref, gc_ref, wo_ref, g2_ref, wr_ref, br_ref,
              x1_ref, t_ref, rec_ref, cst_ref, rst_ref, ext_ref, r_ref, o_ref):
    j = pl.program_id(1)
    tl = x_ref.shape[0]
    x = x_ref[...]
    z = jnp.dot(_rms(x, g1_ref[...]).astype(BF16), w_ref[...], preferred_element_type=F32)

    @pl.when(j == 0)
    def _():
        ext_ref[0:HALO, :] = jnp.zeros((HALO, CONV_CH), F32)
        ext_ref[tl + HALO:, :] = jnp.zeros((SUBLANES, CONV_CH), F32)
        r_ref[...] = jnp.zeros_like(r_ref)

    @pl.when(j > 0)
    def _():
        ext_ref[0:HALO, :] = ext_ref[tl:tl + HALO, :]

    ext_ref[HALO:tl + HALO, :] = z[:, :CONV_CH] * jax.nn.sigmoid(z[:, CONV_CH:2 * CONV_CH])
    acc = _dwconv(lambda s, n: ext_ref[s:s + n, :], cw_ref, tl) + cb_ref[...]
    c = _ln_silu(acc, lg_ref[...], lb_ref[...]).astype(BF16)
    cst_ref[0] = ext_ref[tl + HALO_OFF:tl + HALO, :]

    cos = cos_ref[...]
    sin = sin_ref[...]
    q0 = 2 * CONV_CH
    k0 = q0 + RET_WIDTH
    v0 = k0 + RET_WIDTH
    g0 = v0 + RET_WIDTH
    for hh in range(RET_HEADS):
        lo = hh * HEAD_DIM
        qh = z[:, q0 + lo:q0 + lo + HEAD_DIM]
        kh = z[:, k0 + lo:k0 + lo + HEAD_DIM]
        qr = (qh * cos + pltpu.roll(qh, HEAD_DIM // 2, 1) * sin).astype(BF16)
        kr = ((kh * cos + pltpu.roll(kh, HEAD_DIM // 2, 1) * sin) * (HEAD_DIM ** -0.5)).astype(BF16)
        vh = z[:, v0 + lo:v0 + lo + HEAD_DIM].astype(BF16)
        g = z[:, g0 + lo:g0 + lo + HEAD_DIM]
        gs = g * jax.nn.sigmoid(g)
        r = r_ref[hh]
        for ci in range(tl // CHUNK):
            rows = slice(ci * CHUNK, (ci + 1) * CHUNK)
            o, r = _ret_chunk(qr[rows], kr[rows], vh[rows], r, d_ref[hh], xi_ref[hh], zeta_ref[hh], gc_ref[hh])
            o_ref[rows, lo:lo + HEAD_DIM] = (gs[rows] * _group_norm(o)).astype(BF16)
        r_ref[hh] = r
    rst_ref[0] = r_ref[...]

    x1 = x + jnp.dot(c, wo_ref[0:CONV_CH, :], preferred_element_type=F32)
    x1 = x1 + jnp.dot(o_ref[...], wo_ref[CONV_CH:, :], preferred_element_type=F32)
    x1_ref[...] = x1
    t = _rms(x1, g2_ref[...])
    t_ref[...] = t.astype(t_ref.dtype)
    rec_ref[...] = _route(_dot_hp(t, wr_ref) + br_ref[...])


def _mix(x, g1, w_in, cos, sin, conv_w, conv_b, ln_g, ln_b, w_out, g2, wr2, br, n, l, tl):
    dmat, xi, zeta, gc = _decay_tables(CHUNK)
    per = l // tl
    row = lambda b, j: (b * per + j, 0)
    tab = lambda b, j: (j, 0)
    const = lambda b, j: (0, 0)
    c3 = lambda b, j: (0, 0, 0)
    once = dict(pipeline_mode=pl.Buffered(1))
    return pl.pallas_call(
        _mix_body,
        grid=(n, per),
        in_specs=[
            pl.BlockSpec((tl, D_MODEL), row),
            pl.BlockSpec((1, D_MODEL), const),
            pl.BlockSpec((D_MODEL, IN_COLS), const, **once),
            pl.BlockSpec((tl, HEAD_DIM), tab),
            pl.BlockSpec((tl, HEAD_DIM), tab),
            pl.BlockSpec((CONV_K, CONV_CH), const),
            pl.BlockSpec((1, CONV_CH), const),
            pl.BlockSpec((1, CONV_CH), const),
            pl.BlockSpec((1, CONV_CH), const),
            pl.BlockSpec((RET_HEADS, CHUNK, CHUNK), c3),
            pl.BlockSpec((RET_HEADS, CHUNK, HEAD_DIM), c3),
            pl.BlockSpec((RET_HEADS, CHUNK, HEAD_DIM), c3),
            pl.BlockSpec((RET_HEADS, 1, HEAD_DIM), c3),
            pl.BlockSpec((D_MODEL, D_MODEL), const, **once),
            pl.BlockSpec((1, D_MODEL), const),
            pl.BlockSpec((D_MODEL, 2 * LANES), const, **once),
            pl.BlockSpec((1, LANES), const),
        ],
        out_specs=[
            pl.BlockSpec((tl, D_MODEL), row),
            pl.BlockSpec((tl, D_MODEL), row),
            pl.BlockSpec((tl, LANES), row),
            pl.BlockSpec((1, CONV_K - 1, CONV_CH), lambda b, j: (b, 0, 0)),
            pl.BlockSpec((1, RET_HEADS, HEAD_DIM, HEAD_DIM), lambda b, j: (b, 0, 0, 0)),
        ],
        out_shape=[
            jax.ShapeDtypeStruct((n * l, D_MODEL), F32),
            jax.ShapeDtypeStruct((n * l, D_MODEL), BF16),
            jax.ShapeDtypeStruct((n * l, LANES), F32),
            jax.ShapeDtypeStruct((n, CONV_K - 1, CONV_CH), F32),
            jax.ShapeDtypeStruct((n, RET_HEADS, HEAD_DIM, HEAD_DIM), F32),
        ],
        scratch_shapes=[
            pltpu.VMEM((tl + HALO + SUBLANES, CONV_CH), F32),
            pltpu.VMEM((RET_HEADS, HEAD_DIM, HEAD_DIM), F32),
            pltpu.VMEM((tl, RET_WIDTH), BF16),
        ],
        compiler_params=pltpu.CompilerParams(dimension_semantics=("arbitrary", "arbitrary"),
                                             vmem_limit_bytes=MIX_VMEM_LIMIT),
        name="token_mix",
    )(x, g1, w_in, cos, sin, conv_w, conv_b, ln_g, ln_b, dmat, xi, zeta, gc, w_out, g2, wr2, br)


def _dispatch_body(rec_a_ref, t_a_ref, rec_b_ref, t_b_ref, s_ref, pos_ref, meta_ref, *, nsub_a):
    from_a = pl.program_id(0) * DISPATCH_SUBS < nsub_a
    for s in range(DISPATCH_SUBS):
        rows = slice(s * SUB, (s + 1) * SUB)
        rec = jnp.where(from_a, rec_a_ref[rows, :], rec_b_ref[rows, :])
        tok = jnp.where(from_a, t_a_ref[rows, :], t_b_ref[rows, :])
        chunks = slice(s * CHUNKS_PER_SUB, (s + 1) * CHUNKS_PER_SUB)
        _dispatch_sub_tile(rec, tok, s_ref.at[chunks], pos_ref.at[rows], meta_ref.at[s])


def _dispatch_sub_tile(rec, tok, s_ref, pos_ref, meta_ref):
    lane = _iota_f32(rec.shape, 1)
    a1 = lane == rec[:, 0:1]
    a2 = lane == rec[:, 1:2]
    a1f = jnp.where(a1, 1.0, 0.0)
    a2f = jnp.where(a2, 1.0, 0.0)
    ltri = jnp.where(_iota_f32((SUB, SUB), 1) < _iota_f32((SUB, SUB), 0), 1.0, 0.0).astype(BF16)
    c1 = jnp.dot(ltri, a1f.astype(BF16), preferred_element_type=F32)
    c2 = jnp.dot(ltri, a2f.astype(BF16), preferred_element_type=F32)
    n1 = jnp.sum(a1f, axis=0, keepdims=True)
    n2 = jnp.sum(a2f, axis=0, keepdims=True)
    cnt = jnp.floor((n1 + n2 + (ROW_ALIGN - 1.0)) * (1.0 / ROW_ALIGN))
    utri = jnp.where(_iota_f32((LANES, LANES), 0) < _iota_f32((LANES, LANES), 1), 1.0, 0.0).astype(BF16)
    start = jnp.dot(jnp.broadcast_to(cnt, (SUBLANES, LANES)).astype(BF16), utri,
                    preferred_element_type=F32)[0:1]
    base1 = start * ROW_ALIGN
    base2 = base1 + n1
    pos1 = jnp.sum(jnp.where(a1, c1 + base1, 0.0), axis=1, keepdims=True)
    pos2 = jnp.sum(jnp.where(a2, c2 + base2, 0.0), axis=1, keepdims=True)
    posm = jnp.where(lane == 0.0, pos1, jnp.where(lane == 1.0, pos2, 0.0))
    pos_ref[...] = posm
    row = _iota_f32((SUBLANES, LANES), 0)
    meta_ref[...] = jnp.where(row == 0.0, start, jnp.where(row == 1.0, cnt, 0.0))

    g1 = _split3(rec[:, 2:3])
    g2 = _split3(rec[:, 3:4])
    info = jnp.where(lane == 6.0, rec[:, 0:1], jnp.where(lane == 7.0, rec[:, 1:2], 0.0))
    for i in range(3):
        info = jnp.where(lane == float(i), g1[i].astype(F32), info)
        info = jnp.where(lane == float(3 + i), g2[i].astype(F32), info)
    src = jnp.concatenate([tok, info.astype(BF16)], axis=1)

    post = posm.T
    r = _iota_f32((SUBP, SUB), 0)
    onehot = jnp.where(r == post[0:1, :], 1.0, jnp.where(r == post[1:2, :], 1.0, 0.0)).astype(BF16)
    sorted_rows = jnp.dot(onehot, src, preferred_element_type=F32).astype(BF16)
    s_ref[...] = sorted_rows.reshape(CHUNKS_PER_SUB, ROW_ALIGN, ROW_W)


def _dispatch(rec_a, t_a, rec_b, t_b):
    nsub_a = rec_a.shape[0] // SUB
    nsub_b = rec_b.shape[0] // SUB
    nsub = nsub_a + nsub_b
    assert nsub_a % DISPATCH_SUBS == 0 and nsub_b % DISPATCH_SUBS == 0
    steps_a = nsub_a // DISPATCH_SUBS
    tm = DISPATCH_SUBS * SUB
    row = lambda i: (i, 0)
    row_a = lambda i: (jnp.minimum(i, steps_a - 1), 0)
    row_b = lambda i: (jnp.maximum(i - steps_a, 0), 0)
    return pl.pallas_call(
        functools.partial(_dispatch_body, nsub_a=nsub_a),
        grid=(nsub // DISPATCH_SUBS,),
        in_specs=[
            pl.BlockSpec((tm, LANES), row_a),
            pl.BlockSpec((tm, D_MODEL), row_a),
            pl.BlockSpec((tm, LANES), row_b),
            pl.BlockSpec((tm, D_MODEL), row_b),
        ],
        out_specs=[
            pl.BlockSpec((DISPATCH_SUBS * CHUNKS_PER_SUB, ROW_ALIGN, ROW_W), lambda i: (i, 0, 0)),
            pl.BlockSpec((tm, LANES), row),
            pl.BlockSpec((DISPATCH_SUBS, SUBLANES, LANES), lambda i: (i, 0, 0)),
        ],
        out_shape=[
            jax.ShapeDtypeStruct((nsub * CHUNKS_PER_SUB, ROW_ALIGN, ROW_W), BF16),
            jax.ShapeDtypeStruct((nsub * SUB, LANES), F32),
            jax.ShapeDtypeStruct((nsub, SUBLANES, LANES), F32),
        ],
        compiler_params=_cparams(("parallel",)),
        name="moe_dispatch",
    )(rec_a, t_a, rec_b, t_b)


def _experts_body(start_ref, cnt_ref, s_in, wg_ref, wu_ref, wd_ref, s_hbm,
                  xbuf, ybuf, wgu_ref, wdb_ref, gsem, ssem, list_ref, state_ref, *, nsub):
    del s_in
    e = pl.program_id(0)
    ne = pl.num_programs(0)
    par = e & 1

    list_max = list_ref.shape[0] // 2

    def gather_copy(src, i, slot):
        return pltpu.make_async_copy(s_hbm.at[src], xbuf.at[slot, i], gsem.at[slot])

    def scatter_copy(dst, i, slot):
        return pltpu.make_async_copy(ybuf.at[slot, i], s_hbm.at[dst], ssem.at[slot])

    def build_list(x, which):
        def per_sub(s, k):
            run = s * N_EXPERTS + x
            c = cnt_ref[run]
            base = s * CHUNKS_PER_SUB + start_ref[run]
            list_ref[k] = base
            list_ref[k + 1] = base + 1

            def per_chunk(i, carry):
                list_ref[k + i] = base + i
                return carry
            lax.fori_loop(2, c, per_chunk, 0)
            return k + c
        first = which * list_max
        state_ref[which] = lax.fori_loop(0, nsub, per_sub, first) - first

    def start_all(copy, which, first, n, slot, counter):
        def body(i, carry):
            copy(list_ref[which * list_max + first + i], i, slot).start()
            return carry
        lax.fori_loop(0, n, body, 0)
        state_ref[counter] = n

    def wait_all(copy, block_copy, slot, counter):
        n = state_ref[counter]

        @pl.when(n == CHUNKS_PER_BLK)
        def _():
            block_copy(slot).wait()

        @pl.when(n < CHUNKS_PER_BLK)
        def _():
            def body(i, carry):
                copy(0, 0, slot).wait()
                return carry
            lax.fori_loop(0, n, body, 0)
        state_ref[counter] = 0

    def gather_block(slot):
        return pltpu.make_async_copy(s_hbm.at[pl.ds(0, CHUNKS_PER_BLK)], xbuf.at[slot], gsem.at[slot])

    def scatter_block(slot):
        return pltpu.make_async_copy(ybuf.at[slot], s_hbm.at[pl.ds(0, CHUNKS_PER_BLK)], ssem.at[slot])

    def block_chunks(total, b):
        return jnp.minimum(total - b * CHUNKS_PER_BLK, CHUNKS_PER_BLK)

    @pl.when(e == 0)
    def _():
        for i in range(6):
            state_ref[i] = 0
        xbuf[...] = jnp.zeros_like(xbuf)
        build_list(0, 0)
        n0 = state_ref[0]

        @pl.when(n0 > 0)
        def _():
            start_all(gather_copy, 0, 0, block_chunks(n0, 0), 0, 2)

    @pl.when(e + 1 < ne)
    def _():
        build_list(e + 1, 1 - par)

    total = state_ref[par]
    nblk = (total + CHUNKS_PER_BLK - 1) // CHUNKS_PER_BLK
    wgu_ref[:, 0:EXPERT_FF] = wg_ref[0].astype(BF16)
    wgu_ref[:, EXPERT_FF:] = wu_ref[0].astype(BF16)
    wdb_ref[...] = wd_ref[0].astype(BF16)
    ef = e.astype(F32)

    def block(b, carry):
        slot = b & 1
        first = b * CHUNKS_PER_BLK

        @pl.when(b + 1 < nblk)
        def _():
            start_all(gather_copy, par, first + CHUNKS_PER_BLK, block_chunks(total, b + 1), 1 - slot, 3 - slot)

        wait_all(gather_copy, gather_block, slot, 2 + slot)
        wait_all(scatter_copy, scatter_block, slot, 4 + slot)
        x = xbuf[slot].reshape(MBLK, ROW_W)
        info = x[:, D_MODEL:].astype(F32)
        g_first = info[:, 0:1] + info[:, 1:2] + info[:, 2:3]
        g_second = info[:, 3:4] + info[:, 4:5] + info[:, 5:6]
        gate = jnp.where(info[:, 6:7] == ef, g_first, g_second)
        h = jnp.dot(x[:, :D_MODEL], wgu_ref[...], preferred_element_type=F32)
        h1 = h[:, :EXPERT_FF]
        hid = (h1 * jax.nn.sigmoid(h1)) * h[:, EXPERT_FF:] * gate
        y = jnp.dot(hid.astype(BF16), wdb_ref[...], preferred_element_type=F32).astype(BF16)
        ybuf[slot] = jnp.concatenate([y, x[:, D_MODEL:]], axis=1).reshape(CHUNKS_PER_BLK, ROW_ALIGN, ROW_W)
        start_all(scatter_copy, par, first, block_chunks(total, b), slot, 4 + slot)
        return carry

    lax.fori_loop(0, nblk, block, 0)

    @pl.when(e + 1 < ne)
    def _():
        n1 = state_ref[1 - par]

        @pl.when(n1 > 0)
        def _():
            start_all(gather_copy, 1 - par, 0, block_chunks(n1, 0), 0, 2)

    @pl.when(e == ne - 1)
    def _():
        wait_all(scatter_copy, scatter_block, 0, 4)
        wait_all(scatter_copy, scatter_block, 1, 5)


def _experts(start, cnt, staged, wg, wu, wd):
    nsub = staged.shape[0] // CHUNKS_PER_SUB
    list_max = nsub * SUB // ROW_ALIGN + nsub + LIST_SLACK
    wblk = lambda e, *_: (e, 0, 0)
    grid_spec = pltpu.PrefetchScalarGridSpec(
        num_scalar_prefetch=2,
        grid=(N_EXPERTS,),
        in_specs=[
            pl.BlockSpec(memory_space=pl.ANY),
            pl.BlockSpec((1, D_MODEL, EXPERT_FF), wblk),
            pl.BlockSpec((1, D_MODEL, EXPERT_FF), wblk),
            pl.BlockSpec((1, EXPERT_FF, D_MODEL), wblk),
        ],
        out_specs=pl.BlockSpec(memory_space=pl.ANY),
        scratch_shapes=[
            pltpu.VMEM((2, CHUNKS_PER_BLK, ROW_ALIGN, ROW_W), BF16),
            pltpu.VMEM((2, CHUNKS_PER_BLK, ROW_ALIGN, ROW_W), BF16),
            pltpu.VMEM((D_MODEL, 2 * EXPERT_FF), BF16),
            pltpu.VMEM((EXPERT_FF, D_MODEL), BF16),
            pltpu.SemaphoreType.DMA((2,)),
            pltpu.SemaphoreType.DMA((2,)),
            pltpu.SMEM((2 * list_max,), jnp.int32),
            pltpu.SMEM((6,), jnp.int32),
        ],
    )
    return pl.pallas_call(
        functools.partial(_experts_body, nsub=nsub),
        grid_spec=grid_spec,
        out_shape=jax.ShapeDtypeStruct(staged.shape, staged.dtype),
        input_output_aliases={2: 0},
        compiler_params=_cparams(("arbitrary",)),
        name="moe_experts",
    )(start, cnt, staged, wg, wu, wd)


def _combine_body(ys_ref, pos_ref, x1_ref, p_ref, wp_ref, gp_ref, wpg_ref, gf_ref, y_ref):
    r = _iota_f32((SUB, SUBP), 1)
    moe = []
    for s in range(x1_ref.shape[0] // SUB):
        p1 = pos_ref[s * SUB:(s + 1) * SUB, 0:1]
        p2 = pos_ref[s * SUB:(s + 1) * SUB, 1:2]
        onehot = jnp.where(r == p1, 1.0, jnp.where(r == p2, 1.0, 0.0)).astype(BF16)
        ys = ys_ref[s * CHUNKS_PER_SUB:(s + 1) * CHUNKS_PER_SUB].reshape(SUBP, D_MODEL)
        moe.append(jnp.dot(onehot, ys, preferred_element_type=F32))
    x2 = x1_ref[...] + jnp.concatenate(moe, axis=0)
    ple = _rms(jnp.dot(p_ref[...].astype(BF16), wp_ref[...], preferred_element_type=F32), gp_ref[...])
    gate = jax.nn.sigmoid(jnp.dot(x2.astype(BF16), wpg_ref[...], preferred_element_type=F32))
    y_ref[...] = _rms(x2 + ple * gate, gf_ref[...])


def _combine(ys, pos, x1, p, w_ple, gp, w_ple_gate, gf, sub_off):
    t = x1.shape[0]
    tm = COMBINE_SUBS * SUB
    blk_off = sub_off // COMBINE_SUBS
    assert sub_off % COMBINE_SUBS == 0 and t % tm == 0
    row = lambda i: (i, 0)
    const = lambda i: (0, 0)
    return pl.pallas_call(
        _combine_body,
        grid=(t // tm,),
        in_specs=[
            pl.BlockSpec((COMBINE_SUBS * CHUNKS_PER_SUB, ROW_ALIGN, D_MODEL), lambda i: (i + blk_off, 0, 0)),
            pl.BlockSpec((tm, LANES), lambda i: (i + blk_off, 0)),
            pl.BlockSpec((tm, D_MODEL), row),
            pl.BlockSpec((tm, PLE_DIM), row),
            pl.BlockSpec((PLE_DIM, D_MODEL), const),
            pl.BlockSpec((1, D_MODEL), const),
            pl.BlockSpec((D_MODEL, D_MODEL), const),
            pl.BlockSpec((1, D_MODEL), const),
        ],
        out_specs=pl.BlockSpec((tm, D_MODEL), row),
        out_shape=jax.ShapeDtypeStruct((t, D_MODEL), F32),
        compiler_params=_cparams(("parallel",)),
        name="moe_combine_ple",
    )(ys, pos, x1, p, w_ple, gp, w_ple_gate, gf)


def _rope_tables(pos):
    half = HEAD_DIM // 2
    inv = ROPE_BASE ** (-jnp.arange(half, dtype=F32) / half)
    ang = pos[:, None] * inv[None, :]
    cos = jnp.cos(ang)
    sin = jnp.sin(ang)
    return jnp.concatenate([cos, cos], axis=-1), jnp.concatenate([-sin, sin], axis=-1)


def _router_params(we, be, wg, bg):
    pad = LANES - N_EXPERTS - N_GROUPS
    w = jnp.pad(jnp.concatenate([we, wg], axis=1), ((0, 0), (0, pad)))
    b = jnp.pad(jnp.concatenate([be, bg]), (0, pad))[None, :]
    return jnp.concatenate(_split3(w)[:2], axis=1), b


def kernel(x_prompt, x_sample, p_prompt, p_sample, state_conv, state_ret, w_in, conv_w, conv_b, conv_ln_g, conv_ln_b, w_out, norm1_g, norm2_g, router_group_w, router_group_b, router_expert_w, router_expert_b, w_expert_gate, w_expert_up, w_expert_down, w_ple, ple_norm_g, w_ple_gate, final_norm_g):
    assert w_in.shape[0] == 1, "single-layer trunk"
    nb, seq, _ = x_prompt.shape
    ns, dseq, _ = x_sample.shape
    tm = SAMPLE_TILE

    w_in_b = w_in[0].astype(BF16)
    w_out_b = w_out[0].astype(BF16)
    w_ple_b = w_ple[0].astype(BF16)
    w_pg_b = w_ple_gate[0].astype(BF16)
    g1 = norm1_g[0][None, :]
    g2 = norm2_g[0][None, :]
    gp = ple_norm_g[0][None, :]
    gf = final_norm_g[None, :]
    cb = conv_b[0][None, :]
    lng = conv_ln_g[0][None, :]
    lnb = conv_ln_b[0][None, :]
    wr2, br = _router_params(router_expert_w[0], router_expert_b[0], router_group_w[0], router_group_b[0])

    cos_p, sin_p = _rope_tables(jnp.arange(seq, dtype=F32) + jnp.float32(0))
    pos_s = jnp.tile(jnp.arange(dseq, dtype=F32) + jnp.float32(PAST_LEN), tm // dseq)
    cos_s, sin_s = _rope_tables(pos_s)

    xp = x_prompt.reshape(nb * seq, D_MODEL)
    x1_p, t_p, rec_p, conv_p, ret_p = _mix(xp, g1, w_in_b, cos_p, sin_p, conv_w[0], cb, lng, lnb, w_out_b, g2,
                                           wr2, br, nb, seq, MIX_TILE)

    xs = x_sample.reshape(ns * dseq, D_MODEL)
    u, q, k, v, gs = _inproj(xs, g1, w_in_b, cos_s, sin_s, tm)
    c, conv_s = _conv_sample(u.reshape(ns, dseq, CONV_CH), jnp.transpose(state_conv[0], (1, 0, 2)), conv_w[0], cb,
                             lng, lnb, CONV_SAMPLE_SEQS)
    conv_s = jnp.transpose(conv_s, (1, 0, 2))
    o, ret_s = _ret_sample(q, k, v, gs, state_ret[0], dseq, RET_SAMPLE_SEQS)
    x1_s, t_s, rec_s = _outproj(c.reshape(ns * dseq, CONV_CH), o, xs, w_out_b, g2, wr2, br, tm)

    staged, pos, meta = _dispatch(rec_p, t_p, rec_s, t_s)
    start = meta[:, 0, :N_EXPERTS].astype(jnp.int32).reshape(-1)
    cnt = meta[:, 1, :N_EXPERTS].astype(jnp.int32).reshape(-1)
    ys = _experts(start, cnt, staged, w_expert_gate[0], w_expert_up[0], w_expert_down[0])

    y_p = _combine(ys, pos, x1_p, p_prompt[0].reshape(nb * seq, PLE_DIM), w_ple_b, gp, w_pg_b, gf, 0)
    y_s = _combine(ys, pos, x1_s, p_sample[0].reshape(ns * dseq, PLE_DIM), w_ple_b, gp, w_pg_b, gf,
                   nb * seq // SUB)

    return (y_p.reshape(nb, seq, D_MODEL), y_s.reshape(ns, dseq, D_MODEL),
            conv_p[None], ret_p[None], conv_s[None], ret_s[None])
```

```python
import functools

import jax
import jax.numpy as jnp
from jax import lax
from jax.experimental import pallas as pl
from jax.experimental.pallas import tpu as pltpu

F32 = jnp.float32
BF16 = jnp.bfloat16

D_MODEL = 1024
PLE_DIM = 256
CONV_CH = 512
CONV_K = 31
RET_WIDTH = 512
RET_HEADS = 4
HEAD_DIM = 128
CHUNK = 128
ROPE_BASE = 10000.0
N_GROUPS = 4
EXPERTS_PER_GROUP = 8
N_EXPERTS = 32
EXPERT_FF = 256
IN_COLS = 3072
EPS = 1e-6
PAST_LEN = 16384

LANES = 128
SUBLANES = 8
HALO = 32
HALO_OFF = HALO - (CONV_K - 1)
VMEM_LIMIT = 48 * 1024 * 1024
MIX_VMEM_LIMIT = 56 * 1024 * 1024
MIX_TILE = 512
SAMPLE_TILE = 1024
CONV_SAMPLE_SEQS = 32
RET_SAMPLE_SEQS = 16

SUB = 512
ROW_ALIGN = 16
PBLK = 256
SUBP = -(-(2 * SUB + N_EXPERTS * (ROW_ALIGN - 1)) // PBLK) * PBLK
CHUNKS_PER_SUB = SUBP // ROW_ALIGN
ROW_W = D_MODEL + LANES
MBLK = 512
CHUNKS_PER_BLK = MBLK // ROW_ALIGN
LIST_SLACK = 2
COMBINE_SUBS = 2
DISPATCH_SUBS = 2


def _cparams(sem):
    return pltpu.CompilerParams(dimension_semantics=sem, vmem_limit_bytes=VMEM_LIMIT)


def _rms(x, g):
    return x * lax.rsqrt(jnp.mean(x * x, axis=-1, keepdims=True) + EPS) * g


def _inproj_body(x_ref, g1_ref, w_ref, cos_ref, sin_ref, u_ref, q_ref, k_ref, v_ref, gs_ref):
    h = _rms(x_ref[...], g1_ref[...]).astype(BF16)
    z = jnp.dot(h, w_ref[...], preferred_element_type=F32)
    a = z[:, :CONV_CH]
    b = z[:, CONV_CH:2 * CONV_CH]
    u_ref[...] = a * jax.nn.sigmoid(b)
    cos = cos_ref[...]
    sin = sin_ref[...]
    q0 = 2 * CONV_CH
    k0 = q0 + RET_WIDTH
    for hh in range(RET_HEADS):
        sl = slice(hh * HEAD_DIM, (hh + 1) * HEAD_DIM)
        qh = z[:, q0 + hh * HEAD_DIM:q0 + (hh + 1) * HEAD_DIM]
        kh = z[:, k0 + hh * HEAD_DIM:k0 + (hh + 1) * HEAD_DIM]
        q_ref[:, sl] = qh * cos + pltpu.roll(qh, HEAD_DIM // 2, 1) * sin
        k_ref[:, sl] = (kh * cos + pltpu.roll(kh, HEAD_DIM // 2, 1) * sin) * (HEAD_DIM ** -0.5)
    v_ref[...] = z[:, k0 + RET_WIDTH:k0 + 2 * RET_WIDTH]
    g = z[:, k0 + 2 * RET_WIDTH:]
    gs_ref[...] = g * jax.nn.sigmoid(g)


def _inproj(x, g1, w_in, cos, sin, tm):
    t = x.shape[0]
    row = lambda i: (i, 0)
    const = lambda i: (0, 0)
    return pl.pallas_call(
        _inproj_body,
        grid=(t // tm,),
        in_specs=[
            pl.BlockSpec((tm, D_MODEL), row),
            pl.BlockSpec((1, D_MODEL), const),
            pl.BlockSpec((D_MODEL, IN_COLS), const),
            pl.BlockSpec((tm, HEAD_DIM), const),
            pl.BlockSpec((tm, HEAD_DIM), const),
        ],
        out_specs=[pl.BlockSpec((tm, CONV_CH), row)] + [pl.BlockSpec((tm, RET_WIDTH), row)] * 4,
        out_shape=[jax.ShapeDtypeStruct((t, CONV_CH), F32)] + [jax.ShapeDtypeStruct((t, RET_WIDTH), F32)] * 4,
        compiler_params=_cparams(("parallel",)),
        name="inproj",
    )(x, g1, w_in, cos, sin)


def _ln_silu(acc, g, b):
    mu = jnp.mean(acc, axis=-1, keepdims=True)
    d = acc - mu
    var = jnp.mean(d * d, axis=-1, keepdims=True)
    y = d * lax.rsqrt(var + EPS) * g + b
    return y * jax.nn.sigmoid(y)


def _dwconv(load, w_ref, rows):
    acc = None
    for b in range(SUBLANES):
        part = None
        for a in range((CONV_K + HALO_OFF) // SUBLANES + 1):
            k = SUBLANES * a + b - HALO_OFF
            if 0 <= k < CONV_K:
                term = load(SUBLANES * a, rows + SUBLANES) * w_ref[k:k + 1, :]
                part = term if part is None else part + term
        if part is not None:
            shifted = part[b:b + rows]
            acc = shifted if acc is None else acc + shifted
    return acc


def _conv_sample_body(u_ref, st_ref, w_ref, cb_ref, lg_ref, lb_ref, c_ref, nst_ref, ext_ref):
    l = u_ref.shape[1]
    hist = CONV_K - 1
    ext_ref[0:hist] = st_ref[...]
    ext_ref[hist:] = jnp.transpose(u_ref[...], (1, 0, 2))
    acc = ext_ref[0:l] * w_ref[0:1, :] + cb_ref[...]
    for k in range(1, CONV_K):
        acc = acc + ext_ref[k:k + l] * w_ref[k:k + 1, :]
    y = _ln_silu(acc, lg_ref[...], lb_ref[...])
    c_ref[...] = jnp.transpose(y, (1, 0, 2)).astype(c_ref.dtype)
    nst_ref[...] = ext_ref[l:]


def _conv_sample(u, state_t, conv_w, conv_b, ln_g, ln_b, nb):
    n, l, _ = u.shape
    hist = CONV_K - 1
    const = lambda b: (0, 0)
    tok = lambda b: (b, 0, 0)
    tmaj = lambda b: (0, b, 0)
    return pl.pallas_call(
        _conv_sample_body,
        grid=(n // nb,),
        in_specs=[
            pl.BlockSpec((nb, l, CONV_CH), tok),
            pl.BlockSpec((hist, nb, CONV_CH), tmaj),
            pl.BlockSpec((CONV_K, CONV_CH), const),
            pl.BlockSpec((1, CONV_CH), const),
            pl.BlockSpec((1, CONV_CH), const),
            pl.BlockSpec((1, CONV_CH), const),
        ],
        out_specs=[
            pl.BlockSpec((nb, l, CONV_CH), tok),
            pl.BlockSpec((hist, nb, CONV_CH), tmaj),
        ],
        out_shape=[
            jax.ShapeDtypeStruct((n, l, CONV_CH), BF16),
            jax.ShapeDtypeStruct((hist, n, CONV_CH), F32),
        ],
        scratch_shapes=[pltpu.VMEM((hist + l, nb, CONV_CH), F32)],
        compiler_params=_cparams(("parallel",)),
        name="conv_sample",
    )(u, state_t, conv_w, conv_b, ln_g, ln_b)


def _decay_tables(c):
    lg = jnp.log(1.0 - 2.0 ** (-5.0 - jnp.arange(RET_HEADS, dtype=F32)))
    idx = jnp.arange(c, dtype=F32)
    rel = idx[:, None] - idx[None, :]
    dmat = jnp.where(rel[None] >= 0, jnp.exp(jnp.maximum(rel, 0.0)[None] * lg[:, None, None]), 0.0)
    xi = jnp.exp((idx + 1.0)[None, :] * lg[:, None])
    zeta = jnp.exp((c - 1.0 - idx)[None, :] * lg[:, None])
    gc = jnp.exp(c * lg)
    xi_b = jnp.broadcast_to(xi[:, :, None], (RET_HEADS, c, HEAD_DIM))
    zeta_b = jnp.broadcast_to(zeta[:, :, None], (RET_HEADS, c, HEAD_DIM))
    gc_b = jnp.broadcast_to(gc[:, None, None], (RET_HEADS, 1, HEAD_DIM))
    return dmat, xi_b, zeta_b, gc_b


def _group_norm(o):
    mu = jnp.mean(o, axis=-1, keepdims=True)
    d = o - mu
    var = jnp.mean(d * d, axis=-1, keepdims=True)
    return d * lax.rsqrt(var + EPS)


def _ret_chunk(qh, kh, vh, r, dmat, xi, zeta, gc):
    qb = qh.astype(BF16)
    kb = kh.astype(BF16)
    vb = vh.astype(BF16)
    s = lax.dot_general(qb, kb, (((1,), (1,)), ((), ())), preferred_element_type=F32) * dmat
    o = jnp.dot(s.astype(BF16), vb, preferred_element_type=F32)
    o = o + jnp.dot(qb, r.astype(BF16), preferred_element_type=F32) * xi
    kz = (kh.astype(F32) * zeta).astype(BF16)
    r_new = r * gc + lax.dot_general(kz, vb, (((0,), (0,)), ((), ())), preferred_element_type=F32)
    return o, r_new


def _ret_sample_body(q_ref, k_ref, v_ref, gs_ref, st_ref, d_ref, xi_ref, zeta_ref, gc_ref, o_ref, nst_ref):
    nb = st_ref.shape[0]
    l = q_ref.shape[0] // nb
    for b in range(nb):
        rows = slice(b * l, (b + 1) * l)
        for hh in range(RET_HEADS):
            sl = slice(hh * HEAD_DIM, (hh + 1) * HEAD_DIM)
            o, r = _ret_chunk(q_ref[rows, sl], k_ref[rows, sl], v_ref[rows, sl], st_ref[b, hh],
                              d_ref[hh], xi_ref[hh], zeta_ref[hh], gc_ref[hh])
            o_ref[rows, sl] = (gs_ref[rows, sl] * _group_norm(o)).astype(o_ref.dtype)
            nst_ref[b, hh] = r


def _ret_sample(q, k, v, gs, state, l, nb):
    n = state.shape[0]
    dmat, xi, zeta, gc = _decay_tables(l)
    row = lambda b: (b, 0)
    c3 = lambda b: (0, 0, 0)
    blk4 = lambda b: (b, 0, 0, 0)
    return pl.pallas_call(
        _ret_sample_body,
        grid=(n // nb,),
        in_specs=[pl.BlockSpec((nb * l, RET_WIDTH), row)] * 4 + [
            pl.BlockSpec((nb, RET_HEADS, HEAD_DIM, HEAD_DIM), blk4),
            pl.BlockSpec((RET_HEADS, l, l), c3),
            pl.BlockSpec((RET_HEADS, l, HEAD_DIM), c3),
            pl.BlockSpec((RET_HEADS, l, HEAD_DIM), c3),
            pl.BlockSpec((RET_HEADS, 1, HEAD_DIM), c3),
        ],
        out_specs=[
            pl.BlockSpec((nb * l, RET_WIDTH), row),
            pl.BlockSpec((nb, RET_HEADS, HEAD_DIM, HEAD_DIM), blk4),
        ],
        out_shape=[
            jax.ShapeDtypeStruct((n * l, RET_WIDTH), BF16),
            jax.ShapeDtypeStruct((n, RET_HEADS, HEAD_DIM, HEAD_DIM), F32),
        ],
        compiler_params=_cparams(("parallel",)),
        name="ret_sample",
    )(q, k, v, gs, state, dmat, xi, zeta, gc)


def _split3(x):
    hi = x.astype(BF16)
    r1 = x - hi.astype(F32)
    mid = r1.astype(BF16)
    lo = (r1 - mid.astype(F32)).astype(BF16)
    return hi, mid, lo


def _dot_hp(t, w2_ref):
    t_hi, t_mid, _ = _split3(t)
    d = functools.partial(jnp.dot, preferred_element_type=F32)
    both = d(t_hi, w2_ref[...])
    return both[:, :LANES] + (d(t_mid, w2_ref[:, 0:LANES]) + both[:, LANES:])


def _iota_f32(shape, dim):
    return lax.broadcasted_iota(jnp.int32, shape, dim).astype(F32)


def _route(logits):
    lt = logits.T
    tm = lt.shape[1]
    row = _iota_f32((SUBLANES, tm), 0)
    big = float(SUBLANES)

    def rmax(x):
        return jnp.max(x, axis=0, keepdims=True)

    def first_row(mask):
        return jnp.min(jnp.where(mask, row, big), axis=0, keepdims=True)

    lg = jnp.where(row < float(N_GROUPS), lt[N_EXPERTS:N_EXPERTS + SUBLANES, :], -1e30)
    m = rmax(lg)
    g_top = 1.0 / jnp.sum(jnp.exp(lg - m), axis=0, keepdims=True)
    g_idx = first_row(lg == m)
    lem = lt[0:EXPERTS_PER_GROUP, :]
    for g in range(1, N_GROUPS):
        lem = jnp.where(g_idx == float(g), lt[g * EXPERTS_PER_GROUP:(g + 1) * EXPERTS_PER_GROUP, :], lem)
    pe = jnp.exp(lem - rmax(lem))
    p1 = rmax(pe)
    e1 = first_row(pe == p1)
    rest = row != e1
    pe2 = jnp.where(rest, pe, -1.0)
    p2 = rmax(pe2)
    e2 = first_row(rest & (pe2 == p2))
    scale = g_top / (p1 + p2)
    base = g_idx * float(EXPERTS_PER_GROUP)
    rec_t = jnp.where(row == 0.0, base + e1, jnp.where(row == 1.0, base + e2, 0.0))
    rec_t = rec_t + jnp.where(row == 2.0, p1 * scale, jnp.where(row == 3.0, p2 * scale, 0.0))
    return jnp.concatenate([rec_t, jnp.zeros((LANES - SUBLANES, tm), F32)], axis=0).T


def _outproj_body(c_ref, o_ref, x_ref, wo_ref, g2_ref, wr_ref, br_ref, x1_ref, t_ref, rec_ref):
    x1 = x_ref[...] + jnp.dot(c_ref[...], wo_ref[0:CONV_CH, :], preferred_element_type=F32)
    x1 = x1 + jnp.dot(o_ref[...], wo_ref[CONV_CH:, :], preferred_element_type=F32)
    x1_ref[...] = x1
    t = _rms(x1, g2_ref[...])
    t_ref[...] = t.astype(t_ref.dtype)
    rec_ref[...] = _route(_dot_hp(t, wr_ref) + br_ref[...])


def _outproj(c, o, x, w_out, g2, wr2, br, tm):
    t = x.shape[0]
    row = lambda i: (i, 0)
    const = lambda i: (0, 0)
    return pl.pallas_call(
        _outproj_body,
        grid=(t // tm,),
        in_specs=[
            pl.BlockSpec((tm, CONV_CH), row),
            pl.BlockSpec((tm, RET_WIDTH), row),
            pl.BlockSpec((tm, D_MODEL), row),
            pl.BlockSpec((D_MODEL, D_MODEL), const),
            pl.BlockSpec((1, D_MODEL), const),
            pl.BlockSpec((D_MODEL, 2 * LANES), const),
            pl.BlockSpec((1, LANES), const),
        ],
        out_specs=[
            pl.BlockSpec((tm, D_MODEL), row),
            pl.BlockSpec((tm, D_MODEL), row),
            pl.BlockSpec((tm, LANES), row),
        ],
        out_shape=[
            jax.ShapeDtypeStruct((t, D_MODEL), F32),
            jax.ShapeDtypeStruct((t, D_MODEL), BF16),
            jax.ShapeDtypeStruct((t, LANES), F32),
        ],
        compiler_params=_cparams(("parallel",)),
        name="outproj_router",
    )(c, o, x, w_out, g2, wr2, br)


def _mix_body(x_ref, g1_ref, w_ref, cos_ref, sin_ref, cw_ref, cb_ref, lg_ref, lb_ref,
              d_ref, xi_ref, zeta_ref, gc_ref, wo_ref, g2_ref, wr_ref, br_ref,
              x1_ref, t_ref, rec_ref, cst_ref, rst_ref, ext_ref, r_ref, o_ref):
    j = pl.program_id(1)
    tl = x_ref.shape[0]
    x = x_ref[...]
    z = jnp.dot(_rms(x, g1_ref[...]).astype(BF16), w_ref[...], preferred_element_type=F32)

    @pl.when(j == 0)
    def _():
        ext_ref[0:HALO, :] = jnp.zeros((HALO, CONV_CH), F32)
        ext_ref[tl + HALO:, :] = jnp.zeros((SUBLANES, CONV_CH), F32)
        r_ref[...] = jnp.zeros_like(r_ref)

    @pl.when(j > 0)
    def _():
        ext_ref[0:HALO, :] = ext_ref[tl:tl + HALO, :]

    ext_ref[HALO:tl + HALO, :] = z[:, :CONV_CH] * jax.nn.sigmoid(z[:, CONV_CH:2 * CONV_CH])
    acc = _dwconv(lambda s, n: ext_ref[s:s + n, :], cw_ref, tl) + cb_ref[...]
    c = _ln_silu(acc, lg_ref[...], lb_ref[...]).astype(BF16)
    cst_ref[0] = ext_ref[tl + HALO_OFF:tl + HALO, :]

    cos = cos_ref[...]
    sin = sin_ref[...]
    q0 = 2 * CONV_CH
    k0 = q0 + RET_WIDTH
    v0 = k0 + RET_WIDTH
    g0 = v0 + RET_WIDTH
    for hh in range(RET_HEADS):
        lo = hh * HEAD_DIM
        qh = z[:, q0 + lo:q0 + lo + HEAD_DIM]
        kh = z[:, k0 + lo:k0 + lo + HEAD_DIM]
        qr = (qh * cos + pltpu.roll(qh, HEAD_DIM // 2, 1) * sin).astype(BF16)
        kr = ((kh * cos + pltpu.roll(kh, HEAD_DIM // 2, 1) * sin) * (HEAD_DIM ** -0.5)).astype(BF16)
        vh = z[:, v0 + lo:v0 + lo + HEAD_DIM].astype(BF16)
        g = z[:, g0 + lo:g0 + lo + HEAD_DIM]
        gs = g * jax.nn.sigmoid(g)
        r = r_ref[hh]
        for ci in range(tl // CHUNK):
            rows = slice(ci * CHUNK, (ci + 1) * CHUNK)
            o, r = _ret_chunk(qr[rows], kr[rows], vh[rows], r, d_ref[hh], xi_ref[hh], zeta_ref[hh], gc_ref[hh])
            o_ref[rows, lo:lo + HEAD_DIM] = (gs[rows] * _group_norm(o)).astype(BF16)
        r_ref[hh] = r
    rst_ref[0] = r_ref[...]

    x1 = x + jnp.dot(c, wo_ref[0:CONV_CH, :], preferred_element_type=F32)
    x1 = x1 + jnp.dot(o_ref[...], wo_ref[CONV_CH:, :], preferred_element_type=F32)
    x1_ref[...] = x1
    t = _rms(x1, g2_ref[...])
    t_ref[...] = t.astype(t_ref.dtype)
    rec_ref[...] = _route(_dot_hp(t, wr_ref) + br_ref[...])


def _mix(x, g1, w_in, cos, sin, conv_w, conv_b, ln_g, ln_b, w_out, g2, wr2, br, n, l, tl):
    dmat, xi, zeta, gc = _decay_tables(CHUNK)
    per = l // tl
    row = lambda b, j: (b * per + j, 0)
    tab = lambda b, j: (j, 0)
    const = lambda b, j: (0, 0)
    c3 = lambda b, j: (0, 0, 0)
    once = dict(pipeline_mode=pl.Buffered(1))
    return pl.pallas_call(
        _mix_body,
        grid=(n, per),
        in_specs=[
            pl.BlockSpec((tl, D_MODEL), row),
            pl.BlockSpec((1, D_MODEL), const),
            pl.BlockSpec((D_MODEL, IN_COLS), const, **once),
            pl.BlockSpec((tl, HEAD_DIM), tab),
            pl.BlockSpec((tl, HEAD_DIM), tab),
            pl.BlockSpec((CONV_K, CONV_CH), const),
            pl.BlockSpec((1, CONV_CH), const),
            pl.BlockSpec((1, CONV_CH), const),
            pl.BlockSpec((1, CONV_CH), const),
            pl.BlockSpec((RET_HEADS, CHUNK, CHUNK), c3),
            pl.BlockSpec((RET_HEADS, CHUNK, HEAD_DIM), c3),
            pl.BlockSpec((RET_HEADS, CHUNK, HEAD_DIM), c3),
            pl.BlockSpec((RET_HEADS, 1, HEAD_DIM), c3),
            pl.BlockSpec((D_MODEL, D_MODEL), const, **once),
            pl.BlockSpec((1, D_MODEL), const),
            pl.BlockSpec((D_MODEL, 2 * LANES), const, **once),
            pl.BlockSpec((1, LANES), const),
        ],
        out_specs=[
            pl.BlockSpec((tl, D_MODEL), row),
            pl.BlockSpec((tl, D_MODEL), row),
            pl.BlockSpec((tl, LANES), row),
            pl.BlockSpec((1, CONV_K - 1, CONV_CH), lambda b, j: (b, 0, 0)),
            pl.BlockSpec((1, RET_HEADS, HEAD_DIM, HEAD_DIM), lambda b, j: (b, 0, 0, 0)),
        ],
        out_shape=[
            jax.ShapeDtypeStruct((n * l, D_MODEL), F32),
            jax.ShapeDtypeStruct((n * l, D_MODEL), BF16),
            jax.ShapeDtypeStruct((n * l, LANES), F32),
            jax.ShapeDtypeStruct((n, CONV_K - 1, CONV_CH), F32),
            jax.ShapeDtypeStruct((n, RET_HEADS, HEAD_DIM, HEAD_DIM), F32),
        ],
        scratch_shapes=[
            pltpu.VMEM((tl + HALO + SUBLANES, CONV_CH), F32),
            pltpu.VMEM((RET_HEADS, HEAD_DIM, HEAD_DIM), F32),
            pltpu.VMEM((tl, RET_WIDTH), BF16),
        ],
        compiler_params=pltpu.CompilerParams(dimension_semantics=("arbitrary", "arbitrary"),
                                             vmem_limit_bytes=MIX_VMEM_LIMIT),
        name="token_mix",
    )(x, g1, w_in, cos, sin, conv_w, conv_b, ln_g, ln_b, dmat, xi, zeta, gc, w_out, g2, wr2, br)


def _dispatch_body(rec_a_ref, t_a_ref, rec_b_ref, t_b_ref, s_ref, pos_ref, meta_ref, *, nsub_a):
    from_a = pl.program_id(0) * DISPATCH_SUBS < nsub_a
    for s in range(DISPATCH_SUBS):
        rows = slice(s * SUB, (s + 1) * SUB)
        rec = jnp.where(from_a, rec_a_ref[rows, :], rec_b_ref[rows, :])
        tok = jnp.where(from_a, t_a_ref[rows, :], t_b_ref[rows, :])
        chunks = slice(s * CHUNKS_PER_SUB, (s + 1) * CHUNKS_PER_SUB)
        _dispatch_sub_tile(rec, tok, s_ref.at[chunks], pos_ref.at[rows], meta_ref.at[s])


def _dispatch_sub_tile(rec, tok, s_ref, pos_ref, meta_ref):
    lane = _iota_f32(rec.shape, 1)
    a1 = lane == rec[:, 0:1]
    a2 = lane == rec[:, 1:2]
    a1f = jnp.where(a1, 1.0, 0.0)
    a2f = jnp.where(a2, 1.0, 0.0)
    ltri = jnp.where(_iota_f32((SUB, SUB), 1) < _iota_f32((SUB, SUB), 0), 1.0, 0.0).astype(BF16)
    c1 = jnp.dot(ltri, a1f.astype(BF16), preferred_element_type=F32)
    c2 = jnp.dot(ltri, a2f.astype(BF16), preferred_element_type=F32)
    n1 = jnp.sum(a1f, axis=0, keepdims=True)
    n2 = jnp.sum(a2f, axis=0, keepdims=True)
    cnt = jnp.floor((n1 + n2 + (ROW_ALIGN - 1.0)) * (1.0 / ROW_ALIGN))
    utri = jnp.where(_iota_f32((LANES, LANES), 0) < _iota_f32((LANES, LANES), 1), 1.0, 0.0).astype(BF16)
    start = jnp.dot(jnp.broadcast_to(cnt, (SUBLANES, LANES)).astype(BF16), utri,
                    preferred_element_type=F32)[0:1]
    base1 = start * ROW_ALIGN
    base2 = base1 + n1
    pos1 = jnp.sum(jnp.where(a1, c1 + base1, 0.0), axis=1, keepdims=True)
    pos2 = jnp.sum(jnp.where(a2, c2 + base2, 0.0), axis=1, keepdims=True)
    posm = jnp.where(lane == 0.0, pos1, jnp.where(lane == 1.0, pos2, 0.0))
    pos_ref[...] = posm
    row = _iota_f32((SUBLANES, LANES), 0)
    meta_ref[...] = jnp.where(row == 0.0, start, jnp.where(row == 1.0, cnt, 0.0))

    g1 = _split3(rec[:, 2:3])
    g2 = _split3(rec[:, 3:4])
    info = jnp.where(lane == 6.0, rec[:, 0:1], jnp.where(lane == 7.0, rec[:, 1:2], 0.0))
    for i in range(3):
        info = jnp.where(lane == float(i), g1[i].astype(F32), info)
        info = jnp.where(lane == float(3 + i), g2[i].astype(F32), info)
    src = jnp.concatenate([tok, info.astype(BF16)], axis=1)

    post = posm.T
    r = _iota_f32((SUBP, SUB), 0)
    onehot = jnp.where(r == post[0:1, :], 1.0, jnp.where(r == post[1:2, :], 1.0, 0.0)).astype(BF16)
    sorted_rows = jnp.dot(onehot, src, preferred_element_type=F32).astype(BF16)
    s_ref[...] = sorted_rows.reshape(CHUNKS_PER_SUB, ROW_ALIGN, ROW_W)


def _dispatch(rec_a, t_a, rec_b, t_b):
    nsub_a = rec_a.shape[0] // SUB
    nsub_b = rec_b.shape[0] // SUB
    nsub = nsub_a + nsub_b
    assert nsub_a % DISPATCH_SUBS == 0 and nsub_b % DISPATCH_SUBS == 0
    steps_a = nsub_a // DISPATCH_SUBS
    tm = DISPATCH_SUBS * SUB
    row = lambda i: (i, 0)
    row_a = lambda i: (jnp.minimum(i, steps_a - 1), 0)
    row_b = lambda i: (jnp.maximum(i - steps_a, 0), 0)
    return pl.pallas_call(
        functools.partial(_dispatch_body, nsub_a=nsub_a),
        grid=(nsub // DISPATCH_SUBS,),
        in_specs=[
            pl.BlockSpec((tm, LANES), row_a),
            pl.BlockSpec((tm, D_MODEL), row_a),
            pl.BlockSpec((tm, LANES), row_b),
            pl.BlockSpec((tm, D_MODEL), row_b),
        ],
        out_specs=[
            pl.BlockSpec((DISPATCH_SUBS * CHUNKS_PER_SUB, ROW_ALIGN, ROW_W), lambda i: (i, 0, 0)),
            pl.BlockSpec((tm, LANES), row),
            pl.BlockSpec((DISPATCH_SUBS, SUBLANES, LANES), lambda i: (i, 0, 0)),
        ],
        out_shape=[
            jax.ShapeDtypeStruct((nsub * CHUNKS_PER_SUB, ROW_ALIGN, ROW_W), BF16),
            jax.ShapeDtypeStruct((nsub * SUB, LANES), F32),
            jax.ShapeDtypeStruct((nsub, SUBLANES, LANES), F32),
        ],
        compiler_params=_cparams(("parallel",)),
        name="moe_dispatch",
    )(rec_a, t_a, rec_b, t_b)


def _experts_body(start_ref, cnt_ref, s_in, wg_ref, wu_ref, wd_ref, s_hbm,
                  xbuf, ybuf, wgu_ref, wdb_ref, gsem, ssem, list_ref, state_ref, *, nsub):
    del s_in
    e = pl.program_id(0)
    ne = pl.num_programs(0)
    par = e & 1

    list_max = list_ref.shape[0] // 2

    def gather_copy(src, i, slot):
        return pltpu.make_async_copy(s_hbm.at[src], xbuf.at[slot, i], gsem.at[slot])

    def scatter_copy(dst, i, slot):
        return pltpu.make_async_copy(ybuf.at[slot, i], s_hbm.at[dst], ssem.at[slot])

    def build_list(x, which):
        def per_sub(s, k):
            run = s * N_EXPERTS + x
            c = cnt_ref[run]
            base = s * CHUNKS_PER_SUB + start_ref[run]
            list_ref[k] = base
            list_ref[k + 1] = base + 1

            def per_chunk(i, carry):
                list_ref[k + i] = base + i
                return carry
            lax.fori_loop(2, c, per_chunk, 0)
            return k + c
        first = which * list_max
        state_ref[which] = lax.fori_loop(0, nsub, per_sub, first) - first

    def start_all(copy, which, first, n, slot, counter):
        def body(i, carry):
            copy(list_ref[which * list_max + first + i], i, slot).start()
            return carry
        lax.fori_loop(0, n, body, 0)
        state_ref[counter] = n

    def wait_all(copy, block_copy, slot, counter):
        n = state_ref[counter]

        @pl.when(n == CHUNKS_PER_BLK)
        def _():
            block_copy(slot).wait()

        @pl.when(n < CHUNKS_PER_BLK)
        def _():
            def body(i, carry):
                copy(0, 0, slot).wait()
                return carry
            lax.fori_loop(0, n, body, 0)
        state_ref[counter] = 0

    def gather_block(slot):
        return pltpu.make_async_copy(s_hbm.at[pl.ds(0, CHUNKS_PER_BLK)], xbuf.at[slot], gsem.at[slot])

    def scatter_block(slot):
        return pltpu.make_async_copy(ybuf.at[slot], s_hbm.at[pl.ds(0, CHUNKS_PER_BLK)], ssem.at[slot])

    def block_chunks(total, b):
        return jnp.minimum(total - b * CHUNKS_PER_BLK, CHUNKS_PER_BLK)

    @pl.when(e == 0)
    def _():
        for i in range(6):
            state_ref[i] = 0
        xbuf[...] = jnp.zeros_like(xbuf)
        build_list(0, 0)
        n0 = state_ref[0]

        @pl.when(n0 > 0)
        def _():
            start_all(gather_copy, 0, 0, block_chunks(n0, 0), 0, 2)

    @pl.when(e + 1 < ne)
    def _():
        build_list(e + 1, 1 - par)

    total = state_ref[par]
    nblk = (total + CHUNKS_PER_BLK - 1) // CHUNKS_PER_BLK
    wgu_ref[:, 0:EXPERT_FF] = wg_ref[0].astype(BF16)
    wgu_ref[:, EXPERT_FF:] = wu_ref[0].astype(BF16)
    wdb_ref[...] = wd_ref[0].astype(BF16)
    ef = e.astype(F32)

    def block(b, carry):
        slot = b & 1
        first = b * CHUNKS_PER_BLK

        @pl.when(b + 1 < nblk)
        def _():
            start_all(gather_copy, par, first + CHUNKS_PER_BLK, block_chunks(total, b + 1), 1 - slot, 3 - slot)

        wait_all(gather_copy, gather_block, slot, 2 + slot)
        wait_all(scatter_copy, scatter_block, slot, 4 + slot)
        x = xbuf[slot].reshape(MBLK, ROW_W)
        info = x[:, D_MODEL:].astype(F32)
        g_first = info[:, 0:1] + info[:, 1:2] + info[:, 2:3]
        g_second = info[:, 3:4] + info[:, 4:5] + info[:, 5:6]
        gate = jnp.where(info[:, 6:7] == ef, g_first, g_second)
        h = jnp.dot(x[:, :D_MODEL], wgu_ref[...], preferred_element_type=F32)
        h1 = h[:, :EXPERT_FF]
        hid = (h1 * jax.nn.sigmoid(h1)) * h[:, EXPERT_FF:] * gate
        y = jnp.dot(hid.astype(BF16), wdb_ref[...], preferred_element_type=F32).astype(BF16)
        ybuf[slot] = jnp.concatenate([y, x[:, D_MODEL:]], axis=1).reshape(CHUNKS_PER_BLK, ROW_ALIGN, ROW_W)
        start_all(scatter_copy, par, first, block_chunks(total, b), slot, 4 + slot)
        return carry

    lax.fori_loop(0, nblk, block, 0)

    @pl.when(e + 1 < ne)
    def _():
        n1 = state_ref[1 - par]

        @pl.when(n1 > 0)
        def _():
            start_all(gather_copy, 1 - par, 0, block_chunks(n1, 0), 0, 2)

    @pl.when(e == ne - 1)
    def _():
        wait_all(scatter_copy, scatter_block, 0, 4)
        wait_all(scatter_copy, scatter_block, 1, 5)


def _experts(start, cnt, staged, wg, wu, wd):
    nsub = staged.shape[0] // CHUNKS_PER_SUB
    list_max = nsub * SUB // ROW_ALIGN + nsub + LIST_SLACK
    wblk = lambda e, *_: (e, 0, 0)
    grid_spec = pltpu.PrefetchScalarGridSpec(
        num_scalar_prefetch=2,
        grid=(N_EXPERTS,),
        in_specs=[
            pl.BlockSpec(memory_space=pl.ANY),
            pl.BlockSpec((1, D_MODEL, EXPERT_FF), wblk),
            pl.BlockSpec((1, D_MODEL, EXPERT_FF), wblk),
            pl.BlockSpec((1, EXPERT_FF, D_MODEL), wblk),
        ],
        out_specs=pl.BlockSpec(memory_space=pl.ANY),
        scratch_shapes=[
            pltpu.VMEM((2, CHUNKS_PER_BLK, ROW_ALIGN, ROW_W), BF16),
            pltpu.VMEM((2, CHUNKS_PER_BLK, ROW_ALIGN, ROW_W), BF16),
            pltpu.VMEM((D_MODEL, 2 * EXPERT_FF), BF16),
            pltpu.VMEM((EXPERT_FF, D_MODEL), BF16),
            pltpu.SemaphoreType.DMA((2,)),
            pltpu.SemaphoreType.DMA((2,)),
            pltpu.SMEM((2 * list_max,), jnp.int32),
            pltpu.SMEM((6,), jnp.int32),
        ],
    )
    return pl.pallas_call(
        functools.partial(_experts_body, nsub=nsub),
        grid_spec=grid_spec,
        out_shape=jax.ShapeDtypeStruct(staged.shape, staged.dtype),
        input_output_aliases={2: 0},
        compiler_params=_cparams(("arbitrary",)),
        name="moe_experts",
    )(start, cnt, staged, wg, wu, wd)


def _combine_body(ys_ref, pos_ref, x1_ref, p_ref, wp_ref, gp_ref, wpg_ref, gf_ref, y_ref):
    r = _iota_f32((SUB, SUBP), 1)
    moe = []
    for s in range(x1_ref.shape[0] // SUB):
        p1 = pos_ref[s * SUB:(s + 1) * SUB, 0:1]
        p2 = pos_ref[s * SUB:(s + 1) * SUB, 1:2]
        onehot = jnp.where(r == p1, 1.0, jnp.where(r == p2, 1.0, 0.0)).astype(BF16)
        ys = ys_ref[s * CHUNKS_PER_SUB:(s + 1) * CHUNKS_PER_SUB].reshape(SUBP, D_MODEL)
        moe.append(jnp.dot(onehot, ys, preferred_element_type=F32))
    x2 = x1_ref[...] + jnp.concatenate(moe, axis=0)
    ple = _rms(jnp.dot(p_ref[...].astype(BF16), wp_ref[...], preferred_element_type=F32), gp_ref[...])
    gate = jax.nn.sigmoid(jnp.dot(x2.astype(BF16), wpg_ref[...], preferred_element_type=F32))
    y_ref[...] = _rms(x2 + ple * gate, gf_ref[...])


def _combine(ys, pos, x1, p, w_ple, gp, w_ple_gate, gf, sub_off):
    t = x1.shape[0]
    tm = COMBINE_SUBS * SUB
    blk_off = sub_off // COMBINE_SUBS
    assert sub_off % COMBINE_SUBS == 0 and t % tm == 0
    row = lambda i: (i, 0)
    const = lambda i: (0, 0)
    return pl.pallas_call(
        _combine_body,
        grid=(t // tm,),
        in_specs=[
            pl.BlockSpec((COMBINE_SUBS * CHUNKS_PER_SUB, ROW_ALIGN, D_MODEL), lambda i: (i + blk_off, 0, 0)),
            pl.BlockSpec((tm, LANES), lambda i: (i + blk_off, 0)),
            pl.BlockSpec((tm, D_MODEL), row),
            pl.BlockSpec((tm, PLE_DIM), row),
            pl.BlockSpec((PLE_DIM, D_MODEL), const),
            pl.BlockSpec((1, D_MODEL), const),
            pl.BlockSpec((D_MODEL, D_MODEL), const),
            pl.BlockSpec((1, D_MODEL), const),
        ],
        out_specs=pl.BlockSpec((tm, D_MODEL), row),
        out_shape=jax.ShapeDtypeStruct((t, D_MODEL), F32),
        compiler_params=_cparams(("parallel",)),
        name="moe_combine_ple",
    )(ys, pos, x1, p, w_ple, gp, w_ple_gate, gf)


def _rope_tables(pos):
    half = HEAD_DIM // 2
    inv = ROPE_BASE ** (-jnp.arange(half, dtype=F32) / half)
    ang = pos[:, None] * inv[None, :]
    cos = jnp.cos(ang)
    sin = jnp.sin(ang)
    return jnp.concatenate([cos, cos], axis=-1), jnp.concatenate([-sin, sin], axis=-1)


def _router_params(we, be, wg, bg):
    pad = LANES - N_EXPERTS - N_GROUPS
    w = jnp.pad(jnp.concatenate([we, wg], axis=1), ((0, 0), (0, pad)))
    b = jnp.pad(jnp.concatenate([be, bg]), (0, pad))[None, :]
    return jnp.concatenate(_split3(w)[:2], axis=1), b


def kernel(x_prompt, x_sample, p_prompt, p_sample, state_conv, state_ret, w_in, conv_w, conv_b, conv_ln_g, conv_ln_b, w_out, norm1_g, norm2_g, router_group_w, router_group_b, router_expert_w, router_expert_b, w_expert_gate, w_expert_up, w_expert_down, w_ple, ple_norm_g, w_ple_gate, final_norm_g):
    assert w_in.shape[0] == 1, "single-layer trunk"
    nb, seq, _ = x_prompt.shape
    ns, dseq, _ = x_sample.shape
    tm = SAMPLE_TILE

    w_in_b = w_in[0].astype(BF16)
    w_out_b = w_out[0].astype(BF16)
    w_ple_b = w_ple[0].astype(BF16)
    w_pg_b = w_ple_gate[0].astype(BF16)
    g1 = norm1_g[0][None, :]
    g2 = norm2_g[0][None, :]
    gp = ple_norm_g[0][None, :]
    gf = final_norm_g[None, :]
    cb = conv_b[0][None, :]
    lng = conv_ln_g[0][None, :]
    lnb = conv_ln_b[0][None, :]
    wr2, br = _router_params(router_expert_w[0], router_expert_b[0], router_group_w[0], router_group_b[0])

    cos_p, sin_p = _rope_tables(jnp.arange(seq, dtype=F32) + jnp.float32(0))
    pos_s = jnp.tile(jnp.arange(dseq, dtype=F32) + jnp.float32(PAST_LEN), tm // dseq)
    cos_s, sin_s = _rope_tables(pos_s)

    xp = x_prompt.reshape(nb * seq, D_MODEL)
    x1_p, t_p, rec_p, conv_p, ret_p = _mix(xp, g1, w_in_b, cos_p, sin_p, conv_w[0], cb, lng, lnb, w_out_b, g2,
                                           wr2, br, nb, seq, MIX_TILE)

    xs = x_sample.reshape(ns * dseq, D_MODEL)
    u, q, k, v, gs = _inproj(xs, g1, w_in_b, cos_s, sin_s, tm)
    c, conv_s = _conv_sample(u.reshape(ns, dseq, CONV_CH), jnp.transpose(state_conv[0], (1, 0, 2)), conv_w[0], cb,
                             lng, lnb, CONV_SAMPLE_SEQS)
    conv_s = jnp.transpose(conv_s, (1, 0, 2))
    o, ret_s = _ret_sample(q, k, v, gs, state_ret[0], dseq, RET_SAMPLE_SEQS)
    x1_s, t_s, rec_s = _outproj(c.reshape(ns * dseq, CONV_CH), o, xs, w_out_b, g2, wr2, br, tm)

    staged, pos, meta = _dispatch(rec_p, t_p, rec_s, t_s)
    start = meta[:, 0, :N_EXPERTS].astype(jnp.int32).reshape(-1)
    cnt = meta[:, 1, :N_EXPERTS].astype(jnp.int32).reshape(-1)
    ys = _experts(start, cnt, staged, w_expert_gate[0], w_expert_up[0], w_expert_down[0])

    y_p = _combine(ys, pos, x1_p, p_prompt[0].reshape(nb * seq, PLE_DIM), w_ple_b, gp, w_pg_b, gf, 0)
    y_s = _combine(ys, pos, x1_s, p_sample[0].reshape(ns * dseq, PLE_DIM), w_ple_b, gp, w_pg_b, gf,
                   nb * seq // SUB)

    return (y_p.reshape(nb, seq, D_MODEL), y_s.reshape(ns, dseq, D_MODEL),
            conv_p[None], ret_p[None], conv_s[None], ret_s[None])
```

```python
import functools

import jax
import jax.numpy as jnp
from jax import lax
from jax.experimental import pallas as pl
from jax.experimental.pallas import tpu as pltpu

F32 = jnp.float32
BF16 = jnp.bfloat16

D_MODEL = 1024
PLE_DIM = 256
CONV_CH = 512
CONV_K = 31
RET_WIDTH = 512
RET_HEADS = 4
HEAD_DIM = 128
CHUNK = 128
ROPE_BASE = 10000.0
N_GROUPS = 4
EXPERTS_PER_GROUP = 8
N_EXPERTS = 32
EXPERT_FF = 256
IN_COLS = 3072
EPS = 1e-6
PAST_LEN = 16384

LANES = 128
SUBLANES = 8
HALO = 32
HALO_OFF = HALO - (CONV_K - 1)
VMEM_LIMIT = 48 * 1024 * 1024
MIX_VMEM_LIMIT = 56 * 1024 * 1024
MIX_TILE = 512
SAMPLE_TILE = 1024
CONV_SAMPLE_SEQS = 32
RET_SAMPLE_SEQS = 16

SUB = 256
ROW_ALIGN = 16
PBLK = 256
SUBP = -(-(2 * SUB + N_EXPERTS * (ROW_ALIGN - 1)) // PBLK) * PBLK
CHUNKS_PER_SUB = SUBP // ROW_ALIGN
SUBP_HEAD = SUBP - PBLK
ROW_W = D_MODEL + LANES
MBLK = 512
CHUNKS_PER_BLK = MBLK // ROW_ALIGN
LIST_SLACK = 2
COMBINE_SUBS = 4
DISPATCH_SUBS = 4


def _cparams(sem):
    return pltpu.CompilerParams(dimension_semantics=sem, vmem_limit_bytes=VMEM_LIMIT)


def _rms(x, g):
    return x * lax.rsqrt(jnp.mean(x * x, axis=-1, keepdims=True) + EPS) * g


def _inproj_body(x_ref, g1_ref, w_ref, cos_ref, sin_ref, u_ref, q_ref, k_ref, v_ref, gs_ref):
    h = _rms(x_ref[...], g1_ref[...]).astype(BF16)
    z = jnp.dot(h, w_ref[...], preferred_element_type=F32)
    a = z[:, :CONV_CH]
    b = z[:, CONV_CH:2 * CONV_CH]
    u_ref[...] = a * jax.nn.sigmoid(b)
    cos = cos_ref[...]
    sin = sin_ref[...]
    q0 = 2 * CONV_CH
    k0 = q0 + RET_WIDTH
    for hh in range(RET_HEADS):
        sl = slice(hh * HEAD_DIM, (hh + 1) * HEAD_DIM)
        qh = z[:, q0 + hh * HEAD_DIM:q0 + (hh + 1) * HEAD_DIM]
        kh = z[:, k0 + hh * HEAD_DIM:k0 + (hh + 1) * HEAD_DIM]
        q_ref[:, sl] = qh * cos + pltpu.roll(qh, HEAD_DIM // 2, 1) * sin
        k_ref[:, sl] = (kh * cos + pltpu.roll(kh, HEAD_DIM // 2, 1) * sin) * (HEAD_DIM ** -0.5)
    v_ref[...] = z[:, k0 + RET_WIDTH:k0 + 2 * RET_WIDTH]
    g = z[:, k0 + 2 * RET_WIDTH:]
    gs_ref[...] = g * jax.nn.sigmoid(g)


def _inproj(x, g1, w_in, cos, sin, tm):
    t = x.shape[0]
    row = lambda i: (i, 0)
    const = lambda i: (0, 0)
    return pl.pallas_call(
        _inproj_body,
        grid=(t // tm,),
        in_specs=[
            pl.BlockSpec((tm, D_MODEL), row),
            pl.BlockSpec((1, D_MODEL), const),
            pl.BlockSpec((D_MODEL, IN_COLS), const),
            pl.BlockSpec((tm, HEAD_DIM), const),
            pl.BlockSpec((tm, HEAD_DIM), const),
        ],
        out_specs=[pl.BlockSpec((tm, CONV_CH), row)] + [pl.BlockSpec((tm, RET_WIDTH), row)] * 4,
        out_shape=[jax.ShapeDtypeStruct((t, CONV_CH), F32)] + [jax.ShapeDtypeStruct((t, RET_WIDTH), F32)] * 4,
        compiler_params=_cparams(("parallel",)),
        name="inproj",
    )(x, g1, w_in, cos, sin)


def _ln_silu(acc, g, b):
    mu = jnp.mean(acc, axis=-1, keepdims=True)
    d = acc - mu
    var = jnp.mean(d * d, axis=-1, keepdims=True)
    y = d * lax.rsqrt(var + EPS) * g + b
    return y * jax.nn.sigmoid(y)


def _dwconv(load, w_ref, rows):
    acc = None
    for b in range(SUBLANES):
        part = None
        for a in range((CONV_K + HALO_OFF) // SUBLANES + 1):
            k = SUBLANES * a + b - HALO_OFF
            if 0 <= k < CONV_K:
                term = load(SUBLANES * a, rows + SUBLANES) * w_ref[k:k + 1, :]
                part = term if part is None else part + term
        if part is not None:
            shifted = part[b:b + rows]
            acc = shifted if acc is None else acc + shifted
    return acc


def _conv_sample_body(u_ref, st_ref, w_ref, cb_ref, lg_ref, lb_ref, c_ref, nst_ref, ext_ref):
    l = u_ref.shape[1]
    hist = CONV_K - 1
    ext_ref[0:hist] = st_ref[...]
    ext_ref[hist:] = jnp.transpose(u_ref[...], (1, 0, 2))
    acc = ext_ref[0:l] * w_ref[0:1, :] + cb_ref[...]
    for k in range(1, CONV_K):
        acc = acc + ext_ref[k:k + l] * w_ref[k:k + 1, :]
    y = _ln_silu(acc, lg_ref[...], lb_ref[...])
    c_ref[...] = jnp.transpose(y, (1, 0, 2)).astype(c_ref.dtype)
    nst_ref[...] = ext_ref[l:]


def _conv_sample(u, state_t, conv_w, conv_b, ln_g, ln_b, nb):
    n, l, _ = u.shape
    hist = CONV_K - 1
    const = lambda b: (0, 0)
    tok = lambda b: (b, 0, 0)
    tmaj = lambda b: (0, b, 0)
    return pl.pallas_call(
        _conv_sample_body,
        grid=(n // nb,),
        in_specs=[
            pl.BlockSpec((nb, l, CONV_CH), tok),
            pl.BlockSpec((hist, nb, CONV_CH), tmaj),
            pl.BlockSpec((CONV_K, CONV_CH), const),
            pl.BlockSpec((1, CONV_CH), const),
            pl.BlockSpec((1, CONV_CH), const),
            pl.BlockSpec((1, CONV_CH), const),
        ],
        out_specs=[
            pl.BlockSpec((nb, l, CONV_CH), tok),
            pl.BlockSpec((hist, nb, CONV_CH), tmaj),
        ],
        out_shape=[
            jax.ShapeDtypeStruct((n, l, CONV_CH), BF16),
            jax.ShapeDtypeStruct((hist, n, CONV_CH), F32),
        ],
        scratch_shapes=[pltpu.VMEM((hist + l, nb, CONV_CH), F32)],
        compiler_params=_cparams(("parallel",)),
        name="conv_sample",
    )(u, state_t, conv_w, conv_b, ln_g, ln_b)


def _decay_tables(c):
    lg = jnp.log(1.0 - 2.0 ** (-5.0 - jnp.arange(RET_HEADS, dtype=F32)))
    idx = jnp.arange(c, dtype=F32)
    rel = idx[:, None] - idx[None, :]
    dmat = jnp.where(rel[None] >= 0, jnp.exp(jnp.maximum(rel, 0.0)[None] * lg[:, None, None]), 0.0)
    xi = jnp.exp((idx + 1.0)[None, :] * lg[:, None])
    zeta = jnp.exp((c - 1.0 - idx)[None, :] * lg[:, None])
    gc = jnp.exp(c * lg)
    xi_b = jnp.broadcast_to(xi[:, :, None], (RET_HEADS, c, HEAD_DIM))
    zeta_b = jnp.broadcast_to(zeta[:, :, None], (RET_HEADS, c, HEAD_DIM))
    gc_b = jnp.broadcast_to(gc[:, None, None], (RET_HEADS, 1, HEAD_DIM))
    return dmat, xi_b, zeta_b, gc_b


def _group_norm(o):
    mu = jnp.mean(o, axis=-1, keepdims=True)
    d = o - mu
    var = jnp.mean(d * d, axis=-1, keepdims=True)
    return d * lax.rsqrt(var + EPS)


def _ret_chunk(qh, kh, vh, r, dmat, xi, zeta, gc):
    qb = qh.astype(BF16)
    kb = kh.astype(BF16)
    vb = vh.astype(BF16)
    s = lax.dot_general(qb, kb, (((1,), (1,)), ((), ())), preferred_element_type=F32) * dmat
    o = jnp.dot(s.astype(BF16), vb, preferred_element_type=F32)
    o = o + jnp.dot(qb, r.astype(BF16), preferred_element_type=F32) * xi
    kz = (kh.astype(F32) * zeta).astype(BF16)
    r_new = r * gc + lax.dot_general(kz, vb, (((0,), (0,)), ((), ())), preferred_element_type=F32)
    return o, r_new


def _ret_sample_body(q_ref, k_ref, v_ref, gs_ref, st_ref, d_ref, xi_ref, zeta_ref, gc_ref, o_ref, nst_ref):
    nb = st_ref.shape[0]
    l = q_ref.shape[0] // nb
    for b in range(nb):
        rows = slice(b * l, (b + 1) * l)
        for hh in range(RET_HEADS):
            sl = slice(hh * HEAD_DIM, (hh + 1) * HEAD_DIM)
            o, r = _ret_chunk(q_ref[rows, sl], k_ref[rows, sl], v_ref[rows, sl], st_ref[b, hh],
                              d_ref[hh], xi_ref[hh], zeta_ref[hh], gc_ref[hh])
            o_ref[rows, sl] = (gs_ref[rows, sl] * _group_norm(o)).astype(o_ref.dtype)
            nst_ref[b, hh] = r


def _ret_sample(q, k, v, gs, state, l, nb):
    n = state.shape[0]
    dmat, xi, zeta, gc = _decay_tables(l)
    row = lambda b: (b, 0)
    c3 = lambda b: (0, 0, 0)
    blk4 = lambda b: (b, 0, 0, 0)
    return pl.pallas_call(
        _ret_sample_body,
        grid=(n // nb,),
        in_specs=[pl.BlockSpec((nb * l, RET_WIDTH), row)] * 4 + [
            pl.BlockSpec((nb, RET_HEADS, HEAD_DIM, HEAD_DIM), blk4),
            pl.BlockSpec((RET_HEADS, l, l), c3),
            pl.BlockSpec((RET_HEADS, l, HEAD_DIM), c3),
            pl.BlockSpec((RET_HEADS, l, HEAD_DIM), c3),
            pl.BlockSpec((RET_HEADS, 1, HEAD_DIM), c3),
        ],
        out_specs=[
            pl.BlockSpec((nb * l, RET_WIDTH), row),
            pl.BlockSpec((nb, RET_HEADS, HEAD_DIM, HEAD_DIM), blk4),
        ],
        out_shape=[
            jax.ShapeDtypeStruct((n * l, RET_WIDTH), BF16),
            jax.ShapeDtypeStruct((n, RET_HEADS, HEAD_DIM, HEAD_DIM), F32),
        ],
        compiler_params=_cparams(("parallel",)),
        name="ret_sample",
    )(q, k, v, gs, state, dmat, xi, zeta, gc)


def _split3(x):
    hi = x.astype(BF16)
    r1 = x - hi.astype(F32)
    mid = r1.astype(BF16)
    lo = (r1 - mid.astype(F32)).astype(BF16)
    return hi, mid, lo


def _dot_hp(t, w2_ref):
    t_hi, t_mid, _ = _split3(t)
    d = functools.partial(jnp.dot, preferred_element_type=F32)
    both = d(t_hi, w2_ref[...])
    return both[:, :LANES] + (d(t_mid, w2_ref[:, 0:LANES]) + both[:, LANES:])


def _iota_f32(shape, dim):
    return lax.broadcasted_iota(jnp.int32, shape, dim).astype(F32)


def _route(logits):
    lt = logits.T
    tm = lt.shape[1]
    row = _iota_f32((SUBLANES, tm), 0)
    big = float(SUBLANES)

    def rmax(x):
        return jnp.max(x, axis=0, keepdims=True)

    def first_row(mask):
        return jnp.min(jnp.where(mask, row, big), axis=0, keepdims=True)

    lg = jnp.where(row < float(N_GROUPS), lt[N_EXPERTS:N_EXPERTS + SUBLANES, :], -1e30)
    m = rmax(lg)
    g_top = 1.0 / jnp.sum(jnp.exp(lg - m), axis=0, keepdims=True)
    g_idx = first_row(lg == m)
    lem = lt[0:EXPERTS_PER_GROUP, :]
    for g in range(1, N_GROUPS):
        lem = jnp.where(g_idx == float(g), lt[g * EXPERTS_PER_GROUP:(g + 1) * EXPERTS_PER_GROUP, :], lem)
    pe = jnp.exp(lem - rmax(lem))
    p1 = rmax(pe)
    e1 = first_row(pe == p1)
    rest = row != e1
    pe2 = jnp.where(rest, pe, -1.0)
    p2 = rmax(pe2)
    e2 = first_row(rest & (pe2 == p2))
    scale = g_top / (p1 + p2)
    base = g_idx * float(EXPERTS_PER_GROUP)
    rec_t = jnp.where(row == 0.0, base + e1, jnp.where(row == 1.0, base + e2, 0.0))
    rec_t = rec_t + jnp.where(row == 2.0, p1 * scale, jnp.where(row == 3.0, p2 * scale, 0.0))
    return jnp.concatenate([rec_t, jnp.zeros((LANES - SUBLANES, tm), F32)], axis=0).T


def _outproj_body(c_ref, o_ref, x_ref, wo_ref, g2_ref, wr_ref, br_ref, x1_ref, t_ref, rec_ref):
    x1 = x_ref[...] + jnp.dot(c_ref[...], wo_ref[0:CONV_CH, :], preferred_element_type=F32)
    x1 = x1 + jnp.dot(o_ref[...], wo_ref[CONV_CH:, :], preferred_element_type=F32)
    x1_ref[...] = x1
    t = _rms(x1, g2_ref[...])
    t_ref[...] = t.astype(t_ref.dtype)
    rec_ref[...] = _route(_dot_hp(t, wr_ref) + br_ref[...])


def _outproj(c, o, x, w_out, g2, wr2, br, tm):
    t = x.shape[0]
    row = lambda i: (i, 0)
    const = lambda i: (0, 0)
    return pl.pallas_call(
        _outproj_body,
        grid=(t // tm,),
        in_specs=[
            pl.BlockSpec((tm, CONV_CH), row),
            pl.BlockSpec((tm, RET_WIDTH), row),
            pl.BlockSpec((tm, D_MODEL), row),
            pl.BlockSpec((D_MODEL, D_MODEL), const),
            pl.BlockSpec((1, D_MODEL), const),
            pl.BlockSpec((D_MODEL, 2 * LANES), const),
            pl.BlockSpec((1, LANES), const),
        ],
        out_specs=[
            pl.BlockSpec((tm, D_MODEL), row),
            pl.BlockSpec((tm, D_MODEL), row),
            pl.BlockSpec((tm, LANES), row),
        ],
        out_shape=[
            jax.ShapeDtypeStruct((t, D_MODEL), F32),
            jax.ShapeDtypeStruct((t, D_MODEL), BF16),
            jax.ShapeDtypeStruct((t, LANES), F32),
        ],
        compiler_params=_cparams(("parallel",)),
        name="outproj_router",
    )(c, o, x, w_out, g2, wr2, br)


def _mix_body(x_ref, g1_ref, w_ref, cos_ref, sin_ref, cw_ref, cb_ref, lg_ref, lb_ref,
              d_ref, xi_ref, zeta_ref, gc_ref, wo_ref, g2_ref, wr_ref, br_ref,
              x1_ref, t_ref, rec_ref, cst_ref, rst_ref, ext_ref, r_ref, o_ref):
    j = pl.program_id(1)
    tl = x_ref.shape[0]
    x = x_ref[...]
    z = jnp.dot(_rms(x, g1_ref[...]).astype(BF16), w_ref[...], preferred_element_type=F32)

    @pl.when(j == 0)
    def _():
        ext_ref[0:HALO, :] = jnp.zeros((HALO, CONV_CH), F32)
        ext_ref[tl + HALO:, :] = jnp.zeros((SUBLANES, CONV_CH), F32)
        r_ref[...] = jnp.zeros_like(r_ref)

    @pl.when(j > 0)
    def _():
        ext_ref[0:HALO, :] = ext_ref[tl:tl + HALO, :]

    ext_ref[HALO:tl + HALO, :] = z[:, :CONV_CH] * jax.nn.sigmoid(z[:, CONV_CH:2 * CONV_CH])
    acc = _dwconv(lambda s, n: ext_ref[s:s + n, :], cw_ref, tl) + cb_ref[...]
    c = _ln_silu(acc, lg_ref[...], lb_ref[...]).astype(BF16)
    cst_ref[0] = ext_ref[tl + HALO_OFF:tl + HALO, :]

    cos = cos_ref[...]
    sin = sin_ref[...]
    q0 = 2 * CONV_CH
    k0 = q0 + RET_WIDTH
    v0 = k0 + RET_WIDTH
    g0 = v0 + RET_WIDTH
    for hh in range(RET_HEADS):
        lo = hh * HEAD_DIM
        qh = z[:, q0 + lo:q0 + lo + HEAD_DIM]
        kh = z[:, k0 + lo:k0 + lo + HEAD_DIM]
        qr = (qh * cos + pltpu.roll(qh, HEAD_DIM // 2, 1) * sin).astype(BF16)
        kr = ((kh * cos + pltpu.roll(kh, HEAD_DIM // 2, 1) * sin) * (HEAD_DIM ** -0.5)).astype(BF16)
        vh = z[:, v0 + lo:v0 + lo + HEAD_DIM].astype(BF16)
        g = z[:, g0 + lo:g0 + lo + HEAD_DIM]
        gs = g * jax.nn.sigmoid(g)
        r = r_ref[hh]
        for ci in range(tl // CHUNK):
            rows = slice(ci * CHUNK, (ci + 1) * CHUNK)
            o, r = _ret_chunk(qr[rows], kr[rows], vh[rows], r, d_ref[hh], xi_ref[hh], zeta_ref[hh], gc_ref[hh])
            o_ref[rows, lo:lo + HEAD_DIM] = (gs[rows] * _group_norm(o)).astype(BF16)
        r_ref[hh] = r
    rst_ref[0] = r_ref[...]

    x1 = x + jnp.dot(c, wo_ref[0:CONV_CH, :], preferred_element_type=F32)
    x1 = x1 + jnp.dot(o_ref[...], wo_ref[CONV_CH:, :], preferred_element_type=F32)
    x1_ref[...] = x1
    t = _rms(x1, g2_ref[...])
    t_ref[...] = t.astype(t_ref.dtype)
    rec_ref[...] = _route(_dot_hp(t, wr_ref) + br_ref[...])


def _mix(x, g1, w_in, cos, sin, conv_w, conv_b, ln_g, ln_b, w_out, g2, wr2, br, n, l, tl):
    dmat, xi, zeta, gc = _decay_tables(CHUNK)
    per = l // tl
    row = lambda b, j: (b * per + j, 0)
    tab = lambda b, j: (j, 0)
    const = lambda b, j: (0, 0)
    c3 = lambda b, j: (0, 0, 0)
    once = dict(pipeline_mode=pl.Buffered(1))
    return pl.pallas_call(
        _mix_body,
        grid=(n, per),
        in_specs=[
            pl.BlockSpec((tl, D_MODEL), row),
            pl.BlockSpec((1, D_MODEL), const),
            pl.BlockSpec((D_MODEL, IN_COLS), const, **once),
            pl.BlockSpec((tl, HEAD_DIM), tab),
            pl.BlockSpec((tl, HEAD_DIM), tab),
            pl.BlockSpec((CONV_K, CONV_CH), const),
            pl.BlockSpec((1, CONV_CH), const),
            pl.BlockSpec((1, CONV_CH), const),
            pl.BlockSpec((1, CONV_CH), const),
            pl.BlockSpec((RET_HEADS, CHUNK, CHUNK), c3),
            pl.BlockSpec((RET_HEADS, CHUNK, HEAD_DIM), c3),
            pl.BlockSpec((RET_HEADS, CHUNK, HEAD_DIM), c3),
            pl.BlockSpec((RET_HEADS, 1, HEAD_DIM), c3),
            pl.BlockSpec((D_MODEL, D_MODEL), const, **once),
            pl.BlockSpec((1, D_MODEL), const),
            pl.BlockSpec((D_MODEL, 2 * LANES), const, **once),
            pl.BlockSpec((1, LANES), const),
        ],
        out_specs=[
            pl.BlockSpec((tl, D_MODEL), row),
            pl.BlockSpec((tl, D_MODEL), row),
            pl.BlockSpec((tl, LANES), row),
            pl.BlockSpec((1, CONV_K - 1, CONV_CH), lambda b, j: (b, 0, 0)),
            pl.BlockSpec((1, RET_HEADS, HEAD_DIM, HEAD_DIM), lambda b, j: (b, 0, 0, 0)),
        ],
        out_shape=[
            jax.ShapeDtypeStruct((n * l, D_MODEL), F32),
            jax.ShapeDtypeStruct((n * l, D_MODEL), BF16),
            jax.ShapeDtypeStruct((n * l, LANES), F32),
            jax.ShapeDtypeStruct((n, CONV_K - 1, CONV_CH), F32),
            jax.ShapeDtypeStruct((n, RET_HEADS, HEAD_DIM, HEAD_DIM), F32),
        ],
        scratch_shapes=[
            pltpu.VMEM((tl + HALO + SUBLANES, CONV_CH), F32),
            pltpu.VMEM((RET_HEADS, HEAD_DIM, HEAD_DIM), F32),
            pltpu.VMEM((tl, RET_WIDTH), BF16),
        ],
        compiler_params=pltpu.CompilerParams(dimension_semantics=("arbitrary", "arbitrary"),
                                             vmem_limit_bytes=MIX_VMEM_LIMIT),
        name="token_mix",
    )(x, g1, w_in, cos, sin, conv_w, conv_b, ln_g, ln_b, dmat, xi, zeta, gc, w_out, g2, wr2, br)


def _dispatch_body(rec_a_ref, t_a_ref, rec_b_ref, t_b_ref, s_ref, pos_ref, meta_ref, *, nsub_a):
    from_a = pl.program_id(0) * DISPATCH_SUBS < nsub_a
    for s in range(DISPATCH_SUBS):
        rows = slice(s * SUB, (s + 1) * SUB)
        rec = jnp.where(from_a, rec_a_ref[rows, :], rec_b_ref[rows, :])
        tok = jnp.where(from_a, t_a_ref[rows, :], t_b_ref[rows, :])
        chunks = slice(s * CHUNKS_PER_SUB, (s + 1) * CHUNKS_PER_SUB)
        _dispatch_sub_tile(rec, tok, s_ref.at[chunks], pos_ref.at[rows], meta_ref.at[s])


def _dispatch_sub_tile(rec, tok, s_ref, pos_ref, meta_ref):
    lane = _iota_f32(rec.shape, 1)
    a1 = lane == rec[:, 0:1]
    a2 = lane == rec[:, 1:2]
    a1f = jnp.where(a1, 1.0, 0.0)
    a2f = jnp.where(a2, 1.0, 0.0)
    ltri = jnp.where(_iota_f32((SUB, SUB), 1) < _iota_f32((SUB, SUB), 0), 1.0, 0.0).astype(BF16)
    c1 = jnp.dot(ltri, a1f.astype(BF16), preferred_element_type=F32)
    c2 = jnp.dot(ltri, a2f.astype(BF16), preferred_element_type=F32)
    n1 = jnp.sum(a1f, axis=0, keepdims=True)
    n2 = jnp.sum(a2f, axis=0, keepdims=True)
    cnt = jnp.floor((n1 + n2 + (ROW_ALIGN - 1.0)) * (1.0 / ROW_ALIGN))
    utri = jnp.where(_iota_f32((LANES, LANES), 0) < _iota_f32((LANES, LANES), 1), 1.0, 0.0).astype(BF16)
    start = jnp.dot(jnp.broadcast_to(cnt, (SUBLANES, LANES)).astype(BF16), utri,
                    preferred_element_type=F32)[0:1]
    base1 = start * ROW_ALIGN
    base2 = base1 + n1
    pos1 = jnp.sum(jnp.where(a1, c1 + base1, 0.0), axis=1, keepdims=True)
    pos2 = jnp.sum(jnp.where(a2, c2 + base2, 0.0), axis=1, keepdims=True)
    posm = jnp.where(lane == 0.0, pos1, jnp.where(lane == 1.0, pos2, 0.0))
    pos_ref[...] = posm
    row = _iota_f32((SUBLANES, LANES), 0)
    meta_ref[...] = jnp.where(row == 0.0, start, jnp.where(row == 1.0, cnt, 0.0))

    g1 = _split3(rec[:, 2:3])
    g2 = _split3(rec[:, 3:4])
    info = jnp.where(lane == 6.0, rec[:, 0:1], jnp.where(lane == 7.0, rec[:, 1:2], 0.0))
    for i in range(3):
        info = jnp.where(lane == float(i), g1[i].astype(F32), info)
        info = jnp.where(lane == float(3 + i), g2[i].astype(F32), info)
    src = jnp.concatenate([tok, info.astype(BF16)], axis=1)

    post = posm.T

    def sort_rows(nrows):
        r = _iota_f32((nrows, SUB), 0)
        onehot = jnp.where(r == post[0:1, :], 1.0, jnp.where(r == post[1:2, :], 1.0, 0.0)).astype(BF16)
        rows = jnp.dot(onehot, src, preferred_element_type=F32).astype(BF16)
        return rows.reshape(nrows // ROW_ALIGN, ROW_ALIGN, ROW_W)

    used = jnp.sum(cnt) * ROW_ALIGN

    @pl.when(used <= SUBP_HEAD)
    def _():
        s_ref[0:SUBP_HEAD // ROW_ALIGN] = sort_rows(SUBP_HEAD)
        s_ref[SUBP_HEAD // ROW_ALIGN:] = jnp.zeros(((SUBP - SUBP_HEAD) // ROW_ALIGN, ROW_ALIGN, ROW_W), BF16)

    @pl.when(used > SUBP_HEAD)
    def _():
        s_ref[...] = sort_rows(SUBP)


def _dispatch(rec_a, t_a, rec_b, t_b):
    nsub_a = rec_a.shape[0] // SUB
    nsub_b = rec_b.shape[0] // SUB
    nsub = nsub_a + nsub_b
    assert nsub_a % DISPATCH_SUBS == 0 and nsub_b % DISPATCH_SUBS == 0
    steps_a = nsub_a // DISPATCH_SUBS
    tm = DISPATCH_SUBS * SUB
    row = lambda i: (i, 0)
    row_a = lambda i: (jnp.minimum(i, steps_a - 1), 0)
    row_b = lambda i: (jnp.maximum(i - steps_a, 0), 0)
    return pl.pallas_call(
        functools.partial(_dispatch_body, nsub_a=nsub_a),
        grid=(nsub // DISPATCH_SUBS,),
        in_specs=[
            pl.BlockSpec((tm, LANES), row_a),
            pl.BlockSpec((tm, D_MODEL), row_a),
            pl.BlockSpec((tm, LANES), row_b),
            pl.BlockSpec((tm, D_MODEL), row_b),
        ],
        out_specs=[
            pl.BlockSpec((DISPATCH_SUBS * CHUNKS_PER_SUB, ROW_ALIGN, ROW_W), lambda i: (i, 0, 0)),
            pl.BlockSpec((tm, LANES), row),
            pl.BlockSpec((DISPATCH_SUBS, SUBLANES, LANES), lambda i: (i, 0, 0)),
        ],
        out_shape=[
            jax.ShapeDtypeStruct((nsub * CHUNKS_PER_SUB, ROW_ALIGN, ROW_W), BF16),
            jax.ShapeDtypeStruct((nsub * SUB, LANES), F32),
            jax.ShapeDtypeStruct((nsub, SUBLANES, LANES), F32),
        ],
        compiler_params=_cparams(("parallel",)),
        name="moe_dispatch",
    )(rec_a, t_a, rec_b, t_b)


def _experts_body(start_ref, cnt_ref, s_in, wg_ref, wu_ref, wd_ref, s_hbm,
                  xbuf, ybuf, wgu_ref, wdb_ref, gsem, ssem, list_ref, state_ref, *, nsub):
    del s_in
    e = pl.program_id(0)
    ne = pl.num_programs(0)
    par = e & 1

    list_max = list_ref.shape[0] // 2

    def gather_copy(src, i, slot):
        return pltpu.make_async_copy(s_hbm.at[src], xbuf.at[slot, i], gsem.at[slot])

    def scatter_copy(dst, i, slot):
        return pltpu.make_async_copy(ybuf.at[slot, i], s_hbm.at[dst], ssem.at[slot])

    def build_list(x, which):
        def per_sub(s, k):
            run = s * N_EXPERTS + x
            c = cnt_ref[run]
            base = s * CHUNKS_PER_SUB + start_ref[run]
            list_ref[k] = base
            list_ref[k + 1] = base + 1

            def per_chunk(i, carry):
                list_ref[k + i] = base + i
                return carry
            lax.fori_loop(2, c, per_chunk, 0)
            return k + c
        first = which * list_max
        state_ref[which] = lax.fori_loop(0, nsub, per_sub, first) - first

    def start_all(copy, which, first, n, slot, counter):
        def body(i, carry):
            copy(list_ref[which * list_max + first + i], i, slot).start()
            return carry
        lax.fori_loop(0, n, body, 0)
        state_ref[counter] = n

    def wait_all(copy, block_copy, slot, counter):
        n = state_ref[counter]

        @pl.when(n == CHUNKS_PER_BLK)
        def _():
            block_copy(slot).wait()

        @pl.when(n < CHUNKS_PER_BLK)
        def _():
            def body(i, carry):
                copy(0, 0, slot).wait()
                return carry
            lax.fori_loop(0, n, body, 0)
        state_ref[counter] = 0

    def gather_block(slot):
        return pltpu.make_async_copy(s_hbm.at[pl.ds(0, CHUNKS_PER_BLK)], xbuf.at[slot], gsem.at[slot])

    def scatter_block(slot):
        return pltpu.make_async_copy(ybuf.at[slot], s_hbm.at[pl.ds(0, CHUNKS_PER_BLK)], ssem.at[slot])

    def block_chunks(total, b):
        return jnp.minimum(total - b * CHUNKS_PER_BLK, CHUNKS_PER_BLK)

    @pl.when(e == 0)
    def _():
        for i in range(6):
            state_ref[i] = 0
        xbuf[...] = jnp.zeros_like(xbuf)
        build_list(0, 0)
        n0 = state_ref[0]

        @pl.when(n0 > 0)
        def _():
            start_all(gather_copy, 0, 0, block_chunks(n0, 0), 0, 2)

    @pl.when(e + 1 < ne)
    def _():
        build_list(e + 1, 1 - par)

    total = state_ref[par]
    nblk = (total + CHUNKS_PER_BLK - 1) // CHUNKS_PER_BLK
    wgu_ref[:, 0:EXPERT_FF] = wg_ref[0].astype(BF16)
    wgu_ref[:, EXPERT_FF:] = wu_ref[0].astype(BF16)
    wdb_ref[...] = wd_ref[0].astype(BF16)
    ef = e.astype(F32)

    def block(b, carry):
        slot = b & 1
        first = b * CHUNKS_PER_BLK

        @pl.when(b + 1 < nblk)
        def _():
            start_all(gather_copy, par, first + CHUNKS_PER_BLK, block_chunks(total, b + 1), 1 - slot, 3 - slot)

        wait_all(gather_copy, gather_block, slot, 2 + slot)
        wait_all(scatter_copy, scatter_block, slot, 4 + slot)
        x = xbuf[slot].reshape(MBLK, ROW_W)
        info = x[:, D_MODEL:].astype(F32)
        g_first = info[:, 0:1] + info[:, 1:2] + info[:, 2:3]
        g_second = info[:, 3:4] + info[:, 4:5] + info[:, 5:6]
        gate = jnp.where(info[:, 6:7] == ef, g_first, g_second)
        h = jnp.dot(x[:, :D_MODEL], wgu_ref[...], preferred_element_type=F32)
        h1 = h[:, :EXPERT_FF]
        hid = (h1 * jax.nn.sigmoid(h1)) * h[:, EXPERT_FF:] * gate
        y = jnp.dot(hid.astype(BF16), wdb_ref[...], preferred_element_type=F32).astype(BF16)
        ybuf[slot] = jnp.concatenate([y, x[:, D_MODEL:]], axis=1).reshape(CHUNKS_PER_BLK, ROW_ALIGN, ROW_W)
        start_all(scatter_copy, par, first, block_chunks(total, b), slot, 4 + slot)
        return carry

    lax.fori_loop(0, nblk, block, 0)

    @pl.when(e + 1 < ne)
    def _():
        n1 = state_ref[1 - par]

        @pl.when(n1 > 0)
        def _():
            start_all(gather_copy, 1 - par, 0, block_chunks(n1, 0), 0, 2)

    @pl.when(e == ne - 1)
    def _():
        wait_all(scatter_copy, scatter_block, 0, 4)
        wait_all(scatter_copy, scatter_block, 1, 5)


def _experts(start, cnt, staged, wg, wu, wd):
    nsub = staged.shape[0] // CHUNKS_PER_SUB
    list_max = nsub * SUB // ROW_ALIGN + nsub + LIST_SLACK
    wblk = lambda e, *_: (e, 0, 0)
    grid_spec = pltpu.PrefetchScalarGridSpec(
        num_scalar_prefetch=2,
        grid=(N_EXPERTS,),
        in_specs=[
            pl.BlockSpec(memory_space=pl.ANY),
            pl.BlockSpec((1, D_MODEL, EXPERT_FF), wblk),
            pl.BlockSpec((1, D_MODEL, EXPERT_FF), wblk),
            pl.BlockSpec((1, EXPERT_FF, D_MODEL), wblk),
        ],
        out_specs=pl.BlockSpec(memory_space=pl.ANY),
        scratch_shapes=[
            pltpu.VMEM((2, CHUNKS_PER_BLK, ROW_ALIGN, ROW_W), BF16),
            pltpu.VMEM((2, CHUNKS_PER_BLK, ROW_ALIGN, ROW_W), BF16),
            pltpu.VMEM((D_MODEL, 2 * EXPERT_FF), BF16),
            pltpu.VMEM((EXPERT_FF, D_MODEL), BF16),
            pltpu.SemaphoreType.DMA((2,)),
            pltpu.SemaphoreType.DMA((2,)),
            pltpu.SMEM((2 * list_max,), jnp.int32),
            pltpu.SMEM((6,), jnp.int32),
        ],
    )
    return pl.pallas_call(
        functools.partial(_experts_body, nsub=nsub),
        grid_spec=grid_spec,
        out_shape=jax.ShapeDtypeStruct(staged.shape, staged.dtype),
        input_output_aliases={2: 0},
        compiler_params=_cparams(("arbitrary",)),
        name="moe_experts",
    )(start, cnt, staged, wg, wu, wd)


def _combine_body(start_ref, cnt_ref, ys_ref, pos_ref, x1_ref, p_ref, wp_ref, gp_ref, wpg_ref, gf_ref, y_ref,
                  *, sub_off):
    moe = []
    for s in range(x1_ref.shape[0] // SUB):
        p1 = pos_ref[s * SUB:(s + 1) * SUB, 0:1]
        p2 = pos_ref[s * SUB:(s + 1) * SUB, 1:2]

        def gathered(nrows, s=s, p1=p1, p2=p2):
            r = _iota_f32((SUB, nrows), 1)
            onehot = jnp.where(r == p1, 1.0, jnp.where(r == p2, 1.0, 0.0)).astype(BF16)
            first = s * CHUNKS_PER_SUB
            ys = ys_ref[first:first + nrows // ROW_ALIGN].reshape(nrows, D_MODEL)
            return jnp.dot(onehot, ys, preferred_element_type=F32)

        last_run = (pl.program_id(0) * COMBINE_SUBS + s + sub_off) * N_EXPERTS + (N_EXPERTS - 1)
        used = (start_ref[last_run] + cnt_ref[last_run]) * ROW_ALIGN
        moe.append(lax.cond(used > SUBP_HEAD, functools.partial(gathered, SUBP),
                            functools.partial(gathered, SUBP_HEAD)))
    x2 = x1_ref[...] + jnp.concatenate(moe, axis=0)
    ple = _rms(jnp.dot(p_ref[...].astype(BF16), wp_ref[...], preferred_element_type=F32), gp_ref[...])
    gate = jax.nn.sigmoid(jnp.dot(x2.astype(BF16), wpg_ref[...], preferred_element_type=F32))
    y_ref[...] = _rms(x2 + ple * gate, gf_ref[...])


def _combine(start, cnt, ys, pos, x1, p, w_ple, gp, w_ple_gate, gf, sub_off):
    t = x1.shape[0]
    tm = COMBINE_SUBS * SUB
    blk_off = sub_off // COMBINE_SUBS
    assert sub_off % COMBINE_SUBS == 0 and t % tm == 0
    row = lambda i, *_: (i, 0)
    const = lambda i, *_: (0, 0)
    grid_spec = pltpu.PrefetchScalarGridSpec(
        num_scalar_prefetch=2,
        grid=(t // tm,),
        in_specs=[
            pl.BlockSpec((COMBINE_SUBS * CHUNKS_PER_SUB, ROW_ALIGN, D_MODEL), lambda i, *_: (i + blk_off, 0, 0)),
            pl.BlockSpec((tm, LANES), lambda i, *_: (i + blk_off, 0)),
            pl.BlockSpec((tm, D_MODEL), row),
            pl.BlockSpec((tm, PLE_DIM), row),
            pl.BlockSpec((PLE_DIM, D_MODEL), const),
            pl.BlockSpec((1, D_MODEL), const),
            pl.BlockSpec((D_MODEL, D_MODEL), const),
            pl.BlockSpec((1, D_MODEL), const),
        ],
        out_specs=pl.BlockSpec((tm, D_MODEL), row),
    )
    return pl.pallas_call(
        functools.partial(_combine_body, sub_off=sub_off),
        grid_spec=grid_spec,
        out_shape=jax.ShapeDtypeStruct((t, D_MODEL), F32),
        compiler_params=_cparams(("parallel",)),
        name="moe_combine_ple",
    )(start, cnt, ys, pos, x1, p, w_ple, gp, w_ple_gate, gf)


def _rope_tables(pos):
    half = HEAD_DIM // 2
    inv = ROPE_BASE ** (-jnp.arange(half, dtype=F32) / half)
    ang = pos[:, None] * inv[None, :]
    cos = jnp.cos(ang)
    sin = jnp.sin(ang)
    return jnp.concatenate([cos, cos], axis=-1), jnp.concatenate([-sin, sin], axis=-1)


def _router_params(we, be, wg, bg):
    pad = LANES - N_EXPERTS - N_GROUPS
    w = jnp.pad(jnp.concatenate([we, wg], axis=1), ((0, 0), (0, pad)))
    b = jnp.pad(jnp.concatenate([be, bg]), (0, pad))[None, :]
    return jnp.concatenate(_split3(w)[:2], axis=1), b


def kernel(x_prompt, x_sample, p_prompt, p_sample, state_conv, state_ret, w_in, conv_w, conv_b, conv_ln_g, conv_ln_b, w_out, norm1_g, norm2_g, router_group_w, router_group_b, router_expert_w, router_expert_b, w_expert_gate, w_expert_up, w_expert_down, w_ple, ple_norm_g, w_ple_gate, final_norm_g):
    assert w_in.shape[0] == 1, "single-layer trunk"
    nb, seq, _ = x_prompt.shape
    ns, dseq, _ = x_sample.shape
    tm = SAMPLE_TILE

    w_in_b = w_in[0].astype(BF16)
    w_out_b = w_out[0].astype(BF16)
    w_ple_b = w_ple[0].astype(BF16)
    w_pg_b = w_ple_gate[0].astype(BF16)
    g1 = norm1_g[0][None, :]
    g2 = norm2_g[0][None, :]
    gp = ple_norm_g[0][None, :]
    gf = final_norm_g[None, :]
    cb = conv_b[0][None, :]
    lng = conv_ln_g[0][None, :]
    lnb = conv_ln_b[0][None, :]
    wr2, br = _router_params(router_expert_w[0], router_expert_b[0], router_group_w[0], router_group_b[0])

    cos_p, sin_p = _rope_tables(jnp.arange(seq, dtype=F32) + jnp.float32(0))
    pos_s = jnp.tile(jnp.arange(dseq, dtype=F32) + jnp.float32(PAST_LEN), tm // dseq)
    cos_s, sin_s = _rope_tables(pos_s)

    xp = x_prompt.reshape(nb * seq, D_MODEL)
    x1_p, t_p, rec_p, conv_p, ret_p = _mix(xp, g1, w_in_b, cos_p, sin_p, conv_w[0], cb, lng, lnb, w_out_b, g2,
                                           wr2, br, nb, seq, MIX_TILE)

    xs = x_sample.reshape(ns * dseq, D_MODEL)
    u, q, k, v, gs = _inproj(xs, g1, w_in_b, cos_s, sin_s, tm)
    c, conv_s = _conv_sample(u.reshape(ns, dseq, CONV_CH), jnp.transpose(state_conv[0], (1, 0, 2)), conv_w[0], cb,
                             lng, lnb, CONV_SAMPLE_SEQS)
    conv_s = jnp.transpose(conv_s, (1, 0, 2))
    o, ret_s = _ret_sample(q, k, v, gs, state_ret[0], dseq, RET_SAMPLE_SEQS)
    x1_s, t_s, rec_s = _outproj(c.reshape(ns * dseq, CONV_CH), o, xs, w_out_b, g2, wr2, br, tm)

    staged, pos, meta = _dispatch(rec_p, t_p, rec_s, t_s)
    start = meta[:, 0, :N_EXPERTS].astype(jnp.int32).reshape(-1)
    cnt = meta[:, 1, :N_EXPERTS].astype(jnp.int32).reshape(-1)
    ys = _experts(start, cnt, staged, w_expert_gate[0], w_expert_up[0], w_expert_down[0])

    y_p = _combine(start, cnt, ys, pos, x1_p, p_prompt[0].reshape(nb * seq, PLE_DIM), w_ple_b, gp, w_pg_b, gf, 0)
    y_s = _combine(start, cnt, ys, pos, x1_s, p_sample[0].reshape(ns * dseq, PLE_DIM), w_ple_b, gp, w_pg_b, gf,
                   nb * seq // SUB)

    return (y_p.reshape(nb, seq, D_MODEL), y_s.reshape(ns, dseq, D_MODEL),
            conv_p[None], ret_p[None], conv_s[None], ret_s[None])
```

```python
import functools

import jax
import jax.numpy as jnp
from jax import lax
from jax.experimental import pallas as pl
from jax.experimental.pallas import tpu as pltpu

F32 = jnp.float32
BF16 = jnp.bfloat16

D_MODEL = 1024
PLE_DIM = 256
CONV_CH = 512
CONV_K = 31
RET_WIDTH = 512
RET_HEADS = 4
HEAD_DIM = 128
CHUNK = 128
ROPE_BASE = 10000.0
N_GROUPS = 4
EXPERTS_PER_GROUP = 8
N_EXPERTS = 32
EXPERT_FF = 256
IN_COLS = 3072
EPS = 1e-6
PAST_LEN = 16384

LANES = 128
SUBLANES = 8
HALO = 32
HALO_OFF = HALO - (CONV_K - 1)
VMEM_LIMIT = 48 * 1024 * 1024
MIX_VMEM_LIMIT = 56 * 1024 * 1024
MIX_TILE = 512
SAMPLE_TILE = 1024
CONV_SAMPLE_SEQS = 32
RET_SAMPLE_SEQS = 16

SUB = 256
ROW_ALIGN = 16
PBLK = 256
SUBP = -(-(2 * SUB + N_EXPERTS * (ROW_ALIGN - 1)) // PBLK) * PBLK
CHUNKS_PER_SUB = SUBP // ROW_ALIGN
ROW_W = D_MODEL + LANES
MBLK = 512
CHUNKS_PER_BLK = MBLK // ROW_ALIGN
LIST_SLACK = 2
COMBINE_SUBS = 4
DISPATCH_SUBS = 4


def _cparams(sem):
    return pltpu.CompilerParams(dimension_semantics=sem, vmem_limit_bytes=VMEM_LIMIT)


def _rms(x, g):
    return x * lax.rsqrt(jnp.mean(x * x, axis=-1, keepdims=True) + EPS) * g


def _inproj_body(x_ref, g1_ref, w_ref, cos_ref, sin_ref, u_ref, q_ref, k_ref, v_ref, gs_ref):
    h = _rms(x_ref[...], g1_ref[...]).astype(BF16)
    z = jnp.dot(h, w_ref[...], preferred_element_type=F32)
    a = z[:, :CONV_CH]
    b = z[:, CONV_CH:2 * CONV_CH]
    u_ref[...] = a * jax.nn.sigmoid(b)
    cos = cos_ref[...]
    sin = sin_ref[...]
    q0 = 2 * CONV_CH
    k0 = q0 + RET_WIDTH
    for hh in range(RET_HEADS):
        sl = slice(hh * HEAD_DIM, (hh + 1) * HEAD_DIM)
        qh = z[:, q0 + hh * HEAD_DIM:q0 + (hh + 1) * HEAD_DIM]
        kh = z[:, k0 + hh * HEAD_DIM:k0 + (hh + 1) * HEAD_DIM]
        q_ref[:, sl] = qh * cos + pltpu.roll(qh, HEAD_DIM // 2, 1) * sin
        k_ref[:, sl] = (kh * cos + pltpu.roll(kh, HEAD_DIM // 2, 1) * sin) * (HEAD_DIM ** -0.5)
    v_ref[...] = z[:, k0 + RET_WIDTH:k0 + 2 * RET_WIDTH]
    g = z[:, k0 + 2 * RET_WIDTH:]
    gs_ref[...] = g * jax.nn.sigmoid(g)


def _inproj(x, g1, w_in, cos, sin, tm):
    t = x.shape[0]
    row = lambda i: (i, 0)
    const = lambda i: (0, 0)
    return pl.pallas_call(
        _inproj_body,
        grid=(t // tm,),
        in_specs=[
            pl.BlockSpec((tm, D_MODEL), row),
            pl.BlockSpec((1, D_MODEL), const),
            pl.BlockSpec((D_MODEL, IN_COLS), const),
            pl.BlockSpec((tm, HEAD_DIM), const),
            pl.BlockSpec((tm, HEAD_DIM), const),
        ],
        out_specs=[pl.BlockSpec((tm, CONV_CH), row)] + [pl.BlockSpec((tm, RET_WIDTH), row)] * 4,
        out_shape=[jax.ShapeDtypeStruct((t, CONV_CH), F32)] + [jax.ShapeDtypeStruct((t, RET_WIDTH), F32)] * 4,
        compiler_params=_cparams(("parallel",)),
        name="inproj",
    )(x, g1, w_in, cos, sin)


def _ln_silu(acc, g, b):
    mu = jnp.mean(acc, axis=-1, keepdims=True)
    d = acc - mu
    var = jnp.mean(d * d, axis=-1, keepdims=True)
    y = d * lax.rsqrt(var + EPS) * g + b
    return y * jax.nn.sigmoid(y)


def _dwconv(load, w_ref, rows):
    acc = None
    for b in range(SUBLANES):
        part = None
        for a in range((CONV_K + HALO_OFF) // SUBLANES + 1):
            k = SUBLANES * a + b - HALO_OFF
            if 0 <= k < CONV_K:
                term = load(SUBLANES * a, rows + SUBLANES) * w_ref[k:k + 1, :]
                part = term if part is None else part + term
        if part is not None:
            shifted = part[b:b + rows]
            acc = shifted if acc is None else acc + shifted
    return acc


def _conv_sample_body(u_ref, st_ref, w_ref, cb_ref, lg_ref, lb_ref, c_ref, nst_ref, ext_ref):
    l = u_ref.shape[1]
    hist = CONV_K - 1
    ext_ref[0:hist] = st_ref[...]
    ext_ref[hist:] = jnp.transpose(u_ref[...], (1, 0, 2))
    acc = ext_ref[0:l] * w_ref[0:1, :] + cb_ref[...]
    for k in range(1, CONV_K):
        acc = acc + ext_ref[k:k + l] * w_ref[k:k + 1, :]
    y = _ln_silu(acc, lg_ref[...], lb_ref[...])
    c_ref[...] = jnp.transpose(y, (1, 0, 2)).astype(c_ref.dtype)
    nst_ref[...] = ext_ref[l:]


def _conv_sample(u, state_t, conv_w, conv_b, ln_g, ln_b, nb):
    n, l, _ = u.shape
    hist = CONV_K - 1
    const = lambda b: (0, 0)
    tok = lambda b: (b, 0, 0)
    tmaj = lambda b: (0, b, 0)
    return pl.pallas_call(
        _conv_sample_body,
        grid=(n // nb,),
        in_specs=[
            pl.BlockSpec((nb, l, CONV_CH), tok),
            pl.BlockSpec((hist, nb, CONV_CH), tmaj),
            pl.BlockSpec((CONV_K, CONV_CH), const),
            pl.BlockSpec((1, CONV_CH), const),
            pl.BlockSpec((1, CONV_CH), const),
            pl.BlockSpec((1, CONV_CH), const),
        ],
        out_specs=[
            pl.BlockSpec((nb, l, CONV_CH), tok),
            pl.BlockSpec((hist, nb, CONV_CH), tmaj),
        ],
        out_shape=[
            jax.ShapeDtypeStruct((n, l, CONV_CH), BF16),
            jax.ShapeDtypeStruct((hist, n, CONV_CH), F32),
        ],
        scratch_shapes=[pltpu.VMEM((hist + l, nb, CONV_CH), F32)],
        compiler_params=_cparams(("parallel",)),
        name="conv_sample",
    )(u, state_t, conv_w, conv_b, ln_g, ln_b)


def _decay_tables(c):
    lg = jnp.log(1.0 - 2.0 ** (-5.0 - jnp.arange(RET_HEADS, dtype=F32)))
    idx = jnp.arange(c, dtype=F32)
    rel = idx[:, None] - idx[None, :]
    dmat = jnp.where(rel[None] >= 0, jnp.exp(jnp.maximum(rel, 0.0)[None] * lg[:, None, None]), 0.0)
    xi = jnp.exp((idx + 1.0)[None, :] * lg[:, None])
    zeta = jnp.exp((c - 1.0 - idx)[None, :] * lg[:, None])
    gc = jnp.exp(c * lg)
    xi_b = jnp.broadcast_to(xi[:, :, None], (RET_HEADS, c, HEAD_DIM))
    zeta_b = jnp.broadcast_to(zeta[:, :, None], (RET_HEADS, c, HEAD_DIM))
    gc_b = jnp.broadcast_to(gc[:, None, None], (RET_HEADS, 1, HEAD_DIM))
    return dmat, xi_b, zeta_b, gc_b


def _group_norm(o):
    mu = jnp.mean(o, axis=-1, keepdims=True)
    d = o - mu
    var = jnp.mean(d * d, axis=-1, keepdims=True)
    return d * lax.rsqrt(var + EPS)


def _ret_chunk(qh, kh, vh, r, dmat, xi, zeta, gc):
    qb = qh.astype(BF16)
    kb = kh.astype(BF16)
    vb = vh.astype(BF16)
    s = lax.dot_general(qb, kb, (((1,), (1,)), ((), ())), preferred_element_type=F32) * dmat
    o = jnp.dot(s.astype(BF16), vb, preferred_element_type=F32)
    o = o + jnp.dot(qb, r.astype(BF16), preferred_element_type=F32) * xi
    kz = (kh.astype(F32) * zeta).astype(BF16)
    r_new = r * gc + lax.dot_general(kz, vb, (((0,), (0,)), ((), ())), preferred_element_type=F32)
    return o, r_new


def _ret_sample_body(q_ref, k_ref, v_ref, gs_ref, st_ref, d_ref, xi_ref, zeta_ref, gc_ref, o_ref, nst_ref):
    nb = st_ref.shape[0]
    l = q_ref.shape[0] // nb
    for b in range(nb):
        rows = slice(b * l, (b + 1) * l)
        for hh in range(RET_HEADS):
            sl = slice(hh * HEAD_DIM, (hh + 1) * HEAD_DIM)
            o, r = _ret_chunk(q_ref[rows, sl], k_ref[rows, sl], v_ref[rows, sl], st_ref[b, hh],
                              d_ref[hh], xi_ref[hh], zeta_ref[hh], gc_ref[hh])
            o_ref[rows, sl] = (gs_ref[rows, sl] * _group_norm(o)).astype(o_ref.dtype)
            nst_ref[b, hh] = r


def _ret_sample(q, k, v, gs, state, l, nb):
    n = state.shape[0]
    dmat, xi, zeta, gc = _decay_tables(l)
    row = lambda b: (b, 0)
    c3 = lambda b: (0, 0, 0)
    blk4 = lambda b: (b, 0, 0, 0)
    return pl.pallas_call(
        _ret_sample_body,
        grid=(n // nb,),
        in_specs=[pl.BlockSpec((nb * l, RET_WIDTH), row)] * 4 + [
            pl.BlockSpec((nb, RET_HEADS, HEAD_DIM, HEAD_DIM), blk4),
            pl.BlockSpec((RET_HEADS, l, l), c3),
            pl.BlockSpec((RET_HEADS, l, HEAD_DIM), c3),
            pl.BlockSpec((RET_HEADS, l, HEAD_DIM), c3),
            pl.BlockSpec((RET_HEADS, 1, HEAD_DIM), c3),
        ],
        out_specs=[
            pl.BlockSpec((nb * l, RET_WIDTH), row),
            pl.BlockSpec((nb, RET_HEADS, HEAD_DIM, HEAD_DIM), blk4),
        ],
        out_shape=[
            jax.ShapeDtypeStruct((n * l, RET_WIDTH), BF16),
            jax.ShapeDtypeStruct((n, RET_HEADS, HEAD_DIM, HEAD_DIM), F32),
        ],
        compiler_params=_cparams(("parallel",)),
        name="ret_sample",
    )(q, k, v, gs, state, dmat, xi, zeta, gc)


def _split3(x):
    hi = x.astype(BF16)
    r1 = x - hi.astype(F32)
    mid = r1.astype(BF16)
    lo = (r1 - mid.astype(F32)).astype(BF16)
    return hi, mid, lo


def _dot_hp(t, w2_ref):
    t_hi, t_mid, _ = _split3(t)
    d = functools.partial(jnp.dot, preferred_element_type=F32)
    both = d(t_hi, w2_ref[...])
    return both[:, :LANES] + (d(t_mid, w2_ref[:, 0:LANES]) + both[:, LANES:])


def _iota_f32(shape, dim):
    return lax.broadcasted_iota(jnp.int32, shape, dim).astype(F32)


def _route(logits):
    lt = logits.T
    tm = lt.shape[1]
    row = _iota_f32((SUBLANES, tm), 0)
    big = float(SUBLANES)

    def rmax(x):
        return jnp.max(x, axis=0, keepdims=True)

    def first_row(mask):
        return jnp.min(jnp.where(mask, row, big), axis=0, keepdims=True)

    lg = jnp.where(row < float(N_GROUPS), lt[N_EXPERTS:N_EXPERTS + SUBLANES, :], -1e30)
    m = rmax(lg)
    g_top = 1.0 / jnp.sum(jnp.exp(lg - m), axis=0, keepdims=True)
    g_idx = first_row(lg == m)
    lem = lt[0:EXPERTS_PER_GROUP, :]
    for g in range(1, N_GROUPS):
        lem = jnp.where(g_idx == float(g), lt[g * EXPERTS_PER_GROUP:(g + 1) * EXPERTS_PER_GROUP, :], lem)
    pe = jnp.exp(lem - rmax(lem))
    p1 = rmax(pe)
    e1 = first_row(pe == p1)
    rest = row != e1
    pe2 = jnp.where(rest, pe, -1.0)
    p2 = rmax(pe2)
    e2 = first_row(rest & (pe2 == p2))
    scale = g_top / (p1 + p2)
    base = g_idx * float(EXPERTS_PER_GROUP)
    rec_t = jnp.where(row == 0.0, base + e1, jnp.where(row == 1.0, base + e2, 0.0))
    rec_t = rec_t + jnp.where(row == 2.0, p1 * scale, jnp.where(row == 3.0, p2 * scale, 0.0))
    return jnp.concatenate([rec_t, jnp.zeros((LANES - SUBLANES, tm), F32)], axis=0).T


def _outproj_body(c_ref, o_ref, x_ref, wo_ref, g2_ref, wr_ref, br_ref, x1_ref, t_ref, rec_ref):
    x1 = x_ref[...] + jnp.dot(c_ref[...], wo_ref[0:CONV_CH, :], preferred_element_type=F32)
    x1 = x1 + jnp.dot(o_ref[...], wo_ref[CONV_CH:, :], preferred_element_type=F32)
    x1_ref[...] = x1
    t = _rms(x1, g2_ref[...])
    t_ref[...] = t.astype(t_ref.dtype)
    rec_ref[...] = _route(_dot_hp(t, wr_ref) + br_ref[...])


def _outproj(c, o, x, w_out, g2, wr2, br, tm):
    t = x.shape[0]
    row = lambda i: (i, 0)
    const = lambda i: (0, 0)
    return pl.pallas_call(
        _outproj_body,
        grid=(t // tm,),
        in_specs=[
            pl.BlockSpec((tm, CONV_CH), row),
            pl.BlockSpec((tm, RET_WIDTH), row),
            pl.BlockSpec((tm, D_MODEL), row),
            pl.BlockSpec((D_MODEL, D_MODEL), const),
            pl.BlockSpec((1, D_MODEL), const),
            pl.BlockSpec((D_MODEL, 2 * LANES), const),
            pl.BlockSpec((1, LANES), const),
        ],
        out_specs=[
            pl.BlockSpec((tm, D_MODEL), row),
            pl.BlockSpec((tm, D_MODEL), row),
            pl.BlockSpec((tm, LANES), row),
        ],
        out_shape=[
            jax.ShapeDtypeStruct((t, D_MODEL), F32),
            jax.ShapeDtypeStruct((t, D_MODEL), BF16),
            jax.ShapeDtypeStruct((t, LANES), F32),
        ],
        compiler_params=_cparams(("parallel",)),
        name="outproj_router",
    )(c, o, x, w_out, g2, wr2, br)


def _mix_body(x_ref, g1_ref, w_ref, cos_ref, sin_ref, cw_ref, cb_ref, lg_ref, lb_ref,
              d_ref, xi_ref, zeta_ref, gc_ref, wo_ref, g2_ref, wr_ref, br_ref,
              x1_ref, t_ref, rec_ref, cst_ref, rst_ref, ext_ref, r_ref, o_ref):
    j = pl.program_id(1)
    tl = x_ref.shape[0]
    x = x_ref[...]
    z = jnp.dot(_rms(x, g1_ref[...]).astype(BF16), w_ref[...], preferred_element_type=F32)

    @pl.when(j == 0)
    def _():
        ext_ref[0:HALO, :] = jnp.zeros((HALO, CONV_CH), F32)
        ext_ref[tl + HALO:, :] = jnp.zeros((SUBLANES, CONV_CH), F32)
        r_ref[...] = jnp.zeros_like(r_ref)

    @pl.when(j > 0)
    def _():
        ext_ref[0:HALO, :] = ext_ref[tl:tl + HALO, :]

    ext_ref[HALO:tl + HALO, :] = z[:, :CONV_CH] * jax.nn.sigmoid(z[:, CONV_CH:2 * CONV_CH])
    acc = _dwconv(lambda s, n: ext_ref[s:s + n, :], cw_ref, tl) + cb_ref[...]
    o_ref[:, 0:CONV_CH] = _ln_silu(acc, lg_ref[...], lb_ref[...]).astype(BF16)
    cst_ref[0] = ext_ref[tl + HALO_OFF:tl + HALO, :]

    cos = cos_ref[...]
    sin = sin_ref[...]
    q0 = 2 * CONV_CH
    k0 = q0 + RET_WIDTH
    v0 = k0 + RET_WIDTH
    g0 = v0 + RET_WIDTH
    for hh in range(RET_HEADS):
        lo = hh * HEAD_DIM
        qh = z[:, q0 + lo:q0 + lo + HEAD_DIM]
        kh = z[:, k0 + lo:k0 + lo + HEAD_DIM]
        qr = (qh * cos + pltpu.roll(qh, HEAD_DIM // 2, 1) * sin).astype(BF16)
        kr = ((kh * cos + pltpu.roll(kh, HEAD_DIM // 2, 1) * sin) * (HEAD_DIM ** -0.5)).astype(BF16)
        vh = z[:, v0 + lo:v0 + lo + HEAD_DIM].astype(BF16)
        g = z[:, g0 + lo:g0 + lo + HEAD_DIM]
        gs = g * jax.nn.sigmoid(g)
        r = r_ref[hh]
        for ci in range(tl // CHUNK):
            rows = slice(ci * CHUNK, (ci + 1) * CHUNK)
            o, r = _ret_chunk(qr[rows], kr[rows], vh[rows], r, d_ref[hh], xi_ref[hh], zeta_ref[hh], gc_ref[hh])
            o_ref[rows, CONV_CH + lo:CONV_CH + lo + HEAD_DIM] = (gs[rows] * _group_norm(o)).astype(BF16)
        r_ref[hh] = r
    rst_ref[0] = r_ref[...]

    x1 = x + jnp.dot(o_ref[...], wo_ref[...], preferred_element_type=F32)
    x1_ref[...] = x1
    t = _rms(x1, g2_ref[...])
    t_ref[...] = t.astype(t_ref.dtype)
    rec_ref[...] = _route(_dot_hp(t, wr_ref) + br_ref[...])


def _mix(x, g1, w_in, cos, sin, conv_w, conv_b, ln_g, ln_b, w_out, g2, wr2, br, n, l, tl):
    dmat, xi, zeta, gc = _decay_tables(CHUNK)
    per = l // tl
    row = lambda b, j: (b * per + j, 0)
    tab = lambda b, j: (j, 0)
    const = lambda b, j: (0, 0)
    c3 = lambda b, j: (0, 0, 0)
    once = dict(pipeline_mode=pl.Buffered(1))
    return pl.pallas_call(
        _mix_body,
        grid=(n, per),
        in_specs=[
            pl.BlockSpec((tl, D_MODEL), row),
            pl.BlockSpec((1, D_MODEL), const),
            pl.BlockSpec((D_MODEL, IN_COLS), const, **once),
            pl.BlockSpec((tl, HEAD_DIM), tab),
            pl.BlockSpec((tl, HEAD_DIM), tab),
            pl.BlockSpec((CONV_K, CONV_CH), const),
            pl.BlockSpec((1, CONV_CH), const),
            pl.BlockSpec((1, CONV_CH), const),
            pl.BlockSpec((1, CONV_CH), const),
            pl.BlockSpec((RET_HEADS, CHUNK, CHUNK), c3),
            pl.BlockSpec((RET_HEADS, CHUNK, HEAD_DIM), c3),
            pl.BlockSpec((RET_HEADS, CHUNK, HEAD_DIM), c3),
            pl.BlockSpec((RET_HEADS, 1, HEAD_DIM), c3),
            pl.BlockSpec((D_MODEL, D_MODEL), const, **once),
            pl.BlockSpec((1, D_MODEL), const),
            pl.BlockSpec((D_MODEL, 2 * LANES), const, **once),
            pl.BlockSpec((1, LANES), const),
        ],
        out_specs=[
            pl.BlockSpec((tl, D_MODEL), row),
            pl.BlockSpec((tl, D_MODEL), row),
            pl.BlockSpec((tl, LANES), row),
            pl.BlockSpec((1, CONV_K - 1, CONV_CH), lambda b, j: (b, 0, 0)),
            pl.BlockSpec((1, RET_HEADS, HEAD_DIM, HEAD_DIM), lambda b, j: (b, 0, 0, 0)),
        ],
        out_shape=[
            jax.ShapeDtypeStruct((n * l, D_MODEL), F32),
            jax.ShapeDtypeStruct((n * l, D_MODEL), BF16),
            jax.ShapeDtypeStruct((n * l, LANES), F32),
            jax.ShapeDtypeStruct((n, CONV_K - 1, CONV_CH), F32),
            jax.ShapeDtypeStruct((n, RET_HEADS, HEAD_DIM, HEAD_DIM), F32),
        ],
        scratch_shapes=[
            pltpu.VMEM((tl + HALO + SUBLANES, CONV_CH), F32),
            pltpu.VMEM((RET_HEADS, HEAD_DIM, HEAD_DIM), F32),
            pltpu.VMEM((tl, D_MODEL), BF16),
        ],
        compiler_params=pltpu.CompilerParams(dimension_semantics=("arbitrary", "arbitrary"),
                                             vmem_limit_bytes=MIX_VMEM_LIMIT),
        name="token_mix",
    )(x, g1, w_in, cos, sin, conv_w, conv_b, ln_g, ln_b, dmat, xi, zeta, gc, w_out, g2, wr2, br)


def _dispatch_body(rec_a_ref, t_a_ref, rec_b_ref, t_b_ref, s_ref, pos_ref, meta_ref, *, nsub_a):
    from_a = pl.program_id(0) * DISPATCH_SUBS < nsub_a
    for s in range(DISPATCH_SUBS):
        rows = slice(s * SUB, (s + 1) * SUB)
        rec = jnp.where(from_a, rec_a_ref[rows, :], rec_b_ref[rows, :])
        tok = jnp.where(from_a, t_a_ref[rows, :], t_b_ref[rows, :])
        chunks = slice(s * CHUNKS_PER_SUB, (s + 1) * CHUNKS_PER_SUB)
        _dispatch_sub_tile(rec, tok, s_ref.at[chunks], pos_ref.at[rows], meta_ref.at[s])


def _dispatch_sub_tile(rec, tok, s_ref, pos_ref, meta_ref):
    lane = _iota_f32(rec.shape, 1)
    a1 = lane == rec[:, 0:1]
    a2 = lane == rec[:, 1:2]
    a1f = jnp.where(a1, 1.0, 0.0)
    a2f = jnp.where(a2, 1.0, 0.0)
    ltri = jnp.where(_iota_f32((SUB, SUB), 1) < _iota_f32((SUB, SUB), 0), 1.0, 0.0).astype(BF16)
    c1 = jnp.dot(ltri, a1f.astype(BF16), preferred_element_type=F32)
    c2 = jnp.dot(ltri, a2f.astype(BF16), preferred_element_type=F32)
    n1 = jnp.sum(a1f, axis=0, keepdims=True)
    n2 = jnp.sum(a2f, axis=0, keepdims=True)
    cnt = jnp.floor((n1 + n2 + (ROW_ALIGN - 1.0)) * (1.0 / ROW_ALIGN))
    utri = jnp.where(_iota_f32((LANES, LANES), 0) < _iota_f32((LANES, LANES), 1), 1.0, 0.0).astype(BF16)
    start = jnp.dot(jnp.broadcast_to(cnt, (SUBLANES, LANES)).astype(BF16), utri,
                    preferred_element_type=F32)[0:1]
    base1 = start * ROW_ALIGN
    base2 = base1 + n1
    pos1 = jnp.sum(jnp.where(a1, c1 + base1, 0.0), axis=1, keepdims=True)
    pos2 = jnp.sum(jnp.where(a2, c2 + base2, 0.0), axis=1, keepdims=True)
    posm = jnp.where(lane == 0.0, pos1, jnp.where(lane == 1.0, pos2, 0.0))
    pos_ref[...] = posm
    row = _iota_f32((SUBLANES, LANES), 0)
    meta_ref[...] = jnp.where(row == 0.0, start, jnp.where(row == 1.0, cnt, 0.0))

    g1 = _split3(rec[:, 2:3])
    g2 = _split3(rec[:, 3:4])
    info = jnp.where(lane == 6.0, rec[:, 0:1], jnp.where(lane == 7.0, rec[:, 1:2], 0.0))
    for i in range(3):
        info = jnp.where(lane == float(i), g1[i].astype(F32), info)
        info = jnp.where(lane == float(3 + i), g2[i].astype(F32), info)
    src = jnp.concatenate([tok, info.astype(BF16)], axis=1)

    post = posm.T
    r = _iota_f32((SUBP, SUB), 0)
    onehot = jnp.where(r == post[0:1, :], 1.0, jnp.where(r == post[1:2, :], 1.0, 0.0)).astype(BF16)
    sorted_rows = jnp.dot(onehot, src, preferred_element_type=F32).astype(BF16)
    s_ref[...] = sorted_rows.reshape(CHUNKS_PER_SUB, ROW_ALIGN, ROW_W)


def _dispatch(rec_a, t_a, rec_b, t_b):
    nsub_a = rec_a.shape[0] // SUB
    nsub_b = rec_b.shape[0] // SUB
    nsub = nsub_a + nsub_b
    assert nsub_a % DISPATCH_SUBS == 0 and nsub_b % DISPATCH_SUBS == 0
    steps_a = nsub_a // DISPATCH_SUBS
    tm = DISPATCH_SUBS * SUB
    row = lambda i: (i, 0)
    row_a = lambda i: (jnp.minimum(i, steps_a - 1), 0)
    row_b = lambda i: (jnp.maximum(i - steps_a, 0), 0)
    return pl.pallas_call(
        functools.partial(_dispatch_body, nsub_a=nsub_a),
        grid=(nsub // DISPATCH_SUBS,),
        in_specs=[
            pl.BlockSpec((tm, LANES), row_a),
            pl.BlockSpec((tm, D_MODEL), row_a),
            pl.BlockSpec((tm, LANES), row_b),
            pl.BlockSpec((tm, D_MODEL), row_b),
        ],
        out_specs=[
            pl.BlockSpec((DISPATCH_SUBS * CHUNKS_PER_SUB, ROW_ALIGN, ROW_W), lambda i: (i, 0, 0)),
            pl.BlockSpec((tm, LANES), row),
            pl.BlockSpec((DISPATCH_SUBS, SUBLANES, LANES), lambda i: (i, 0, 0)),
        ],
        out_shape=[
            jax.ShapeDtypeStruct((nsub * CHUNKS_PER_SUB, ROW_ALIGN, ROW_W), BF16),
            jax.ShapeDtypeStruct((nsub * SUB, LANES), F32),
            jax.ShapeDtypeStruct((nsub, SUBLANES, LANES), F32),
        ],
        compiler_params=_cparams(("parallel",)),
        name="moe_dispatch",
    )(rec_a, t_a, rec_b, t_b)


def _experts_body(start_ref, cnt_ref, s_in, wg_ref, wu_ref, wd_ref, s_hbm,
                  xbuf, ybuf, wgu_ref, wdb_ref, gsem, ssem, list_ref, state_ref, *, nsub):
    del s_in
    e = pl.program_id(0)
    ne = pl.num_programs(0)
    par = e & 1

    list_max = list_ref.shape[0] // 2

    def gather_copy(src, i, slot):
        return pltpu.make_async_copy(s_hbm.at[src], xbuf.at[slot, i], gsem.at[slot])

    def scatter_copy(dst, i, slot):
        return pltpu.make_async_copy(ybuf.at[slot, i], s_hbm.at[dst], ssem.at[slot])

    def build_list(x, which):
        def per_sub(s, k):
            run = s * N_EXPERTS + x
            c = cnt_ref[run]
            base = s * CHUNKS_PER_SUB + start_ref[run]
            list_ref[k] = base
            list_ref[k + 1] = base + 1

            def per_chunk(i, carry):
                list_ref[k + i] = base + i
                return carry
            lax.fori_loop(2, c, per_chunk, 0)
            return k + c
        first = which * list_max
        state_ref[which] = lax.fori_loop(0, nsub, per_sub, first) - first

    def start_all(copy, which, first, n, slot, counter):
        def body(i, carry):
            copy(list_ref[which * list_max + first + i], i, slot).start()
            return carry
        lax.fori_loop(0, n, body, 0)
        state_ref[counter] = n

    def wait_all(copy, block_copy, slot, counter):
        n = state_ref[counter]

        @pl.when(n == CHUNKS_PER_BLK)
        def _():
            block_copy(slot).wait()

        @pl.when(n < CHUNKS_PER_BLK)
        def _():
            def body(i, carry):
                copy(0, 0, slot).wait()
                return carry
            lax.fori_loop(0, n, body, 0)
        state_ref[counter] = 0

    def gather_block(slot):
        return pltpu.make_async_copy(s_hbm.at[pl.ds(0, CHUNKS_PER_BLK)], xbuf.at[slot], gsem.at[slot])

    def scatter_block(slot):
        return pltpu.make_async_copy(ybuf.at[slot], s_hbm.at[pl.ds(0, CHUNKS_PER_BLK)], ssem.at[slot])

    def block_chunks(total, b):
        return jnp.minimum(total - b * CHUNKS_PER_BLK, CHUNKS_PER_BLK)

    @pl.when(e == 0)
    def _():
        for i in range(6):
            state_ref[i] = 0
        xbuf[...] = jnp.zeros_like(xbuf)
        build_list(0, 0)
        n0 = state_ref[0]

        @pl.when(n0 > 0)
        def _():
            start_all(gather_copy, 0, 0, block_chunks(n0, 0), 0, 2)

    @pl.when(e + 1 < ne)
    def _():
        build_list(e + 1, 1 - par)

    total = state_ref[par]
    nblk = (total + CHUNKS_PER_BLK - 1) // CHUNKS_PER_BLK
    wgu_ref[:, 0:EXPERT_FF] = wg_ref[0].astype(BF16)
    wgu_ref[:, EXPERT_FF:] = wu_ref[0].astype(BF16)
    wdb_ref[...] = wd_ref[0].astype(BF16)
    ef = e.astype(F32)

    def block(b, carry):
        slot = b & 1
        first = b * CHUNKS_PER_BLK

        @pl.when(b + 1 < nblk)
        def _():
            start_all(gather_copy, par, first + CHUNKS_PER_BLK, block_chunks(total, b + 1), 1 - slot, 3 - slot)

        wait_all(gather_copy, gather_block, slot, 2 + slot)
        wait_all(scatter_copy, scatter_block, slot, 4 + slot)
        x = xbuf[slot].reshape(MBLK, ROW_W)
        info = x[:, D_MODEL:].astype(F32)
        g_first = info[:, 0:1] + info[:, 1:2] + info[:, 2:3]
        g_second = info[:, 3:4] + info[:, 4:5] + info[:, 5:6]
        gate = jnp.where(info[:, 6:7] == ef, g_first, g_second)
        h = jnp.dot(x[:, :D_MODEL], wgu_ref[...], preferred_element_type=F32)
        h1 = h[:, :EXPERT_FF]
        hid = (h1 * jax.nn.sigmoid(h1)) * h[:, EXPERT_FF:] * gate
        y = jnp.dot(hid.astype(BF16), wdb_ref[...], preferred_element_type=F32).astype(BF16)
        ybuf[slot] = jnp.concatenate([y, x[:, D_MODEL:]], axis=1).reshape(CHUNKS_PER_BLK, ROW_ALIGN, ROW_W)
        start_all(scatter_copy, par, first, block_chunks(total, b), slot, 4 + slot)
        return carry

    lax.fori_loop(0, nblk, block, 0)

    @pl.when(e + 1 < ne)
    def _():
        n1 = state_ref[1 - par]

        @pl.when(n1 > 0)
        def _():
            start_all(gather_copy, 1 - par, 0, block_chunks(n1, 0), 0, 2)

    @pl.when(e == ne - 1)
    def _():
        wait_all(scatter_copy, scatter_block, 0, 4)
        wait_all(scatter_copy, scatter_block, 1, 5)


def _experts(start, cnt, staged, wg, wu, wd):
    nsub = staged.shape[0] // CHUNKS_PER_SUB
    list_max = nsub * SUB // ROW_ALIGN + nsub + LIST_SLACK
    wblk = lambda e, *_: (e, 0, 0)
    grid_spec = pltpu.PrefetchScalarGridSpec(
        num_scalar_prefetch=2,
        grid=(N_EXPERTS,),
        in_specs=[
            pl.BlockSpec(memory_space=pl.ANY),
            pl.BlockSpec((1, D_MODEL, EXPERT_FF), wblk),
            pl.BlockSpec((1, D_MODEL, EXPERT_FF), wblk),
            pl.BlockSpec((1, EXPERT_FF, D_MODEL), wblk),
        ],
        out_specs=pl.BlockSpec(memory_space=pl.ANY),
        scratch_shapes=[
            pltpu.VMEM((2, CHUNKS_PER_BLK, ROW_ALIGN, ROW_W), BF16),
            pltpu.VMEM((2, CHUNKS_PER_BLK, ROW_ALIGN, ROW_W), BF16),
            pltpu.VMEM((D_MODEL, 2 * EXPERT_FF), BF16),
            pltpu.VMEM((EXPERT_FF, D_MODEL), BF16),
            pltpu.SemaphoreType.DMA((2,)),
            pltpu.SemaphoreType.DMA((2,)),
            pltpu.SMEM((2 * list_max,), jnp.int32),
            pltpu.SMEM((6,), jnp.int32),
        ],
    )
    return pl.pallas_call(
        functools.partial(_experts_body, nsub=nsub),
        grid_spec=grid_spec,
        out_shape=jax.ShapeDtypeStruct(staged.shape, staged.dtype),
        input_output_aliases={2: 0},
        compiler_params=_cparams(("arbitrary",)),
        name="moe_experts",
    )(start, cnt, staged, wg, wu, wd)


def _combine_body(ys_ref, pos_ref, x1_ref, p_ref, wp_ref, gp_ref, wpg_ref, gf_ref, y_ref):
    r = _iota_f32((SUB, SUBP), 1)
    moe = []
    for s in range(x1_ref.shape[0] // SUB):
        p1 = pos_ref[s * SUB:(s + 1) * SUB, 0:1]
        p2 = pos_ref[s * SUB:(s + 1) * SUB, 1:2]
        onehot = jnp.where(r == p1, 1.0, jnp.where(r == p2, 1.0, 0.0)).astype(BF16)
        ys = ys_ref[s * CHUNKS_PER_SUB:(s + 1) * CHUNKS_PER_SUB].reshape(SUBP, D_MODEL)
        moe.append(jnp.dot(onehot, ys, preferred_element_type=F32))
    x2 = x1_ref[...] + jnp.concatenate(moe, axis=0)
    ple = _rms(jnp.dot(p_ref[...].astype(BF16), wp_ref[...], preferred_element_type=F32), gp_ref[...])
    gate = jax.nn.sigmoid(jnp.dot(x2.astype(BF16), wpg_ref[...], preferred_element_type=F32))
    y_ref[...] = _rms(x2 + ple * gate, gf_ref[...])


def _combine(ys, pos, x1, p, w_ple, gp, w_ple_gate, gf, sub_off):
    t = x1.shape[0]
    tm = COMBINE_SUBS * SUB
    blk_off = sub_off // COMBINE_SUBS
    assert sub_off % COMBINE_SUBS == 0 and t % tm == 0
    row = lambda i: (i, 0)
    const = lambda i: (0, 0)
    return pl.pallas_call(
        _combine_body,
        grid=(t // tm,),
        in_specs=[
            pl.BlockSpec((COMBINE_SUBS * CHUNKS_PER_SUB, ROW_ALIGN, D_MODEL), lambda i: (i + blk_off, 0, 0)),
            pl.BlockSpec((tm, LANES), lambda i: (i + blk_off, 0)),
            pl.BlockSpec((tm, D_MODEL), row),
            pl.BlockSpec((tm, PLE_DIM), row),
            pl.BlockSpec((PLE_DIM, D_MODEL), const),
            pl.BlockSpec((1, D_MODEL), const),
            pl.BlockSpec((D_MODEL, D_MODEL), const),
            pl.BlockSpec((1, D_MODEL), const),
        ],
        out_specs=pl.BlockSpec((tm, D_MODEL), row),
        out_shape=jax.ShapeDtypeStruct((t, D_MODEL), F32),
        compiler_params=_cparams(("parallel",)),
        name="moe_combine_ple",
    )(ys, pos, x1, p, w_ple, gp, w_ple_gate, gf)


def _rope_tables(pos):
    half = HEAD_DIM // 2
    inv = ROPE_BASE ** (-jnp.arange(half, dtype=F32) / half)
    ang = pos[:, None] * inv[None, :]
    cos = jnp.cos(ang)
    sin = jnp.sin(ang)
    return jnp.concatenate([cos, cos], axis=-1), jnp.concatenate([-sin, sin], axis=-1)


def _router_params(we, be, wg, bg):
    pad = LANES - N_EXPERTS - N_GROUPS
    w = jnp.pad(jnp.concatenate([we, wg], axis=1), ((0, 0), (0, pad)))
    b = jnp.pad(jnp.concatenate([be, bg]), (0, pad))[None, :]
    return jnp.concatenate(_split3(w)[:2], axis=1), b


def kernel(x_prompt, x_sample, p_prompt, p_sample, state_conv, state_ret, w_in, conv_w, conv_b, conv_ln_g, conv_ln_b, w_out, norm1_g, norm2_g, router_group_w, router_group_b, router_expert_w, router_expert_b, w_expert_gate, w_expert_up, w_expert_down, w_ple, ple_norm_g, w_ple_gate, final_norm_g):
    assert w_in.shape[0] == 1, "single-layer trunk"
    nb, seq, _ = x_prompt.shape
    ns, dseq, _ = x_sample.shape
    tm = SAMPLE_TILE

    w_in_b = w_in[0].astype(BF16)
    w_out_b = w_out[0].astype(BF16)
    w_ple_b = w_ple[0].astype(BF16)
    w_pg_b = w_ple_gate[0].astype(BF16)
    g1 = norm1_g[0][None, :]
    g2 = norm2_g[0][None, :]
    gp = ple_norm_g[0][None, :]
    gf = final_norm_g[None, :]
    cb = conv_b[0][None, :]
    lng = conv_ln_g[0][None, :]
    lnb = conv_ln_b[0][None, :]
    wr2, br = _router_params(router_expert_w[0], router_expert_b[0], router_group_w[0], router_group_b[0])

    cos_p, sin_p = _rope_tables(jnp.arange(seq, dtype=F32) + jnp.float32(0))
    pos_s = jnp.tile(jnp.arange(dseq, dtype=F32) + jnp.float32(PAST_LEN), tm // dseq)
    cos_s, sin_s = _rope_tables(pos_s)

    xp = x_prompt.reshape(nb * seq, D_MODEL)
    x1_p, t_p, rec_p, conv_p, ret_p = _mix(xp, g1, w_in_b, cos_p, sin_p, conv_w[0], cb, lng, lnb, w_out_b, g2,
                                           wr2, br, nb, seq, MIX_TILE)

    xs = x_sample.reshape(ns * dseq, D_MODEL)
    u, q, k, v, gs = _inproj(xs, g1, w_in_b, cos_s, sin_s, tm)
    c, conv_s = _conv_sample(u.reshape(ns, dseq, CONV_CH), jnp.transpose(state_conv[0], (1, 0, 2)), conv_w[0], cb,
                             lng, lnb, CONV_SAMPLE_SEQS)
    conv_s = jnp.transpose(conv_s, (1, 0, 2))
    o, ret_s = _ret_sample(q, k, v, gs, state_ret[0], dseq, RET_SAMPLE_SEQS)
    x1_s, t_s, rec_s = _outproj(c.reshape(ns * dseq, CONV_CH), o, xs, w_out_b, g2, wr2, br, tm)

    staged, pos, meta = _dispatch(rec_p, t_p, rec_s, t_s)
    start = meta[:, 0, :N_EXPERTS].astype(jnp.int32).reshape(-1)
    cnt = meta[:, 1, :N_EXPERTS].astype(jnp.int32).reshape(-1)
    ys = _experts(start, cnt, staged, w_expert_gate[0], w_expert_up[0], w_expert_down[0])

    y_p = _combine(ys, pos, x1_p, p_prompt[0].reshape(nb * seq, PLE_DIM), w_ple_b, gp, w_pg_b, gf, 0)
    y_s = _combine(ys, pos, x1_s, p_sample[0].reshape(ns * dseq, PLE_DIM), w_ple_b, gp, w_pg_b, gf,
                   nb * seq // SUB)

    return (y_p.reshape(nb, seq, D_MODEL), y_s.reshape(ns, dseq, D_MODEL),
            conv_p[None], ret_p[None], conv_s[None], ret_s[None])
```

```python
import functools

import jax
import jax.numpy as jnp
from jax import lax
from jax.experimental import pallas as pl
from jax.experimental.pallas import tpu as pltpu

F32 = jnp.float32
BF16 = jnp.bfloat16

D_MODEL = 1024
PLE_DIM = 256
CONV_CH = 512
CONV_K = 31
RET_WIDTH = 512
RET_HEADS = 4
HEAD_DIM = 128
CHUNK = 128
ROPE_BASE = 10000.0
N_GROUPS = 4
EXPERTS_PER_GROUP = 8
N_EXPERTS = 32
EXPERT_FF = 256
IN_COLS = 3072
EPS = 1e-6
PAST_LEN = 16384

LANES = 128
SUBLANES = 8
HALO = 32
HALO_OFF = HALO - (CONV_K - 1)
VMEM_LIMIT = 48 * 1024 * 1024
MIX_VMEM_LIMIT = 56 * 1024 * 1024
MIX_TILE = 512
SAMPLE_TILE = 1024
CONV_SAMPLE_SEQS = 32
RET_SAMPLE_SEQS = 32

SUB = 256
ROW_ALIGN = 16
PBLK = 256
SUBP = -(-(2 * SUB + N_EXPERTS * (ROW_ALIGN - 1)) // PBLK) * PBLK
CHUNKS_PER_SUB = SUBP // ROW_ALIGN
ROW_W = D_MODEL + LANES
MBLK = 512
CHUNKS_PER_BLK = MBLK // ROW_ALIGN
LIST_SLACK = 2
COMBINE_SUBS = 4
DISPATCH_SUBS = 4


def _cparams(sem):
    return pltpu.CompilerParams(dimension_semantics=sem, vmem_limit_bytes=VMEM_LIMIT)


def _rms(x, g):
    return x * lax.rsqrt(jnp.mean(x * x, axis=-1, keepdims=True) + EPS) * g


def _inproj_body(x_ref, g1_ref, w_ref, cos_ref, sin_ref, u_ref, q_ref, k_ref, v_ref, gs_ref):
    h = _rms(x_ref[...], g1_ref[...]).astype(BF16)
    z = jnp.dot(h, w_ref[...], preferred_element_type=F32)
    a = z[:, :CONV_CH]
    b = z[:, CONV_CH:2 * CONV_CH]
    u_ref[...] = a * jax.nn.sigmoid(b)
    cos = cos_ref[...]
    sin = sin_ref[...]
    q0 = 2 * CONV_CH
    k0 = q0 + RET_WIDTH
    for hh in range(RET_HEADS):
        sl = slice(hh * HEAD_DIM, (hh + 1) * HEAD_DIM)
        qh = z[:, q0 + hh * HEAD_DIM:q0 + (hh + 1) * HEAD_DIM]
        kh = z[:, k0 + hh * HEAD_DIM:k0 + (hh + 1) * HEAD_DIM]
        q_ref[:, sl] = qh * cos + pltpu.roll(qh, HEAD_DIM // 2, 1) * sin
        k_ref[:, sl] = (kh * cos + pltpu.roll(kh, HEAD_DIM // 2, 1) * sin) * (HEAD_DIM ** -0.5)
    v_ref[...] = z[:, k0 + RET_WIDTH:k0 + 2 * RET_WIDTH]
    g = z[:, k0 + 2 * RET_WIDTH:]
    gs_ref[...] = g * jax.nn.sigmoid(g)


def _inproj(x, g1, w_in, cos, sin, tm):
    t = x.shape[0]
    row = lambda i: (i, 0)
    const = lambda i: (0, 0)
    return pl.pallas_call(
        _inproj_body,
        grid=(t // tm,),
        in_specs=[
            pl.BlockSpec((tm, D_MODEL), row),
            pl.BlockSpec((1, D_MODEL), const),
            pl.BlockSpec((D_MODEL, IN_COLS), const),
            pl.BlockSpec((tm, HEAD_DIM), const),
            pl.BlockSpec((tm, HEAD_DIM), const),
        ],
        out_specs=[pl.BlockSpec((tm, CONV_CH), row)] + [pl.BlockSpec((tm, RET_WIDTH), row)] * 4,
        out_shape=[jax.ShapeDtypeStruct((t, CONV_CH), F32)] + [jax.ShapeDtypeStruct((t, RET_WIDTH), F32)] * 4,
        compiler_params=_cparams(("parallel",)),
        name="inproj",
    )(x, g1, w_in, cos, sin)


def _ln_silu(acc, g, b):
    mu = jnp.mean(acc, axis=-1, keepdims=True)
    d = acc - mu
    var = jnp.mean(d * d, axis=-1, keepdims=True)
    y = d * lax.rsqrt(var + EPS) * g + b
    return y * jax.nn.sigmoid(y)


def _dwconv(load, w_ref, rows):
    acc = None
    for b in range(SUBLANES):
        part = None
        for a in range((CONV_K + HALO_OFF) // SUBLANES + 1):
            k = SUBLANES * a + b - HALO_OFF
            if 0 <= k < CONV_K:
                term = load(SUBLANES * a, rows + SUBLANES) * w_ref[k:k + 1, :]
                part = term if part is None else part + term
        if part is not None:
            shifted = part[b:b + rows]
            acc = shifted if acc is None else acc + shifted
    return acc


def _conv_sample_body(u_ref, st_ref, w_ref, cb_ref, lg_ref, lb_ref, c_ref, nst_ref, ext_ref):
    l = u_ref.shape[1]
    hist = CONV_K - 1
    ext_ref[0:hist] = st_ref[...]
    ext_ref[hist:] = jnp.transpose(u_ref[...], (1, 0, 2))
    acc = ext_ref[0:l] * w_ref[0:1, :] + cb_ref[...]
    for k in range(1, CONV_K):
        acc = acc + ext_ref[k:k + l] * w_ref[k:k + 1, :]
    y = _ln_silu(acc, lg_ref[...], lb_ref[...])
    c_ref[...] = jnp.transpose(y, (1, 0, 2)).astype(c_ref.dtype)
    nst_ref[...] = ext_ref[l:]


def _conv_sample(u, state_t, conv_w, conv_b, ln_g, ln_b, nb):
    n, l, _ = u.shape
    hist = CONV_K - 1
    const = lambda b: (0, 0)
    tok = lambda b: (b, 0, 0)
    tmaj = lambda b: (0, b, 0)
    return pl.pallas_call(
        _conv_sample_body,
        grid=(n // nb,),
        in_specs=[
            pl.BlockSpec((nb, l, CONV_CH), tok),
            pl.BlockSpec((hist, nb, CONV_CH), tmaj),
            pl.BlockSpec((CONV_K, CONV_CH), const),
            pl.BlockSpec((1, CONV_CH), const),
            pl.BlockSpec((1, CONV_CH), const),
            pl.BlockSpec((1, CONV_CH), const),
        ],
        out_specs=[
            pl.BlockSpec((nb, l, CONV_CH), tok),
            pl.BlockSpec((hist, nb, CONV_CH), tmaj),
        ],
        out_shape=[
            jax.ShapeDtypeStruct((n, l, CONV_CH), BF16),
            jax.ShapeDtypeStruct((hist, n, CONV_CH), F32),
        ],
        scratch_shapes=[pltpu.VMEM((hist + l, nb, CONV_CH), F32)],
        compiler_params=_cparams(("parallel",)),
        name="conv_sample",
    )(u, state_t, conv_w, conv_b, ln_g, ln_b)


def _decay_tables(c):
    lg = jnp.log(1.0 - 2.0 ** (-5.0 - jnp.arange(RET_HEADS, dtype=F32)))
    idx = jnp.arange(c, dtype=F32)
    rel = idx[:, None] - idx[None, :]
    dmat = jnp.where(rel[None] >= 0, jnp.exp(jnp.maximum(rel, 0.0)[None] * lg[:, None, None]), 0.0)
    xi = jnp.exp((idx + 1.0)[None, :] * lg[:, None])
    zeta = jnp.exp((c - 1.0 - idx)[None, :] * lg[:, None])
    gc = jnp.exp(c * lg)
    xi_b = jnp.broadcast_to(xi[:, :, None], (RET_HEADS, c, HEAD_DIM))
    zeta_b = jnp.broadcast_to(zeta[:, :, None], (RET_HEADS, c, HEAD_DIM))
    gc_b = jnp.broadcast_to(gc[:, None, None], (RET_HEADS, 1, HEAD_DIM))
    return dmat, xi_b, zeta_b, gc_b


def _group_norm(o):
    mu = jnp.mean(o, axis=-1, keepdims=True)
    d = o - mu
    var = jnp.mean(d * d, axis=-1, keepdims=True)
    return d * lax.rsqrt(var + EPS)


def _ret_chunk(qh, kh, vh, r, dmat, xi, zeta, gc):
    qb = qh.astype(BF16)
    kb = kh.astype(BF16)
    vb = vh.astype(BF16)
    s = lax.dot_general(qb, kb, (((1,), (1,)), ((), ())), preferred_element_type=F32) * dmat
    o = jnp.dot(s.astype(BF16), vb, preferred_element_type=F32)
    o = o + jnp.dot(qb, r.astype(BF16), preferred_element_type=F32) * xi
    kz = (kh.astype(F32) * zeta).astype(BF16)
    r_new = r * gc + lax.dot_general(kz, vb, (((0,), (0,)), ((), ())), preferred_element_type=F32)
    return o, r_new


def _ret_sample_body(q_ref, k_ref, v_ref, gs_ref, st_ref, d_ref, xi_ref, zeta_ref, gc_ref, o_ref, nst_ref):
    nb = st_ref.shape[0]
    l = q_ref.shape[0] // nb
    for b in range(nb):
        rows = slice(b * l, (b + 1) * l)
        for hh in range(RET_HEADS):
            sl = slice(hh * HEAD_DIM, (hh + 1) * HEAD_DIM)
            o, r = _ret_chunk(q_ref[rows, sl], k_ref[rows, sl], v_ref[rows, sl], st_ref[b, hh],
                              d_ref[hh], xi_ref[hh], zeta_ref[hh], gc_ref[hh])
            o_ref[rows, sl] = (gs_ref[rows, sl] * _group_norm(o)).astype(o_ref.dtype)
            nst_ref[b, hh] = r


def _ret_sample(q, k, v, gs, state, l, nb):
    n = state.shape[0]
    dmat, xi, zeta, gc = _decay_tables(l)
    row = lambda b: (b, 0)
    c3 = lambda b: (0, 0, 0)
    blk4 = lambda b: (b, 0, 0, 0)
    return pl.pallas_call(
        _ret_sample_body,
        grid=(n // nb,),
        in_specs=[pl.BlockSpec((nb * l, RET_WIDTH), row)] * 4 + [
            pl.BlockSpec((nb, RET_HEADS, HEAD_DIM, HEAD_DIM), blk4),
            pl.BlockSpec((RET_HEADS, l, l), c3),
            pl.BlockSpec((RET_HEADS, l, HEAD_DIM), c3),
            pl.BlockSpec((RET_HEADS, l, HEAD_DIM), c3),
            pl.BlockSpec((RET_HEADS, 1, HEAD_DIM), c3),
        ],
        out_specs=[
            pl.BlockSpec((nb * l, RET_WIDTH), row),
            pl.BlockSpec((nb, RET_HEADS, HEAD_DIM, HEAD_DIM), blk4),
        ],
        out_shape=[
            jax.ShapeDtypeStruct((n * l, RET_WIDTH), BF16),
            jax.ShapeDtypeStruct((n, RET_HEADS, HEAD_DIM, HEAD_DIM), F32),
        ],
        compiler_params=_cparams(("parallel",)),
        name="ret_sample",
    )(q, k, v, gs, state, dmat, xi, zeta, gc)


def _split3(x):
    hi = x.astype(BF16)
    r1 = x - hi.astype(F32)
    mid = r1.astype(BF16)
    lo = (r1 - mid.astype(F32)).astype(BF16)
    return hi, mid, lo


def _dot_hp(t, w2_ref):
    t_hi, t_mid, _ = _split3(t)
    d = functools.partial(jnp.dot, preferred_element_type=F32)
    both = d(t_hi, w2_ref[...])
    return both[:, :LANES] + (d(t_mid, w2_ref[:, 0:LANES]) + both[:, LANES:])


def _iota_f32(shape, dim):
    return lax.broadcasted_iota(jnp.int32, shape, dim).astype(F32)


def _route(logits):
    lt = logits.T
    tm = lt.shape[1]
    row = _iota_f32((SUBLANES, tm), 0)
    big = float(SUBLANES)

    def rmax(x):
        return jnp.max(x, axis=0, keepdims=True)

    def first_row(mask):
        return jnp.min(jnp.where(mask, row, big), axis=0, keepdims=True)

    lg = jnp.where(row < float(N_GROUPS), lt[N_EXPERTS:N_EXPERTS + SUBLANES, :], -1e30)
    m = rmax(lg)
    g_top = 1.0 / jnp.sum(jnp.exp(lg - m), axis=0, keepdims=True)
    g_idx = first_row(lg == m)
    lem = lt[0:EXPERTS_PER_GROUP, :]
    for g in range(1, N_GROUPS):
        lem = jnp.where(g_idx == float(g), lt[g * EXPERTS_PER_GROUP:(g + 1) * EXPERTS_PER_GROUP, :], lem)
    pe = jnp.exp(lem - rmax(lem))
    p1 = rmax(pe)
    e1 = first_row(pe == p1)
    rest = row != e1
    pe2 = jnp.where(rest, pe, -1.0)
    p2 = rmax(pe2)
    e2 = first_row(rest & (pe2 == p2))
    scale = g_top / (p1 + p2)
    base = g_idx * float(EXPERTS_PER_GROUP)
    rec_t = jnp.where(row == 0.0, base + e1, jnp.where(row == 1.0, base + e2, 0.0))
    rec_t = rec_t + jnp.where(row == 2.0, p1 * scale, jnp.where(row == 3.0, p2 * scale, 0.0))
    return jnp.concatenate([rec_t, jnp.zeros((LANES - SUBLANES, tm), F32)], axis=0).T


def _outproj_body(c_ref, o_ref, x_ref, wo_ref, g2_ref, wr_ref, br_ref, x1_ref, t_ref, rec_ref):
    x1 = x_ref[...] + jnp.dot(c_ref[...], wo_ref[0:CONV_CH, :], preferred_element_type=F32)
    x1 = x1 + jnp.dot(o_ref[...], wo_ref[CONV_CH:, :], preferred_element_type=F32)
    x1_ref[...] = x1
    t = _rms(x1, g2_ref[...])
    t_ref[...] = t.astype(t_ref.dtype)
    rec_ref[...] = _route(_dot_hp(t, wr_ref) + br_ref[...])


def _outproj(c, o, x, w_out, g2, wr2, br, tm):
    t = x.shape[0]
    row = lambda i: (i, 0)
    const = lambda i: (0, 0)
    return pl.pallas_call(
        _outproj_body,
        grid=(t // tm,),
        in_specs=[
            pl.BlockSpec((tm, CONV_CH), row),
            pl.BlockSpec((tm, RET_WIDTH), row),
            pl.BlockSpec((tm, D_MODEL), row),
            pl.BlockSpec((D_MODEL, D_MODEL), const),
            pl.BlockSpec((1, D_MODEL), const),
            pl.BlockSpec((D_MODEL, 2 * LANES), const),
            pl.BlockSpec((1, LANES), const),
        ],
        out_specs=[
            pl.BlockSpec((tm, D_MODEL), row),
            pl.BlockSpec((tm, D_MODEL), row),
            pl.BlockSpec((tm, LANES), row),
        ],
        out_shape=[
            jax.ShapeDtypeStruct((t, D_MODEL), F32),
            jax.ShapeDtypeStruct((t, D_MODEL), BF16),
            jax.ShapeDtypeStruct((t, LANES), F32),
        ],
        compiler_params=_cparams(("parallel",)),
        name="outproj_router",
    )(c, o, x, w_out, g2, wr2, br)


def _mix_body(x_ref, g1_ref, w_ref, cos_ref, sin_ref, cw_ref, cb_ref, lg_ref, lb_ref,
              d_ref, xi_ref, zeta_ref, gc_ref, wo_ref, g2_ref, wr_ref, br_ref,
              x1_ref, t_ref, rec_ref, cst_ref, rst_ref, ext_ref, r_ref, o_ref):
    j = pl.program_id(1)
    tl = x_ref.shape[0]
    x = x_ref[...]
    z = jnp.dot(_rms(x, g1_ref[...]).astype(BF16), w_ref[...], preferred_element_type=F32)

    @pl.when(j == 0)
    def _():
        ext_ref[0:HALO, :] = jnp.zeros((HALO, CONV_CH), F32)
        ext_ref[tl + HALO:, :] = jnp.zeros((SUBLANES, CONV_CH), F32)
        r_ref[...] = jnp.zeros_like(r_ref)

    @pl.when(j > 0)
    def _():
        ext_ref[0:HALO, :] = ext_ref[tl:tl + HALO, :]

    ext_ref[HALO:tl + HALO, :] = z[:, :CONV_CH] * jax.nn.sigmoid(z[:, CONV_CH:2 * CONV_CH])
    acc = _dwconv(lambda s, n: ext_ref[s:s + n, :], cw_ref, tl) + cb_ref[...]
    o_ref[:, 0:CONV_CH] = _ln_silu(acc, lg_ref[...], lb_ref[...]).astype(BF16)
    cst_ref[0] = ext_ref[tl + HALO_OFF:tl + HALO, :]

    cos = cos_ref[...]
    sin = sin_ref[...]
    q0 = 2 * CONV_CH
    k0 = q0 + RET_WIDTH
    v0 = k0 + RET_WIDTH
    g0 = v0 + RET_WIDTH
    for hh in range(RET_HEADS):
        lo = hh * HEAD_DIM
        qh = z[:, q0 + lo:q0 + lo + HEAD_DIM]
        kh = z[:, k0 + lo:k0 + lo + HEAD_DIM]
        qr = (qh * cos + pltpu.roll(qh, HEAD_DIM // 2, 1) * sin).astype(BF16)
        kr = ((kh * cos + pltpu.roll(kh, HEAD_DIM // 2, 1) * sin) * (HEAD_DIM ** -0.5)).astype(BF16)
        vh = z[:, v0 + lo:v0 + lo + HEAD_DIM].astype(BF16)
        g = z[:, g0 + lo:g0 + lo + HEAD_DIM]
        gs = g * jax.nn.sigmoid(g)
        r = r_ref[hh]
        for ci in range(tl // CHUNK):
            rows = slice(ci * CHUNK, (ci + 1) * CHUNK)
            o, r = _ret_chunk(qr[rows], kr[rows], vh[rows], r, d_ref[hh], xi_ref[hh], zeta_ref[hh], gc_ref[hh])
            o_ref[rows, CONV_CH + lo:CONV_CH + lo + HEAD_DIM] = (gs[rows] * _group_norm(o)).astype(BF16)
        r_ref[hh] = r
    rst_ref[0] = r_ref[...]

    x1 = x + jnp.dot(o_ref[...], wo_ref[...], preferred_element_type=F32)
    x1_ref[...] = x1
    t = _rms(x1, g2_ref[...])
    t_ref[...] = t.astype(t_ref.dtype)
    rec_ref[...] = _route(_dot_hp(t, wr_ref) + br_ref[...])


def _mix(x, g1, w_in, cos, sin, conv_w, conv_b, ln_g, ln_b, w_out, g2, wr2, br, n, l, tl):
    dmat, xi, zeta, gc = _decay_tables(CHUNK)
    per = l // tl
    row = lambda b, j: (b * per + j, 0)
    tab = lambda b, j: (j, 0)
    const = lambda b, j: (0, 0)
    c3 = lambda b, j: (0, 0, 0)
    once = dict(pipeline_mode=pl.Buffered(1))
    return pl.pallas_call(
        _mix_body,
        grid=(n, per),
        in_specs=[
            pl.BlockSpec((tl, D_MODEL), row),
            pl.BlockSpec((1, D_MODEL), const),
            pl.BlockSpec((D_MODEL, IN_COLS), const, **once),
            pl.BlockSpec((tl, HEAD_DIM), tab),
            pl.BlockSpec((tl, HEAD_DIM), tab),
            pl.BlockSpec((CONV_K, CONV_CH), const),
            pl.BlockSpec((1, CONV_CH), const),
            pl.BlockSpec((1, CONV_CH), const),
            pl.BlockSpec((1, CONV_CH), const),
            pl.BlockSpec((RET_HEADS, CHUNK, CHUNK), c3),
            pl.BlockSpec((RET_HEADS, CHUNK, HEAD_DIM), c3),
            pl.BlockSpec((RET_HEADS, CHUNK, HEAD_DIM), c3),
            pl.BlockSpec((RET_HEADS, 1, HEAD_DIM), c3),
            pl.BlockSpec((D_MODEL, D_MODEL), const, **once),
            pl.BlockSpec((1, D_MODEL), const),
            pl.BlockSpec((D_MODEL, 2 * LANES), const, **once),
            pl.BlockSpec((1, LANES), const),
        ],
        out_specs=[
            pl.BlockSpec((tl, D_MODEL), row),
            pl.BlockSpec((tl, D_MODEL), row),
            pl.BlockSpec((tl, LANES), row),
            pl.BlockSpec((1, CONV_K - 1, CONV_CH), lambda b, j: (b, 0, 0)),
            pl.BlockSpec((1, RET_HEADS, HEAD_DIM, HEAD_DIM), lambda b, j: (b, 0, 0, 0)),
        ],
        out_shape=[
            jax.ShapeDtypeStruct((n * l, D_MODEL), F32),
            jax.ShapeDtypeStruct((n * l, D_MODEL), BF16),
            jax.ShapeDtypeStruct((n * l, LANES), F32),
            jax.ShapeDtypeStruct((n, CONV_K - 1, CONV_CH), F32),
            jax.ShapeDtypeStruct((n, RET_HEADS, HEAD_DIM, HEAD_DIM), F32),
        ],
        scratch_shapes=[
            pltpu.VMEM((tl + HALO + SUBLANES, CONV_CH), F32),
            pltpu.VMEM((RET_HEADS, HEAD_DIM, HEAD_DIM), F32),
            pltpu.VMEM((tl, D_MODEL), BF16),
        ],
        compiler_params=pltpu.CompilerParams(dimension_semantics=("arbitrary", "arbitrary"),
                                             vmem_limit_bytes=MIX_VMEM_LIMIT),
        name="token_mix",
    )(x, g1, w_in, cos, sin, conv_w, conv_b, ln_g, ln_b, dmat, xi, zeta, gc, w_out, g2, wr2, br)


def _dispatch_body(rec_a_ref, t_a_ref, rec_b_ref, t_b_ref, s_ref, pos_ref, meta_ref, *, nsub_a):
    from_a = pl.program_id(0) * DISPATCH_SUBS < nsub_a
    for s in range(DISPATCH_SUBS):
        rows = slice(s * SUB, (s + 1) * SUB)
        rec = jnp.where(from_a, rec_a_ref[rows, :], rec_b_ref[rows, :])
        tok = jnp.where(from_a, t_a_ref[rows, :], t_b_ref[rows, :])
        chunks = slice(s * CHUNKS_PER_SUB, (s + 1) * CHUNKS_PER_SUB)
        _dispatch_sub_tile(rec, tok, s_ref.at[chunks], pos_ref.at[rows], meta_ref.at[s])


def _dispatch_sub_tile(rec, tok, s_ref, pos_ref, meta_ref):
    lane = _iota_f32(rec.shape, 1)
    a1 = lane == rec[:, 0:1]
    a2 = lane == rec[:, 1:2]
    a1f = jnp.where(a1, 1.0, 0.0)
    a2f = jnp.where(a2, 1.0, 0.0)
    ltri = jnp.where(_iota_f32((SUB, SUB), 1) < _iota_f32((SUB, SUB), 0), 1.0, 0.0).astype(BF16)
    c1 = jnp.dot(ltri, a1f.astype(BF16), preferred_element_type=F32)
    c2 = jnp.dot(ltri, a2f.astype(BF16), preferred_element_type=F32)
    n1 = jnp.sum(a1f, axis=0, keepdims=True)
    n2 = jnp.sum(a2f, axis=0, keepdims=True)
    cnt = jnp.floor((n1 + n2 + (ROW_ALIGN - 1.0)) * (1.0 / ROW_ALIGN))
    utri = jnp.where(_iota_f32((LANES, LANES), 0) < _iota_f32((LANES, LANES), 1), 1.0, 0.0).astype(BF16)
    start = jnp.dot(jnp.broadcast_to(cnt, (SUBLANES, LANES)).astype(BF16), utri,
                    preferred_element_type=F32)[0:1]
    base1 = start * ROW_ALIGN
    base2 = base1 + n1
    pos1 = jnp.sum(jnp.where(a1, c1 + base1, 0.0), axis=1, keepdims=True)
    pos2 = jnp.sum(jnp.where(a2, c2 + base2, 0.0), axis=1, keepdims=True)
    posm = jnp.where(lane == 0.0, pos1, jnp.where(lane == 1.0, pos2, 0.0))
    pos_ref[...] = posm
    row = _iota_f32((SUBLANES, LANES), 0)
    meta_ref[...] = jnp.where(row == 0.0, start, jnp.where(row == 1.0, cnt, 0.0))

    g1 = _split3(rec[:, 2:3])
    g2 = _split3(rec[:, 3:4])
    info = jnp.where(lane == 6.0, rec[:, 0:1], jnp.where(lane == 7.0, rec[:, 1:2], 0.0))
    for i in range(3):
        info = jnp.where(lane == float(i), g1[i].astype(F32), info)
        info = jnp.where(lane == float(3 + i), g2[i].astype(F32), info)
    src = jnp.concatenate([tok, info.astype(BF16)], axis=1)

    post = posm.T
    r = _iota_f32((SUBP, SUB), 0)
    onehot = jnp.where(r == post[0:1, :], 1.0, jnp.where(r == post[1:2, :], 1.0, 0.0)).astype(BF16)
    sorted_rows = jnp.dot(onehot, src, preferred_element_type=F32).astype(BF16)
    s_ref[...] = sorted_rows.reshape(CHUNKS_PER_SUB, ROW_ALIGN, ROW_W)


def _dispatch(rec_a, t_a, rec_b, t_b):
    nsub_a = rec_a.shape[0] // SUB
    nsub_b = rec_b.shape[0] // SUB
    nsub = nsub_a + nsub_b
    assert nsub_a % DISPATCH_SUBS == 0 and nsub_b % DISPATCH_SUBS == 0
    steps_a = nsub_a // DISPATCH_SUBS
    tm = DISPATCH_SUBS * SUB
    row = lambda i: (i, 0)
    row_a = lambda i: (jnp.minimum(i, steps_a - 1), 0)
    row_b = lambda i: (jnp.maximum(i - steps_a, 0), 0)
    return pl.pallas_call(
        functools.partial(_dispatch_body, nsub_a=nsub_a),
        grid=(nsub // DISPATCH_SUBS,),
        in_specs=[
            pl.BlockSpec((tm, LANES), row_a),
            pl.BlockSpec((tm, D_MODEL), row_a),
            pl.BlockSpec((tm, LANES), row_b),
            pl.BlockSpec((tm, D_MODEL), row_b),
        ],
        out_specs=[
            pl.BlockSpec((DISPATCH_SUBS * CHUNKS_PER_SUB, ROW_ALIGN, ROW_W), lambda i: (i, 0, 0)),
            pl.BlockSpec((tm, LANES), row),
            pl.BlockSpec((DISPATCH_SUBS, SUBLANES, LANES), lambda i: (i, 0, 0)),
        ],
        out_shape=[
            jax.ShapeDtypeStruct((nsub * CHUNKS_PER_SUB, ROW_ALIGN, ROW_W), BF16),
            jax.ShapeDtypeStruct((nsub * SUB, LANES), F32),
            jax.ShapeDtypeStruct((nsub, SUBLANES, LANES), F32),
        ],
        compiler_params=_cparams(("parallel",)),
        name="moe_dispatch",
    )(rec_a, t_a, rec_b, t_b)


def _experts_body(start_ref, cnt_ref, s_in, wg_ref, wu_ref, wd_ref, s_hbm,
                  xbuf, ybuf, wgu_ref, wdb_ref, gsem, ssem, list_ref, state_ref, *, nsub):
    del s_in
    e = pl.program_id(0)
    ne = pl.num_programs(0)
    par = e & 1

    list_max = list_ref.shape[0] // 2

    def gather_copy(src, i, slot):
        return pltpu.make_async_copy(s_hbm.at[src], xbuf.at[slot, i], gsem.at[slot])

    def scatter_copy(dst, i, slot):
        return pltpu.make_async_copy(ybuf.at[slot, i], s_hbm.at[dst], ssem.at[slot])

    def build_list(x, which):
        def per_sub(s, k):
            run = s * N_EXPERTS + x
            c = cnt_ref[run]
            base = s * CHUNKS_PER_SUB + start_ref[run]
            list_ref[k] = base
            list_ref[k + 1] = base + 1

            def per_chunk(i, carry):
                list_ref[k + i] = base + i
                return carry
            lax.fori_loop(2, c, per_chunk, 0)
            return k + c
        first = which * list_max
        state_ref[which] = lax.fori_loop(0, nsub, per_sub, first) - first

    def start_all(copy, which, first, n, slot, counter):
        def body(i, carry):
            copy(list_ref[which * list_max + first + i], i, slot).start()
            return carry
        lax.fori_loop(0, n, body, 0)
        state_ref[counter] = n

    def wait_all(copy, block_copy, slot, counter):
        n = state_ref[counter]

        @pl.when(n == CHUNKS_PER_BLK)
        def _():
            block_copy(slot).wait()

        @pl.when(n < CHUNKS_PER_BLK)
        def _():
            def body(i, carry):
                copy(0, 0, slot).wait()
                return carry
            lax.fori_loop(0, n, body, 0)
        state_ref[counter] = 0

    def gather_block(slot):
        return pltpu.make_async_copy(s_hbm.at[pl.ds(0, CHUNKS_PER_BLK)], xbuf.at[slot], gsem.at[slot])

    def scatter_block(slot):
        return pltpu.make_async_copy(ybuf.at[slot], s_hbm.at[pl.ds(0, CHUNKS_PER_BLK)], ssem.at[slot])

    def block_chunks(total, b):
        return jnp.minimum(total - b * CHUNKS_PER_BLK, CHUNKS_PER_BLK)

    @pl.when(e == 0)
    def _():
        for i in range(6):
            state_ref[i] = 0
        xbuf[...] = jnp.zeros_like(xbuf)
        build_list(0, 0)
        n0 = state_ref[0]

        @pl.when(n0 > 0)
        def _():
            start_all(gather_copy, 0, 0, block_chunks(n0, 0), 0, 2)

    @pl.when(e + 1 < ne)
    def _():
        build_list(e + 1, 1 - par)

    total = state_ref[par]
    nblk = (total + CHUNKS_PER_BLK - 1) // CHUNKS_PER_BLK
    wgu_ref[:, 0:EXPERT_FF] = wg_ref[0].astype(BF16)
    wgu_ref[:, EXPERT_FF:] = wu_ref[0].astype(BF16)
    wdb_ref[...] = wd_ref[0].astype(BF16)
    ef = e.astype(F32)

    def block(b, carry):
        slot = b & 1
        first = b * CHUNKS_PER_BLK

        @pl.when(b + 1 < nblk)
        def _():
            start_all(gather_copy, par, first + CHUNKS_PER_BLK, block_chunks(total, b + 1), 1 - slot, 3 - slot)

        wait_all(gather_copy, gather_block, slot, 2 + slot)
        wait_all(scatter_copy, scatter_block, slot, 4 + slot)
        x = xbuf[slot].reshape(MBLK, ROW_W)
        info = x[:, D_MODEL:].astype(F32)
        g_first = info[:, 0:1] + info[:, 1:2] + info[:, 2:3]
        g_second = info[:, 3:4] + info[:, 4:5] + info[:, 5:6]
        gate = jnp.where(info[:, 6:7] == ef, g_first, g_second)
        h = jnp.dot(x[:, :D_MODEL], wgu_ref[...], preferred_element_type=F32)
        h1 = h[:, :EXPERT_FF]
        hid = (h1 * jax.nn.sigmoid(h1)) * h[:, EXPERT_FF:] * gate
        y = jnp.dot(hid.astype(BF16), wdb_ref[...], preferred_element_type=F32).astype(BF16)
        ybuf[slot] = jnp.concatenate([y, x[:, D_MODEL:]], axis=1).reshape(CHUNKS_PER_BLK, ROW_ALIGN, ROW_W)
        start_all(scatter_copy, par, first, block_chunks(total, b), slot, 4 + slot)
        return carry

    lax.fori_loop(0, nblk, block, 0)

    @pl.when(e + 1 < ne)
    def _():
        n1 = state_ref[1 - par]

        @pl.when(n1 > 0)
        def _():
            start_all(gather_copy, 1 - par, 0, block_chunks(n1, 0), 0, 2)

    @pl.when(e == ne - 1)
    def _():
        wait_all(scatter_copy, scatter_block, 0, 4)
        wait_all(scatter_copy, scatter_block, 1, 5)


def _experts(start, cnt, staged, wg, wu, wd):
    nsub = staged.shape[0] // CHUNKS_PER_SUB
    list_max = nsub * SUB // ROW_ALIGN + nsub + LIST_SLACK
    wblk = lambda e, *_: (e, 0, 0)
    grid_spec = pltpu.PrefetchScalarGridSpec(
        num_scalar_prefetch=2,
        grid=(N_EXPERTS,),
        in_specs=[
            pl.BlockSpec(memory_space=pl.ANY),
            pl.BlockSpec((1, D_MODEL, EXPERT_FF), wblk),
            pl.BlockSpec((1, D_MODEL, EXPERT_FF), wblk),
            pl.BlockSpec((1, EXPERT_FF, D_MODEL), wblk),
        ],
        out_specs=pl.BlockSpec(memory_space=pl.ANY),
        scratch_shapes=[
            pltpu.VMEM((2, CHUNKS_PER_BLK, ROW_ALIGN, ROW_W), BF16),
            pltpu.VMEM((2, CHUNKS_PER_BLK, ROW_ALIGN, ROW_W), BF16),
            pltpu.VMEM((D_MODEL, 2 * EXPERT_FF), BF16),
            pltpu.VMEM((EXPERT_FF, D_MODEL), BF16),
            pltpu.SemaphoreType.DMA((2,)),
            pltpu.SemaphoreType.DMA((2,)),
            pltpu.SMEM((2 * list_max,), jnp.int32),
            pltpu.SMEM((6,), jnp.int32),
        ],
    )
    return pl.pallas_call(
        functools.partial(_experts_body, nsub=nsub),
        grid_spec=grid_spec,
        out_shape=jax.ShapeDtypeStruct(staged.shape, staged.dtype),
        input_output_aliases={2: 0},
        compiler_params=_cparams(("arbitrary",)),
        name="moe_experts",
    )(start, cnt, staged, wg, wu, wd)


def _combine_body(ys_ref, pos_ref, x1_ref, p_ref, wp_ref, gp_ref, wpg_ref, gf_ref, y_ref):
    r = _iota_f32((SUB, SUBP), 1)
    moe = []
    for s in range(x1_ref.shape[0] // SUB):
        p1 = pos_ref[s * SUB:(s + 1) * SUB, 0:1]
        p2 = pos_ref[s * SUB:(s + 1) * SUB, 1:2]
        onehot = jnp.where(r == p1, 1.0, jnp.where(r == p2, 1.0, 0.0)).astype(BF16)
        ys = ys_ref[s * CHUNKS_PER_SUB:(s + 1) * CHUNKS_PER_SUB].reshape(SUBP, D_MODEL)
        moe.append(jnp.dot(onehot, ys, preferred_element_type=F32))
    x2 = x1_ref[...] + jnp.concatenate(moe, axis=0)
    ple = _rms(jnp.dot(p_ref[...].astype(BF16), wp_ref[...], preferred_element_type=F32), gp_ref[...])
    gate = jax.nn.sigmoid(jnp.dot(x2.astype(BF16), wpg_ref[...], preferred_element_type=F32))
    y_ref[...] = _rms(x2 + ple * gate, gf_ref[...])


def _combine(ys, pos, x1, p, w_ple, gp, w_ple_gate, gf, sub_off):
    t = x1.shape[0]
    tm = COMBINE_SUBS * SUB
    blk_off = sub_off // COMBINE_SUBS
    assert sub_off % COMBINE_SUBS == 0 and t % tm == 0
    row = lambda i: (i, 0)
    const = lambda i: (0, 0)
    return pl.pallas_call(
        _combine_body,
        grid=(t // tm,),
        in_specs=[
            pl.BlockSpec((COMBINE_SUBS * CHUNKS_PER_SUB, ROW_ALIGN, D_MODEL), lambda i: (i + blk_off, 0, 0)),
            pl.BlockSpec((tm, LANES), lambda i: (i + blk_off, 0)),
            pl.BlockSpec((tm, D_MODEL), row),
            pl.BlockSpec((tm, PLE_DIM), row),
            pl.BlockSpec((PLE_DIM, D_MODEL), const),
            pl.BlockSpec((1, D_MODEL), const),
            pl.BlockSpec((D_MODEL, D_MODEL), const),
            pl.BlockSpec((1, D_MODEL), const),
        ],
        out_specs=pl.BlockSpec((tm, D_MODEL), row),
        out_shape=jax.ShapeDtypeStruct((t, D_MODEL), F32),
        compiler_params=_cparams(("parallel",)),
        name="moe_combine_ple",
    )(ys, pos, x1, p, w_ple, gp, w_ple_gate, gf)


def _rope_tables(pos):
    half = HEAD_DIM // 2
    inv = ROPE_BASE ** (-jnp.arange(half, dtype=F32) / half)
    ang = pos[:, None] * inv[None, :]
    cos = jnp.cos(ang)
    sin = jnp.sin(ang)
    return jnp.concatenate([cos, cos], axis=-1), jnp.concatenate([-sin, sin], axis=-1)


def _router_params(we, be, wg, bg):
    pad = LANES - N_EXPERTS - N_GROUPS
    w = jnp.pad(jnp.concatenate([we, wg], axis=1), ((0, 0), (0, pad)))
    b = jnp.pad(jnp.concatenate([be, bg]), (0, pad))[None, :]
    return jnp.concatenate(_split3(w)[:2], axis=1), b


def kernel(x_prompt, x_sample, p_prompt, p_sample, state_conv, state_ret, w_in, conv_w, conv_b, conv_ln_g, conv_ln_b, w_out, norm1_g, norm2_g, router_group_w, router_group_b, router_expert_w, router_expert_b, w_expert_gate, w_expert_up, w_expert_down, w_ple, ple_norm_g, w_ple_gate, final_norm_g):
    assert w_in.shape[0] == 1, "single-layer trunk"
    nb, seq, _ = x_prompt.shape
    ns, dseq, _ = x_sample.shape
    tm = SAMPLE_TILE

    w_in_b = w_in[0].astype(BF16)
    w_out_b = w_out[0].astype(BF16)
    w_ple_b = w_ple[0].astype(BF16)
    w_pg_b = w_ple_gate[0].astype(BF16)
    g1 = norm1_g[0][None, :]
    g2 = norm2_g[0][None, :]
    gp = ple_norm_g[0][None, :]
    gf = final_norm_g[None, :]
    cb = conv_b[0][None, :]
    lng = conv_ln_g[0][None, :]
    lnb = conv_ln_b[0][None, :]
    wr2, br = _router_params(router_expert_w[0], router_expert_b[0], router_group_w[0], router_group_b[0])

    cos_p, sin_p = _rope_tables(jnp.arange(seq, dtype=F32) + jnp.float32(0))
    pos_s = jnp.tile(jnp.arange(dseq, dtype=F32) + jnp.float32(PAST_LEN), tm // dseq)
    cos_s, sin_s = _rope_tables(pos_s)

    xp = x_prompt.reshape(nb * seq, D_MODEL)
    x1_p, t_p, rec_p, conv_p, ret_p = _mix(xp, g1, w_in_b, cos_p, sin_p, conv_w[0], cb, lng, lnb, w_out_b, g2,
                                           wr2, br, nb, seq, MIX_TILE)

    xs = x_sample.reshape(ns * dseq, D_MODEL)
    u, q, k, v, gs = _inproj(xs, g1, w_in_b, cos_s, sin_s, tm)
    c, conv_s = _conv_sample(u.reshape(ns, dseq, CONV_CH), jnp.transpose(state_conv[0], (1, 0, 2)), conv_w[0], cb,
                             lng, lnb, CONV_SAMPLE_SEQS)
    conv_s = jnp.transpose(conv_s, (1, 0, 2))
    o, ret_s = _ret_sample(q, k, v, gs, state_ret[0], dseq, RET_SAMPLE_SEQS)
    x1_s, t_s, rec_s = _outproj(c.reshape(ns * dseq, CONV_CH), o, xs, w_out_b, g2, wr2, br, tm)

    staged, pos, meta = _dispatch(rec_p, t_p, rec_s, t_s)
    start = meta[:, 0, :N_EXPERTS].astype(jnp.int32).reshape(-1)
    cnt = meta[:, 1, :N_EXPERTS].astype(jnp.int32).reshape(-1)
    ys = _experts(start, cnt, staged, w_expert_gate[0], w_expert_up[0], w_expert_down[0])

    y_p = _combine(ys, pos, x1_p, p_prompt[0].reshape(nb * seq, PLE_DIM), w_ple_b, gp, w_pg_b, gf, 0)
    y_s = _combine(ys, pos, x1_s, p_sample[0].reshape(ns * dseq, PLE_DIM), w_ple_b, gp, w_pg_b, gf,
                   nb * seq // SUB)

    return (y_p.reshape(nb, seq, D_MODEL), y_s.reshape(ns, dseq, D_MODEL),
            conv_p[None], ret_p[None], conv_s[None], ret_s[None])
```

```python
import functools

import jax
import jax.numpy as jnp
from jax import lax
from jax.experimental import pallas as pl
from jax.experimental.pallas import tpu as pltpu

F32 = jnp.float32
BF16 = jnp.bfloat16

D_MODEL = 1024
PLE_DIM = 256
CONV_CH = 512
CONV_K = 31
RET_WIDTH = 512
RET_HEADS = 4
HEAD_DIM = 128
CHUNK = 128
ROPE_BASE = 10000.0
N_GROUPS = 4
EXPERTS_PER_GROUP = 8
N_EXPERTS = 32
EXPERT_FF = 256
IN_COLS = 3072
EPS = 1e-6
PAST_LEN = 16384

LANES = 128
SUBLANES = 8
HALO = 32
HALO_OFF = HALO - (CONV_K - 1)
VMEM_LIMIT = 48 * 1024 * 1024
MIX_VMEM_LIMIT = 56 * 1024 * 1024
COMBINE_STAGING_BUFFERS = 3
COMBINE_VMEM_LIMIT = 58 * 1024 * 1024
MIX_TILE = 512
SAMPLE_TILE = 1024
CONV_SAMPLE_SEQS = 32
RET_SAMPLE_SEQS = 16

SUB = 256
ROW_ALIGN = 16
PBLK = 256
SUBP = -(-(2 * SUB + N_EXPERTS * (ROW_ALIGN - 1)) // PBLK) * PBLK
CHUNKS_PER_SUB = SUBP // ROW_ALIGN
ROW_W = D_MODEL + LANES
MBLK = 512
CHUNKS_PER_BLK = MBLK // ROW_ALIGN
LIST_SLACK = 2
COMBINE_SUBS = 4
DISPATCH_SUBS = 4


def _cparams(sem):
    return pltpu.CompilerParams(dimension_semantics=sem, vmem_limit_bytes=VMEM_LIMIT)


def _rms(x, g):
    return x * lax.rsqrt(jnp.mean(x * x, axis=-1, keepdims=True) + EPS) * g


def _inproj_body(x_ref, g1_ref, w_ref, cos_ref, sin_ref, u_ref, q_ref, k_ref, v_ref, gs_ref):
    h = _rms(x_ref[...], g1_ref[...]).astype(BF16)
    z = jnp.dot(h, w_ref[...], preferred_element_type=F32)
    a = z[:, :CONV_CH]
    b = z[:, CONV_CH:2 * CONV_CH]
    u_ref[...] = a * jax.nn.sigmoid(b)
    cos = cos_ref[...]
    sin = sin_ref[...]
    q0 = 2 * CONV_CH
    k0 = q0 + RET_WIDTH
    for hh in range(RET_HEADS):
        sl = slice(hh * HEAD_DIM, (hh + 1) * HEAD_DIM)
        qh = z[:, q0 + hh * HEAD_DIM:q0 + (hh + 1) * HEAD_DIM]
        kh = z[:, k0 + hh * HEAD_DIM:k0 + (hh + 1) * HEAD_DIM]
        q_ref[:, sl] = qh * cos + pltpu.roll(qh, HEAD_DIM // 2, 1) * sin
        k_ref[:, sl] = (kh * cos + pltpu.roll(kh, HEAD_DIM // 2, 1) * sin) * (HEAD_DIM ** -0.5)
    v_ref[...] = z[:, k0 + RET_WIDTH:k0 + 2 * RET_WIDTH]
    g = z[:, k0 + 2 * RET_WIDTH:]
    gs_ref[...] = g * jax.nn.sigmoid(g)


def _inproj(x, g1, w_in, cos, sin, tm):
    t = x.shape[0]
    row = lambda i: (i, 0)
    const = lambda i: (0, 0)
    return pl.pallas_call(
        _inproj_body,
        grid=(t // tm,),
        in_specs=[
            pl.BlockSpec((tm, D_MODEL), row),
            pl.BlockSpec((1, D_MODEL), const),
            pl.BlockSpec((D_MODEL, IN_COLS), const),
            pl.BlockSpec((tm, HEAD_DIM), const),
            pl.BlockSpec((tm, HEAD_DIM), const),
        ],
        out_specs=[pl.BlockSpec((tm, CONV_CH), row)] + [pl.BlockSpec((tm, RET_WIDTH), row)] * 4,
        out_shape=[jax.ShapeDtypeStruct((t, CONV_CH), F32)] + [jax.ShapeDtypeStruct((t, RET_WIDTH), F32)] * 4,
        compiler_params=_cparams(("parallel",)),
        name="inproj",
    )(x, g1, w_in, cos, sin)


def _ln_silu(acc, g, b):
    mu = jnp.mean(acc, axis=-1, keepdims=True)
    d = acc - mu
    var = jnp.mean(d * d, axis=-1, keepdims=True)
    y = d * lax.rsqrt(var + EPS) * g + b
    return y * jax.nn.sigmoid(y)


def _dwconv(load, w_ref, rows):
    acc = None
    for b in range(SUBLANES):
        part = None
        for a in range((CONV_K + HALO_OFF) // SUBLANES + 1):
            k = SUBLANES * a + b - HALO_OFF
            if 0 <= k < CONV_K:
                term = load(SUBLANES * a, rows + SUBLANES) * w_ref[k:k + 1, :]
                part = term if part is None else part + term
        if part is not None:
            shifted = part[b:b + rows]
            acc = shifted if acc is None else acc + shifted
    return acc


def _conv_sample_body(u_ref, st_ref, w_ref, cb_ref, lg_ref, lb_ref, c_ref, nst_ref, ext_ref):
    l = u_ref.shape[1]
    hist = CONV_K - 1
    ext_ref[0:hist] = st_ref[...]
    ext_ref[hist:] = jnp.transpose(u_ref[...], (1, 0, 2))
    acc = ext_ref[0:l] * w_ref[0:1, :] + cb_ref[...]
    for k in range(1, CONV_K):
        acc = acc + ext_ref[k:k + l] * w_ref[k:k + 1, :]
    y = _ln_silu(acc, lg_ref[...], lb_ref[...])
    c_ref[...] = jnp.transpose(y, (1, 0, 2)).astype(c_ref.dtype)
    nst_ref[...] = ext_ref[l:]


def _conv_sample(u, state_t, conv_w, conv_b, ln_g, ln_b, nb):
    n, l, _ = u.shape
    hist = CONV_K - 1
    const = lambda b: (0, 0)
    tok = lambda b: (b, 0, 0)
    tmaj = lambda b: (0, b, 0)
    return pl.pallas_call(
        _conv_sample_body,
        grid=(n // nb,),
        in_specs=[
            pl.BlockSpec((nb, l, CONV_CH), tok),
            pl.BlockSpec((hist, nb, CONV_CH), tmaj),
            pl.BlockSpec((CONV_K, CONV_CH), const),
            pl.BlockSpec((1, CONV_CH), const),
            pl.BlockSpec((1, CONV_CH), const),
            pl.BlockSpec((1, CONV_CH), const),
        ],
        out_specs=[
            pl.BlockSpec((nb, l, CONV_CH), tok),
            pl.BlockSpec((hist, nb, CONV_CH), tmaj),
        ],
        out_shape=[
            jax.ShapeDtypeStruct((n, l, CONV_CH), BF16),
            jax.ShapeDtypeStruct((hist, n, CONV_CH), F32),
        ],
        scratch_shapes=[pltpu.VMEM((hist + l, nb, CONV_CH), F32)],
        compiler_params=_cparams(("parallel",)),
        name="conv_sample",
    )(u, state_t, conv_w, conv_b, ln_g, ln_b)


def _decay_tables(c):
    lg = jnp.log(1.0 - 2.0 ** (-5.0 - jnp.arange(RET_HEADS, dtype=F32)))
    idx = jnp.arange(c, dtype=F32)
    rel = idx[:, None] - idx[None, :]
    dmat = jnp.where(rel[None] >= 0, jnp.exp(jnp.maximum(rel, 0.0)[None] * lg[:, None, None]), 0.0)
    xi = jnp.exp((idx + 1.0)[None, :] * lg[:, None])
    zeta = jnp.exp((c - 1.0 - idx)[None, :] * lg[:, None])
    gc = jnp.exp(c * lg)
    xi_b = jnp.broadcast_to(xi[:, :, None], (RET_HEADS, c, HEAD_DIM))
    zeta_b = jnp.broadcast_to(zeta[:, :, None], (RET_HEADS, c, HEAD_DIM))
    gc_b = jnp.broadcast_to(gc[:, None, None], (RET_HEADS, 1, HEAD_DIM))
    return dmat, xi_b, zeta_b, gc_b


def _group_norm(o):
    mu = jnp.mean(o, axis=-1, keepdims=True)
    d = o - mu
    var = jnp.mean(d * d, axis=-1, keepdims=True)
    return d * lax.rsqrt(var + EPS)


def _ret_chunk(qh, kh, vh, r, dmat, xi, zeta, gc):
    qb = qh.astype(BF16)
    kb = kh.astype(BF16)
    vb = vh.astype(BF16)
    s = lax.dot_general(qb, kb, (((1,), (1,)), ((), ())), preferred_element_type=F32) * dmat
    o = jnp.dot(s.astype(BF16), vb, preferred_element_type=F32)
    o = o + jnp.dot(qb, r.astype(BF16), preferred_element_type=F32) * xi
    kz = (kh.astype(F32) * zeta).astype(BF16)
    r_new = r * gc + lax.dot_general(kz, vb, (((0,), (0,)), ((), ())), preferred_element_type=F32)
    return o, r_new


def _ret_sample_body(q_ref, k_ref, v_ref, gs_ref, st_ref, d_ref, xi_ref, zeta_ref, gc_ref, o_ref, nst_ref):
    nb = st_ref.shape[0]
    l = q_ref.shape[0] // nb
    for b in range(nb):
        rows = slice(b * l, (b + 1) * l)
        for hh in range(RET_HEADS):
            sl = slice(hh * HEAD_DIM, (hh + 1) * HEAD_DIM)
            o, r = _ret_chunk(q_ref[rows, sl], k_ref[rows, sl], v_ref[rows, sl], st_ref[b, hh],
                              d_ref[hh], xi_ref[hh], zeta_ref[hh], gc_ref[hh])
            o_ref[rows, sl] = (gs_ref[rows, sl] * _group_norm(o)).astype(o_ref.dtype)
            nst_ref[b, hh] = r


def _ret_sample(q, k, v, gs, state, l, nb):
    n = state.shape[0]
    dmat, xi, zeta, gc = _decay_tables(l)
    row = lambda b: (b, 0)
    c3 = lambda b: (0, 0, 0)
    blk4 = lambda b: (b, 0, 0, 0)
    return pl.pallas_call(
        _ret_sample_body,
        grid=(n // nb,),
        in_specs=[pl.BlockSpec((nb * l, RET_WIDTH), row)] * 4 + [
            pl.BlockSpec((nb, RET_HEADS, HEAD_DIM, HEAD_DIM), blk4),
            pl.BlockSpec((RET_HEADS, l, l), c3),
            pl.BlockSpec((RET_HEADS, l, HEAD_DIM), c3),
            pl.BlockSpec((RET_HEADS, l, HEAD_DIM), c3),
            pl.BlockSpec((RET_HEADS, 1, HEAD_DIM), c3),
        ],
        out_specs=[
            pl.BlockSpec((nb * l, RET_WIDTH), row),
            pl.BlockSpec((nb, RET_HEADS, HEAD_DIM, HEAD_DIM), blk4),
        ],
        out_shape=[
            jax.ShapeDtypeStruct((n * l, RET_WIDTH), BF16),
            jax.ShapeDtypeStruct((n, RET_HEADS, HEAD_DIM, HEAD_DIM), F32),
        ],
        compiler_params=_cparams(("parallel",)),
        name="ret_sample",
    )(q, k, v, gs, state, dmat, xi, zeta, gc)


def _split3(x):
    hi = x.astype(BF16)
    r1 = x - hi.astype(F32)
    mid = r1.astype(BF16)
    lo = (r1 - mid.astype(F32)).astype(BF16)
    return hi, mid, lo


def _dot_hp(t, w2_ref):
    t_hi, t_mid, _ = _split3(t)
    d = functools.partial(jnp.dot, preferred_element_type=F32)
    both = d(t_hi, w2_ref[...])
    return both[:, :LANES] + (d(t_mid, w2_ref[:, 0:LANES]) + both[:, LANES:])


def _iota_f32(shape, dim):
    return lax.broadcasted_iota(jnp.int32, shape, dim).astype(F32)


def _route(logits):
    lt = logits.T
    tm = lt.shape[1]
    row = _iota_f32((SUBLANES, tm), 0)
    big = float(SUBLANES)

    def rmax(x):
        return jnp.max(x, axis=0, keepdims=True)

    def first_row(mask):
        return jnp.min(jnp.where(mask, row, big), axis=0, keepdims=True)

    lg = jnp.where(row < float(N_GROUPS), lt[N_EXPERTS:N_EXPERTS + SUBLANES, :], -1e30)
    m = rmax(lg)
    g_top = 1.0 / jnp.sum(jnp.exp(lg - m), axis=0, keepdims=True)
    g_idx = first_row(lg == m)
    lem = lt[0:EXPERTS_PER_GROUP, :]
    for g in range(1, N_GROUPS):
        lem = jnp.where(g_idx == float(g), lt[g * EXPERTS_PER_GROUP:(g + 1) * EXPERTS_PER_GROUP, :], lem)
    pe = jnp.exp(lem - rmax(lem))
    p1 = rmax(pe)
    e1 = first_row(pe == p1)
    rest = row != e1
    pe2 = jnp.where(rest, pe, -1.0)
    p2 = rmax(pe2)
    e2 = first_row(rest & (pe2 == p2))
    scale = g_top / (p1 + p2)
    base = g_idx * float(EXPERTS_PER_GROUP)
    rec_t = jnp.where(row == 0.0, base + e1, jnp.where(row == 1.0, base + e2, 0.0))
    rec_t = rec_t + jnp.where(row == 2.0, p1 * scale, jnp.where(row == 3.0, p2 * scale, 0.0))
    return jnp.concatenate([rec_t, jnp.zeros((LANES - SUBLANES, tm), F32)], axis=0).T


def _outproj_body(c_ref, o_ref, x_ref, wo_ref, g2_ref, wr_ref, br_ref, x1_ref, t_ref, rec_ref):
    x1 = x_ref[...] + jnp.dot(c_ref[...], wo_ref[0:CONV_CH, :], preferred_element_type=F32)
    x1 = x1 + jnp.dot(o_ref[...], wo_ref[CONV_CH:, :], preferred_element_type=F32)
    x1_ref[...] = x1
    t = _rms(x1, g2_ref[...])
    t_ref[...] = t.astype(t_ref.dtype)
    rec_ref[...] = _route(_dot_hp(t, wr_ref) + br_ref[...])


def _outproj(c, o, x, w_out, g2, wr2, br, tm):
    t = x.shape[0]
    row = lambda i: (i, 0)
    const = lambda i: (0, 0)
    return pl.pallas_call(
        _outproj_body,
        grid=(t // tm,),
        in_specs=[
            pl.BlockSpec((tm, CONV_CH), row),
            pl.BlockSpec((tm, RET_WIDTH), row),
            pl.BlockSpec((tm, D_MODEL), row),
            pl.BlockSpec((D_MODEL, D_MODEL), const),
            pl.BlockSpec((1, D_MODEL), const),
            pl.BlockSpec((D_MODEL, 2 * LANES), const),
            pl.BlockSpec((1, LANES), const),
        ],
        out_specs=[
            pl.BlockSpec((tm, D_MODEL), row),
            pl.BlockSpec((tm, D_MODEL), row),
            pl.BlockSpec((tm, LANES), row),
        ],
        out_shape=[
            jax.ShapeDtypeStruct((t, D_MODEL), F32),
            jax.ShapeDtypeStruct((t, D_MODEL), BF16),
            jax.ShapeDtypeStruct((t, LANES), F32),
        ],
        compiler_params=_cparams(("parallel",)),
        name="outproj_router",
    )(c, o, x, w_out, g2, wr2, br)


def _mix_body(x_ref, g1_ref, w_ref, cos_ref, sin_ref, cw_ref, cb_ref, lg_ref, lb_ref,
              d_ref, xi_ref, zeta_ref, gc_ref, wo_ref, g2_ref, wr_ref, br_ref,
              x1_ref, t_ref, rec_ref, cst_ref, rst_ref, ext_ref, r_ref, o_ref):
    j = pl.program_id(1)
    tl = x_ref.shape[0]
    x = x_ref[...]
    z = jnp.dot(_rms(x, g1_ref[...]).astype(BF16), w_ref[...], preferred_element_type=F32)

    @pl.when(j == 0)
    def _():
        ext_ref[0:HALO, :] = jnp.zeros((HALO, CONV_CH), F32)
        ext_ref[tl + HALO:, :] = jnp.zeros((SUBLANES, CONV_CH), F32)
        r_ref[...] = jnp.zeros_like(r_ref)

    @pl.when(j > 0)
    def _():
        ext_ref[0:HALO, :] = ext_ref[tl:tl + HALO, :]

    ext_ref[HALO:tl + HALO, :] = z[:, :CONV_CH] * jax.nn.sigmoid(z[:, CONV_CH:2 * CONV_CH])
    acc = _dwconv(lambda s, n: ext_ref[s:s + n, :], cw_ref, tl) + cb_ref[...]
    o_ref[:, 0:CONV_CH] = _ln_silu(acc, lg_ref[...], lb_ref[...]).astype(BF16)
    cst_ref[0] = ext_ref[tl + HALO_OFF:tl + HALO, :]

    cos = cos_ref[...]
    sin = sin_ref[...]
    q0 = 2 * CONV_CH
    k0 = q0 + RET_WIDTH
    v0 = k0 + RET_WIDTH
    g0 = v0 + RET_WIDTH
    for hh in range(RET_HEADS):
        lo = hh * HEAD_DIM
        qh = z[:, q0 + lo:q0 + lo + HEAD_DIM]
        kh = z[:, k0 + lo:k0 + lo + HEAD_DIM]
        qr = (qh * cos + pltpu.roll(qh, HEAD_DIM // 2, 1) * sin).astype(BF16)
        kr = ((kh * cos + pltpu.roll(kh, HEAD_DIM // 2, 1) * sin) * (HEAD_DIM ** -0.5)).astype(BF16)
        vh = z[:, v0 + lo:v0 + lo + HEAD_DIM].astype(BF16)
        g = z[:, g0 + lo:g0 + lo + HEAD_DIM]
        gs = g * jax.nn.sigmoid(g)
        r = r_ref[hh]
        for ci in range(tl // CHUNK):
            rows = slice(ci * CHUNK, (ci + 1) * CHUNK)
            o, r = _ret_chunk(qr[rows], kr[rows], vh[rows], r, d_ref[hh], xi_ref[hh], zeta_ref[hh], gc_ref[hh])
            o_ref[rows, CONV_CH + lo:CONV_CH + lo + HEAD_DIM] = (gs[rows] * _group_norm(o)).astype(BF16)
        r_ref[hh] = r
    rst_ref[0] = r_ref[...]

    x1 = x + jnp.dot(o_ref[...], wo_ref[...], preferred_element_type=F32)
    x1_ref[...] = x1
    t = _rms(x1, g2_ref[...])
    t_ref[...] = t.astype(t_ref.dtype)
    rec_ref[...] = _route(_dot_hp(t, wr_ref) + br_ref[...])


def _mix(x, g1, w_in, cos, sin, conv_w, conv_b, ln_g, ln_b, w_out, g2, wr2, br, n, l, tl):
    dmat, xi, zeta, gc = _decay_tables(CHUNK)
    per = l // tl
    row = lambda b, j: (b * per + j, 0)
    tab = lambda b, j: (j, 0)
    const = lambda b, j: (0, 0)
    c3 = lambda b, j: (0, 0, 0)
    once = dict(pipeline_mode=pl.Buffered(1))
    return pl.pallas_call(
        _mix_body,
        grid=(n, per),
        in_specs=[
            pl.BlockSpec((tl, D_MODEL), row),
            pl.BlockSpec((1, D_MODEL), const),
            pl.BlockSpec((D_MODEL, IN_COLS), const, **once),
            pl.BlockSpec((tl, HEAD_DIM), tab),
            pl.BlockSpec((tl, HEAD_DIM), tab),
            pl.BlockSpec((CONV_K, CONV_CH), const),
            pl.BlockSpec((1, CONV_CH), const),
            pl.BlockSpec((1, CONV_CH), const),
            pl.BlockSpec((1, CONV_CH), const),
            pl.BlockSpec((RET_HEADS, CHUNK, CHUNK), c3),
            pl.BlockSpec((RET_HEADS, CHUNK, HEAD_DIM), c3),
            pl.BlockSpec((RET_HEADS, CHUNK, HEAD_DIM), c3),
            pl.BlockSpec((RET_HEADS, 1, HEAD_DIM), c3),
            pl.BlockSpec((D_MODEL, D_MODEL), const, **once),
            pl.BlockSpec((1, D_MODEL), const),
            pl.BlockSpec((D_MODEL, 2 * LANES), const, **once),
            pl.BlockSpec((1, LANES), const),
        ],
        out_specs=[
            pl.BlockSpec((tl, D_MODEL), row),
            pl.BlockSpec((tl, D_MODEL), row),
            pl.BlockSpec((tl, LANES), row),
            pl.BlockSpec((1, CONV_K - 1, CONV_CH), lambda b, j: (b, 0, 0)),
            pl.BlockSpec((1, RET_HEADS, HEAD_DIM, HEAD_DIM), lambda b, j: (b, 0, 0, 0)),
        ],
        out_shape=[
            jax.ShapeDtypeStruct((n * l, D_MODEL), F32),
            jax.ShapeDtypeStruct((n * l, D_MODEL), BF16),
            jax.ShapeDtypeStruct((n * l, LANES), F32),
            jax.ShapeDtypeStruct((n, CONV_K - 1, CONV_CH), F32),
            jax.ShapeDtypeStruct((n, RET_HEADS, HEAD_DIM, HEAD_DIM), F32),
        ],
        scratch_shapes=[
            pltpu.VMEM((tl + HALO + SUBLANES, CONV_CH), F32),
            pltpu.VMEM((RET_HEADS, HEAD_DIM, HEAD_DIM), F32),
            pltpu.VMEM((tl, D_MODEL), BF16),
        ],
        compiler_params=pltpu.CompilerParams(dimension_semantics=("arbitrary", "arbitrary"),
                                             vmem_limit_bytes=MIX_VMEM_LIMIT),
        name="token_mix",
    )(x, g1, w_in, cos, sin, conv_w, conv_b, ln_g, ln_b, dmat, xi, zeta, gc, w_out, g2, wr2, br)


def _dispatch_body(rec_a_ref, t_a_ref, rec_b_ref, t_b_ref, s_ref, pos_ref, meta_ref, *, nsub_a):
    from_a = pl.program_id(0) * DISPATCH_SUBS < nsub_a
    for s in range(DISPATCH_SUBS):
        rows = slice(s * SUB, (s + 1) * SUB)
        rec = jnp.where(from_a, rec_a_ref[rows, :], rec_b_ref[rows, :])
        tok = jnp.where(from_a, t_a_ref[rows, :], t_b_ref[rows, :])
        chunks = slice(s * CHUNKS_PER_SUB, (s + 1) * CHUNKS_PER_SUB)
        _dispatch_sub_tile(rec, tok, s_ref.at[chunks], pos_ref.at[rows], meta_ref.at[s])


def _dispatch_sub_tile(rec, tok, s_ref, pos_ref, meta_ref):
    lane = _iota_f32(rec.shape, 1)
    a1 = lane == rec[:, 0:1]
    a2 = lane == rec[:, 1:2]
    a1f = jnp.where(a1, 1.0, 0.0)
    a2f = jnp.where(a2, 1.0, 0.0)
    ltri = jnp.where(_iota_f32((SUB, SUB), 1) < _iota_f32((SUB, SUB), 0), 1.0, 0.0).astype(BF16)
    c1 = jnp.dot(ltri, a1f.astype(BF16), preferred_element_type=F32)
    c2 = jnp.dot(ltri, a2f.astype(BF16), preferred_element_type=F32)
    n1 = jnp.sum(a1f, axis=0, keepdims=True)
    n2 = jnp.sum(a2f, axis=0, keepdims=True)
    cnt = jnp.floor((n1 + n2 + (ROW_ALIGN - 1.0)) * (1.0 / ROW_ALIGN))
    utri = jnp.where(_iota_f32((LANES, LANES), 0) < _iota_f32((LANES, LANES), 1), 1.0, 0.0).astype(BF16)
    start = jnp.dot(jnp.broadcast_to(cnt, (SUBLANES, LANES)).astype(BF16), utri,
                    preferred_element_type=F32)[0:1]
    base1 = start * ROW_ALIGN
    base2 = base1 + n1
    pos1 = jnp.sum(jnp.where(a1, c1 + base1, 0.0), axis=1, keepdims=True)
    pos2 = jnp.sum(jnp.where(a2, c2 + base2, 0.0), axis=1, keepdims=True)
    posm = jnp.where(lane == 0.0, pos1, jnp.where(lane == 1.0, pos2, 0.0))
    pos_ref[...] = posm
    row = _iota_f32((SUBLANES, LANES), 0)
    meta_ref[...] = jnp.where(row == 0.0, start, jnp.where(row == 1.0, cnt, 0.0))

    g1 = _split3(rec[:, 2:3])
    g2 = _split3(rec[:, 3:4])
    info = jnp.where(lane == 6.0, rec[:, 0:1], jnp.where(lane == 7.0, rec[:, 1:2], 0.0))
    for i in range(3):
        info = jnp.where(lane == float(i), g1[i].astype(F32), info)
        info = jnp.where(lane == float(3 + i), g2[i].astype(F32), info)
    src = jnp.concatenate([tok, info.astype(BF16)], axis=1)

    post = posm.T
    r = _iota_f32((SUBP, SUB), 0)
    onehot = jnp.where(r == post[0:1, :], 1.0, jnp.where(r == post[1:2, :], 1.0, 0.0)).astype(BF16)
    sorted_rows = jnp.dot(onehot, src, preferred_element_type=F32).astype(BF16)
    s_ref[...] = sorted_rows.reshape(CHUNKS_PER_SUB, ROW_ALIGN, ROW_W)


def _dispatch(rec_a, t_a, rec_b, t_b):
    nsub_a = rec_a.shape[0] // SUB
    nsub_b = rec_b.shape[0] // SUB
    nsub = nsub_a + nsub_b
    assert nsub_a % DISPATCH_SUBS == 0 and nsub_b % DISPATCH_SUBS == 0
    steps_a = nsub_a // DISPATCH_SUBS
    tm = DISPATCH_SUBS * SUB
    row = lambda i: (i, 0)
    row_a = lambda i: (jnp.minimum(i, steps_a - 1), 0)
    row_b = lambda i: (jnp.maximum(i - steps_a, 0), 0)
    return pl.pallas_call(
        functools.partial(_dispatch_body, nsub_a=nsub_a),
        grid=(nsub // DISPATCH_SUBS,),
        in_specs=[
            pl.BlockSpec((tm, LANES), row_a),
            pl.BlockSpec((tm, D_MODEL), row_a),
            pl.BlockSpec((tm, LANES), row_b),
            pl.BlockSpec((tm, D_MODEL), row_b),
        ],
        out_specs=[
            pl.BlockSpec((DISPATCH_SUBS * CHUNKS_PER_SUB, ROW_ALIGN, ROW_W), lambda i: (i, 0, 0)),
            pl.BlockSpec((tm, LANES), row),
            pl.BlockSpec((DISPATCH_SUBS, SUBLANES, LANES), lambda i: (i, 0, 0)),
        ],
        out_shape=[
            jax.ShapeDtypeStruct((nsub * CHUNKS_PER_SUB, ROW_ALIGN, ROW_W), BF16),
            jax.ShapeDtypeStruct((nsub * SUB, LANES), F32),
            jax.ShapeDtypeStruct((nsub, SUBLANES, LANES), F32),
        ],
        compiler_params=_cparams(("parallel",)),
        name="moe_dispatch",
    )(rec_a, t_a, rec_b, t_b)


def _experts_body(start_ref, cnt_ref, s_in, wg_ref, wu_ref, wd_ref, s_hbm,
                  xbuf, ybuf, wgu_ref, wdb_ref, gsem, ssem, list_ref, state_ref, *, nsub):
    del s_in
    e = pl.program_id(0)
    ne = pl.num_programs(0)
    par = e & 1

    list_max = list_ref.shape[0] // 2

    def gather_copy(src, i, slot):
        return pltpu.make_async_copy(s_hbm.at[src], xbuf.at[slot, i], gsem.at[slot])

    def scatter_copy(dst, i, slot):
        return pltpu.make_async_copy(ybuf.at[slot, i], s_hbm.at[dst], ssem.at[slot])

    def build_list(x, which):
        def per_sub(s, k):
            run = s * N_EXPERTS + x
            c = cnt_ref[run]
            base = s * CHUNKS_PER_SUB + start_ref[run]
            list_ref[k] = base
            list_ref[k + 1] = base + 1

            def per_chunk(i, carry):
                list_ref[k + i] = base + i
                return carry
            lax.fori_loop(2, c, per_chunk, 0)
            return k + c
        first = which * list_max
        state_ref[which] = lax.fori_loop(0, nsub, per_sub, first) - first

    def start_all(copy, which, first, n, slot, counter):
        def body(i, carry):
            copy(list_ref[which * list_max + first + i], i, slot).start()
            return carry
        lax.fori_loop(0, n, body, 0)
        state_ref[counter] = n

    def wait_all(copy, block_copy, slot, counter):
        n = state_ref[counter]

        @pl.when(n == CHUNKS_PER_BLK)
        def _():
            block_copy(slot).wait()

        @pl.when(n < CHUNKS_PER_BLK)
        def _():
            def body(i, carry):
                copy(0, 0, slot).wait()
                return carry
            lax.fori_loop(0, n, body, 0)
        state_ref[counter] = 0

    def gather_block(slot):
        return pltpu.make_async_copy(s_hbm.at[pl.ds(0, CHUNKS_PER_BLK)], xbuf.at[slot], gsem.at[slot])

    def scatter_block(slot):
        return pltpu.make_async_copy(ybuf.at[slot], s_hbm.at[pl.ds(0, CHUNKS_PER_BLK)], ssem.at[slot])

    def block_chunks(total, b):
        return jnp.minimum(total - b * CHUNKS_PER_BLK, CHUNKS_PER_BLK)

    @pl.when(e == 0)
    def _():
        for i in range(6):
            state_ref[i] = 0
        xbuf[...] = jnp.zeros_like(xbuf)
        build_list(0, 0)
        n0 = state_ref[0]

        @pl.when(n0 > 0)
        def _():
            start_all(gather_copy, 0, 0, block_chunks(n0, 0), 0, 2)

    @pl.when(e + 1 < ne)
    def _():
        build_list(e + 1, 1 - par)

    total = state_ref[par]
    nblk = (total + CHUNKS_PER_BLK - 1) // CHUNKS_PER_BLK
    wgu_ref[:, 0:EXPERT_FF] = wg_ref[0].astype(BF16)
    wgu_ref[:, EXPERT_FF:] = wu_ref[0].astype(BF16)
    wdb_ref[...] = wd_ref[0].astype(BF16)
    ef = e.astype(F32)

    def block(b, carry):
        slot = b & 1
        first = b * CHUNKS_PER_BLK

        @pl.when(b + 1 < nblk)
        def _():
            start_all(gather_copy, par, first + CHUNKS_PER_BLK, block_chunks(total, b + 1), 1 - slot, 3 - slot)

        wait_all(gather_copy, gather_block, slot, 2 + slot)
        wait_all(scatter_copy, scatter_block, slot, 4 + slot)
        x = xbuf[slot].reshape(MBLK, ROW_W)
        info = x[:, D_MODEL:].astype(F32)
        g_first = info[:, 0:1] + info[:, 1:2] + info[:, 2:3]
        g_second = info[:, 3:4] + info[:, 4:5] + info[:, 5:6]
        gate = jnp.where(info[:, 6:7] == ef, g_first, g_second)
        h = jnp.dot(x[:, :D_MODEL], wgu_ref[...], preferred_element_type=F32)
        h1 = h[:, :EXPERT_FF]
        hid = (h1 * jax.nn.sigmoid(h1)) * h[:, EXPERT_FF:] * gate
        y = jnp.dot(hid.astype(BF16), wdb_ref[...], preferred_element_type=F32).astype(BF16)
        ybuf[slot] = jnp.concatenate([y, x[:, D_MODEL:]], axis=1).reshape(CHUNKS_PER_BLK, ROW_ALIGN, ROW_W)
        start_all(scatter_copy, par, first, block_chunks(total, b), slot, 4 + slot)
        return carry

    lax.fori_loop(0, nblk, block, 0)

    @pl.when(e + 1 < ne)
    def _():
        n1 = state_ref[1 - par]

        @pl.when(n1 > 0)
        def _():
            start_all(gather_copy, 1 - par, 0, block_chunks(n1, 0), 0, 2)

    @pl.when(e == ne - 1)
    def _():
        wait_all(scatter_copy, scatter_block, 0, 4)
        wait_all(scatter_copy, scatter_block, 1, 5)


def _experts(start, cnt, staged, wg, wu, wd):
    nsub = staged.shape[0] // CHUNKS_PER_SUB
    list_max = nsub * SUB // ROW_ALIGN + nsub + LIST_SLACK
    wblk = lambda e, *_: (e, 0, 0)
    grid_spec = pltpu.PrefetchScalarGridSpec(
        num_scalar_prefetch=2,
        grid=(N_EXPERTS,),
        in_specs=[
            pl.BlockSpec(memory_space=pl.ANY),
            pl.BlockSpec((1, D_MODEL, EXPERT_FF), wblk),
            pl.BlockSpec((1, D_MODEL, EXPERT_FF), wblk),
            pl.BlockSpec((1, EXPERT_FF, D_MODEL), wblk),
        ],
        out_specs=pl.BlockSpec(memory_space=pl.ANY),
        scratch_shapes=[
            pltpu.VMEM((2, CHUNKS_PER_BLK, ROW_ALIGN, ROW_W), BF16),
            pltpu.VMEM((2, CHUNKS_PER_BLK, ROW_ALIGN, ROW_W), BF16),
            pltpu.VMEM((D_MODEL, 2 * EXPERT_FF), BF16),
            pltpu.VMEM((EXPERT_FF, D_MODEL), BF16),
            pltpu.SemaphoreType.DMA((2,)),
            pltpu.SemaphoreType.DMA((2,)),
            pltpu.SMEM((2 * list_max,), jnp.int32),
            pltpu.SMEM((6,), jnp.int32),
        ],
    )
    return pl.pallas_call(
        functools.partial(_experts_body, nsub=nsub),
        grid_spec=grid_spec,
        out_shape=jax.ShapeDtypeStruct(staged.shape, staged.dtype),
        input_output_aliases={2: 0},
        compiler_params=_cparams(("arbitrary",)),
        name="moe_experts",
    )(start, cnt, staged, wg, wu, wd)


def _combine_body(ys_ref, pos_ref, x1_ref, p_ref, wp_ref, gp_ref, wpg_ref, gf_ref, y_ref):
    r = _iota_f32((SUB, SUBP), 1)
    moe = []
    for s in range(x1_ref.shape[0] // SUB):
        p1 = pos_ref[s * SUB:(s + 1) * SUB, 0:1]
        p2 = pos_ref[s * SUB:(s + 1) * SUB, 1:2]
        onehot = jnp.where(r == p1, 1.0, jnp.where(r == p2, 1.0, 0.0)).astype(BF16)
        ys = ys_ref[s * CHUNKS_PER_SUB:(s + 1) * CHUNKS_PER_SUB].reshape(SUBP, D_MODEL)
        moe.append(jnp.dot(onehot, ys, preferred_element_type=F32))
    x2 = x1_ref[...] + jnp.concatenate(moe, axis=0)
    ple = _rms(jnp.dot(p_ref[...].astype(BF16), wp_ref[...], preferred_element_type=F32), gp_ref[...])
    gate = jax.nn.sigmoid(jnp.dot(x2.astype(BF16), wpg_ref[...], preferred_element_type=F32))
    y_ref[...] = _rms(x2 + ple * gate, gf_ref[...])


def _combine(ys, pos, x1, p, w_ple, gp, w_ple_gate, gf, sub_off):
    t = x1.shape[0]
    tm = COMBINE_SUBS * SUB
    blk_off = sub_off // COMBINE_SUBS
    assert sub_off % COMBINE_SUBS == 0 and t % tm == 0
    steps = t // tm
    row = lambda i: (i, 0)
    deep = dict(pipeline_mode=pl.Buffered(COMBINE_STAGING_BUFFERS)) if steps >= COMBINE_STAGING_BUFFERS else {}
    stream_in = [
        pl.BlockSpec((COMBINE_SUBS * CHUNKS_PER_SUB, ROW_ALIGN, D_MODEL), lambda i: (i + blk_off, 0, 0), **deep),
        pl.BlockSpec((tm, LANES), lambda i: (i + blk_off, 0)),
        pl.BlockSpec((tm, D_MODEL), row),
        pl.BlockSpec((tm, PLE_DIM), row),
    ]
    stream_out = [pl.BlockSpec((tm, D_MODEL), row)]

    def body(ys_hbm, pos_hbm, x1_hbm, p_hbm, wp_ref, gp_ref, wpg_ref, gf_ref, y_hbm):
        def step(ys_ref, pos_ref, x1_ref, p_ref, y_ref):
            _combine_body(ys_ref, pos_ref, x1_ref, p_ref, wp_ref, gp_ref, wpg_ref, gf_ref, y_ref)
        pltpu.emit_pipeline(step, grid=(steps,), in_specs=stream_in, out_specs=stream_out)(
            ys_hbm, pos_hbm, x1_hbm, p_hbm, y_hbm)

    hbm = pl.BlockSpec(memory_space=pl.ANY)
    vmem = pl.BlockSpec(memory_space=pltpu.VMEM)
    return pl.pallas_call(
        body,
        in_specs=[hbm, hbm, hbm, hbm, vmem, vmem, vmem, vmem],
        out_specs=hbm,
        out_shape=jax.ShapeDtypeStruct((t, D_MODEL), F32),
        compiler_params=pltpu.CompilerParams(vmem_limit_bytes=COMBINE_VMEM_LIMIT),
        name="moe_combine_ple",
    )(ys, pos, x1, p, w_ple, gp, w_ple_gate, gf)


def _rope_tables(pos):
    half = HEAD_DIM // 2
    inv = ROPE_BASE ** (-jnp.arange(half, dtype=F32) / half)
    ang = pos[:, None] * inv[None, :]
    cos = jnp.cos(ang)
    sin = jnp.sin(ang)
    return jnp.concatenate([cos, cos], axis=-1), jnp.concatenate([-sin, sin], axis=-1)


def _router_params(we, be, wg, bg):
    pad = LANES - N_EXPERTS - N_GROUPS
    w = jnp.pad(jnp.concatenate([we, wg], axis=1), ((0, 0), (0, pad)))
    b = jnp.pad(jnp.concatenate([be, bg]), (0, pad))[None, :]
    return jnp.concatenate(_split3(w)[:2], axis=1), b


def kernel(x_prompt, x_sample, p_prompt, p_sample, state_conv, state_ret, w_in, conv_w, conv_b, conv_ln_g, conv_ln_b, w_out, norm1_g, norm2_g, router_group_w, router_group_b, router_expert_w, router_expert_b, w_expert_gate, w_expert_up, w_expert_down, w_ple, ple_norm_g, w_ple_gate, final_norm_g):
    assert w_in.shape[0] == 1, "single-layer trunk"
    nb, seq, _ = x_prompt.shape
    ns, dseq, _ = x_sample.shape
    tm = SAMPLE_TILE

    w_in_b = w_in[0].astype(BF16)
    w_out_b = w_out[0].astype(BF16)
    w_ple_b = w_ple[0].astype(BF16)
    w_pg_b = w_ple_gate[0].astype(BF16)
    g1 = norm1_g[0][None, :]
    g2 = norm2_g[0][None, :]
    gp = ple_norm_g[0][None, :]
    gf = final_norm_g[None, :]
    cb = conv_b[0][None, :]
    lng = conv_ln_g[0][None, :]
    lnb = conv_ln_b[0][None, :]
    wr2, br = _router_params(router_expert_w[0], router_expert_b[0], router_group_w[0], router_group_b[0])

    cos_p, sin_p = _rope_tables(jnp.arange(seq, dtype=F32) + jnp.float32(0))
    pos_s = jnp.tile(jnp.arange(dseq, dtype=F32) + jnp.float32(PAST_LEN), tm // dseq)
    cos_s, sin_s = _rope_tables(pos_s)

    xp = x_prompt.reshape(nb * seq, D_MODEL)
    x1_p, t_p, rec_p, conv_p, ret_p = _mix(xp, g1, w_in_b, cos_p, sin_p, conv_w[0], cb, lng, lnb, w_out_b, g2,
                                           wr2, br, nb, seq, MIX_TILE)

    xs = x_sample.reshape(ns * dseq, D_MODEL)
    u, q, k, v, gs = _inproj(xs, g1, w_in_b, cos_s, sin_s, tm)
    c, conv_s = _conv_sample(u.reshape(ns, dseq, CONV_CH), jnp.transpose(state_conv[0], (1, 0, 2)), conv_w[0], cb,
                             lng, lnb, CONV_SAMPLE_SEQS)
    conv_s = jnp.transpose(conv_s, (1, 0, 2))
    o, ret_s = _ret_sample(q, k, v, gs, state_ret[0], dseq, RET_SAMPLE_SEQS)
    x1_s, t_s, rec_s = _outproj(c.reshape(ns * dseq, CONV_CH), o, xs, w_out_b, g2, wr2, br, tm)

    staged, pos, meta = _dispatch(rec_p, t_p, rec_s, t_s)
    start = meta[:, 0, :N_EXPERTS].astype(jnp.int32).reshape(-1)
    cnt = meta[:, 1, :N_EXPERTS].astype(jnp.int32).reshape(-1)
    ys = _experts(start, cnt, staged, w_expert_gate[0], w_expert_up[0], w_expert_down[0])

    y_p = _combine(ys, pos, x1_p, p_prompt[0].reshape(nb * seq, PLE_DIM), w_ple_b, gp, w_pg_b, gf, 0)
    y_s = _combine(ys, pos, x1_s, p_sample[0].reshape(ns * dseq, PLE_DIM), w_ple_b, gp, w_pg_b, gf,
                   nb * seq // SUB)

    return (y_p.reshape(nb, seq, D_MODEL), y_s.reshape(ns, dseq, D_MODEL),
            conv_p[None], ret_p[None], conv_s[None], ret_s[None])
```

```python
import functools

import jax
import jax.numpy as jnp
from jax import lax
from jax.experimental import pallas as pl
from jax.experimental.pallas import tpu as pltpu

F32 = jnp.float32
BF16 = jnp.bfloat16

D_MODEL = 1024
PLE_DIM = 256
CONV_CH = 512
CONV_K = 31
RET_WIDTH = 512
RET_HEADS = 4
HEAD_DIM = 128
CHUNK = 128
ROPE_BASE = 10000.0
N_GROUPS = 4
EXPERTS_PER_GROUP = 8
N_EXPERTS = 32
EXPERT_FF = 256
IN_COLS = 3072
EPS = 1e-6
PAST_LEN = 16384

LANES = 128
SUBLANES = 8
HALO = 32
HALO_OFF = HALO - (CONV_K - 1)
VMEM_LIMIT = 48 * 1024 * 1024
MIX_VMEM_LIMIT = 56 * 1024 * 1024
MIX_TILE = 512
SAMPLE_TILE = 1024
CONV_SAMPLE_SEQS = 32
RET_SAMPLE_SEQS = 16

SUB = 256
ROW_ALIGN = 16
PBLK = 256
SUBP = -(-(2 * SUB + N_EXPERTS * (ROW_ALIGN - 1)) // PBLK) * PBLK
CHUNKS_PER_SUB = SUBP // ROW_ALIGN
ROW_W = D_MODEL + LANES
MBLK = 512
CHUNKS_PER_BLK = MBLK // ROW_ALIGN
LIST_SLACK = 2
COMBINE_SUBS = 4
DISPATCH_SUBS = 4


def _cparams(sem):
    return pltpu.CompilerParams(dimension_semantics=sem, vmem_limit_bytes=VMEM_LIMIT)


def _rms(x, g):
    return x * lax.rsqrt(jnp.mean(x * x, axis=-1, keepdims=True) + EPS) * g


def _inproj_body(x_ref, g1_ref, w_ref, cos_ref, sin_ref, u_ref, q_ref, k_ref, v_ref, gs_ref):
    h = _rms(x_ref[...], g1_ref[...]).astype(BF16)
    z = jnp.dot(h, w_ref[...], preferred_element_type=F32)
    a = z[:, :CONV_CH]
    b = z[:, CONV_CH:2 * CONV_CH]
    u_ref[...] = a * jax.nn.sigmoid(b)
    cos = cos_ref[...]
    sin = sin_ref[...]
    q0 = 2 * CONV_CH
    k0 = q0 + RET_WIDTH
    for hh in range(RET_HEADS):
        sl = slice(hh * HEAD_DIM, (hh + 1) * HEAD_DIM)
        qh = z[:, q0 + hh * HEAD_DIM:q0 + (hh + 1) * HEAD_DIM]
        kh = z[:, k0 + hh * HEAD_DIM:k0 + (hh + 1) * HEAD_DIM]
        q_ref[:, sl] = qh * cos + pltpu.roll(qh, HEAD_DIM // 2, 1) * sin
        k_ref[:, sl] = (kh * cos + pltpu.roll(kh, HEAD_DIM // 2, 1) * sin) * (HEAD_DIM ** -0.5)
    v_ref[...] = z[:, k0 + RET_WIDTH:k0 + 2 * RET_WIDTH]
    g = z[:, k0 + 2 * RET_WIDTH:]
    gs_ref[...] = g * jax.nn.sigmoid(g)


def _inproj(x, g1, w_in, cos, sin, tm):
    t = x.shape[0]
    row = lambda i: (i, 0)
    const = lambda i: (0, 0)
    return pl.pallas_call(
        _inproj_body,
        grid=(t // tm,),
        in_specs=[
            pl.BlockSpec((tm, D_MODEL), row),
            pl.BlockSpec((1, D_MODEL), const),
            pl.BlockSpec((D_MODEL, IN_COLS), const),
            pl.BlockSpec((tm, HEAD_DIM), const),
            pl.BlockSpec((tm, HEAD_DIM), const),
        ],
        out_specs=[pl.BlockSpec((tm, CONV_CH), row)] + [pl.BlockSpec((tm, RET_WIDTH), row)] * 4,
        out_shape=[jax.ShapeDtypeStruct((t, CONV_CH), F32)] + [jax.ShapeDtypeStruct((t, RET_WIDTH), F32)] * 4,
        compiler_params=_cparams(("parallel",)),
        name="inproj",
    )(x, g1, w_in, cos, sin)


def _ln_silu(acc, g, b):
    mu = jnp.mean(acc, axis=-1, keepdims=True)
    d = acc - mu
    var = jnp.mean(d * d, axis=-1, keepdims=True)
    y = d * lax.rsqrt(var + EPS) * g + b
    return y * jax.nn.sigmoid(y)


def _dwconv(load, w_ref, rows):
    acc = None
    for b in range(SUBLANES):
        part = None
        for a in range((CONV_K + HALO_OFF) // SUBLANES + 1):
            k = SUBLANES * a + b - HALO_OFF
            if 0 <= k < CONV_K:
                term = load(SUBLANES * a, rows + SUBLANES) * w_ref[k:k + 1, :]
                part = term if part is None else part + term
        if part is not None:
            shifted = part[b:b + rows]
            acc = shifted if acc is None else acc + shifted
    return acc


def _conv_sample_body(u_ref, st_ref, w_ref, cb_ref, lg_ref, lb_ref, c_ref, nst_ref, ext_ref):
    l = u_ref.shape[1]
    hist = CONV_K - 1
    ext_ref[0:hist] = st_ref[...]
    ext_ref[hist:] = jnp.transpose(u_ref[...], (1, 0, 2))
    acc = ext_ref[0:l] * w_ref[0:1, :] + cb_ref[...]
    for k in range(1, CONV_K):
        acc = acc + ext_ref[k:k + l] * w_ref[k:k + 1, :]
    y = _ln_silu(acc, lg_ref[...], lb_ref[...])
    c_ref[...] = jnp.transpose(y, (1, 0, 2)).astype(c_ref.dtype)
    nst_ref[...] = ext_ref[l:]


def _conv_sample(u, state_t, conv_w, conv_b, ln_g, ln_b, nb):
    n, l, _ = u.shape
    hist = CONV_K - 1
    const = lambda b: (0, 0)
    tok = lambda b: (b, 0, 0)
    tmaj = lambda b: (0, b, 0)
    return pl.pallas_call(
        _conv_sample_body,
        grid=(n // nb,),
        in_specs=[
            pl.BlockSpec((nb, l, CONV_CH), tok),
            pl.BlockSpec((hist, nb, CONV_CH), tmaj),
            pl.BlockSpec((CONV_K, CONV_CH), const),
            pl.BlockSpec((1, CONV_CH), const),
            pl.BlockSpec((1, CONV_CH), const),
            pl.BlockSpec((1, CONV_CH), const),
        ],
        out_specs=[
            pl.BlockSpec((nb, l, CONV_CH), tok),
            pl.BlockSpec((hist, nb, CONV_CH), tmaj),
        ],
        out_shape=[
            jax.ShapeDtypeStruct((n, l, CONV_CH), BF16),
            jax.ShapeDtypeStruct((hist, n, CONV_CH), F32),
        ],
        scratch_shapes=[pltpu.VMEM((hist + l, nb, CONV_CH), F32)],
        compiler_params=_cparams(("parallel",)),
        name="conv_sample",
    )(u, state_t, conv_w, conv_b, ln_g, ln_b)


def _decay_tables(c):
    lg = jnp.log(1.0 - 2.0 ** (-5.0 - jnp.arange(RET_HEADS, dtype=F32)))
    idx = jnp.arange(c, dtype=F32)
    rel = idx[:, None] - idx[None, :]
    dmat = jnp.where(rel[None] >= 0, jnp.exp(jnp.maximum(rel, 0.0)[None] * lg[:, None, None]), 0.0)
    xi = jnp.exp((idx + 1.0)[None, :] * lg[:, None])
    zeta = jnp.exp((c - 1.0 - idx)[None, :] * lg[:, None])
    gc = jnp.exp(c * lg)
    xi_b = jnp.broadcast_to(xi[:, :, None], (RET_HEADS, c, HEAD_DIM))
    zeta_b = jnp.broadcast_to(zeta[:, :, None], (RET_HEADS, c, HEAD_DIM))
    gc_b = jnp.broadcast_to(gc[:, None, None], (RET_HEADS, 1, HEAD_DIM))
    return dmat, xi_b, zeta_b, gc_b


def _group_norm(o):
    mu = jnp.mean(o, axis=-1, keepdims=True)
    d = o - mu
    var = jnp.mean(d * d, axis=-1, keepdims=True)
    return d * lax.rsqrt(var + EPS)


def _ret_chunk(qh, kh, vh, r, dmat, xi, zeta, gc):
    qb = qh.astype(BF16)
    kb = kh.astype(BF16)
    vb = vh.astype(BF16)
    s = lax.dot_general(qb, kb, (((1,), (1,)), ((), ())), preferred_element_type=F32) * dmat
    o = jnp.dot(s.astype(BF16), vb, preferred_element_type=F32)
    o = o + jnp.dot(qb, r.astype(BF16), preferred_element_type=F32) * xi
    kz = (kh.astype(F32) * zeta).astype(BF16)
    r_new = r * gc + lax.dot_general(kz, vb, (((0,), (0,)), ((), ())), preferred_element_type=F32)
    return o, r_new


def _ret_sample_body(q_ref, k_ref, v_ref, gs_ref, st_ref, d_ref, xi_ref, zeta_ref, gc_ref, o_ref, nst_ref):
    nb = st_ref.shape[0]
    l = q_ref.shape[0] // nb
    for b in range(nb):
        rows = slice(b * l, (b + 1) * l)
        for hh in range(RET_HEADS):
            sl = slice(hh * HEAD_DIM, (hh + 1) * HEAD_DIM)
            o, r = _ret_chunk(q_ref[rows, sl], k_ref[rows, sl], v_ref[rows, sl], st_ref[b, hh],
                              d_ref[hh], xi_ref[hh], zeta_ref[hh], gc_ref[hh])
            o_ref[rows, sl] = (gs_ref[rows, sl] * _group_norm(o)).astype(o_ref.dtype)
            nst_ref[b, hh] = r


def _ret_sample(q, k, v, gs, state, l, nb):
    n = state.shape[0]
    dmat, xi, zeta, gc = _decay_tables(l)
    row = lambda b: (b, 0)
    c3 = lambda b: (0, 0, 0)
    blk4 = lambda b: (b, 0, 0, 0)
    return pl.pallas_call(
        _ret_sample_body,
        grid=(n // nb,),
        in_specs=[pl.BlockSpec((nb * l, RET_WIDTH), row)] * 4 + [
            pl.BlockSpec((nb, RET_HEADS, HEAD_DIM, HEAD_DIM), blk4),
            pl.BlockSpec((RET_HEADS, l, l), c3),
            pl.BlockSpec((RET_HEADS, l, HEAD_DIM), c3),
            pl.BlockSpec((RET_HEADS, l, HEAD_DIM), c3),
            pl.BlockSpec((RET_HEADS, 1, HEAD_DIM), c3),
        ],
        out_specs=[
            pl.BlockSpec((nb * l, RET_WIDTH), row),
            pl.BlockSpec((nb, RET_HEADS, HEAD_DIM, HEAD_DIM), blk4),
        ],
        out_shape=[
            jax.ShapeDtypeStruct((n * l, RET_WIDTH), BF16),
            jax.ShapeDtypeStruct((n, RET_HEADS, HEAD_DIM, HEAD_DIM), F32),
        ],
        compiler_params=_cparams(("parallel",)),
        name="ret_sample",
    )(q, k, v, gs, state, dmat, xi, zeta, gc)


def _split3(x):
    hi = x.astype(BF16)
    r1 = x - hi.astype(F32)
    mid = r1.astype(BF16)
    lo = (r1 - mid.astype(F32)).astype(BF16)
    return hi, mid, lo


def _dot_hp(t, w2_ref):
    t_hi, t_mid, _ = _split3(t)
    d = functools.partial(jnp.dot, preferred_element_type=F32)
    both = d(t_hi, w2_ref[...])
    return both[:, :LANES] + (d(t_mid, w2_ref[:, 0:LANES]) + both[:, LANES:])


def _iota_f32(shape, dim):
    return lax.broadcasted_iota(jnp.int32, shape, dim).astype(F32)


def _route(logits):
    lt = logits.T
    tm = lt.shape[1]
    row = _iota_f32((SUBLANES, tm), 0)
    big = float(SUBLANES)

    def rmax(x):
        return jnp.max(x, axis=0, keepdims=True)

    def first_row(mask):
        return jnp.min(jnp.where(mask, row, big), axis=0, keepdims=True)

    lg = jnp.where(row < float(N_GROUPS), lt[N_EXPERTS:N_EXPERTS + SUBLANES, :], -1e30)
    m = rmax(lg)
    g_top = 1.0 / jnp.sum(jnp.exp(lg - m), axis=0, keepdims=True)
    g_idx = first_row(lg == m)
    lem = lt[0:EXPERTS_PER_GROUP, :]
    for g in range(1, N_GROUPS):
        lem = jnp.where(g_idx == float(g), lt[g * EXPERTS_PER_GROUP:(g + 1) * EXPERTS_PER_GROUP, :], lem)
    pe = jnp.exp(lem - rmax(lem))
    p1 = rmax(pe)
    e1 = first_row(pe == p1)
    rest = row != e1
    pe2 = jnp.where(rest, pe, -1.0)
    p2 = rmax(pe2)
    e2 = first_row(rest & (pe2 == p2))
    scale = g_top / (p1 + p2)
    base = g_idx * float(EXPERTS_PER_GROUP)
    rec_t = jnp.where(row == 0.0, base + e1, jnp.where(row == 1.0, base + e2, 0.0))
    rec_t = rec_t + jnp.where(row == 2.0, p1 * scale, jnp.where(row == 3.0, p2 * scale, 0.0))
    return jnp.concatenate([rec_t, jnp.zeros((LANES - SUBLANES, tm), F32)], axis=0).T


def _outproj_body(c_ref, o_ref, x_ref, wo_ref, g2_ref, wr_ref, br_ref, x1_ref, t_ref, rec_ref):
    x1 = x_ref[...] + jnp.dot(c_ref[...], wo_ref[0:CONV_CH, :], preferred_element_type=F32)
    x1 = x1 + jnp.dot(o_ref[...], wo_ref[CONV_CH:, :], preferred_element_type=F32)
    x1_ref[...] = x1
    t = _rms(x1, g2_ref[...])
    t_ref[...] = t.astype(t_ref.dtype)
    rec_ref[...] = _route(_dot_hp(t, wr_ref) + br_ref[...])


def _outproj(c, o, x, w_out, g2, wr2, br, tm):
    t = x.shape[0]
    row = lambda i: (i, 0)
    const = lambda i: (0, 0)
    return pl.pallas_call(
        _outproj_body,
        grid=(t // tm,),
        in_specs=[
            pl.BlockSpec((tm, CONV_CH), row),
            pl.BlockSpec((tm, RET_WIDTH), row),
            pl.BlockSpec((tm, D_MODEL), row),
            pl.BlockSpec((D_MODEL, D_MODEL), const),
            pl.BlockSpec((1, D_MODEL), const),
            pl.BlockSpec((D_MODEL, 2 * LANES), const),
            pl.BlockSpec((1, LANES), const),
        ],
        out_specs=[
            pl.BlockSpec((tm, D_MODEL), row),
            pl.BlockSpec((tm, D_MODEL), row),
            pl.BlockSpec((tm, LANES), row),
        ],
        out_shape=[
            jax.ShapeDtypeStruct((t, D_MODEL), F32),
            jax.ShapeDtypeStruct((t, D_MODEL), BF16),
            jax.ShapeDtypeStruct((t, LANES), F32),
        ],
        compiler_params=_cparams(("parallel",)),
        name="outproj_router",
    )(c, o, x, w_out, g2, wr2, br)


def _mix_body(x_ref, g1_ref, w_ref, cos_ref, sin_ref, cw_ref, cb_ref, lg_ref, lb_ref,
              d_ref, xi_ref, zeta_ref, gc_ref, wo_ref, g2_ref, wr_ref, br_ref,
              x1_ref, t_ref, rec_ref, cst_ref, rst_ref, ext_ref, r_ref, o_ref):
    j = pl.program_id(1)
    tl = x_ref.shape[0]
    x = x_ref[...]
    xn = _rms(x, g1_ref[...]).astype(BF16)
    zc = jnp.dot(xn, w_ref[:, :2 * CONV_CH], preferred_element_type=F32)

    @pl.when(j == 0)
    def _():
        ext_ref[0:HALO, :] = jnp.zeros((HALO, CONV_CH), F32)
        ext_ref[tl + HALO:, :] = jnp.zeros((SUBLANES, CONV_CH), F32)
        r_ref[...] = jnp.zeros_like(r_ref)

    @pl.when(j > 0)
    def _():
        ext_ref[0:HALO, :] = ext_ref[tl:tl + HALO, :]

    ext_ref[HALO:tl + HALO, :] = zc[:, :CONV_CH] * jax.nn.sigmoid(zc[:, CONV_CH:])
    z = jnp.dot(xn, w_ref[:, 2 * CONV_CH:], preferred_element_type=F32)
    acc = _dwconv(lambda s, n: ext_ref[s:s + n, :], cw_ref, tl) + cb_ref[...]
    o_ref[:, 0:CONV_CH] = _ln_silu(acc, lg_ref[...], lb_ref[...]).astype(BF16)
    cst_ref[0] = ext_ref[tl + HALO_OFF:tl + HALO, :]

    cos = cos_ref[...]
    sin = sin_ref[...]
    q0 = 0
    k0 = q0 + RET_WIDTH
    v0 = k0 + RET_WIDTH
    g0 = v0 + RET_WIDTH
    for hh in range(RET_HEADS):
        lo = hh * HEAD_DIM
        qh = z[:, q0 + lo:q0 + lo + HEAD_DIM]
        kh = z[:, k0 + lo:k0 + lo + HEAD_DIM]
        qr = (qh * cos + pltpu.roll(qh, HEAD_DIM // 2, 1) * sin).astype(BF16)
        kr = ((kh * cos + pltpu.roll(kh, HEAD_DIM // 2, 1) * sin) * (HEAD_DIM ** -0.5)).astype(BF16)
        vh = z[:, v0 + lo:v0 + lo + HEAD_DIM].astype(BF16)
        g = z[:, g0 + lo:g0 + lo + HEAD_DIM]
        gs = g * jax.nn.sigmoid(g)
        r = r_ref[hh]
        for ci in range(tl // CHUNK):
            rows = slice(ci * CHUNK, (ci + 1) * CHUNK)
            o, r = _ret_chunk(qr[rows], kr[rows], vh[rows], r, d_ref[hh], xi_ref[hh], zeta_ref[hh], gc_ref[hh])
            o_ref[rows, CONV_CH + lo:CONV_CH + lo + HEAD_DIM] = (gs[rows] * _group_norm(o)).astype(BF16)
        r_ref[hh] = r
    rst_ref[0] = r_ref[...]

    x1 = x + jnp.dot(o_ref[...], wo_ref[...], preferred_element_type=F32)
    x1_ref[...] = x1
    t = _rms(x1, g2_ref[...])
    t_ref[...] = t.astype(t_ref.dtype)
    rec_ref[...] = _route(_dot_hp(t, wr_ref) + br_ref[...])


def _mix(x, g1, w_in, cos, sin, conv_w, conv_b, ln_g, ln_b, w_out, g2, wr2, br, n, l, tl):
    dmat, xi, zeta, gc = _decay_tables(CHUNK)
    per = l // tl
    row = lambda b, j: (b * per + j, 0)
    tab = lambda b, j: (j, 0)
    const = lambda b, j: (0, 0)
    c3 = lambda b, j: (0, 0, 0)
    once = dict(pipeline_mode=pl.Buffered(1))
    return pl.pallas_call(
        _mix_body,
        grid=(n, per),
        in_specs=[
            pl.BlockSpec((tl, D_MODEL), row),
            pl.BlockSpec((1, D_MODEL), const),
            pl.BlockSpec((D_MODEL, IN_COLS), const, **once),
            pl.BlockSpec((tl, HEAD_DIM), tab),
            pl.BlockSpec((tl, HEAD_DIM), tab),
            pl.BlockSpec((CONV_K, CONV_CH), const),
            pl.BlockSpec((1, CONV_CH), const),
            pl.BlockSpec((1, CONV_CH), const),
            pl.BlockSpec((1, CONV_CH), const),
            pl.BlockSpec((RET_HEADS, CHUNK, CHUNK), c3),
            pl.BlockSpec((RET_HEADS, CHUNK, HEAD_DIM), c3),
            pl.BlockSpec((RET_HEADS, CHUNK, HEAD_DIM), c3),
            pl.BlockSpec((RET_HEADS, 1, HEAD_DIM), c3),
            pl.BlockSpec((D_MODEL, D_MODEL), const, **once),
            pl.BlockSpec((1, D_MODEL), const),
            pl.BlockSpec((D_MODEL, 2 * LANES), const, **once),
            pl.BlockSpec((1, LANES), const),
        ],
        out_specs=[
            pl.BlockSpec((tl, D_MODEL), row),
            pl.BlockSpec((tl, D_MODEL), row),
            pl.BlockSpec((tl, LANES), row),
            pl.BlockSpec((1, CONV_K - 1, CONV_CH), lambda b, j: (b, 0, 0)),
            pl.BlockSpec((1, RET_HEADS, HEAD_DIM, HEAD_DIM), lambda b, j: (b, 0, 0, 0)),
        ],
        out_shape=[
            jax.ShapeDtypeStruct((n * l, D_MODEL), F32),
            jax.ShapeDtypeStruct((n * l, D_MODEL), BF16),
            jax.ShapeDtypeStruct((n * l, LANES), F32),
            jax.ShapeDtypeStruct((n, CONV_K - 1, CONV_CH), F32),
            jax.ShapeDtypeStruct((n, RET_HEADS, HEAD_DIM, HEAD_DIM), F32),
        ],
        scratch_shapes=[
            pltpu.VMEM((tl + HALO + SUBLANES, CONV_CH), F32),
            pltpu.VMEM((RET_HEADS, HEAD_DIM, HEAD_DIM), F32),
            pltpu.VMEM((tl, D_MODEL), BF16),
        ],
        compiler_params=pltpu.CompilerParams(dimension_semantics=("arbitrary", "arbitrary"),
                                             vmem_limit_bytes=MIX_VMEM_LIMIT),
        name="token_mix",
    )(x, g1, w_in, cos, sin, conv_w, conv_b, ln_g, ln_b, dmat, xi, zeta, gc, w_out, g2, wr2, br)


def _dispatch_body(rec_a_ref, t_a_ref, rec_b_ref, t_b_ref, s_ref, pos_ref, meta_ref, *, nsub_a):
    from_a = pl.program_id(0) * DISPATCH_SUBS < nsub_a
    for s in range(DISPATCH_SUBS):
        rows = slice(s * SUB, (s + 1) * SUB)
        rec = jnp.where(from_a, rec_a_ref[rows, :], rec_b_ref[rows, :])
        tok = jnp.where(from_a, t_a_ref[rows, :], t_b_ref[rows, :])
        chunks = slice(s * CHUNKS_PER_SUB, (s + 1) * CHUNKS_PER_SUB)
        _dispatch_sub_tile(rec, tok, s_ref.at[chunks], pos_ref.at[rows], meta_ref.at[s])


def _dispatch_sub_tile(rec, tok, s_ref, pos_ref, meta_ref):
    lane = _iota_f32(rec.shape, 1)
    a1 = lane == rec[:, 0:1]
    a2 = lane == rec[:, 1:2]
    a1f = jnp.where(a1, 1.0, 0.0)
    a2f = jnp.where(a2, 1.0, 0.0)
    ltri = jnp.where(_iota_f32((SUB, SUB), 1) < _iota_f32((SUB, SUB), 0), 1.0, 0.0).astype(BF16)
    c1 = jnp.dot(ltri, a1f.astype(BF16), preferred_element_type=F32)
    c2 = jnp.dot(ltri, a2f.astype(BF16), preferred_element_type=F32)
    n1 = jnp.sum(a1f, axis=0, keepdims=True)
    n2 = jnp.sum(a2f, axis=0, keepdims=True)
    cnt = jnp.floor((n1 + n2 + (ROW_ALIGN - 1.0)) * (1.0 / ROW_ALIGN))
    utri = jnp.where(_iota_f32((LANES, LANES), 0) < _iota_f32((LANES, LANES), 1), 1.0, 0.0).astype(BF16)
    start = jnp.dot(jnp.broadcast_to(cnt, (SUBLANES, LANES)).astype(BF16), utri,
                    preferred_element_type=F32)[0:1]
    base1 = start * ROW_ALIGN
    base2 = base1 + n1
    pos1 = jnp.sum(jnp.where(a1, c1 + base1, 0.0), axis=1, keepdims=True)
    pos2 = jnp.sum(jnp.where(a2, c2 + base2, 0.0), axis=1, keepdims=True)
    posm = jnp.where(lane == 0.0, pos1, jnp.where(lane == 1.0, pos2, 0.0))
    pos_ref[...] = posm
    row = _iota_f32((SUBLANES, LANES), 0)
    meta_ref[...] = jnp.where(row == 0.0, start, jnp.where(row == 1.0, cnt, 0.0))

    g1 = _split3(rec[:, 2:3])
    g2 = _split3(rec[:, 3:4])
    info = jnp.where(lane == 6.0, rec[:, 0:1], jnp.where(lane == 7.0, rec[:, 1:2], 0.0))
    for i in range(3):
        info = jnp.where(lane == float(i), g1[i].astype(F32), info)
        info = jnp.where(lane == float(3 + i), g2[i].astype(F32), info)
    src = jnp.concatenate([tok, info.astype(BF16)], axis=1)

    post = posm.T
    r = _iota_f32((SUBP, SUB), 0)
    onehot = jnp.where(r == post[0:1, :], 1.0, jnp.where(r == post[1:2, :], 1.0, 0.0)).astype(BF16)
    sorted_rows = jnp.dot(onehot, src, preferred_element_type=F32).astype(BF16)
    s_ref[...] = sorted_rows.reshape(CHUNKS_PER_SUB, ROW_ALIGN, ROW_W)


def _dispatch(rec_a, t_a, rec_b, t_b):
    nsub_a = rec_a.shape[0] // SUB
    nsub_b = rec_b.shape[0] // SUB
    nsub = nsub_a + nsub_b
    assert nsub_a % DISPATCH_SUBS == 0 and nsub_b % DISPATCH_SUBS == 0
    steps_a = nsub_a // DISPATCH_SUBS
    tm = DISPATCH_SUBS * SUB
    row = lambda i: (i, 0)
    row_a = lambda i: (jnp.minimum(i, steps_a - 1), 0)
    row_b = lambda i: (jnp.maximum(i - steps_a, 0), 0)
    return pl.pallas_call(
        functools.partial(_dispatch_body, nsub_a=nsub_a),
        grid=(nsub // DISPATCH_SUBS,),
        in_specs=[
            pl.BlockSpec((tm, LANES), row_a),
            pl.BlockSpec((tm, D_MODEL), row_a),
            pl.BlockSpec((tm, LANES), row_b),
            pl.BlockSpec((tm, D_MODEL), row_b),
        ],
        out_specs=[
            pl.BlockSpec((DISPATCH_SUBS * CHUNKS_PER_SUB, ROW_ALIGN, ROW_W), lambda i: (i, 0, 0)),
            pl.BlockSpec((tm, LANES), row),
            pl.BlockSpec((DISPATCH_SUBS, SUBLANES, LANES), lambda i: (i, 0, 0)),
        ],
        out_shape=[
            jax.ShapeDtypeStruct((nsub * CHUNKS_PER_SUB, ROW_ALIGN, ROW_W), BF16),
            jax.ShapeDtypeStruct((nsub * SUB, LANES), F32),
            jax.ShapeDtypeStruct((nsub, SUBLANES, LANES), F32),
        ],
        compiler_params=_cparams(("parallel",)),
        name="moe_dispatch",
    )(rec_a, t_a, rec_b, t_b)


def _experts_body(start_ref, cnt_ref, s_in, wg_ref, wu_ref, wd_ref, s_hbm,
                  xbuf, ybuf, wgu_ref, wdb_ref, gsem, ssem, list_ref, state_ref, *, nsub):
    del s_in
    e = pl.program_id(0)
    ne = pl.num_programs(0)
    par = e & 1

    list_max = list_ref.shape[0] // 2

    def gather_copy(src, i, slot):
        return pltpu.make_async_copy(s_hbm.at[src], xbuf.at[slot, i], gsem.at[slot])

    def scatter_copy(dst, i, slot):
        return pltpu.make_async_copy(ybuf.at[slot, i], s_hbm.at[dst], ssem.at[slot])

    def build_list(x, which):
        def per_sub(s, k):
            run = s * N_EXPERTS + x
            c = cnt_ref[run]
            base = s * CHUNKS_PER_SUB + start_ref[run]
            list_ref[k] = base
            list_ref[k + 1] = base + 1

            def per_chunk(i, carry):
                list_ref[k + i] = base + i
                return carry
            lax.fori_loop(2, c, per_chunk, 0)
            return k + c
        first = which * list_max
        state_ref[which] = lax.fori_loop(0, nsub, per_sub, first) - first

    def start_all(copy, which, first, n, slot, counter):
        def body(i, carry):
            copy(list_ref[which * list_max + first + i], i, slot).start()
            return carry
        lax.fori_loop(0, n, body, 0)
        state_ref[counter] = n

    def wait_all(copy, block_copy, slot, counter):
        n = state_ref[counter]

        @pl.when(n == CHUNKS_PER_BLK)
        def _():
            block_copy(slot).wait()

        @pl.when(n < CHUNKS_PER_BLK)
        def _():
            def body(i, carry):
                copy(0, 0, slot).wait()
                return carry
            lax.fori_loop(0, n, body, 0)
        state_ref[counter] = 0

    def gather_block(slot):
        return pltpu.make_async_copy(s_hbm.at[pl.ds(0, CHUNKS_PER_BLK)], xbuf.at[slot], gsem.at[slot])

    def scatter_block(slot):
        return pltpu.make_async_copy(ybuf.at[slot], s_hbm.at[pl.ds(0, CHUNKS_PER_BLK)], ssem.at[slot])

    def block_chunks(total, b):
        return jnp.minimum(total - b * CHUNKS_PER_BLK, CHUNKS_PER_BLK)

    @pl.when(e == 0)
    def _():
        for i in range(6):
            state_ref[i] = 0
        xbuf[...] = jnp.zeros_like(xbuf)
        build_list(0, 0)
        n0 = state_ref[0]

        @pl.when(n0 > 0)
        def _():
            start_all(gather_copy, 0, 0, block_chunks(n0, 0), 0, 2)

    @pl.when(e + 1 < ne)
    def _():
        build_list(e + 1, 1 - par)

    total = state_ref[par]
    nblk = (total + CHUNKS_PER_BLK - 1) // CHUNKS_PER_BLK
    wgu_ref[:, 0:EXPERT_FF] = wg_ref[0].astype(BF16)
    wgu_ref[:, EXPERT_FF:] = wu_ref[0].astype(BF16)
    wdb_ref[...] = wd_ref[0].astype(BF16)
    ef = e.astype(F32)

    def block(b, carry):
        slot = b & 1
        first = b * CHUNKS_PER_BLK

        @pl.when(b + 1 < nblk)
        def _():
            start_all(gather_copy, par, first + CHUNKS_PER_BLK, block_chunks(total, b + 1), 1 - slot, 3 - slot)

        wait_all(gather_copy, gather_block, slot, 2 + slot)
        wait_all(scatter_copy, scatter_block, slot, 4 + slot)
        x = xbuf[slot].reshape(MBLK, ROW_W)
        info = x[:, D_MODEL:].astype(F32)
        g_first = info[:, 0:1] + info[:, 1:2] + info[:, 2:3]
        g_second = info[:, 3:4] + info[:, 4:5] + info[:, 5:6]
        gate = jnp.where(info[:, 6:7] == ef, g_first, g_second)
        h = jnp.dot(x[:, :D_MODEL], wgu_ref[...], preferred_element_type=F32)
        h1 = h[:, :EXPERT_FF]
        hid = (h1 * jax.nn.sigmoid(h1)) * h[:, EXPERT_FF:] * gate
        y = jnp.dot(hid.astype(BF16), wdb_ref[...], preferred_element_type=F32).astype(BF16)
        ybuf[slot] = jnp.concatenate([y, x[:, D_MODEL:]], axis=1).reshape(CHUNKS_PER_BLK, ROW_ALIGN, ROW_W)
        start_all(scatter_copy, par, first, block_chunks(total, b), slot, 4 + slot)
        return carry

    lax.fori_loop(0, nblk, block, 0)

    @pl.when(e + 1 < ne)
    def _():
        n1 = state_ref[1 - par]

        @pl.when(n1 > 0)
        def _():
            start_all(gather_copy, 1 - par, 0, block_chunks(n1, 0), 0, 2)

    @pl.when(e == ne - 1)
    def _():
        wait_all(scatter_copy, scatter_block, 0, 4)
        wait_all(scatter_copy, scatter_block, 1, 5)


def _experts(start, cnt, staged, wg, wu, wd):
    nsub = staged.shape[0] // CHUNKS_PER_SUB
    list_max = nsub * SUB // ROW_ALIGN + nsub + LIST_SLACK
    wblk = lambda e, *_: (e, 0, 0)
    grid_spec = pltpu.PrefetchScalarGridSpec(
        num_scalar_prefetch=2,
        grid=(N_EXPERTS,),
        in_specs=[
            pl.BlockSpec(memory_space=pl.ANY),
            pl.BlockSpec((1, D_MODEL, EXPERT_FF), wblk),
            pl.BlockSpec((1, D_MODEL, EXPERT_FF), wblk),
            pl.BlockSpec((1, EXPERT_FF, D_MODEL), wblk),
        ],
        out_specs=pl.BlockSpec(memory_space=pl.ANY),
        scratch_shapes=[
            pltpu.VMEM((2, CHUNKS_PER_BLK, ROW_ALIGN, ROW_W), BF16),
            pltpu.VMEM((2, CHUNKS_PER_BLK, ROW_ALIGN, ROW_W), BF16),
            pltpu.VMEM((D_MODEL, 2 * EXPERT_FF), BF16),
            pltpu.VMEM((EXPERT_FF, D_MODEL), BF16),
            pltpu.SemaphoreType.DMA((2,)),
            pltpu.SemaphoreType.DMA((2,)),
            pltpu.SMEM((2 * list_max,), jnp.int32),
            pltpu.SMEM((6,), jnp.int32),
        ],
    )
    return pl.pallas_call(
        functools.partial(_experts_body, nsub=nsub),
        grid_spec=grid_spec,
        out_shape=jax.ShapeDtypeStruct(staged.shape, staged.dtype),
        input_output_aliases={2: 0},
        compiler_params=_cparams(("arbitrary",)),
        name="moe_experts",
    )(start, cnt, staged, wg, wu, wd)


def _combine_body(ys_ref, pos_ref, x1_ref, p_ref, wp_ref, gp_ref, wpg_ref, gf_ref, y_ref):
    r = _iota_f32((SUB, SUBP), 1)
    moe = []
    for s in range(x1_ref.shape[0] // SUB):
        p1 = pos_ref[s * SUB:(s + 1) * SUB, 0:1]
        p2 = pos_ref[s * SUB:(s + 1) * SUB, 1:2]
        onehot = jnp.where(r == p1, 1.0, jnp.where(r == p2, 1.0, 0.0)).astype(BF16)
        ys = ys_ref[s * CHUNKS_PER_SUB:(s + 1) * CHUNKS_PER_SUB].reshape(SUBP, D_MODEL)
        moe.append(jnp.dot(onehot, ys, preferred_element_type=F32))
    x2 = x1_ref[...] + jnp.concatenate(moe, axis=0)
    ple = _rms(jnp.dot(p_ref[...].astype(BF16), wp_ref[...], preferred_element_type=F32), gp_ref[...])
    gate = jax.nn.sigmoid(jnp.dot(x2.astype(BF16), wpg_ref[...], preferred_element_type=F32))
    y_ref[...] = _rms(x2 + ple * gate, gf_ref[...])


def _combine(ys, pos, x1, p, w_ple, gp, w_ple_gate, gf, sub_off):
    t = x1.shape[0]
    tm = COMBINE_SUBS * SUB
    blk_off = sub_off // COMBINE_SUBS
    assert sub_off % COMBINE_SUBS == 0 and t % tm == 0
    row = lambda i: (i, 0)
    const = lambda i: (0, 0)
    return pl.pallas_call(
        _combine_body,
        grid=(t // tm,),
        in_specs=[
            pl.BlockSpec((COMBINE_SUBS * CHUNKS_PER_SUB, ROW_ALIGN, D_MODEL), lambda i: (i + blk_off, 0, 0)),
            pl.BlockSpec((tm, LANES), lambda i: (i + blk_off, 0)),
            pl.BlockSpec((tm, D_MODEL), row),
            pl.BlockSpec((tm, PLE_DIM), row),
            pl.BlockSpec((PLE_DIM, D_MODEL), const),
            pl.BlockSpec((1, D_MODEL), const),
            pl.BlockSpec((D_MODEL, D_MODEL), const),
            pl.BlockSpec((1, D_MODEL), const),
        ],
        out_specs=pl.BlockSpec((tm, D_MODEL), row),
        out_shape=jax.ShapeDtypeStruct((t, D_MODEL), F32),
        compiler_params=_cparams(("parallel",)),
        name="moe_combine_ple",
    )(ys, pos, x1, p, w_ple, gp, w_ple_gate, gf)


def _rope_tables(pos):
    half = HEAD_DIM // 2
    inv = ROPE_BASE ** (-jnp.arange(half, dtype=F32) / half)
    ang = pos[:, None] * inv[None, :]
    cos = jnp.cos(ang)
    sin = jnp.sin(ang)
    return jnp.concatenate([cos, cos], axis=-1), jnp.concatenate([-sin, sin], axis=-1)


def _router_params(we, be, wg, bg):
    pad = LANES - N_EXPERTS - N_GROUPS
    w = jnp.pad(jnp.concatenate([we, wg], axis=1), ((0, 0), (0, pad)))
    b = jnp.pad(jnp.concatenate([be, bg]), (0, pad))[None, :]
    return jnp.concatenate(_split3(w)[:2], axis=1), b


def kernel(x_prompt, x_sample, p_prompt, p_sample, state_conv, state_ret, w_in, conv_w, conv_b, conv_ln_g, conv_ln_b, w_out, norm1_g, norm2_g, router_group_w, router_group_b, router_expert_w, router_expert_b, w_expert_gate, w_expert_up, w_expert_down, w_ple, ple_norm_g, w_ple_gate, final_norm_g):
    assert w_in.shape[0] == 1, "single-layer trunk"
    nb, seq, _ = x_prompt.shape
    ns, dseq, _ = x_sample.shape
    tm = SAMPLE_TILE

    w_in_b = w_in[0].astype(BF16)
    w_out_b = w_out[0].astype(BF16)
    w_ple_b = w_ple[0].astype(BF16)
    w_pg_b = w_ple_gate[0].astype(BF16)
    g1 = norm1_g[0][None, :]
    g2 = norm2_g[0][None, :]
    gp = ple_norm_g[0][None, :]
    gf = final_norm_g[None, :]
    cb = conv_b[0][None, :]
    lng = conv_ln_g[0][None, :]
    lnb = conv_ln_b[0][None, :]
    wr2, br = _router_params(router_expert_w[0], router_expert_b[0], router_group_w[0], router_group_b[0])

    cos_p, sin_p = _rope_tables(jnp.arange(seq, dtype=F32) + jnp.float32(0))
    pos_s = jnp.tile(jnp.arange(dseq, dtype=F32) + jnp.float32(PAST_LEN), tm // dseq)
    cos_s, sin_s = _rope_tables(pos_s)

    xp = x_prompt.reshape(nb * seq, D_MODEL)
    x1_p, t_p, rec_p, conv_p, ret_p = _mix(xp, g1, w_in_b, cos_p, sin_p, conv_w[0], cb, lng, lnb, w_out_b, g2,
                                           wr2, br, nb, seq, MIX_TILE)

    xs = x_sample.reshape(ns * dseq, D_MODEL)
    u, q, k, v, gs = _inproj(xs, g1, w_in_b, cos_s, sin_s, tm)
    c, conv_s = _conv_sample(u.reshape(ns, dseq, CONV_CH), jnp.transpose(state_conv[0], (1, 0, 2)), conv_w[0], cb,
                             lng, lnb, CONV_SAMPLE_SEQS)
    conv_s = jnp.transpose(conv_s, (1, 0, 2))
    o, ret_s = _ret_sample(q, k, v, gs, state_ret[0], dseq, RET_SAMPLE_SEQS)
    x1_s, t_s, rec_s = _outproj(c.reshape(ns * dseq, CONV_CH), o, xs, w_out_b, g2, wr2, br, tm)

    staged, pos, meta = _dispatch(rec_p, t_p, rec_s, t_s)
    start = meta[:, 0, :N_EXPERTS].astype(jnp.int32).reshape(-1)
    cnt = meta[:, 1, :N_EXPERTS].astype(jnp.int32).reshape(-1)
    ys = _experts(start, cnt, staged, w_expert_gate[0], w_expert_up[0], w_expert_down[0])

    y_p = _combine(ys, pos, x1_p, p_prompt[0].reshape(nb * seq, PLE_DIM), w_ple_b, gp, w_pg_b, gf, 0)
    y_s = _combine(ys, pos, x1_s, p_sample[0].reshape(ns * dseq, PLE_DIM), w_ple_b, gp, w_pg_b, gf,
                   nb * seq // SUB)

    return (y_p.reshape(nb, seq, D_MODEL), y_s.reshape(ns, dseq, D_MODEL),
            conv_p[None], ret_p[None], conv_s[None], ret_s[None])
```

```python
import functools

import jax
import jax.numpy as jnp
from jax import lax
from jax.experimental import pallas as pl
from jax.experimental.pallas import tpu as pltpu

F32 = jnp.float32
BF16 = jnp.bfloat16

D_MODEL = 1024
PLE_DIM = 256
CONV_CH = 512
CONV_K = 31
RET_WIDTH = 512
RET_HEADS = 4
HEAD_DIM = 128
CHUNK = 128
ROPE_BASE = 10000.0
N_GROUPS = 4
EXPERTS_PER_GROUP = 8
N_EXPERTS = 32
EXPERT_FF = 256
IN_COLS = 3072
EPS = 1e-6
PAST_LEN = 16384

LANES = 128
SUBLANES = 8
HALO = 32
HALO_OFF = HALO - (CONV_K - 1)
VMEM_LIMIT = 48 * 1024 * 1024
MIX_VMEM_LIMIT = 56 * 1024 * 1024
MIX_TILE = 512
SAMPLE_TILE = 1024
CONV_SAMPLE_SEQS = 32
RET_SAMPLE_SEQS = 16

SUB = 256
ROW_ALIGN = 16
PBLK = 256
SUBP = -(-(2 * SUB + N_EXPERTS * (ROW_ALIGN - 1)) // PBLK) * PBLK
CHUNKS_PER_SUB = SUBP // ROW_ALIGN
ROW_W = D_MODEL + LANES
MBLK = 512
CHUNKS_PER_BLK = MBLK // ROW_ALIGN
LIST_SLACK = 2
COMBINE_SUBS = 4
COMBINE_SAMPLE_SUBS = 1
DISPATCH_SUBS = 4


def _cparams(sem):
    return pltpu.CompilerParams(dimension_semantics=sem, vmem_limit_bytes=VMEM_LIMIT)


def _rms(x, g):
    return x * lax.rsqrt(jnp.mean(x * x, axis=-1, keepdims=True) + EPS) * g


def _inproj_body(x_ref, g1_ref, w_ref, cos_ref, sin_ref, u_ref, q_ref, k_ref, v_ref, gs_ref):
    h = _rms(x_ref[...], g1_ref[...]).astype(BF16)
    z = jnp.dot(h, w_ref[...], preferred_element_type=F32)
    a = z[:, :CONV_CH]
    b = z[:, CONV_CH:2 * CONV_CH]
    u_ref[...] = a * jax.nn.sigmoid(b)
    cos = cos_ref[...]
    sin = sin_ref[...]
    q0 = 2 * CONV_CH
    k0 = q0 + RET_WIDTH
    for hh in range(RET_HEADS):
        sl = slice(hh * HEAD_DIM, (hh + 1) * HEAD_DIM)
        qh = z[:, q0 + hh * HEAD_DIM:q0 + (hh + 1) * HEAD_DIM]
        kh = z[:, k0 + hh * HEAD_DIM:k0 + (hh + 1) * HEAD_DIM]
        q_ref[:, sl] = qh * cos + pltpu.roll(qh, HEAD_DIM // 2, 1) * sin
        k_ref[:, sl] = (kh * cos + pltpu.roll(kh, HEAD_DIM // 2, 1) * sin) * (HEAD_DIM ** -0.5)
    v_ref[...] = z[:, k0 + RET_WIDTH:k0 + 2 * RET_WIDTH]
    g = z[:, k0 + 2 * RET_WIDTH:]
    gs_ref[...] = g * jax.nn.sigmoid(g)


def _inproj(x, g1, w_in, cos, sin, tm):
    t = x.shape[0]
    row = lambda i: (i, 0)
    const = lambda i: (0, 0)
    return pl.pallas_call(
        _inproj_body,
        grid=(t // tm,),
        in_specs=[
            pl.BlockSpec((tm, D_MODEL), row),
            pl.BlockSpec((1, D_MODEL), const),
            pl.BlockSpec((D_MODEL, IN_COLS), const),
            pl.BlockSpec((tm, HEAD_DIM), const),
            pl.BlockSpec((tm, HEAD_DIM), const),
        ],
        out_specs=[pl.BlockSpec((tm, CONV_CH), row)] + [pl.BlockSpec((tm, RET_WIDTH), row)] * 4,
        out_shape=[jax.ShapeDtypeStruct((t, CONV_CH), F32)] + [jax.ShapeDtypeStruct((t, RET_WIDTH), F32)] * 4,
        compiler_params=_cparams(("parallel",)),
        name="inproj",
    )(x, g1, w_in, cos, sin)


def _ln_silu(acc, g, b):
    mu = jnp.mean(acc, axis=-1, keepdims=True)
    d = acc - mu
    var = jnp.mean(d * d, axis=-1, keepdims=True)
    y = d * lax.rsqrt(var + EPS) * g + b
    return y * jax.nn.sigmoid(y)


def _dwconv(load, w_ref, rows):
    acc = None
    for b in range(SUBLANES):
        part = None
        for a in range((CONV_K + HALO_OFF) // SUBLANES + 1):
            k = SUBLANES * a + b - HALO_OFF
            if 0 <= k < CONV_K:
                term = load(SUBLANES * a, rows + SUBLANES) * w_ref[k:k + 1, :]
                part = term if part is None else part + term
        if part is not None:
            shifted = part[b:b + rows]
            acc = shifted if acc is None else acc + shifted
    return acc


def _conv_sample_body(u_ref, st_ref, w_ref, cb_ref, lg_ref, lb_ref, c_ref, nst_ref, ext_ref):
    l = u_ref.shape[1]
    hist = CONV_K - 1
    ext_ref[0:hist] = st_ref[...]
    ext_ref[hist:] = jnp.transpose(u_ref[...], (1, 0, 2))
    acc = ext_ref[0:l] * w_ref[0:1, :] + cb_ref[...]
    for k in range(1, CONV_K):
        acc = acc + ext_ref[k:k + l] * w_ref[k:k + 1, :]
    y = _ln_silu(acc, lg_ref[...], lb_ref[...])
    c_ref[...] = jnp.transpose(y, (1, 0, 2)).astype(c_ref.dtype)
    nst_ref[...] = ext_ref[l:]


def _conv_sample(u, state_t, conv_w, conv_b, ln_g, ln_b, nb):
    n, l, _ = u.shape
    hist = CONV_K - 1
    const = lambda b: (0, 0)
    tok = lambda b: (b, 0, 0)
    tmaj = lambda b: (0, b, 0)
    return pl.pallas_call(
        _conv_sample_body,
        grid=(n // nb,),
        in_specs=[
            pl.BlockSpec((nb, l, CONV_CH), tok),
            pl.BlockSpec((hist, nb, CONV_CH), tmaj),
            pl.BlockSpec((CONV_K, CONV_CH), const),
            pl.BlockSpec((1, CONV_CH), const),
            pl.BlockSpec((1, CONV_CH), const),
            pl.BlockSpec((1, CONV_CH), const),
        ],
        out_specs=[
            pl.BlockSpec((nb, l, CONV_CH), tok),
            pl.BlockSpec((hist, nb, CONV_CH), tmaj),
        ],
        out_shape=[
            jax.ShapeDtypeStruct((n, l, CONV_CH), BF16),
            jax.ShapeDtypeStruct((hist, n, CONV_CH), F32),
        ],
        scratch_shapes=[pltpu.VMEM((hist + l, nb, CONV_CH), F32)],
        compiler_params=_cparams(("parallel",)),
        name="conv_sample",
    )(u, state_t, conv_w, conv_b, ln_g, ln_b)


def _decay_tables(c):
    lg = jnp.log(1.0 - 2.0 ** (-5.0 - jnp.arange(RET_HEADS, dtype=F32)))
    idx = jnp.arange(c, dtype=F32)
    rel = idx[:, None] - idx[None, :]
    dmat = jnp.where(rel[None] >= 0, jnp.exp(jnp.maximum(rel, 0.0)[None] * lg[:, None, None]), 0.0)
    xi = jnp.exp((idx + 1.0)[None, :] * lg[:, None])
    zeta = jnp.exp((c - 1.0 - idx)[None, :] * lg[:, None])
    gc = jnp.exp(c * lg)
    xi_b = jnp.broadcast_to(xi[:, :, None], (RET_HEADS, c, HEAD_DIM))
    zeta_b = jnp.broadcast_to(zeta[:, :, None], (RET_HEADS, c, HEAD_DIM))
    gc_b = jnp.broadcast_to(gc[:, None, None], (RET_HEADS, 1, HEAD_DIM))
    return dmat, xi_b, zeta_b, gc_b


def _group_norm(o):
    mu = jnp.mean(o, axis=-1, keepdims=True)
    d = o - mu
    var = jnp.mean(d * d, axis=-1, keepdims=True)
    return d * lax.rsqrt(var + EPS)


def _ret_chunk(qh, kh, vh, r, dmat, xi, zeta, gc):
    qb = qh.astype(BF16)
    kb = kh.astype(BF16)
    vb = vh.astype(BF16)
    s = lax.dot_general(qb, kb, (((1,), (1,)), ((), ())), preferred_element_type=F32) * dmat
    o = jnp.dot(s.astype(BF16), vb, preferred_element_type=F32)
    o = o + jnp.dot(qb, r.astype(BF16), preferred_element_type=F32) * xi
    kz = (kh.astype(F32) * zeta).astype(BF16)
    r_new = r * gc + lax.dot_general(kz, vb, (((0,), (0,)), ((), ())), preferred_element_type=F32)
    return o, r_new


def _ret_sample_body(q_ref, k_ref, v_ref, gs_ref, st_ref, d_ref, xi_ref, zeta_ref, gc_ref, o_ref, nst_ref):
    nb = st_ref.shape[0]
    l = q_ref.shape[0] // nb
    for b in range(nb):
        rows = slice(b * l, (b + 1) * l)
        for hh in range(RET_HEADS):
            sl = slice(hh * HEAD_DIM, (hh + 1) * HEAD_DIM)
            o, r = _ret_chunk(q_ref[rows, sl], k_ref[rows, sl], v_ref[rows, sl], st_ref[b, hh],
                              d_ref[hh], xi_ref[hh], zeta_ref[hh], gc_ref[hh])
            o_ref[rows, sl] = (gs_ref[rows, sl] * _group_norm(o)).astype(o_ref.dtype)
            nst_ref[b, hh] = r


def _ret_sample(q, k, v, gs, state, l, nb):
    n = state.shape[0]
    dmat, xi, zeta, gc = _decay_tables(l)
    row = lambda b: (b, 0)
    c3 = lambda b: (0, 0, 0)
    blk4 = lambda b: (b, 0, 0, 0)
    return pl.pallas_call(
        _ret_sample_body,
        grid=(n // nb,),
        in_specs=[pl.BlockSpec((nb * l, RET_WIDTH), row)] * 4 + [
            pl.BlockSpec((nb, RET_HEADS, HEAD_DIM, HEAD_DIM), blk4),
            pl.BlockSpec((RET_HEADS, l, l), c3),
            pl.BlockSpec((RET_HEADS, l, HEAD_DIM), c3),
            pl.BlockSpec((RET_HEADS, l, HEAD_DIM), c3),
            pl.BlockSpec((RET_HEADS, 1, HEAD_DIM), c3),
        ],
        out_specs=[
            pl.BlockSpec((nb * l, RET_WIDTH), row),
            pl.BlockSpec((nb, RET_HEADS, HEAD_DIM, HEAD_DIM), blk4),
        ],
        out_shape=[
            jax.ShapeDtypeStruct((n * l, RET_WIDTH), BF16),
            jax.ShapeDtypeStruct((n, RET_HEADS, HEAD_DIM, HEAD_DIM), F32),
        ],
        compiler_params=_cparams(("parallel",)),
        name="ret_sample",
    )(q, k, v, gs, state, dmat, xi, zeta, gc)


def _split3(x):
    hi = x.astype(BF16)
    r1 = x - hi.astype(F32)
    mid = r1.astype(BF16)
    lo = (r1 - mid.astype(F32)).astype(BF16)
    return hi, mid, lo


def _dot_hp(t, w2_ref):
    t_hi, t_mid, _ = _split3(t)
    d = functools.partial(jnp.dot, preferred_element_type=F32)
    both = d(t_hi, w2_ref[...])
    return both[:, :LANES] + (d(t_mid, w2_ref[:, 0:LANES]) + both[:, LANES:])


def _iota_f32(shape, dim):
    return lax.broadcasted_iota(jnp.int32, shape, dim).astype(F32)


def _route(logits):
    lt = logits.T
    tm = lt.shape[1]
    row = _iota_f32((SUBLANES, tm), 0)
    big = float(SUBLANES)

    def rmax(x):
        return jnp.max(x, axis=0, keepdims=True)

    def first_row(mask):
        return jnp.min(jnp.where(mask, row, big), axis=0, keepdims=True)

    lg = jnp.where(row < float(N_GROUPS), lt[N_EXPERTS:N_EXPERTS + SUBLANES, :], -1e30)
    m = rmax(lg)
    g_top = 1.0 / jnp.sum(jnp.exp(lg - m), axis=0, keepdims=True)
    g_idx = first_row(lg == m)
    lem = lt[0:EXPERTS_PER_GROUP, :]
    for g in range(1, N_GROUPS):
        lem = jnp.where(g_idx == float(g), lt[g * EXPERTS_PER_GROUP:(g + 1) * EXPERTS_PER_GROUP, :], lem)
    pe = jnp.exp(lem - rmax(lem))
    p1 = rmax(pe)
    e1 = first_row(pe == p1)
    rest = row != e1
    pe2 = jnp.where(rest, pe, -1.0)
    p2 = rmax(pe2)
    e2 = first_row(rest & (pe2 == p2))
    scale = g_top / (p1 + p2)
    base = g_idx * float(EXPERTS_PER_GROUP)
    rec_t = jnp.where(row == 0.0, base + e1, jnp.where(row == 1.0, base + e2, 0.0))
    rec_t = rec_t + jnp.where(row == 2.0, p1 * scale, jnp.where(row == 3.0, p2 * scale, 0.0))
    return jnp.concatenate([rec_t, jnp.zeros((LANES - SUBLANES, tm), F32)], axis=0).T


def _outproj_body(c_ref, o_ref, x_ref, wo_ref, g2_ref, wr_ref, br_ref, x1_ref, t_ref, rec_ref):
    x1 = x_ref[...] + jnp.dot(c_ref[...], wo_ref[0:CONV_CH, :], preferred_element_type=F32)
    x1 = x1 + jnp.dot(o_ref[...], wo_ref[CONV_CH:, :], preferred_element_type=F32)
    x1_ref[...] = x1
    t = _rms(x1, g2_ref[...])
    t_ref[...] = t.astype(t_ref.dtype)
    rec_ref[...] = _route(_dot_hp(t, wr_ref) + br_ref[...])


def _outproj(c, o, x, w_out, g2, wr2, br, tm):
    t = x.shape[0]
    row = lambda i: (i, 0)
    const = lambda i: (0, 0)
    return pl.pallas_call(
        _outproj_body,
        grid=(t // tm,),
        in_specs=[
            pl.BlockSpec((tm, CONV_CH), row),
            pl.BlockSpec((tm, RET_WIDTH), row),
            pl.BlockSpec((tm, D_MODEL), row),
            pl.BlockSpec((D_MODEL, D_MODEL), const),
            pl.BlockSpec((1, D_MODEL), const),
            pl.BlockSpec((D_MODEL, 2 * LANES), const),
            pl.BlockSpec((1, LANES), const),
        ],
        out_specs=[
            pl.BlockSpec((tm, D_MODEL), row),
            pl.BlockSpec((tm, D_MODEL), row),
            pl.BlockSpec((tm, LANES), row),
        ],
        out_shape=[
            jax.ShapeDtypeStruct((t, D_MODEL), F32),
            jax.ShapeDtypeStruct((t, D_MODEL), BF16),
            jax.ShapeDtypeStruct((t, LANES), F32),
        ],
        compiler_params=_cparams(("parallel",)),
        name="outproj_router",
    )(c, o, x, w_out, g2, wr2, br)


def _mix_body(x_ref, g1_ref, w_ref, cos_ref, sin_ref, cw_ref, cb_ref, lg_ref, lb_ref,
              d_ref, xi_ref, zeta_ref, gc_ref, wo_ref, g2_ref, wr_ref, br_ref,
              x1_ref, t_ref, rec_ref, cst_ref, rst_ref, ext_ref, r_ref, o_ref):
    j = pl.program_id(1)
    tl = x_ref.shape[0]
    x = x_ref[...]
    z = jnp.dot(_rms(x, g1_ref[...]).astype(BF16), w_ref[...], preferred_element_type=F32)

    @pl.when(j == 0)
    def _():
        ext_ref[0:HALO, :] = jnp.zeros((HALO, CONV_CH), F32)
        ext_ref[tl + HALO:, :] = jnp.zeros((SUBLANES, CONV_CH), F32)
        r_ref[...] = jnp.zeros_like(r_ref)

    @pl.when(j > 0)
    def _():
        ext_ref[0:HALO, :] = ext_ref[tl:tl + HALO, :]

    ext_ref[HALO:tl + HALO, :] = z[:, :CONV_CH] * jax.nn.sigmoid(z[:, CONV_CH:2 * CONV_CH])
    acc = _dwconv(lambda s, n: ext_ref[s:s + n, :], cw_ref, tl) + cb_ref[...]
    o_ref[:, 0:CONV_CH] = _ln_silu(acc, lg_ref[...], lb_ref[...]).astype(BF16)
    cst_ref[0] = ext_ref[tl + HALO_OFF:tl + HALO, :]

    cos = cos_ref[...]
    sin = sin_ref[...]
    q0 = 2 * CONV_CH
    k0 = q0 + RET_WIDTH
    v0 = k0 + RET_WIDTH
    g0 = v0 + RET_WIDTH
    for hh in range(RET_HEADS):
        lo = hh * HEAD_DIM
        qh = z[:, q0 + lo:q0 + lo + HEAD_DIM]
        kh = z[:, k0 + lo:k0 + lo + HEAD_DIM]
        qr = (qh * cos + pltpu.roll(qh, HEAD_DIM // 2, 1) * sin).astype(BF16)
        kr = ((kh * cos + pltpu.roll(kh, HEAD_DIM // 2, 1) * sin) * (HEAD_DIM ** -0.5)).astype(BF16)
        vh = z[:, v0 + lo:v0 + lo + HEAD_DIM].astype(BF16)
        g = z[:, g0 + lo:g0 + lo + HEAD_DIM]
        gs = g * jax.nn.sigmoid(g)
        r = r_ref[hh]
        for ci in range(tl // CHUNK):
            rows = slice(ci * CHUNK, (ci + 1) * CHUNK)
            o, r = _ret_chunk(qr[rows], kr[rows], vh[rows], r, d_ref[hh], xi_ref[hh], zeta_ref[hh], gc_ref[hh])
            o_ref[rows, CONV_CH + lo:CONV_CH + lo + HEAD_DIM] = (gs[rows] * _group_norm(o)).astype(BF16)
        r_ref[hh] = r
    rst_ref[0] = r_ref[...]

    x1 = x + jnp.dot(o_ref[...], wo_ref[...], preferred_element_type=F32)
    x1_ref[...] = x1
    t = _rms(x1, g2_ref[...])
    t_ref[...] = t.astype(t_ref.dtype)
    rec_ref[...] = _route(_dot_hp(t, wr_ref) + br_ref[...])


def _mix(x, g1, w_in, cos, sin, conv_w, conv_b, ln_g, ln_b, w_out, g2, wr2, br, n, l, tl):
    dmat, xi, zeta, gc = _decay_tables(CHUNK)
    per = l // tl
    row = lambda b, j: (b * per + j, 0)
    tab = lambda b, j: (j, 0)
    const = lambda b, j: (0, 0)
    c3 = lambda b, j: (0, 0, 0)
    once = dict(pipeline_mode=pl.Buffered(1))
    return pl.pallas_call(
        _mix_body,
        grid=(n, per),
        in_specs=[
            pl.BlockSpec((tl, D_MODEL), row),
            pl.BlockSpec((1, D_MODEL), const),
            pl.BlockSpec((D_MODEL, IN_COLS), const, **once),
            pl.BlockSpec((tl, HEAD_DIM), tab),
            pl.BlockSpec((tl, HEAD_DIM), tab),
            pl.BlockSpec((CONV_K, CONV_CH), const),
            pl.BlockSpec((1, CONV_CH), const),
            pl.BlockSpec((1, CONV_CH), const),
            pl.BlockSpec((1, CONV_CH), const),
            pl.BlockSpec((RET_HEADS, CHUNK, CHUNK), c3),
            pl.BlockSpec((RET_HEADS, CHUNK, HEAD_DIM), c3),
            pl.BlockSpec((RET_HEADS, CHUNK, HEAD_DIM), c3),
            pl.BlockSpec((RET_HEADS, 1, HEAD_DIM), c3),
            pl.BlockSpec((D_MODEL, D_MODEL), const, **once),
            pl.BlockSpec((1, D_MODEL), const),
            pl.BlockSpec((D_MODEL, 2 * LANES), const, **once),
            pl.BlockSpec((1, LANES), const),
        ],
        out_specs=[
            pl.BlockSpec((tl, D_MODEL), row),
            pl.BlockSpec((tl, D_MODEL), row),
            pl.BlockSpec((tl, LANES), row),
            pl.BlockSpec((1, CONV_K - 1, CONV_CH), lambda b, j: (b, 0, 0)),
            pl.BlockSpec((1, RET_HEADS, HEAD_DIM, HEAD_DIM), lambda b, j: (b, 0, 0, 0)),
        ],
        out_shape=[
            jax.ShapeDtypeStruct((n * l, D_MODEL), F32),
            jax.ShapeDtypeStruct((n * l, D_MODEL), BF16),
            jax.ShapeDtypeStruct((n * l, LANES), F32),
            jax.ShapeDtypeStruct((n, CONV_K - 1, CONV_CH), F32),
            jax.ShapeDtypeStruct((n, RET_HEADS, HEAD_DIM, HEAD_DIM), F32),
        ],
        scratch_shapes=[
            pltpu.VMEM((tl + HALO + SUBLANES, CONV_CH), F32),
            pltpu.VMEM((RET_HEADS, HEAD_DIM, HEAD_DIM), F32),
            pltpu.VMEM((tl, D_MODEL), BF16),
        ],
        compiler_params=pltpu.CompilerParams(dimension_semantics=("arbitrary", "arbitrary"),
                                             vmem_limit_bytes=MIX_VMEM_LIMIT),
        name="token_mix",
    )(x, g1, w_in, cos, sin, conv_w, conv_b, ln_g, ln_b, dmat, xi, zeta, gc, w_out, g2, wr2, br)


def _dispatch_body(rec_a_ref, t_a_ref, rec_b_ref, t_b_ref, s_ref, pos_ref, meta_ref, *, nsub_a):
    from_a = pl.program_id(0) * DISPATCH_SUBS < nsub_a
    for s in range(DISPATCH_SUBS):
        rows = slice(s * SUB, (s + 1) * SUB)
        rec = jnp.where(from_a, rec_a_ref[rows, :], rec_b_ref[rows, :])
        tok = jnp.where(from_a, t_a_ref[rows, :], t_b_ref[rows, :])
        chunks = slice(s * CHUNKS_PER_SUB, (s + 1) * CHUNKS_PER_SUB)
        _dispatch_sub_tile(rec, tok, s_ref.at[chunks], pos_ref.at[rows], meta_ref.at[s])


def _dispatch_sub_tile(rec, tok, s_ref, pos_ref, meta_ref):
    lane = _iota_f32(rec.shape, 1)
    a1 = lane == rec[:, 0:1]
    a2 = lane == rec[:, 1:2]
    a1f = jnp.where(a1, 1.0, 0.0)
    a2f = jnp.where(a2, 1.0, 0.0)
    ltri = jnp.where(_iota_f32((SUB, SUB), 1) < _iota_f32((SUB, SUB), 0), 1.0, 0.0).astype(BF16)
    c1 = jnp.dot(ltri, a1f.astype(BF16), preferred_element_type=F32)
    c2 = jnp.dot(ltri, a2f.astype(BF16), preferred_element_type=F32)
    n1 = jnp.sum(a1f, axis=0, keepdims=True)
    n2 = jnp.sum(a2f, axis=0, keepdims=True)
    cnt = jnp.floor((n1 + n2 + (ROW_ALIGN - 1.0)) * (1.0 / ROW_ALIGN))
    utri = jnp.where(_iota_f32((LANES, LANES), 0) < _iota_f32((LANES, LANES), 1), 1.0, 0.0).astype(BF16)
    start = jnp.dot(jnp.broadcast_to(cnt, (SUBLANES, LANES)).astype(BF16), utri,
                    preferred_element_type=F32)[0:1]
    base1 = start * ROW_ALIGN
    base2 = base1 + n1
    pos1 = jnp.sum(jnp.where(a1, c1 + base1, 0.0), axis=1, keepdims=True)
    pos2 = jnp.sum(jnp.where(a2, c2 + base2, 0.0), axis=1, keepdims=True)
    posm = jnp.where(lane == 0.0, pos1, jnp.where(lane == 1.0, pos2, 0.0))
    pos_ref[...] = posm
    row = _iota_f32((SUBLANES, LANES), 0)
    meta_ref[...] = jnp.where(row == 0.0, start, jnp.where(row == 1.0, cnt, 0.0))

    g1 = _split3(rec[:, 2:3])
    g2 = _split3(rec[:, 3:4])
    info = jnp.where(lane == 6.0, rec[:, 0:1], jnp.where(lane == 7.0, rec[:, 1:2], 0.0))
    for i in range(3):
        info = jnp.where(lane == float(i), g1[i].astype(F32), info)
        info = jnp.where(lane == float(3 + i), g2[i].astype(F32), info)
    src = jnp.concatenate([tok, info.astype(BF16)], axis=1)

    post = posm.T
    r = _iota_f32((SUBP, SUB), 0)
    onehot = jnp.where(r == post[0:1, :], 1.0, jnp.where(r == post[1:2, :], 1.0, 0.0)).astype(BF16)
    sorted_rows = jnp.dot(onehot, src, preferred_element_type=F32).astype(BF16)
    s_ref[...] = sorted_rows.reshape(CHUNKS_PER_SUB, ROW_ALIGN, ROW_W)


def _dispatch(rec_a, t_a, rec_b, t_b):
    nsub_a = rec_a.shape[0] // SUB
    nsub_b = rec_b.shape[0] // SUB
    nsub = nsub_a + nsub_b
    assert nsub_a % DISPATCH_SUBS == 0 and nsub_b % DISPATCH_SUBS == 0
    steps_a = nsub_a // DISPATCH_SUBS
    tm = DISPATCH_SUBS * SUB
    row = lambda i: (i, 0)
    row_a = lambda i: (jnp.minimum(i, steps_a - 1), 0)
    row_b = lambda i: (jnp.maximum(i - steps_a, 0), 0)
    return pl.pallas_call(
        functools.partial(_dispatch_body, nsub_a=nsub_a),
        grid=(nsub // DISPATCH_SUBS,),
        in_specs=[
            pl.BlockSpec((tm, LANES), row_a),
            pl.BlockSpec((tm, D_MODEL), row_a),
            pl.BlockSpec((tm, LANES), row_b),
            pl.BlockSpec((tm, D_MODEL), row_b),
        ],
        out_specs=[
            pl.BlockSpec((DISPATCH_SUBS * CHUNKS_PER_SUB, ROW_ALIGN, ROW_W), lambda i: (i, 0, 0)),
            pl.BlockSpec((tm, LANES), row),
            pl.BlockSpec((DISPATCH_SUBS, SUBLANES, LANES), lambda i: (i, 0, 0)),
        ],
        out_shape=[
            jax.ShapeDtypeStruct((nsub * CHUNKS_PER_SUB, ROW_ALIGN, ROW_W), BF16),
            jax.ShapeDtypeStruct((nsub * SUB, LANES), F32),
            jax.ShapeDtypeStruct((nsub, SUBLANES, LANES), F32),
        ],
        compiler_params=_cparams(("parallel",)),
        name="moe_dispatch",
    )(rec_a, t_a, rec_b, t_b)


def _experts_body(start_ref, cnt_ref, s_in, wg_ref, wu_ref, wd_ref, s_hbm,
                  xbuf, ybuf, wgu_ref, wdb_ref, gsem, ssem, list_ref, state_ref, *, nsub):
    del s_in
    e = pl.program_id(0)
    ne = pl.num_programs(0)
    par = e & 1

    list_max = list_ref.shape[0] // 2

    def gather_copy(src, i, slot):
        return pltpu.make_async_copy(s_hbm.at[src], xbuf.at[slot, i], gsem.at[slot])

    def scatter_copy(dst, i, slot):
        return pltpu.make_async_copy(ybuf.at[slot, i], s_hbm.at[dst], ssem.at[slot])

    def build_list(x, which):
        def per_sub(s, k):
            run = s * N_EXPERTS + x
            c = cnt_ref[run]
            base = s * CHUNKS_PER_SUB + start_ref[run]
            list_ref[k] = base
            list_ref[k + 1] = base + 1

            def per_chunk(i, carry):
                list_ref[k + i] = base + i
                return carry
            lax.fori_loop(2, c, per_chunk, 0)
            return k + c
        first = which * list_max
        state_ref[which] = lax.fori_loop(0, nsub, per_sub, first) - first

    def start_all(copy, which, first, n, slot, counter):
        def body(i, carry):
            copy(list_ref[which * list_max + first + i], i, slot).start()
            return carry
        lax.fori_loop(0, n, body, 0)
        state_ref[counter] = n

    def wait_all(copy, block_copy, slot, counter):
        n = state_ref[counter]

        @pl.when(n == CHUNKS_PER_BLK)
        def _():
            block_copy(slot).wait()

        @pl.when(n < CHUNKS_PER_BLK)
        def _():
            def body(i, carry):
                copy(0, 0, slot).wait()
                return carry
            lax.fori_loop(0, n, body, 0)
        state_ref[counter] = 0

    def gather_block(slot):
        return pltpu.make_async_copy(s_hbm.at[pl.ds(0, CHUNKS_PER_BLK)], xbuf.at[slot], gsem.at[slot])

    def scatter_block(slot):
        return pltpu.make_async_copy(ybuf.at[slot], s_hbm.at[pl.ds(0, CHUNKS_PER_BLK)], ssem.at[slot])

    def block_chunks(total, b):
        return jnp.minimum(total - b * CHUNKS_PER_BLK, CHUNKS_PER_BLK)

    @pl.when(e == 0)
    def _():
        for i in range(6):
            state_ref[i] = 0
        xbuf[...] = jnp.zeros_like(xbuf)
        build_list(0, 0)
        n0 = state_ref[0]

        @pl.when(n0 > 0)
        def _():
            start_all(gather_copy, 0, 0, block_chunks(n0, 0), 0, 2)

    @pl.when(e + 1 < ne)
    def _():
        build_list(e + 1, 1 - par)

    total = state_ref[par]
    nblk = (total + CHUNKS_PER_BLK - 1) // CHUNKS_PER_BLK
    wgu_ref[:, 0:EXPERT_FF] = wg_ref[0].astype(BF16)
    wgu_ref[:, EXPERT_FF:] = wu_ref[0].astype(BF16)
    wdb_ref[...] = wd_ref[0].astype(BF16)
    ef = e.astype(F32)

    def block(b, carry):
        slot = b & 1
        first = b * CHUNKS_PER_BLK

        @pl.when(b + 1 < nblk)
        def _():
            start_all(gather_copy, par, first + CHUNKS_PER_BLK, block_chunks(total, b + 1), 1 - slot, 3 - slot)

        wait_all(gather_copy, gather_block, slot, 2 + slot)
        wait_all(scatter_copy, scatter_block, slot, 4 + slot)
        x = xbuf[slot].reshape(MBLK, ROW_W)
        info = x[:, D_MODEL:].astype(F32)
        g_first = info[:, 0:1] + info[:, 1:2] + info[:, 2:3]
        g_second = info[:, 3:4] + info[:, 4:5] + info[:, 5:6]
        gate = jnp.where(info[:, 6:7] == ef, g_first, g_second)
        h = jnp.dot(x[:, :D_MODEL], wgu_ref[...], preferred_element_type=F32)
        h1 = h[:, :EXPERT_FF]
        hid = (h1 * jax.nn.sigmoid(h1)) * h[:, EXPERT_FF:] * gate
        y = jnp.dot(hid.astype(BF16), wdb_ref[...], preferred_element_type=F32).astype(BF16)
        ybuf[slot] = jnp.concatenate([y, x[:, D_MODEL:]], axis=1).reshape(CHUNKS_PER_BLK, ROW_ALIGN, ROW_W)
        start_all(scatter_copy, par, first, block_chunks(total, b), slot, 4 + slot)
        return carry

    lax.fori_loop(0, nblk, block, 0)

    @pl.when(e + 1 < ne)
    def _():
        n1 = state_ref[1 - par]

        @pl.when(n1 > 0)
        def _():
            start_all(gather_copy, 1 - par, 0, block_chunks(n1, 0), 0, 2)

    @pl.when(e == ne - 1)
    def _():
        wait_all(scatter_copy, scatter_block, 0, 4)
        wait_all(scatter_copy, scatter_block, 1, 5)


def _experts(start, cnt, staged, wg, wu, wd):
    nsub = staged.shape[0] // CHUNKS_PER_SUB
    list_max = nsub * SUB // ROW_ALIGN + nsub + LIST_SLACK
    wblk = lambda e, *_: (e, 0, 0)
    grid_spec = pltpu.PrefetchScalarGridSpec(
        num_scalar_prefetch=2,
        grid=(N_EXPERTS,),
        in_specs=[
            pl.BlockSpec(memory_space=pl.ANY),
            pl.BlockSpec((1, D_MODEL, EXPERT_FF), wblk),
            pl.BlockSpec((1, D_MODEL, EXPERT_FF), wblk),
            pl.BlockSpec((1, EXPERT_FF, D_MODEL), wblk),
        ],
        out_specs=pl.BlockSpec(memory_space=pl.ANY),
        scratch_shapes=[
            pltpu.VMEM((2, CHUNKS_PER_BLK, ROW_ALIGN, ROW_W), BF16),
            pltpu.VMEM((2, CHUNKS_PER_BLK, ROW_ALIGN, ROW_W), BF16),
            pltpu.VMEM((D_MODEL, 2 * EXPERT_FF), BF16),
            pltpu.VMEM((EXPERT_FF, D_MODEL), BF16),
            pltpu.SemaphoreType.DMA((2,)),
            pltpu.SemaphoreType.DMA((2,)),
            pltpu.SMEM((2 * list_max,), jnp.int32),
            pltpu.SMEM((6,), jnp.int32),
        ],
    )
    return pl.pallas_call(
        functools.partial(_experts_body, nsub=nsub),
        grid_spec=grid_spec,
        out_shape=jax.ShapeDtypeStruct(staged.shape, staged.dtype),
        input_output_aliases={2: 0},
        compiler_params=_cparams(("arbitrary",)),
        name="moe_experts",
    )(start, cnt, staged, wg, wu, wd)


def _combine_body(ys_ref, pos_ref, x1_ref, p_ref, wp_ref, gp_ref, wpg_ref, gf_ref, y_ref):
    r = _iota_f32((SUB, SUBP), 1)
    moe = []
    for s in range(x1_ref.shape[0] // SUB):
        p1 = pos_ref[s * SUB:(s + 1) * SUB, 0:1]
        p2 = pos_ref[s * SUB:(s + 1) * SUB, 1:2]
        onehot = jnp.where(r == p1, 1.0, jnp.where(r == p2, 1.0, 0.0)).astype(BF16)
        ys = ys_ref[s * CHUNKS_PER_SUB:(s + 1) * CHUNKS_PER_SUB].reshape(SUBP, D_MODEL)
        moe.append(jnp.dot(onehot, ys, preferred_element_type=F32))
    x2 = x1_ref[...] + jnp.concatenate(moe, axis=0)
    ple = _rms(jnp.dot(p_ref[...].astype(BF16), wp_ref[...], preferred_element_type=F32), gp_ref[...])
    gate = jax.nn.sigmoid(jnp.dot(x2.astype(BF16), wpg_ref[...], preferred_element_type=F32))
    y_ref[...] = _rms(x2 + ple * gate, gf_ref[...])


def _combine(ys, pos, x1, p, w_ple, gp, w_ple_gate, gf, sub_off, subs):
    t = x1.shape[0]
    tm = subs * SUB
    blk_off = sub_off // subs
    assert sub_off % subs == 0 and t % tm == 0
    row = lambda i: (i, 0)
    const = lambda i: (0, 0)
    return pl.pallas_call(
        _combine_body,
        grid=(t // tm,),
        in_specs=[
            pl.BlockSpec((subs * CHUNKS_PER_SUB, ROW_ALIGN, D_MODEL), lambda i: (i + blk_off, 0, 0)),
            pl.BlockSpec((tm, LANES), lambda i: (i + blk_off, 0)),
            pl.BlockSpec((tm, D_MODEL), row),
            pl.BlockSpec((tm, PLE_DIM), row),
            pl.BlockSpec((PLE_DIM, D_MODEL), const),
            pl.BlockSpec((1, D_MODEL), const),
            pl.BlockSpec((D_MODEL, D_MODEL), const),
            pl.BlockSpec((1, D_MODEL), const),
        ],
        out_specs=pl.BlockSpec((tm, D_MODEL), row),
        out_shape=jax.ShapeDtypeStruct((t, D_MODEL), F32),
        compiler_params=_cparams(("parallel",)),
        name="moe_combine_ple",
    )(ys, pos, x1, p, w_ple, gp, w_ple_gate, gf)


def _rope_tables(pos):
    half = HEAD_DIM // 2
    inv = ROPE_BASE ** (-jnp.arange(half, dtype=F32) / half)
    ang = pos[:, None] * inv[None, :]
    cos = jnp.cos(ang)
    sin = jnp.sin(ang)
    return jnp.concatenate([cos, cos], axis=-1), jnp.concatenate([-sin, sin], axis=-1)


def _router_params(we, be, wg, bg):
    pad = LANES - N_EXPERTS - N_GROUPS
    w = jnp.pad(jnp.concatenate([we, wg], axis=1), ((0, 0), (0, pad)))
    b = jnp.pad(jnp.concatenate([be, bg]), (0, pad))[None, :]
    return jnp.concatenate(_split3(w)[:2], axis=1), b


def kernel(x_prompt, x_sample, p_prompt, p_sample, state_conv, state_ret, w_in, conv_w, conv_b, conv_ln_g, conv_ln_b, w_out, norm1_g, norm2_g, router_group_w, router_group_b, router_expert_w, router_expert_b, w_expert_gate, w_expert_up, w_expert_down, w_ple, ple_norm_g, w_ple_gate, final_norm_g):
    assert w_in.shape[0] == 1, "single-layer trunk"
    nb, seq, _ = x_prompt.shape
    ns, dseq, _ = x_sample.shape
    tm = SAMPLE_TILE

    w_in_b = w_in[0].astype(BF16)
    w_out_b = w_out[0].astype(BF16)
    w_ple_b = w_ple[0].astype(BF16)
    w_pg_b = w_ple_gate[0].astype(BF16)
    g1 = norm1_g[0][None, :]
    g2 = norm2_g[0][None, :]
    gp = ple_norm_g[0][None, :]
    gf = final_norm_g[None, :]
    cb = conv_b[0][None, :]
    lng = conv_ln_g[0][None, :]
    lnb = conv_ln_b[0][None, :]
    wr2, br = _router_params(router_expert_w[0], router_expert_b[0], router_group_w[0], router_group_b[0])

    cos_p, sin_p = _rope_tables(jnp.arange(seq, dtype=F32) + jnp.float32(0))
    pos_s = jnp.tile(jnp.arange(dseq, dtype=F32) + jnp.float32(PAST_LEN), tm // dseq)
    cos_s, sin_s = _rope_tables(pos_s)

    xp = x_prompt.reshape(nb * seq, D_MODEL)
    x1_p, t_p, rec_p, conv_p, ret_p = _mix(xp, g1, w_in_b, cos_p, sin_p, conv_w[0], cb, lng, lnb, w_out_b, g2,
                                           wr2, br, nb, seq, MIX_TILE)

    xs = x_sample.reshape(ns * dseq, D_MODEL)
    u, q, k, v, gs = _inproj(xs, g1, w_in_b, cos_s, sin_s, tm)
    c, conv_s = _conv_sample(u.reshape(ns, dseq, CONV_CH), jnp.transpose(state_conv[0], (1, 0, 2)), conv_w[0], cb,
                             lng, lnb, CONV_SAMPLE_SEQS)
    conv_s = jnp.transpose(conv_s, (1, 0, 2))
    o, ret_s = _ret_sample(q, k, v, gs, state_ret[0], dseq, RET_SAMPLE_SEQS)
    x1_s, t_s, rec_s = _outproj(c.reshape(ns * dseq, CONV_CH), o, xs, w_out_b, g2, wr2, br, tm)

    staged, pos, meta = _dispatch(rec_p, t_p, rec_s, t_s)
    start = meta[:, 0, :N_EXPERTS].astype(jnp.int32).reshape(-1)
    cnt = meta[:, 1, :N_EXPERTS].astype(jnp.int32).reshape(-1)
    ys = _experts(start, cnt, staged, w_expert_gate[0], w_expert_up[0], w_expert_down[0])

    y_p = _combine(ys, pos, x1_p, p_prompt[0].reshape(nb * seq, PLE_DIM), w_ple_b, gp, w_pg_b, gf, 0, COMBINE_SUBS)
    y_s = _combine(ys, pos, x1_s, p_sample[0].reshape(ns * dseq, PLE_DIM), w_ple_b, gp, w_pg_b, gf,
                   nb * seq // SUB, COMBINE_SAMPLE_SUBS)

    return (y_p.reshape(nb, seq, D_MODEL), y_s.reshape(ns, dseq, D_MODEL),
            conv_p[None], ret_p[None], conv_s[None], ret_s[None])
```
